```python
import math
import jax, jax.numpy as jnp
from jax import lax
import numpy as np

D_MODEL = 2048
BATCH = 16
SEQ = 2048
DEPTH = 2

HEAD_DIM = 64
N_MIXERS = 4
GROUP_WIDTH = D_MODEL // N_MIXERS
MIX_WIDTH = N_MIXERS * GROUP_WIDTH
SGU_GROUPS = GROUP_WIDTH // HEAD_DIM
SGU_CHUNK = 128
DIL_HEADS = GROUP_WIDTH // HEAD_DIM
DIL_PATTERNS = ((128, 1), (512, 4), (2048, 16))
CONV_CH = GROUP_WIDTH
CONV_WIDTH = 31
GQA_Q_HEADS = GROUP_WIDTH // HEAD_DIM
GQA_KV_HEADS = GQA_Q_HEADS // 4
KV_WIDTH = GQA_KV_HEADS * HEAD_DIM
Q_BLOCK = 128
GRID_W = 64
ROPE_THETA = 10000.0
REL_BUCKETS = 32
REL_MAX_DIST = 1024
FFN_HIDDEN = ((8 * D_MODEL + 3 * 256 - 1) // (3 * 256)) * 256
IN_SIZES = (GROUP_WIDTH, GROUP_WIDTH,
            GROUP_WIDTH, GROUP_WIDTH, GROUP_WIDTH,
            CONV_CH, CONV_CH,
            GROUP_WIDTH, KV_WIDTH, KV_WIDTH)
IN_WIDTH = sum(IN_SIZES)
RMS_EPS = 1e-6
LN_EPS = 1e-5

kernel_name = "hymba_style_hybrid_encoder_block"


def rms_norm(x, g):
    xf = x.astype(jnp.float32)
    y = xf * lax.rsqrt(jnp.mean(xf * xf, axis=-1, keepdims=True) + RMS_EPS)
    return (y * g.astype(jnp.float32)).astype(x.dtype)


def layer_norm_stats(x):
    xf = x.astype(jnp.float32)
    mu = jnp.mean(xf, axis=-1, keepdims=True)
    xc = xf - mu
    return xc * lax.rsqrt(jnp.mean(xc * xc, axis=-1, keepdims=True) + LN_EPS)


def split_heads(t):
    return t.reshape(t.shape[0], t.shape[1], -1, HEAD_DIM)


def t5_buckets(rel):
    nb = REL_BUCKETS // 2
    max_exact = nb // 2
    ret = jnp.where(rel > 0, nb, 0)
    n = jnp.abs(rel)
    nf = jnp.maximum(n, 1).astype(jnp.float32)
    large = max_exact + (jnp.log(nf / max_exact) / math.log(REL_MAX_DIST / max_exact)
                         * (nb - max_exact)).astype(jnp.int32)
    large = jnp.minimum(large, nb - 1)
    return ret + jnp.where(n < max_exact, n, large)


def sgu_branch(u, v, w_s, b_s):
    bn, s, w = u.shape
    nc = s // SGU_CHUNK
    u = jax.nn.gelu(u)
    v = jax.nn.gelu(v).reshape(bn, nc, SGU_CHUNK, SGU_GROUPS, HEAD_DIM)
    vn = layer_norm_stats(v).astype(u.dtype)
    mixed = jnp.einsum('gpq,bcqgd->bcpgd', w_s, vn) + b_s.T[None, None, :, :, None]
    return u * mixed.reshape(bn, s, w)


def dilated_pattern(q, k, v, rel_table, window, dil):
    bn, s, h, dh = q.shape
    half = window // (2 * dil)
    blk = half
    L = s // dil
    nb = -(-L // blk)
    lp = nb * blk

    def to_sub(t):
        return t.reshape(bn, L, dil, h, dh).transpose(0, 2, 1, 3, 4)

    qs = jnp.pad(to_sub(q), ((0, 0), (0, 0), (0, lp - L), (0, 0), (0, 0)))
    qs = qs.reshape(bn, dil, nb, blk, h, dh)
    pad_kv = ((0, 0), (0, 0), (blk, lp - L + blk), (0, 0), (0, 0))

    def band(t):
        t = jnp.pad(to_sub(t), pad_kv).reshape(bn, dil, nb + 2, blk, h, dh)
        return jnp.concatenate([t[:, :, :-2], t[:, :, 1:-1], t[:, :, 2:]], axis=3)

    kb, vb = band(k), band(v)
    sc = jnp.einsum('brnqhd,brnkhd->brnhqk', qs, kb, preferred_element_type=jnp.float32)
    off = jnp.arange(3 * blk)[None, :] - blk - jnp.arange(blk)[:, None]
    key_idx = jnp.arange(nb)[:, None] * blk - blk + jnp.arange(3 * blk)[None, :]
    valid = (jnp.abs(off) <= half)[None] & ((key_idx >= 0) & (key_idx < L))[:, None, :]
    bias = rel_table[t5_buckets(off * dil)].astype(jnp.float32).transpose(2, 0, 1)
    sc = sc + bias[None, None, None]
    sc = jnp.where(valid[None, None, :, None], sc, -1e30)
    lse = jax.nn.logsumexp(sc, axis=-1)
    p = jnp.exp(sc - lse[..., None])
    o = jnp.einsum('brnhqk,brnkhd->brnqhd', p.astype(v.dtype), vb)
    o = o.reshape(bn, dil, lp, h, dh)[:, :, :L].transpose(0, 2, 1, 3, 4).reshape(bn, s, h, dh)
    lse = lse.transpose(0, 1, 2, 4, 3).reshape(bn, dil, lp, h)[:, :, :L]
    lse = lse.transpose(0, 2, 1, 3).reshape(bn, s, h)
    return o, lse


def dilated_mixture(q, k, v, rel_table):
    outs, lses = [], []
    for window, dil in DIL_PATTERNS:
        o, lse = dilated_pattern(q, k, v, rel_table, window, dil)
        outs.append(o)
        lses.append(lse)
    w = jax.nn.softmax(jnp.stack(lses, axis=0), axis=0)
    return jnp.einsum('gbsh,gbshd->bshd', w.astype(v.dtype), jnp.stack(outs, axis=0))


def conv_branch(a, gate, w_dw, b_dw, ln_g, ln_b):
    hdn = a * jax.nn.sigmoid(gate)
    pad = CONV_WIDTH // 2
    hdn = lax.conv_general_dilated(hdn, w_dw[:, None, :], (1,), [(pad, pad)],
                                   dimension_numbers=('NWC', 'WIO', 'NWC'),
                                   feature_group_count=hdn.shape[-1]) + b_dw
    hdn = (layer_norm_stats(hdn) * ln_g.astype(jnp.float32) + ln_b.astype(jnp.float32)).astype(a.dtype)
    return jax.nn.silu(hdn)


def rope_axis(x, pos):
    half = x.shape[-1] // 2
    freqs = ROPE_THETA ** (-jnp.arange(half, dtype=jnp.float32) / half)
    ang = pos.astype(jnp.float32)[:, None] * freqs[None, :]
    cos = jnp.cos(ang)[:, None, :]
    sin = jnp.sin(ang)[:, None, :]
    xf = x.astype(jnp.float32)
    x1, x2 = xf[..., :half], xf[..., half:]
    return jnp.concatenate([x1 * cos - x2 * sin, x2 * cos + x1 * sin], axis=-1).astype(x.dtype)


def axial_rope(x, row, col):
    d2 = x.shape[-1] // 2
    return jnp.concatenate([rope_axis(x[..., :d2], row), rope_axis(x[..., d2:], col)], axis=-1)


def gqa_branch(q, k, v):
    bn, s, hq, dh = q.shape
    hkv = k.shape[2]
    g = hq // hkv
    nq = s // Q_BLOCK
    qb = q.reshape(bn, nq, Q_BLOCK, hkv, g, dh).transpose(1, 0, 2, 3, 4, 5)

    def block(qblk):
        sc = jnp.einsum('bqhgd,bkhd->bhgqk', qblk, k, preferred_element_type=jnp.float32)
        p = jax.nn.softmax(sc, axis=-1)
        return jnp.einsum('bhgqk,bkhd->bqhgd', p.astype(v.dtype), v)

    o = lax.map(block, qb)
    return o.transpose(1, 0, 2, 3, 4, 5).reshape(bn, s, hq, dh)


def _fwd_setup_inputs(seed: int = 0) -> dict:
    key = jax.random.key(seed)
    ks = jax.random.split(key, 20)
    f32 = jnp.float32
    nrm = lambda k, shape, scale: jax.random.normal(k, shape, f32) * scale
    gain = lambda k, shape: 1.0 + 0.02 * jax.random.normal(k, shape, f32)
    return {
        "x": jax.random.normal(ks[0], (BATCH, SEQ, D_MODEL), f32),
        "rel_bias": nrm(ks[1], (REL_BUCKETS, DIL_HEADS), 0.5),
        "norm1_g": gain(ks[2], (DEPTH, D_MODEL)),
        "w_in": nrm(ks[3], (DEPTH, D_MODEL, IN_WIDTH), D_MODEL ** -0.5),
        "sgu_w": nrm(ks[4], (DEPTH, SGU_GROUPS, SGU_CHUNK, SGU_CHUNK), SGU_CHUNK ** -0.5),
        "sgu_b": gain(ks[5], (DEPTH, SGU_GROUPS, SGU_CHUNK)),
        "dil_qn_g": gain(ks[6], (DEPTH, HEAD_DIM)),
        "dil_kn_g": gain(ks[7], (DEPTH, HEAD_DIM)),
        "conv_w": nrm(ks[8], (DEPTH, CONV_WIDTH, CONV_CH), CONV_WIDTH ** -0.5),
        "conv_b": nrm(ks[9], (DEPTH, CONV_CH), 0.02),
        "conv_ln_g": gain(ks[10], (DEPTH, CONV_CH)),
        "conv_ln_b": nrm(ks[11], (DEPTH, CONV_CH), 0.02),
        "gqa_qn_g": gain(ks[12], (DEPTH, HEAD_DIM)),
        "gqa_kn_g": gain(ks[13], (DEPTH, HEAD_DIM)),
        "mix_norm_g": gain(ks[14], (DEPTH, MIX_WIDTH)),
        "w_out": nrm(ks[15], (DEPTH, MIX_WIDTH, D_MODEL), MIX_WIDTH ** -0.5),
        "norm2_g": gain(ks[16], (DEPTH, D_MODEL)),
        "w_gate": nrm(ks[17], (DEPTH, D_MODEL, FFN_HIDDEN), D_MODEL ** -0.5),
        "w_up": nrm(ks[18], (DEPTH, D_MODEL, FFN_HIDDEN), D_MODEL ** -0.5),
        "w_down": nrm(ks[19], (DEPTH, FFN_HIDDEN, D_MODEL), FFN_HIDDEN ** -0.5),
    }


def _fwd_reference(x, rel_bias, norm1_g, w_in, sgu_w, sgu_b, dil_qn_g, dil_kn_g, conv_w, conv_b,
              conv_ln_g, conv_ln_b, gqa_qn_g, gqa_kn_g, mix_norm_g, w_out, norm2_g,
              w_gate, w_up, w_down):
    bn, s, _ = x.shape
    rows = s // GRID_W
    row = jnp.repeat(jnp.arange(rows), GRID_W)
    col = jnp.tile(jnp.arange(GRID_W), rows)
    split_at = np.cumsum(IN_SIZES)[:-1].tolist()
    scale = HEAD_DIM ** -0.5
    for l in range(DEPTH):
        h = rms_norm(x, norm1_g[l])
        z = h @ w_in[l]
        a_u, a_v, b_q, b_k, b_v, c_a, c_g, d_q, d_k, d_v = jnp.split(z, split_at, axis=-1)
        y_a = sgu_branch(a_u, a_v, sgu_w[l], sgu_b[l])
        qb = rms_norm(split_heads(b_q), dil_qn_g[l]) * scale
        kb = rms_norm(split_heads(b_k), dil_kn_g[l])
        y_b = dilated_mixture(qb, kb, split_heads(b_v), rel_bias).reshape(bn, s, GROUP_WIDTH)
        y_c = conv_branch(c_a, c_g, conv_w[l], conv_b[l], conv_ln_g[l], conv_ln_b[l])
        qd = axial_rope(rms_norm(split_heads(d_q), gqa_qn_g[l]), row, col) * scale
        kd = axial_rope(rms_norm(split_heads(d_k), gqa_kn_g[l]), row, col)
        y_d = gqa_branch(qd, kd, split_heads(d_v)).reshape(bn, s, GROUP_WIDTH)
        y = jnp.stack([y_a, y_b, y_c, y_d], axis=2)
        y = rms_norm(y, mix_norm_g[l].reshape(N_MIXERS, GROUP_WIDTH)).reshape(bn, s, MIX_WIDTH)
        x = x + y @ w_out[l]
        h = rms_norm(x, norm2_g[l])
        x = x + (jax.nn.silu(h @ w_gate[l]) * (h @ w_up[l])) @ w_down[l]
    return x


import jax as _jax
import jax.numpy as _jnp

TWIN_FORMAT = 'train_step'
FWD_PARAMS = ['x', 'rel_bias', 'norm1_g', 'w_in', 'sgu_w', 'sgu_b', 'dil_qn_g', 'dil_kn_g', 'conv_w', 'conv_b', 'conv_ln_g', 'conv_ln_b', 'gqa_qn_g', 'gqa_kn_g', 'mix_norm_g', 'w_out', 'norm2_g', 'w_gate', 'w_up', 'w_down']
TWIN_WEIGHTS = ['rel_bias', 'norm1_g', 'w_in', 'sgu_w', 'sgu_b', 'dil_qn_g', 'dil_kn_g', 'conv_w', 'conv_b', 'conv_ln_g', 'conv_ln_b', 'gqa_qn_g', 'gqa_kn_g', 'mix_norm_g', 'w_out', 'norm2_g', 'w_gate', 'w_up', 'w_down']
TWIN_DIFF_INPUT = 'x'
TWIN_INPUTS = ['x', 'rel_bias', 'norm1_g', 'w_in', 'sgu_w', 'sgu_b', 'dil_qn_g', 'dil_kn_g', 'conv_w', 'conv_b', 'conv_ln_g', 'conv_ln_b', 'gqa_qn_g', 'gqa_kn_g', 'mix_norm_g', 'w_out', 'norm2_g', 'w_gate', 'w_up', 'w_down', 'loss_target', 'm_rel_bias', 'm_norm1_g', 'm_w_in', 'm_sgu_w', 'm_sgu_b', 'm_dil_qn_g', 'm_dil_kn_g', 'm_conv_w', 'm_conv_b', 'm_conv_ln_g', 'm_conv_ln_b', 'm_gqa_qn_g', 'm_gqa_kn_g', 'm_mix_norm_g', 'm_w_out', 'm_norm2_g', 'm_w_gate', 'm_w_up', 'm_w_down', 'v_rel_bias', 'v_norm1_g', 'v_w_in', 'v_sgu_w', 'v_sgu_b', 'v_dil_qn_g', 'v_dil_kn_g', 'v_conv_w', 'v_conv_b', 'v_conv_ln_g', 'v_conv_ln_b', 'v_gqa_qn_g', 'v_gqa_kn_g', 'v_mix_norm_g', 'v_w_out', 'v_norm2_g', 'v_w_gate', 'v_w_up', 'v_w_down']
TWIN_OUTPUTS = ['loss', 'grad_x', 'grad_rel_bias', 'grad_norm1_g', 'grad_w_in', 'grad_sgu_w', 'grad_sgu_b', 'grad_dil_qn_g', 'grad_dil_kn_g', 'grad_conv_w', 'grad_conv_b', 'grad_conv_ln_g', 'grad_conv_ln_b', 'grad_gqa_qn_g', 'grad_gqa_kn_g', 'grad_mix_norm_g', 'grad_w_out', 'grad_norm2_g', 'grad_w_gate', 'grad_w_up', 'grad_w_down', 'delta_rel_bias', 'delta_norm1_g', 'delta_w_in', 'delta_sgu_w', 'delta_sgu_b', 'delta_dil_qn_g', 'delta_dil_kn_g', 'delta_conv_w', 'delta_conv_b', 'delta_conv_ln_g', 'delta_conv_ln_b', 'delta_gqa_qn_g', 'delta_gqa_kn_g', 'delta_mix_norm_g', 'delta_w_out', 'delta_norm2_g', 'delta_w_gate', 'delta_w_up', 'delta_w_down', 'new_m_rel_bias', 'new_m_norm1_g', 'new_m_w_in', 'new_m_sgu_w', 'new_m_sgu_b', 'new_m_dil_qn_g', 'new_m_dil_kn_g', 'new_m_conv_w', 'new_m_conv_b', 'new_m_conv_ln_g', 'new_m_conv_ln_b', 'new_m_gqa_qn_g', 'new_m_gqa_kn_g', 'new_m_mix_norm_g', 'new_m_w_out', 'new_m_norm2_g', 'new_m_w_gate', 'new_m_w_up', 'new_m_w_down', 'new_v_rel_bias', 'new_v_norm1_g', 'new_v_w_in', 'new_v_sgu_w', 'new_v_sgu_b', 'new_v_dil_qn_g', 'new_v_dil_kn_g', 'new_v_conv_w', 'new_v_conv_b', 'new_v_conv_ln_g', 'new_v_conv_ln_b', 'new_v_gqa_qn_g', 'new_v_gqa_kn_g', 'new_v_mix_norm_g', 'new_v_w_out', 'new_v_norm2_g', 'new_v_w_gate', 'new_v_w_up', 'new_v_w_down']
TWIN_LEAF_KINDS = {'loss': 'loss', 'grad_x': 'grad_x', 'grad_rel_bias': 'grad_w', 'grad_norm1_g': 'grad_w', 'grad_w_in': 'grad_w', 'grad_sgu_w': 'grad_w', 'grad_sgu_b': 'grad_w', 'grad_dil_qn_g': 'grad_w', 'grad_dil_kn_g': 'grad_w', 'grad_conv_w': 'grad_w', 'grad_conv_b': 'grad_w', 'grad_conv_ln_g': 'grad_w', 'grad_conv_ln_b': 'grad_w', 'grad_gqa_qn_g': 'grad_w', 'grad_gqa_kn_g': 'grad_w', 'grad_mix_norm_g': 'grad_w', 'grad_w_out': 'grad_w', 'grad_norm2_g': 'grad_w', 'grad_w_gate': 'grad_w', 'grad_w_up': 'grad_w', 'grad_w_down': 'grad_w', 'delta_rel_bias': 'delta_w', 'delta_norm1_g': 'delta_w', 'delta_w_in': 'delta_w', 'delta_sgu_w': 'delta_w', 'delta_sgu_b': 'delta_w', 'delta_dil_qn_g': 'delta_w', 'delta_dil_kn_g': 'delta_w', 'delta_conv_w': 'delta_w', 'delta_conv_b': 'delta_w', 'delta_conv_ln_g': 'delta_w', 'delta_conv_ln_b': 'delta_w', 'delta_gqa_qn_g': 'delta_w', 'delta_gqa_kn_g': 'delta_w', 'delta_mix_norm_g': 'delta_w', 'delta_w_out': 'delta_w', 'delta_norm2_g': 'delta_w', 'delta_w_gate': 'delta_w', 'delta_w_up': 'delta_w', 'delta_w_down': 'delta_w', 'new_m_rel_bias': 'new_m', 'new_m_norm1_g': 'new_m', 'new_m_w_in': 'new_m', 'new_m_sgu_w': 'new_m', 'new_m_sgu_b': 'new_m', 'new_m_dil_qn_g': 'new_m', 'new_m_dil_kn_g': 'new_m', 'new_m_conv_w': 'new_m', 'new_m_conv_b': 'new_m', 'new_m_conv_ln_g': 'new_m', 'new_m_conv_ln_b': 'new_m', 'new_m_gqa_qn_g': 'new_m', 'new_m_gqa_kn_g': 'new_m', 'new_m_mix_norm_g': 'new_m', 'new_m_w_out': 'new_m', 'new_m_norm2_g': 'new_m', 'new_m_w_gate': 'new_m', 'new_m_w_up': 'new_m', 'new_m_w_down': 'new_m', 'new_v_rel_bias': 'new_v', 'new_v_norm1_g': 'new_v', 'new_v_w_in': 'new_v', 'new_v_sgu_w': 'new_v', 'new_v_sgu_b': 'new_v', 'new_v_dil_qn_g': 'new_v', 'new_v_dil_kn_g': 'new_v', 'new_v_conv_w': 'new_v', 'new_v_conv_b': 'new_v', 'new_v_conv_ln_g': 'new_v', 'new_v_conv_ln_b': 'new_v', 'new_v_gqa_qn_g': 'new_v', 'new_v_gqa_kn_g': 'new_v', 'new_v_mix_norm_g': 'new_v', 'new_v_w_out': 'new_v', 'new_v_norm2_g': 'new_v', 'new_v_w_gate': 'new_v', 'new_v_w_up': 'new_v', 'new_v_w_down': 'new_v'}


def _forward(args):
    return _fwd_reference(*[args[k] for k in FWD_PARAMS])


def _output_shape():
    out = _jax.eval_shape(lambda: _forward(_fwd_setup_inputs(0)))
    return out.shape, out.dtype

N_MICROBATCH = 1
ADAM_LR = 0.001
ADAM_B1 = 0.9
ADAM_B2 = 0.999
ADAM_EPS = 1e-08
ADAM_WD = 0.01
ADAM_STEP = 10
PER_EXAMPLE_BATCH_AXIS = {'x': 0, 'loss_target': 0}
SHARED_INPUTS = []
_WEIGHT_DTYPES = {'rel_bias': _jnp.float32, 'norm1_g': _jnp.float32, 'w_in': _jnp.float32, 'sgu_w': _jnp.float32, 'sgu_b': _jnp.float32, 'dil_qn_g': _jnp.float32, 'dil_kn_g': _jnp.float32, 'conv_w': _jnp.float32, 'conv_b': _jnp.float32, 'conv_ln_g': _jnp.float32, 'conv_ln_b': _jnp.float32, 'gqa_qn_g': _jnp.float32, 'gqa_kn_g': _jnp.float32, 'mix_norm_g': _jnp.float32, 'w_out': _jnp.float32, 'norm2_g': _jnp.float32, 'w_gate': _jnp.float32, 'w_up': _jnp.float32, 'w_down': _jnp.float32}
MOMENT_SCALE = {'rel_bias': 5.254793e-01, 'norm1_g': 4.339029e+00, 'w_in': 3.040999e+00, 'sgu_w': 1.437898e-01, 'sgu_b': 2.048768e-01, 'dil_qn_g': 9.034708e-01, 'dil_kn_g': 9.078697e-01, 'conv_w': 1.915440e+00, 'conv_b': 2.380111e+01, 'conv_ln_g': 9.019726e+00, 'conv_ln_b': 1.253677e+01, 'gqa_qn_g': 1.749087e+00, 'gqa_kn_g': 2.009824e+00, 'mix_norm_g': 1.852889e+01, 'w_out': 6.197075e+00, 'norm2_g': 1.272728e+01, 'w_gate': 5.422146e-01, 'w_up': 5.887362e-01, 'w_down': 9.301355e-01}


def _to_microbatches(a, axis):
    t = _jnp.moveaxis(a, axis, 0)
    t = t.reshape((N_MICROBATCH, t.shape[0] // N_MICROBATCH) + t.shape[1:])
    return _jnp.moveaxis(t, 1, axis + 1)


def setup_inputs(seed: int = 0) -> dict:
    inp = _fwd_setup_inputs(seed)
    key = _jax.random.fold_in(_jax.random.key(seed), 7919)
    shape, _ = _output_shape()
    out = dict(inp)
    out["loss_target"] = _jax.random.normal(_jax.random.fold_in(key, 0), shape, _jnp.float32)
    for i, name in enumerate(TWIN_WEIGHTS):
        w = inp[name].astype(_jnp.float32)
        if MOMENT_SCALE is None:
            s = _jnp.sqrt(_jnp.mean(_jnp.square(w)) + 1e-30)
        else:
            s = MOMENT_SCALE[name]
        km, kv = _jax.random.split(_jax.random.fold_in(key, i + 1))
        out[name] = w
        out["m_" + name] = s * _jax.random.normal(km, w.shape, _jnp.float32)
        out["v_" + name] = (s * s) * _jax.random.uniform(kv, w.shape, _jnp.float32, 0.5, 1.5)
    if N_MICROBATCH > 1:
        for name, axis in PER_EXAMPLE_BATCH_AXIS.items():
            out[name] = _to_microbatches(out[name], axis)
    return {'x': out['x'], 'rel_bias': out['rel_bias'], 'norm1_g': out['norm1_g'], 'w_in': out['w_in'], 'sgu_w': out['sgu_w'], 'sgu_b': out['sgu_b'], 'dil_qn_g': out['dil_qn_g'], 'dil_kn_g': out['dil_kn_g'], 'conv_w': out['conv_w'], 'conv_b': out['conv_b'], 'conv_ln_g': out['conv_ln_g'], 'conv_ln_b': out['conv_ln_b'], 'gqa_qn_g': out['gqa_qn_g'], 'gqa_kn_g': out['gqa_kn_g'], 'mix_norm_g': out['mix_norm_g'], 'w_out': out['w_out'], 'norm2_g': out['norm2_g'], 'w_gate': out['w_gate'], 'w_up': out['w_up'], 'w_down': out['w_down'], 'loss_target': out['loss_target'], 'm_rel_bias': out['m_rel_bias'], 'm_norm1_g': out['m_norm1_g'], 'm_w_in': out['m_w_in'], 'm_sgu_w': out['m_sgu_w'], 'm_sgu_b': out['m_sgu_b'], 'm_dil_qn_g': out['m_dil_qn_g'], 'm_dil_kn_g': out['m_dil_kn_g'], 'm_conv_w': out['m_conv_w'], 'm_conv_b': out['m_conv_b'], 'm_conv_ln_g': out['m_conv_ln_g'], 'm_conv_ln_b': out['m_conv_ln_b'], 'm_gqa_qn_g': out['m_gqa_qn_g'], 'm_gqa_kn_g': out['m_gqa_kn_g'], 'm_mix_norm_g': out['m_mix_norm_g'], 'm_w_out': out['m_w_out'], 'm_norm2_g': out['m_norm2_g'], 'm_w_gate': out['m_w_gate'], 'm_w_up': out['m_w_up'], 'm_w_down': out['m_w_down'], 'v_rel_bias': out['v_rel_bias'], 'v_norm1_g': out['v_norm1_g'], 'v_w_in': out['v_w_in'], 'v_sgu_w': out['v_sgu_w'], 'v_sgu_b': out['v_sgu_b'], 'v_dil_qn_g': out['v_dil_qn_g'], 'v_dil_kn_g': out['v_dil_kn_g'], 'v_conv_w': out['v_conv_w'], 'v_conv_b': out['v_conv_b'], 'v_conv_ln_g': out['v_conv_ln_g'], 'v_conv_ln_b': out['v_conv_ln_b'], 'v_gqa_qn_g': out['v_gqa_qn_g'], 'v_gqa_kn_g': out['v_gqa_kn_g'], 'v_mix_norm_g': out['v_mix_norm_g'], 'v_w_out': out['v_w_out'], 'v_norm2_g': out['v_norm2_g'], 'v_w_gate': out['v_w_gate'], 'v_w_up': out['v_w_up'], 'v_w_down': out['v_w_down']}


def _loss(weights, diff, rest, loss_target):
    with _jax.named_scope("forward"):
        args = {**rest, TWIN_DIFF_INPUT: diff, **{k: w.astype(_WEIGHT_DTYPES[k]) for k, w in weights.items()}}
        y = _forward(args)
    with _jax.named_scope("loss_head"):
        err = _jnp.square(y.astype(_jnp.float32) - loss_target)
        return 0.5 * _jnp.sum(_jnp.mean(err, axis=-1)) if err.ndim else 0.5 * err


def _adamw(w, g, m, v):
    m = ADAM_B1 * m + (1.0 - ADAM_B1) * g
    v = ADAM_B2 * v + (1.0 - ADAM_B2) * _jnp.square(g)
    m_hat = m / (1.0 - ADAM_B1 ** ADAM_STEP)
    v_hat = v / (1.0 - ADAM_B2 ** ADAM_STEP)
    delta = -ADAM_LR * (m_hat / (_jnp.sqrt(v_hat) + ADAM_EPS) + ADAM_WD * w)
    return delta, m, v


def reference(x, rel_bias, norm1_g, w_in, sgu_w, sgu_b, dil_qn_g, dil_kn_g, conv_w, conv_b, conv_ln_g, conv_ln_b, gqa_qn_g, gqa_kn_g, mix_norm_g, w_out, norm2_g, w_gate, w_up, w_down, loss_target, m_rel_bias, m_norm1_g, m_w_in, m_sgu_w, m_sgu_b, m_dil_qn_g, m_dil_kn_g, m_conv_w, m_conv_b, m_conv_ln_g, m_conv_ln_b, m_gqa_qn_g, m_gqa_kn_g, m_mix_norm_g, m_w_out, m_norm2_g, m_w_gate, m_w_up, m_w_down, v_rel_bias, v_norm1_g, v_w_in, v_sgu_w, v_sgu_b, v_dil_qn_g, v_dil_kn_g, v_conv_w, v_conv_b, v_conv_ln_g, v_conv_ln_b, v_gqa_qn_g, v_gqa_kn_g, v_mix_norm_g, v_w_out, v_norm2_g, v_w_gate, v_w_up, v_w_down):
    given = dict(x=x, rel_bias=rel_bias, norm1_g=norm1_g, w_in=w_in, sgu_w=sgu_w, sgu_b=sgu_b, dil_qn_g=dil_qn_g, dil_kn_g=dil_kn_g, conv_w=conv_w, conv_b=conv_b, conv_ln_g=conv_ln_g, conv_ln_b=conv_ln_b, gqa_qn_g=gqa_qn_g, gqa_kn_g=gqa_kn_g, mix_norm_g=mix_norm_g, w_out=w_out, norm2_g=norm2_g, w_gate=w_gate, w_up=w_up, w_down=w_down, loss_target=loss_target, m_rel_bias=m_rel_bias, m_norm1_g=m_norm1_g, m_w_in=m_w_in, m_sgu_w=m_sgu_w, m_sgu_b=m_sgu_b, m_dil_qn_g=m_dil_qn_g, m_dil_kn_g=m_dil_kn_g, m_conv_w=m_conv_w, m_conv_b=m_conv_b, m_conv_ln_g=m_conv_ln_g, m_conv_ln_b=m_conv_ln_b, m_gqa_qn_g=m_gqa_qn_g, m_gqa_kn_g=m_gqa_kn_g, m_mix_norm_g=m_mix_norm_g, m_w_out=m_w_out, m_norm2_g=m_norm2_g, m_w_gate=m_w_gate, m_w_up=m_w_up, m_w_down=m_w_down, v_rel_bias=v_rel_bias, v_norm1_g=v_norm1_g, v_w_in=v_w_in, v_sgu_w=v_sgu_w, v_sgu_b=v_sgu_b, v_dil_qn_g=v_dil_qn_g, v_dil_kn_g=v_dil_kn_g, v_conv_w=v_conv_w, v_conv_b=v_conv_b, v_conv_ln_g=v_conv_ln_g, v_conv_ln_b=v_conv_ln_b, v_gqa_qn_g=v_gqa_qn_g, v_gqa_kn_g=v_gqa_kn_g, v_mix_norm_g=v_mix_norm_g, v_w_out=v_w_out, v_norm2_g=v_norm2_g, v_w_gate=v_w_gate, v_w_up=v_w_up, v_w_down=v_w_down)
    weights = {n: given[n] for n in TWIN_WEIGHTS}
    shared = {n: given[n] for n in SHARED_INPUTS}
    per_example = {n: given[n] for n in ['x']}
    grad_fn = _jax.value_and_grad(_loss, argnums=(0, 1))

    def one_microbatch(ex, loss_target):
        ex = dict(ex)
        diff = ex.pop(TWIN_DIFF_INPUT)
        return grad_fn(weights, diff, {**shared, **ex}, loss_target)

    if N_MICROBATCH == 1:
        loss, (grad_w, grad_x) = one_microbatch(per_example, given["loss_target"])
    else:
        def body(carry, xs):
            loss_sum, grad_sum = carry
            l_k, (gw_k, gx_k) = one_microbatch(xs[0], xs[1])
            with _jax.named_scope("update"):
                return (loss_sum + l_k, _jax.tree.map(_jnp.add, grad_sum, gw_k)), gx_k

        init = (_jnp.zeros((), _jnp.float32), _jax.tree.map(_jnp.zeros_like, weights))
        (loss, grad_w), grad_x = _jax.lax.scan(body, init, (per_example, given["loss_target"]))
    with _jax.named_scope("update"):
        delta_w, new_m, new_v = {}, {}, {}
        for n in TWIN_WEIGHTS:
            delta_w[n], new_m[n], new_v[n] = _adamw(weights[n], grad_w[n], given["m_" + n], given["v_" + n])
    return (loss, grad_x, *[grad_w[n] for n in TWIN_WEIGHTS], *[delta_w[n] for n in TWIN_WEIGHTS],
            *[new_m[n] for n in TWIN_WEIGHTS], *[new_v[n] for n in TWIN_WEIGHTS])
```

```python
import functools
import math

import numpy as np
import jax
import jax.numpy as jnp
from jax import lax
from jax.experimental import pallas as pl
from jax.experimental.pallas import tpu as pltpu

F32 = jnp.float32
BF16 = jnp.bfloat16
HIGHEST = lax.Precision.HIGHEST
MESH_ID = pl.DeviceIdType.MESH

D_MODEL = 2048
DEPTH = 2
HEAD_DIM = 64
GROUP_WIDTH = 512
N_HEADS = 8
KV_HEADS = 2
KV_WIDTH = 128
SGU_CHUNK = 128
CONV_WIDTH = 31
CONV_PAD = 16
GRID_W = 64
ROPE_THETA = 10000.0
REL_BUCKETS = 32
REL_MAX_DIST = 1024
DIL_PATTERNS = ((128, 1), (512, 4), (2048, 16))
FFN_HIDDEN = 5632
IN_WIDTH = 4352
RMS_EPS = 1e-6
LN_EPS = 1e-5
MASKED = -1e30
N_DEV = 8

ADAM_LR = 0.001
ADAM_B1 = 0.9
ADAM_B2 = 0.999
ADAM_EPS = 1e-08
ADAM_WD = 0.01
ADAM_STEP = 10

COL_AU, COL_AV, COL_BQ, COL_BK, COL_BV, COL_CA, COL_CG, COL_DQ = range(8)
COL_DK128, COL_DV128 = 32, 33

VMEM_LIMIT = 56 * 1024 * 1024
ATTN_TQ = 256


def _params(sem=None, vmem=VMEM_LIMIT):
    return pltpu.CompilerParams(dimension_semantics=sem, vmem_limit_bytes=vmem)


def _dot(a, b, dims, precision=None):
    return lax.dot_general(a, b, (dims, ((), ())), precision=precision, preferred_element_type=F32)


def _nn(a, b, precision=None):
    return _dot(a, b, ((1,), (0,)), precision)


def _nt(a, b):
    return _dot(a, b, ((1,), (1,)))


def _tn(a, b):
    return _dot(a, b, ((0,), (0,)))


def _matmul(a, b, mode, name, res=None, out_dtype=F32, tm=512, tn=None, tk=None):
    if mode == "nn":
        (M, K), N = a.shape, b.shape[1]
    elif mode == "nt":
        (M, K), N = a.shape, b.shape[0]
    else:
        (K, M), N = a.shape, b.shape[1]
    tm = min(tm, M)

    def pick(n, cands):
        for c in cands:
            if n % c == 0:
                return c
        return n

    tn = tn or pick(N, (2176, 2048, 1408, 1024, 512))
    tk = tk or pick(K, (1024, 2176, 1408, 512))
    assert M % tm == 0 and N % tn == 0 and K % tk == 0, (M, N, K, tm, tn, tk)
    nk = K // tk
    dims = {"nn": ((1,), (0,)), "nt": ((1,), (1,)), "tn": ((0,), (0,))}[mode]

    def body(*refs):
        if res is None:
            a_ref, b_ref, o_ref, acc = refs
            r_ref = None
        else:
            a_ref, b_ref, r_ref, o_ref, acc = refs
        k = pl.program_id(2)

        @pl.when(k == 0)
        def _():
            acc[...] = jnp.zeros_like(acc)

        acc[...] += _dot(a_ref[...].astype(BF16), b_ref[...].astype(BF16), dims)

        @pl.when(k == nk - 1)
        def _():
            r = acc[...]
            if r_ref is not None:
                r = r + r_ref[...]
            o_ref[...] = r.astype(out_dtype)

    a_spec = pl.BlockSpec((tk, tm), lambda i, j, k: (k, i)) if mode == "tn" else pl.BlockSpec((tm, tk), lambda i, j, k: (i, k))
    b_spec = pl.BlockSpec((tn, tk), lambda i, j, k: (j, k)) if mode == "nt" else pl.BlockSpec((tk, tn), lambda i, j, k: (k, j))
    o_spec = pl.BlockSpec((tm, tn), lambda i, j, k: (i, j))
    ins, specs = [a, b], [a_spec, b_spec]
    if res is not None:
        ins.append(res)
        specs.append(o_spec)
    return pl.pallas_call(
        body, name=name, grid=(M // tm, N // tn, nk), in_specs=specs, out_specs=o_spec,
        out_shape=jax.ShapeDtypeStruct((M, N), out_dtype), scratch_shapes=[pltpu.VMEM((tm, tn), F32)],
        compiler_params=_params(("parallel", "parallel", "arbitrary")),
    )(*ins)


def _seg_matrix(width):
    return jnp.asarray(np.kron(np.eye(width // HEAD_DIM, dtype=np.float32), np.full((HEAD_DIM, HEAD_DIM), 1.0 / HEAD_DIM, np.float32)))


def _segmean(v, p):
    return _nn(v, p, HIGHEST)


def _gelu(x):
    c0 = math.sqrt(2.0 / math.pi)
    t = jnp.tanh(c0 * (x + 0.044715 * x * x * x))
    return 0.5 * x * (1.0 + t), t


def _gelu_grad(x, t):
    c0 = math.sqrt(2.0 / math.pi)
    return 0.5 * (1.0 + t) + 0.5 * x * (1.0 - t * t) * c0 * (1.0 + 3.0 * 0.044715 * x * x)


def _sigmoid(x):
    return 1.0 / (1.0 + jnp.exp(-x))


def _rms_fwd(x, g, name):
    T, D = x.shape
    tm = min(256, T)

    def body(x_ref, g_ref, o_ref):
        xv = x_ref[...]
        r = lax.rsqrt(jnp.mean(xv * xv, axis=-1, keepdims=True) + RMS_EPS)
        o_ref[...] = (xv * r * g_ref[...]).astype(BF16)

    return pl.pallas_call(
        body, name=name, grid=(T // tm,),
        in_specs=[pl.BlockSpec((tm, D), lambda i: (i, 0)), pl.BlockSpec((1, D), lambda i: (0, 0))],
        out_specs=pl.BlockSpec((tm, D), lambda i: (i, 0)), out_shape=jax.ShapeDtypeStruct((T, D), BF16),
        compiler_params=_params(("parallel",)),
    )(x, g.reshape(1, D))


def _rms_bwd(dh, x, g, dres, name):
    T, D = x.shape
    tm = min(256, T)

    def body(dh_ref, x_ref, g_ref, dres_ref, dx_ref, dg_ref):
        @pl.when(pl.program_id(0) == 0)
        def _():
            dg_ref[...] = jnp.zeros_like(dg_ref)

        xv, dhv = x_ref[...], dh_ref[...]
        r = lax.rsqrt(jnp.mean(xv * xv, axis=-1, keepdims=True) + RMS_EPS)
        y = xv * r
        dy = dhv * g_ref[...]
        dx_ref[...] = dres_ref[...] + r * (dy - y * jnp.mean(dy * y, axis=-1, keepdims=True))
        dg_ref[...] += jnp.sum(dhv * y, axis=0, keepdims=True)

    row = pl.BlockSpec((tm, D), lambda i: (i, 0))
    vec = pl.BlockSpec((1, D), lambda i: (0, 0))
    return pl.pallas_call(
        body, name=name, grid=(T // tm,), in_specs=[row, row, vec, row], out_specs=[row, vec],
        out_shape=[jax.ShapeDtypeStruct((T, D), F32), jax.ShapeDtypeStruct((1, D), F32)],
        compiler_params=_params(("arbitrary",)),
    )(dh, x, g.reshape(1, D), dres)


def _sgu_core(zu, zv, ws_ref, bias, p):
    ug, tu = _gelu(zu)
    vg, tv = _gelu(zv)
    xc = vg - _segmean(vg, p)
    rs = lax.rsqrt(_segmean(xc * xc, p) + LN_EPS)
    vn = xc * rs
    vnb = vn.astype(BF16)
    low = lax.broadcasted_iota(jnp.int32, (SGU_CHUNK, 128), 1) < HEAD_DIM
    parts = []
    for j in range(4):
        vp = vnb[:, 128 * j:128 * (j + 1)]
        parts.append(jnp.where(low, _nn(ws_ref[2 * j], vp), _nn(ws_ref[2 * j + 1], vp)))
    mixed = jnp.concatenate(parts, axis=1) + bias
    return ug, tu, tv, rs, vn, vnb, mixed, low


def _sgu_fwd(z, ws, bias, name):
    T = z.shape[0]

    def body(zu_ref, zv_ref, ws_ref, b_ref, p_ref, y_ref):
        ug, _, _, _, _, _, mixed, _ = _sgu_core(zu_ref[...], zv_ref[...], ws_ref, b_ref[...], p_ref[...])
        y_ref[...] = ug * mixed

    full = lambda shape: pl.BlockSpec(shape, lambda i: (0,) * len(shape))
    return pl.pallas_call(
        body, name=name, grid=(T // SGU_CHUNK,),
        in_specs=[pl.BlockSpec((SGU_CHUNK, 512), lambda i: (i, COL_AU)), pl.BlockSpec((SGU_CHUNK, 512), lambda i: (i, COL_AV)),
                  full((8, 128, 128)), full((128, 512)), full((512, 512))],
        out_specs=pl.BlockSpec((SGU_CHUNK, 512), lambda i: (i, 0)), out_shape=jax.ShapeDtypeStruct((T, 512), F32),
        compiler_params=_params(("parallel",)),
    )(z, z, ws, bias, _seg_matrix(512))


def _sgu_bwd(z, dy, ws, ws_t, bias, name):
    T = z.shape[0]

    def body(zu_ref, zv_ref, dy_ref, ws_ref, wst_ref, b_ref, p_ref, dz_ref, dws_ref, db_ref):
        @pl.when(pl.program_id(0) == 0)
        def _():
            dws_ref[...] = jnp.zeros_like(dws_ref)
            db_ref[...] = jnp.zeros_like(db_ref)

        zu, zv, p = zu_ref[...], zv_ref[...], p_ref[...]
        ug, tu, tv, rs, vn, vnb, mixed, low = _sgu_core(zu, zv, ws_ref, b_ref[...], p)
        dyv = dy_ref[...]
        dmixed = dyv * ug
        db_ref[...] += dmixed
        dmb = dmixed.astype(BF16)
        zero = jnp.zeros((SGU_CHUNK, 128), BF16)
        parts = []
        for j in range(4):
            dmp, vp = dmb[:, 128 * j:128 * (j + 1)], vnb[:, 128 * j:128 * (j + 1)]
            dws_ref[2 * j] += _nt(jnp.where(low, dmp, zero), vp)
            dws_ref[2 * j + 1] += _nt(jnp.where(low, zero, dmp), vp)
            parts.append(jnp.where(low, _nn(wst_ref[2 * j], dmp), _nn(wst_ref[2 * j + 1], dmp)))
        dvn = jnp.concatenate(parts, axis=1)
        dvg = rs * (dvn - _segmean(dvn, p) - vn * _segmean(dvn * vn, p))
        dz_ref[:, 0:512] = dyv * mixed * _gelu_grad(zu, tu)
        dz_ref[:, 512:1024] = dvg * _gelu_grad(zv, tv)

    full = lambda shape: pl.BlockSpec(shape, lambda i: (0,) * len(shape))
    return pl.pallas_call(
        body, name=name, grid=(T // SGU_CHUNK,),
        in_specs=[pl.BlockSpec((SGU_CHUNK, 512), lambda i: (i, COL_AU)), pl.BlockSpec((SGU_CHUNK, 512), lambda i: (i, COL_AV)),
                  pl.BlockSpec((SGU_CHUNK, 512), lambda i: (i, 0)), full((8, 128, 128)), full((8, 128, 128)), full((128, 512)), full((512, 512))],
        out_specs=[pl.BlockSpec((SGU_CHUNK, 1024), lambda i: (i, 0)), full((8, 128, 128)), full((128, 512))],
        out_shape=[jax.ShapeDtypeStruct((T, 1024), F32), jax.ShapeDtypeStruct((8, 128, 128), F32), jax.ShapeDtypeStruct((128, 512), F32)],
        compiler_params=_params(("arbitrary",)),
    )(z, z, dy, ws, ws_t, bias, _seg_matrix(512))


CONV_ROWS = 256


def _conv_taps(pad_ref, w_ref, base, flip):
    blk = pad_ref[pl.ds(base, CONV_ROWS + 2 * CONV_PAD), :]
    acc = jnp.zeros((CONV_ROWS, blk.shape[1]), F32)
    for k in range(CONV_WIDTH):
        wk = w_ref[CONV_WIDTH - 1 - k if flip else k]
        acc = acc + wk * blk[k + 1:k + 1 + CONV_ROWS, :]
    return acc


def _conv_fwd1(z, w, cb, B, S, name):
    T = B * S
    rows = min(CONV_ROWS, S)
    assert rows == CONV_ROWS

    def body(a_ref, g_ref, w_ref, cb_ref, c_ref, pad):
        pad[0:CONV_PAD, :] = jnp.zeros((CONV_PAD, 128), F32)
        pad[CONV_PAD + S:2 * CONV_PAD + S, :] = jnp.zeros((CONV_PAD, 128), F32)
        pad[CONV_PAD:CONV_PAD + S, :] = a_ref[...] * _sigmoid(g_ref[...])

        def tile(r, carry):
            base = pl.multiple_of(r * CONV_ROWS, CONV_ROWS)
            c_ref[pl.ds(base, CONV_ROWS), :] = _conv_taps(pad, w_ref, base, False) + cb_ref[...]
            return carry

        lax.fori_loop(0, S // CONV_ROWS, tile, 0)

    return pl.pallas_call(
        body, name=name, grid=(4, B),
        in_specs=[pl.BlockSpec((S, 128), lambda j, b: (b, 4 * COL_CA + j)), pl.BlockSpec((S, 128), lambda j, b: (b, 4 * COL_CG + j)),
                  pl.BlockSpec((32, 1, 128), lambda j, b: (0, 0, j)), pl.BlockSpec((1, 128), lambda j, b: (0, j))],
        out_specs=pl.BlockSpec((S, 128), lambda j, b: (b, j)), out_shape=jax.ShapeDtypeStruct((T, 512), F32),
        scratch_shapes=[pltpu.VMEM((S + 2 * CONV_PAD, 128), F32)], compiler_params=_params(("parallel", "parallel")),
    )(z, z, w, cb)


def _ln_rows(c):
    mu = jnp.mean(c, axis=-1, keepdims=True)
    xc = c - mu
    rs = lax.rsqrt(jnp.mean(xc * xc, axis=-1, keepdims=True) + LN_EPS)
    return xc * rs, rs


def _conv_fwd2(c, lng, lnb, name):
    T = c.shape[0]
    tm = min(512, T)

    def body(c_ref, g_ref, b_ref, y_ref):
        n, _ = _ln_rows(c_ref[...])
        t = n * g_ref[...] + b_ref[...]
        y_ref[...] = t * _sigmoid(t)

    row = pl.BlockSpec((tm, 512), lambda i: (i, 0))
    vec = pl.BlockSpec((1, 512), lambda i: (0, 0))
    return pl.pallas_call(body, name=name, grid=(T // tm,), in_specs=[row, vec, vec], out_specs=row,
                          out_shape=jax.ShapeDtypeStruct((T, 512), F32), compiler_params=_params(("parallel",)))(c, lng, lnb)


def _conv_bwd1(c, dy, lng, lnb, name):
    T = c.shape[0]
    tm = min(512, T)

    def body(c_ref, dy_ref, g_ref, b_ref, dc_ref, dg_ref, db_ref, dcb_ref):
        @pl.when(pl.program_id(0) == 0)
        def _():
            dg_ref[...] = jnp.zeros_like(dg_ref)
            db_ref[...] = jnp.zeros_like(db_ref)
            dcb_ref[...] = jnp.zeros_like(dcb_ref)

        n, rs = _ln_rows(c_ref[...])
        t = n * g_ref[...] + b_ref[...]
        s = _sigmoid(t)
        dt = dy_ref[...] * s * (1.0 + t * (1.0 - s))
        dg_ref[...] += jnp.sum(dt * n, axis=0, keepdims=True)
        db_ref[...] += jnp.sum(dt, axis=0, keepdims=True)
        dn = dt * g_ref[...]
        dc = rs * (dn - jnp.mean(dn, axis=-1, keepdims=True) - n * jnp.mean(dn * n, axis=-1, keepdims=True))
        dc_ref[...] = dc
        dcb_ref[...] += jnp.sum(dc, axis=0, keepdims=True)

    row = pl.BlockSpec((tm, 512), lambda i: (i, 0))
    vec = pl.BlockSpec((1, 512), lambda i: (0, 0))
    vshape = jax.ShapeDtypeStruct((1, 512), F32)
    return pl.pallas_call(body, name=name, grid=(T // tm,), in_specs=[row, row, vec, vec], out_specs=[row, vec, vec, vec],
                          out_shape=[jax.ShapeDtypeStruct((T, 512), F32), vshape, vshape, vshape],
                          compiler_params=_params(("arbitrary",)))(c, dy, lng, lnb)


def _conv_bwd2(z, dc, w, B, S, name):
    T = B * S

    def body(a_ref, g_ref, dc_ref, w_ref, da_ref, dg_ref, dw_ref, hpad, dpad, dwacc):
        @pl.when(pl.program_id(1) == 0)
        def _():
            dw_ref[...] = jnp.zeros_like(dw_ref)

        zeros = jnp.zeros((CONV_PAD, 128), F32)
        for ref in (hpad, dpad):
            ref[0:CONV_PAD, :] = zeros
            ref[CONV_PAD + S:2 * CONV_PAD + S, :] = zeros
        hpad[CONV_PAD:CONV_PAD + S, :] = a_ref[...] * _sigmoid(g_ref[...])
        dpad[CONV_PAD:CONV_PAD + S, :] = dc_ref[...]
        dwacc[...] = jnp.zeros_like(dwacc)

        def tile(r, carry):
            base = pl.multiple_of(r * CONV_ROWS, CONV_ROWS)
            dh = _conv_taps(dpad, w_ref, base, True)
            av, gv = a_ref[pl.ds(base, CONV_ROWS), :], g_ref[pl.ds(base, CONV_ROWS), :]
            sg = _sigmoid(gv)
            da_ref[pl.ds(base, CONV_ROWS), :] = dh * sg
            dg_ref[pl.ds(base, CONV_ROWS), :] = dh * av * sg * (1.0 - sg)
            dcv = dc_ref[pl.ds(base, CONV_ROWS), :]
            blk = hpad[pl.ds(base, CONV_ROWS + 2 * CONV_PAD), :]
            for k in range(CONV_WIDTH):
                prod = dcv * blk[k + 1:k + 1 + CONV_ROWS, :]
                dwacc[k] += jnp.sum(prod.reshape(CONV_ROWS // 8, 8, 128), axis=0)
            return carry

        lax.fori_loop(0, S // CONV_ROWS, tile, 0)
        for k in range(CONV_WIDTH):
            dw_ref[k] += jnp.sum(dwacc[k], axis=0, keepdims=True)

    return pl.pallas_call(
        body, name=name, grid=(4, B),
        in_specs=[pl.BlockSpec((S, 128), lambda j, b: (b, 4 * COL_CA + j)), pl.BlockSpec((S, 128), lambda j, b: (b, 4 * COL_CG + j)),
                  pl.BlockSpec((S, 128), lambda j, b: (b, j)), pl.BlockSpec((32, 1, 128), lambda j, b: (0, 0, j))],
        out_specs=[pl.BlockSpec((S, 128), lambda j, b: (b, j)), pl.BlockSpec((S, 128), lambda j, b: (b, j)),
                   pl.BlockSpec((32, 1, 128), lambda j, b: (0, 0, j))],
        out_shape=[jax.ShapeDtypeStruct((T, 512), F32), jax.ShapeDtypeStruct((T, 512), F32), jax.ShapeDtypeStruct((32, 1, 512), F32)],
        scratch_shapes=[pltpu.VMEM((S + 2 * CONV_PAD, 128), F32), pltpu.VMEM((S + 2 * CONV_PAD, 128), F32), pltpu.VMEM((32, 8, 128), F32)],
        compiler_params=_params(("parallel", "arbitrary")),
    )(z, z, dc, w)


def _swap16(x):
    n = x.shape[1]
    first = (lax.broadcasted_iota(jnp.int32, x.shape, 1) % 32) < 16
    return jnp.where(first, pltpu.roll(x, n - 16, 1), pltpu.roll(x, 16, 1))


def _rope(x, cos, sin):
    return x * cos + _swap16(x) * sin


def _rope_t(dy, cos, sin):
    return dy * cos + _swap16(dy * sin)


def _qk_norm(x, p):
    r = lax.rsqrt(_segmean(x * x, p) + RMS_EPS)
    return x * r, r


def _store_heads(ref, val, n):
    for h in range(n):
        ref[h] = val[:, HEAD_DIM * h:HEAD_DIM * (h + 1)].astype(ref.dtype)


def _load_heads(ref, n):
    return jnp.concatenate([ref[h] for h in range(n)], axis=1)


def _prep_fwd(z, gq, gk, rope, B, S, kv_heads, cols, name):
    tm = min(256, S)
    ns = S // tm
    kw = kv_heads * HEAD_DIM
    scale = HEAD_DIM ** -0.5
    qc, kc, vc = cols

    def body(*refs):
        if rope is None:
            q_ref, k_ref, v_ref, gq_ref, gk_ref, p_ref, qo, ko, vo = refs
        else:
            q_ref, k_ref, v_ref, gq_ref, gk_ref, p_ref, cos_ref, sin_ref, qo, ko, vo = refs
        p = p_ref[...]
        qn, _ = _qk_norm(q_ref[...], p)
        kn, _ = _qk_norm(k_ref[...], p[:kw, :kw])
        qn, kn = qn * gq_ref[...], kn * gk_ref[...]
        if rope is not None:
            cos, sin = cos_ref[...], sin_ref[...]
            qn, kn = _rope(qn, cos, sin), _rope(kn, cos[:, :kw], sin[:, :kw])
        _store_heads(qo, qn * scale, N_HEADS)
        _store_heads(ko, kn, kv_heads)
        _store_heads(vo, v_ref[...], kv_heads)

    row = lambda w, c: pl.BlockSpec((tm, w), lambda b, i: (b * ns + i, c))
    const = lambda shape: pl.BlockSpec(shape, lambda b, i: (0,) * len(shape))
    heads = lambda n: pl.BlockSpec((None, n, tm, HEAD_DIM), lambda b, i: (b, 0, i, 0))
    ins = [z, z, z, gq, gk, _seg_matrix(512)]
    specs = [row(512, qc), row(kw, kc), row(kw, vc), const((1, 512)), const((1, kw)), const((512, 512))]
    if rope is not None:
        ins += list(rope)
        specs += [pl.BlockSpec((tm, 512), lambda b, i: (i, 0))] * 2
    return pl.pallas_call(
        body, name=name, grid=(B, ns), in_specs=specs, out_specs=[heads(N_HEADS), heads(kv_heads), heads(kv_heads)],
        out_shape=[jax.ShapeDtypeStruct((B, N_HEADS, S, HEAD_DIM), BF16), jax.ShapeDtypeStruct((B, kv_heads, S, HEAD_DIM), BF16),
                   jax.ShapeDtypeStruct((B, kv_heads, S, HEAD_DIM), BF16)],
        compiler_params=_params(("parallel", "parallel")),
    )(*ins)


def _prep_bwd(z, dq, dk, dv, gq, gk, rope, B, S, kv_heads, cols, name):
    T = B * S
    tm = min(256, S)
    ns = S // tm
    kw = kv_heads * HEAD_DIM
    scale = HEAD_DIM ** -0.5
    qc, kc, _ = cols

    def body(*refs):
        if rope is None:
            q_ref, k_ref, dq_ref, dk_ref, dv_ref, gq_ref, gk_ref, p_ref, dz_ref, dgq_ref, dgk_ref = refs
        else:
            q_ref, k_ref, dq_ref, dk_ref, dv_ref, gq_ref, gk_ref, p_ref, cos_ref, sin_ref, dz_ref, dgq_ref, dgk_ref = refs

        @pl.when((pl.program_id(0) == 0) & (pl.program_id(1) == 0))
        def _():
            dgq_ref[...] = jnp.zeros_like(dgq_ref)
            dgk_ref[...] = jnp.zeros_like(dgk_ref)

        p = p_ref[...]
        dqv = _load_heads(dq_ref, N_HEADS) * scale
        dkv = _load_heads(dk_ref, kv_heads)
        if rope is not None:
            cos, sin = cos_ref[...], sin_ref[...]
            dqv, dkv = _rope_t(dqv, cos, sin), _rope_t(dkv, cos[:, :kw], sin[:, :kw])

        def through_norm(xv, dy, g, pm, dg_ref):
            xh, r = _qk_norm(xv, pm)
            dg_ref[...] += jnp.sum(dy * xh, axis=0, keepdims=True)
            dxh = dy * g
            return r * (dxh - xh * _segmean(dxh * xh, pm))

        dz_ref[:, 0:512] = through_norm(q_ref[...], dqv, gq_ref[...], p, dgq_ref)
        dz_ref[:, 512:512 + kw] = through_norm(k_ref[...], dkv, gk_ref[...], p[:kw, :kw], dgk_ref)
        dz_ref[:, 512 + kw:512 + 2 * kw] = _load_heads(dv_ref, kv_heads)

    row = lambda w, c: pl.BlockSpec((tm, w), lambda b, i: (b * ns + i, c))
    const = lambda shape: pl.BlockSpec(shape, lambda b, i: (0,) * len(shape))
    heads = lambda n: pl.BlockSpec((None, n, tm, HEAD_DIM), lambda b, i: (b, 0, i, 0))
    ins = [z, z, dq, dk, dv, gq, gk, _seg_matrix(512)]
    specs = [row(512, qc), row(kw, kc), heads(N_HEADS), heads(kv_heads), heads(kv_heads), const((1, 512)), const((1, kw)), const((512, 512))]
    if rope is not None:
        ins += list(rope)
        specs += [pl.BlockSpec((tm, 512), lambda b, i: (i, 0))] * 2
    return pl.pallas_call(
        body, name=name, grid=(B, ns), in_specs=specs, out_specs=[row(512 + 2 * kw, 0), const((1, 512)), const((1, kw))],
        out_shape=[jax.ShapeDtypeStruct((T, 512 + 2 * kw), F32), jax.ShapeDtypeStruct((1, 512), F32), jax.ShapeDtypeStruct((1, kw), F32)],
        compiler_params=_params(("arbitrary", "arbitrary")),
    )(*ins)


def _toeplitz(win, tq, S):
    r = pltpu.roll(jnp.broadcast_to(win, (tq, S + tq)), 0, 1, stride=1, stride_axis=0)
    return r[:, tq:tq + S]


def _attn_fwd(q, k, v, win, name):
    B, H, S, _ = q.shape
    group = H // k.shape[1]
    tq = min(ATTN_TQ, S)

    def body(*refs):
        if win is None:
            q_ref, k_ref, v_ref, o_ref = refs
        else:
            q_ref, k_ref, v_ref, w_ref, o_ref = refs
        s = _nt(q_ref[...], k_ref[...])
        if win is not None:
            s = s + _toeplitz(w_ref[...], tq, S)
        p = jnp.exp(s - jnp.max(s, axis=-1, keepdims=True))
        l = jnp.sum(p, axis=-1, keepdims=True)
        o_ref[...] = _nn(p.astype(BF16), v_ref[...]) / l

    qs = pl.BlockSpec((None, None, tq, HEAD_DIM), lambda b, h, i: (b, h, i, 0))
    ks = pl.BlockSpec((None, None, S, HEAD_DIM), lambda b, h, i: (b, h // group, 0, 0))
    ins, specs = [q, k, v], [qs, ks, ks]
    if win is not None:
        ins.append(win)
        specs.append(pl.BlockSpec((None, None, 1, S + tq), lambda b, h, i: (h, i, 0, 0)))
    return pl.pallas_call(body, name=name, grid=(B, H, S // tq), in_specs=specs, out_specs=qs,
                          out_shape=jax.ShapeDtypeStruct((B, H, S, HEAD_DIM), F32),
                          compiler_params=_params(("parallel", "parallel", "parallel")))(*ins)


def _attn_bwd(q, k, v, o, do, win, name):
    B, H, S, _ = q.shape
    hkv = k.shape[1]
    group = H // hkv
    tq = min(ATTN_TQ, S)
    nq = S // tq

    def body(*refs):
        if win is None:
            q_ref, k_ref, v_ref, o_ref, do_ref, dq_ref, dk_ref, dv_ref = refs
        else:
            q_ref, k_ref, v_ref, o_ref, do_ref, w_ref, rev_ref, dq_ref, dk_ref, dv_ref, dw_ref = refs

        @pl.when((pl.program_id(2) == 0) & (pl.program_id(3) == 0))
        def _():
            dk_ref[...] = jnp.zeros_like(dk_ref)
            dv_ref[...] = jnp.zeros_like(dv_ref)

        qv, kv, vv = q_ref[...], k_ref[...], v_ref[...]
        s = _nt(qv, kv)
        if win is not None:
            s = s + _toeplitz(w_ref[...], tq, S)
        p = jnp.exp(s - jnp.max(s, axis=-1, keepdims=True))
        p = p / jnp.sum(p, axis=-1, keepdims=True)
        dov = do_ref[...]
        dob = dov.astype(BF16)
        dp = _nt(dob, vv)
        ds = p * (dp - jnp.sum(dov * o_ref[...], axis=-1, keepdims=True))
        dsb = ds.astype(BF16)
        dv_ref[...] += _tn(p.astype(BF16), dob)
        dk_ref[...] += _tn(dsb, qv)
        dq_ref[...] = _nn(dsb, kv)
        if win is not None:
            rev = _nn(rev_ref[...], dsb)
            wide = jnp.concatenate([rev, jnp.zeros((tq, tq), F32)], axis=1)
            dw_ref[...] = jnp.sum(pltpu.roll(wide, 0, 1, stride=1, stride_axis=0), axis=0, keepdims=True)

    qs = pl.BlockSpec((None, None, tq, HEAD_DIM), lambda b, h, g, i: (b, h * group + g, i, 0))
    ks = pl.BlockSpec((None, None, S, HEAD_DIM), lambda b, h, g, i: (b, h, 0, 0))
    ins, specs = [q, k, v, o, do], [qs, ks, ks, qs, qs]
    outs = [jax.ShapeDtypeStruct((B, H, S, HEAD_DIM), F32), jax.ShapeDtypeStruct((B, hkv, S, HEAD_DIM), F32), jax.ShapeDtypeStruct((B, hkv, S, HEAD_DIM), F32)]
    ospecs = [qs, ks, ks]
    if win is not None:
        ins += [win, jnp.asarray(np.eye(tq, dtype=np.float32)[::-1].copy(), BF16)]
        specs += [pl.BlockSpec((None, None, 1, S + tq), lambda b, h, g, i: (h * group + g, i, 0, 0)), pl.BlockSpec((tq, tq), lambda b, h, g, i: (0, 0))]
        outs.append(jax.ShapeDtypeStruct((B, H, nq, 1, S + tq), F32))
        ospecs.append(pl.BlockSpec((None, None, None, 1, S + tq), lambda b, h, g, i: (b, h * group + g, i, 0, 0)))
    return pl.pallas_call(body, name=name, grid=(B, hkv, group, nq), in_specs=specs, out_specs=ospecs, out_shape=outs,
                          compiler_params=_params(("parallel", "parallel", "arbitrary", "arbitrary")))(*ins)


def _pattern_count(delta):
    n = jnp.zeros(delta.shape, jnp.int32)
    for window, dil in DIL_PATTERNS:
        n = n + ((delta % dil == 0) & (jnp.abs(delta) <= window // 2)).astype(jnp.int32)
    return n


def _t5_bucket(rel):
    nb = REL_BUCKETS // 2
    max_exact = nb // 2
    ret = jnp.where(rel > 0, nb, 0)
    n = jnp.abs(rel)
    nf = jnp.maximum(n, 1).astype(F32)
    large = max_exact + (jnp.log(nf / max_exact) / math.log(REL_MAX_DIST / max_exact) * (nb - max_exact)).astype(jnp.int32)
    large = jnp.minimum(large, nb - 1)
    return ret + jnp.where(n < max_exact, n, large)


def _bias_windows(rel_bias, S):
    tq = min(ATTN_TQ, S)
    nq = S // tq
    delta = jnp.arange(S + tq)[None, :] - (jnp.arange(nq)[:, None] + 1) * tq
    count = _pattern_count(delta)
    val = rel_bias[_t5_bucket(delta)] + jnp.log(jnp.maximum(count, 1).astype(F32))[..., None]
    val = jnp.where((count > 0)[..., None], val, MASKED)
    return val.transpose(2, 0, 1)[:, :, None, :]


def _bias_fold(dwin, S, name):
    B, H, nq = dwin.shape[:3]
    tq = min(ATTN_TQ, S)
    n = nq * (S + tq)
    delta = (jnp.arange(S + tq)[None, :] - (tq - 1) - jnp.arange(nq)[:, None] * tq).reshape(n)
    onehot = (_t5_bucket(delta)[:, None] == jnp.arange(128)[None, :]).astype(F32)

    def body(d_ref, oh_ref, o_ref):
        tot = d_ref[0]
        for b in range(1, B):
            tot = tot + d_ref[b]
        o_ref[...] = _nn(tot, oh_ref[...], HIGHEST)

    out = pl.pallas_call(body, name=name, out_shape=jax.ShapeDtypeStruct((H, 128), F32), compiler_params=_params())(dwin.reshape(B, H, n), onehot)
    return out[:, :REL_BUCKETS].T


def _rope_tables(S):
    half = 16
    freqs = ROPE_THETA ** (-jnp.arange(half, dtype=F32) / half)
    t = jnp.arange(S)
    ang_r = (t // GRID_W).astype(F32)[:, None] * freqs[None, :]
    ang_c = (t % GRID_W).astype(F32)[:, None] * freqs[None, :]
    cos = jnp.concatenate([jnp.cos(ang_r)] * 2 + [jnp.cos(ang_c)] * 2, axis=1)
    sin = jnp.concatenate([-jnp.sin(ang_r), jnp.sin(ang_r), -jnp.sin(ang_c), jnp.sin(ang_c)], axis=1)
    return jnp.tile(cos, (1, N_HEADS)), jnp.tile(sin, (1, N_HEADS))


def _mix_fwd(ya, ob, yc, od, gain, B, S, name):
    T = B * S
    tm = min(256, S)
    ns = S // tm

    def body(ya_ref, ob_ref, yc_ref, od_ref, g_ref, o_ref):
        ys = [ya_ref[...], _load_heads(ob_ref, N_HEADS), yc_ref[...], _load_heads(od_ref, N_HEADS)]
        for m, y in enumerate(ys):
            r = lax.rsqrt(jnp.mean(y * y, axis=-1, keepdims=True) + RMS_EPS)
            o_ref[:, 512 * m:512 * (m + 1)] = (y * r * g_ref[:, 512 * m:512 * (m + 1)]).astype(BF16)

    row = pl.BlockSpec((tm, 512), lambda b, i: (b * ns + i, 0))
    heads = pl.BlockSpec((None, N_HEADS, tm, HEAD_DIM), lambda b, i: (b, 0, i, 0))
    return pl.pallas_call(
        body, name=name, grid=(B, ns), in_specs=[row, heads, row, heads, pl.BlockSpec((1, 2048), lambda b, i: (0, 0))],
        out_specs=pl.BlockSpec((tm, 2048), lambda b, i: (b * ns + i, 0)), out_shape=jax.ShapeDtypeStruct((T, 2048), BF16),
        compiler_params=_params(("parallel", "parallel")),
    )(ya, ob, yc, od, gain)


def _mix_bwd(ya, ob, yc, od, dycat, gain, B, S, name):
    T = B * S
    tm = min(256, S)
    ns = S // tm

    def body(ya_ref, ob_ref, yc_ref, od_ref, dy_ref, g_ref, dya_ref, dob_ref, dyc_ref, dod_ref, dg_ref):
        @pl.when((pl.program_id(0) == 0) & (pl.program_id(1) == 0))
        def _():
            dg_ref[...] = jnp.zeros_like(dg_ref)

        ys = [ya_ref[...], _load_heads(ob_ref, N_HEADS), yc_ref[...], _load_heads(od_ref, N_HEADS)]
        outs = [dya_ref, dob_ref, dyc_ref, dod_ref]
        for m, y in enumerate(ys):
            cols = slice(512 * m, 512 * (m + 1))
            r = lax.rsqrt(jnp.mean(y * y, axis=-1, keepdims=True) + RMS_EPS)
            yh = y * r
            dh = dy_ref[:, cols]
            dg_ref[:, cols] += jnp.sum(dh * yh, axis=0, keepdims=True)
            dyh = dh * g_ref[:, cols]
            dyv = r * (dyh - yh * jnp.mean(dyh * yh, axis=-1, keepdims=True))
            if m % 2 == 0:
                outs[m][...] = dyv
            else:
                _store_heads(outs[m], dyv, N_HEADS)

    row = pl.BlockSpec((tm, 512), lambda b, i: (b * ns + i, 0))
    heads = pl.BlockSpec((None, N_HEADS, tm, HEAD_DIM), lambda b, i: (b, 0, i, 0))
    vec = pl.BlockSpec((1, 2048), lambda b, i: (0, 0))
    flat = jax.ShapeDtypeStruct((T, 512), F32)
    hm = jax.ShapeDtypeStruct((B, N_HEADS, S, HEAD_DIM), F32)
    return pl.pallas_call(
        body, name=name, grid=(B, ns), in_specs=[row, heads, row, heads, pl.BlockSpec((tm, 2048), lambda b, i: (b * ns + i, 0)), vec],
        out_specs=[row, heads, row, heads, vec], out_shape=[flat, hm, flat, hm, jax.ShapeDtypeStruct((1, 2048), F32)],
        compiler_params=_params(("arbitrary", "arbitrary")),
    )(ya, ob, yc, od, dycat, gain)


def _swiglu_fwd(gate, up, name):
    T, Fh = gate.shape
    tm, tn = min(512, T), 1408

    def body(g_ref, u_ref, o_ref):
        g = g_ref[...]
        o_ref[...] = (g * _sigmoid(g) * u_ref[...]).astype(BF16)

    blk = pl.BlockSpec((tm, tn), lambda i, j: (i, j))
    return pl.pallas_call(body, name=name, grid=(T // tm, Fh // tn), in_specs=[blk, blk], out_specs=blk,
                          out_shape=jax.ShapeDtypeStruct((T, Fh), BF16), compiler_params=_params(("parallel", "parallel")))(gate, up)


def _swiglu_bwd(gate, up, dact, name):
    T, Fh = gate.shape
    tm, tn = min(512, T), 1408

    def body(g_ref, u_ref, d_ref, dg_ref, du_ref):
        g, d = g_ref[...], d_ref[...]
        s = _sigmoid(g)
        dg_ref[...] = (d * u_ref[...] * s * (1.0 + g * (1.0 - s))).astype(BF16)
        du_ref[...] = (d * g * s).astype(BF16)

    blk = pl.BlockSpec((tm, tn), lambda i, j: (i, j))
    shape = jax.ShapeDtypeStruct((T, Fh), BF16)
    return pl.pallas_call(body, name=name, grid=(T // tm, Fh // tn), in_specs=[blk, blk, blk], out_specs=[blk, blk],
                          out_shape=[shape, shape], compiler_params=_params(("parallel", "parallel")))(gate, up, dact)


def _loss_grad(y, target, name):
    T, D = y.shape
    tm = min(256, T)
    n = T // tm

    def body(y_ref, t_ref, loss_ref, dy_ref, acc):
        i = pl.program_id(0)

        @pl.when(i == 0)
        def _():
            acc[...] = jnp.zeros_like(acc)

        err = y_ref[...] - t_ref[...]
        dy_ref[...] = err * (1.0 / D)
        acc[...] += jnp.sum((err * err).reshape(tm // 8, 8, D), axis=0)

        @pl.when(i == n - 1)
        def _():
            loss_ref[...] = jnp.full((8, 128), 0.5 / D, F32) * jnp.sum(acc[...])

    row = pl.BlockSpec((tm, D), lambda i: (i, 0))
    return pl.pallas_call(body, name=name, grid=(n,), in_specs=[row, row], out_specs=[pl.BlockSpec((8, 128), lambda i: (0, 0)), row],
                          out_shape=[jax.ShapeDtypeStruct((8, 128), F32), jax.ShapeDtypeStruct((T, D), F32)],
                          scratch_shapes=[pltpu.VMEM((8, D), F32)], compiler_params=_params(("arbitrary",)))(y, target)


def _adamw(w, m, v, partials, name):
    R, C = w.shape
    tm = R
    for cand in (512, 256, 128, 64, 8):
        if R % cand == 0:
            tm = cand
            break
    npart = len(partials)
    c1 = 1.0 - ADAM_B1 ** ADAM_STEP
    c2 = 1.0 - ADAM_B2 ** ADAM_STEP

    def body(*refs):
        w_ref, m_ref, v_ref = refs[:3]
        g_ref, d_ref, mo_ref, vo_ref = refs[3 + npart:]
        g = refs[3][...].astype(F32)
        for r in refs[4:3 + npart]:
            g = g + r[...].astype(F32)
        mn = ADAM_B1 * m_ref[...] + (1.0 - ADAM_B1) * g
        vn = ADAM_B2 * v_ref[...] + (1.0 - ADAM_B2) * (g * g)
        g_ref[...] = g
        mo_ref[...] = mn
        vo_ref[...] = vn
        d_ref[...] = -ADAM_LR * ((mn / c1) / (jnp.sqrt(vn / c2) + ADAM_EPS) + ADAM_WD * w_ref[...])

    blk = pl.BlockSpec((tm, C), lambda i: (i, 0))
    shape = jax.ShapeDtypeStruct((R, C), F32)
    return pl.pallas_call(body, name=name, grid=(R // tm,), in_specs=[blk] * (3 + npart), out_specs=[blk] * 4,
                          out_shape=[shape] * 4, compiler_params=_params(("parallel",)))(w, m, v, *partials)


def _sum_rows(stack, name):
    n, R, C = stack.shape
    tm = R
    for cand in (512, 256, 128, 64, 8):
        if R % cand == 0:
            tm = cand
            break

    def body(s_ref, o_ref):
        tot = s_ref[0]
        for k in range(1, n):
            tot = tot + s_ref[k]
        o_ref[...] = tot

    return pl.pallas_call(body, name=name, grid=(R // tm,), in_specs=[pl.BlockSpec((n, tm, C), lambda i: (0, i, 0))],
                          out_specs=pl.BlockSpec((tm, C), lambda i: (i, 0)), out_shape=jax.ShapeDtypeStruct((R, C), F32),
                          compiler_params=_params(("parallel",)))(stack)


def _add_cast(a, b, name):
    shape = a.shape
    C = shape[-1]
    R = int(np.prod(shape[:-1]))
    a2, b2 = a.reshape(R, C), b.reshape(R, C)
    tm = 512 if R % 512 == 0 else 256

    def body(a_ref, b_ref, o_ref):
        o_ref[...] = (a_ref[...].astype(F32) + b_ref[...].astype(F32)).astype(BF16)

    blk = pl.BlockSpec((tm, C), lambda i: (i, 0))
    out = pl.pallas_call(body, name=name, grid=(R // tm,), in_specs=[blk, blk], out_specs=blk,
                         out_shape=jax.ShapeDtypeStruct((R, C), BF16), compiler_params=_params(("parallel",)))(a2, b2)
    return out.reshape(shape)


ANY = pl.BlockSpec(memory_space=pl.ANY)


def _place():
    x, y, c = lax.axis_index("x"), lax.axis_index("y"), lax.axis_index("c")
    return x, y, c, [(1 - x, y), (x, 1 - y), (1 - x, 1 - y)]


def _all_gather(arrs, name):
    n = len(arrs)

    def body(*refs):
        ins, outs = refs[:n], refs[n:2 * n]
        send_sems, recv_sems, local_sems = refs[2 * n:]
        x, y, c, chips = _place()
        me, sibling = 4 * x + 2 * y + c, (x, y, 1 - c)

        def copy(a, k, block, to, src=None):
            return pltpu.make_async_remote_copy(
                src_ref=outs[a].at[block] if src is None else src, dst_ref=outs[a].at[block],
                send_sem=send_sems.at[a, k], recv_sem=recv_sems.at[a, k], device_id=to, device_id_type=MESH_ID)

        mine = [pltpu.make_async_copy(ins[a], outs[a].at[me], local_sems.at[a]) for a in range(n)]
        for cp in mine:
            cp.start()
        sends = []
        for a in range(n):
            sends.append(copy(a, 0, me, sibling, src=ins[a]))
            sends += [copy(a, 1 + j, me, (cx, cy, c), src=ins[a]) for j, (cx, cy) in enumerate(chips)]
        for cp in sends:
            cp.start()
        for j, (cx, cy) in enumerate(chips):
            blk = 4 * cx + 2 * cy + c
            for a in range(n):
                copy(a, 1 + j, blk, sibling).wait_recv()
                fwd = copy(a, 4 + j, blk, sibling)
                fwd.start()
                sends.append(fwd)
        for a in range(n):
            copy(a, 0, 4 * x + 2 * y + 1 - c, sibling).wait_recv()
            for j, (cx, cy) in enumerate(chips):
                copy(a, 4 + j, 4 * cx + 2 * cy + 1 - c, sibling).wait_recv()
        for cp in sends:
            cp.wait_send()
        for cp in mine:
            cp.wait()

    return pl.pallas_call(
        body, name=name, in_specs=[ANY] * n, out_specs=[ANY] * n,
        out_shape=[jax.ShapeDtypeStruct((N_DEV,) + a.shape, a.dtype) for a in arrs],
        scratch_shapes=[pltpu.SemaphoreType.DMA((n, 7)), pltpu.SemaphoreType.DMA((n, 7)), pltpu.SemaphoreType.DMA((n,))],
        compiler_params=pltpu.CompilerParams(has_side_effects=True),
    )(*arrs)


def _swap_sibling(arrs, name):
    n = len(arrs)

    def body(*refs):
        ins, outs = refs[:n], refs[n:2 * n]
        send_sems, recv_sems = refs[2 * n:]
        x, y, c, _ = _place()
        cps = [pltpu.make_async_remote_copy(src_ref=ins[a], dst_ref=outs[a], send_sem=send_sems.at[a], recv_sem=recv_sems.at[a],
                                            device_id=(x, y, 1 - c), device_id_type=MESH_ID) for a in range(n)]
        for cp in cps:
            cp.start()
        for cp in cps:
            cp.wait_recv()
        for cp in cps:
            cp.wait_send()

    return pl.pallas_call(
        body, name=name, in_specs=[ANY] * n, out_specs=[ANY] * n, out_shape=[jax.ShapeDtypeStruct(a.shape, a.dtype) for a in arrs],
        scratch_shapes=[pltpu.SemaphoreType.DMA((n,)), pltpu.SemaphoreType.DMA((n,))],
        compiler_params=pltpu.CompilerParams(has_side_effects=True),
    )(*arrs)


def _exchange_chips(arrs, name):
    n = len(arrs)

    def body(*refs):
        ins, outs = refs[:n], refs[n:2 * n]
        send_sems, recv_sems = refs[2 * n:]
        x, y, c, chips = _place()
        cps = []
        for a in range(n):
            for j, (cx, cy) in enumerate(chips):
                cps.append(pltpu.make_async_remote_copy(
                    src_ref=ins[a].at[2 * cx + cy], dst_ref=outs[a].at[j], send_sem=send_sems.at[a, j], recv_sem=recv_sems.at[a, j],
                    device_id=(cx, cy, c), device_id_type=MESH_ID))
        for cp in cps:
            cp.start()
        for cp in cps:
            cp.wait_recv()
        for cp in cps:
            cp.wait_send()

    return pl.pallas_call(
        body, name=name, in_specs=[ANY] * n, out_specs=[ANY] * n,
        out_shape=[jax.ShapeDtypeStruct((3,) + a.shape[1:], a.dtype) for a in arrs],
        scratch_shapes=[pltpu.SemaphoreType.DMA((n, 3)), pltpu.SemaphoreType.DMA((n, 3))],
        compiler_params=pltpu.CompilerParams(has_side_effects=True),
    )(*arrs)


SMALL = ("rel_bias", "norm1_g", "sgu_w", "sgu_b", "dil_qn_g", "dil_kn_g", "conv_w", "conv_b", "conv_ln_g", "conv_ln_b",
         "gqa_qn_g", "gqa_kn_g", "mix_norm_g", "norm2_g")
LARGE = ("w_in", "w_out", "w_gate", "w_up", "w_down")


def _local_step(x, target, p, B, S):
    rope = _rope_tables(S)
    win = _bias_windows(p["rel_bias"], S)
    tile8 = lambda g: jnp.tile(g.reshape(1, HEAD_DIM), (1, N_HEADS))
    cols_b = (COL_BQ, COL_BK, COL_BV)
    cols_d = (COL_DQ, COL_DK128, COL_DV128)
    saved = []
    for l in range(DEPTH):
        s = {"x": x}
        s["ws"] = p["sgu_w"][l].astype(BF16)
        s["bias"] = jnp.repeat(p["sgu_b"][l].T, HEAD_DIM, axis=1)
        s["cw"] = jnp.pad(p["conv_w"][l], ((0, 1), (0, 0))).reshape(32, 1, 512)
        s["h"] = _rms_fwd(x, p["norm1_g"][l], f"rms1_fwd_{l}")
        z = s["z"] = _matmul(s["h"], p["w_in"][l], "nn", f"in_proj_{l}")
        s["ya"] = _sgu_fwd(z, s["ws"], s["bias"], f"sgu_fwd_{l}")
        s["c"] = _conv_fwd1(z, s["cw"], p["conv_b"][l].reshape(1, 512), B, S, f"conv_fwd_{l}")
        s["yc"] = _conv_fwd2(s["c"], p["conv_ln_g"][l].reshape(1, 512), p["conv_ln_b"][l].reshape(1, 512), f"conv_ln_fwd_{l}")
        s["gb"] = (tile8(p["dil_qn_g"][l]), tile8(p["dil_kn_g"][l]))
        s["gd"] = (tile8(p["gqa_qn_g"][l]), tile8(p["gqa_kn_g"][l])[:, :KV_WIDTH])
        s["qkv_b"] = _prep_fwd(z, *s["gb"], None, B, S, N_HEADS, cols_b, f"prep_b_fwd_{l}")
        s["qkv_d"] = _prep_fwd(z, *s["gd"], rope, B, S, KV_HEADS, cols_d, f"prep_d_fwd_{l}")
        s["ob"] = _attn_fwd(*s["qkv_b"], win, f"attn_b_fwd_{l}")
        s["od"] = _attn_fwd(*s["qkv_d"], None, f"attn_d_fwd_{l}")
        s["gmix"] = p["mix_norm_g"][l].reshape(1, 2048)
        s["ycat"] = _mix_fwd(s["ya"], s["ob"], s["yc"], s["od"], s["gmix"], B, S, f"mix_fwd_{l}")
        x1 = s["x1"] = _matmul(s["ycat"], p["w_out"][l], "nn", f"out_proj_{l}", res=x)
        s["h2"] = _rms_fwd(x1, p["norm2_g"][l], f"rms2_fwd_{l}")
        s["gate"] = _matmul(s["h2"], p["w_gate"][l], "nn", f"ffn_gate_{l}")
        s["up"] = _matmul(s["h2"], p["w_up"][l], "nn", f"ffn_up_{l}")
        s["act"] = _swiglu_fwd(s["gate"], s["up"], f"swiglu_fwd_{l}")
        x = _matmul(s["act"], p["w_down"][l], "nn", f"ffn_down_{l}", res=x1)
        saved.append(s)

    loss_blk, dx = _loss_grad(x, target, "loss")
    g = {k: [None] * DEPTH for k in SMALL + LARGE if k != "rel_bias"}
    dwin_total = None
    for l in reversed(range(DEPTH)):
        s = saved[l]
        z = s["z"]
        dact = _matmul(dx, p["w_down"][l], "nt", f"ffn_down_dx_{l}")
        g["w_down"][l] = _matmul(s["act"], dx, "tn", f"ffn_down_dw_{l}")
        dgate, dup = _swiglu_bwd(s["gate"], s["up"], dact, f"swiglu_bwd_{l}")
        g["w_gate"][l] = _matmul(s["h2"], dgate, "tn", f"ffn_gate_dw_{l}")
        g["w_up"][l] = _matmul(s["h2"], dup, "tn", f"ffn_up_dw_{l}")
        dh2 = _matmul(dgate, p["w_gate"][l], "nt", f"ffn_gate_dx_{l}")
        dh2 = _matmul(dup, p["w_up"][l], "nt", f"ffn_up_dx_{l}", res=dh2)
        dx1, dg2 = _rms_bwd(dh2, s["x1"], p["norm2_g"][l], dx, f"rms2_bwd_{l}")
        g["norm2_g"][l] = dg2[0]
        dycat = _matmul(dx1, p["w_out"][l], "nt", f"out_proj_dx_{l}")
        g["w_out"][l] = _matmul(s["ycat"], dx1, "tn", f"out_proj_dw_{l}")
        dya, dob, dyc, dod, dgm = _mix_bwd(s["ya"], s["ob"], s["yc"], s["od"], dycat, s["gmix"], B, S, f"mix_bwd_{l}")
        g["mix_norm_g"][l] = dgm[0]
        dz_a, dws, dbias = _sgu_bwd(z, dya, s["ws"], jnp.swapaxes(s["ws"], 1, 2), s["bias"], f"sgu_bwd_{l}")
        g["sgu_w"][l] = dws
        g["sgu_b"][l] = dbias.reshape(128, 8, HEAD_DIM).sum(-1).T
        dc, dlg, dlb, dcb = _conv_bwd1(s["c"], dyc, p["conv_ln_g"][l].reshape(1, 512), p["conv_ln_b"][l].reshape(1, 512), f"conv_ln_bwd_{l}")
        g["conv_ln_g"][l], g["conv_ln_b"][l], g["conv_b"][l] = dlg[0], dlb[0], dcb[0]
        dz_ca, dz_cg, dcw = _conv_bwd2(z, dc, s["cw"], B, S, f"conv_bwd_{l}")
        g["conv_w"][l] = dcw.reshape(32, 512)[:CONV_WIDTH]
        dq, dk, dv, dwin = _attn_bwd(*s["qkv_b"], s["ob"], dob, win, f"attn_b_bwd_{l}")
        dwin_total = dwin if dwin_total is None else dwin_total + dwin
        dz_b, dgq, dgk = _prep_bwd(z, dq, dk, dv, *s["gb"], None, B, S, N_HEADS, cols_b, f"prep_b_bwd_{l}")
        g["dil_qn_g"][l] = dgq.reshape(N_HEADS, HEAD_DIM).sum(0)
        g["dil_kn_g"][l] = dgk.reshape(N_HEADS, HEAD_DIM).sum(0)
        dq, dk, dv = _attn_bwd(*s["qkv_d"], s["od"], dod, None, f"attn_d_bwd_{l}")
        dz_d, dgq, dgk = _prep_bwd(z, dq, dk, dv, *s["gd"], rope, B, S, KV_HEADS, cols_d, f"prep_d_bwd_{l}")
        g["gqa_qn_g"][l] = dgq.reshape(N_HEADS, HEAD_DIM).sum(0)
        g["gqa_kn_g"][l] = dgk.reshape(KV_HEADS, HEAD_DIM).sum(0)
        dz = jnp.concatenate([dz_a, dz_b, dz_ca, dz_cg, dz_d], axis=1)
        g["w_in"][l] = _matmul(s["h"], dz, "tn", f"in_proj_dw_{l}")
        dh = _matmul(dz, p["w_in"][l], "nt", f"in_proj_dx_{l}")
        dx, dg1 = _rms_bwd(dh, s["x"], p["norm1_g"][l], dx1, f"rms1_bwd_{l}")
        g["norm1_g"][l] = dg1[0]

    grads = {k: jnp.stack(v) for k, v in g.items() if k in SMALL}
    grads["rel_bias"] = _bias_fold(dwin_total, S, "bias_fold")
    for k in LARGE:
        grads[k] = g[k]
    return loss_blk[0, 0], dx, grads


def _pack(parts):
    flat = jnp.concatenate([a.reshape(-1) for a in parts])
    rows = -(-flat.shape[0] // 1024) * 8
    return jnp.pad(flat, (0, rows * 128 - flat.shape[0])).reshape(rows, 128)


def _unpack(buf, shapes):
    flat, out, off = buf.reshape(-1), [], 0
    for shp in shapes:
        n = int(np.prod(shp))
        out.append(flat[off:off + n].reshape(shp))
        off += n
    return out


def kernel(x, rel_bias, norm1_g, w_in, sgu_w, sgu_b, dil_qn_g, dil_kn_g, conv_w, conv_b, conv_ln_g, conv_ln_b, gqa_qn_g, gqa_kn_g, mix_norm_g, w_out, norm2_g, w_gate, w_up, w_down, loss_target, m_rel_bias, m_norm1_g, m_w_in, m_sgu_w, m_sgu_b, m_dil_qn_g, m_dil_kn_g, m_conv_w, m_conv_b, m_conv_ln_g, m_conv_ln_b, m_gqa_qn_g, m_gqa_kn_g, m_mix_norm_g, m_w_out, m_norm2_g, m_w_gate, m_w_up, m_w_down, v_rel_bias, v_norm1_g, v_w_in, v_sgu_w, v_sgu_b, v_dil_qn_g, v_dil_kn_g, v_conv_w, v_conv_b, v_conv_ln_g, v_conv_ln_b, v_gqa_qn_g, v_gqa_kn_g, v_mix_norm_g, v_w_out, v_norm2_g, v_w_gate, v_w_up, v_w_down):
    w = dict(rel_bias=rel_bias, norm1_g=norm1_g, w_in=w_in, sgu_w=sgu_w, sgu_b=sgu_b, dil_qn_g=dil_qn_g, dil_kn_g=dil_kn_g, conv_w=conv_w,
             conv_b=conv_b, conv_ln_g=conv_ln_g, conv_ln_b=conv_ln_b, gqa_qn_g=gqa_qn_g, gqa_kn_g=gqa_kn_g, mix_norm_g=mix_norm_g,
             w_out=w_out, norm2_g=norm2_g, w_gate=w_gate, w_up=w_up, w_down=w_down)
    m = dict(rel_bias=m_rel_bias, norm1_g=m_norm1_g, w_in=m_w_in, sgu_w=m_sgu_w, sgu_b=m_sgu_b, dil_qn_g=m_dil_qn_g, dil_kn_g=m_dil_kn_g,
             conv_w=m_conv_w, conv_b=m_conv_b, conv_ln_g=m_conv_ln_g, conv_ln_b=m_conv_ln_b, gqa_qn_g=m_gqa_qn_g, gqa_kn_g=m_gqa_kn_g,
             mix_norm_g=m_mix_norm_g, w_out=m_w_out, norm2_g=m_norm2_g, w_gate=m_w_gate, w_up=m_w_up, w_down=m_w_down)
    v = dict(rel_bias=v_rel_bias, norm1_g=v_norm1_g, w_in=v_w_in, sgu_w=v_sgu_w, sgu_b=v_sgu_b, dil_qn_g=v_dil_qn_g, dil_kn_g=v_dil_kn_g,
             conv_w=v_conv_w, conv_b=v_conv_b, conv_ln_g=v_conv_ln_g, conv_ln_b=v_conv_ln_b, gqa_qn_g=v_gqa_qn_g, gqa_kn_g=v_gqa_kn_g,
             mix_norm_g=v_mix_norm_g, w_out=v_w_out, norm2_g=v_norm2_g, w_gate=v_w_gate, w_up=v_w_up, w_down=v_w_down)
    names = list(w)
    B, S, D = x.shape
    T = B * S
    cx, cy, cc = lax.axis_index("x"), lax.axis_index("y"), lax.axis_index("c")
    me = 4 * cx + 2 * cy + cc
    chip = 2 * cx + cy
    col_sharded = ("w_in", "w_gate", "w_up")

    gathered = _all_gather([w[k].astype(BF16) for k in LARGE] + [conv_w], "gather_weights")
    full = dict(w)
    for k, ga in zip(LARGE, gathered):
        if k in col_sharded:
            full[k] = jnp.transpose(ga, (1, 2, 0, 3)).reshape(DEPTH, ga.shape[2], N_DEV * ga.shape[3])
        else:
            full[k] = jnp.transpose(ga, (1, 0, 2, 3)).reshape(DEPTH, N_DEV * ga.shape[2], ga.shape[3])
    full["conv_w"] = jnp.transpose(gathered[-1], (1, 2, 0, 3)).reshape(DEPTH, CONV_WIDTH, 512)

    loss_part, dx, grads = _local_step(x.reshape(T, D), loss_target.reshape(T, D), full, B, S)
    loss = lax.psum(loss_part, ("x", "y", "c"))

    def shard_major(k):
        gl = jnp.stack(grads[k])
        if k in col_sharded:
            gl = jnp.transpose(gl.reshape(DEPTH, gl.shape[1], N_DEV, gl.shape[2] // N_DEV), (2, 0, 1, 3))
        else:
            gl = jnp.transpose(gl.reshape(DEPTH, N_DEV, gl.shape[1] // N_DEV, gl.shape[2]), (1, 0, 2, 3))
        return gl.reshape((4, 2) + gl.shape[1:])

    sm = {k: shard_major(k) for k in LARGE}
    mine = {k: lax.dynamic_index_in_dim(sm[k], cc, 1, keepdims=False) for k in LARGE}
    theirs = [lax.dynamic_index_in_dim(sm[k], 1 - cc, 1, keepdims=False).astype(BF16) for k in LARGE]
    from_sibling = _swap_sibling(theirs, "reduce_siblings")
    chip_sums = [_add_cast(mine[k], r, f"chip_sum_{k}") for k, r in zip(LARGE, from_sibling)]
    from_chips = _exchange_chips(chip_sums, "reduce_chips")

    out_g, out_d, out_m, out_v = {}, {}, {}, {}
    for k, r1, r2 in zip(LARGE, from_sibling, from_chips):
        shp = w[k].shape
        two_d = lambda a: a.reshape(-1, shp[-1])
        parts = [two_d(lax.dynamic_index_in_dim(mine[k], chip, 0, keepdims=False)),
                 two_d(lax.dynamic_index_in_dim(r1, chip, 0, keepdims=False)), two_d(r2[0]), two_d(r2[1]), two_d(r2[2])]
        res = _adamw(two_d(w[k]), two_d(m[k]), two_d(v[k]), parts, f"adamw_{k}")
        out_g[k], out_d[k], out_m[k], out_v[k] = [a.reshape(shp) for a in res]

    small_shapes = [grads[k].shape for k in SMALL]
    stack = _all_gather([_pack([grads[k] for k in SMALL])], "gather_small_grads")[0]
    summed = dict(zip(SMALL, _unpack(_sum_rows(stack, "sum_small_grads"), small_shapes)))
    summed["conv_w"] = lax.dynamic_slice_in_dim(summed["conv_w"], me * (512 // N_DEV), 512 // N_DEV, axis=2)
    shapes = [w[k].shape for k in SMALL]
    res = _adamw(_pack([w[k] for k in SMALL]), _pack([m[k] for k in SMALL]), _pack([v[k] for k in SMALL]),
                 [_pack([summed[k] for k in SMALL])], "adamw_small")
    for dst, buf in zip((out_g, out_d, out_m, out_v), res):
        dst.update(zip(SMALL, _unpack(buf, shapes)))

    weights = [k for k in names]
    return (loss, dx.reshape(B, S, D), *[out_g[k] for k in weights], *[out_d[k] for k in weights],
            *[out_m[k] for k in weights], *[out_v[k] for k in weights])
```

```python
import functools
import math

import numpy as np
import jax
import jax.numpy as jnp
from jax import lax
from jax.experimental import pallas as pl
from jax.experimental.pallas import tpu as pltpu

F32 = jnp.float32
BF16 = jnp.bfloat16
HIGHEST = lax.Precision.HIGHEST
MESH_ID = pl.DeviceIdType.MESH

D_MODEL = 2048
DEPTH = 2
HEAD_DIM = 64
GROUP_WIDTH = 512
N_HEADS = 8
KV_HEADS = 2
KV_WIDTH = 128
SGU_CHUNK = 128
CONV_WIDTH = 31
CONV_PAD = 16
GRID_W = 64
ROPE_THETA = 10000.0
REL_BUCKETS = 32
REL_MAX_DIST = 1024
DIL_PATTERNS = ((128, 1), (512, 4), (2048, 16))
FFN_HIDDEN = 5632
IN_WIDTH = 4352
RMS_EPS = 1e-6
LN_EPS = 1e-5
MASKED = -1e30
N_DEV = 8

ADAM_LR = 0.001
ADAM_B1 = 0.9
ADAM_B2 = 0.999
ADAM_EPS = 1e-08
ADAM_WD = 0.01
ADAM_STEP = 10

COL_AU, COL_AV, COL_BQ, COL_BK, COL_BV, COL_CA, COL_CG, COL_DQ = range(8)
COL_DK128, COL_DV128 = 32, 33

VMEM_LIMIT = 56 * 1024 * 1024
ATTN_TQ = 256


def _params(sem=None, vmem=VMEM_LIMIT):
    return pltpu.CompilerParams(dimension_semantics=sem, vmem_limit_bytes=vmem)


def _dot(a, b, dims, precision=None):
    return lax.dot_general(a, b, (dims, ((), ())), precision=precision, preferred_element_type=F32)


def _nn(a, b, precision=None):
    return _dot(a, b, ((1,), (0,)), precision)


def _nt(a, b):
    return _dot(a, b, ((1,), (1,)))


def _tn(a, b):
    return _dot(a, b, ((0,), (0,)))


DIMS = {"nn": ((1,), (0,)), "nt": ((1,), (1,)), "tn": ((0,), (0,))}


def _pick(n, cands):
    for c in cands:
        if n % c == 0:
            return c
    return n


def _mm_call(name, mode, pairs, specs, o_spec, out_sds, grid, acc_shape, res=None):
    npair, nk, dims = len(pairs), grid[2], DIMS[mode]

    def body(*refs):
        ab = refs[:2 * npair]
        r_ref = refs[2 * npair] if res is not None else None
        o_ref = refs[2 * npair + (res is not None)]
        part = None
        for t in range(npair):
            d = _dot(ab[2 * t][...].astype(BF16), ab[2 * t + 1][...].astype(BF16), dims)
            part = d if part is None else part + d

        def finish(r):
            if r_ref is not None:
                r = r + r_ref[...]
            o_ref[...] = r.astype(o_ref.dtype)

        if nk == 1:
            finish(part)
            return
        acc, k = refs[-1], pl.program_id(2)

        @pl.when(k == 0)
        def _():
            acc[...] = part

        @pl.when(k > 0)
        def _():
            acc[...] += part

        @pl.when(k == nk - 1)
        def _():
            finish(acc[...])

    ins = [t for pair in pairs for t in pair]
    in_specs = [t for pair in specs for t in pair]
    if res is not None:
        ins.append(res)
        in_specs.append(o_spec)
    return pl.pallas_call(
        body, name=name, grid=grid, in_specs=in_specs, out_specs=o_spec, out_shape=out_sds,
        scratch_shapes=[pltpu.VMEM(acc_shape, F32)] if nk > 1 else [],
        compiler_params=_params(("parallel", "parallel", "arbitrary")),
    )(*ins)


def _matmul(a, b, mode, name, res=None, out_dtype=F32, tm=512, tn=None, tk=None):
    if mode == "nn":
        (M, K), N = a.shape, b.shape[1]
    elif mode == "nt":
        (M, K), N = a.shape, b.shape[0]
    else:
        (K, M), N = a.shape, b.shape[1]
    tm = min(tm, M)
    tn = tn or _pick(N, (2176, 2048, 1408, 1024, 512))
    tk = tk or _pick(K, (1024, 2176, 1408, 512))
    assert M % tm == 0 and N % tn == 0 and K % tk == 0, (M, N, K, tm, tn, tk)
    a_spec = pl.BlockSpec((tk, tm), lambda i, j, k: (k, i)) if mode == "tn" else pl.BlockSpec((tm, tk), lambda i, j, k: (i, k))
    b_spec = pl.BlockSpec((tn, tk), lambda i, j, k: (j, k)) if mode == "nt" else pl.BlockSpec((tk, tn), lambda i, j, k: (k, j))
    o_spec = pl.BlockSpec((tm, tn), lambda i, j, k: (i, j))
    return _mm_call(name, mode, [(a, b)], [(a_spec, b_spec)], o_spec, jax.ShapeDtypeStruct((M, N), out_dtype),
                    (M // tm, N // tn, K // tk), (tm, tn), res)


def _mm_shard_out(a, bs, mode, name, out_dtype=F32, tm=512, tk=None):
    J = bs.shape[0]
    n = bs.shape[1] if mode == "nt" else bs.shape[2]
    (K, M) = a.shape if mode == "tn" else a.shape[::-1]
    tm = min(tm, M)
    tk = tk or (K if mode != "tn" else _pick(K, (1024, 512)))
    a_spec = pl.BlockSpec((tk, tm), lambda j, i, k: (k, i)) if mode == "tn" else pl.BlockSpec((tm, tk), lambda j, i, k: (i, k))
    b_spec = pl.BlockSpec((None, n, tk), lambda j, i, k: (j, 0, k)) if mode == "nt" else pl.BlockSpec((None, tk, n), lambda j, i, k: (j, k, 0))
    o_spec = pl.BlockSpec((None, tm, n), lambda j, i, k: (j, i, 0))
    return _mm_call(name, mode, [(a, bs)], [(a_spec, b_spec)], o_spec, jax.ShapeDtypeStruct((J, M, n), out_dtype),
                    (J, M // tm, K // tk), (tm, n))


def _mm_shard_k(pairs, mode, name, res=None, out_dtype=F32, tm=512, tn=None):
    J, M, n = pairs[0][0].shape
    N = pairs[0][1].shape[2] if mode == "nn" else pairs[0][1].shape[1]
    tm = min(tm, M)
    tn = tn or _pick(N, (2048, 1024, 512))
    a_spec = pl.BlockSpec((None, tm, n), lambda i, j, k: (k, i, 0))
    b_spec = pl.BlockSpec((None, n, tn), lambda i, j, k: (k, 0, j)) if mode == "nn" else pl.BlockSpec((None, tn, n), lambda i, j, k: (k, j, 0))
    o_spec = pl.BlockSpec((tm, tn), lambda i, j, k: (i, j))
    return _mm_call(name, mode, pairs, [(a_spec, b_spec)] * len(pairs), o_spec, jax.ShapeDtypeStruct((M, N), out_dtype),
                    (M // tm, N // tn, J), (tm, tn), res)


def _mm_shard_m(as_, b, name, out_dtype=F32, tn=None, tk=512):
    J, K, n = as_.shape
    N = b.shape[1]
    tn = tn or _pick(N, (2048, 1024, 512))
    tk = min(tk, K)
    a_spec = pl.BlockSpec((None, tk, n), lambda j, i, k: (j, k, 0))
    b_spec = pl.BlockSpec((tk, tn), lambda j, i, k: (k, i))
    o_spec = pl.BlockSpec((None, n, tn), lambda j, i, k: (j, 0, i))
    return _mm_call(name, "tn", [(as_, b)], [(a_spec, b_spec)], o_spec, jax.ShapeDtypeStruct((J, n, N), out_dtype),
                    (J, N // tn, K // tk), (n, tn))


def _seg_matrix(width):
    return jnp.asarray(np.kron(np.eye(width // HEAD_DIM, dtype=np.float32), np.full((HEAD_DIM, HEAD_DIM), 1.0 / HEAD_DIM, np.float32)))


def _segmean(v, p):
    return _nn(v, p, HIGHEST)


def _gelu(x):
    c0 = math.sqrt(2.0 / math.pi)
    t = jnp.tanh(c0 * (x + 0.044715 * x * x * x))
    return 0.5 * x * (1.0 + t), t


def _gelu_grad(x, t):
    c0 = math.sqrt(2.0 / math.pi)
    return 0.5 * (1.0 + t) + 0.5 * x * (1.0 - t * t) * c0 * (1.0 + 3.0 * 0.044715 * x * x)


def _sigmoid(x):
    return 1.0 / (1.0 + jnp.exp(-x))


def _rms_fwd(x, g, name):
    T, D = x.shape
    tm = min(256, T)

    def body(x_ref, g_ref, o_ref):
        xv = x_ref[...]
        r = lax.rsqrt(jnp.mean(xv * xv, axis=-1, keepdims=True) + RMS_EPS)
        o_ref[...] = (xv * r * g_ref[...]).astype(BF16)

    return pl.pallas_call(
        body, name=name, grid=(T // tm,),
        in_specs=[pl.BlockSpec((tm, D), lambda i: (i, 0)), pl.BlockSpec((1, D), lambda i: (0, 0))],
        out_specs=pl.BlockSpec((tm, D), lambda i: (i, 0)), out_shape=jax.ShapeDtypeStruct((T, D), BF16),
        compiler_params=_params(("parallel",)),
    )(x, g.reshape(1, D))


def _rms_bwd(dh, x, g, dres, name):
    T, D = x.shape
    tm = min(256, T)

    def body(dh_ref, x_ref, g_ref, dres_ref, dx_ref, dg_ref):
        @pl.when(pl.program_id(0) == 0)
        def _():
            dg_ref[...] = jnp.zeros_like(dg_ref)

        xv, dhv = x_ref[...], dh_ref[...]
        r = lax.rsqrt(jnp.mean(xv * xv, axis=-1, keepdims=True) + RMS_EPS)
        y = xv * r
        dy = dhv * g_ref[...]
        dx_ref[...] = dres_ref[...] + r * (dy - y * jnp.mean(dy * y, axis=-1, keepdims=True))
        dg_ref[...] += jnp.sum(dhv * y, axis=0, keepdims=True)

    row = pl.BlockSpec((tm, D), lambda i: (i, 0))
    vec = pl.BlockSpec((1, D), lambda i: (0, 0))
    return pl.pallas_call(
        body, name=name, grid=(T // tm,), in_specs=[row, row, vec, row], out_specs=[row, vec],
        out_shape=[jax.ShapeDtypeStruct((T, D), F32), jax.ShapeDtypeStruct((1, D), F32)],
        compiler_params=_params(("arbitrary",)),
    )(dh, x, g.reshape(1, D), dres)


def _sgu_core(zu, zv, ws_ref, bias, p):
    ug, tu = _gelu(zu)
    vg, tv = _gelu(zv)
    xc = vg - _segmean(vg, p)
    rs = lax.rsqrt(_segmean(xc * xc, p) + LN_EPS)
    vn = xc * rs
    vnb = vn.astype(BF16)
    low = lax.broadcasted_iota(jnp.int32, (SGU_CHUNK, 128), 1) < HEAD_DIM
    parts = []
    for j in range(4):
        vp = vnb[:, 128 * j:128 * (j + 1)]
        parts.append(jnp.where(low, _nn(ws_ref[2 * j], vp), _nn(ws_ref[2 * j + 1], vp)))
    mixed = jnp.concatenate(parts, axis=1) + bias
    return ug, tu, tv, rs, vn, vnb, mixed, low


def _sgu_fwd(z, ws, bias, name):
    T = z.shape[0]

    def body(zu_ref, zv_ref, ws_ref, b_ref, p_ref, y_ref):
        ug, _, _, _, _, _, mixed, _ = _sgu_core(zu_ref[...], zv_ref[...], ws_ref, b_ref[...], p_ref[...])
        y_ref[...] = ug * mixed

    full = lambda shape: pl.BlockSpec(shape, lambda i: (0,) * len(shape))
    return pl.pallas_call(
        body, name=name, grid=(T // SGU_CHUNK,),
        in_specs=[pl.BlockSpec((SGU_CHUNK, 512), lambda i: (i, COL_AU)), pl.BlockSpec((SGU_CHUNK, 512), lambda i: (i, COL_AV)),
                  full((8, 128, 128)), full((128, 512)), full((512, 512))],
        out_specs=pl.BlockSpec((SGU_CHUNK, 512), lambda i: (i, 0)), out_shape=jax.ShapeDtypeStruct((T, 512), F32),
        compiler_params=_params(("parallel",)),
    )(z, z, ws, bias, _seg_matrix(512))


def _sgu_bwd(z, dy, ws, ws_t, bias, name):
    T = z.shape[0]

    def body(zu_ref, zv_ref, dy_ref, ws_ref, wst_ref, b_ref, p_ref, dz_ref, dws_ref, db_ref):
        @pl.when(pl.program_id(0) == 0)
        def _():
            dws_ref[...] = jnp.zeros_like(dws_ref)
            db_ref[...] = jnp.zeros_like(db_ref)

        zu, zv, p = zu_ref[...], zv_ref[...], p_ref[...]
        ug, tu, tv, rs, vn, vnb, mixed, low = _sgu_core(zu, zv, ws_ref, b_ref[...], p)
        dyv = dy_ref[...]
        dmixed = dyv * ug
        db_ref[...] += dmixed
        dmb = dmixed.astype(BF16)
        zero = jnp.zeros((SGU_CHUNK, 128), BF16)
        parts = []
        for j in range(4):
            dmp, vp = dmb[:, 128 * j:128 * (j + 1)], vnb[:, 128 * j:128 * (j + 1)]
            dws_ref[2 * j] += _nt(jnp.where(low, dmp, zero), vp)
            dws_ref[2 * j + 1] += _nt(jnp.where(low, zero, dmp), vp)
            parts.append(jnp.where(low, _nn(wst_ref[2 * j], dmp), _nn(wst_ref[2 * j + 1], dmp)))
        dvn = jnp.concatenate(parts, axis=1)
        dvg = rs * (dvn - _segmean(dvn, p) - vn * _segmean(dvn * vn, p))
        dz_ref[:, 0:512] = dyv * mixed * _gelu_grad(zu, tu)
        dz_ref[:, 512:1024] = dvg * _gelu_grad(zv, tv)

    full = lambda shape: pl.BlockSpec(shape, lambda i: (0,) * len(shape))
    return pl.pallas_call(
        body, name=name, grid=(T // SGU_CHUNK,),
        in_specs=[pl.BlockSpec((SGU_CHUNK, 512), lambda i: (i, COL_AU)), pl.BlockSpec((SGU_CHUNK, 512), lambda i: (i, COL_AV)),
                  pl.BlockSpec((SGU_CHUNK, 512), lambda i: (i, 0)), full((8, 128, 128)), full((8, 128, 128)), full((128, 512)), full((512, 512))],
        out_specs=[pl.BlockSpec((SGU_CHUNK, 1024), lambda i: (i, 0)), full((8, 128, 128)), full((128, 512))],
        out_shape=[jax.ShapeDtypeStruct((T, 1024), F32), jax.ShapeDtypeStruct((8, 128, 128), F32), jax.ShapeDtypeStruct((128, 512), F32)],
        compiler_params=_params(("arbitrary",)),
    )(z, z, dy, ws, ws_t, bias, _seg_matrix(512))


CONV_ROWS = 256


def _conv_taps(pad_ref, w_ref, base, flip):
    blk = pad_ref[pl.ds(base, CONV_ROWS + 2 * CONV_PAD), :]
    acc = jnp.zeros((CONV_ROWS, blk.shape[1]), F32)
    for k in range(CONV_WIDTH):
        wk = w_ref[CONV_WIDTH - 1 - k if flip else k]
        acc = acc + wk * blk[k + 1:k + 1 + CONV_ROWS, :]
    return acc


def _conv_fwd1(z, w, cb, B, S, name):
    T = B * S
    rows = min(CONV_ROWS, S)
    assert rows == CONV_ROWS

    def body(a_ref, g_ref, w_ref, cb_ref, c_ref, pad):
        pad[0:CONV_PAD, :] = jnp.zeros((CONV_PAD, 128), F32)
        pad[CONV_PAD + S:2 * CONV_PAD + S, :] = jnp.zeros((CONV_PAD, 128), F32)
        pad[CONV_PAD:CONV_PAD + S, :] = a_ref[...] * _sigmoid(g_ref[...])

        def tile(r, carry):
            base = pl.multiple_of(r * CONV_ROWS, CONV_ROWS)
            c_ref[pl.ds(base, CONV_ROWS), :] = _conv_taps(pad, w_ref, base, False) + cb_ref[...]
            return carry

        lax.fori_loop(0, S // CONV_ROWS, tile, 0)

    return pl.pallas_call(
        body, name=name, grid=(4, B),
        in_specs=[pl.BlockSpec((S, 128), lambda j, b: (b, 4 * COL_CA + j)), pl.BlockSpec((S, 128), lambda j, b: (b, 4 * COL_CG + j)),
                  pl.BlockSpec((32, 1, 128), lambda j, b: (0, 0, j)), pl.BlockSpec((1, 128), lambda j, b: (0, j))],
        out_specs=pl.BlockSpec((S, 128), lambda j, b: (b, j)), out_shape=jax.ShapeDtypeStruct((T, 512), F32),
        scratch_shapes=[pltpu.VMEM((S + 2 * CONV_PAD, 128), F32)], compiler_params=_params(("parallel", "parallel")),
    )(z, z, w, cb)


def _ln_rows(c):
    mu = jnp.mean(c, axis=-1, keepdims=True)
    xc = c - mu
    rs = lax.rsqrt(jnp.mean(xc * xc, axis=-1, keepdims=True) + LN_EPS)
    return xc * rs, rs


def _conv_fwd2(c, lng, lnb, name):
    T = c.shape[0]
    tm = min(512, T)

    def body(c_ref, g_ref, b_ref, y_ref):
        n, _ = _ln_rows(c_ref[...])
        t = n * g_ref[...] + b_ref[...]
        y_ref[...] = t * _sigmoid(t)

    row = pl.BlockSpec((tm, 512), lambda i: (i, 0))
    vec = pl.BlockSpec((1, 512), lambda i: (0, 0))
    return pl.pallas_call(body, name=name, grid=(T // tm,), in_specs=[row, vec, vec], out_specs=row,
                          out_shape=jax.ShapeDtypeStruct((T, 512), F32), compiler_params=_params(("parallel",)))(c, lng, lnb)


def _conv_bwd1(c, dy, lng, lnb, name):
    T = c.shape[0]
    tm = min(512, T)

    def body(c_ref, dy_ref, g_ref, b_ref, dc_ref, dg_ref, db_ref, dcb_ref):
        @pl.when(pl.program_id(0) == 0)
        def _():
            dg_ref[...] = jnp.zeros_like(dg_ref)
            db_ref[...] = jnp.zeros_like(db_ref)
            dcb_ref[...] = jnp.zeros_like(dcb_ref)

        n, rs = _ln_rows(c_ref[...])
        t = n * g_ref[...] + b_ref[...]
        s = _sigmoid(t)
        dt = dy_ref[...] * s * (1.0 + t * (1.0 - s))
        dg_ref[...] += jnp.sum(dt * n, axis=0, keepdims=True)
        db_ref[...] += jnp.sum(dt, axis=0, keepdims=True)
        dn = dt * g_ref[...]
        dc = rs * (dn - jnp.mean(dn, axis=-1, keepdims=True) - n * jnp.mean(dn * n, axis=-1, keepdims=True))
        dc_ref[...] = dc
        dcb_ref[...] += jnp.sum(dc, axis=0, keepdims=True)

    row = pl.BlockSpec((tm, 512), lambda i: (i, 0))
    vec = pl.BlockSpec((1, 512), lambda i: (0, 0))
    vshape = jax.ShapeDtypeStruct((1, 512), F32)
    return pl.pallas_call(body, name=name, grid=(T // tm,), in_specs=[row, row, vec, vec], out_specs=[row, vec, vec, vec],
                          out_shape=[jax.ShapeDtypeStruct((T, 512), F32), vshape, vshape, vshape],
                          compiler_params=_params(("arbitrary",)))(c, dy, lng, lnb)


def _conv_bwd2(z, dc, w, B, S, name):
    T = B * S

    def body(a_ref, g_ref, dc_ref, w_ref, da_ref, dg_ref, dw_ref, hpad, dpad, dwacc):
        @pl.when(pl.program_id(1) == 0)
        def _():
            dw_ref[...] = jnp.zeros_like(dw_ref)

        zeros = jnp.zeros((CONV_PAD, 128), F32)
        for ref in (hpad, dpad):
            ref[0:CONV_PAD, :] = zeros
            ref[CONV_PAD + S:2 * CONV_PAD + S, :] = zeros
        hpad[CONV_PAD:CONV_PAD + S, :] = a_ref[...] * _sigmoid(g_ref[...])
        dpad[CONV_PAD:CONV_PAD + S, :] = dc_ref[...]
        dwacc[...] = jnp.zeros_like(dwacc)

        def tile(r, carry):
            base = pl.multiple_of(r * CONV_ROWS, CONV_ROWS)
            dh = _conv_taps(dpad, w_ref, base, True)
            av, gv = a_ref[pl.ds(base, CONV_ROWS), :], g_ref[pl.ds(base, CONV_ROWS), :]
            sg = _sigmoid(gv)
            da_ref[pl.ds(base, CONV_ROWS), :] = dh * sg
            dg_ref[pl.ds(base, CONV_ROWS), :] = dh * av * sg * (1.0 - sg)
            dcv = dc_ref[pl.ds(base, CONV_ROWS), :]
            blk = hpad[pl.ds(base, CONV_ROWS + 2 * CONV_PAD), :]
            for k in range(CONV_WIDTH):
                prod = dcv * blk[k + 1:k + 1 + CONV_ROWS, :]
                dwacc[k] += jnp.sum(prod.reshape(CONV_ROWS // 8, 8, 128), axis=0)
            return carry

        lax.fori_loop(0, S // CONV_ROWS, tile, 0)
        for k in range(CONV_WIDTH):
            dw_ref[k] += jnp.sum(dwacc[k], axis=0, keepdims=True)

    return pl.pallas_call(
        body, name=name, grid=(4, B),
        in_specs=[pl.BlockSpec((S, 128), lambda j, b: (b, 4 * COL_CA + j)), pl.BlockSpec((S, 128), lambda j, b: (b, 4 * COL_CG + j)),
                  pl.BlockSpec((S, 128), lambda j, b: (b, j)), pl.BlockSpec((32, 1, 128), lambda j, b: (0, 0, j))],
        out_specs=[pl.BlockSpec((S, 128), lambda j, b: (b, j)), pl.BlockSpec((S, 128), lambda j, b: (b, j)),
                   pl.BlockSpec((32, 1, 128), lambda j, b: (0, 0, j))],
        out_shape=[jax.ShapeDtypeStruct((T, 512), F32), jax.ShapeDtypeStruct((T, 512), F32), jax.ShapeDtypeStruct((32, 1, 512), F32)],
        scratch_shapes=[pltpu.VMEM((S + 2 * CONV_PAD, 128), F32), pltpu.VMEM((S + 2 * CONV_PAD, 128), F32), pltpu.VMEM((32, 8, 128), F32)],
        compiler_params=_params(("parallel", "arbitrary")),
    )(z, z, dc, w)


def _swap16(x):
    n = x.shape[1]
    first = (lax.broadcasted_iota(jnp.int32, x.shape, 1) % 32) < 16
    return jnp.where(first, pltpu.roll(x, n - 16, 1), pltpu.roll(x, 16, 1))


def _rope(x, cos, sin):
    return x * cos + _swap16(x) * sin


def _rope_t(dy, cos, sin):
    return dy * cos + _swap16(dy * sin)


def _qk_norm(x, p):
    r = lax.rsqrt(_segmean(x * x, p) + RMS_EPS)
    return x * r, r


def _store_heads(ref, val, n):
    for h in range(n):
        ref[h] = val[:, HEAD_DIM * h:HEAD_DIM * (h + 1)].astype(ref.dtype)


def _load_heads(ref, n):
    return jnp.concatenate([ref[h] for h in range(n)], axis=1)


def _prep_fwd(z, gq, gk, rope, B, S, kv_heads, cols, name):
    tm = min(256, S)
    ns = S // tm
    kw = kv_heads * HEAD_DIM
    scale = HEAD_DIM ** -0.5
    qc, kc, vc = cols

    def body(*refs):
        if rope is None:
            q_ref, k_ref, v_ref, gq_ref, gk_ref, p_ref, qo, ko, vo = refs
        else:
            q_ref, k_ref, v_ref, gq_ref, gk_ref, p_ref, cos_ref, sin_ref, qo, ko, vo = refs
        p = p_ref[...]
        qn, _ = _qk_norm(q_ref[...], p)
        kn, _ = _qk_norm(k_ref[...], p[:kw, :kw])
        qn, kn = qn * gq_ref[...], kn * gk_ref[...]
        if rope is not None:
            cos, sin = cos_ref[...], sin_ref[...]
            qn, kn = _rope(qn, cos, sin), _rope(kn, cos[:, :kw], sin[:, :kw])
        _store_heads(qo, qn * scale, N_HEADS)
        _store_heads(ko, kn, kv_heads)
        _store_heads(vo, v_ref[...], kv_heads)

    row = lambda w, c: pl.BlockSpec((tm, w), lambda b, i: (b * ns + i, c))
    const = lambda shape: pl.BlockSpec(shape, lambda b, i: (0,) * len(shape))
    heads = lambda n: pl.BlockSpec((None, n, tm, HEAD_DIM), lambda b, i: (b, 0, i, 0))
    ins = [z, z, z, gq, gk, _seg_matrix(512)]
    specs = [row(512, qc), row(kw, kc), row(kw, vc), const((1, 512)), const((1, kw)), const((512, 512))]
    if rope is not None:
        ins += list(rope)
        specs += [pl.BlockSpec((tm, 512), lambda b, i: (i, 0))] * 2
    return pl.pallas_call(
        body, name=name, grid=(B, ns), in_specs=specs, out_specs=[heads(N_HEADS), heads(kv_heads), heads(kv_heads)],
        out_shape=[jax.ShapeDtypeStruct((B, N_HEADS, S, HEAD_DIM), BF16), jax.ShapeDtypeStruct((B, kv_heads, S, HEAD_DIM), BF16),
                   jax.ShapeDtypeStruct((B, kv_heads, S, HEAD_DIM), BF16)],
        compiler_params=_params(("parallel", "parallel")),
    )(*ins)


def _prep_bwd(z, dq, dk, dv, gq, gk, rope, B, S, kv_heads, cols, name):
    T = B * S
    tm = min(256, S)
    ns = S // tm
    kw = kv_heads * HEAD_DIM
    scale = HEAD_DIM ** -0.5
    qc, kc, _ = cols

    def body(*refs):
        if rope is None:
            q_ref, k_ref, dq_ref, dk_ref, dv_ref, gq_ref, gk_ref, p_ref, dz_ref, dgq_ref, dgk_ref = refs
        else:
            q_ref, k_ref, dq_ref, dk_ref, dv_ref, gq_ref, gk_ref, p_ref, cos_ref, sin_ref, dz_ref, dgq_ref, dgk_ref = refs

        @pl.when((pl.program_id(0) == 0) & (pl.program_id(1) == 0))
        def _():
            dgq_ref[...] = jnp.zeros_like(dgq_ref)
            dgk_ref[...] = jnp.zeros_like(dgk_ref)

        p = p_ref[...]
        dqv = _load_heads(dq_ref, N_HEADS) * scale
        dkv = _load_heads(dk_ref, kv_heads)
        if rope is not None:
            cos, sin = cos_ref[...], sin_ref[...]
            dqv, dkv = _rope_t(dqv, cos, sin), _rope_t(dkv, cos[:, :kw], sin[:, :kw])

        def through_norm(xv, dy, g, pm, dg_ref):
            xh, r = _qk_norm(xv, pm)
            dg_ref[...] += jnp.sum(dy * xh, axis=0, keepdims=True)
            dxh = dy * g
            return r * (dxh - xh * _segmean(dxh * xh, pm))

        dz_ref[:, 0:512] = through_norm(q_ref[...], dqv, gq_ref[...], p, dgq_ref)
        dz_ref[:, 512:512 + kw] = through_norm(k_ref[...], dkv, gk_ref[...], p[:kw, :kw], dgk_ref)
        dz_ref[:, 512 + kw:512 + 2 * kw] = _load_heads(dv_ref, kv_heads)

    row = lambda w, c: pl.BlockSpec((tm, w), lambda b, i: (b * ns + i, c))
    const = lambda shape: pl.BlockSpec(shape, lambda b, i: (0,) * len(shape))
    heads = lambda n: pl.BlockSpec((None, n, tm, HEAD_DIM), lambda b, i: (b, 0, i, 0))
    ins = [z, z, dq, dk, dv, gq, gk, _seg_matrix(512)]
    specs = [row(512, qc), row(kw, kc), heads(N_HEADS), heads(kv_heads), heads(kv_heads), const((1, 512)), const((1, kw)), const((512, 512))]
    if rope is not None:
        ins += list(rope)
        specs += [pl.BlockSpec((tm, 512), lambda b, i: (i, 0))] * 2
    return pl.pallas_call(
        body, name=name, grid=(B, ns), in_specs=specs, out_specs=[row(512 + 2 * kw, 0), const((1, 512)), const((1, kw))],
        out_shape=[jax.ShapeDtypeStruct((T, 512 + 2 * kw), F32), jax.ShapeDtypeStruct((1, 512), F32), jax.ShapeDtypeStruct((1, kw), F32)],
        compiler_params=_params(("arbitrary", "arbitrary")),
    )(*ins)


def _toeplitz(win, tq, S):
    r = pltpu.roll(jnp.broadcast_to(win, (tq, S + tq)), 0, 1, stride=1, stride_axis=0)
    return r[:, tq:tq + S]


def _attn_fwd(q, k, v, win, name):
    B, H, S, _ = q.shape
    group = H // k.shape[1]
    tq = min(ATTN_TQ, S)

    def body(*refs):
        if win is None:
            q_ref, k_ref, v_ref, o_ref = refs
        else:
            q_ref, k_ref, v_ref, w_ref, o_ref = refs
        s = _nt(q_ref[...], k_ref[...])
        if win is not None:
            s = s + _toeplitz(w_ref[...], tq, S)
        p = jnp.exp(s - jnp.max(s, axis=-1, keepdims=True))
        l = jnp.sum(p, axis=-1, keepdims=True)
        o_ref[...] = _nn(p.astype(BF16), v_ref[...]) / l

    qs = pl.BlockSpec((None, None, tq, HEAD_DIM), lambda b, h, i: (b, h, i, 0))
    ks = pl.BlockSpec((None, None, S, HEAD_DIM), lambda b, h, i: (b, h // group, 0, 0))
    ins, specs = [q, k, v], [qs, ks, ks]
    if win is not None:
        ins.append(win)
        specs.append(pl.BlockSpec((None, None, 1, S + tq), lambda b, h, i: (h, i, 0, 0)))
    return pl.pallas_call(body, name=name, grid=(B, H, S // tq), in_specs=specs, out_specs=qs,
                          out_shape=jax.ShapeDtypeStruct((B, H, S, HEAD_DIM), F32),
                          compiler_params=_params(("parallel", "parallel", "parallel")))(*ins)


def _attn_bwd(q, k, v, o, do, win, name):
    B, H, S, _ = q.shape
    hkv = k.shape[1]
    group = H // hkv
    tq = min(ATTN_TQ, S)
    nq = S // tq

    def body(*refs):
        if win is None:
            q_ref, k_ref, v_ref, o_ref, do_ref, dq_ref, dk_ref, dv_ref = refs
        else:
            q_ref, k_ref, v_ref, o_ref, do_ref, w_ref, rev_ref, dq_ref, dk_ref, dv_ref, dw_ref = refs

        @pl.when((pl.program_id(2) == 0) & (pl.program_id(3) == 0))
        def _():
            dk_ref[...] = jnp.zeros_like(dk_ref)
            dv_ref[...] = jnp.zeros_like(dv_ref)

        qv, kv, vv = q_ref[...], k_ref[...], v_ref[...]
        s = _nt(qv, kv)
        if win is not None:
            s = s + _toeplitz(w_ref[...], tq, S)
        p = jnp.exp(s - jnp.max(s, axis=-1, keepdims=True))
        p = p / jnp.sum(p, axis=-1, keepdims=True)
        dov = do_ref[...]
        dob = dov.astype(BF16)
        dp = _nt(dob, vv)
        ds = p * (dp - jnp.sum(dov * o_ref[...], axis=-1, keepdims=True))
        dsb = ds.astype(BF16)
        dv_ref[...] += _tn(p.astype(BF16), dob)
        dk_ref[...] += _tn(dsb, qv)
        dq_ref[...] = _nn(dsb, kv)
        if win is not None:
            rev = _nn(rev_ref[...], dsb)
            wide = jnp.concatenate([rev, jnp.zeros((tq, tq), F32)], axis=1)
            dw_ref[...] = jnp.sum(pltpu.roll(wide, 0, 1, stride=1, stride_axis=0), axis=0, keepdims=True)

    qs = pl.BlockSpec((None, None, tq, HEAD_DIM), lambda b, h, g, i: (b, h * group + g, i, 0))
    ks = pl.BlockSpec((None, None, S, HEAD_DIM), lambda b, h, g, i: (b, h, 0, 0))
    ins, specs = [q, k, v, o, do], [qs, ks, ks, qs, qs]
    outs = [jax.ShapeDtypeStruct((B, H, S, HEAD_DIM), F32), jax.ShapeDtypeStruct((B, hkv, S, HEAD_DIM), F32), jax.ShapeDtypeStruct((B, hkv, S, HEAD_DIM), F32)]
    ospecs = [qs, ks, ks]
    if win is not None:
        ins += [win, jnp.asarray(np.eye(tq, dtype=np.float32)[::-1].copy(), BF16)]
        specs += [pl.BlockSpec((None, None, 1, S + tq), lambda b, h, g, i: (h * group + g, i, 0, 0)), pl.BlockSpec((tq, tq), lambda b, h, g, i: (0, 0))]
        outs.append(jax.ShapeDtypeStruct((B, H, nq, 1, S + tq), F32))
        ospecs.append(pl.BlockSpec((None, None, None, 1, S + tq), lambda b, h, g, i: (b, h * group + g, i, 0, 0)))
    return pl.pallas_call(body, name=name, grid=(B, hkv, group, nq), in_specs=specs, out_specs=ospecs, out_shape=outs,
                          compiler_params=_params(("parallel", "parallel", "arbitrary", "arbitrary")))(*ins)


def _pattern_count(delta):
    n = jnp.zeros(delta.shape, jnp.int32)
    for window, dil in DIL_PATTERNS:
        n = n + ((delta % dil == 0) & (jnp.abs(delta) <= window // 2)).astype(jnp.int32)
    return n


def _t5_bucket(rel):
    nb = REL_BUCKETS // 2
    max_exact = nb // 2
    ret = jnp.where(rel > 0, nb, 0)
    n = jnp.abs(rel)
    nf = jnp.maximum(n, 1).astype(F32)
    large = max_exact + (jnp.log(nf / max_exact) / math.log(REL_MAX_DIST / max_exact) * (nb - max_exact)).astype(jnp.int32)
    large = jnp.minimum(large, nb - 1)
    return ret + jnp.where(n < max_exact, n, large)


def _bias_windows(rel_bias, S):
    tq = min(ATTN_TQ, S)
    nq = S // tq
    n = nq * (S + tq)
    delta = (jnp.arange(S + tq)[None, :] - (jnp.arange(nq)[:, None] + 1) * tq).reshape(n)
    count = _pattern_count(delta)
    onehot = (_t5_bucket(delta)[None, :] == jnp.arange(REL_BUCKETS)[:, None]).astype(F32)
    extra = jnp.where(count > 0, jnp.log(jnp.maximum(count, 1).astype(F32)), MASKED).reshape(1, n)
    live = (count > 0).astype(F32).reshape(1, n)

    def body(t_ref, oh_ref, live_ref, extra_ref, o_ref):
        o_ref[...] = _nn(t_ref[...], oh_ref[...], HIGHEST) * live_ref[...] + extra_ref[...]

    val = pl.pallas_call(body, name="bias_windows", out_shape=jax.ShapeDtypeStruct((N_HEADS, n), F32),
                         compiler_params=_params())(rel_bias.T, onehot, live, extra)
    return val.reshape(N_HEADS, nq, 1, S + tq)


def _bias_fold(dwin, S, name):
    B, H, nq = dwin.shape[:3]
    tq = min(ATTN_TQ, S)
    n = nq * (S + tq)
    delta = (jnp.arange(S + tq)[None, :] - (tq - 1) - jnp.arange(nq)[:, None] * tq).reshape(n)
    onehot = (_t5_bucket(delta)[:, None] == jnp.arange(128)[None, :]).astype(F32)

    def body(d_ref, oh_ref, o_ref):
        tot = d_ref[0]
        for b in range(1, B):
            tot = tot + d_ref[b]
        o_ref[...] = _nn(tot, oh_ref[...], HIGHEST)

    out = pl.pallas_call(body, name=name, out_shape=jax.ShapeDtypeStruct((H, 128), F32), compiler_params=_params())(dwin.reshape(B, H, n), onehot)
    return out[:, :REL_BUCKETS].T


def _rope_tables(S):
    half = 16
    freqs = ROPE_THETA ** (-jnp.arange(half, dtype=F32) / half)
    t = jnp.arange(S)
    ang_r = (t // GRID_W).astype(F32)[:, None] * freqs[None, :]
    ang_c = (t % GRID_W).astype(F32)[:, None] * freqs[None, :]
    cos = jnp.concatenate([jnp.cos(ang_r)] * 2 + [jnp.cos(ang_c)] * 2, axis=1)
    sin = jnp.concatenate([-jnp.sin(ang_r), jnp.sin(ang_r), -jnp.sin(ang_c), jnp.sin(ang_c)], axis=1)
    return jnp.tile(cos, (1, N_HEADS)), jnp.tile(sin, (1, N_HEADS))


def _mix_fwd(ya, ob, yc, od, gain, B, S, name):
    T = B * S
    tm = min(256, S)
    ns = S // tm

    def body(ya_ref, ob_ref, yc_ref, od_ref, g_ref, o_ref):
        ys = [ya_ref[...], _load_heads(ob_ref, N_HEADS), yc_ref[...], _load_heads(od_ref, N_HEADS)]
        for m, y in enumerate(ys):
            r = lax.rsqrt(jnp.mean(y * y, axis=-1, keepdims=True) + RMS_EPS)
            o_ref[:, 512 * m:512 * (m + 1)] = (y * r * g_ref[:, 512 * m:512 * (m + 1)]).astype(BF16)

    row = pl.BlockSpec((tm, 512), lambda b, i: (b * ns + i, 0))
    heads = pl.BlockSpec((None, N_HEADS, tm, HEAD_DIM), lambda b, i: (b, 0, i, 0))
    return pl.pallas_call(
        body, name=name, grid=(B, ns), in_specs=[row, heads, row, heads, pl.BlockSpec((1, 2048), lambda b, i: (0, 0))],
        out_specs=pl.BlockSpec((tm, 2048), lambda b, i: (b * ns + i, 0)), out_shape=jax.ShapeDtypeStruct((T, 2048), BF16),
        compiler_params=_params(("parallel", "parallel")),
    )(ya, ob, yc, od, gain)


def _mix_bwd(ya, ob, yc, od, dycat, gain, B, S, name):
    T = B * S
    tm = min(256, S)
    ns = S // tm

    def body(ya_ref, ob_ref, yc_ref, od_ref, dy_ref, g_ref, dya_ref, dob_ref, dyc_ref, dod_ref, dg_ref):
        @pl.when((pl.program_id(0) == 0) & (pl.program_id(1) == 0))
        def _():
            dg_ref[...] = jnp.zeros_like(dg_ref)

        ys = [ya_ref[...], _load_heads(ob_ref, N_HEADS), yc_ref[...], _load_heads(od_ref, N_HEADS)]
        outs = [dya_ref, dob_ref, dyc_ref, dod_ref]
        for m, y in enumerate(ys):
            cols = slice(512 * m, 512 * (m + 1))
            r = lax.rsqrt(jnp.mean(y * y, axis=-1, keepdims=True) + RMS_EPS)
            yh = y * r
            dh = dy_ref[:, cols]
            dg_ref[:, cols] += jnp.sum(dh * yh, axis=0, keepdims=True)
            dyh = dh * g_ref[:, cols]
            dyv = r * (dyh - yh * jnp.mean(dyh * yh, axis=-1, keepdims=True))
            if m % 2 == 0:
                outs[m][...] = dyv
            else:
                _store_heads(outs[m], dyv, N_HEADS)

    row = pl.BlockSpec((tm, 512), lambda b, i: (b * ns + i, 0))
    heads = pl.BlockSpec((None, N_HEADS, tm, HEAD_DIM), lambda b, i: (b, 0, i, 0))
    vec = pl.BlockSpec((1, 2048), lambda b, i: (0, 0))
    flat = jax.ShapeDtypeStruct((T, 512), F32)
    hm = jax.ShapeDtypeStruct((B, N_HEADS, S, HEAD_DIM), F32)
    return pl.pallas_call(
        body, name=name, grid=(B, ns), in_specs=[row, heads, row, heads, pl.BlockSpec((tm, 2048), lambda b, i: (b * ns + i, 0)), vec],
        out_specs=[row, heads, row, heads, vec], out_shape=[flat, hm, flat, hm, jax.ShapeDtypeStruct((1, 2048), F32)],
        compiler_params=_params(("arbitrary", "arbitrary")),
    )(ya, ob, yc, od, dycat, gain)


def _swiglu_fwd(gate, up, name):
    J, T, n = gate.shape
    tm = min(512, T)

    def body(g_ref, u_ref, o_ref):
        g = g_ref[...]
        o_ref[...] = (g * _sigmoid(g) * u_ref[...]).astype(BF16)

    blk = pl.BlockSpec((None, tm, n), lambda j, i: (j, i, 0))
    return pl.pallas_call(body, name=name, grid=(J, T // tm), in_specs=[blk, blk], out_specs=blk,
                          out_shape=jax.ShapeDtypeStruct((J, T, n), BF16), compiler_params=_params(("parallel", "parallel")))(gate, up)


def _swiglu_bwd(gate, up, dact, name):
    J, T, n = gate.shape
    tm = min(512, T)

    def body(g_ref, u_ref, d_ref, dg_ref, du_ref):
        g, d = g_ref[...], d_ref[...]
        s = _sigmoid(g)
        dg_ref[...] = (d * u_ref[...] * s * (1.0 + g * (1.0 - s))).astype(BF16)
        du_ref[...] = (d * g * s).astype(BF16)

    blk = pl.BlockSpec((None, tm, n), lambda j, i: (j, i, 0))
    shape = jax.ShapeDtypeStruct((J, T, n), BF16)
    return pl.pallas_call(body, name=name, grid=(J, T // tm), in_specs=[blk, blk, blk], out_specs=[blk, blk],
                          out_shape=[shape, shape], compiler_params=_params(("parallel", "parallel")))(gate, up, dact)


def _loss_grad(y, target, name):
    T, D = y.shape
    tm = min(256, T)
    n = T // tm

    def body(y_ref, t_ref, loss_ref, dy_ref, acc):
        i = pl.program_id(0)

        @pl.when(i == 0)
        def _():
            acc[...] = jnp.zeros_like(acc)

        err = y_ref[...] - t_ref[...]
        dy_ref[...] = err * (1.0 / D)
        acc[...] += jnp.sum((err * err).reshape(tm // 8, 8, D), axis=0)

        @pl.when(i == n - 1)
        def _():
            loss_ref[...] = jnp.full((8, 128), 0.5 / D, F32) * jnp.sum(acc[...])

    row = pl.BlockSpec((tm, D), lambda i: (i, 0))
    return pl.pallas_call(body, name=name, grid=(n,), in_specs=[row, row], out_specs=[pl.BlockSpec((8, 128), lambda i: (0, 0)), row],
                          out_shape=[jax.ShapeDtypeStruct((8, 128), F32), jax.ShapeDtypeStruct((T, D), F32)],
                          scratch_shapes=[pltpu.VMEM((8, D), F32)], compiler_params=_params(("arbitrary",)))(y, target)


def _row_tile(R):
    for cand in (512, 256, 128, 64, 32, 16, 8):
        if R % cand == 0:
            return cand
    return R


def _adamw(w, m, v, stack, name, layer=None, prev=None):
    n, R, C = stack.shape
    tm = _row_tile(R)
    nb = R // tm
    off = 0 if layer is None else layer * nb
    c1 = 1.0 - ADAM_B1 ** ADAM_STEP
    c2 = 1.0 - ADAM_B2 ** ADAM_STEP

    def body(w_ref, m_ref, v_ref, s_ref, *rest):
        g_ref, d_ref, mo_ref, vo_ref = rest[-4:]
        g = s_ref[0].astype(F32)
        for k in range(1, n):
            g = g + s_ref[k].astype(F32)
        mn = ADAM_B1 * m_ref[...] + (1.0 - ADAM_B1) * g
        vn = ADAM_B2 * v_ref[...] + (1.0 - ADAM_B2) * (g * g)
        g_ref[...] = g
        mo_ref[...] = mn
        vo_ref[...] = vn
        d_ref[...] = -ADAM_LR * ((mn / c1) / (jnp.sqrt(vn / c2) + ADAM_EPS) + ADAM_WD * w_ref[...])

    blk = pl.BlockSpec((tm, C), lambda i: (i + off, 0))
    ins = [w, m, v, stack]
    specs = [blk, blk, blk, pl.BlockSpec((n, tm, C), lambda i: (0, i, 0))]
    aliases = {}
    if prev is not None:
        ins += list(prev)
        specs += [pl.BlockSpec(memory_space=pl.ANY)] * 4
        aliases = {4 + t: t for t in range(4)}
    shape = jax.ShapeDtypeStruct(w.shape, F32)
    return pl.pallas_call(body, name=name, grid=(nb,), in_specs=specs, out_specs=[blk] * 4, out_shape=[shape] * 4,
                          input_output_aliases=aliases, compiler_params=_params(("parallel",)))(*ins)


def _sum_rows(stack, name):
    n, R, C = stack.shape
    tm = _row_tile(R)

    def body(s_ref, o_ref):
        tot = s_ref[0]
        for k in range(1, n):
            tot = tot + s_ref[k]
        o_ref[...] = tot

    return pl.pallas_call(body, name=name, grid=(R // tm,), in_specs=[pl.BlockSpec((n, tm, C), lambda i: (0, i, 0))],
                          out_specs=pl.BlockSpec((tm, C), lambda i: (i, 0)), out_shape=jax.ShapeDtypeStruct((R, C), F32),
                          compiler_params=_params(("parallel",)))(stack)


ANY = pl.BlockSpec(memory_space=pl.ANY)


def _place():
    x, y, c = lax.axis_index("x"), lax.axis_index("y"), lax.axis_index("c")
    return x, y, c, [(1 - x, y), (x, 1 - y), (1 - x, 1 - y)]


def _all_gather(arrs, name):
    n = len(arrs)

    def body(*refs):
        ins, outs = refs[:n], refs[n:2 * n]
        send_sems, recv_sems, local_sems = refs[2 * n:]
        x, y, c, chips = _place()
        me, sibling = 4 * x + 2 * y + c, (x, y, 1 - c)

        def copy(a, k, block, to, src=None):
            return pltpu.make_async_remote_copy(
                src_ref=outs[a].at[block] if src is None else src, dst_ref=outs[a].at[block],
                send_sem=send_sems.at[a, k], recv_sem=recv_sems.at[a, k], device_id=to, device_id_type=MESH_ID)

        mine = [pltpu.make_async_copy(ins[a], outs[a].at[me], local_sems.at[a]) for a in range(n)]
        for cp in mine:
            cp.start()
        sends = []
        for a in range(n):
            sends.append(copy(a, 0, me, sibling, src=ins[a]))
            sends += [copy(a, 1 + j, me, (cx, cy, c), src=ins[a]) for j, (cx, cy) in enumerate(chips)]
        for cp in sends:
            cp.start()
        for j, (cx, cy) in enumerate(chips):
            blk = 4 * cx + 2 * cy + c
            for a in range(n):
                copy(a, 1 + j, blk, sibling).wait_recv()
                fwd = copy(a, 4 + j, blk, sibling)
                fwd.start()
                sends.append(fwd)
        for a in range(n):
            copy(a, 0, 4 * x + 2 * y + 1 - c, sibling).wait_recv()
            for j, (cx, cy) in enumerate(chips):
                copy(a, 4 + j, 4 * cx + 2 * cy + 1 - c, sibling).wait_recv()
        for cp in sends:
            cp.wait_send()
        for cp in mine:
            cp.wait()

    return pl.pallas_call(
        body, name=name, in_specs=[ANY] * n, out_specs=[ANY] * n,
        out_shape=[jax.ShapeDtypeStruct((N_DEV,) + a.shape, a.dtype) for a in arrs],
        scratch_shapes=[pltpu.SemaphoreType.DMA((n, 7)), pltpu.SemaphoreType.DMA((n, 7)), pltpu.SemaphoreType.DMA((n,))],
        compiler_params=pltpu.CompilerParams(has_side_effects=True),
    )(*arrs)


HBM = pl.BlockSpec(memory_space=pltpu.HBM)
SEM = pl.BlockSpec(memory_space=pltpu.SEMAPHORE)
EFFECT = pltpu.SideEffectType.DATAFLOW_SIDE_EFFECTING


def _spread_copies(srcs, lands, send_sems, recv_sems, local_sems, scatter, waiting):
    x, y, c = lax.axis_index("x"), lax.axis_index("y"), lax.axis_index("c")
    me = 4 * x + 2 * y + c
    remote, local = [], []
    for a, (s, l) in enumerate(zip(srcs, lands)):
        for d in range(N_DEV - 1):
            bits = d + 1
            peer = (1 - x if bits & 4 else x, 1 - y if bits & 2 else y, 1 - c if bits & 1 else c)
            pid = 4 * peer[0] + 2 * peer[1] + peer[2]
            remote.append(pltpu.make_async_remote_copy(
                src_ref=s.at[pid] if scatter else s, dst_ref=l.at[pid if waiting else me],
                send_sem=send_sems.at[a * (N_DEV - 1) + d], recv_sem=recv_sems.at[a * (N_DEV - 1) + d], device_id=peer, device_id_type=MESH_ID))
        local.append(pltpu.make_async_copy(s.at[me] if scatter else s, l.at[me], local_sems.at[a]))
    return remote, local


def _spread_start(srcs, scatter, name, after=None):
    n = len(srcs)
    land_shapes = [a.shape if scatter else (N_DEV,) + a.shape for a in srcs]
    extra = [] if after is None else [after]

    def body(*refs):
        src_refs, land_refs = refs[:n], refs[n:2 * n]
        send_sems, recv_sems, local_sems = refs[2 * n + len(extra):2 * n + len(extra) + 3]
        remote, local = _spread_copies(src_refs, land_refs, send_sems, recv_sems, local_sems, scatter, False)
        for cp in remote + local:
            cp.start()
        refs[-1][...] = jnp.zeros((8, 128), F32)

    outs = pl.pallas_call(
        body, name=name,
        out_shape=(pltpu.SemaphoreType.DMA((n * (N_DEV - 1),)), pltpu.SemaphoreType.DMA((n * (N_DEV - 1),)), pltpu.SemaphoreType.DMA((n,)),
                   *[pltpu.HBM(a.shape, a.dtype) for a in srcs], *[pltpu.HBM(shp, a.dtype) for shp, a in zip(land_shapes, srcs)],
                   jax.ShapeDtypeStruct((8, 128), F32)),
        in_specs=[HBM] * (2 * n) + [pl.BlockSpec(memory_space=pl.ANY)] * len(extra),
        out_specs=(SEM, SEM, SEM, *[HBM] * (2 * n), pl.BlockSpec(memory_space=pltpu.VMEM)),
        input_output_aliases={i: 3 + i for i in range(2 * n)},
        compiler_params=pltpu.CompilerParams(has_side_effects=EFFECT),
    )(*[pltpu.with_memory_space_constraint(a, pltpu.HBM) for a in srcs],
      *[pltpu.with_memory_space_constraint(lax.empty(shp, a.dtype), pltpu.HBM) for shp, a in zip(land_shapes, srcs)], *extra)
    return outs[:3], list(outs[3:3 + n]), list(outs[3 + n:3 + 2 * n]), outs[-1]


def _spread_wait(sems, srcs, lands, after, scatter, name):
    n = len(srcs)
    after = list(after) if isinstance(after, (list, tuple)) else [after]

    def body(*refs):
        src_refs, land_refs = refs[:n], refs[n:2 * n]
        send_sems, recv_sems, local_sems = refs[2 * n:2 * n + 3]
        remote, local = _spread_copies(src_refs, land_refs, send_sems, recv_sems, local_sems, scatter, True)
        for cp in remote:
            cp.wait_send()
            cp.wait_recv()
        for cp in local:
            cp.wait()

    outs = pl.pallas_call(
        body, name=name, out_shape=tuple(pltpu.HBM(a.shape, a.dtype) for a in srcs + lands),
        in_specs=[HBM] * (2 * n) + [SEM] * 3 + [pl.BlockSpec(memory_space=pl.ANY)] * len(after), out_specs=tuple([HBM] * (2 * n)),
        input_output_aliases={i: i for i in range(2 * n)}, compiler_params=pltpu.CompilerParams(has_side_effects=EFFECT),
    )(*srcs, *lands, *sems, *after)
    return list(outs[n:])


SMALL = ("rel_bias", "norm1_g", "sgu_w", "sgu_b", "dil_qn_g", "dil_kn_g", "conv_w", "conv_b", "conv_ln_g", "conv_ln_b",
         "gqa_qn_g", "gqa_kn_g", "mix_norm_g", "norm2_g")
LARGE = ("w_in", "w_out", "w_gate", "w_up", "w_down")


def _local_step(x, target, p, B, S, fetch, emit, mid):
    rope = _rope_tables(S)
    win = _bias_windows(p["rel_bias"], S)
    tile8 = lambda g: jnp.tile(g.reshape(1, HEAD_DIM), (1, N_HEADS))
    cols_b = (COL_BQ, COL_BK, COL_BV)
    cols_d = (COL_DQ, COL_DK128, COL_DV128)
    saved = []
    for l in range(DEPTH):
        s = {"x": x}
        s["ws"] = p["sgu_w"][l].astype(BF16)
        s["bias"] = jnp.repeat(p["sgu_b"][l].T, HEAD_DIM, axis=1)
        s["h"] = _rms_fwd(x, p["norm1_g"][l], f"rms1_fwd_{l}")
        s["win"] = fetch(l, "in", s["h"])
        s["cw"] = jnp.pad(s["win"]["conv_w"], ((0, 1), (0, 0))).reshape(32, 1, 512)
        z = s["z"] = _matmul(s["h"], s["win"]["w_in"], "nn", f"in_proj_{l}")
        s["bias"] = s["bias"] + mid(l, z)
        s["ya"] = _sgu_fwd(z, s["ws"], s["bias"], f"sgu_fwd_{l}")
        s["c"] = _conv_fwd1(z, s["cw"], p["conv_b"][l].reshape(1, 512), B, S, f"conv_fwd_{l}")
        s["yc"] = _conv_fwd2(s["c"], p["conv_ln_g"][l].reshape(1, 512), p["conv_ln_b"][l].reshape(1, 512), f"conv_ln_fwd_{l}")
        s["gb"] = (tile8(p["dil_qn_g"][l]), tile8(p["dil_kn_g"][l]))
        s["gd"] = (tile8(p["gqa_qn_g"][l]), tile8(p["gqa_kn_g"][l])[:, :KV_WIDTH])
        s["qkv_b"] = _prep_fwd(z, *s["gb"], None, B, S, N_HEADS, cols_b, f"prep_b_fwd_{l}")
        s["qkv_d"] = _prep_fwd(z, *s["gd"], rope, B, S, KV_HEADS, cols_d, f"prep_d_fwd_{l}")
        s["ob"] = _attn_fwd(*s["qkv_b"], win, f"attn_b_fwd_{l}")
        s["od"] = _attn_fwd(*s["qkv_d"], None, f"attn_d_fwd_{l}")
        s["gmix"] = p["mix_norm_g"][l].reshape(1, 2048)
        s["ycat"] = _mix_fwd(s["ya"], s["ob"], s["yc"], s["od"], s["gmix"], B, S, f"mix_fwd_{l}")
        s["wout"] = fetch(l, "out", s["ycat"])["w_out"]
        x1 = s["x1"] = _matmul(s["ycat"], s["wout"], "nn", f"out_proj_{l}", res=x)
        s["h2"] = _rms_fwd(x1, p["norm2_g"][l], f"rms2_fwd_{l}")
        s["ffn"] = fetch(l, "ffn", s["h2"])
        s["gate"] = _mm_shard_out(s["h2"], s["ffn"]["w_gate"], "nn", f"ffn_gate_{l}")
        s["up"] = _mm_shard_out(s["h2"], s["ffn"]["w_up"], "nn", f"ffn_up_{l}")
        s["act"] = _swiglu_fwd(s["gate"], s["up"], f"swiglu_fwd_{l}")
        x = _mm_shard_k([(s["act"], s["ffn"]["w_down"])], "nn", f"ffn_down_{l}", res=x1)
        saved.append(s)

    loss_blk, dx = _loss_grad(x, target, "loss")
    g = {k: [None] * DEPTH for k in SMALL if k != "rel_bias"}
    dwin_total = None
    for l in reversed(range(DEPTH)):
        s = saved[l]
        z, ffn = s["z"], s["ffn"]
        dact = _mm_shard_out(dx, ffn["w_down"], "nt", f"ffn_down_dx_{l}")
        tok = emit(l, "w_down", _mm_shard_m(s["act"], dx, f"ffn_down_dw_{l}", out_dtype=BF16))
        dgate, dup = _swiglu_bwd(s["gate"], s["up"], dact, f"swiglu_bwd_{l}")
        tok += emit(l, "w_gate", _mm_shard_out(s["h2"], dgate, "tn", f"ffn_gate_dw_{l}", out_dtype=BF16, tm=1024))
        tok += emit(l, "w_up", _mm_shard_out(s["h2"], dup, "tn", f"ffn_up_dw_{l}", out_dtype=BF16, tm=1024))
        dh2 = _mm_shard_k([(dgate, ffn["w_gate"]), (dup, ffn["w_up"])], "nt", f"ffn_up_dx_{l}")
        dx1, dg2 = _rms_bwd(dh2, s["x1"], p["norm2_g"][l] + tok, dx, f"rms2_bwd_{l}")
        g["norm2_g"][l] = dg2[0]
        dycat = _matmul(dx1, s["wout"], "nt", f"out_proj_dx_{l}")
        tok = emit(l, "w_out", _matmul(s["ycat"], dx1, "tn", f"out_proj_dw_{l}", out_dtype=BF16))
        dya, dob, dyc, dod, dgm = _mix_bwd(s["ya"], s["ob"], s["yc"], s["od"], dycat, s["gmix"] + tok, B, S, f"mix_bwd_{l}")
        g["mix_norm_g"][l] = dgm[0]
        dz_a, dws, dbias = _sgu_bwd(z, dya, s["ws"], jnp.swapaxes(s["ws"], 1, 2), s["bias"], f"sgu_bwd_{l}")
        g["sgu_w"][l] = dws
        g["sgu_b"][l] = dbias.reshape(128, 8, HEAD_DIM).sum(-1).T
        dc, dlg, dlb, dcb = _conv_bwd1(s["c"], dyc, p["conv_ln_g"][l].reshape(1, 512), p["conv_ln_b"][l].reshape(1, 512), f"conv_ln_bwd_{l}")
        g["conv_ln_g"][l], g["conv_ln_b"][l], g["conv_b"][l] = dlg[0], dlb[0], dcb[0]
        dz_ca, dz_cg, dcw = _conv_bwd2(z, dc, s["cw"], B, S, f"conv_bwd_{l}")
        g["conv_w"][l] = dcw.reshape(32, 512)[:CONV_WIDTH]
        dq, dk, dv, dwin = _attn_bwd(*s["qkv_b"], s["ob"], dob, win, f"attn_b_bwd_{l}")
        dwin_total = dwin if dwin_total is None else dwin_total + dwin
        dz_b, dgq, dgk = _prep_bwd(z, dq, dk, dv, *s["gb"], None, B, S, N_HEADS, cols_b, f"prep_b_bwd_{l}")
        g["dil_qn_g"][l] = dgq.reshape(N_HEADS, HEAD_DIM).sum(0)
        g["dil_kn_g"][l] = dgk.reshape(N_HEADS, HEAD_DIM).sum(0)
        dq, dk, dv = _attn_bwd(*s["qkv_d"], s["od"], dod, None, f"attn_d_bwd_{l}")
        dz_d, dgq, dgk = _prep_bwd(z, dq, dk, dv, *s["gd"], rope, B, S, KV_HEADS, cols_d, f"prep_d_bwd_{l}")
        g["gqa_qn_g"][l] = dgq.reshape(N_HEADS, HEAD_DIM).sum(0)
        g["gqa_kn_g"][l] = dgk.reshape(KV_HEADS, HEAD_DIM).sum(0)
        dz = jnp.concatenate([dz_a, dz_b, dz_ca, dz_cg, dz_d], axis=1)
        tok = emit(l, "w_in", _matmul(s["h"], dz, "tn", f"in_proj_dw_{l}", out_dtype=BF16))
        dh = _matmul(dz, s["win"]["w_in"], "nt", f"in_proj_dx_{l}")
        dx, dg1 = _rms_bwd(dh, s["x"], p["norm1_g"][l] + tok, dx1, f"rms1_bwd_{l}")
        g["norm1_g"][l] = dg1[0]

    grads = {k: jnp.stack(v) for k, v in g.items()}
    grads["rel_bias"] = _bias_fold(dwin_total, S, "bias_fold")
    return loss_blk[0, 0], dx, grads


PACK_ROWS = 512


def _pack(parts):
    flat = jnp.concatenate([a.reshape(-1) for a in parts])
    rows = -(-flat.shape[0] // (128 * PACK_ROWS)) * PACK_ROWS
    return jnp.pad(flat, (0, rows * 128 - flat.shape[0])).reshape(rows, 128)


def _unpack(buf, shapes):
    flat, out, off = buf.reshape(-1), [], 0
    for shp in shapes:
        n = int(np.prod(shp))
        out.append(flat[off:off + n].reshape(shp))
        off += n
    return out


GROUPS = {"in": ("w_in",), "out": ("w_out",), "ffn": ("w_gate", "w_up", "w_down")}
COL_SHARDED = ("w_in", "w_gate", "w_up")


def kernel(x, rel_bias, norm1_g, w_in, sgu_w, sgu_b, dil_qn_g, dil_kn_g, conv_w, conv_b, conv_ln_g, conv_ln_b, gqa_qn_g, gqa_kn_g, mix_norm_g, w_out, norm2_g, w_gate, w_up, w_down, loss_target, m_rel_bias, m_norm1_g, m_w_in, m_sgu_w, m_sgu_b, m_dil_qn_g, m_dil_kn_g, m_conv_w, m_conv_b, m_conv_ln_g, m_conv_ln_b, m_gqa_qn_g, m_gqa_kn_g, m_mix_norm_g, m_w_out, m_norm2_g, m_w_gate, m_w_up, m_w_down, v_rel_bias, v_norm1_g, v_w_in, v_sgu_w, v_sgu_b, v_dil_qn_g, v_dil_kn_g, v_conv_w, v_conv_b, v_conv_ln_g, v_conv_ln_b, v_gqa_qn_g, v_gqa_kn_g, v_mix_norm_g, v_w_out, v_norm2_g, v_w_gate, v_w_up, v_w_down):
    w = dict(rel_bias=rel_bias, norm1_g=norm1_g, w_in=w_in, sgu_w=sgu_w, sgu_b=sgu_b, dil_qn_g=dil_qn_g, dil_kn_g=dil_kn_g, conv_w=conv_w,
             conv_b=conv_b, conv_ln_g=conv_ln_g, conv_ln_b=conv_ln_b, gqa_qn_g=gqa_qn_g, gqa_kn_g=gqa_kn_g, mix_norm_g=mix_norm_g,
             w_out=w_out, norm2_g=norm2_g, w_gate=w_gate, w_up=w_up, w_down=w_down)
    m = dict(rel_bias=m_rel_bias, norm1_g=m_norm1_g, w_in=m_w_in, sgu_w=m_sgu_w, sgu_b=m_sgu_b, dil_qn_g=m_dil_qn_g, dil_kn_g=m_dil_kn_g,
             conv_w=m_conv_w, conv_b=m_conv_b, conv_ln_g=m_conv_ln_g, conv_ln_b=m_conv_ln_b, gqa_qn_g=m_gqa_qn_g, gqa_kn_g=m_gqa_kn_g,
             mix_norm_g=m_mix_norm_g, w_out=m_w_out, norm2_g=m_norm2_g, w_gate=m_w_gate, w_up=m_w_up, w_down=m_w_down)
    v = dict(rel_bias=v_rel_bias, norm1_g=v_norm1_g, w_in=v_w_in, sgu_w=v_sgu_w, sgu_b=v_sgu_b, dil_qn_g=v_dil_qn_g, dil_kn_g=v_dil_kn_g,
             conv_w=v_conv_w, conv_b=v_conv_b, conv_ln_g=v_conv_ln_g, conv_ln_b=v_conv_ln_b, gqa_qn_g=v_gqa_qn_g, gqa_kn_g=v_gqa_kn_g,
             mix_norm_g=v_mix_norm_g, w_out=v_w_out, norm2_g=v_norm2_g, w_gate=v_w_gate, w_up=v_w_up, w_down=v_w_down)
    names = list(w)
    B, S, D = x.shape
    T = B * S
    me = 4 * lax.axis_index("x") + 2 * lax.axis_index("y") + lax.axis_index("c")

    bf = {k: w[k].astype(BF16) for k in LARGE}
    spreads = {}

    def start_gather(l, group, after=None):
        srcs = [bf[k][l] for k in GROUPS[group]] + ([conv_w[l]] if group == "in" else [])
        spreads[l, group] = _spread_start(srcs, False, f"gather_{group}_{l}_start", after)
        return spreads[l, group][3][0, 0]

    tok0 = start_gather(0, "in") + start_gather(0, "out") + start_gather(0, "ffn")
    small = {k: w[k] for k in SMALL}
    small["norm1_g"] = norm1_g.at[0].add(tok0)

    def mid(l, z):
        if l > 0:
            return jnp.zeros((), F32)
        return start_gather(1, "in", z) + start_gather(1, "out", z) + start_gather(1, "ffn", z)

    def fetch(l, group, after):
        sems, srcs, lands, _ = spreads[l, group]
        got = dict(zip(GROUPS[group] + ("conv_w",), _spread_wait(sems, srcs, lands, after, False, f"gather_{group}_{l}_wait")))
        if group == "in":
            got["w_in"] = jnp.transpose(got["w_in"], (1, 0, 2)).reshape(D, IN_WIDTH)
            got["conv_w"] = jnp.transpose(got["conv_w"], (1, 0, 2)).reshape(CONV_WIDTH, 512)
        if group == "out":
            got["w_out"] = got["w_out"].reshape(D, D)
        return got

    scatters = {}

    def emit(l, k, dw):
        if k == "w_in":
            dw = jnp.transpose(dw.reshape(D, N_DEV, IN_WIDTH // N_DEV), (1, 0, 2))
        if k == "w_out":
            dw = dw.reshape(N_DEV, D // N_DEV, D)
        scatters[l, k] = _spread_start([dw], True, f"scatter_{k}_{l}_start")
        return scatters[l, k][3][0, 0]

    loss_part, dx, grads = _local_step(x.reshape(T, D), loss_target.reshape(T, D), small, B, S, fetch, emit, mid)
    loss = lax.psum(loss_part, ("x", "y", "c"))

    out_g, out_d, out_m, out_v = {}, {}, {}, {}

    def update_large(k, after):
        shp = w[k].shape
        two_d = lambda a: a.reshape(-1, shp[-1])
        res = None
        for l in reversed(range(DEPTH)):
            sems, srcs, lands, _ = scatters[l, k]
            stack = _spread_wait(sems, srcs, lands, after, True, f"scatter_{k}_{l}_wait")[0]
            res = _adamw(two_d(w[k]), two_d(m[k]), two_d(v[k]), stack.reshape(N_DEV, -1, shp[-1]), f"adamw_{k}_{l}", layer=l, prev=res)
        out_g[k], out_d[k], out_m[k], out_v[k] = [a.reshape(shp) for a in res]
        return res[0]

    after = dx
    for k in ("w_down", "w_gate", "w_up", "w_out"):
        after = update_large(k, after)

    small_shapes = [grads[k].shape for k in SMALL]
    stack = _all_gather([_pack([grads[k] for k in SMALL])], "gather_small_grads")[0]
    summed = dict(zip(SMALL, _unpack(_sum_rows(stack, "sum_small_grads"), small_shapes)))
    summed["conv_w"] = lax.dynamic_slice_in_dim(summed["conv_w"], me * (512 // N_DEV), 512 // N_DEV, axis=2)
    shapes = [w[k].shape for k in SMALL]
    res = _adamw(_pack([w[k] for k in SMALL]), _pack([m[k] for k in SMALL]), _pack([v[k] for k in SMALL]),
                 _pack([summed[k] for k in SMALL])[None], "adamw_small")
    for dst, buf in zip((out_g, out_d, out_m, out_v), res):
        dst.update(zip(SMALL, _unpack(buf, shapes)))
    update_large("w_in", res[0])

    return (loss, dx.reshape(B, S, D), *[out_g[k] for k in names], *[out_d[k] for k in names],
            *[out_m[k] for k in names], *[out_v[k] for k in names])
```

```python
import functools
import math

import numpy as np
import jax
import jax.numpy as jnp
from jax import lax
from jax.experimental import pallas as pl
from jax.experimental.pallas import tpu as pltpu

F32 = jnp.float32
BF16 = jnp.bfloat16
HIGHEST = lax.Precision.HIGHEST
MESH_ID = pl.DeviceIdType.MESH

D_MODEL = 2048
DEPTH = 2
HEAD_DIM = 64
GROUP_WIDTH = 512
N_HEADS = 8
KV_HEADS = 2
KV_WIDTH = 128
SGU_CHUNK = 128
CONV_WIDTH = 31
CONV_PAD = 16
GRID_W = 64
ROPE_THETA = 10000.0
REL_BUCKETS = 32
REL_MAX_DIST = 1024
DIL_PATTERNS = ((128, 1), (512, 4), (2048, 16))
FFN_HIDDEN = 5632
IN_WIDTH = 4352
RMS_EPS = 1e-6
LN_EPS = 1e-5
MASKED = -1e30
N_DEV = 8

ADAM_LR = 0.001
ADAM_B1 = 0.9
ADAM_B2 = 0.999
ADAM_EPS = 1e-08
ADAM_WD = 0.01
ADAM_STEP = 10

COL_AU, COL_AV, COL_BQ, COL_BK, COL_BV, COL_CA, COL_CG, COL_DQ = range(8)
COL_DK128, COL_DV128 = 32, 33

VMEM_LIMIT = 56 * 1024 * 1024
ATTN_TQ = 256


def _params(sem=None, vmem=VMEM_LIMIT):
    return pltpu.CompilerParams(dimension_semantics=sem, vmem_limit_bytes=vmem)


def _dot(a, b, dims, precision=None):
    return lax.dot_general(a, b, (dims, ((), ())), precision=precision, preferred_element_type=F32)


def _nn(a, b, precision=None):
    return _dot(a, b, ((1,), (0,)), precision)


def _nt(a, b):
    return _dot(a, b, ((1,), (1,)))


def _tn(a, b):
    return _dot(a, b, ((0,), (0,)))


DIMS = {"nn": ((1,), (0,)), "nt": ((1,), (1,)), "tn": ((0,), (0,))}


def _pick(n, cands):
    for c in cands:
        if n % c == 0:
            return c
    return n


def _mm_call(name, mode, pairs, specs, o_spec, out_sds, grid, acc_shape, res=None, fold=None):
    npair, nk, dims = len(pairs), grid[2], DIMS[mode]

    def body(*refs):
        ab = refs[:2 * npair]
        r_ref = refs[2 * npair] if res is not None else None
        o_ref = refs[2 * npair + (res is not None)]
        part = None
        for t in range(npair):
            for s in ([None] if fold is None else range(fold)):
                a_blk = ab[2 * t][...] if s is None else ab[2 * t][s]
                b_blk = ab[2 * t + 1][...] if s is None else ab[2 * t + 1][s]
                d = _dot(a_blk.astype(BF16), b_blk.astype(BF16), dims)
                part = d if part is None else part + d

        def finish(r):
            if r_ref is not None:
                r = r + r_ref[...]
            o_ref[...] = r.astype(o_ref.dtype)

        if nk == 1:
            finish(part)
            return
        acc, k = refs[-1], pl.program_id(2)

        @pl.when(k == 0)
        def _():
            acc[...] = part

        @pl.when(k > 0)
        def _():
            acc[...] += part

        @pl.when(k == nk - 1)
        def _():
            finish(acc[...])

    ins = [t for pair in pairs for t in pair]
    in_specs = [t for pair in specs for t in pair]
    if res is not None:
        ins.append(res)
        in_specs.append(o_spec)
    return pl.pallas_call(
        body, name=name, grid=grid, in_specs=in_specs, out_specs=o_spec, out_shape=out_sds,
        scratch_shapes=[pltpu.VMEM(acc_shape, F32)] if nk > 1 else [],
        compiler_params=_params(("parallel", "parallel", "arbitrary")),
    )(*ins)


def _matmul(a, b, mode, name, res=None, out_dtype=F32, tm=512, tn=None, tk=None):
    if mode == "nn":
        (M, K), N = a.shape, b.shape[1]
    elif mode == "nt":
        (M, K), N = a.shape, b.shape[0]
    else:
        (K, M), N = a.shape, b.shape[1]
    tm = min(tm, M)
    tn = tn or _pick(N, (2176, 2048, 1408, 1024, 512))
    tk = tk or _pick(K, (1024, 2176, 1408, 512))
    assert M % tm == 0 and N % tn == 0 and K % tk == 0, (M, N, K, tm, tn, tk)
    a_spec = pl.BlockSpec((tk, tm), lambda i, j, k: (k, i)) if mode == "tn" else pl.BlockSpec((tm, tk), lambda i, j, k: (i, k))
    b_spec = pl.BlockSpec((tn, tk), lambda i, j, k: (j, k)) if mode == "nt" else pl.BlockSpec((tk, tn), lambda i, j, k: (k, j))
    o_spec = pl.BlockSpec((tm, tn), lambda i, j, k: (i, j))
    return _mm_call(name, mode, [(a, b)], [(a_spec, b_spec)], o_spec, jax.ShapeDtypeStruct((M, N), out_dtype),
                    (M // tm, N // tn, K // tk), (tm, tn), res)


def _mm_shard_out(a, bs, mode, name, out_dtype=F32, tm=512, tk=None):
    J = bs.shape[0]
    n = bs.shape[1] if mode == "nt" else bs.shape[2]
    (K, M) = a.shape if mode == "tn" else a.shape[::-1]
    tm = min(tm, M)
    tk = tk or (K if mode != "tn" else _pick(K, (1024, 512)))
    a_spec = pl.BlockSpec((tk, tm), lambda j, i, k: (k, i)) if mode == "tn" else pl.BlockSpec((tm, tk), lambda j, i, k: (i, k))
    b_spec = pl.BlockSpec((None, n, tk), lambda j, i, k: (j, 0, k)) if mode == "nt" else pl.BlockSpec((None, tk, n), lambda j, i, k: (j, k, 0))
    o_spec = pl.BlockSpec((None, tm, n), lambda j, i, k: (j, i, 0))
    return _mm_call(name, mode, [(a, bs)], [(a_spec, b_spec)], o_spec, jax.ShapeDtypeStruct((J, M, n), out_dtype),
                    (J, M // tm, K // tk), (tm, n))


def _mm_shard_k(pairs, mode, name, res=None, out_dtype=F32, tm=512, tn=None, fold=1):
    J, M, n = pairs[0][0].shape
    N = pairs[0][1].shape[2] if mode == "nn" else pairs[0][1].shape[1]
    tm = min(tm, M)
    tn = tn or _pick(N, (2048, 1024, 512))
    a_spec = pl.BlockSpec((fold, tm, n), lambda i, j, k: (k, i, 0))
    b_spec = pl.BlockSpec((fold, n, tn), lambda i, j, k: (k, 0, j)) if mode == "nn" else pl.BlockSpec((fold, tn, n), lambda i, j, k: (k, j, 0))
    o_spec = pl.BlockSpec((tm, tn), lambda i, j, k: (i, j))
    return _mm_call(name, mode, pairs, [(a_spec, b_spec)] * len(pairs), o_spec, jax.ShapeDtypeStruct((M, N), out_dtype),
                    (M // tm, N // tn, J // fold), (tm, tn), res, fold)


def _mm_shard_m(as_, b, name, out_dtype=F32, tn=None, tk=512):
    J, K, n = as_.shape
    N = b.shape[1]
    tn = tn or _pick(N, (2048, 1024, 512))
    tk = min(tk, K)
    a_spec = pl.BlockSpec((None, tk, n), lambda j, i, k: (j, k, 0))
    b_spec = pl.BlockSpec((tk, tn), lambda j, i, k: (k, i))
    o_spec = pl.BlockSpec((None, n, tn), lambda j, i, k: (j, 0, i))
    return _mm_call(name, "tn", [(as_, b)], [(a_spec, b_spec)], o_spec, jax.ShapeDtypeStruct((J, n, N), out_dtype),
                    (J, N // tn, K // tk), (n, tn))


def _seg_matrix(width):
    return jnp.asarray(np.kron(np.eye(width // HEAD_DIM, dtype=np.float32), np.full((HEAD_DIM, HEAD_DIM), 1.0 / HEAD_DIM, np.float32)))


def _segmean(v, p):
    return _nn(v, p, HIGHEST)


def _gelu(x):
    c0 = math.sqrt(2.0 / math.pi)
    t = jnp.tanh(c0 * (x + 0.044715 * x * x * x))
    return 0.5 * x * (1.0 + t), t


def _gelu_grad(x, t):
    c0 = math.sqrt(2.0 / math.pi)
    return 0.5 * (1.0 + t) + 0.5 * x * (1.0 - t * t) * c0 * (1.0 + 3.0 * 0.044715 * x * x)


def _sigmoid(x):
    return 1.0 / (1.0 + jnp.exp(-x))


def _rms_fwd(x, g, name):
    T, D = x.shape
    tm = min(256, T)

    def body(x_ref, g_ref, o_ref):
        xv = x_ref[...]
        r = lax.rsqrt(jnp.mean(xv * xv, axis=-1, keepdims=True) + RMS_EPS)
        o_ref[...] = (xv * r * g_ref[...]).astype(BF16)

    return pl.pallas_call(
        body, name=name, grid=(T // tm,),
        in_specs=[pl.BlockSpec((tm, D), lambda i: (i, 0)), pl.BlockSpec((1, D), lambda i: (0, 0))],
        out_specs=pl.BlockSpec((tm, D), lambda i: (i, 0)), out_shape=jax.ShapeDtypeStruct((T, D), BF16),
        compiler_params=_params(("parallel",)),
    )(x, g.reshape(1, D))


def _rms_bwd(dh, x, g, dres, name):
    T, D = x.shape
    tm = min(256, T)

    def body(dh_ref, x_ref, g_ref, dres_ref, dx_ref, dg_ref):
        @pl.when(pl.program_id(0) == 0)
        def _():
            dg_ref[...] = jnp.zeros_like(dg_ref)

        xv, dhv = x_ref[...], dh_ref[...]
        r = lax.rsqrt(jnp.mean(xv * xv, axis=-1, keepdims=True) + RMS_EPS)
        y = xv * r
        dy = dhv * g_ref[...]
        dx_ref[...] = dres_ref[...] + r * (dy - y * jnp.mean(dy * y, axis=-1, keepdims=True))
        dg_ref[...] += jnp.sum(dhv * y, axis=0, keepdims=True)

    row = pl.BlockSpec((tm, D), lambda i: (i, 0))
    vec = pl.BlockSpec((1, D), lambda i: (0, 0))
    return pl.pallas_call(
        body, name=name, grid=(T // tm,), in_specs=[row, row, vec, row], out_specs=[row, vec],
        out_shape=[jax.ShapeDtypeStruct((T, D), F32), jax.ShapeDtypeStruct((1, D), F32)],
        compiler_params=_params(("arbitrary",)),
    )(dh, x, g.reshape(1, D), dres)


def _sgu_core(zu, zv, ws_ref, bias, p):
    ug, tu = _gelu(zu)
    vg, tv = _gelu(zv)
    xc = vg - _segmean(vg, p)
    rs = lax.rsqrt(_segmean(xc * xc, p) + LN_EPS)
    vn = xc * rs
    vnb = vn.astype(BF16)
    low = lax.broadcasted_iota(jnp.int32, (SGU_CHUNK, 128), 1) < HEAD_DIM
    parts = []
    for j in range(4):
        vp = vnb[:, 128 * j:128 * (j + 1)]
        parts.append(jnp.where(low, _nn(ws_ref[2 * j], vp), _nn(ws_ref[2 * j + 1], vp)))
    mixed = jnp.concatenate(parts, axis=1) + bias
    return ug, tu, tv, rs, vn, vnb, mixed, low


def _sgu_fwd(z, ws, bias, name):
    T = z.shape[0]

    def body(zu_ref, zv_ref, ws_ref, b_ref, p_ref, y_ref):
        ug, _, _, _, _, _, mixed, _ = _sgu_core(zu_ref[...], zv_ref[...], ws_ref, b_ref[...], p_ref[...])
        y_ref[...] = ug * mixed

    full = lambda shape: pl.BlockSpec(shape, lambda i: (0,) * len(shape))
    return pl.pallas_call(
        body, name=name, grid=(T // SGU_CHUNK,),
        in_specs=[pl.BlockSpec((SGU_CHUNK, 512), lambda i: (i, COL_AU)), pl.BlockSpec((SGU_CHUNK, 512), lambda i: (i, COL_AV)),
                  full((8, 128, 128)), full((128, 512)), full((512, 512))],
        out_specs=pl.BlockSpec((SGU_CHUNK, 512), lambda i: (i, 0)), out_shape=jax.ShapeDtypeStruct((T, 512), F32),
        compiler_params=_params(("parallel",)),
    )(z, z, ws, bias, _seg_matrix(512))


def _sgu_bwd(z, dy, ws, ws_t, bias, name):
    T = z.shape[0]

    def body(zu_ref, zv_ref, dy_ref, ws_ref, wst_ref, b_ref, p_ref, dz_ref, dws_ref, db_ref):
        @pl.when(pl.program_id(0) == 0)
        def _():
            dws_ref[...] = jnp.zeros_like(dws_ref)
            db_ref[...] = jnp.zeros_like(db_ref)

        zu, zv, p = zu_ref[...], zv_ref[...], p_ref[...]
        ug, tu, tv, rs, vn, vnb, mixed, low = _sgu_core(zu, zv, ws_ref, b_ref[...], p)
        dyv = dy_ref[...]
        dmixed = dyv * ug
        db_ref[...] += dmixed
        dmb = dmixed.astype(BF16)
        zero = jnp.zeros((SGU_CHUNK, 128), BF16)
        parts = []
        for j in range(4):
            dmp, vp = dmb[:, 128 * j:128 * (j + 1)], vnb[:, 128 * j:128 * (j + 1)]
            dws_ref[2 * j] += _nt(jnp.where(low, dmp, zero), vp)
            dws_ref[2 * j + 1] += _nt(jnp.where(low, zero, dmp), vp)
            parts.append(jnp.where(low, _nn(wst_ref[2 * j], dmp), _nn(wst_ref[2 * j + 1], dmp)))
        dvn = jnp.concatenate(parts, axis=1)
        dvg = rs * (dvn - _segmean(dvn, p) - vn * _segmean(dvn * vn, p))
        dz_ref[:, 0:512] = (dyv * mixed * _gelu_grad(zu, tu)).astype(BF16)
        dz_ref[:, 512:1024] = (dvg * _gelu_grad(zv, tv)).astype(BF16)

    full = lambda shape: pl.BlockSpec(shape, lambda i: (0,) * len(shape))
    return pl.pallas_call(
        body, name=name, grid=(T // SGU_CHUNK,),
        in_specs=[pl.BlockSpec((SGU_CHUNK, 512), lambda i: (i, COL_AU)), pl.BlockSpec((SGU_CHUNK, 512), lambda i: (i, COL_AV)),
                  pl.BlockSpec((SGU_CHUNK, 512), lambda i: (i, 0)), full((8, 128, 128)), full((8, 128, 128)), full((128, 512)), full((512, 512))],
        out_specs=[pl.BlockSpec((SGU_CHUNK, 1024), lambda i: (i, 0)), full((8, 128, 128)), full((128, 512))],
        out_shape=[jax.ShapeDtypeStruct((T, 1024), BF16), jax.ShapeDtypeStruct((8, 128, 128), F32), jax.ShapeDtypeStruct((128, 512), F32)],
        compiler_params=_params(("arbitrary",)),
    )(z, z, dy, ws, ws_t, bias, _seg_matrix(512))


CONV_ROWS = 256


def _conv_taps(pad_ref, w_ref, base, flip):
    blk = pad_ref[pl.ds(base, CONV_ROWS + 2 * CONV_PAD), :]
    acc = jnp.zeros((CONV_ROWS, blk.shape[1]), F32)
    for k in range(CONV_WIDTH):
        wk = w_ref[CONV_WIDTH - 1 - k if flip else k]
        acc = acc + wk * blk[k + 1:k + 1 + CONV_ROWS, :]
    return acc


def _conv_fwd1(z, w, cb, B, S, name):
    T = B * S
    rows = min(CONV_ROWS, S)
    assert rows == CONV_ROWS

    def body(a_ref, g_ref, w_ref, cb_ref, c_ref, pad):
        pad[0:CONV_PAD, :] = jnp.zeros((CONV_PAD, 128), F32)
        pad[CONV_PAD + S:2 * CONV_PAD + S, :] = jnp.zeros((CONV_PAD, 128), F32)
        pad[CONV_PAD:CONV_PAD + S, :] = a_ref[...] * _sigmoid(g_ref[...])

        def tile(r, carry):
            base = pl.multiple_of(r * CONV_ROWS, CONV_ROWS)
            c_ref[pl.ds(base, CONV_ROWS), :] = _conv_taps(pad, w_ref, base, False) + cb_ref[...]
            return carry

        lax.fori_loop(0, S // CONV_ROWS, tile, 0)

    return pl.pallas_call(
        body, name=name, grid=(4, B),
        in_specs=[pl.BlockSpec((S, 128), lambda j, b: (b, 4 * COL_CA + j)), pl.BlockSpec((S, 128), lambda j, b: (b, 4 * COL_CG + j)),
                  pl.BlockSpec((32, 1, 128), lambda j, b: (0, 0, j)), pl.BlockSpec((1, 128), lambda j, b: (0, j))],
        out_specs=pl.BlockSpec((S, 128), lambda j, b: (b, j)), out_shape=jax.ShapeDtypeStruct((T, 512), F32),
        scratch_shapes=[pltpu.VMEM((S + 2 * CONV_PAD, 128), F32)], compiler_params=_params(("parallel", "parallel")),
    )(z, z, w, cb)


def _ln_rows(c):
    mu = jnp.mean(c, axis=-1, keepdims=True)
    xc = c - mu
    rs = lax.rsqrt(jnp.mean(xc * xc, axis=-1, keepdims=True) + LN_EPS)
    return xc * rs, rs


def _conv_fwd2(c, lng, lnb, name):
    T = c.shape[0]
    tm = min(512, T)

    def body(c_ref, g_ref, b_ref, y_ref):
        n, _ = _ln_rows(c_ref[...])
        t = n * g_ref[...] + b_ref[...]
        y_ref[...] = t * _sigmoid(t)

    row = pl.BlockSpec((tm, 512), lambda i: (i, 0))
    vec = pl.BlockSpec((1, 512), lambda i: (0, 0))
    return pl.pallas_call(body, name=name, grid=(T // tm,), in_specs=[row, vec, vec], out_specs=row,
                          out_shape=jax.ShapeDtypeStruct((T, 512), F32), compiler_params=_params(("parallel",)))(c, lng, lnb)


def _conv_bwd1(c, dy, lng, lnb, name):
    T = c.shape[0]
    tm = min(512, T)

    def body(c_ref, dy_ref, g_ref, b_ref, dc_ref, dg_ref, db_ref, dcb_ref):
        @pl.when(pl.program_id(0) == 0)
        def _():
            dg_ref[...] = jnp.zeros_like(dg_ref)
            db_ref[...] = jnp.zeros_like(db_ref)
            dcb_ref[...] = jnp.zeros_like(dcb_ref)

        n, rs = _ln_rows(c_ref[...])
        t = n * g_ref[...] + b_ref[...]
        s = _sigmoid(t)
        dt = dy_ref[...] * s * (1.0 + t * (1.0 - s))
        dg_ref[...] += jnp.sum(dt * n, axis=0, keepdims=True)
        db_ref[...] += jnp.sum(dt, axis=0, keepdims=True)
        dn = dt * g_ref[...]
        dc = rs * (dn - jnp.mean(dn, axis=-1, keepdims=True) - n * jnp.mean(dn * n, axis=-1, keepdims=True))
        dc_ref[...] = dc
        dcb_ref[...] += jnp.sum(dc, axis=0, keepdims=True)

    row = pl.BlockSpec((tm, 512), lambda i: (i, 0))
    vec = pl.BlockSpec((1, 512), lambda i: (0, 0))
    vshape = jax.ShapeDtypeStruct((1, 512), F32)
    return pl.pallas_call(body, name=name, grid=(T // tm,), in_specs=[row, row, vec, vec], out_specs=[row, vec, vec, vec],
                          out_shape=[jax.ShapeDtypeStruct((T, 512), F32), vshape, vshape, vshape],
                          compiler_params=_params(("arbitrary",)))(c, dy, lng, lnb)


def _conv_bwd2(z, dc, w, B, S, name):
    T = B * S

    def body(a_ref, g_ref, dc_ref, w_ref, da_ref, dg_ref, dw_ref, hpad, dpad, dwacc):
        @pl.when(pl.program_id(1) == 0)
        def _():
            dw_ref[...] = jnp.zeros_like(dw_ref)

        zeros = jnp.zeros((CONV_PAD, 128), F32)
        for ref in (hpad, dpad):
            ref[0:CONV_PAD, :] = zeros
            ref[CONV_PAD + S:2 * CONV_PAD + S, :] = zeros
        hpad[CONV_PAD:CONV_PAD + S, :] = a_ref[...] * _sigmoid(g_ref[...])
        dpad[CONV_PAD:CONV_PAD + S, :] = dc_ref[...]
        dwacc[...] = jnp.zeros_like(dwacc)

        def tile(r, carry):
            base = pl.multiple_of(r * CONV_ROWS, CONV_ROWS)
            dh = _conv_taps(dpad, w_ref, base, True)
            av, gv = a_ref[pl.ds(base, CONV_ROWS), :], g_ref[pl.ds(base, CONV_ROWS), :]
            sg = _sigmoid(gv)
            da_ref[pl.ds(base, CONV_ROWS), :] = (dh * sg).astype(BF16)
            dg_ref[pl.ds(base, CONV_ROWS), :] = (dh * av * sg * (1.0 - sg)).astype(BF16)
            dcv = dc_ref[pl.ds(base, CONV_ROWS), :]
            blk = hpad[pl.ds(base, CONV_ROWS + 2 * CONV_PAD), :]
            for k in range(CONV_WIDTH):
                prod = dcv * blk[k + 1:k + 1 + CONV_ROWS, :]
                dwacc[k] += jnp.sum(prod.reshape(CONV_ROWS // 8, 8, 128), axis=0)
            return carry

        lax.fori_loop(0, S // CONV_ROWS, tile, 0)
        for k in range(CONV_WIDTH):
            dw_ref[k] += jnp.sum(dwacc[k], axis=0, keepdims=True)

    return pl.pallas_call(
        body, name=name, grid=(4, B),
        in_specs=[pl.BlockSpec((S, 128), lambda j, b: (b, 4 * COL_CA + j)), pl.BlockSpec((S, 128), lambda j, b: (b, 4 * COL_CG + j)),
                  pl.BlockSpec((S, 128), lambda j, b: (b, j)), pl.BlockSpec((32, 1, 128), lambda j, b: (0, 0, j))],
        out_specs=[pl.BlockSpec((S, 128), lambda j, b: (b, j)), pl.BlockSpec((S, 128), lambda j, b: (b, j)),
                   pl.BlockSpec((32, 1, 128), lambda j, b: (0, 0, j))],
        out_shape=[jax.ShapeDtypeStruct((T, 512), BF16), jax.ShapeDtypeStruct((T, 512), BF16), jax.ShapeDtypeStruct((32, 1, 512), F32)],
        scratch_shapes=[pltpu.VMEM((S + 2 * CONV_PAD, 128), F32), pltpu.VMEM((S + 2 * CONV_PAD, 128), F32), pltpu.VMEM((32, 8, 128), F32)],
        compiler_params=_params(("parallel", "arbitrary")),
    )(z, z, dc, w)


def _swap16(x):
    n = x.shape[1]
    first = (lax.broadcasted_iota(jnp.int32, x.shape, 1) % 32) < 16
    return jnp.where(first, pltpu.roll(x, n - 16, 1), pltpu.roll(x, 16, 1))


def _rope(x, cos, sin):
    return x * cos + _swap16(x) * sin


def _rope_t(dy, cos, sin):
    return dy * cos + _swap16(dy * sin)


def _qk_norm(x, p):
    r = lax.rsqrt(_segmean(x * x, p) + RMS_EPS)
    return x * r, r


def _store_heads(ref, val, n):
    for h in range(n):
        ref[h] = val[:, HEAD_DIM * h:HEAD_DIM * (h + 1)].astype(ref.dtype)


def _load_heads(ref, n):
    return jnp.concatenate([ref[h] for h in range(n)], axis=1)


def _prep_fwd(z, gq, gk, rope, B, S, kv_heads, cols, name):
    tm = min(256, S)
    ns = S // tm
    kw = kv_heads * HEAD_DIM
    scale = HEAD_DIM ** -0.5
    qc, kc, vc = cols

    def body(*refs):
        if rope is None:
            q_ref, k_ref, v_ref, gq_ref, gk_ref, p_ref, qo, ko, vo = refs
        else:
            q_ref, k_ref, v_ref, gq_ref, gk_ref, p_ref, cos_ref, sin_ref, qo, ko, vo = refs
        p = p_ref[...]
        qn, _ = _qk_norm(q_ref[...], p)
        kn, _ = _qk_norm(k_ref[...], p[:kw, :kw])
        qn, kn = qn * gq_ref[...], kn * gk_ref[...]
        if rope is not None:
            cos, sin = cos_ref[...], sin_ref[...]
            qn, kn = _rope(qn, cos, sin), _rope(kn, cos[:, :kw], sin[:, :kw])
        _store_heads(qo, qn * scale, N_HEADS)
        _store_heads(ko, kn, kv_heads)
        _store_heads(vo, v_ref[...], kv_heads)

    row = lambda w, c: pl.BlockSpec((tm, w), lambda b, i: (b * ns + i, c))
    const = lambda shape: pl.BlockSpec(shape, lambda b, i: (0,) * len(shape))
    heads = lambda n: pl.BlockSpec((None, n, tm, HEAD_DIM), lambda b, i: (b, 0, i, 0))
    ins = [z, z, z, gq, gk, _seg_matrix(512)]
    specs = [row(512, qc), row(kw, kc), row(kw, vc), const((1, 512)), const((1, kw)), const((512, 512))]
    if rope is not None:
        ins += list(rope)
        specs += [pl.BlockSpec((tm, 512), lambda b, i: (i, 0))] * 2
    return pl.pallas_call(
        body, name=name, grid=(B, ns), in_specs=specs, out_specs=[heads(N_HEADS), heads(kv_heads), heads(kv_heads)],
        out_shape=[jax.ShapeDtypeStruct((B, N_HEADS, S, HEAD_DIM), BF16), jax.ShapeDtypeStruct((B, kv_heads, S, HEAD_DIM), BF16),
                   jax.ShapeDtypeStruct((B, kv_heads, S, HEAD_DIM), BF16)],
        compiler_params=_params(("parallel", "parallel")),
    )(*ins)


def _prep_bwd(z, dq, dk, dv, gq, gk, rope, B, S, kv_heads, cols, name):
    T = B * S
    tm = min(256, S)
    ns = S // tm
    kw = kv_heads * HEAD_DIM
    scale = HEAD_DIM ** -0.5
    qc, kc, _ = cols

    def body(*refs):
        if rope is None:
            q_ref, k_ref, dq_ref, dk_ref, dv_ref, gq_ref, gk_ref, p_ref, dz_ref, dgq_ref, dgk_ref = refs
        else:
            q_ref, k_ref, dq_ref, dk_ref, dv_ref, gq_ref, gk_ref, p_ref, cos_ref, sin_ref, dz_ref, dgq_ref, dgk_ref = refs

        @pl.when((pl.program_id(0) == 0) & (pl.program_id(1) == 0))
        def _():
            dgq_ref[...] = jnp.zeros_like(dgq_ref)
            dgk_ref[...] = jnp.zeros_like(dgk_ref)

        p = p_ref[...]
        dqv = _load_heads(dq_ref, N_HEADS) * scale
        dkv = _load_heads(dk_ref, kv_heads)
        if rope is not None:
            cos, sin = cos_ref[...], sin_ref[...]
            dqv, dkv = _rope_t(dqv, cos, sin), _rope_t(dkv, cos[:, :kw], sin[:, :kw])

        def through_norm(xv, dy, g, pm, dg_ref):
            xh, r = _qk_norm(xv, pm)
            dg_ref[...] += jnp.sum(dy * xh, axis=0, keepdims=True)
            dxh = dy * g
            return r * (dxh - xh * _segmean(dxh * xh, pm))

        dz_ref[:, 0:512] = through_norm(q_ref[...], dqv, gq_ref[...], p, dgq_ref).astype(BF16)
        dz_ref[:, 512:512 + kw] = through_norm(k_ref[...], dkv, gk_ref[...], p[:kw, :kw], dgk_ref).astype(BF16)
        dz_ref[:, 512 + kw:512 + 2 * kw] = _load_heads(dv_ref, kv_heads).astype(BF16)

    row = lambda w, c: pl.BlockSpec((tm, w), lambda b, i: (b * ns + i, c))
    const = lambda shape: pl.BlockSpec(shape, lambda b, i: (0,) * len(shape))
    heads = lambda n: pl.BlockSpec((None, n, tm, HEAD_DIM), lambda b, i: (b, 0, i, 0))
    ins = [z, z, dq, dk, dv, gq, gk, _seg_matrix(512)]
    specs = [row(512, qc), row(kw, kc), heads(N_HEADS), heads(kv_heads), heads(kv_heads), const((1, 512)), const((1, kw)), const((512, 512))]
    if rope is not None:
        ins += list(rope)
        specs += [pl.BlockSpec((tm, 512), lambda b, i: (i, 0))] * 2
    return pl.pallas_call(
        body, name=name, grid=(B, ns), in_specs=specs, out_specs=[row(512 + 2 * kw, 0), const((1, 512)), const((1, kw))],
        out_shape=[jax.ShapeDtypeStruct((T, 512 + 2 * kw), BF16), jax.ShapeDtypeStruct((1, 512), F32), jax.ShapeDtypeStruct((1, kw), F32)],
        compiler_params=_params(("arbitrary", "arbitrary")),
    )(*ins)


def _toeplitz(win, tq, S):
    r = pltpu.roll(jnp.broadcast_to(win, (tq, S + tq)), 0, 1, stride=1, stride_axis=0)
    return r[:, tq:tq + S]


def _attn_fwd(q, k, v, win, name):
    B, H, S, _ = q.shape
    group = H // k.shape[1]
    tq = min(ATTN_TQ, S)

    def body(*refs):
        if win is None:
            q_ref, k_ref, v_ref, o_ref = refs
        else:
            q_ref, k_ref, v_ref, w_ref, o_ref = refs
        s = _nt(q_ref[...], k_ref[...])
        if win is not None:
            s = s + _toeplitz(w_ref[...], tq, S)
        p = jnp.exp(s - jnp.max(s, axis=-1, keepdims=True))
        l = jnp.sum(p, axis=-1, keepdims=True)
        o_ref[...] = _nn(p.astype(BF16), v_ref[...]) / l

    qs = pl.BlockSpec((None, None, tq, HEAD_DIM), lambda b, h, i: (b, h, i, 0))
    ks = pl.BlockSpec((None, None, S, HEAD_DIM), lambda b, h, i: (b, h // group, 0, 0))
    ins, specs = [q, k, v], [qs, ks, ks]
    if win is not None:
        ins.append(win)
        specs.append(pl.BlockSpec((None, None, 1, S + tq), lambda b, h, i: (h, i, 0, 0)))
    return pl.pallas_call(body, name=name, grid=(B, H, S // tq), in_specs=specs, out_specs=qs,
                          out_shape=jax.ShapeDtypeStruct((B, H, S, HEAD_DIM), F32),
                          compiler_params=_params(("parallel", "parallel", "parallel")))(*ins)


def _attn_bwd(q, k, v, o, do, win, name):
    B, H, S, _ = q.shape
    hkv = k.shape[1]
    group = H // hkv
    tq = min(ATTN_TQ, S)
    nq = S // tq

    def body(*refs):
        if win is None:
            q_ref, k_ref, v_ref, o_ref, do_ref, dq_ref, dk_ref, dv_ref = refs
        else:
            q_ref, k_ref, v_ref, o_ref, do_ref, w_ref, rev_ref, dq_ref, dk_ref, dv_ref, dw_ref = refs

        @pl.when((pl.program_id(2) == 0) & (pl.program_id(3) == 0))
        def _():
            dk_ref[...] = jnp.zeros_like(dk_ref)
            dv_ref[...] = jnp.zeros_like(dv_ref)

        qv, kv, vv = q_ref[...], k_ref[...], v_ref[...]
        s = _nt(qv, kv)
        if win is not None:
            s = s + _toeplitz(w_ref[...], tq, S)
        p = jnp.exp(s - jnp.max(s, axis=-1, keepdims=True))
        p = p / jnp.sum(p, axis=-1, keepdims=True)
        dov = do_ref[...]
        dob = dov.astype(BF16)
        dp = _nt(dob, vv)
        ds = p * (dp - jnp.sum(dov * o_ref[...], axis=-1, keepdims=True))
        dsb = ds.astype(BF16)
        dv_ref[...] += _tn(p.astype(BF16), dob)
        dk_ref[...] += _tn(dsb, qv)
        dq_ref[...] = _nn(dsb, kv)
        if win is not None:
            rev = _nn(rev_ref[...], dsb)
            wide = jnp.concatenate([rev, jnp.zeros((tq, tq), F32)], axis=1)
            dw_ref[...] = jnp.sum(pltpu.roll(wide, 0, 1, stride=1, stride_axis=0), axis=0, keepdims=True)

    qs = pl.BlockSpec((None, None, tq, HEAD_DIM), lambda b, h, g, i: (b, h * group + g, i, 0))
    ks = pl.BlockSpec((None, None, S, HEAD_DIM), lambda b, h, g, i: (b, h, 0, 0))
    ins, specs = [q, k, v, o, do], [qs, ks, ks, qs, qs]
    outs = [jax.ShapeDtypeStruct((B, H, S, HEAD_DIM), F32), jax.ShapeDtypeStruct((B, hkv, S, HEAD_DIM), F32), jax.ShapeDtypeStruct((B, hkv, S, HEAD_DIM), F32)]
    ospecs = [qs, ks, ks]
    if win is not None:
        ins += [win, jnp.asarray(np.eye(tq, dtype=np.float32)[::-1].copy(), BF16)]
        specs += [pl.BlockSpec((None, None, 1, S + tq), lambda b, h, g, i: (h * group + g, i, 0, 0)), pl.BlockSpec((tq, tq), lambda b, h, g, i: (0, 0))]
        outs.append(jax.ShapeDtypeStruct((B, H, nq, 1, S + tq), F32))
        ospecs.append(pl.BlockSpec((None, None, None, 1, S + tq), lambda b, h, g, i: (b, h * group + g, i, 0, 0)))
    return pl.pallas_call(body, name=name, grid=(B, hkv, group, nq), in_specs=specs, out_specs=ospecs, out_shape=outs,
                          compiler_params=_params(("parallel", "parallel", "arbitrary", "arbitrary")))(*ins)


def _pattern_count(delta):
    n = jnp.zeros(delta.shape, jnp.int32)
    for window, dil in DIL_PATTERNS:
        n = n + ((delta % dil == 0) & (jnp.abs(delta) <= window // 2)).astype(jnp.int32)
    return n


def _t5_bucket(rel):
    nb = REL_BUCKETS // 2
    max_exact = nb // 2
    ret = jnp.where(rel > 0, nb, 0)
    n = jnp.abs(rel)
    nf = jnp.maximum(n, 1).astype(F32)
    large = max_exact + (jnp.log(nf / max_exact) / math.log(REL_MAX_DIST / max_exact) * (nb - max_exact)).astype(jnp.int32)
    large = jnp.minimum(large, nb - 1)
    return ret + jnp.where(n < max_exact, n, large)


def _bias_windows(rel_bias, S):
    tq = min(ATTN_TQ, S)
    nq = S // tq
    n = nq * (S + tq)
    delta = (jnp.arange(S + tq)[None, :] - (jnp.arange(nq)[:, None] + 1) * tq).reshape(n)
    count = _pattern_count(delta)
    onehot = (_t5_bucket(delta)[None, :] == jnp.arange(REL_BUCKETS)[:, None]).astype(F32)
    extra = jnp.where(count > 0, jnp.log(jnp.maximum(count, 1).astype(F32)), MASKED).reshape(1, n)
    live = (count > 0).astype(F32).reshape(1, n)

    def body(t_ref, oh_ref, live_ref, extra_ref, o_ref):
        o_ref[...] = _nn(t_ref[...], oh_ref[...], HIGHEST) * live_ref[...] + extra_ref[...]

    val = pl.pallas_call(body, name="bias_windows", out_shape=jax.ShapeDtypeStruct((N_HEADS, n), F32),
                         compiler_params=_params())(rel_bias.T, onehot, live, extra)
    return val.reshape(N_HEADS, nq, 1, S + tq)


def _bias_fold(dwin, S, name):
    B, H, nq = dwin.shape[:3]
    tq = min(ATTN_TQ, S)
    n = nq * (S + tq)
    delta = (jnp.arange(S + tq)[None, :] - (tq - 1) - jnp.arange(nq)[:, None] * tq).reshape(n)
    onehot = (_t5_bucket(delta)[:, None] == jnp.arange(128)[None, :]).astype(F32)

    def body(d_ref, oh_ref, o_ref):
        tot = d_ref[0]
        for b in range(1, B):
            tot = tot + d_ref[b]
        o_ref[...] = _nn(tot, oh_ref[...], HIGHEST)

    out = pl.pallas_call(body, name=name, out_shape=jax.ShapeDtypeStruct((H, 128), F32), compiler_params=_params())(dwin.reshape(B, H, n), onehot)
    return out[:, :REL_BUCKETS].T


def _rope_tables(S):
    half = 16
    freqs = ROPE_THETA ** (-jnp.arange(half, dtype=F32) / half)
    t = jnp.arange(S)
    ang_r = (t // GRID_W).astype(F32)[:, None] * freqs[None, :]
    ang_c = (t % GRID_W).astype(F32)[:, None] * freqs[None, :]
    cos = jnp.concatenate([jnp.cos(ang_r)] * 2 + [jnp.cos(ang_c)] * 2, axis=1)
    sin = jnp.concatenate([-jnp.sin(ang_r), jnp.sin(ang_r), -jnp.sin(ang_c), jnp.sin(ang_c)], axis=1)
    return jnp.tile(cos, (1, N_HEADS)), jnp.tile(sin, (1, N_HEADS))


def _mix_fwd(ya, ob, yc, od, gain, B, S, name):
    T = B * S
    tm = min(256, S)
    ns = S // tm

    def body(ya_ref, ob_ref, yc_ref, od_ref, g_ref, o_ref):
        ys = [ya_ref[...], _load_heads(ob_ref, N_HEADS), yc_ref[...], _load_heads(od_ref, N_HEADS)]
        for m, y in enumerate(ys):
            r = lax.rsqrt(jnp.mean(y * y, axis=-1, keepdims=True) + RMS_EPS)
            o_ref[:, 512 * m:512 * (m + 1)] = (y * r * g_ref[:, 512 * m:512 * (m + 1)]).astype(BF16)

    row = pl.BlockSpec((tm, 512), lambda b, i: (b * ns + i, 0))
    heads = pl.BlockSpec((None, N_HEADS, tm, HEAD_DIM), lambda b, i: (b, 0, i, 0))
    return pl.pallas_call(
        body, name=name, grid=(B, ns), in_specs=[row, heads, row, heads, pl.BlockSpec((1, 2048), lambda b, i: (0, 0))],
        out_specs=pl.BlockSpec((tm, 2048), lambda b, i: (b * ns + i, 0)), out_shape=jax.ShapeDtypeStruct((T, 2048), BF16),
        compiler_params=_params(("parallel", "parallel")),
    )(ya, ob, yc, od, gain)


def _mix_bwd(ya, ob, yc, od, dycat, gain, B, S, name):
    T = B * S
    tm = min(256, S)
    ns = S // tm

    def body(ya_ref, ob_ref, yc_ref, od_ref, dy_ref, g_ref, dya_ref, dob_ref, dyc_ref, dod_ref, dg_ref):
        @pl.when((pl.program_id(0) == 0) & (pl.program_id(1) == 0))
        def _():
            dg_ref[...] = jnp.zeros_like(dg_ref)

        ys = [ya_ref[...], _load_heads(ob_ref, N_HEADS), yc_ref[...], _load_heads(od_ref, N_HEADS)]
        outs = [dya_ref, dob_ref, dyc_ref, dod_ref]
        for m, y in enumerate(ys):
            cols = slice(512 * m, 512 * (m + 1))
            r = lax.rsqrt(jnp.mean(y * y, axis=-1, keepdims=True) + RMS_EPS)
            yh = y * r
            dh = dy_ref[:, cols]
            dg_ref[:, cols] += jnp.sum(dh * yh, axis=0, keepdims=True)
            dyh = dh * g_ref[:, cols]
            dyv = r * (dyh - yh * jnp.mean(dyh * yh, axis=-1, keepdims=True))
            if m % 2 == 0:
                outs[m][...] = dyv
            else:
                _store_heads(outs[m], dyv, N_HEADS)

    row = pl.BlockSpec((tm, 512), lambda b, i: (b * ns + i, 0))
    heads = pl.BlockSpec((None, N_HEADS, tm, HEAD_DIM), lambda b, i: (b, 0, i, 0))
    vec = pl.BlockSpec((1, 2048), lambda b, i: (0, 0))
    flat = jax.ShapeDtypeStruct((T, 512), F32)
    hm = jax.ShapeDtypeStruct((B, N_HEADS, S, HEAD_DIM), F32)
    return pl.pallas_call(
        body, name=name, grid=(B, ns), in_specs=[row, heads, row, heads, pl.BlockSpec((tm, 2048), lambda b, i: (b * ns + i, 0)), vec],
        out_specs=[row, heads, row, heads, vec], out_shape=[flat, hm, flat, hm, jax.ShapeDtypeStruct((1, 2048), F32)],
        compiler_params=_params(("arbitrary", "arbitrary")),
    )(ya, ob, yc, od, dycat, gain)


def _swiglu_fwd(gate, up, name):
    J, T, n = gate.shape
    tm = min(512, T)

    def body(g_ref, u_ref, o_ref):
        g = g_ref[...].astype(F32)
        o_ref[...] = (g * _sigmoid(g) * u_ref[...].astype(F32)).astype(BF16)

    blk = pl.BlockSpec((None, tm, n), lambda j, i: (j, i, 0))
    return pl.pallas_call(body, name=name, grid=(J, T // tm), in_specs=[blk, blk], out_specs=blk,
                          out_shape=jax.ShapeDtypeStruct((J, T, n), BF16), compiler_params=_params(("parallel", "parallel")))(gate, up)


def _swiglu_bwd(gate, up, dact, name):
    J, T, n = gate.shape
    tm = min(512, T)

    def body(g_ref, u_ref, d_ref, dg_ref, du_ref):
        g, d = g_ref[...].astype(F32), d_ref[...].astype(F32)
        s = _sigmoid(g)
        dg_ref[...] = (d * u_ref[...].astype(F32) * s * (1.0 + g * (1.0 - s))).astype(BF16)
        du_ref[...] = (d * g * s).astype(BF16)

    blk = pl.BlockSpec((None, tm, n), lambda j, i: (j, i, 0))
    shape = jax.ShapeDtypeStruct((J, T, n), BF16)
    return pl.pallas_call(body, name=name, grid=(J, T // tm), in_specs=[blk, blk, blk], out_specs=[blk, blk],
                          out_shape=[shape, shape], compiler_params=_params(("parallel", "parallel")))(gate, up, dact)


def _loss_grad(y, target, name):
    T, D = y.shape
    tm = min(256, T)
    n = T // tm

    def body(y_ref, t_ref, loss_ref, dy_ref, acc):
        i = pl.program_id(0)

        @pl.when(i == 0)
        def _():
            acc[...] = jnp.zeros_like(acc)

        err = y_ref[...] - t_ref[...]
        dy_ref[...] = err * (1.0 / D)
        acc[...] += jnp.sum((err * err).reshape(tm // 8, 8, D), axis=0)

        @pl.when(i == n - 1)
        def _():
            loss_ref[...] = jnp.full((8, 128), 0.5 / D, F32) * jnp.sum(acc[...])

    row = pl.BlockSpec((tm, D), lambda i: (i, 0))
    return pl.pallas_call(body, name=name, grid=(n,), in_specs=[row, row], out_specs=[pl.BlockSpec((8, 128), lambda i: (0, 0)), row],
                          out_shape=[jax.ShapeDtypeStruct((8, 128), F32), jax.ShapeDtypeStruct((T, D), F32)],
                          scratch_shapes=[pltpu.VMEM((8, D), F32)], compiler_params=_params(("arbitrary",)))(y, target)


def _row_tile(R):
    for cand in (512, 256, 128, 64, 32, 16, 8):
        if R % cand == 0:
            return cand
    return R


def _adamw(w, m, v, stack, name, layer=None, prev=None):
    n, R, C = stack.shape
    tm = _row_tile(R)
    nb = R // tm
    off = 0 if layer is None else layer * nb
    c1 = 1.0 - ADAM_B1 ** ADAM_STEP
    c2 = 1.0 - ADAM_B2 ** ADAM_STEP

    def body(w_ref, m_ref, v_ref, s_ref, *rest):
        g_ref, d_ref, mo_ref, vo_ref = rest[-4:]
        g = s_ref[0].astype(F32)
        for k in range(1, n):
            g = g + s_ref[k].astype(F32)
        mn = ADAM_B1 * m_ref[...] + (1.0 - ADAM_B1) * g
        vn = ADAM_B2 * v_ref[...] + (1.0 - ADAM_B2) * (g * g)
        g_ref[...] = g
        mo_ref[...] = mn
        vo_ref[...] = vn
        d_ref[...] = -ADAM_LR * ((mn / c1) / (jnp.sqrt(vn / c2) + ADAM_EPS) + ADAM_WD * w_ref[...])

    blk = pl.BlockSpec((tm, C), lambda i: (i + off, 0))
    ins = [w, m, v, stack]
    specs = [blk, blk, blk, pl.BlockSpec((n, tm, C), lambda i: (0, i, 0))]
    aliases = {}
    if prev is not None:
        ins += list(prev)
        specs += [pl.BlockSpec(memory_space=pl.ANY)] * 4
        aliases = {4 + t: t for t in range(4)}
    shape = jax.ShapeDtypeStruct(w.shape, F32)
    return pl.pallas_call(body, name=name, grid=(nb,), in_specs=specs, out_specs=[blk] * 4, out_shape=[shape] * 4,
                          input_output_aliases=aliases, compiler_params=_params(("parallel",)))(*ins)


HBM = pl.BlockSpec(memory_space=pltpu.HBM)
SEM = pl.BlockSpec(memory_space=pltpu.SEMAPHORE)
EFFECT = pltpu.SideEffectType.DATAFLOW_SIDE_EFFECTING


def _spread_copies(srcs, lands, send_sems, recv_sems, local_sems, scatter, waiting):
    x, y, c = lax.axis_index("x"), lax.axis_index("y"), lax.axis_index("c")
    me = 4 * x + 2 * y + c
    remote, local = [], []
    for a, (s, l) in enumerate(zip(srcs, lands)):
        for d in range(N_DEV - 1):
            bits = d + 1
            peer = (1 - x if bits & 4 else x, 1 - y if bits & 2 else y, 1 - c if bits & 1 else c)
            pid = 4 * peer[0] + 2 * peer[1] + peer[2]
            remote.append(pltpu.make_async_remote_copy(
                src_ref=s.at[pid] if scatter else s, dst_ref=l.at[pid if waiting else me],
                send_sem=send_sems.at[a * (N_DEV - 1) + d], recv_sem=recv_sems.at[a * (N_DEV - 1) + d], device_id=peer, device_id_type=MESH_ID))
        local.append(pltpu.make_async_copy(s.at[me] if scatter else s, l.at[me], local_sems.at[a]))
    return remote, local


def _spread_start(srcs, scatter, name, after=None):
    n = len(srcs)
    land_shapes = [a.shape if scatter else (N_DEV,) + a.shape for a in srcs]
    extra = [] if after is None else [after]

    def body(*refs):
        src_refs, land_refs = refs[:n], refs[n:2 * n]
        send_sems, recv_sems, local_sems = refs[2 * n + len(extra):2 * n + len(extra) + 3]
        remote, local = _spread_copies(src_refs, land_refs, send_sems, recv_sems, local_sems, scatter, False)
        for cp in remote + local:
            cp.start()
        refs[-1][...] = jnp.zeros((8, 128), F32)

    outs = pl.pallas_call(
        body, name=name,
        out_shape=(pltpu.SemaphoreType.DMA((n * (N_DEV - 1),)), pltpu.SemaphoreType.DMA((n * (N_DEV - 1),)), pltpu.SemaphoreType.DMA((n,)),
                   *[pltpu.HBM(a.shape, a.dtype) for a in srcs], *[pltpu.HBM(shp, a.dtype) for shp, a in zip(land_shapes, srcs)],
                   jax.ShapeDtypeStruct((8, 128), F32)),
        in_specs=[HBM] * (2 * n) + [pl.BlockSpec(memory_space=pl.ANY)] * len(extra),
        out_specs=(SEM, SEM, SEM, *[HBM] * (2 * n), pl.BlockSpec(memory_space=pltpu.VMEM)),
        input_output_aliases={i: 3 + i for i in range(2 * n)},
        compiler_params=pltpu.CompilerParams(has_side_effects=EFFECT),
    )(*[pltpu.with_memory_space_constraint(a, pltpu.HBM) for a in srcs],
      *[pltpu.with_memory_space_constraint(lax.empty(shp, a.dtype), pltpu.HBM) for shp, a in zip(land_shapes, srcs)], *extra)
    return outs[:3], list(outs[3:3 + n]), list(outs[3 + n:3 + 2 * n]), outs[-1]


def _spread_wait(sems, srcs, lands, after, scatter, name):
    n = len(srcs)
    after = list(after) if isinstance(after, (list, tuple)) else [after]

    def body(*refs):
        src_refs, land_refs = refs[:n], refs[n:2 * n]
        send_sems, recv_sems, local_sems = refs[2 * n:2 * n + 3]
        remote, local = _spread_copies(src_refs, land_refs, send_sems, recv_sems, local_sems, scatter, True)
        for cp in remote:
            cp.wait_send()
            cp.wait_recv()
        for cp in local:
            cp.wait()

    outs = pl.pallas_call(
        body, name=name, out_shape=tuple(pltpu.HBM(a.shape, a.dtype) for a in srcs + lands),
        in_specs=[HBM] * (2 * n) + [SEM] * 3 + [pl.BlockSpec(memory_space=pl.ANY)] * len(after), out_specs=tuple([HBM] * (2 * n)),
        input_output_aliases={i: i for i in range(2 * n)}, compiler_params=pltpu.CompilerParams(has_side_effects=EFFECT),
    )(*srcs, *lands, *sems, *after)
    return list(outs[n:])


SMALL = ("rel_bias", "norm1_g", "sgu_w", "sgu_b", "dil_qn_g", "dil_kn_g", "conv_w", "conv_b", "conv_ln_g", "conv_ln_b",
         "gqa_qn_g", "gqa_kn_g", "mix_norm_g", "norm2_g")
LARGE = ("w_in", "w_out", "w_gate", "w_up", "w_down")


def _local_step(x, target, p, B, S, fetch, emit, mid):
    rope = _rope_tables(S)
    win = _bias_windows(p["rel_bias"], S)
    tile8 = lambda g: jnp.tile(g.reshape(1, HEAD_DIM), (1, N_HEADS))
    cols_b = (COL_BQ, COL_BK, COL_BV)
    cols_d = (COL_DQ, COL_DK128, COL_DV128)
    saved = []
    for l in range(DEPTH):
        s = {"x": x}
        s["ws"] = p["sgu_w"][l].astype(BF16)
        s["bias"] = jnp.repeat(p["sgu_b"][l].T, HEAD_DIM, axis=1)
        s["h"] = _rms_fwd(x, p["norm1_g"][l], f"rms1_fwd_{l}")
        s["win"] = fetch(l, "in", s["h"])
        s["cw"] = jnp.pad(s["win"]["conv_w"], ((0, 1), (0, 0))).reshape(32, 1, 512)
        z = s["z"] = _matmul(s["h"], s["win"]["w_in"], "nn", f"in_proj_{l}", tk=D_MODEL)
        s["bias"] = s["bias"] + mid(l, z)
        s["ya"] = _sgu_fwd(z, s["ws"], s["bias"], f"sgu_fwd_{l}")
        s["c"] = _conv_fwd1(z, s["cw"], p["conv_b"][l].reshape(1, 512), B, S, f"conv_fwd_{l}")
        s["yc"] = _conv_fwd2(s["c"], p["conv_ln_g"][l].reshape(1, 512), p["conv_ln_b"][l].reshape(1, 512), f"conv_ln_fwd_{l}")
        s["gb"] = (tile8(p["dil_qn_g"][l]), tile8(p["dil_kn_g"][l]))
        s["gd"] = (tile8(p["gqa_qn_g"][l]), tile8(p["gqa_kn_g"][l])[:, :KV_WIDTH])
        s["qkv_b"] = _prep_fwd(z, *s["gb"], None, B, S, N_HEADS, cols_b, f"prep_b_fwd_{l}")
        s["qkv_d"] = _prep_fwd(z, *s["gd"], rope, B, S, KV_HEADS, cols_d, f"prep_d_fwd_{l}")
        s["ob"] = _attn_fwd(*s["qkv_b"], win, f"attn_b_fwd_{l}")
        s["od"] = _attn_fwd(*s["qkv_d"], None, f"attn_d_fwd_{l}")
        s["gmix"] = p["mix_norm_g"][l].reshape(1, 2048)
        s["ycat"] = _mix_fwd(s["ya"], s["ob"], s["yc"], s["od"], s["gmix"], B, S, f"mix_fwd_{l}")
        s["wout"] = fetch(l, "out", s["ycat"])["w_out"]
        x1 = s["x1"] = _matmul(s["ycat"], s["wout"], "nn", f"out_proj_{l}", res=x, tk=D_MODEL)
        s["h2"] = _rms_fwd(x1, p["norm2_g"][l], f"rms2_fwd_{l}")
        s["ffn"] = fetch(l, "ffn", s["h2"])
        s["gate"] = _mm_shard_out(s["h2"], s["ffn"]["w_gate"], "nn", f"ffn_gate_{l}", out_dtype=BF16)
        s["up"] = _mm_shard_out(s["h2"], s["ffn"]["w_up"], "nn", f"ffn_up_{l}", out_dtype=BF16)
        s["act"] = _swiglu_fwd(s["gate"], s["up"], f"swiglu_fwd_{l}")
        x = _mm_shard_k([(s["act"], s["ffn"]["w_down"])], "nn", f"ffn_down_{l}", res=x1, tn=1024, fold=N_DEV)
        saved.append(s)

    loss_blk, dx = _loss_grad(x, target, "loss")
    g = {k: [None] * DEPTH for k in SMALL if k != "rel_bias"}
    dwin_total = None
    for l in reversed(range(DEPTH)):
        s = saved[l]
        z, ffn = s["z"], s["ffn"]
        dact = _mm_shard_out(dx, ffn["w_down"], "nt", f"ffn_down_dx_{l}", out_dtype=BF16)
        tok = emit(l, "w_down", _mm_shard_m(s["act"], dx, f"ffn_down_dw_{l}", out_dtype=BF16))
        dgate, dup = _swiglu_bwd(s["gate"], s["up"], dact, f"swiglu_bwd_{l}")
        tok += emit(l, "w_gate", _mm_shard_out(s["h2"], dgate, "tn", f"ffn_gate_dw_{l}", out_dtype=BF16, tm=1024))
        tok += emit(l, "w_up", _mm_shard_out(s["h2"], dup, "tn", f"ffn_up_dw_{l}", out_dtype=BF16, tm=1024))
        dh2 = _mm_shard_k([(dgate, ffn["w_gate"]), (dup, ffn["w_up"])], "nt", f"ffn_up_dx_{l}", tn=1024, fold=4)
        dx1, dg2 = _rms_bwd(dh2, s["x1"], p["norm2_g"][l] + tok, dx, f"rms2_bwd_{l}")
        g["norm2_g"][l] = dg2[0]
        dycat = _matmul(dx1, s["wout"], "nt", f"out_proj_dx_{l}", tk=D_MODEL)
        tok = emit(l, "w_out", _matmul(s["ycat"], dx1, "tn", f"out_proj_dw_{l}", out_dtype=BF16))
        dya, dob, dyc, dod, dgm = _mix_bwd(s["ya"], s["ob"], s["yc"], s["od"], dycat, s["gmix"] + tok, B, S, f"mix_bwd_{l}")
        g["mix_norm_g"][l] = dgm[0]
        dz_a, dws, dbias = _sgu_bwd(z, dya, s["ws"], jnp.swapaxes(s["ws"], 1, 2), s["bias"], f"sgu_bwd_{l}")
        g["sgu_w"][l] = dws
        g["sgu_b"][l] = dbias.reshape(128, 8, HEAD_DIM).sum(-1).T
        dc, dlg, dlb, dcb = _conv_bwd1(s["c"], dyc, p["conv_ln_g"][l].reshape(1, 512), p["conv_ln_b"][l].reshape(1, 512), f"conv_ln_bwd_{l}")
        g["conv_ln_g"][l], g["conv_ln_b"][l], g["conv_b"][l] = dlg[0], dlb[0], dcb[0]
        dz_ca, dz_cg, dcw = _conv_bwd2(z, dc, s["cw"], B, S, f"conv_bwd_{l}")
        g["conv_w"][l] = dcw.reshape(32, 512)[:CONV_WIDTH]
        dq, dk, dv, dwin = _attn_bwd(*s["qkv_b"], s["ob"], dob, win, f"attn_b_bwd_{l}")
        dwin_total = dwin if dwin_total is None else dwin_total + dwin
        dz_b, dgq, dgk = _prep_bwd(z, dq, dk, dv, *s["gb"], None, B, S, N_HEADS, cols_b, f"prep_b_bwd_{l}")
        g["dil_qn_g"][l] = dgq.reshape(N_HEADS, HEAD_DIM).sum(0)
        g["dil_kn_g"][l] = dgk.reshape(N_HEADS, HEAD_DIM).sum(0)
        dq, dk, dv = _attn_bwd(*s["qkv_d"], s["od"], dod, None, f"attn_d_bwd_{l}")
        dz_d, dgq, dgk = _prep_bwd(z, dq, dk, dv, *s["gd"], rope, B, S, KV_HEADS, cols_d, f"prep_d_bwd_{l}")
        g["gqa_qn_g"][l] = dgq.reshape(N_HEADS, HEAD_DIM).sum(0)
        g["gqa_kn_g"][l] = dgk.reshape(KV_HEADS, HEAD_DIM).sum(0)
        dz = jnp.concatenate([dz_a, dz_b, dz_ca, dz_cg, dz_d], axis=1)
        tok = emit(l, "w_in", _matmul(s["h"], dz, "tn", f"in_proj_dw_{l}", out_dtype=BF16))
        dh = _matmul(dz, s["win"]["w_in"], "nt", f"in_proj_dx_{l}", tn=1024, tk=IN_WIDTH)
        dx, dg1 = _rms_bwd(dh, s["x"], p["norm1_g"][l] + tok, dx1, f"rms1_bwd_{l}")
        g["norm1_g"][l] = dg1[0]

    grads = {k: jnp.stack(v) for k, v in g.items()}
    grads["rel_bias"] = _bias_fold(dwin_total, S, "bias_fold")
    return loss_blk[0, 0], dx, grads


GROUPS = {"in": ("w_in",), "out": ("w_out",), "ffn": ("w_gate", "w_up", "w_down")}
COL_SHARDED = ("w_in", "w_gate", "w_up")


def kernel(x, rel_bias, norm1_g, w_in, sgu_w, sgu_b, dil_qn_g, dil_kn_g, conv_w, conv_b, conv_ln_g, conv_ln_b, gqa_qn_g, gqa_kn_g, mix_norm_g, w_out, norm2_g, w_gate, w_up, w_down, loss_target, m_rel_bias, m_norm1_g, m_w_in, m_sgu_w, m_sgu_b, m_dil_qn_g, m_dil_kn_g, m_conv_w, m_conv_b, m_conv_ln_g, m_conv_ln_b, m_gqa_qn_g, m_gqa_kn_g, m_mix_norm_g, m_w_out, m_norm2_g, m_w_gate, m_w_up, m_w_down, v_rel_bias, v_norm1_g, v_w_in, v_sgu_w, v_sgu_b, v_dil_qn_g, v_dil_kn_g, v_conv_w, v_conv_b, v_conv_ln_g, v_conv_ln_b, v_gqa_qn_g, v_gqa_kn_g, v_mix_norm_g, v_w_out, v_norm2_g, v_w_gate, v_w_up, v_w_down):
    w = dict(rel_bias=rel_bias, norm1_g=norm1_g, w_in=w_in, sgu_w=sgu_w, sgu_b=sgu_b, dil_qn_g=dil_qn_g, dil_kn_g=dil_kn_g, conv_w=conv_w,
             conv_b=conv_b, conv_ln_g=conv_ln_g, conv_ln_b=conv_ln_b, gqa_qn_g=gqa_qn_g, gqa_kn_g=gqa_kn_g, mix_norm_g=mix_norm_g,
             w_out=w_out, norm2_g=norm2_g, w_gate=w_gate, w_up=w_up, w_down=w_down)
    m = dict(rel_bias=m_rel_bias, norm1_g=m_norm1_g, w_in=m_w_in, sgu_w=m_sgu_w, sgu_b=m_sgu_b, dil_qn_g=m_dil_qn_g, dil_kn_g=m_dil_kn_g,
             conv_w=m_conv_w, conv_b=m_conv_b, conv_ln_g=m_conv_ln_g, conv_ln_b=m_conv_ln_b, gqa_qn_g=m_gqa_qn_g, gqa_kn_g=m_gqa_kn_g,
             mix_norm_g=m_mix_norm_g, w_out=m_w_out, norm2_g=m_norm2_g, w_gate=m_w_gate, w_up=m_w_up, w_down=m_w_down)
    v = dict(rel_bias=v_rel_bias, norm1_g=v_norm1_g, w_in=v_w_in, sgu_w=v_sgu_w, sgu_b=v_sgu_b, dil_qn_g=v_dil_qn_g, dil_kn_g=v_dil_kn_g,
             conv_w=v_conv_w, conv_b=v_conv_b, conv_ln_g=v_conv_ln_g, conv_ln_b=v_conv_ln_b, gqa_qn_g=v_gqa_qn_g, gqa_kn_g=v_gqa_kn_g,
             mix_norm_g=v_mix_norm_g, w_out=v_w_out, norm2_g=v_norm2_g, w_gate=v_w_gate, w_up=v_w_up, w_down=v_w_down)
    names = list(w)
    B, S, D = x.shape
    T = B * S
    me = 4 * lax.axis_index("x") + 2 * lax.axis_index("y") + lax.axis_index("c")

    bf = {k: w[k].astype(BF16) for k in LARGE}
    spreads = {}

    def start_gather(l, group, after=None):
        srcs = [bf[k][l] for k in GROUPS[group]] + ([conv_w[l]] if group == "in" else [])
        spreads[l, group] = _spread_start(srcs, False, f"gather_{group}_{l}_start", after)
        return spreads[l, group][3][0, 0]

    tok0 = start_gather(0, "in") + start_gather(0, "out") + start_gather(0, "ffn")
    small = {k: w[k] for k in SMALL}
    small["norm1_g"] = norm1_g.at[0].add(tok0)

    def mid(l, z):
        if l > 0:
            return jnp.zeros((), F32)
        return start_gather(1, "in", z) + start_gather(1, "out", z) + start_gather(1, "ffn", z)

    def fetch(l, group, after):
        sems, srcs, lands, _ = spreads[l, group]
        got = dict(zip(GROUPS[group] + ("conv_w",), _spread_wait(sems, srcs, lands, after, False, f"gather_{group}_{l}_wait")))
        if group == "in":
            got["w_in"] = jnp.transpose(got["w_in"], (1, 0, 2)).reshape(D, IN_WIDTH)
            got["conv_w"] = jnp.transpose(got["conv_w"], (1, 0, 2)).reshape(CONV_WIDTH, 512)
        if group == "out":
            got["w_out"] = got["w_out"].reshape(D, D)
        return got

    scatters = {}

    def emit(l, k, dw):
        if k == "w_in":
            dw = jnp.transpose(dw.reshape(D, N_DEV, IN_WIDTH // N_DEV), (1, 0, 2))
        if k == "w_out":
            dw = dw.reshape(N_DEV, D // N_DEV, D)
        scatters[l, k] = _spread_start([dw], True, f"scatter_{k}_{l}_start")
        return scatters[l, k][3][0, 0]

    loss_part, dx, grads = _local_step(x.reshape(T, D), loss_target.reshape(T, D), small, B, S, fetch, emit, mid)
    loss = lax.psum(loss_part, ("x", "y", "c"))

    out_g, out_d, out_m, out_v = {}, {}, {}, {}

    def update_large(k, after):
        shp = w[k].shape
        two_d = lambda a: a.reshape(-1, shp[-1])
        res = None
        for l in reversed(range(DEPTH)):
            sems, srcs, lands, _ = scatters[l, k]
            stack = _spread_wait(sems, srcs, lands, after, True, f"scatter_{k}_{l}_wait")[0]
            res = _adamw(two_d(w[k]), two_d(m[k]), two_d(v[k]), stack.reshape(N_DEV, -1, shp[-1]), f"adamw_{k}_{l}", layer=l, prev=res)
        out_g[k], out_d[k], out_m[k], out_v[k] = [a.reshape(shp) for a in res]
        return res[0]

    flat2 = lambda a: a.reshape(-1, a.shape[-1])
    small_sems, small_srcs, small_lands, small_tok = _spread_start([flat2(grads[k]) for k in SMALL], False, "gather_small_grads_start")
    after = [dx, small_tok]
    for k in ("w_down", "w_gate", "w_up", "w_out"):
        after = update_large(k, after)
    stacks = _spread_wait(small_sems, small_srcs, small_lands, after, False, "gather_small_grads_wait")
    for k, stack in zip(SMALL, stacks):
        if k == "conv_w":
            stack = lax.dynamic_slice_in_dim(stack, me * (512 // N_DEV), 512 // N_DEV, axis=2)
        res = _adamw(flat2(w[k]), flat2(m[k]), flat2(v[k]), stack, f"adamw_{k}")
        out_g[k], out_d[k], out_m[k], out_v[k] = [a.reshape(w[k].shape) for a in res]
        after = res[0]
    update_large("w_in", after)

    return (loss, dx.reshape(B, S, D), *[out_g[k] for k in names], *[out_d[k] for k in names],
            *[out_m[k] for k in names], *[out_v[k] for k in names])
```

```python
import functools
import math

import numpy as np
import jax
import jax.numpy as jnp
from jax import lax
from jax.experimental import pallas as pl
from jax.experimental.pallas import tpu as pltpu

F32 = jnp.float32
BF16 = jnp.bfloat16
HIGHEST = lax.Precision.HIGHEST
MESH_ID = pl.DeviceIdType.MESH

D_MODEL = 2048
DEPTH = 2
HEAD_DIM = 64
GROUP_WIDTH = 512
N_HEADS = 8
KV_HEADS = 2
KV_WIDTH = 128
SGU_CHUNK = 128
CONV_WIDTH = 31
CONV_PAD = 16
GRID_W = 64
ROPE_THETA = 10000.0
REL_BUCKETS = 32
REL_MAX_DIST = 1024
DIL_PATTERNS = ((128, 1), (512, 4), (2048, 16))
FFN_HIDDEN = 5632
IN_WIDTH = 4352
RMS_EPS = 1e-6
LN_EPS = 1e-5
MASKED = -1e30
N_DEV = 8

ADAM_LR = 0.001
ADAM_B1 = 0.9
ADAM_B2 = 0.999
ADAM_EPS = 1e-08
ADAM_WD = 0.01
ADAM_STEP = 10

COL_AU, COL_AV, COL_BQ, COL_BK, COL_BV, COL_CA, COL_CG, COL_DQ = range(8)
COL_DK128, COL_DV128 = 32, 33

VMEM_LIMIT = 56 * 1024 * 1024
ATTN_TQ = 256


def _params(sem=None, vmem=VMEM_LIMIT):
    return pltpu.CompilerParams(dimension_semantics=sem, vmem_limit_bytes=vmem)


def _dot(a, b, dims, precision=None):
    return lax.dot_general(a, b, (dims, ((), ())), precision=precision, preferred_element_type=F32)


def _nn(a, b, precision=None):
    return _dot(a, b, ((1,), (0,)), precision)


def _nt(a, b):
    return _dot(a, b, ((1,), (1,)))


def _tn(a, b):
    return _dot(a, b, ((0,), (0,)))


DIMS = {"nn": ((1,), (0,)), "nt": ((1,), (1,)), "tn": ((0,), (0,))}


def _pick(n, cands):
    for c in cands:
        if n % c == 0:
            return c
    return n


def _mm_call(name, mode, pairs, specs, o_spec, out_sds, grid, acc_shape, res=None, fold=None):
    npair, nk, dims = len(pairs), grid[2], DIMS[mode]

    def body(*refs):
        ab = refs[:2 * npair]
        r_ref = refs[2 * npair] if res is not None else None
        o_ref = refs[2 * npair + (res is not None)]
        part = None
        for t in range(npair):
            for s in ([None] if fold is None else range(fold)):
                a_blk = ab[2 * t][...] if s is None else ab[2 * t][s]
                b_blk = ab[2 * t + 1][...] if s is None else ab[2 * t + 1][s]
                d = _dot(a_blk.astype(BF16), b_blk.astype(BF16), dims)
                part = d if part is None else part + d

        def finish(r):
            if r_ref is not None:
                r = r + r_ref[...]
            o_ref[...] = r.astype(o_ref.dtype)

        if nk == 1:
            finish(part)
            return
        acc, k = refs[-1], pl.program_id(2)

        @pl.when(k == 0)
        def _():
            acc[...] = part

        @pl.when(k > 0)
        def _():
            acc[...] += part

        @pl.when(k == nk - 1)
        def _():
            finish(acc[...])

    ins = [t for pair in pairs for t in pair]
    in_specs = [t for pair in specs for t in pair]
    if res is not None:
        ins.append(res)
        in_specs.append(o_spec)
    return pl.pallas_call(
        body, name=name, grid=grid, in_specs=in_specs, out_specs=o_spec, out_shape=out_sds,
        scratch_shapes=[pltpu.VMEM(acc_shape, F32)] if nk > 1 else [],
        compiler_params=_params(("parallel", "parallel", "arbitrary")),
    )(*ins)


def _matmul(a, b, mode, name, res=None, out_dtype=F32, tm=512, tn=None, tk=None):
    if mode == "nn":
        (M, K), N = a.shape, b.shape[1]
    elif mode == "nt":
        (M, K), N = a.shape, b.shape[0]
    else:
        (K, M), N = a.shape, b.shape[1]
    tm = min(tm, M)
    tn = tn or _pick(N, (2176, 2048, 1408, 1024, 512))
    tk = tk or _pick(K, (1024, 2176, 1408, 512))
    assert M % tm == 0 and N % tn == 0 and K % tk == 0, (M, N, K, tm, tn, tk)
    a_spec = pl.BlockSpec((tk, tm), lambda i, j, k: (k, i)) if mode == "tn" else pl.BlockSpec((tm, tk), lambda i, j, k: (i, k))
    b_spec = pl.BlockSpec((tn, tk), lambda i, j, k: (j, k)) if mode == "nt" else pl.BlockSpec((tk, tn), lambda i, j, k: (k, j))
    o_spec = pl.BlockSpec((tm, tn), lambda i, j, k: (i, j))
    return _mm_call(name, mode, [(a, b)], [(a_spec, b_spec)], o_spec, jax.ShapeDtypeStruct((M, N), out_dtype),
                    (M // tm, N // tn, K // tk), (tm, tn), res)


def _mm_shard_out(a, bs, mode, name, out_dtype=F32, tm=512, tk=None):
    J = bs.shape[0]
    n = bs.shape[1] if mode == "nt" else bs.shape[2]
    (K, M) = a.shape if mode == "tn" else a.shape[::-1]
    tm = min(tm, M)
    tk = tk or (K if mode != "tn" else _pick(K, (1024, 512)))
    a_spec = pl.BlockSpec((tk, tm), lambda j, i, k: (k, i)) if mode == "tn" else pl.BlockSpec((tm, tk), lambda j, i, k: (i, k))
    b_spec = pl.BlockSpec((None, n, tk), lambda j, i, k: (j, 0, k)) if mode == "nt" else pl.BlockSpec((None, tk, n), lambda j, i, k: (j, k, 0))
    o_spec = pl.BlockSpec((None, tm, n), lambda j, i, k: (j, i, 0))
    return _mm_call(name, mode, [(a, bs)], [(a_spec, b_spec)], o_spec, jax.ShapeDtypeStruct((J, M, n), out_dtype),
                    (J, M // tm, K // tk), (tm, n))


def _mm_shard_k(pairs, mode, name, res=None, out_dtype=F32, tm=512, tn=None, fold=1):
    J, M, n = pairs[0][0].shape
    N = pairs[0][1].shape[2] if mode == "nn" else pairs[0][1].shape[1]
    tm = min(tm, M)
    tn = tn or _pick(N, (2048, 1024, 512))
    a_spec = pl.BlockSpec((fold, tm, n), lambda i, j, k: (k, i, 0))
    b_spec = pl.BlockSpec((fold, n, tn), lambda i, j, k: (k, 0, j)) if mode == "nn" else pl.BlockSpec((fold, tn, n), lambda i, j, k: (k, j, 0))
    o_spec = pl.BlockSpec((tm, tn), lambda i, j, k: (i, j))
    return _mm_call(name, mode, pairs, [(a_spec, b_spec)] * len(pairs), o_spec, jax.ShapeDtypeStruct((M, N), out_dtype),
                    (M // tm, N // tn, J // fold), (tm, tn), res, fold)


def _mm_shard_m(as_, b, name, out_dtype=F32, tn=None, tk=512):
    J, K, n = as_.shape
    N = b.shape[1]
    tn = tn or _pick(N, (2048, 1024, 512))
    tk = min(tk, K)
    a_spec = pl.BlockSpec((None, tk, n), lambda j, i, k: (j, k, 0))
    b_spec = pl.BlockSpec((tk, tn), lambda j, i, k: (k, i))
    o_spec = pl.BlockSpec((None, n, tn), lambda j, i, k: (j, 0, i))
    return _mm_call(name, "tn", [(as_, b)], [(a_spec, b_spec)], o_spec, jax.ShapeDtypeStruct((J, n, N), out_dtype),
                    (J, N // tn, K // tk), (n, tn))


def _seg_matrix(width):
    return jnp.asarray(np.kron(np.eye(width // HEAD_DIM, dtype=np.float32), np.full((HEAD_DIM, HEAD_DIM), 1.0 / HEAD_DIM, np.float32)))


def _segmean(v, p):
    return _nn(v, p, HIGHEST)


def _gelu(x):
    c0 = math.sqrt(2.0 / math.pi)
    t = jnp.tanh(c0 * (x + 0.044715 * x * x * x))
    return 0.5 * x * (1.0 + t), t


def _gelu_grad(x, t):
    c0 = math.sqrt(2.0 / math.pi)
    return 0.5 * (1.0 + t) + 0.5 * x * (1.0 - t * t) * c0 * (1.0 + 3.0 * 0.044715 * x * x)


def _sigmoid(x):
    return 1.0 / (1.0 + jnp.exp(-x))


def _rms_fwd(x, g, name):
    T, D = x.shape
    tm = min(256, T)

    def body(x_ref, g_ref, o_ref):
        xv = x_ref[...]
        r = lax.rsqrt(jnp.mean(xv * xv, axis=-1, keepdims=True) + RMS_EPS)
        o_ref[...] = (xv * r * g_ref[...]).astype(BF16)

    return pl.pallas_call(
        body, name=name, grid=(T // tm,),
        in_specs=[pl.BlockSpec((tm, D), lambda i: (i, 0)), pl.BlockSpec((1, D), lambda i: (0, 0))],
        out_specs=pl.BlockSpec((tm, D), lambda i: (i, 0)), out_shape=jax.ShapeDtypeStruct((T, D), BF16),
        compiler_params=_params(("parallel",)),
    )(x, g.reshape(1, D))


def _rms_bwd(dh, x, g, dres, name):
    T, D = x.shape
    tm = min(256, T)

    def body(dh_ref, x_ref, g_ref, dres_ref, dx_ref, dg_ref):
        @pl.when(pl.program_id(0) == 0)
        def _():
            dg_ref[...] = jnp.zeros_like(dg_ref)

        xv, dhv = x_ref[...], dh_ref[...]
        r = lax.rsqrt(jnp.mean(xv * xv, axis=-1, keepdims=True) + RMS_EPS)
        y = xv * r
        dy = dhv * g_ref[...]
        dx_ref[...] = dres_ref[...] + r * (dy - y * jnp.mean(dy * y, axis=-1, keepdims=True))
        dg_ref[...] += jnp.sum(dhv * y, axis=0, keepdims=True)

    row = pl.BlockSpec((tm, D), lambda i: (i, 0))
    vec = pl.BlockSpec((1, D), lambda i: (0, 0))
    return pl.pallas_call(
        body, name=name, grid=(T // tm,), in_specs=[row, row, vec, row], out_specs=[row, vec],
        out_shape=[jax.ShapeDtypeStruct((T, D), F32), jax.ShapeDtypeStruct((1, D), F32)],
        compiler_params=_params(("arbitrary",)),
    )(dh, x, g.reshape(1, D), dres)


def _sgu_core(zu, zv, ws_ref, bias, p):
    ug, tu = _gelu(zu)
    vg, tv = _gelu(zv)
    xc = vg - _segmean(vg, p)
    rs = lax.rsqrt(_segmean(xc * xc, p) + LN_EPS)
    vn = xc * rs
    vnb = vn.astype(BF16)
    low = lax.broadcasted_iota(jnp.int32, (SGU_CHUNK, 128), 1) < HEAD_DIM
    parts = []
    for j in range(4):
        vp = vnb[:, 128 * j:128 * (j + 1)]
        parts.append(jnp.where(low, _nn(ws_ref[2 * j], vp), _nn(ws_ref[2 * j + 1], vp)))
    mixed = jnp.concatenate(parts, axis=1) + bias
    return ug, tu, tv, rs, vn, vnb, mixed, low


def _sgu_fwd(z, ws, bias, name):
    T = z.shape[0]

    def body(zu_ref, zv_ref, ws_ref, b_ref, p_ref, y_ref):
        ug, _, _, _, _, _, mixed, _ = _sgu_core(zu_ref[...], zv_ref[...], ws_ref, b_ref[...], p_ref[...])
        y_ref[...] = ug * mixed

    full = lambda shape: pl.BlockSpec(shape, lambda i: (0,) * len(shape))
    return pl.pallas_call(
        body, name=name, grid=(T // SGU_CHUNK,),
        in_specs=[pl.BlockSpec((SGU_CHUNK, 512), lambda i: (i, COL_AU)), pl.BlockSpec((SGU_CHUNK, 512), lambda i: (i, COL_AV)),
                  full((8, 128, 128)), full((128, 512)), full((512, 512))],
        out_specs=pl.BlockSpec((SGU_CHUNK, 512), lambda i: (i, 0)), out_shape=jax.ShapeDtypeStruct((T, 512), F32),
        compiler_params=_params(("parallel",)),
    )(z, z, ws, bias, _seg_matrix(512))


def _sgu_bwd(z, dy, ws, ws_t, bias, name):
    T = z.shape[0]

    def body(zu_ref, zv_ref, dy_ref, ws_ref, wst_ref, b_ref, p_ref, dz_ref, dws_ref, db_ref):
        @pl.when(pl.program_id(0) == 0)
        def _():
            dws_ref[...] = jnp.zeros_like(dws_ref)
            db_ref[...] = jnp.zeros_like(db_ref)

        zu, zv, p = zu_ref[...], zv_ref[...], p_ref[...]
        ug, tu, tv, rs, vn, vnb, mixed, low = _sgu_core(zu, zv, ws_ref, b_ref[...], p)
        dyv = dy_ref[...]
        dmixed = dyv * ug
        db_ref[...] += dmixed
        dmb = dmixed.astype(BF16)
        zero = jnp.zeros((SGU_CHUNK, 128), BF16)
        parts = []
        for j in range(4):
            dmp, vp = dmb[:, 128 * j:128 * (j + 1)], vnb[:, 128 * j:128 * (j + 1)]
            dws_ref[2 * j] += _nt(jnp.where(low, dmp, zero), vp)
            dws_ref[2 * j + 1] += _nt(jnp.where(low, zero, dmp), vp)
            parts.append(jnp.where(low, _nn(wst_ref[2 * j], dmp), _nn(wst_ref[2 * j + 1], dmp)))
        dvn = jnp.concatenate(parts, axis=1)
        dvg = rs * (dvn - _segmean(dvn, p) - vn * _segmean(dvn * vn, p))
        dz_ref[:, 0:512] = (dyv * mixed * _gelu_grad(zu, tu)).astype(BF16)
        dz_ref[:, 512:1024] = (dvg * _gelu_grad(zv, tv)).astype(BF16)

    full = lambda shape: pl.BlockSpec(shape, lambda i: (0,) * len(shape))
    return pl.pallas_call(
        body, name=name, grid=(T // SGU_CHUNK,),
        in_specs=[pl.BlockSpec((SGU_CHUNK, 512), lambda i: (i, COL_AU)), pl.BlockSpec((SGU_CHUNK, 512), lambda i: (i, COL_AV)),
                  pl.BlockSpec((SGU_CHUNK, 512), lambda i: (i, 0)), full((8, 128, 128)), full((8, 128, 128)), full((128, 512)), full((512, 512))],
        out_specs=[pl.BlockSpec((SGU_CHUNK, 1024), lambda i: (i, 0)), full((8, 128, 128)), full((128, 512))],
        out_shape=[jax.ShapeDtypeStruct((T, 1024), BF16), jax.ShapeDtypeStruct((8, 128, 128), F32), jax.ShapeDtypeStruct((128, 512), F32)],
        compiler_params=_params(("arbitrary",)),
    )(z, z, dy, ws, ws_t, bias, _seg_matrix(512))


CONV_ROWS = 256


def _conv_taps(pad_ref, w_ref, base, flip):
    blk = pad_ref[pl.ds(base, CONV_ROWS + 2 * CONV_PAD), :]
    acc = jnp.zeros((CONV_ROWS, blk.shape[1]), F32)
    for k in range(CONV_WIDTH):
        wk = w_ref[CONV_WIDTH - 1 - k if flip else k]
        acc = acc + wk * blk[k + 1:k + 1 + CONV_ROWS, :]
    return acc


def _conv_fwd1(z, w, cb, B, S, name):
    T = B * S
    rows = min(CONV_ROWS, S)
    assert rows == CONV_ROWS

    def body(a_ref, g_ref, w_ref, cb_ref, c_ref, pad):
        pad[0:CONV_PAD, :] = jnp.zeros((CONV_PAD, 128), F32)
        pad[CONV_PAD + S:2 * CONV_PAD + S, :] = jnp.zeros((CONV_PAD, 128), F32)
        pad[CONV_PAD:CONV_PAD + S, :] = a_ref[...] * _sigmoid(g_ref[...])

        def tile(r, carry):
            base = pl.multiple_of(r * CONV_ROWS, CONV_ROWS)
            c_ref[pl.ds(base, CONV_ROWS), :] = _conv_taps(pad, w_ref, base, False) + cb_ref[...]
            return carry

        lax.fori_loop(0, S // CONV_ROWS, tile, 0)

    return pl.pallas_call(
        body, name=name, grid=(4, B),
        in_specs=[pl.BlockSpec((S, 128), lambda j, b: (b, 4 * COL_CA + j)), pl.BlockSpec((S, 128), lambda j, b: (b, 4 * COL_CG + j)),
                  pl.BlockSpec((32, 1, 128), lambda j, b: (0, 0, j)), pl.BlockSpec((1, 128), lambda j, b: (0, j))],
        out_specs=pl.BlockSpec((S, 128), lambda j, b: (b, j)), out_shape=jax.ShapeDtypeStruct((T, 512), F32),
        scratch_shapes=[pltpu.VMEM((S + 2 * CONV_PAD, 128), F32)], compiler_params=_params(("parallel", "parallel")),
    )(z, z, w, cb)


def _ln_rows(c):
    mu = jnp.mean(c, axis=-1, keepdims=True)
    xc = c - mu
    rs = lax.rsqrt(jnp.mean(xc * xc, axis=-1, keepdims=True) + LN_EPS)
    return xc * rs, rs


def _conv_fwd2(c, lng, lnb, name):
    T = c.shape[0]
    tm = min(512, T)

    def body(c_ref, g_ref, b_ref, y_ref):
        n, _ = _ln_rows(c_ref[...])
        t = n * g_ref[...] + b_ref[...]
        y_ref[...] = t * _sigmoid(t)

    row = pl.BlockSpec((tm, 512), lambda i: (i, 0))
    vec = pl.BlockSpec((1, 512), lambda i: (0, 0))
    return pl.pallas_call(body, name=name, grid=(T // tm,), in_specs=[row, vec, vec], out_specs=row,
                          out_shape=jax.ShapeDtypeStruct((T, 512), F32), compiler_params=_params(("parallel",)))(c, lng, lnb)


def _conv_bwd1(c, dy, lng, lnb, name):
    T = c.shape[0]
    tm = min(512, T)

    def body(c_ref, dy_ref, g_ref, b_ref, dc_ref, dg_ref, db_ref, dcb_ref):
        @pl.when(pl.program_id(0) == 0)
        def _():
            dg_ref[...] = jnp.zeros_like(dg_ref)
            db_ref[...] = jnp.zeros_like(db_ref)
            dcb_ref[...] = jnp.zeros_like(dcb_ref)

        n, rs = _ln_rows(c_ref[...])
        t = n * g_ref[...] + b_ref[...]
        s = _sigmoid(t)
        dt = dy_ref[...] * s * (1.0 + t * (1.0 - s))
        dg_ref[...] += jnp.sum(dt * n, axis=0, keepdims=True)
        db_ref[...] += jnp.sum(dt, axis=0, keepdims=True)
        dn = dt * g_ref[...]
        dc = rs * (dn - jnp.mean(dn, axis=-1, keepdims=True) - n * jnp.mean(dn * n, axis=-1, keepdims=True))
        dc_ref[...] = dc
        dcb_ref[...] += jnp.sum(dc, axis=0, keepdims=True)

    row = pl.BlockSpec((tm, 512), lambda i: (i, 0))
    vec = pl.BlockSpec((1, 512), lambda i: (0, 0))
    vshape = jax.ShapeDtypeStruct((1, 512), F32)
    return pl.pallas_call(body, name=name, grid=(T // tm,), in_specs=[row, row, vec, vec], out_specs=[row, vec, vec, vec],
                          out_shape=[jax.ShapeDtypeStruct((T, 512), F32), vshape, vshape, vshape],
                          compiler_params=_params(("arbitrary",)))(c, dy, lng, lnb)


def _conv_bwd2(z, dc, w, B, S, name):
    T = B * S

    def body(a_ref, g_ref, dc_ref, w_ref, da_ref, dg_ref, dw_ref, hpad, dpad, dwacc):
        @pl.when(pl.program_id(1) == 0)
        def _():
            dw_ref[...] = jnp.zeros_like(dw_ref)

        zeros = jnp.zeros((CONV_PAD, 128), F32)
        for ref in (hpad, dpad):
            ref[0:CONV_PAD, :] = zeros
            ref[CONV_PAD + S:2 * CONV_PAD + S, :] = zeros
        hpad[CONV_PAD:CONV_PAD + S, :] = a_ref[...] * _sigmoid(g_ref[...])
        dpad[CONV_PAD:CONV_PAD + S, :] = dc_ref[...]
        dwacc[...] = jnp.zeros_like(dwacc)

        def tile(r, carry):
            base = pl.multiple_of(r * CONV_ROWS, CONV_ROWS)
            dh = _conv_taps(dpad, w_ref, base, True)
            av, gv = a_ref[pl.ds(base, CONV_ROWS), :], g_ref[pl.ds(base, CONV_ROWS), :]
            sg = _sigmoid(gv)
            da_ref[pl.ds(base, CONV_ROWS), :] = (dh * sg).astype(BF16)
            dg_ref[pl.ds(base, CONV_ROWS), :] = (dh * av * sg * (1.0 - sg)).astype(BF16)
            dcv = dc_ref[pl.ds(base, CONV_ROWS), :]
            blk = hpad[pl.ds(base, CONV_ROWS + 2 * CONV_PAD), :]
            for k in range(CONV_WIDTH):
                prod = dcv * blk[k + 1:k + 1 + CONV_ROWS, :]
                dwacc[k] += jnp.sum(prod.reshape(CONV_ROWS // 8, 8, 128), axis=0)
            return carry

        lax.fori_loop(0, S // CONV_ROWS, tile, 0)
        for k in range(CONV_WIDTH):
            dw_ref[k] += jnp.sum(dwacc[k], axis=0, keepdims=True)

    return pl.pallas_call(
        body, name=name, grid=(4, B),
        in_specs=[pl.BlockSpec((S, 128), lambda j, b: (b, 4 * COL_CA + j)), pl.BlockSpec((S, 128), lambda j, b: (b, 4 * COL_CG + j)),
                  pl.BlockSpec((S, 128), lambda j, b: (b, j)), pl.BlockSpec((32, 1, 128), lambda j, b: (0, 0, j))],
        out_specs=[pl.BlockSpec((S, 128), lambda j, b: (b, j)), pl.BlockSpec((S, 128), lambda j, b: (b, j)),
                   pl.BlockSpec((32, 1, 128), lambda j, b: (0, 0, j))],
        out_shape=[jax.ShapeDtypeStruct((T, 512), BF16), jax.ShapeDtypeStruct((T, 512), BF16), jax.ShapeDtypeStruct((32, 1, 512), F32)],
        scratch_shapes=[pltpu.VMEM((S + 2 * CONV_PAD, 128), F32), pltpu.VMEM((S + 2 * CONV_PAD, 128), F32), pltpu.VMEM((32, 8, 128), F32)],
        compiler_params=_params(("parallel", "arbitrary")),
    )(z, z, dc, w)


def _swap16(x):
    n = x.shape[1]
    first = (lax.broadcasted_iota(jnp.int32, x.shape, 1) % 32) < 16
    return jnp.where(first, pltpu.roll(x, n - 16, 1), pltpu.roll(x, 16, 1))


def _rope(x, cos, sin):
    return x * cos + _swap16(x) * sin


def _rope_t(dy, cos, sin):
    return dy * cos + _swap16(dy * sin)


def _qk_norm(x, p):
    r = lax.rsqrt(_segmean(x * x, p) + RMS_EPS)
    return x * r, r


def _store_heads(ref, val, n):
    for h in range(n):
        ref[h] = val[:, HEAD_DIM * h:HEAD_DIM * (h + 1)].astype(ref.dtype)


def _load_heads(ref, n):
    return jnp.concatenate([ref[h] for h in range(n)], axis=1)


def _prep_fwd(z, gq, gk, rope, B, S, kv_heads, cols, name):
    tm = min(256, S)
    ns = S // tm
    kw = kv_heads * HEAD_DIM
    scale = HEAD_DIM ** -0.5
    qc, kc, vc = cols

    def body(*refs):
        if rope is None:
            q_ref, k_ref, v_ref, gq_ref, gk_ref, p_ref, qo, ko, vo = refs
        else:
            q_ref, k_ref, v_ref, gq_ref, gk_ref, p_ref, cos_ref, sin_ref, qo, ko, vo = refs
        p = p_ref[...]
        qn, _ = _qk_norm(q_ref[...], p)
        kn, _ = _qk_norm(k_ref[...], p[:kw, :kw])
        qn, kn = qn * gq_ref[...], kn * gk_ref[...]
        if rope is not None:
            cos, sin = cos_ref[...], sin_ref[...]
            qn, kn = _rope(qn, cos, sin), _rope(kn, cos[:, :kw], sin[:, :kw])
        _store_heads(qo, qn * scale, N_HEADS)
        _store_heads(ko, kn, kv_heads)
        _store_heads(vo, v_ref[...], kv_heads)

    row = lambda w, c: pl.BlockSpec((tm, w), lambda b, i: (b * ns + i, c))
    const = lambda shape: pl.BlockSpec(shape, lambda b, i: (0,) * len(shape))
    heads = lambda n: pl.BlockSpec((None, n, tm, HEAD_DIM), lambda b, i: (b, 0, i, 0))
    ins = [z, z, z, gq, gk, _seg_matrix(512)]
    specs = [row(512, qc), row(kw, kc), row(kw, vc), const((1, 512)), const((1, kw)), const((512, 512))]
    if rope is not None:
        ins += list(rope)
        specs += [pl.BlockSpec((tm, 512), lambda b, i: (i, 0))] * 2
    return pl.pallas_call(
        body, name=name, grid=(B, ns), in_specs=specs, out_specs=[heads(N_HEADS), heads(kv_heads), heads(kv_heads)],
        out_shape=[jax.ShapeDtypeStruct((B, N_HEADS, S, HEAD_DIM), BF16), jax.ShapeDtypeStruct((B, kv_heads, S, HEAD_DIM), BF16),
                   jax.ShapeDtypeStruct((B, kv_heads, S, HEAD_DIM), BF16)],
        compiler_params=_params(("parallel", "parallel")),
    )(*ins)


def _prep_bwd(z, dq, dk, dv, gq, gk, rope, B, S, kv_heads, cols, name):
    T = B * S
    tm = min(256, S)
    ns = S // tm
    kw = kv_heads * HEAD_DIM
    scale = HEAD_DIM ** -0.5
    qc, kc, _ = cols

    def body(*refs):
        if rope is None:
            q_ref, k_ref, dq_ref, dk_ref, dv_ref, gq_ref, gk_ref, p_ref, dz_ref, dgq_ref, dgk_ref = refs
        else:
            q_ref, k_ref, dq_ref, dk_ref, dv_ref, gq_ref, gk_ref, p_ref, cos_ref, sin_ref, dz_ref, dgq_ref, dgk_ref = refs

        @pl.when((pl.program_id(0) == 0) & (pl.program_id(1) == 0))
        def _():
            dgq_ref[...] = jnp.zeros_like(dgq_ref)
            dgk_ref[...] = jnp.zeros_like(dgk_ref)

        p = p_ref[...]
        dqv = _load_heads(dq_ref, N_HEADS) * scale
        dkv = _load_heads(dk_ref, kv_heads)
        if rope is not None:
            cos, sin = cos_ref[...], sin_ref[...]
            dqv, dkv = _rope_t(dqv, cos, sin), _rope_t(dkv, cos[:, :kw], sin[:, :kw])

        def through_norm(xv, dy, g, pm, dg_ref):
            xh, r = _qk_norm(xv, pm)
            dg_ref[...] += jnp.sum(dy * xh, axis=0, keepdims=True)
            dxh = dy * g
            return r * (dxh - xh * _segmean(dxh * xh, pm))

        dz_ref[:, 0:512] = through_norm(q_ref[...], dqv, gq_ref[...], p, dgq_ref).astype(BF16)
        dz_ref[:, 512:512 + kw] = through_norm(k_ref[...], dkv, gk_ref[...], p[:kw, :kw], dgk_ref).astype(BF16)
        dz_ref[:, 512 + kw:512 + 2 * kw] = _load_heads(dv_ref, kv_heads).astype(BF16)

    row = lambda w, c: pl.BlockSpec((tm, w), lambda b, i: (b * ns + i, c))
    const = lambda shape: pl.BlockSpec(shape, lambda b, i: (0,) * len(shape))
    heads = lambda n: pl.BlockSpec((None, n, tm, HEAD_DIM), lambda b, i: (b, 0, i, 0))
    ins = [z, z, dq, dk, dv, gq, gk, _seg_matrix(512)]
    specs = [row(512, qc), row(kw, kc), heads(N_HEADS), heads(kv_heads), heads(kv_heads), const((1, 512)), const((1, kw)), const((512, 512))]
    if rope is not None:
        ins += list(rope)
        specs += [pl.BlockSpec((tm, 512), lambda b, i: (i, 0))] * 2
    return pl.pallas_call(
        body, name=name, grid=(B, ns), in_specs=specs, out_specs=[row(512 + 2 * kw, 0), const((1, 512)), const((1, kw))],
        out_shape=[jax.ShapeDtypeStruct((T, 512 + 2 * kw), BF16), jax.ShapeDtypeStruct((1, 512), F32), jax.ShapeDtypeStruct((1, kw), F32)],
        compiler_params=_params(("arbitrary", "arbitrary")),
    )(*ins)


def _toeplitz(win, tq, S):
    r = pltpu.roll(jnp.broadcast_to(win, (tq, S + tq)), 0, 1, stride=1, stride_axis=0)
    return r[:, tq:tq + S]


ATTN_HEADS = 4


def _attn_fwd(q, k, v, win, name, nh=ATTN_HEADS):
    B, H, S, _ = q.shape
    shared = k.shape[1] != H
    assert not shared or H // k.shape[1] == nh
    tq = min(ATTN_TQ, S)

    def body(*refs):
        if win is None:
            q_ref, k_ref, v_ref, o_ref = refs
        else:
            q_ref, k_ref, v_ref, w_ref, o_ref = refs
        kvs = [(k_ref[...], v_ref[...])] * nh if shared else [(k_ref[h], v_ref[h]) for h in range(nh)]
        scores = []
        for h in range(nh):
            s = _nt(q_ref[h], kvs[h][0])
            if win is not None:
                s = s + _toeplitz(w_ref[h], tq, S)
            scores.append(s)
        probs = []
        for s in scores:
            p = jnp.exp(s - jnp.max(s, axis=-1, keepdims=True))
            probs.append((p.astype(BF16), jnp.sum(p, axis=-1, keepdims=True)))
        for h, (p, l) in enumerate(probs):
            o_ref[h] = _nn(p, kvs[h][1]) / l

    qs = pl.BlockSpec((None, nh, tq, HEAD_DIM), lambda b, h, i: (b, h, i, 0))
    ks = (pl.BlockSpec((None, None, S, HEAD_DIM), lambda b, h, i: (b, h, 0, 0)) if shared
          else pl.BlockSpec((None, nh, S, HEAD_DIM), lambda b, h, i: (b, h, 0, 0)))
    ins, specs = [q, k, v], [qs, ks, ks]
    if win is not None:
        ins.append(win)
        specs.append(pl.BlockSpec((nh, None, 1, S + tq), lambda b, h, i: (h, i, 0, 0)))
    return pl.pallas_call(body, name=name, grid=(B, H // nh, S // tq), in_specs=specs, out_specs=qs,
                          out_shape=jax.ShapeDtypeStruct((B, H, S, HEAD_DIM), F32),
                          compiler_params=_params(("parallel", "parallel", "parallel")))(*ins)


def _attn_bwd(q, k, v, o, do, win, name, nh=ATTN_HEADS):
    B, H, S, _ = q.shape
    hkv = k.shape[1]
    shared = hkv != H
    assert not shared or H // hkv == nh
    tq = min(ATTN_TQ, S)
    nq = S // tq

    def body(*refs):
        if win is None:
            q_ref, k_ref, v_ref, o_ref, do_ref, dq_ref, dk_ref, dv_ref = refs
        else:
            q_ref, k_ref, v_ref, o_ref, do_ref, w_ref, rev_ref, dq_ref, dk_ref, dv_ref, dw_ref = refs

        @pl.when(pl.program_id(2) == 0)
        def _():
            dk_ref[...] = jnp.zeros_like(dk_ref)
            dv_ref[...] = jnp.zeros_like(dv_ref)

        kvs = [(k_ref[...], v_ref[...])] * nh if shared else [(k_ref[h], v_ref[h]) for h in range(nh)]
        qvs, dobs, scores, dps = [], [], [], []
        for h in range(nh):
            qv, dov = q_ref[h], do_ref[h]
            dob = dov.astype(BF16)
            s = _nt(qv, kvs[h][0])
            if win is not None:
                s = s + _toeplitz(w_ref[h], tq, S)
            dp = _nt(dob, kvs[h][1]) - jnp.sum(dov * o_ref[h], axis=-1, keepdims=True)
            qvs.append(qv)
            dobs.append(dob)
            scores.append(s)
            dps.append(dp)
        pbs, dsbs = [], []
        for s, dp in zip(scores, dps):
            p = jnp.exp(s - jnp.max(s, axis=-1, keepdims=True))
            p = p * (1.0 / jnp.sum(p, axis=-1, keepdims=True))
            pbs.append(p.astype(BF16))
            dsbs.append((p * dp).astype(BF16))
        dk_acc = dv_acc = None
        for h in range(nh):
            dvh, dkh = _tn(pbs[h], dobs[h]), _tn(dsbs[h], qvs[h])
            dq_ref[h] = _nn(dsbs[h], kvs[h][0])
            if shared:
                dv_acc = dvh if dv_acc is None else dv_acc + dvh
                dk_acc = dkh if dk_acc is None else dk_acc + dkh
            else:
                dv_ref[h] += dvh
                dk_ref[h] += dkh
            if win is not None:
                rev = _nn(rev_ref[...], dsbs[h])
                wide = jnp.concatenate([rev, jnp.zeros((tq, tq), F32)], axis=1)
                dw_ref[h] = jnp.sum(pltpu.roll(wide, 0, 1, stride=1, stride_axis=0), axis=0, keepdims=True)
        if shared:
            dv_ref[...] += dv_acc
            dk_ref[...] += dk_acc

    qs = pl.BlockSpec((None, nh, tq, HEAD_DIM), lambda b, h, i: (b, h, i, 0))
    ks = (pl.BlockSpec((None, None, S, HEAD_DIM), lambda b, h, i: (b, h, 0, 0)) if shared
          else pl.BlockSpec((None, nh, S, HEAD_DIM), lambda b, h, i: (b, h, 0, 0)))
    ins, specs = [q, k, v, o, do], [qs, ks, ks, qs, qs]
    outs = [jax.ShapeDtypeStruct((B, H, S, HEAD_DIM), F32), jax.ShapeDtypeStruct((B, hkv, S, HEAD_DIM), F32), jax.ShapeDtypeStruct((B, hkv, S, HEAD_DIM), F32)]
    ospecs = [qs, ks, ks]
    if win is not None:
        ins += [win, jnp.asarray(np.eye(tq, dtype=np.float32)[::-1].copy(), BF16)]
        specs += [pl.BlockSpec((nh, None, 1, S + tq), lambda b, h, i: (h, i, 0, 0)), pl.BlockSpec((tq, tq), lambda b, h, i: (0, 0))]
        outs.append(jax.ShapeDtypeStruct((B, H, nq, 1, S + tq), F32))
        ospecs.append(pl.BlockSpec((None, nh, None, 1, S + tq), lambda b, h, i: (b, h, i, 0, 0)))
    return pl.pallas_call(body, name=name, grid=(B, H // nh, nq), in_specs=specs, out_specs=ospecs, out_shape=outs,
                          compiler_params=_params(("parallel", "parallel", "arbitrary")))(*ins)


def _pattern_count(delta):
    n = jnp.zeros(delta.shape, jnp.int32)
    for window, dil in DIL_PATTERNS:
        n = n + ((delta % dil == 0) & (jnp.abs(delta) <= window // 2)).astype(jnp.int32)
    return n


def _t5_bucket(rel):
    nb = REL_BUCKETS // 2
    max_exact = nb // 2
    ret = jnp.where(rel > 0, nb, 0)
    n = jnp.abs(rel)
    nf = jnp.maximum(n, 1).astype(F32)
    large = max_exact + (jnp.log(nf / max_exact) / math.log(REL_MAX_DIST / max_exact) * (nb - max_exact)).astype(jnp.int32)
    large = jnp.minimum(large, nb - 1)
    return ret + jnp.where(n < max_exact, n, large)


def _bias_windows(rel_bias, S):
    tq = min(ATTN_TQ, S)
    nq = S // tq
    n = nq * (S + tq)
    delta = (jnp.arange(S + tq)[None, :] - (jnp.arange(nq)[:, None] + 1) * tq).reshape(n)
    count = _pattern_count(delta)
    onehot = (_t5_bucket(delta)[None, :] == jnp.arange(REL_BUCKETS)[:, None]).astype(F32)
    extra = jnp.where(count > 0, jnp.log(jnp.maximum(count, 1).astype(F32)), MASKED).reshape(1, n)
    live = (count > 0).astype(F32).reshape(1, n)

    def body(t_ref, oh_ref, live_ref, extra_ref, o_ref):
        o_ref[...] = _nn(t_ref[...], oh_ref[...], HIGHEST) * live_ref[...] + extra_ref[...]

    val = pl.pallas_call(body, name="bias_windows", out_shape=jax.ShapeDtypeStruct((N_HEADS, n), F32),
                         compiler_params=_params())(rel_bias.T, onehot, live, extra)
    return val.reshape(N_HEADS, nq, 1, S + tq)


def _bias_fold(dwin, S, name):
    B, H, nq = dwin.shape[:3]
    tq = min(ATTN_TQ, S)
    n = nq * (S + tq)
    delta = (jnp.arange(S + tq)[None, :] - (tq - 1) - jnp.arange(nq)[:, None] * tq).reshape(n)
    onehot = (_t5_bucket(delta)[:, None] == jnp.arange(128)[None, :]).astype(F32)

    def body(d_ref, oh_ref, o_ref):
        tot = d_ref[0]
        for b in range(1, B):
            tot = tot + d_ref[b]
        o_ref[...] = _nn(tot, oh_ref[...], HIGHEST)

    out = pl.pallas_call(body, name=name, out_shape=jax.ShapeDtypeStruct((H, 128), F32), compiler_params=_params())(dwin.reshape(B, H, n), onehot)
    return out[:, :REL_BUCKETS].T


def _rope_tables(S):
    half = 16
    freqs = ROPE_THETA ** (-jnp.arange(half, dtype=F32) / half)
    t = jnp.arange(S)
    ang_r = (t // GRID_W).astype(F32)[:, None] * freqs[None, :]
    ang_c = (t % GRID_W).astype(F32)[:, None] * freqs[None, :]
    cos = jnp.concatenate([jnp.cos(ang_r)] * 2 + [jnp.cos(ang_c)] * 2, axis=1)
    sin = jnp.concatenate([-jnp.sin(ang_r), jnp.sin(ang_r), -jnp.sin(ang_c), jnp.sin(ang_c)], axis=1)
    return jnp.tile(cos, (1, N_HEADS)), jnp.tile(sin, (1, N_HEADS))


def _mix_fwd(ya, ob, yc, od, gain, B, S, name):
    T = B * S
    tm = min(256, S)
    ns = S // tm

    def body(ya_ref, ob_ref, yc_ref, od_ref, g_ref, o_ref):
        ys = [ya_ref[...], _load_heads(ob_ref, N_HEADS), yc_ref[...], _load_heads(od_ref, N_HEADS)]
        for m, y in enumerate(ys):
            r = lax.rsqrt(jnp.mean(y * y, axis=-1, keepdims=True) + RMS_EPS)
            o_ref[:, 512 * m:512 * (m + 1)] = (y * r * g_ref[:, 512 * m:512 * (m + 1)]).astype(BF16)

    row = pl.BlockSpec((tm, 512), lambda b, i: (b * ns + i, 0))
    heads = pl.BlockSpec((None, N_HEADS, tm, HEAD_DIM), lambda b, i: (b, 0, i, 0))
    return pl.pallas_call(
        body, name=name, grid=(B, ns), in_specs=[row, heads, row, heads, pl.BlockSpec((1, 2048), lambda b, i: (0, 0))],
        out_specs=pl.BlockSpec((tm, 2048), lambda b, i: (b * ns + i, 0)), out_shape=jax.ShapeDtypeStruct((T, 2048), BF16),
        compiler_params=_params(("parallel", "parallel")),
    )(ya, ob, yc, od, gain)


def _mix_bwd(ya, ob, yc, od, dycat, gain, B, S, name):
    T = B * S
    tm = min(256, S)
    ns = S // tm

    def body(ya_ref, ob_ref, yc_ref, od_ref, dy_ref, g_ref, dya_ref, dob_ref, dyc_ref, dod_ref, dg_ref):
        @pl.when((pl.program_id(0) == 0) & (pl.program_id(1) == 0))
        def _():
            dg_ref[...] = jnp.zeros_like(dg_ref)

        ys = [ya_ref[...], _load_heads(ob_ref, N_HEADS), yc_ref[...], _load_heads(od_ref, N_HEADS)]
        outs = [dya_ref, dob_ref, dyc_ref, dod_ref]
        for m, y in enumerate(ys):
            cols = slice(512 * m, 512 * (m + 1))
            r = lax.rsqrt(jnp.mean(y * y, axis=-1, keepdims=True) + RMS_EPS)
            yh = y * r
            dh = dy_ref[:, cols]
            dg_ref[:, cols] += jnp.sum(dh * yh, axis=0, keepdims=True)
            dyh = dh * g_ref[:, cols]
            dyv = r * (dyh - yh * jnp.mean(dyh * yh, axis=-1, keepdims=True))
            if m % 2 == 0:
                outs[m][...] = dyv
            else:
                _store_heads(outs[m], dyv, N_HEADS)

    row = pl.BlockSpec((tm, 512), lambda b, i: (b * ns + i, 0))
    heads = pl.BlockSpec((None, N_HEADS, tm, HEAD_DIM), lambda b, i: (b, 0, i, 0))
    vec = pl.BlockSpec((1, 2048), lambda b, i: (0, 0))
    flat = jax.ShapeDtypeStruct((T, 512), F32)
    hm = jax.ShapeDtypeStruct((B, N_HEADS, S, HEAD_DIM), F32)
    return pl.pallas_call(
        body, name=name, grid=(B, ns), in_specs=[row, heads, row, heads, pl.BlockSpec((tm, 2048), lambda b, i: (b * ns + i, 0)), vec],
        out_specs=[row, heads, row, heads, vec], out_shape=[flat, hm, flat, hm, jax.ShapeDtypeStruct((1, 2048), F32)],
        compiler_params=_params(("arbitrary", "arbitrary")),
    )(ya, ob, yc, od, dycat, gain)


def _swiglu_fwd(gate, up, name):
    J, T, n = gate.shape
    tm = min(512, T)

    def body(g_ref, u_ref, o_ref):
        g = g_ref[...].astype(F32)
        o_ref[...] = (g * _sigmoid(g) * u_ref[...].astype(F32)).astype(BF16)

    blk = pl.BlockSpec((None, tm, n), lambda j, i: (j, i, 0))
    return pl.pallas_call(body, name=name, grid=(J, T // tm), in_specs=[blk, blk], out_specs=blk,
                          out_shape=jax.ShapeDtypeStruct((J, T, n), BF16), compiler_params=_params(("parallel", "parallel")))(gate, up)


def _swiglu_bwd(gate, up, dact, name):
    J, T, n = gate.shape
    tm = min(512, T)

    def body(g_ref, u_ref, d_ref, dg_ref, du_ref):
        g, d = g_ref[...].astype(F32), d_ref[...].astype(F32)
        s = _sigmoid(g)
        dg_ref[...] = (d * u_ref[...].astype(F32) * s * (1.0 + g * (1.0 - s))).astype(BF16)
        du_ref[...] = (d * g * s).astype(BF16)

    blk = pl.BlockSpec((None, tm, n), lambda j, i: (j, i, 0))
    shape = jax.ShapeDtypeStruct((J, T, n), BF16)
    return pl.pallas_call(body, name=name, grid=(J, T // tm), in_specs=[blk, blk, blk], out_specs=[blk, blk],
                          out_shape=[shape, shape], compiler_params=_params(("parallel", "parallel")))(gate, up, dact)


def _loss_grad(y, target, name):
    T, D = y.shape
    tm = min(256, T)
    n = T // tm

    def body(y_ref, t_ref, loss_ref, dy_ref, acc):
        i = pl.program_id(0)

        @pl.when(i == 0)
        def _():
            acc[...] = jnp.zeros_like(acc)

        err = y_ref[...] - t_ref[...]
        dy_ref[...] = err * (1.0 / D)
        acc[...] += jnp.sum((err * err).reshape(tm // 8, 8, D), axis=0)

        @pl.when(i == n - 1)
        def _():
            loss_ref[...] = jnp.full((8, 128), 0.5 / D, F32) * jnp.sum(acc[...])

    row = pl.BlockSpec((tm, D), lambda i: (i, 0))
    return pl.pallas_call(body, name=name, grid=(n,), in_specs=[row, row], out_specs=[pl.BlockSpec((8, 128), lambda i: (0, 0)), row],
                          out_shape=[jax.ShapeDtypeStruct((8, 128), F32), jax.ShapeDtypeStruct((T, D), F32)],
                          scratch_shapes=[pltpu.VMEM((8, D), F32)], compiler_params=_params(("arbitrary",)))(y, target)


def _row_tile(R):
    for cand in (512, 256, 128, 64, 32, 16, 8):
        if R % cand == 0:
            return cand
    return R


def _adamw(w, m, v, stack, name, layer=None, prev=None):
    n, R, C = stack.shape
    tm = _row_tile(R)
    nb = R // tm
    off = 0 if layer is None else layer * nb
    c1 = 1.0 - ADAM_B1 ** ADAM_STEP
    c2 = 1.0 - ADAM_B2 ** ADAM_STEP

    def body(w_ref, m_ref, v_ref, s_ref, *rest):
        g_ref, d_ref, mo_ref, vo_ref = rest[-4:]
        g = s_ref[0].astype(F32)
        for k in range(1, n):
            g = g + s_ref[k].astype(F32)
        mn = ADAM_B1 * m_ref[...] + (1.0 - ADAM_B1) * g
        vn = ADAM_B2 * v_ref[...] + (1.0 - ADAM_B2) * (g * g)
        g_ref[...] = g
        mo_ref[...] = mn
        vo_ref[...] = vn
        d_ref[...] = -ADAM_LR * ((mn / c1) / (jnp.sqrt(vn / c2) + ADAM_EPS) + ADAM_WD * w_ref[...])

    blk = pl.BlockSpec((tm, C), lambda i: (i + off, 0))
    ins = [w, m, v, stack]
    specs = [blk, blk, blk, pl.BlockSpec((n, tm, C), lambda i: (0, i, 0))]
    aliases = {}
    if prev is not None:
        ins += list(prev)
        specs += [pl.BlockSpec(memory_space=pl.ANY)] * 4
        aliases = {4 + t: t for t in range(4)}
    shape = jax.ShapeDtypeStruct(w.shape, F32)
    return pl.pallas_call(body, name=name, grid=(nb,), in_specs=specs, out_specs=[blk] * 4, out_shape=[shape] * 4,
                          input_output_aliases=aliases, compiler_params=_params(("parallel",)))(*ins)


HBM = pl.BlockSpec(memory_space=pltpu.HBM)
SEM = pl.BlockSpec(memory_space=pltpu.SEMAPHORE)
EFFECT = pltpu.SideEffectType.DATAFLOW_SIDE_EFFECTING


def _spread_copies(srcs, lands, send_sems, recv_sems, local_sems, scatter, waiting):
    x, y, c = lax.axis_index("x"), lax.axis_index("y"), lax.axis_index("c")
    me = 4 * x + 2 * y + c
    remote, local = [], []
    for a, (s, l) in enumerate(zip(srcs, lands)):
        for d in range(N_DEV - 1):
            bits = d + 1
            peer = (1 - x if bits & 4 else x, 1 - y if bits & 2 else y, 1 - c if bits & 1 else c)
            pid = 4 * peer[0] + 2 * peer[1] + peer[2]
            remote.append(pltpu.make_async_remote_copy(
                src_ref=s.at[pid] if scatter else s, dst_ref=l.at[pid if waiting else me],
                send_sem=send_sems.at[a * (N_DEV - 1) + d], recv_sem=recv_sems.at[a * (N_DEV - 1) + d], device_id=peer, device_id_type=MESH_ID))
        local.append(pltpu.make_async_copy(s.at[me] if scatter else s, l.at[me], local_sems.at[a]))
    return remote, local


def _spread_start(srcs, scatter, name, after=None):
    n = len(srcs)
    land_shapes = [a.shape if scatter else (N_DEV,) + a.shape for a in srcs]
    extra = [] if after is None else [after]

    def body(*refs):
        src_refs, land_refs = refs[:n], refs[n:2 * n]
        send_sems, recv_sems, local_sems = refs[2 * n + len(extra):2 * n + len(extra) + 3]
        remote, local = _spread_copies(src_refs, land_refs, send_sems, recv_sems, local_sems, scatter, False)
        for cp in remote + local:
            cp.start()
        refs[-1][...] = jnp.zeros((8, 128), F32)

    outs = pl.pallas_call(
        body, name=name,
        out_shape=(pltpu.SemaphoreType.DMA((n * (N_DEV - 1),)), pltpu.SemaphoreType.DMA((n * (N_DEV - 1),)), pltpu.SemaphoreType.DMA((n,)),
                   *[pltpu.HBM(a.shape, a.dtype) for a in srcs], *[pltpu.HBM(shp, a.dtype) for shp, a in zip(land_shapes, srcs)],
                   jax.ShapeDtypeStruct((8, 128), F32)),
        in_specs=[HBM] * (2 * n) + [pl.BlockSpec(memory_space=pl.ANY)] * len(extra),
        out_specs=(SEM, SEM, SEM, *[HBM] * (2 * n), pl.BlockSpec(memory_space=pltpu.VMEM)),
        input_output_aliases={i: 3 + i for i in range(2 * n)},
        compiler_params=pltpu.CompilerParams(has_side_effects=EFFECT),
    )(*[pltpu.with_memory_space_constraint(a, pltpu.HBM) for a in srcs],
      *[pltpu.with_memory_space_constraint(lax.empty(shp, a.dtype), pltpu.HBM) for shp, a in zip(land_shapes, srcs)], *extra)
    return outs[:3], list(outs[3:3 + n]), list(outs[3 + n:3 + 2 * n]), outs[-1]


def _spread_wait(sems, srcs, lands, after, scatter, name):
    n = len(srcs)
    after = list(after) if isinstance(after, (list, tuple)) else [after]

    def body(*refs):
        src_refs, land_refs = refs[:n], refs[n:2 * n]
        send_sems, recv_sems, local_sems = refs[2 * n:2 * n + 3]
        remote, local = _spread_copies(src_refs, land_refs, send_sems, recv_sems, local_sems, scatter, True)
        for cp in remote:
            cp.wait_send()
            cp.wait_recv()
        for cp in local:
            cp.wait()

    outs = pl.pallas_call(
        body, name=name, out_shape=tuple(pltpu.HBM(a.shape, a.dtype) for a in srcs + lands),
        in_specs=[HBM] * (2 * n) + [SEM] * 3 + [pl.BlockSpec(memory_space=pl.ANY)] * len(after), out_specs=tuple([HBM] * (2 * n)),
        input_output_aliases={i: i for i in range(2 * n)}, compiler_params=pltpu.CompilerParams(has_side_effects=EFFECT),
    )(*srcs, *lands, *sems, *after)
    return list(outs[n:])


SMALL = ("rel_bias", "norm1_g", "sgu_w", "sgu_b", "dil_qn_g", "dil_kn_g", "conv_w", "conv_b", "conv_ln_g", "conv_ln_b",
         "gqa_qn_g", "gqa_kn_g", "mix_norm_g", "norm2_g")
LARGE = ("w_in", "w_out", "w_gate", "w_up", "w_down")


def _local_step(x, target, p, B, S, fetch, emit, mid):
    T = B * S
    rope = _rope_tables(S)
    win = _bias_windows(p["rel_bias"], S)
    tile8 = lambda g: jnp.tile(g.reshape(1, HEAD_DIM), (1, N_HEADS))
    cols_b = (COL_BQ, COL_BK, COL_BV)
    cols_d = (COL_DQ, COL_DK128, COL_DV128)
    saved = []
    for l in range(DEPTH):
        s = {"x": x}
        s["ws"] = p["sgu_w"][l].astype(BF16)
        s["bias"] = jnp.repeat(p["sgu_b"][l].T, HEAD_DIM, axis=1)
        s["h"] = _rms_fwd(x, p["norm1_g"][l], f"rms1_fwd_{l}")
        s["win"] = fetch(l, "in", s["h"])
        s["cw"] = jnp.pad(s["win"]["conv_w"], ((0, 1), (0, 0))).reshape(32, 1, 512)
        z = s["z"] = _matmul(s["h"], s["win"]["w_in"], "nn", f"in_proj_{l}", tk=D_MODEL)
        s["bias"] = s["bias"] + mid(l, z)
        s["ya"] = _sgu_fwd(z, s["ws"], s["bias"], f"sgu_fwd_{l}")
        s["c"] = _conv_fwd1(z, s["cw"], p["conv_b"][l].reshape(1, 512), B, S, f"conv_fwd_{l}")
        s["yc"] = _conv_fwd2(s["c"], p["conv_ln_g"][l].reshape(1, 512), p["conv_ln_b"][l].reshape(1, 512), f"conv_ln_fwd_{l}")
        s["gb"] = (tile8(p["dil_qn_g"][l]), tile8(p["dil_kn_g"][l]))
        s["gd"] = (tile8(p["gqa_qn_g"][l]), tile8(p["gqa_kn_g"][l])[:, :KV_WIDTH])
        s["qkv_b"] = _prep_fwd(z, *s["gb"], None, B, S, N_HEADS, cols_b, f"prep_b_fwd_{l}")
        s["qkv_d"] = _prep_fwd(z, *s["gd"], rope, B, S, KV_HEADS, cols_d, f"prep_d_fwd_{l}")
        s["ob"] = _attn_fwd(*s["qkv_b"], win, f"attn_b_fwd_{l}")
        s["od"] = _attn_fwd(*s["qkv_d"], None, f"attn_d_fwd_{l}")
        s["gmix"] = p["mix_norm_g"][l].reshape(1, 2048)
        s["ycat"] = _mix_fwd(s["ya"], s["ob"], s["yc"], s["od"], s["gmix"], B, S, f"mix_fwd_{l}")
        s["wout"] = fetch(l, "out", s["ycat"])["w_out"]
        x1 = s["x1"] = _matmul(s["ycat"], s["wout"], "nn", f"out_proj_{l}", res=x, tk=D_MODEL)
        s["h2"] = _rms_fwd(x1, p["norm2_g"][l], f"rms2_fwd_{l}")
        s["ffn"] = fetch(l, "ffn", s["h2"])
        s["gate"] = _mm_shard_out(s["h2"], s["ffn"]["w_gate"], "nn", f"ffn_gate_{l}", out_dtype=BF16)
        s["up"] = _mm_shard_out(s["h2"], s["ffn"]["w_up"], "nn", f"ffn_up_{l}", out_dtype=BF16)
        s["act"] = _swiglu_fwd(s["gate"], s["up"], f"swiglu_fwd_{l}")
        x = _mm_shard_k([(s["act"], s["ffn"]["w_down"])], "nn", f"ffn_down_{l}", res=x1, tn=1024, fold=N_DEV)
        saved.append(s)

    loss_blk, dx = _loss_grad(x, target, "loss")
    g = {k: [None] * DEPTH for k in SMALL if k != "rel_bias"}
    dwin_total = None
    for l in reversed(range(DEPTH)):
        s = saved[l]
        z, ffn = s["z"], s["ffn"]
        dact = _mm_shard_out(dx, ffn["w_down"], "nt", f"ffn_down_dx_{l}", out_dtype=BF16)
        tok = emit(l, "w_down", _mm_shard_m(s["act"], dx, f"ffn_down_dw_{l}", out_dtype=BF16, tn=512, tk=T))
        dgate, dup = _swiglu_bwd(s["gate"], s["up"], dact, f"swiglu_bwd_{l}")
        tok += emit(l, "w_gate", _mm_shard_out(s["h2"], dgate, "tn", f"ffn_gate_dw_{l}", out_dtype=BF16, tm=1024, tk=T))
        tok += emit(l, "w_up", _mm_shard_out(s["h2"], dup, "tn", f"ffn_up_dw_{l}", out_dtype=BF16, tm=1024, tk=T))
        dh2 = _mm_shard_k([(dgate, ffn["w_gate"]), (dup, ffn["w_up"])], "nt", f"ffn_up_dx_{l}", tn=512, fold=N_DEV)
        dx1, dg2 = _rms_bwd(dh2, s["x1"], p["norm2_g"][l] + tok, dx, f"rms2_bwd_{l}")
        g["norm2_g"][l] = dg2[0]
        dycat = _matmul(dx1, s["wout"], "nt", f"out_proj_dx_{l}", tk=D_MODEL)
        tok = emit(l, "w_out", _matmul(s["ycat"], dx1, "tn", f"out_proj_dw_{l}", out_dtype=BF16, tn=1024, tk=T))
        dya, dob, dyc, dod, dgm = _mix_bwd(s["ya"], s["ob"], s["yc"], s["od"], dycat, s["gmix"] + tok, B, S, f"mix_bwd_{l}")
        g["mix_norm_g"][l] = dgm[0]
        dz_a, dws, dbias = _sgu_bwd(z, dya, s["ws"], jnp.swapaxes(s["ws"], 1, 2), s["bias"], f"sgu_bwd_{l}")
        g["sgu_w"][l] = dws
        g["sgu_b"][l] = dbias.reshape(128, 8, HEAD_DIM).sum(-1).T
        dc, dlg, dlb, dcb = _conv_bwd1(s["c"], dyc, p["conv_ln_g"][l].reshape(1, 512), p["conv_ln_b"][l].reshape(1, 512), f"conv_ln_bwd_{l}")
        g["conv_ln_g"][l], g["conv_ln_b"][l], g["conv_b"][l] = dlg[0], dlb[0], dcb[0]
        dz_ca, dz_cg, dcw = _conv_bwd2(z, dc, s["cw"], B, S, f"conv_bwd_{l}")
        g["conv_w"][l] = dcw.reshape(32, 512)[:CONV_WIDTH]
        dq, dk, dv, dwin = _attn_bwd(*s["qkv_b"], s["ob"], dob, win, f"attn_b_bwd_{l}")
        dwin_total = dwin if dwin_total is None else dwin_total + dwin
        dz_b, dgq, dgk = _prep_bwd(z, dq, dk, dv, *s["gb"], None, B, S, N_HEADS, cols_b, f"prep_b_bwd_{l}")
        g["dil_qn_g"][l] = dgq.reshape(N_HEADS, HEAD_DIM).sum(0)
        g["dil_kn_g"][l] = dgk.reshape(N_HEADS, HEAD_DIM).sum(0)
        dq, dk, dv = _attn_bwd(*s["qkv_d"], s["od"], dod, None, f"attn_d_bwd_{l}")
        dz_d, dgq, dgk = _prep_bwd(z, dq, dk, dv, *s["gd"], rope, B, S, KV_HEADS, cols_d, f"prep_d_bwd_{l}")
        g["gqa_qn_g"][l] = dgq.reshape(N_HEADS, HEAD_DIM).sum(0)
        g["gqa_kn_g"][l] = dgk.reshape(KV_HEADS, HEAD_DIM).sum(0)
        dz = jnp.concatenate([dz_a, dz_b, dz_ca, dz_cg, dz_d], axis=1)
        tok = emit(l, "w_in", _matmul(s["h"], dz, "tn", f"in_proj_dw_{l}", out_dtype=BF16, tk=T))
        dh = _matmul(dz, s["win"]["w_in"], "nt", f"in_proj_dx_{l}", tn=1024, tk=IN_WIDTH)
        dx, dg1 = _rms_bwd(dh, s["x"], p["norm1_g"][l] + tok, dx1, f"rms1_bwd_{l}")
        g["norm1_g"][l] = dg1[0]

    grads = {k: jnp.stack(v) for k, v in g.items()}
    grads["rel_bias"] = _bias_fold(dwin_total, S, "bias_fold")
    return loss_blk[0, 0], dx, grads


GROUPS = {"in": ("w_in",), "out": ("w_out",), "ffn": ("w_gate", "w_up", "w_down")}
COL_SHARDED = ("w_in", "w_gate", "w_up")


def kernel(x, rel_bias, norm1_g, w_in, sgu_w, sgu_b, dil_qn_g, dil_kn_g, conv_w, conv_b, conv_ln_g, conv_ln_b, gqa_qn_g, gqa_kn_g, mix_norm_g, w_out, norm2_g, w_gate, w_up, w_down, loss_target, m_rel_bias, m_norm1_g, m_w_in, m_sgu_w, m_sgu_b, m_dil_qn_g, m_dil_kn_g, m_conv_w, m_conv_b, m_conv_ln_g, m_conv_ln_b, m_gqa_qn_g, m_gqa_kn_g, m_mix_norm_g, m_w_out, m_norm2_g, m_w_gate, m_w_up, m_w_down, v_rel_bias, v_norm1_g, v_w_in, v_sgu_w, v_sgu_b, v_dil_qn_g, v_dil_kn_g, v_conv_w, v_conv_b, v_conv_ln_g, v_conv_ln_b, v_gqa_qn_g, v_gqa_kn_g, v_mix_norm_g, v_w_out, v_norm2_g, v_w_gate, v_w_up, v_w_down):
    w = dict(rel_bias=rel_bias, norm1_g=norm1_g, w_in=w_in, sgu_w=sgu_w, sgu_b=sgu_b, dil_qn_g=dil_qn_g, dil_kn_g=dil_kn_g, conv_w=conv_w,
             conv_b=conv_b, conv_ln_g=conv_ln_g, conv_ln_b=conv_ln_b, gqa_qn_g=gqa_qn_g, gqa_kn_g=gqa_kn_g, mix_norm_g=mix_norm_g,
             w_out=w_out, norm2_g=norm2_g, w_gate=w_gate, w_up=w_up, w_down=w_down)
    m = dict(rel_bias=m_rel_bias, norm1_g=m_norm1_g, w_in=m_w_in, sgu_w=m_sgu_w, sgu_b=m_sgu_b, dil_qn_g=m_dil_qn_g, dil_kn_g=m_dil_kn_g,
             conv_w=m_conv_w, conv_b=m_conv_b, conv_ln_g=m_conv_ln_g, conv_ln_b=m_conv_ln_b, gqa_qn_g=m_gqa_qn_g, gqa_kn_g=m_gqa_kn_g,
             mix_norm_g=m_mix_norm_g, w_out=m_w_out, norm2_g=m_norm2_g, w_gate=m_w_gate, w_up=m_w_up, w_down=m_w_down)
    v = dict(rel_bias=v_rel_bias, norm1_g=v_norm1_g, w_in=v_w_in, sgu_w=v_sgu_w, sgu_b=v_sgu_b, dil_qn_g=v_dil_qn_g, dil_kn_g=v_dil_kn_g,
             conv_w=v_conv_w, conv_b=v_conv_b, conv_ln_g=v_conv_ln_g, conv_ln_b=v_conv_ln_b, gqa_qn_g=v_gqa_qn_g, gqa_kn_g=v_gqa_kn_g,
             mix_norm_g=v_mix_norm_g, w_out=v_w_out, norm2_g=v_norm2_g, w_gate=v_w_gate, w_up=v_w_up, w_down=v_w_down)
    names = list(w)
    B, S, D = x.shape
    T = B * S
    me = 4 * lax.axis_index("x") + 2 * lax.axis_index("y") + lax.axis_index("c")

    bf = {k: w[k].astype(BF16) for k in LARGE}
    spreads = {}

    def start_gather(l, group, after=None):
        srcs = [bf[k][l] for k in GROUPS[group]] + ([conv_w[l]] if group == "in" else [])
        spreads[l, group] = _spread_start(srcs, False, f"gather_{group}_{l}_start", after)
        return spreads[l, group][3][0, 0]

    tok0 = start_gather(0, "in") + start_gather(0, "out") + start_gather(0, "ffn")
    small = {k: w[k] for k in SMALL}
    small["norm1_g"] = norm1_g.at[0].add(tok0)

    def mid(l, z):
        if l > 0:
            return jnp.zeros((), F32)
        return start_gather(1, "in", z) + start_gather(1, "out", z) + start_gather(1, "ffn", z)

    def fetch(l, group, after):
        sems, srcs, lands, _ = spreads[l, group]
        got = dict(zip(GROUPS[group] + ("conv_w",), _spread_wait(sems, srcs, lands, after, False, f"gather_{group}_{l}_wait")))
        if group == "in":
            got["w_in"] = jnp.transpose(got["w_in"], (1, 0, 2)).reshape(D, IN_WIDTH)
            got["conv_w"] = jnp.transpose(got["conv_w"], (1, 0, 2)).reshape(CONV_WIDTH, 512)
        if group == "out":
            got["w_out"] = got["w_out"].reshape(D, D)
        return got

    scatters = {}

    def emit(l, k, dw):
        if k == "w_in":
            dw = jnp.transpose(dw.reshape(D, N_DEV, IN_WIDTH // N_DEV), (1, 0, 2))
        if k == "w_out":
            dw = dw.reshape(N_DEV, D // N_DEV, D)
        scatters[l, k] = _spread_start([dw], True, f"scatter_{k}_{l}_start")
        return scatters[l, k][3][0, 0]

    loss_part, dx, grads = _local_step(x.reshape(T, D), loss_target.reshape(T, D), small, B, S, fetch, emit, mid)
    loss = lax.psum(loss_part, ("x", "y", "c"))

    out_g, out_d, out_m, out_v = {}, {}, {}, {}

    def update_large(k, after):
        shp = w[k].shape
        two_d = lambda a: a.reshape(-1, shp[-1])
        res = None
        for l in reversed(range(DEPTH)):
            sems, srcs, lands, _ = scatters[l, k]
            stack = _spread_wait(sems, srcs, lands, after, True, f"scatter_{k}_{l}_wait")[0]
            res = _adamw(two_d(w[k]), two_d(m[k]), two_d(v[k]), stack.reshape(N_DEV, -1, shp[-1]), f"adamw_{k}_{l}", layer=l, prev=res)
        out_g[k], out_d[k], out_m[k], out_v[k] = [a.reshape(shp) for a in res]
        return res[0]

    flat2 = lambda a: a.reshape(-1, a.shape[-1])
    small_sems, small_srcs, small_lands, small_tok = _spread_start([flat2(grads[k]) for k in SMALL], False, "gather_small_grads_start")
    after = [dx, small_tok]
    for k in ("w_down", "w_gate", "w_up", "w_out"):
        after = update_large(k, after)
    stacks = _spread_wait(small_sems, small_srcs, small_lands, after, False, "gather_small_grads_wait")
    for k, stack in zip(SMALL, stacks):
        if k == "conv_w":
            stack = lax.dynamic_slice_in_dim(stack, me * (512 // N_DEV), 512 // N_DEV, axis=2)
        res = _adamw(flat2(w[k]), flat2(m[k]), flat2(v[k]), stack, f"adamw_{k}")
        out_g[k], out_d[k], out_m[k], out_v[k] = [a.reshape(w[k].shape) for a in res]
        after = res[0]
    update_large("w_in", after)

    return (loss, dx.reshape(B, S, D), *[out_g[k] for k in names], *[out_d[k] for k in names],
            *[out_m[k] for k in names], *[out_v[k] for k in names])
```

```python
import functools
import math

import numpy as np
import jax
import jax.numpy as jnp
from jax import lax
from jax.experimental import pallas as pl
from jax.experimental.pallas import tpu as pltpu

F32 = jnp.float32
BF16 = jnp.bfloat16
HIGHEST = lax.Precision.HIGHEST
MESH_ID = pl.DeviceIdType.MESH

D_MODEL = 2048
DEPTH = 2
HEAD_DIM = 64
GROUP_WIDTH = 512
N_HEADS = 8
KV_HEADS = 2
KV_WIDTH = 128
SGU_CHUNK = 128
CONV_WIDTH = 31
CONV_PAD = 16
GRID_W = 64
ROPE_THETA = 10000.0
REL_BUCKETS = 32
REL_MAX_DIST = 1024
DIL_PATTERNS = ((128, 1), (512, 4), (2048, 16))
FFN_HIDDEN = 5632
IN_WIDTH = 4352
RMS_EPS = 1e-6
LN_EPS = 1e-5
MASKED = -1e30
N_DEV = 8

ADAM_LR = 0.001
ADAM_B1 = 0.9
ADAM_B2 = 0.999
ADAM_EPS = 1e-08
ADAM_WD = 0.01
ADAM_STEP = 10

COL_AU, COL_AV, COL_BQ, COL_BK, COL_BV, COL_CA, COL_CG, COL_DQ = range(8)
COL_DK128, COL_DV128 = 32, 33

VMEM_LIMIT = 56 * 1024 * 1024
ATTN_TQ = 256


def _params(sem=None, vmem=VMEM_LIMIT):
    return pltpu.CompilerParams(dimension_semantics=sem, vmem_limit_bytes=vmem)


def _dot(a, b, dims, precision=None):
    return lax.dot_general(a, b, (dims, ((), ())), precision=precision, preferred_element_type=F32)


def _nn(a, b, precision=None):
    return _dot(a, b, ((1,), (0,)), precision)


def _nt(a, b):
    return _dot(a, b, ((1,), (1,)))


def _tn(a, b):
    return _dot(a, b, ((0,), (0,)))


DIMS = {"nn": ((1,), (0,)), "nt": ((1,), (1,)), "tn": ((0,), (0,))}


def _pick(n, cands):
    for c in cands:
        if n % c == 0:
            return c
    return n


def _mm_call(name, mode, pairs, specs, o_spec, out_sds, grid, acc_shape, res=None, fold=None):
    npair, nk, dims = len(pairs), grid[2], DIMS[mode]

    def body(*refs):
        ab = refs[:2 * npair]
        r_ref = refs[2 * npair] if res is not None else None
        o_ref = refs[2 * npair + (res is not None)]
        part = None
        for t in range(npair):
            for s in ([None] if fold is None else range(fold)):
                a_blk = ab[2 * t][...] if s is None else ab[2 * t][s]
                b_blk = ab[2 * t + 1][...] if s is None else ab[2 * t + 1][s]
                d = _dot(a_blk.astype(BF16), b_blk.astype(BF16), dims)
                part = d if part is None else part + d

        def finish(r):
            if r_ref is not None:
                r = r + r_ref[...]
            o_ref[...] = r.astype(o_ref.dtype)

        if nk == 1:
            finish(part)
            return
        acc, k = refs[-1], pl.program_id(2)

        @pl.when(k == 0)
        def _():
            acc[...] = part

        @pl.when(k > 0)
        def _():
            acc[...] += part

        @pl.when(k == nk - 1)
        def _():
            finish(acc[...])

    ins = [t for pair in pairs for t in pair]
    in_specs = [t for pair in specs for t in pair]
    if res is not None:
        ins.append(res)
        in_specs.append(o_spec)
    return pl.pallas_call(
        body, name=name, grid=grid, in_specs=in_specs, out_specs=o_spec, out_shape=out_sds,
        scratch_shapes=[pltpu.VMEM(acc_shape, F32)] if nk > 1 else [],
        compiler_params=_params(("parallel", "parallel", "arbitrary")),
    )(*ins)


def _matmul(a, b, mode, name, res=None, out_dtype=F32, tm=512, tn=None, tk=None):
    if mode == "nn":
        (M, K), N = a.shape, b.shape[1]
    elif mode == "nt":
        (M, K), N = a.shape, b.shape[0]
    else:
        (K, M), N = a.shape, b.shape[1]
    tm = min(tm, M)
    tn = tn or _pick(N, (2176, 2048, 1408, 1024, 512))
    tk = tk or _pick(K, (1024, 2176, 1408, 512))
    assert M % tm == 0 and N % tn == 0 and K % tk == 0, (M, N, K, tm, tn, tk)
    a_spec = pl.BlockSpec((tk, tm), lambda i, j, k: (k, i)) if mode == "tn" else pl.BlockSpec((tm, tk), lambda i, j, k: (i, k))
    b_spec = pl.BlockSpec((tn, tk), lambda i, j, k: (j, k)) if mode == "nt" else pl.BlockSpec((tk, tn), lambda i, j, k: (k, j))
    o_spec = pl.BlockSpec((tm, tn), lambda i, j, k: (i, j))
    return _mm_call(name, mode, [(a, b)], [(a_spec, b_spec)], o_spec, jax.ShapeDtypeStruct((M, N), out_dtype),
                    (M // tm, N // tn, K // tk), (tm, tn), res)


def _mm_shard_out(a, bs, mode, name, out_dtype=F32, tm=512, tk=None):
    J = bs.shape[0]
    n = bs.shape[1] if mode == "nt" else bs.shape[2]
    (K, M) = a.shape if mode == "tn" else a.shape[::-1]
    tm = min(tm, M)
    tk = tk or (K if mode != "tn" else _pick(K, (1024, 512)))
    a_spec = pl.BlockSpec((tk, tm), lambda j, i, k: (k, i)) if mode == "tn" else pl.BlockSpec((tm, tk), lambda j, i, k: (i, k))
    b_spec = pl.BlockSpec((None, n, tk), lambda j, i, k: (j, 0, k)) if mode == "nt" else pl.BlockSpec((None, tk, n), lambda j, i, k: (j, k, 0))
    o_spec = pl.BlockSpec((None, tm, n), lambda j, i, k: (j, i, 0))
    return _mm_call(name, mode, [(a, bs)], [(a_spec, b_spec)], o_spec, jax.ShapeDtypeStruct((J, M, n), out_dtype),
                    (J, M // tm, K // tk), (tm, n))


def _mm_shard_k(pairs, mode, name, res=None, out_dtype=F32, tm=512, tn=None, fold=1):
    J, M, n = pairs[0][0].shape
    N = pairs[0][1].shape[2] if mode == "nn" else pairs[0][1].shape[1]
    tm = min(tm, M)
    tn = tn or _pick(N, (2048, 1024, 512))
    a_spec = pl.BlockSpec((fold, tm, n), lambda i, j, k: (k, i, 0))
    b_spec = pl.BlockSpec((fold, n, tn), lambda i, j, k: (k, 0, j)) if mode == "nn" else pl.BlockSpec((fold, tn, n), lambda i, j, k: (k, j, 0))
    o_spec = pl.BlockSpec((tm, tn), lambda i, j, k: (i, j))
    return _mm_call(name, mode, pairs, [(a_spec, b_spec)] * len(pairs), o_spec, jax.ShapeDtypeStruct((M, N), out_dtype),
                    (M // tm, N // tn, J // fold), (tm, tn), res, fold)


def _mm_shard_m(as_, b, name, out_dtype=F32, tn=None, tk=512):
    J, K, n = as_.shape
    N = b.shape[1]
    tn = tn or _pick(N, (2048, 1024, 512))
    tk = min(tk, K)
    a_spec = pl.BlockSpec((None, tk, n), lambda j, i, k: (j, k, 0))
    b_spec = pl.BlockSpec((tk, tn), lambda j, i, k: (k, i))
    o_spec = pl.BlockSpec((None, n, tn), lambda j, i, k: (j, 0, i))
    return _mm_call(name, "tn", [(as_, b)], [(a_spec, b_spec)], o_spec, jax.ShapeDtypeStruct((J, n, N), out_dtype),
                    (J, N // tn, K // tk), (n, tn))


def _seg_matrix(width):
    return jnp.asarray(np.kron(np.eye(width // HEAD_DIM, dtype=np.float32), np.full((HEAD_DIM, HEAD_DIM), 1.0 / HEAD_DIM, np.float32)), BF16)


def _segmean(v, p):
    hi = v.astype(BF16)
    r = v - hi.astype(F32)
    mid = r.astype(BF16)
    lo = (r - mid.astype(F32)).astype(BF16)
    return _nn(hi, p) + _nn(mid, p) + _nn(lo, p)


def _gelu(x):
    c0 = math.sqrt(2.0 / math.pi)
    t = jnp.tanh(c0 * (x + 0.044715 * x * x * x))
    return 0.5 * x * (1.0 + t), t


def _gelu_grad(x, t):
    c0 = math.sqrt(2.0 / math.pi)
    return 0.5 * (1.0 + t) + 0.5 * x * (1.0 - t * t) * c0 * (1.0 + 3.0 * 0.044715 * x * x)


def _sigmoid(x):
    return 1.0 / (1.0 + jnp.exp(-x))


def _rms_fwd(x, g, name):
    T, D = x.shape
    tm = min(256, T)

    def body(x_ref, g_ref, o_ref):
        xv = x_ref[...]
        r = lax.rsqrt(jnp.mean(xv * xv, axis=-1, keepdims=True) + RMS_EPS)
        o_ref[...] = (xv * r * g_ref[...]).astype(BF16)

    return pl.pallas_call(
        body, name=name, grid=(T // tm,),
        in_specs=[pl.BlockSpec((tm, D), lambda i: (i, 0)), pl.BlockSpec((1, D), lambda i: (0, 0))],
        out_specs=pl.BlockSpec((tm, D), lambda i: (i, 0)), out_shape=jax.ShapeDtypeStruct((T, D), BF16),
        compiler_params=_params(("parallel",)),
    )(x, g.reshape(1, D))


def _rms_bwd(dh, x, g, dres, name):
    T, D = x.shape
    tm = min(256, T)

    def body(dh_ref, x_ref, g_ref, dres_ref, dx_ref, dg_ref):
        @pl.when(pl.program_id(0) == 0)
        def _():
            dg_ref[...] = jnp.zeros_like(dg_ref)

        xv, dhv = x_ref[...], dh_ref[...]
        r = lax.rsqrt(jnp.mean(xv * xv, axis=-1, keepdims=True) + RMS_EPS)
        y = xv * r
        dy = dhv * g_ref[...]
        dx_ref[...] = dres_ref[...] + r * (dy - y * jnp.mean(dy * y, axis=-1, keepdims=True))
        dg_ref[...] += jnp.sum(dhv * y, axis=0, keepdims=True)

    row = pl.BlockSpec((tm, D), lambda i: (i, 0))
    vec = pl.BlockSpec((1, D), lambda i: (0, 0))
    return pl.pallas_call(
        body, name=name, grid=(T // tm,), in_specs=[row, row, vec, row], out_specs=[row, vec],
        out_shape=[jax.ShapeDtypeStruct((T, D), F32), jax.ShapeDtypeStruct((1, D), F32)],
        compiler_params=_params(("arbitrary",)),
    )(dh, x, g.reshape(1, D), dres)


def _sgu_core(zu, zv, ws_ref, bias, p):
    ug, tu = _gelu(zu)
    vg, tv = _gelu(zv)
    xc = vg - _segmean(vg, p)
    rs = lax.rsqrt(_segmean(xc * xc, p) + LN_EPS)
    vn = xc * rs
    vnb = vn.astype(BF16)
    low = lax.broadcasted_iota(jnp.int32, (SGU_CHUNK, 128), 1) < HEAD_DIM
    parts = []
    for j in range(4):
        vp = vnb[:, 128 * j:128 * (j + 1)]
        parts.append(jnp.where(low, _nn(ws_ref[2 * j], vp), _nn(ws_ref[2 * j + 1], vp)))
    mixed = jnp.concatenate(parts, axis=1) + bias
    return ug, tu, tv, rs, vn, vnb, mixed, low


def _sgu_fwd(z, ws, bias, name):
    T = z.shape[0]

    def body(zu_ref, zv_ref, ws_ref, b_ref, p_ref, y_ref):
        ug, _, _, _, _, _, mixed, _ = _sgu_core(zu_ref[...], zv_ref[...], ws_ref, b_ref[...], p_ref[...])
        y_ref[...] = ug * mixed

    full = lambda shape: pl.BlockSpec(shape, lambda i: (0,) * len(shape))
    return pl.pallas_call(
        body, name=name, grid=(T // SGU_CHUNK,),
        in_specs=[pl.BlockSpec((SGU_CHUNK, 512), lambda i: (i, COL_AU)), pl.BlockSpec((SGU_CHUNK, 512), lambda i: (i, COL_AV)),
                  full((8, 128, 128)), full((128, 512)), full((512, 512))],
        out_specs=pl.BlockSpec((SGU_CHUNK, 512), lambda i: (i, 0)), out_shape=jax.ShapeDtypeStruct((T, 512), F32),
        compiler_params=_params(("parallel",)),
    )(z, z, ws, bias, _seg_matrix(512))


def _sgu_bwd(z, dy, ws, ws_t, bias, name):
    T = z.shape[0]

    def body(zu_ref, zv_ref, dy_ref, ws_ref, wst_ref, b_ref, p_ref, dz_ref, dws_ref, db_ref):
        @pl.when(pl.program_id(0) == 0)
        def _():
            dws_ref[...] = jnp.zeros_like(dws_ref)
            db_ref[...] = jnp.zeros_like(db_ref)

        zu, zv, p = zu_ref[...], zv_ref[...], p_ref[...]
        ug, tu, tv, rs, vn, vnb, mixed, low = _sgu_core(zu, zv, ws_ref, b_ref[...], p)
        dyv = dy_ref[...]
        dmixed = dyv * ug
        db_ref[...] += dmixed
        dmb = dmixed.astype(BF16)
        zero = jnp.zeros((SGU_CHUNK, 128), BF16)
        parts = []
        for j in range(4):
            dmp, vp = dmb[:, 128 * j:128 * (j + 1)], vnb[:, 128 * j:128 * (j + 1)]
            dws_ref[2 * j] += _nt(jnp.where(low, dmp, zero), vp)
            dws_ref[2 * j + 1] += _nt(jnp.where(low, zero, dmp), vp)
            parts.append(jnp.where(low, _nn(wst_ref[2 * j], dmp), _nn(wst_ref[2 * j + 1], dmp)))
        dvn = jnp.concatenate(parts, axis=1)
        dvg = rs * (dvn - _segmean(dvn, p) - vn * _segmean(dvn * vn, p))
        dz_ref[:, 0:512] = (dyv * mixed * _gelu_grad(zu, tu)).astype(BF16)
        dz_ref[:, 512:1024] = (dvg * _gelu_grad(zv, tv)).astype(BF16)

    full = lambda shape: pl.BlockSpec(shape, lambda i: (0,) * len(shape))
    return pl.pallas_call(
        body, name=name, grid=(T // SGU_CHUNK,),
        in_specs=[pl.BlockSpec((SGU_CHUNK, 512), lambda i: (i, COL_AU)), pl.BlockSpec((SGU_CHUNK, 512), lambda i: (i, COL_AV)),
                  pl.BlockSpec((SGU_CHUNK, 512), lambda i: (i, 0)), full((8, 128, 128)), full((8, 128, 128)), full((128, 512)), full((512, 512))],
        out_specs=[pl.BlockSpec((SGU_CHUNK, 1024), lambda i: (i, 0)), full((8, 128, 128)), full((128, 512))],
        out_shape=[jax.ShapeDtypeStruct((T, 1024), BF16), jax.ShapeDtypeStruct((8, 128, 128), F32), jax.ShapeDtypeStruct((128, 512), F32)],
        compiler_params=_params(("arbitrary",)),
    )(z, z, dy, ws, ws_t, bias, _seg_matrix(512))


CONV_ROWS = 256


def _conv_taps(pad_ref, w_ref, base, flip):
    blk = pad_ref[pl.ds(base, CONV_ROWS + 2 * CONV_PAD), :]
    acc = jnp.zeros((CONV_ROWS, blk.shape[1]), F32)
    for k in range(CONV_WIDTH):
        wk = w_ref[CONV_WIDTH - 1 - k if flip else k]
        acc = acc + wk * blk[k + 1:k + 1 + CONV_ROWS, :]
    return acc


def _conv_fwd1(z, w, cb, B, S, name):
    T = B * S
    rows = min(CONV_ROWS, S)
    assert rows == CONV_ROWS

    def body(a_ref, g_ref, w_ref, cb_ref, c_ref, pad):
        pad[0:CONV_PAD, :] = jnp.zeros((CONV_PAD, 128), F32)
        pad[CONV_PAD + S:2 * CONV_PAD + S, :] = jnp.zeros((CONV_PAD, 128), F32)
        pad[CONV_PAD:CONV_PAD + S, :] = a_ref[...] * _sigmoid(g_ref[...])

        def tile(r, carry):
            base = pl.multiple_of(r * CONV_ROWS, CONV_ROWS)
            c_ref[pl.ds(base, CONV_ROWS), :] = _conv_taps(pad, w_ref, base, False) + cb_ref[...]
            return carry

        lax.fori_loop(0, S // CONV_ROWS, tile, 0)

    return pl.pallas_call(
        body, name=name, grid=(4, B),
        in_specs=[pl.BlockSpec((S, 128), lambda j, b: (b, 4 * COL_CA + j)), pl.BlockSpec((S, 128), lambda j, b: (b, 4 * COL_CG + j)),
                  pl.BlockSpec((32, 1, 128), lambda j, b: (0, 0, j)), pl.BlockSpec((1, 128), lambda j, b: (0, j))],
        out_specs=pl.BlockSpec((S, 128), lambda j, b: (b, j)), out_shape=jax.ShapeDtypeStruct((T, 512), F32),
        scratch_shapes=[pltpu.VMEM((S + 2 * CONV_PAD, 128), F32)], compiler_params=_params(("parallel", "parallel")),
    )(z, z, w, cb)


def _ln_rows(c):
    mu = jnp.mean(c, axis=-1, keepdims=True)
    xc = c - mu
    rs = lax.rsqrt(jnp.mean(xc * xc, axis=-1, keepdims=True) + LN_EPS)
    return xc * rs, rs


def _conv_fwd2(c, lng, lnb, name):
    T = c.shape[0]
    tm = min(512, T)

    def body(c_ref, g_ref, b_ref, y_ref):
        n, _ = _ln_rows(c_ref[...])
        t = n * g_ref[...] + b_ref[...]
        y_ref[...] = t * _sigmoid(t)

    row = pl.BlockSpec((tm, 512), lambda i: (i, 0))
    vec = pl.BlockSpec((1, 512), lambda i: (0, 0))
    return pl.pallas_call(body, name=name, grid=(T // tm,), in_specs=[row, vec, vec], out_specs=row,
                          out_shape=jax.ShapeDtypeStruct((T, 512), F32), compiler_params=_params(("parallel",)))(c, lng, lnb)


def _conv_bwd1(c, dy, lng, lnb, name):
    T = c.shape[0]
    tm = min(512, T)

    def body(c_ref, dy_ref, g_ref, b_ref, dc_ref, dg_ref, db_ref, dcb_ref):
        @pl.when(pl.program_id(0) == 0)
        def _():
            dg_ref[...] = jnp.zeros_like(dg_ref)
            db_ref[...] = jnp.zeros_like(db_ref)
            dcb_ref[...] = jnp.zeros_like(dcb_ref)

        n, rs = _ln_rows(c_ref[...])
        t = n * g_ref[...] + b_ref[...]
        s = _sigmoid(t)
        dt = dy_ref[...] * s * (1.0 + t * (1.0 - s))
        dg_ref[...] += jnp.sum(dt * n, axis=0, keepdims=True)
        db_ref[...] += jnp.sum(dt, axis=0, keepdims=True)
        dn = dt * g_ref[...]
        dc = rs * (dn - jnp.mean(dn, axis=-1, keepdims=True) - n * jnp.mean(dn * n, axis=-1, keepdims=True))
        dc_ref[...] = dc
        dcb_ref[...] += jnp.sum(dc, axis=0, keepdims=True)

    row = pl.BlockSpec((tm, 512), lambda i: (i, 0))
    vec = pl.BlockSpec((1, 512), lambda i: (0, 0))
    vshape = jax.ShapeDtypeStruct((1, 512), F32)
    return pl.pallas_call(body, name=name, grid=(T // tm,), in_specs=[row, row, vec, vec], out_specs=[row, vec, vec, vec],
                          out_shape=[jax.ShapeDtypeStruct((T, 512), F32), vshape, vshape, vshape],
                          compiler_params=_params(("arbitrary",)))(c, dy, lng, lnb)


def _conv_bwd2(z, dc, w, B, S, name):
    T = B * S

    def body(a_ref, g_ref, dc_ref, w_ref, da_ref, dg_ref, dw_ref, hpad, dpad, dwacc):
        @pl.when(pl.program_id(1) == 0)
        def _():
            dw_ref[...] = jnp.zeros_like(dw_ref)

        zeros = jnp.zeros((CONV_PAD, 128), F32)
        for ref in (hpad, dpad):
            ref[0:CONV_PAD, :] = zeros
            ref[CONV_PAD + S:2 * CONV_PAD + S, :] = zeros
        hpad[CONV_PAD:CONV_PAD + S, :] = a_ref[...] * _sigmoid(g_ref[...])
        dpad[CONV_PAD:CONV_PAD + S, :] = dc_ref[...]
        dwacc[...] = jnp.zeros_like(dwacc)

        def tile(r, carry):
            base = pl.multiple_of(r * CONV_ROWS, CONV_ROWS)
            dh = _conv_taps(dpad, w_ref, base, True)
            av, gv = a_ref[pl.ds(base, CONV_ROWS), :], g_ref[pl.ds(base, CONV_ROWS), :]
            sg = _sigmoid(gv)
            da_ref[pl.ds(base, CONV_ROWS), :] = (dh * sg).astype(BF16)
            dg_ref[pl.ds(base, CONV_ROWS), :] = (dh * av * sg * (1.0 - sg)).astype(BF16)
            dcv = dc_ref[pl.ds(base, CONV_ROWS), :]
            blk = hpad[pl.ds(base, CONV_ROWS + 2 * CONV_PAD), :]
            for k in range(CONV_WIDTH):
                prod = dcv * blk[k + 1:k + 1 + CONV_ROWS, :]
                dwacc[k] += jnp.sum(prod.reshape(CONV_ROWS // 8, 8, 128), axis=0)
            return carry

        lax.fori_loop(0, S // CONV_ROWS, tile, 0)
        for k in range(CONV_WIDTH):
            dw_ref[k] += jnp.sum(dwacc[k], axis=0, keepdims=True)

    return pl.pallas_call(
        body, name=name, grid=(4, B),
        in_specs=[pl.BlockSpec((S, 128), lambda j, b: (b, 4 * COL_CA + j)), pl.BlockSpec((S, 128), lambda j, b: (b, 4 * COL_CG + j)),
                  pl.BlockSpec((S, 128), lambda j, b: (b, j)), pl.BlockSpec((32, 1, 128), lambda j, b: (0, 0, j))],
        out_specs=[pl.BlockSpec((S, 128), lambda j, b: (b, j)), pl.BlockSpec((S, 128), lambda j, b: (b, j)),
                   pl.BlockSpec((32, 1, 128), lambda j, b: (0, 0, j))],
        out_shape=[jax.ShapeDtypeStruct((T, 512), BF16), jax.ShapeDtypeStruct((T, 512), BF16), jax.ShapeDtypeStruct((32, 1, 512), F32)],
        scratch_shapes=[pltpu.VMEM((S + 2 * CONV_PAD, 128), F32), pltpu.VMEM((S + 2 * CONV_PAD, 128), F32), pltpu.VMEM((32, 8, 128), F32)],
        compiler_params=_params(("parallel", "arbitrary")),
    )(z, z, dc, w)


def _swap16(x):
    n = x.shape[1]
    first = (lax.broadcasted_iota(jnp.int32, x.shape, 1) % 32) < 16
    return jnp.where(first, pltpu.roll(x, n - 16, 1), pltpu.roll(x, 16, 1))


def _rope(x, cos, sin):
    return x * cos + _swap16(x) * sin


def _rope_t(dy, cos, sin):
    return dy * cos + _swap16(dy * sin)


def _qk_norm(x, p):
    r = lax.rsqrt(_segmean(x * x, p) + RMS_EPS)
    return x * r, r


def _store_heads(ref, val, n):
    for h in range(n):
        ref[h] = val[:, HEAD_DIM * h:HEAD_DIM * (h + 1)].astype(ref.dtype)


def _load_heads(ref, n):
    return jnp.concatenate([ref[h] for h in range(n)], axis=1)


def _prep_fwd(z, gq, gk, rope, B, S, kv_heads, cols, name):
    tm = min(256, S)
    ns = S // tm
    kw = kv_heads * HEAD_DIM
    scale = HEAD_DIM ** -0.5
    qc, kc, vc = cols

    def body(*refs):
        if rope is None:
            q_ref, k_ref, v_ref, gq_ref, gk_ref, p_ref, qo, ko, vo = refs
        else:
            q_ref, k_ref, v_ref, gq_ref, gk_ref, p_ref, cos_ref, sin_ref, qo, ko, vo = refs
        p = p_ref[...]
        qn, _ = _qk_norm(q_ref[...], p)
        kn, _ = _qk_norm(k_ref[...], p[:kw, :kw])
        qn, kn = qn * gq_ref[...], kn * gk_ref[...]
        if rope is not None:
            cos, sin = cos_ref[...], sin_ref[...]
            qn, kn = _rope(qn, cos, sin), _rope(kn, cos[:, :kw], sin[:, :kw])
        _store_heads(qo, qn * scale, N_HEADS)
        _store_heads(ko, kn, kv_heads)
        _store_heads(vo, v_ref[...], kv_heads)

    row = lambda w, c: pl.BlockSpec((tm, w), lambda b, i: (b * ns + i, c))
    const = lambda shape: pl.BlockSpec(shape, lambda b, i: (0,) * len(shape))
    heads = lambda n: pl.BlockSpec((None, n, tm, HEAD_DIM), lambda b, i: (b, 0, i, 0))
    ins = [z, z, z, gq, gk, _seg_matrix(512)]
    specs = [row(512, qc), row(kw, kc), row(kw, vc), const((1, 512)), const((1, kw)), const((512, 512))]
    if rope is not None:
        ins += list(rope)
        specs += [pl.BlockSpec((tm, 512), lambda b, i: (i, 0))] * 2
    return pl.pallas_call(
        body, name=name, grid=(B, ns), in_specs=specs, out_specs=[heads(N_HEADS), heads(kv_heads), heads(kv_heads)],
        out_shape=[jax.ShapeDtypeStruct((B, N_HEADS, S, HEAD_DIM), BF16), jax.ShapeDtypeStruct((B, kv_heads, S, HEAD_DIM), BF16),
                   jax.ShapeDtypeStruct((B, kv_heads, S, HEAD_DIM), BF16)],
        compiler_params=_params(("parallel", "parallel")),
    )(*ins)


def _prep_bwd(z, dq, dk, dv, gq, gk, rope, B, S, kv_heads, cols, name):
    T = B * S
    tm = min(256, S)
    ns = S // tm
    kw = kv_heads * HEAD_DIM
    scale = HEAD_DIM ** -0.5
    qc, kc, _ = cols

    def body(*refs):
        if rope is None:
            q_ref, k_ref, dq_ref, dk_ref, dv_ref, gq_ref, gk_ref, p_ref, dz_ref, dgq_ref, dgk_ref = refs
        else:
            q_ref, k_ref, dq_ref, dk_ref, dv_ref, gq_ref, gk_ref, p_ref, cos_ref, sin_ref, dz_ref, dgq_ref, dgk_ref = refs

        @pl.when((pl.program_id(0) == 0) & (pl.program_id(1) == 0))
        def _():
            dgq_ref[...] = jnp.zeros_like(dgq_ref)
            dgk_ref[...] = jnp.zeros_like(dgk_ref)

        p = p_ref[...]
        dqv = _load_heads(dq_ref, N_HEADS) * scale
        dkv = _load_heads(dk_ref, kv_heads)
        if rope is not None:
            cos, sin = cos_ref[...], sin_ref[...]
            dqv, dkv = _rope_t(dqv, cos, sin), _rope_t(dkv, cos[:, :kw], sin[:, :kw])

        def through_norm(xv, dy, g, pm, dg_ref):
            xh, r = _qk_norm(xv, pm)
            dg_ref[...] += jnp.sum(dy * xh, axis=0, keepdims=True)
            dxh = dy * g
            return r * (dxh - xh * _segmean(dxh * xh, pm))

        dz_ref[:, 0:512] = through_norm(q_ref[...], dqv, gq_ref[...], p, dgq_ref).astype(BF16)
        dz_ref[:, 512:512 + kw] = through_norm(k_ref[...], dkv, gk_ref[...], p[:kw, :kw], dgk_ref).astype(BF16)
        dz_ref[:, 512 + kw:512 + 2 * kw] = _load_heads(dv_ref, kv_heads).astype(BF16)

    row = lambda w, c: pl.BlockSpec((tm, w), lambda b, i: (b * ns + i, c))
    const = lambda shape: pl.BlockSpec(shape, lambda b, i: (0,) * len(shape))
    heads = lambda n: pl.BlockSpec((None, n, tm, HEAD_DIM), lambda b, i: (b, 0, i, 0))
    ins = [z, z, dq, dk, dv, gq, gk, _seg_matrix(512)]
    specs = [row(512, qc), row(kw, kc), heads(N_HEADS), heads(kv_heads), heads(kv_heads), const((1, 512)), const((1, kw)), const((512, 512))]
    if rope is not None:
        ins += list(rope)
        specs += [pl.BlockSpec((tm, 512), lambda b, i: (i, 0))] * 2
    return pl.pallas_call(
        body, name=name, grid=(B, ns), in_specs=specs, out_specs=[row(512 + 2 * kw, 0), const((1, 512)), const((1, kw))],
        out_shape=[jax.ShapeDtypeStruct((T, 512 + 2 * kw), BF16), jax.ShapeDtypeStruct((1, 512), F32), jax.ShapeDtypeStruct((1, kw), F32)],
        compiler_params=_params(("arbitrary", "arbitrary")),
    )(*ins)


def _toeplitz(win, tq, S):
    r = pltpu.roll(jnp.broadcast_to(win, (tq, S + tq)), 0, 1, stride=1, stride_axis=0)
    return r[:, tq:tq + S]


ATTN_HEADS = 4


def _attn_fwd(q, k, v, win, name, nh=ATTN_HEADS):
    B, H, S, _ = q.shape
    shared = k.shape[1] != H
    assert not shared or H // k.shape[1] == nh
    tq = min(ATTN_TQ, S)

    def body(*refs):
        if win is None:
            q_ref, k_ref, v_ref, o_ref = refs
        else:
            q_ref, k_ref, v_ref, w_ref, o_ref = refs
        kvs = [(k_ref[...], v_ref[...])] * nh if shared else [(k_ref[h], v_ref[h]) for h in range(nh)]
        scores = []
        for h in range(nh):
            s = _nt(q_ref[h], kvs[h][0])
            if win is not None:
                s = s + _toeplitz(w_ref[h], tq, S)
            scores.append(s)
        probs = []
        for s in scores:
            p = jnp.exp(s - jnp.max(s, axis=-1, keepdims=True))
            probs.append((p.astype(BF16), jnp.sum(p, axis=-1, keepdims=True)))
        for h, (p, l) in enumerate(probs):
            o_ref[h] = _nn(p, kvs[h][1]) / l

    qs = pl.BlockSpec((None, nh, tq, HEAD_DIM), lambda b, h, i: (b, h, i, 0))
    ks = (pl.BlockSpec((None, None, S, HEAD_DIM), lambda b, h, i: (b, h, 0, 0)) if shared
          else pl.BlockSpec((None, nh, S, HEAD_DIM), lambda b, h, i: (b, h, 0, 0)))
    ins, specs = [q, k, v], [qs, ks, ks]
    if win is not None:
        ins.append(win)
        specs.append(pl.BlockSpec((nh, None, 1, S + tq), lambda b, h, i: (h, i, 0, 0)))
    return pl.pallas_call(body, name=name, grid=(B, H // nh, S // tq), in_specs=specs, out_specs=qs,
                          out_shape=jax.ShapeDtypeStruct((B, H, S, HEAD_DIM), F32),
                          compiler_params=_params(("parallel", "parallel", "parallel")))(*ins)


def _attn_bwd(q, k, v, o, do, win, name, nh=ATTN_HEADS):
    B, H, S, _ = q.shape
    hkv = k.shape[1]
    shared = hkv != H
    assert not shared or H // hkv == nh
    tq = min(ATTN_TQ, S)
    nq = S // tq

    def body(*refs):
        if win is None:
            q_ref, k_ref, v_ref, o_ref, do_ref, dq_ref, dk_ref, dv_ref = refs
        else:
            q_ref, k_ref, v_ref, o_ref, do_ref, w_ref, rev_ref, dq_ref, dk_ref, dv_ref, dw_ref = refs

        @pl.when(pl.program_id(2) == 0)
        def _():
            dk_ref[...] = jnp.zeros_like(dk_ref)
            dv_ref[...] = jnp.zeros_like(dv_ref)

        kvs = [(k_ref[...], v_ref[...])] * nh if shared else [(k_ref[h], v_ref[h]) for h in range(nh)]
        qvs, dobs, scores, dps = [], [], [], []
        for h in range(nh):
            qv, dov = q_ref[h], do_ref[h]
            dob = dov.astype(BF16)
            s = _nt(qv, kvs[h][0])
            if win is not None:
                s = s + _toeplitz(w_ref[h], tq, S)
            dp = _nt(dob, kvs[h][1]) - jnp.sum(dov * o_ref[h], axis=-1, keepdims=True)
            qvs.append(qv)
            dobs.append(dob)
            scores.append(s)
            dps.append(dp)
        pbs, dsbs = [], []
        for s, dp in zip(scores, dps):
            p = jnp.exp(s - jnp.max(s, axis=-1, keepdims=True))
            p = p * (1.0 / jnp.sum(p, axis=-1, keepdims=True))
            pbs.append(p.astype(BF16))
            dsbs.append((p * dp).astype(BF16))
        dk_acc = dv_acc = None
        for h in range(nh):
            dvh, dkh = _tn(pbs[h], dobs[h]), _tn(dsbs[h], qvs[h])
            dq_ref[h] = _nn(dsbs[h], kvs[h][0])
            if shared:
                dv_acc = dvh if dv_acc is None else dv_acc + dvh
                dk_acc = dkh if dk_acc is None else dk_acc + dkh
            else:
                dv_ref[h] += dvh
                dk_ref[h] += dkh
            if win is not None:
                rev = _nn(rev_ref[...], dsbs[h])
                wide = jnp.concatenate([rev, jnp.zeros((tq, tq), F32)], axis=1)
                dw_ref[h] = jnp.sum(pltpu.roll(wide, 0, 1, stride=1, stride_axis=0), axis=0, keepdims=True)
        if shared:
            dv_ref[...] += dv_acc
            dk_ref[...] += dk_acc

    qs = pl.BlockSpec((None, nh, tq, HEAD_DIM), lambda b, h, i: (b, h, i, 0))
    ks = (pl.BlockSpec((None, None, S, HEAD_DIM), lambda b, h, i: (b, h, 0, 0)) if shared
          else pl.BlockSpec((None, nh, S, HEAD_DIM), lambda b, h, i: (b, h, 0, 0)))
    ins, specs = [q, k, v, o, do], [qs, ks, ks, qs, qs]
    outs = [jax.ShapeDtypeStruct((B, H, S, HEAD_DIM), F32), jax.ShapeDtypeStruct((B, hkv, S, HEAD_DIM), F32), jax.ShapeDtypeStruct((B, hkv, S, HEAD_DIM), F32)]
    ospecs = [qs, ks, ks]
    if win is not None:
        ins += [win, jnp.asarray(np.eye(tq, dtype=np.float32)[::-1].copy(), BF16)]
        specs += [pl.BlockSpec((nh, None, 1, S + tq), lambda b, h, i: (h, i, 0, 0)), pl.BlockSpec((tq, tq), lambda b, h, i: (0, 0))]
        outs.append(jax.ShapeDtypeStruct((B, H, nq, 1, S + tq), F32))
        ospecs.append(pl.BlockSpec((None, nh, None, 1, S + tq), lambda b, h, i: (b, h, i, 0, 0)))
    return pl.pallas_call(body, name=name, grid=(B, H // nh, nq), in_specs=specs, out_specs=ospecs, out_shape=outs,
                          compiler_params=_params(("parallel", "parallel", "arbitrary")))(*ins)


def _pattern_count(delta):
    n = jnp.zeros(delta.shape, jnp.int32)
    for window, dil in DIL_PATTERNS:
        n = n + ((delta % dil == 0) & (jnp.abs(delta) <= window // 2)).astype(jnp.int32)
    return n


def _t5_bucket(rel):
    nb = REL_BUCKETS // 2
    max_exact = nb // 2
    ret = jnp.where(rel > 0, nb, 0)
    n = jnp.abs(rel)
    nf = jnp.maximum(n, 1).astype(F32)
    large = max_exact + (jnp.log(nf / max_exact) / math.log(REL_MAX_DIST / max_exact) * (nb - max_exact)).astype(jnp.int32)
    large = jnp.minimum(large, nb - 1)
    return ret + jnp.where(n < max_exact, n, large)


def _bias_windows(rel_bias, S):
    tq = min(ATTN_TQ, S)
    nq = S // tq
    n = nq * (S + tq)
    delta = (jnp.arange(S + tq)[None, :] - (jnp.arange(nq)[:, None] + 1) * tq).reshape(n)
    count = _pattern_count(delta)
    onehot = (_t5_bucket(delta)[None, :] == jnp.arange(REL_BUCKETS)[:, None]).astype(F32)
    extra = jnp.where(count > 0, jnp.log(jnp.maximum(count, 1).astype(F32)), MASKED).reshape(1, n)
    live = (count > 0).astype(F32).reshape(1, n)

    def body(t_ref, oh_ref, live_ref, extra_ref, o_ref):
        o_ref[...] = _nn(t_ref[...], oh_ref[...], HIGHEST) * live_ref[...] + extra_ref[...]

    val = pl.pallas_call(body, name="bias_windows", out_shape=jax.ShapeDtypeStruct((N_HEADS, n), F32),
                         compiler_params=_params())(rel_bias.T, onehot, live, extra)
    return val.reshape(N_HEADS, nq, 1, S + tq)


def _bias_fold(dwin, S, name):
    B, H, nq = dwin.shape[:3]
    tq = min(ATTN_TQ, S)
    n = nq * (S + tq)
    delta = (jnp.arange(S + tq)[None, :] - (tq - 1) - jnp.arange(nq)[:, None] * tq).reshape(n)
    onehot = (_t5_bucket(delta)[:, None] == jnp.arange(128)[None, :]).astype(F32)

    def body(d_ref, oh_ref, o_ref):
        tot = d_ref[0]
        for b in range(1, B):
            tot = tot + d_ref[b]
        o_ref[...] = _nn(tot, oh_ref[...], HIGHEST)

    out = pl.pallas_call(body, name=name, out_shape=jax.ShapeDtypeStruct((H, 128), F32), compiler_params=_params())(dwin.reshape(B, H, n), onehot)
    return out[:, :REL_BUCKETS].T


def _rope_tables(S):
    half = 16
    freqs = ROPE_THETA ** (-jnp.arange(half, dtype=F32) / half)
    t = jnp.arange(S)
    ang_r = (t // GRID_W).astype(F32)[:, None] * freqs[None, :]
    ang_c = (t % GRID_W).astype(F32)[:, None] * freqs[None, :]
    cos = jnp.concatenate([jnp.cos(ang_r)] * 2 + [jnp.cos(ang_c)] * 2, axis=1)
    sin = jnp.concatenate([-jnp.sin(ang_r), jnp.sin(ang_r), -jnp.sin(ang_c), jnp.sin(ang_c)], axis=1)
    return jnp.tile(cos, (1, N_HEADS)), jnp.tile(sin, (1, N_HEADS))


def _mix_fwd(ya, ob, yc, od, gain, B, S, name):
    T = B * S
    tm = min(256, S)
    ns = S // tm

    def body(ya_ref, ob_ref, yc_ref, od_ref, g_ref, o_ref):
        ys = [ya_ref[...], _load_heads(ob_ref, N_HEADS), yc_ref[...], _load_heads(od_ref, N_HEADS)]
        for m, y in enumerate(ys):
            r = lax.rsqrt(jnp.mean(y * y, axis=-1, keepdims=True) + RMS_EPS)
            o_ref[:, 512 * m:512 * (m + 1)] = (y * r * g_ref[:, 512 * m:512 * (m + 1)]).astype(BF16)

    row = pl.BlockSpec((tm, 512), lambda b, i: (b * ns + i, 0))
    heads = pl.BlockSpec((None, N_HEADS, tm, HEAD_DIM), lambda b, i: (b, 0, i, 0))
    return pl.pallas_call(
        body, name=name, grid=(B, ns), in_specs=[row, heads, row, heads, pl.BlockSpec((1, 2048), lambda b, i: (0, 0))],
        out_specs=pl.BlockSpec((tm, 2048), lambda b, i: (b * ns + i, 0)), out_shape=jax.ShapeDtypeStruct((T, 2048), BF16),
        compiler_params=_params(("parallel", "parallel")),
    )(ya, ob, yc, od, gain)


def _mix_bwd(ya, ob, yc, od, dycat, gain, B, S, name):
    T = B * S
    tm = min(256, S)
    ns = S // tm

    def body(ya_ref, ob_ref, yc_ref, od_ref, dy_ref, g_ref, dya_ref, dob_ref, dyc_ref, dod_ref, dg_ref):
        @pl.when((pl.program_id(0) == 0) & (pl.program_id(1) == 0))
        def _():
            dg_ref[...] = jnp.zeros_like(dg_ref)

        ys = [ya_ref[...], _load_heads(ob_ref, N_HEADS), yc_ref[...], _load_heads(od_ref, N_HEADS)]
        outs = [dya_ref, dob_ref, dyc_ref, dod_ref]
        for m, y in enumerate(ys):
            cols = slice(512 * m, 512 * (m + 1))
            r = lax.rsqrt(jnp.mean(y * y, axis=-1, keepdims=True) + RMS_EPS)
            yh = y * r
            dh = dy_ref[:, cols]
            dg_ref[:, cols] += jnp.sum(dh * yh, axis=0, keepdims=True)
            dyh = dh * g_ref[:, cols]
            dyv = r * (dyh - yh * jnp.mean(dyh * yh, axis=-1, keepdims=True))
            if m % 2 == 0:
                outs[m][...] = dyv
            else:
                _store_heads(outs[m], dyv, N_HEADS)

    row = pl.BlockSpec((tm, 512), lambda b, i: (b * ns + i, 0))
    heads = pl.BlockSpec((None, N_HEADS, tm, HEAD_DIM), lambda b, i: (b, 0, i, 0))
    vec = pl.BlockSpec((1, 2048), lambda b, i: (0, 0))
    flat = jax.ShapeDtypeStruct((T, 512), F32)
    hm = jax.ShapeDtypeStruct((B, N_HEADS, S, HEAD_DIM), F32)
    return pl.pallas_call(
        body, name=name, grid=(B, ns), in_specs=[row, heads, row, heads, pl.BlockSpec((tm, 2048), lambda b, i: (b * ns + i, 0)), vec],
        out_specs=[row, heads, row, heads, vec], out_shape=[flat, hm, flat, hm, jax.ShapeDtypeStruct((1, 2048), F32)],
        compiler_params=_params(("arbitrary", "arbitrary")),
    )(ya, ob, yc, od, dycat, gain)


def _ffn_down(gate, up, w_down, res, name):
    J, T, n = gate.shape
    N = w_down.shape[2]
    tm = min(256, T)

    def body(g_ref, u_ref, w_ref, r_ref, o_ref, act_ref):
        acc = None
        for j in range(J):
            g = g_ref[j].astype(F32)
            a = (g * _sigmoid(g) * u_ref[j].astype(F32)).astype(BF16)
            act_ref[j] = a
            d = _nn(a, w_ref[j])
            acc = d if acc is None else acc + d
        o_ref[...] = acc + r_ref[...]

    gu = pl.BlockSpec((J, tm, n), lambda i: (0, i, 0))
    row = pl.BlockSpec((tm, N), lambda i: (i, 0))
    return pl.pallas_call(body, name=name, grid=(T // tm,),
                          in_specs=[gu, gu, pl.BlockSpec((J, n, N), lambda i: (0, 0, 0)), row], out_specs=[row, gu],
                          out_shape=[jax.ShapeDtypeStruct((T, N), F32), jax.ShapeDtypeStruct((J, T, n), BF16)],
                          compiler_params=_params(("parallel",)))(gate, up, w_down, res)


def _ffn_down_dx(dx, w_down, gate, up, name):
    J, n, D = w_down.shape
    T = dx.shape[0]
    tm = min(512, T)

    def body(dx_ref, w_ref, g_ref, u_ref, dg_ref, du_ref):
        d = _nt(dx_ref[...].astype(BF16), w_ref[...])
        g = g_ref[...].astype(F32)
        s = _sigmoid(g)
        dg_ref[...] = (d * u_ref[...].astype(F32) * s * (1.0 + g * (1.0 - s))).astype(BF16)
        du_ref[...] = (d * g * s).astype(BF16)

    blk = pl.BlockSpec((None, tm, n), lambda j, i: (j, i, 0))
    shape = jax.ShapeDtypeStruct((J, T, n), BF16)
    return pl.pallas_call(body, name=name, grid=(J, T // tm),
                          in_specs=[pl.BlockSpec((tm, D), lambda j, i: (i, 0)), pl.BlockSpec((None, n, D), lambda j, i: (j, 0, 0)), blk, blk],
                          out_specs=[blk, blk], out_shape=[shape, shape], compiler_params=_params(("parallel", "parallel")))(dx, w_down, gate, up)


def _loss_grad(y, target, name):
    T, D = y.shape
    tm = min(256, T)
    n = T // tm

    def body(y_ref, t_ref, loss_ref, dy_ref, acc):
        i = pl.program_id(0)

        @pl.when(i == 0)
        def _():
            acc[...] = jnp.zeros_like(acc)

        err = y_ref[...] - t_ref[...]
        dy_ref[...] = err * (1.0 / D)
        acc[...] += jnp.sum((err * err).reshape(tm // 8, 8, D), axis=0)

        @pl.when(i == n - 1)
        def _():
            loss_ref[...] = jnp.full((8, 128), 0.5 / D, F32) * jnp.sum(acc[...])

    row = pl.BlockSpec((tm, D), lambda i: (i, 0))
    return pl.pallas_call(body, name=name, grid=(n,), in_specs=[row, row], out_specs=[pl.BlockSpec((8, 128), lambda i: (0, 0)), row],
                          out_shape=[jax.ShapeDtypeStruct((8, 128), F32), jax.ShapeDtypeStruct((T, D), F32)],
                          scratch_shapes=[pltpu.VMEM((8, D), F32)], compiler_params=_params(("arbitrary",)))(y, target)


def _row_tile(R):
    for cand in (512, 256, 128, 64, 32, 16, 8):
        if R % cand == 0:
            return cand
    return R


def _adamw(w, m, v, stack, name, layer=None, prev=None):
    n, R, C = stack.shape
    tm = _row_tile(R)
    nb = R // tm
    off = 0 if layer is None else layer * nb
    c1 = 1.0 - ADAM_B1 ** ADAM_STEP
    c2 = 1.0 - ADAM_B2 ** ADAM_STEP

    def body(w_ref, m_ref, v_ref, s_ref, *rest):
        g_ref, d_ref, mo_ref, vo_ref = rest[-4:]
        g = s_ref[0].astype(F32)
        for k in range(1, n):
            g = g + s_ref[k].astype(F32)
        mn = ADAM_B1 * m_ref[...] + (1.0 - ADAM_B1) * g
        vn = ADAM_B2 * v_ref[...] + (1.0 - ADAM_B2) * (g * g)
        g_ref[...] = g
        mo_ref[...] = mn
        vo_ref[...] = vn
        d_ref[...] = -ADAM_LR * ((mn / c1) / (jnp.sqrt(vn / c2) + ADAM_EPS) + ADAM_WD * w_ref[...])

    blk = pl.BlockSpec((tm, C), lambda i: (i + off, 0))
    ins = [w, m, v, stack]
    specs = [blk, blk, blk, pl.BlockSpec((n, tm, C), lambda i: (0, i, 0))]
    aliases = {}
    if prev is not None:
        ins += list(prev)
        specs += [pl.BlockSpec(memory_space=pl.ANY)] * 4
        aliases = {4 + t: t for t in range(4)}
    shape = jax.ShapeDtypeStruct(w.shape, F32)
    return pl.pallas_call(body, name=name, grid=(nb,), in_specs=specs, out_specs=[blk] * 4, out_shape=[shape] * 4,
                          input_output_aliases=aliases, compiler_params=_params(("parallel",)))(*ins)


HBM = pl.BlockSpec(memory_space=pltpu.HBM)
SEM = pl.BlockSpec(memory_space=pltpu.SEMAPHORE)
EFFECT = pltpu.SideEffectType.DATAFLOW_SIDE_EFFECTING


def _spread_copies(srcs, lands, send_sems, recv_sems, local_sems, scatter, waiting):
    x, y, c = lax.axis_index("x"), lax.axis_index("y"), lax.axis_index("c")
    me = 4 * x + 2 * y + c
    remote, local = [], []
    for a, (s, l) in enumerate(zip(srcs, lands)):
        for d in range(N_DEV - 1):
            bits = d + 1
            peer = (1 - x if bits & 4 else x, 1 - y if bits & 2 else y, 1 - c if bits & 1 else c)
            pid = 4 * peer[0] + 2 * peer[1] + peer[2]
            remote.append(pltpu.make_async_remote_copy(
                src_ref=s.at[pid] if scatter else s, dst_ref=l.at[pid if waiting else me],
                send_sem=send_sems.at[a * (N_DEV - 1) + d], recv_sem=recv_sems.at[a * (N_DEV - 1) + d], device_id=peer, device_id_type=MESH_ID))
        local.append(pltpu.make_async_copy(s.at[me] if scatter else s, l.at[me], local_sems.at[a]))
    return remote, local


def _spread_start(srcs, scatter, name, after=None):
    n = len(srcs)
    land_shapes = [a.shape if scatter else (N_DEV,) + a.shape for a in srcs]
    extra = [] if after is None else [after]

    def body(*refs):
        src_refs, land_refs = refs[:n], refs[n:2 * n]
        send_sems, recv_sems, local_sems = refs[2 * n + len(extra):2 * n + len(extra) + 3]
        remote, local = _spread_copies(src_refs, land_refs, send_sems, recv_sems, local_sems, scatter, False)
        for cp in remote + local:
            cp.start()
        refs[-1][...] = jnp.zeros((8, 128), F32)

    outs = pl.pallas_call(
        body, name=name,
        out_shape=(pltpu.SemaphoreType.DMA((n * (N_DEV - 1),)), pltpu.SemaphoreType.DMA((n * (N_DEV - 1),)), pltpu.SemaphoreType.DMA((n,)),
                   *[pltpu.HBM(a.shape, a.dtype) for a in srcs], *[pltpu.HBM(shp, a.dtype) for shp, a in zip(land_shapes, srcs)],
                   jax.ShapeDtypeStruct((8, 128), F32)),
        in_specs=[HBM] * (2 * n) + [pl.BlockSpec(memory_space=pl.ANY)] * len(extra),
        out_specs=(SEM, SEM, SEM, *[HBM] * (2 * n), pl.BlockSpec(memory_space=pltpu.VMEM)),
        input_output_aliases={i: 3 + i for i in range(2 * n)},
        compiler_params=pltpu.CompilerParams(has_side_effects=EFFECT),
    )(*[pltpu.with_memory_space_constraint(a, pltpu.HBM) for a in srcs],
      *[pltpu.with_memory_space_constraint(lax.empty(shp, a.dtype), pltpu.HBM) for shp, a in zip(land_shapes, srcs)], *extra)
    return outs[:3], list(outs[3:3 + n]), list(outs[3 + n:3 + 2 * n]), outs[-1]


def _spread_wait(sems, srcs, lands, after, scatter, name):
    n = len(srcs)
    after = list(after) if isinstance(after, (list, tuple)) else [after]

    def body(*refs):
        src_refs, land_refs = refs[:n], refs[n:2 * n]
        send_sems, recv_sems, local_sems = refs[2 * n:2 * n + 3]
        remote, local = _spread_copies(src_refs, land_refs, send_sems, recv_sems, local_sems, scatter, True)
        for cp in remote:
            cp.wait_send()
            cp.wait_recv()
        for cp in local:
            cp.wait()

    outs = pl.pallas_call(
        body, name=name, out_shape=tuple(pltpu.HBM(a.shape, a.dtype) for a in srcs + lands),
        in_specs=[HBM] * (2 * n) + [SEM] * 3 + [pl.BlockSpec(memory_space=pl.ANY)] * len(after), out_specs=tuple([HBM] * (2 * n)),
        input_output_aliases={i: i for i in range(2 * n)}, compiler_params=pltpu.CompilerParams(has_side_effects=EFFECT),
    )(*srcs, *lands, *sems, *after)
    return list(outs[n:])


SMALL = ("rel_bias", "norm1_g", "sgu_w", "sgu_b", "dil_qn_g", "dil_kn_g", "conv_w", "conv_b", "conv_ln_g", "conv_ln_b",
         "gqa_qn_g", "gqa_kn_g", "mix_norm_g", "norm2_g")
LARGE = ("w_in", "w_out", "w_gate", "w_up", "w_down")


def _local_step(x, target, p, B, S, fetch, emit, mid):
    T = B * S
    rope = _rope_tables(S)
    win = _bias_windows(p["rel_bias"], S)
    tile8 = lambda g: jnp.tile(g.reshape(1, HEAD_DIM), (1, N_HEADS))
    cols_b = (COL_BQ, COL_BK, COL_BV)
    cols_d = (COL_DQ, COL_DK128, COL_DV128)
    saved = []
    for l in range(DEPTH):
        s = {"x": x}
        s["ws"] = p["sgu_w"][l].astype(BF16)
        s["bias"] = jnp.repeat(p["sgu_b"][l].T, HEAD_DIM, axis=1)
        s["h"] = _rms_fwd(x, p["norm1_g"][l], f"rms1_fwd_{l}")
        s["win"] = fetch(l, "in", s["h"])
        s["cw"] = jnp.pad(s["win"]["conv_w"], ((0, 1), (0, 0))).reshape(32, 1, 512)
        z = s["z"] = _matmul(s["h"], s["win"]["w_in"], "nn", f"in_proj_{l}", tk=D_MODEL)
        s["bias"] = s["bias"] + mid(l, z)
        s["ya"] = _sgu_fwd(z, s["ws"], s["bias"], f"sgu_fwd_{l}")
        s["c"] = _conv_fwd1(z, s["cw"], p["conv_b"][l].reshape(1, 512), B, S, f"conv_fwd_{l}")
        s["yc"] = _conv_fwd2(s["c"], p["conv_ln_g"][l].reshape(1, 512), p["conv_ln_b"][l].reshape(1, 512), f"conv_ln_fwd_{l}")
        s["gb"] = (tile8(p["dil_qn_g"][l]), tile8(p["dil_kn_g"][l]))
        s["gd"] = (tile8(p["gqa_qn_g"][l]), tile8(p["gqa_kn_g"][l])[:, :KV_WIDTH])
        s["qkv_b"] = _prep_fwd(z, *s["gb"], None, B, S, N_HEADS, cols_b, f"prep_b_fwd_{l}")
        s["qkv_d"] = _prep_fwd(z, *s["gd"], rope, B, S, KV_HEADS, cols_d, f"prep_d_fwd_{l}")
        s["ob"] = _attn_fwd(*s["qkv_b"], win, f"attn_b_fwd_{l}")
        s["od"] = _attn_fwd(*s["qkv_d"], None, f"attn_d_fwd_{l}")
        s["gmix"] = p["mix_norm_g"][l].reshape(1, 2048)
        s["ycat"] = _mix_fwd(s["ya"], s["ob"], s["yc"], s["od"], s["gmix"], B, S, f"mix_fwd_{l}")
        s["wout"] = fetch(l, "out", s["ycat"])["w_out"]
        x1 = s["x1"] = _matmul(s["ycat"], s["wout"], "nn", f"out_proj_{l}", res=x, tk=D_MODEL)
        s["h2"] = _rms_fwd(x1, p["norm2_g"][l], f"rms2_fwd_{l}")
        s["ffn"] = fetch(l, "ffn", s["h2"])
        s["gate"] = _mm_shard_out(s["h2"], s["ffn"]["w_gate"], "nn", f"ffn_gate_{l}", out_dtype=BF16)
        s["up"] = _mm_shard_out(s["h2"], s["ffn"]["w_up"], "nn", f"ffn_up_{l}", out_dtype=BF16)
        x, s["act"] = _ffn_down(s["gate"], s["up"], s["ffn"]["w_down"], x1, f"ffn_down_{l}")
        saved.append(s)

    loss_blk, dx = _loss_grad(x, target, "loss")
    g = {k: [None] * DEPTH for k in SMALL if k != "rel_bias"}
    dwin_total = None
    for l in reversed(range(DEPTH)):
        s = saved[l]
        z, ffn = s["z"], s["ffn"]
        dgate, dup = _ffn_down_dx(dx, ffn["w_down"], s["gate"], s["up"], f"ffn_down_dx_{l}")
        tok = emit(l, "w_down", _mm_shard_m(s["act"], dx, f"ffn_down_dw_{l}", out_dtype=BF16, tn=512, tk=T))
        tok += emit(l, "w_gate", _mm_shard_out(s["h2"], dgate, "tn", f"ffn_gate_dw_{l}", out_dtype=BF16, tm=1024, tk=T))
        tok += emit(l, "w_up", _mm_shard_out(s["h2"], dup, "tn", f"ffn_up_dw_{l}", out_dtype=BF16, tm=1024, tk=T))
        dh2 = _mm_shard_k([(dgate, ffn["w_gate"]), (dup, ffn["w_up"])], "nt", f"ffn_up_dx_{l}", tn=512, fold=N_DEV)
        dx1, dg2 = _rms_bwd(dh2, s["x1"], p["norm2_g"][l] + tok, dx, f"rms2_bwd_{l}")
        g["norm2_g"][l] = dg2[0]
        dycat = _matmul(dx1, s["wout"], "nt", f"out_proj_dx_{l}", tk=D_MODEL)
        tok = emit(l, "w_out", _matmul(s["ycat"], dx1, "tn", f"out_proj_dw_{l}", out_dtype=BF16, tn=1024, tk=T))
        dya, dob, dyc, dod, dgm = _mix_bwd(s["ya"], s["ob"], s["yc"], s["od"], dycat, s["gmix"] + tok, B, S, f"mix_bwd_{l}")
        g["mix_norm_g"][l] = dgm[0]
        dz_a, dws, dbias = _sgu_bwd(z, dya, s["ws"], jnp.swapaxes(s["ws"], 1, 2), s["bias"], f"sgu_bwd_{l}")
        g["sgu_w"][l] = dws
        g["sgu_b"][l] = dbias.reshape(128, 8, HEAD_DIM).sum(-1).T
        dc, dlg, dlb, dcb = _conv_bwd1(s["c"], dyc, p["conv_ln_g"][l].reshape(1, 512), p["conv_ln_b"][l].reshape(1, 512), f"conv_ln_bwd_{l}")
        g["conv_ln_g"][l], g["conv_ln_b"][l], g["conv_b"][l] = dlg[0], dlb[0], dcb[0]
        dz_ca, dz_cg, dcw = _conv_bwd2(z, dc, s["cw"], B, S, f"conv_bwd_{l}")
        g["conv_w"][l] = dcw.reshape(32, 512)[:CONV_WIDTH]
        dq, dk, dv, dwin = _attn_bwd(*s["qkv_b"], s["ob"], dob, win, f"attn_b_bwd_{l}")
        dwin_total = dwin if dwin_total is None else dwin_total + dwin
        dz_b, dgq, dgk = _prep_bwd(z, dq, dk, dv, *s["gb"], None, B, S, N_HEADS, cols_b, f"prep_b_bwd_{l}")
        g["dil_qn_g"][l] = dgq.reshape(N_HEADS, HEAD_DIM).sum(0)
        g["dil_kn_g"][l] = dgk.reshape(N_HEADS, HEAD_DIM).sum(0)
        dq, dk, dv = _attn_bwd(*s["qkv_d"], s["od"], dod, None, f"attn_d_bwd_{l}")
        dz_d, dgq, dgk = _prep_bwd(z, dq, dk, dv, *s["gd"], rope, B, S, KV_HEADS, cols_d, f"prep_d_bwd_{l}")
        g["gqa_qn_g"][l] = dgq.reshape(N_HEADS, HEAD_DIM).sum(0)
        g["gqa_kn_g"][l] = dgk.reshape(KV_HEADS, HEAD_DIM).sum(0)
        dz = jnp.concatenate([dz_a, dz_b, dz_ca, dz_cg, dz_d], axis=1)
        tok = emit(l, "w_in", _matmul(s["h"], dz, "tn", f"in_proj_dw_{l}", out_dtype=BF16, tk=T))
        dh = _matmul(dz, s["win"]["w_in"], "nt", f"in_proj_dx_{l}", tn=1024, tk=IN_WIDTH)
        dx, dg1 = _rms_bwd(dh, s["x"], p["norm1_g"][l] + tok, dx1, f"rms1_bwd_{l}")
        g["norm1_g"][l] = dg1[0]

    grads = {k: jnp.stack(v) for k, v in g.items()}
    grads["rel_bias"] = _bias_fold(dwin_total, S, "bias_fold")
    return loss_blk[0, 0], dx, grads


GROUPS = {"in": ("w_in",), "out": ("w_out",), "ffn": ("w_gate", "w_up", "w_down")}
COL_SHARDED = ("w_in", "w_gate", "w_up")


def kernel(x, rel_bias, norm1_g, w_in, sgu_w, sgu_b, dil_qn_g, dil_kn_g, conv_w, conv_b, conv_ln_g, conv_ln_b, gqa_qn_g, gqa_kn_g, mix_norm_g, w_out, norm2_g, w_gate, w_up, w_down, loss_target, m_rel_bias, m_norm1_g, m_w_in, m_sgu_w, m_sgu_b, m_dil_qn_g, m_dil_kn_g, m_conv_w, m_conv_b, m_conv_ln_g, m_conv_ln_b, m_gqa_qn_g, m_gqa_kn_g, m_mix_norm_g, m_w_out, m_norm2_g, m_w_gate, m_w_up, m_w_down, v_rel_bias, v_norm1_g, v_w_in, v_sgu_w, v_sgu_b, v_dil_qn_g, v_dil_kn_g, v_conv_w, v_conv_b, v_conv_ln_g, v_conv_ln_b, v_gqa_qn_g, v_gqa_kn_g, v_mix_norm_g, v_w_out, v_norm2_g, v_w_gate, v_w_up, v_w_down):
    w = dict(rel_bias=rel_bias, norm1_g=norm1_g, w_in=w_in, sgu_w=sgu_w, sgu_b=sgu_b, dil_qn_g=dil_qn_g, dil_kn_g=dil_kn_g, conv_w=conv_w,
             conv_b=conv_b, conv_ln_g=conv_ln_g, conv_ln_b=conv_ln_b, gqa_qn_g=gqa_qn_g, gqa_kn_g=gqa_kn_g, mix_norm_g=mix_norm_g,
             w_out=w_out, norm2_g=norm2_g, w_gate=w_gate, w_up=w_up, w_down=w_down)
    m = dict(rel_bias=m_rel_bias, norm1_g=m_norm1_g, w_in=m_w_in, sgu_w=m_sgu_w, sgu_b=m_sgu_b, dil_qn_g=m_dil_qn_g, dil_kn_g=m_dil_kn_g,
             conv_w=m_conv_w, conv_b=m_conv_b, conv_ln_g=m_conv_ln_g, conv_ln_b=m_conv_ln_b, gqa_qn_g=m_gqa_qn_g, gqa_kn_g=m_gqa_kn_g,
             mix_norm_g=m_mix_norm_g, w_out=m_w_out, norm2_g=m_norm2_g, w_gate=m_w_gate, w_up=m_w_up, w_down=m_w_down)
    v = dict(rel_bias=v_rel_bias, norm1_g=v_norm1_g, w_in=v_w_in, sgu_w=v_sgu_w, sgu_b=v_sgu_b, dil_qn_g=v_dil_qn_g, dil_kn_g=v_dil_kn_g,
             conv_w=v_conv_w, conv_b=v_conv_b, conv_ln_g=v_conv_ln_g, conv_ln_b=v_conv_ln_b, gqa_qn_g=v_gqa_qn_g, gqa_kn_g=v_gqa_kn_g,
             mix_norm_g=v_mix_norm_g, w_out=v_w_out, norm2_g=v_norm2_g, w_gate=v_w_gate, w_up=v_w_up, w_down=v_w_down)
    names = list(w)
    B, S, D = x.shape
    T = B * S
    me = 4 * lax.axis_index("x") + 2 * lax.axis_index("y") + lax.axis_index("c")

    bf = {k: w[k].astype(BF16) for k in LARGE}
    spreads = {}

    def start_gather(l, group, after=None):
        srcs = [bf[k][l] for k in GROUPS[group]] + ([conv_w[l]] if group == "in" else [])
        spreads[l, group] = _spread_start(srcs, False, f"gather_{group}_{l}_start", after)
        return spreads[l, group][3][0, 0]

    tok0 = start_gather(0, "in") + start_gather(0, "out") + start_gather(0, "ffn")
    small = {k: w[k] for k in SMALL}
    small["norm1_g"] = norm1_g.at[0].add(tok0)

    def mid(l, z):
        if l > 0:
            return jnp.zeros((), F32)
        return start_gather(1, "in", z) + start_gather(1, "out", z) + start_gather(1, "ffn", z)

    def fetch(l, group, after):
        sems, srcs, lands, _ = spreads[l, group]
        got = dict(zip(GROUPS[group] + ("conv_w",), _spread_wait(sems, srcs, lands, after, False, f"gather_{group}_{l}_wait")))
        if group == "in":
            got["w_in"] = jnp.transpose(got["w_in"], (1, 0, 2)).reshape(D, IN_WIDTH)
            got["conv_w"] = jnp.transpose(got["conv_w"], (1, 0, 2)).reshape(CONV_WIDTH, 512)
        if group == "out":
            got["w_out"] = got["w_out"].reshape(D, D)
        return got

    scatters = {}

    def emit(l, k, dw):
        if k == "w_in":
            dw = jnp.transpose(dw.reshape(D, N_DEV, IN_WIDTH // N_DEV), (1, 0, 2))
        if k == "w_out":
            dw = dw.reshape(N_DEV, D // N_DEV, D)
        scatters[l, k] = _spread_start([dw], True, f"scatter_{k}_{l}_start")
        return scatters[l, k][3][0, 0]

    loss_part, dx, grads = _local_step(x.reshape(T, D), loss_target.reshape(T, D), small, B, S, fetch, emit, mid)
    loss = lax.psum(loss_part, ("x", "y", "c"))

    out_g, out_d, out_m, out_v = {}, {}, {}, {}

    def update_large(k, after):
        shp = w[k].shape
        two_d = lambda a: a.reshape(-1, shp[-1])
        res = None
        for l in reversed(range(DEPTH)):
            sems, srcs, lands, _ = scatters[l, k]
            stack = _spread_wait(sems, srcs, lands, after, True, f"scatter_{k}_{l}_wait")[0]
            res = _adamw(two_d(w[k]), two_d(m[k]), two_d(v[k]), stack.reshape(N_DEV, -1, shp[-1]), f"adamw_{k}_{l}", layer=l, prev=res)
        out_g[k], out_d[k], out_m[k], out_v[k] = [a.reshape(shp) for a in res]
        return res[0]

    flat2 = lambda a: a.reshape(-1, a.shape[-1])
    small_sems, small_srcs, small_lands, small_tok = _spread_start([flat2(grads[k]) for k in SMALL], False, "gather_small_grads_start")
    after = [dx, small_tok]
    for k in ("w_down", "w_gate", "w_up", "w_out"):
        after = update_large(k, after)
    stacks = _spread_wait(small_sems, small_srcs, small_lands, after, False, "gather_small_grads_wait")
    for k, stack in zip(SMALL, stacks):
        if k == "conv_w":
            stack = lax.dynamic_slice_in_dim(stack, me * (512 // N_DEV), 512 // N_DEV, axis=2)
        res = _adamw(flat2(w[k]), flat2(m[k]), flat2(v[k]), stack, f"adamw_{k}")
        out_g[k], out_d[k], out_m[k], out_v[k] = [a.reshape(w[k].shape) for a in res]
        after = res[0]
    update_large("w_in", after)

    return (loss, dx.reshape(B, S, D), *[out_g[k] for k in names], *[out_d[k] for k in names],
            *[out_m[k] for k in names], *[out_v[k] for k in names])
```

```python
import functools
import math

import numpy as np
import jax
import jax.numpy as jnp
from jax import lax
from jax.experimental import pallas as pl
from jax.experimental.pallas import tpu as pltpu

F32 = jnp.float32
BF16 = jnp.bfloat16
HIGHEST = lax.Precision.HIGHEST
MESH_ID = pl.DeviceIdType.MESH

D_MODEL = 2048
DEPTH = 2
HEAD_DIM = 64
GROUP_WIDTH = 512
N_HEADS = 8
KV_HEADS = 2
KV_WIDTH = 128
SGU_CHUNK = 128
CONV_WIDTH = 31
CONV_PAD = 16
GRID_W = 64
ROPE_THETA = 10000.0
REL_BUCKETS = 32
REL_MAX_DIST = 1024
DIL_PATTERNS = ((128, 1), (512, 4), (2048, 16))
FFN_HIDDEN = 5632
IN_WIDTH = 4352
RMS_EPS = 1e-6
LN_EPS = 1e-5
MASKED = -1e30
N_DEV = 8

ADAM_LR = 0.001
ADAM_B1 = 0.9
ADAM_B2 = 0.999
ADAM_EPS = 1e-08
ADAM_WD = 0.01
ADAM_STEP = 10

COL_AU, COL_AV, COL_BQ, COL_BK, COL_BV, COL_CA, COL_CG, COL_DQ = range(8)
COL_DK128, COL_DV128 = 32, 33

VMEM_LIMIT = 56 * 1024 * 1024
ATTN_TQ = 256


def _params(sem=None, vmem=VMEM_LIMIT):
    return pltpu.CompilerParams(dimension_semantics=sem, vmem_limit_bytes=vmem)


def _dot(a, b, dims, precision=None):
    return lax.dot_general(a, b, (dims, ((), ())), precision=precision, preferred_element_type=F32)


def _nn(a, b, precision=None):
    return _dot(a, b, ((1,), (0,)), precision)


def _nt(a, b):
    return _dot(a, b, ((1,), (1,)))


def _tn(a, b):
    return _dot(a, b, ((0,), (0,)))


DIMS = {"nn": ((1,), (0,)), "nt": ((1,), (1,)), "tn": ((0,), (0,))}


def _pick(n, cands):
    for c in cands:
        if n % c == 0:
            return c
    return n


def _mm_call(name, mode, pairs, specs, o_spec, out_sds, grid, acc_shape, res=None, fold=None):
    npair, nk, dims = len(pairs), grid[2], DIMS[mode]

    def body(*refs):
        ab = refs[:2 * npair]
        r_ref = refs[2 * npair] if res is not None else None
        o_ref = refs[2 * npair + (res is not None)]
        part = None
        for t in range(npair):
            for s in ([None] if fold is None else range(fold)):
                a_blk = ab[2 * t][...] if s is None else ab[2 * t][s]
                b_blk = ab[2 * t + 1][...] if s is None else ab[2 * t + 1][s]
                d = _dot(a_blk.astype(BF16), b_blk.astype(BF16), dims)
                part = d if part is None else part + d

        def finish(r):
            if r_ref is not None:
                r = r + r_ref[...]
            o_ref[...] = r.astype(o_ref.dtype)

        if nk == 1:
            finish(part)
            return
        acc, k = refs[-1], pl.program_id(2)

        @pl.when(k == 0)
        def _():
            acc[...] = part

        @pl.when(k > 0)
        def _():
            acc[...] += part

        @pl.when(k == nk - 1)
        def _():
            finish(acc[...])

    ins = [t for pair in pairs for t in pair]
    in_specs = [t for pair in specs for t in pair]
    if res is not None:
        ins.append(res)
        in_specs.append(o_spec)
    return pl.pallas_call(
        body, name=name, grid=grid, in_specs=in_specs, out_specs=o_spec, out_shape=out_sds,
        scratch_shapes=[pltpu.VMEM(acc_shape, F32)] if nk > 1 else [],
        compiler_params=_params(("parallel", "parallel", "arbitrary")),
    )(*ins)


def _matmul(a, b, mode, name, res=None, out_dtype=F32, tm=512, tn=None, tk=None):
    if mode == "nn":
        (M, K), N = a.shape, b.shape[1]
    elif mode == "nt":
        (M, K), N = a.shape, b.shape[0]
    else:
        (K, M), N = a.shape, b.shape[1]
    tm = min(tm, M)
    tn = tn or _pick(N, (2176, 2048, 1408, 1024, 512))
    tk = tk or _pick(K, (1024, 2176, 1408, 512))
    assert M % tm == 0 and N % tn == 0 and K % tk == 0, (M, N, K, tm, tn, tk)
    a_spec = pl.BlockSpec((tk, tm), lambda i, j, k: (k, i)) if mode == "tn" else pl.BlockSpec((tm, tk), lambda i, j, k: (i, k))
    b_spec = pl.BlockSpec((tn, tk), lambda i, j, k: (j, k)) if mode == "nt" else pl.BlockSpec((tk, tn), lambda i, j, k: (k, j))
    o_spec = pl.BlockSpec((tm, tn), lambda i, j, k: (i, j))
    return _mm_call(name, mode, [(a, b)], [(a_spec, b_spec)], o_spec, jax.ShapeDtypeStruct((M, N), out_dtype),
                    (M // tm, N // tn, K // tk), (tm, tn), res)


def _mm_shard_out(a, bs, mode, name, out_dtype=F32, tm=512, tk=None):
    J = bs.shape[0]
    n = bs.shape[1] if mode == "nt" else bs.shape[2]
    (K, M) = a.shape if mode == "tn" else a.shape[::-1]
    tm = min(tm, M)
    tk = tk or (K if mode != "tn" else _pick(K, (1024, 512)))
    a_spec = pl.BlockSpec((tk, tm), lambda j, i, k: (k, i)) if mode == "tn" else pl.BlockSpec((tm, tk), lambda j, i, k: (i, k))
    b_spec = pl.BlockSpec((None, n, tk), lambda j, i, k: (j, 0, k)) if mode == "nt" else pl.BlockSpec((None, tk, n), lambda j, i, k: (j, k, 0))
    o_spec = pl.BlockSpec((None, tm, n), lambda j, i, k: (j, i, 0))
    return _mm_call(name, mode, [(a, bs)], [(a_spec, b_spec)], o_spec, jax.ShapeDtypeStruct((J, M, n), out_dtype),
                    (J, M // tm, K // tk), (tm, n))


def _mm_shard_k(pairs, mode, name, res=None, out_dtype=F32, tm=512, tn=None, fold=1):
    J, M, n = pairs[0][0].shape
    N = pairs[0][1].shape[2] if mode == "nn" else pairs[0][1].shape[1]
    tm = min(tm, M)
    tn = tn or _pick(N, (2048, 1024, 512))
    a_spec = pl.BlockSpec((fold, tm, n), lambda i, j, k: (k, i, 0))
    b_spec = pl.BlockSpec((fold, n, tn), lambda i, j, k: (k, 0, j)) if mode == "nn" else pl.BlockSpec((fold, tn, n), lambda i, j, k: (k, j, 0))
    o_spec = pl.BlockSpec((tm, tn), lambda i, j, k: (i, j))
    return _mm_call(name, mode, pairs, [(a_spec, b_spec)] * len(pairs), o_spec, jax.ShapeDtypeStruct((M, N), out_dtype),
                    (M // tm, N // tn, J // fold), (tm, tn), res, fold)


def _mm_shard_m(as_, b, name, out_dtype=F32, tn=None, tk=512):
    J, K, n = as_.shape
    N = b.shape[1]
    tn = tn or _pick(N, (2048, 1024, 512))
    tk = min(tk, K)
    a_spec = pl.BlockSpec((None, tk, n), lambda j, i, k: (j, k, 0))
    b_spec = pl.BlockSpec((tk, tn), lambda j, i, k: (k, i))
    o_spec = pl.BlockSpec((None, n, tn), lambda j, i, k: (j, 0, i))
    return _mm_call(name, "tn", [(as_, b)], [(a_spec, b_spec)], o_spec, jax.ShapeDtypeStruct((J, n, N), out_dtype),
                    (J, N // tn, K // tk), (n, tn))


def _seg_matrix(width):
    return jnp.asarray(np.kron(np.eye(width // HEAD_DIM, dtype=np.float32), np.full((HEAD_DIM, HEAD_DIM), 1.0 / HEAD_DIM, np.float32)), BF16)


def _segmean(v, p):
    hi = v.astype(BF16)
    r = v - hi.astype(F32)
    mid = r.astype(BF16)
    lo = (r - mid.astype(F32)).astype(BF16)
    return _nn(hi, p) + _nn(mid, p) + _nn(lo, p)


def _gelu(x):
    c0 = math.sqrt(2.0 / math.pi)
    t = jnp.tanh(c0 * (x + 0.044715 * x * x * x))
    return 0.5 * x * (1.0 + t), t


def _gelu_grad(x, t):
    c0 = math.sqrt(2.0 / math.pi)
    return 0.5 * (1.0 + t) + 0.5 * x * (1.0 - t * t) * c0 * (1.0 + 3.0 * 0.044715 * x * x)


def _sigmoid(x):
    return 1.0 / (1.0 + jnp.exp(-x))


def _rms_fwd(x, g, name):
    T, D = x.shape
    tm = min(256, T)

    def body(x_ref, g_ref, o_ref):
        xv = x_ref[...]
        r = lax.rsqrt(jnp.mean(xv * xv, axis=-1, keepdims=True) + RMS_EPS)
        o_ref[...] = (xv * r * g_ref[...]).astype(BF16)

    return pl.pallas_call(
        body, name=name, grid=(T // tm,),
        in_specs=[pl.BlockSpec((tm, D), lambda i: (i, 0)), pl.BlockSpec((1, D), lambda i: (0, 0))],
        out_specs=pl.BlockSpec((tm, D), lambda i: (i, 0)), out_shape=jax.ShapeDtypeStruct((T, D), BF16),
        compiler_params=_params(("parallel",)),
    )(x, g.reshape(1, D))


def _rms_bwd(dh, x, g, dres, name):
    T, D = x.shape
    tm = min(256, T)

    def body(dh_ref, x_ref, g_ref, dres_ref, dx_ref, dg_ref):
        @pl.when(pl.program_id(0) == 0)
        def _():
            dg_ref[...] = jnp.zeros_like(dg_ref)

        xv, dhv = x_ref[...], dh_ref[...]
        r = lax.rsqrt(jnp.mean(xv * xv, axis=-1, keepdims=True) + RMS_EPS)
        y = xv * r
        dy = dhv * g_ref[...]
        dx_ref[...] = dres_ref[...] + r * (dy - y * jnp.mean(dy * y, axis=-1, keepdims=True))
        dg_ref[...] += jnp.sum(dhv * y, axis=0, keepdims=True)

    row = pl.BlockSpec((tm, D), lambda i: (i, 0))
    vec = pl.BlockSpec((1, D), lambda i: (0, 0))
    return pl.pallas_call(
        body, name=name, grid=(T // tm,), in_specs=[row, row, vec, row], out_specs=[row, vec],
        out_shape=[jax.ShapeDtypeStruct((T, D), F32), jax.ShapeDtypeStruct((1, D), F32)],
        compiler_params=_params(("arbitrary",)),
    )(dh, x, g.reshape(1, D), dres)


def _sgu_core(zu, zv, ws_ref, bias, p):
    ug, tu = _gelu(zu)
    vg, tv = _gelu(zv)
    xc = vg - _segmean(vg, p)
    rs = lax.rsqrt(_segmean(xc * xc, p) + LN_EPS)
    vn = xc * rs
    vnb = vn.astype(BF16)
    low = lax.broadcasted_iota(jnp.int32, (SGU_CHUNK, 128), 1) < HEAD_DIM
    parts = []
    for j in range(4):
        vp = vnb[:, 128 * j:128 * (j + 1)]
        parts.append(jnp.where(low, _nn(ws_ref[2 * j], vp), _nn(ws_ref[2 * j + 1], vp)))
    mixed = jnp.concatenate(parts, axis=1) + bias
    return ug, tu, tv, rs, vn, vnb, mixed, low


def _sgu_fwd(z, ws, bias, name):
    T = z.shape[0]

    def body(zu_ref, zv_ref, ws_ref, b_ref, p_ref, y_ref):
        ug, _, _, _, _, _, mixed, _ = _sgu_core(zu_ref[...], zv_ref[...], ws_ref, b_ref[...], p_ref[...])
        y_ref[...] = ug * mixed

    full = lambda shape: pl.BlockSpec(shape, lambda i: (0,) * len(shape))
    return pl.pallas_call(
        body, name=name, grid=(T // SGU_CHUNK,),
        in_specs=[pl.BlockSpec((SGU_CHUNK, 512), lambda i: (i, COL_AU)), pl.BlockSpec((SGU_CHUNK, 512), lambda i: (i, COL_AV)),
                  full((8, 128, 128)), full((128, 512)), full((512, 512))],
        out_specs=pl.BlockSpec((SGU_CHUNK, 512), lambda i: (i, 0)), out_shape=jax.ShapeDtypeStruct((T, 512), F32),
        compiler_params=_params(("parallel",)),
    )(z, z, ws, bias, _seg_matrix(512))


def _sgu_bwd(z, dy, ws, ws_t, bias, name):
    T = z.shape[0]

    def body(zu_ref, zv_ref, dy_ref, ws_ref, wst_ref, b_ref, p_ref, dz_ref, dws_ref, db_ref):
        @pl.when(pl.program_id(0) == 0)
        def _():
            dws_ref[...] = jnp.zeros_like(dws_ref)
            db_ref[...] = jnp.zeros_like(db_ref)

        zu, zv, p = zu_ref[...], zv_ref[...], p_ref[...]
        ug, tu, tv, rs, vn, vnb, mixed, low = _sgu_core(zu, zv, ws_ref, b_ref[...], p)
        dyv = dy_ref[...]
        dmixed = dyv * ug
        db_ref[...] += dmixed
        dmb = dmixed.astype(BF16)
        zero = jnp.zeros((SGU_CHUNK, 128), BF16)
        parts = []
        for j in range(4):
            dmp, vp = dmb[:, 128 * j:128 * (j + 1)], vnb[:, 128 * j:128 * (j + 1)]
            dws_ref[2 * j] += _nt(jnp.where(low, dmp, zero), vp)
            dws_ref[2 * j + 1] += _nt(jnp.where(low, zero, dmp), vp)
            parts.append(jnp.where(low, _nn(wst_ref[2 * j], dmp), _nn(wst_ref[2 * j + 1], dmp)))
        dvn = jnp.concatenate(parts, axis=1)
        dvg = rs * (dvn - _segmean(dvn, p) - vn * _segmean(dvn * vn, p))
        dz_ref[:, 0:512] = (dyv * mixed * _gelu_grad(zu, tu)).astype(BF16)
        dz_ref[:, 512:1024] = (dvg * _gelu_grad(zv, tv)).astype(BF16)

    full = lambda shape: pl.BlockSpec(shape, lambda i: (0,) * len(shape))
    return pl.pallas_call(
        body, name=name, grid=(T // SGU_CHUNK,),
        in_specs=[pl.BlockSpec((SGU_CHUNK, 512), lambda i: (i, COL_AU)), pl.BlockSpec((SGU_CHUNK, 512), lambda i: (i, COL_AV)),
                  pl.BlockSpec((SGU_CHUNK, 512), lambda i: (i, 0)), full((8, 128, 128)), full((8, 128, 128)), full((128, 512)), full((512, 512))],
        out_specs=[pl.BlockSpec((SGU_CHUNK, 1024), lambda i: (i, 0)), full((8, 128, 128)), full((128, 512))],
        out_shape=[jax.ShapeDtypeStruct((T, 1024), BF16), jax.ShapeDtypeStruct((8, 128, 128), F32), jax.ShapeDtypeStruct((128, 512), F32)],
        compiler_params=_params(("arbitrary",)),
    )(z, z, dy, ws, ws_t, bias, _seg_matrix(512))


CONV_ROWS = 256


def _conv_taps(pad_ref, w_ref, base, flip):
    blk = pad_ref[pl.ds(base, CONV_ROWS + 2 * CONV_PAD), :]
    acc = jnp.zeros((CONV_ROWS, blk.shape[1]), F32)
    for k in range(CONV_WIDTH):
        wk = w_ref[CONV_WIDTH - 1 - k if flip else k]
        acc = acc + wk * blk[k + 1:k + 1 + CONV_ROWS, :]
    return acc


def _conv_fwd1(z, w, cb, B, S, name):
    T = B * S
    rows = min(CONV_ROWS, S)
    assert rows == CONV_ROWS

    def body(a_ref, g_ref, w_ref, cb_ref, c_ref, pad):
        pad[0:CONV_PAD, :] = jnp.zeros((CONV_PAD, 128), F32)
        pad[CONV_PAD + S:2 * CONV_PAD + S, :] = jnp.zeros((CONV_PAD, 128), F32)
        pad[CONV_PAD:CONV_PAD + S, :] = a_ref[...] * _sigmoid(g_ref[...])

        def tile(r, carry):
            base = pl.multiple_of(r * CONV_ROWS, CONV_ROWS)
            c_ref[pl.ds(base, CONV_ROWS), :] = _conv_taps(pad, w_ref, base, False) + cb_ref[...]
            return carry

        lax.fori_loop(0, S // CONV_ROWS, tile, 0)

    return pl.pallas_call(
        body, name=name, grid=(4, B),
        in_specs=[pl.BlockSpec((S, 128), lambda j, b: (b, 4 * COL_CA + j)), pl.BlockSpec((S, 128), lambda j, b: (b, 4 * COL_CG + j)),
                  pl.BlockSpec((32, 1, 128), lambda j, b: (0, 0, j)), pl.BlockSpec((1, 128), lambda j, b: (0, j))],
        out_specs=pl.BlockSpec((S, 128), lambda j, b: (b, j)), out_shape=jax.ShapeDtypeStruct((T, 512), F32),
        scratch_shapes=[pltpu.VMEM((S + 2 * CONV_PAD, 128), F32)], compiler_params=_params(("parallel", "parallel")),
    )(z, z, w, cb)


def _ln_rows(c):
    mu = jnp.mean(c, axis=-1, keepdims=True)
    xc = c - mu
    rs = lax.rsqrt(jnp.mean(xc * xc, axis=-1, keepdims=True) + LN_EPS)
    return xc * rs, rs


def _conv_fwd2(c, lng, lnb, name):
    T = c.shape[0]
    tm = min(512, T)

    def body(c_ref, g_ref, b_ref, y_ref):
        n, _ = _ln_rows(c_ref[...])
        t = n * g_ref[...] + b_ref[...]
        y_ref[...] = t * _sigmoid(t)

    row = pl.BlockSpec((tm, 512), lambda i: (i, 0))
    vec = pl.BlockSpec((1, 512), lambda i: (0, 0))
    return pl.pallas_call(body, name=name, grid=(T // tm,), in_specs=[row, vec, vec], out_specs=row,
                          out_shape=jax.ShapeDtypeStruct((T, 512), F32), compiler_params=_params(("parallel",)))(c, lng, lnb)


def _conv_bwd1(c, dy, lng, lnb, name):
    T = c.shape[0]
    tm = min(512, T)

    def body(c_ref, dy_ref, g_ref, b_ref, dc_ref, dg_ref, db_ref, dcb_ref):
        @pl.when(pl.program_id(0) == 0)
        def _():
            dg_ref[...] = jnp.zeros_like(dg_ref)
            db_ref[...] = jnp.zeros_like(db_ref)
            dcb_ref[...] = jnp.zeros_like(dcb_ref)

        n, rs = _ln_rows(c_ref[...])
        t = n * g_ref[...] + b_ref[...]
        s = _sigmoid(t)
        dt = dy_ref[...] * s * (1.0 + t * (1.0 - s))
        dg_ref[...] += jnp.sum(dt * n, axis=0, keepdims=True)
        db_ref[...] += jnp.sum(dt, axis=0, keepdims=True)
        dn = dt * g_ref[...]
        dc = rs * (dn - jnp.mean(dn, axis=-1, keepdims=True) - n * jnp.mean(dn * n, axis=-1, keepdims=True))
        dc_ref[...] = dc
        dcb_ref[...] += jnp.sum(dc, axis=0, keepdims=True)

    row = pl.BlockSpec((tm, 512), lambda i: (i, 0))
    vec = pl.BlockSpec((1, 512), lambda i: (0, 0))
    vshape = jax.ShapeDtypeStruct((1, 512), F32)
    return pl.pallas_call(body, name=name, grid=(T // tm,), in_specs=[row, row, vec, vec], out_specs=[row, vec, vec, vec],
                          out_shape=[jax.ShapeDtypeStruct((T, 512), F32), vshape, vshape, vshape],
                          compiler_params=_params(("arbitrary",)))(c, dy, lng, lnb)


def _conv_bwd2(z, dc, w, B, S, name):
    T = B * S

    def body(a_ref, g_ref, dc_ref, w_ref, da_ref, dg_ref, dw_ref, hpad, dpad, dwacc):
        @pl.when(pl.program_id(1) == 0)
        def _():
            dw_ref[...] = jnp.zeros_like(dw_ref)

        zeros = jnp.zeros((CONV_PAD, 128), F32)
        for ref in (hpad, dpad):
            ref[0:CONV_PAD, :] = zeros
            ref[CONV_PAD + S:2 * CONV_PAD + S, :] = zeros
        hpad[CONV_PAD:CONV_PAD + S, :] = a_ref[...] * _sigmoid(g_ref[...])
        dpad[CONV_PAD:CONV_PAD + S, :] = dc_ref[...]
        dwacc[...] = jnp.zeros_like(dwacc)

        def tile(r, carry):
            base = pl.multiple_of(r * CONV_ROWS, CONV_ROWS)
            dh = _conv_taps(dpad, w_ref, base, True)
            av, gv = a_ref[pl.ds(base, CONV_ROWS), :], g_ref[pl.ds(base, CONV_ROWS), :]
            sg = _sigmoid(gv)
            da_ref[pl.ds(base, CONV_ROWS), :] = (dh * sg).astype(BF16)
            dg_ref[pl.ds(base, CONV_ROWS), :] = (dh * av * sg * (1.0 - sg)).astype(BF16)
            dcv = dc_ref[pl.ds(base, CONV_ROWS), :]
            blk = hpad[pl.ds(base, CONV_ROWS + 2 * CONV_PAD), :]
            for k in range(CONV_WIDTH):
                prod = dcv * blk[k + 1:k + 1 + CONV_ROWS, :]
                dwacc[k] += jnp.sum(prod.reshape(CONV_ROWS // 8, 8, 128), axis=0)
            return carry

        lax.fori_loop(0, S // CONV_ROWS, tile, 0)
        for k in range(CONV_WIDTH):
            dw_ref[k] += jnp.sum(dwacc[k], axis=0, keepdims=True)

    return pl.pallas_call(
        body, name=name, grid=(4, B),
        in_specs=[pl.BlockSpec((S, 128), lambda j, b: (b, 4 * COL_CA + j)), pl.BlockSpec((S, 128), lambda j, b: (b, 4 * COL_CG + j)),
                  pl.BlockSpec((S, 128), lambda j, b: (b, j)), pl.BlockSpec((32, 1, 128), lambda j, b: (0, 0, j))],
        out_specs=[pl.BlockSpec((S, 128), lambda j, b: (b, j)), pl.BlockSpec((S, 128), lambda j, b: (b, j)),
                   pl.BlockSpec((32, 1, 128), lambda j, b: (0, 0, j))],
        out_shape=[jax.ShapeDtypeStruct((T, 512), BF16), jax.ShapeDtypeStruct((T, 512), BF16), jax.ShapeDtypeStruct((32, 1, 512), F32)],
        scratch_shapes=[pltpu.VMEM((S + 2 * CONV_PAD, 128), F32), pltpu.VMEM((S + 2 * CONV_PAD, 128), F32), pltpu.VMEM((32, 8, 128), F32)],
        compiler_params=_params(("parallel", "arbitrary")),
    )(z, z, dc, w)


def _swap16(x):
    n = x.shape[1]
    first = (lax.broadcasted_iota(jnp.int32, x.shape, 1) % 32) < 16
    return jnp.where(first, pltpu.roll(x, n - 16, 1), pltpu.roll(x, 16, 1))


def _rope(x, cos, sin):
    return x * cos + _swap16(x) * sin


def _rope_t(dy, cos, sin):
    return dy * cos + _swap16(dy * sin)


def _qk_norm(x, p):
    r = lax.rsqrt(_segmean(x * x, p) + RMS_EPS)
    return x * r, r


def _store_heads(ref, val, n):
    for h in range(n):
        ref[h] = val[:, HEAD_DIM * h:HEAD_DIM * (h + 1)].astype(ref.dtype)


def _load_heads(ref, n):
    return jnp.concatenate([ref[h] for h in range(n)], axis=1)


def _prep_fwd(z, gq, gk, rope, B, S, kv_heads, cols, name):
    tm = min(256, S)
    ns = S // tm
    kw = kv_heads * HEAD_DIM
    scale = HEAD_DIM ** -0.5
    qc, kc, vc = cols

    def body(*refs):
        if rope is None:
            q_ref, k_ref, v_ref, gq_ref, gk_ref, p_ref, qo, ko, vo = refs
        else:
            q_ref, k_ref, v_ref, gq_ref, gk_ref, p_ref, cos_ref, sin_ref, qo, ko, vo = refs
        p = p_ref[...]
        qn, _ = _qk_norm(q_ref[...], p)
        kn, _ = _qk_norm(k_ref[...], p[:kw, :kw])
        qn, kn = qn * gq_ref[...], kn * gk_ref[...]
        if rope is not None:
            cos, sin = cos_ref[...], sin_ref[...]
            qn, kn = _rope(qn, cos, sin), _rope(kn, cos[:, :kw], sin[:, :kw])
        _store_heads(qo, qn * scale, N_HEADS)
        _store_heads(ko, kn, kv_heads)
        _store_heads(vo, v_ref[...], kv_heads)

    row = lambda w, c: pl.BlockSpec((tm, w), lambda b, i: (b * ns + i, c))
    const = lambda shape: pl.BlockSpec(shape, lambda b, i: (0,) * len(shape))
    heads = lambda n: pl.BlockSpec((None, n, tm, HEAD_DIM), lambda b, i: (b, 0, i, 0))
    ins = [z, z, z, gq, gk, _seg_matrix(512)]
    specs = [row(512, qc), row(kw, kc), row(kw, vc), const((1, 512)), const((1, kw)), const((512, 512))]
    if rope is not None:
        ins += list(rope)
        specs += [pl.BlockSpec((tm, 512), lambda b, i: (i, 0))] * 2
    return pl.pallas_call(
        body, name=name, grid=(B, ns), in_specs=specs, out_specs=[heads(N_HEADS), heads(kv_heads), heads(kv_heads)],
        out_shape=[jax.ShapeDtypeStruct((B, N_HEADS, S, HEAD_DIM), BF16), jax.ShapeDtypeStruct((B, kv_heads, S, HEAD_DIM), BF16),
                   jax.ShapeDtypeStruct((B, kv_heads, S, HEAD_DIM), BF16)],
        compiler_params=_params(("parallel", "parallel")),
    )(*ins)


def _prep_bwd(z, dq, dk, dv, gq, gk, rope, B, S, kv_heads, cols, name):
    T = B * S
    tm = min(256, S)
    ns = S // tm
    kw = kv_heads * HEAD_DIM
    scale = HEAD_DIM ** -0.5
    qc, kc, _ = cols

    def body(*refs):
        if rope is None:
            q_ref, k_ref, dq_ref, dk_ref, dv_ref, gq_ref, gk_ref, p_ref, dz_ref, dgq_ref, dgk_ref = refs
        else:
            q_ref, k_ref, dq_ref, dk_ref, dv_ref, gq_ref, gk_ref, p_ref, cos_ref, sin_ref, dz_ref, dgq_ref, dgk_ref = refs

        @pl.when((pl.program_id(0) == 0) & (pl.program_id(1) == 0))
        def _():
            dgq_ref[...] = jnp.zeros_like(dgq_ref)
            dgk_ref[...] = jnp.zeros_like(dgk_ref)

        p = p_ref[...]
        dqv = _load_heads(dq_ref, N_HEADS) * scale
        dkv = _load_heads(dk_ref, kv_heads)
        if rope is not None:
            cos, sin = cos_ref[...], sin_ref[...]
            dqv, dkv = _rope_t(dqv, cos, sin), _rope_t(dkv, cos[:, :kw], sin[:, :kw])

        def through_norm(xv, dy, g, pm, dg_ref):
            xh, r = _qk_norm(xv, pm)
            dg_ref[...] += jnp.sum(dy * xh, axis=0, keepdims=True)
            dxh = dy * g
            return r * (dxh - xh * _segmean(dxh * xh, pm))

        dz_ref[:, 0:512] = through_norm(q_ref[...], dqv, gq_ref[...], p, dgq_ref).astype(BF16)
        dz_ref[:, 512:512 + kw] = through_norm(k_ref[...], dkv, gk_ref[...], p[:kw, :kw], dgk_ref).astype(BF16)
        dz_ref[:, 512 + kw:512 + 2 * kw] = _load_heads(dv_ref, kv_heads).astype(BF16)

    row = lambda w, c: pl.BlockSpec((tm, w), lambda b, i: (b * ns + i, c))
    const = lambda shape: pl.BlockSpec(shape, lambda b, i: (0,) * len(shape))
    heads = lambda n: pl.BlockSpec((None, n, tm, HEAD_DIM), lambda b, i: (b, 0, i, 0))
    ins = [z, z, dq, dk, dv, gq, gk, _seg_matrix(512)]
    specs = [row(512, qc), row(kw, kc), heads(N_HEADS), heads(kv_heads), heads(kv_heads), const((1, 512)), const((1, kw)), const((512, 512))]
    if rope is not None:
        ins += list(rope)
        specs += [pl.BlockSpec((tm, 512), lambda b, i: (i, 0))] * 2
    return pl.pallas_call(
        body, name=name, grid=(B, ns), in_specs=specs, out_specs=[row(512 + 2 * kw, 0), const((1, 512)), const((1, kw))],
        out_shape=[jax.ShapeDtypeStruct((T, 512 + 2 * kw), BF16), jax.ShapeDtypeStruct((1, 512), F32), jax.ShapeDtypeStruct((1, kw), F32)],
        compiler_params=_params(("arbitrary", "arbitrary")),
    )(*ins)


def _toeplitz(win, tq, S):
    r = pltpu.roll(jnp.broadcast_to(win, (tq, S + tq)), 0, 1, stride=1, stride_axis=0)
    return r[:, tq:tq + S]


ATTN_HEADS = 4


def _attn_fwd(q, k, v, win, name, nh=ATTN_HEADS):
    B, H, S, _ = q.shape
    shared = k.shape[1] != H
    assert not shared or H // k.shape[1] == nh
    tq = min(ATTN_TQ, S)

    def body(*refs):
        if win is None:
            q_ref, k_ref, v_ref, o_ref = refs
        else:
            q_ref, k_ref, v_ref, w_ref, o_ref = refs
        kvs = [(k_ref[...], v_ref[...])] * nh if shared else [(k_ref[h], v_ref[h]) for h in range(nh)]
        scores = []
        for h in range(nh):
            s = _nt(q_ref[h], kvs[h][0])
            if win is not None:
                s = s + _toeplitz(w_ref[h], tq, S)
            scores.append(s)
        probs = []
        for s in scores:
            p = jnp.exp(s - jnp.max(s, axis=-1, keepdims=True))
            probs.append((p.astype(BF16), jnp.sum(p, axis=-1, keepdims=True)))
        for h, (p, l) in enumerate(probs):
            o_ref[h] = _nn(p, kvs[h][1]) / l

    qs = pl.BlockSpec((None, nh, tq, HEAD_DIM), lambda b, h, i: (b, h, i, 0))
    ks = (pl.BlockSpec((None, None, S, HEAD_DIM), lambda b, h, i: (b, h, 0, 0)) if shared
          else pl.BlockSpec((None, nh, S, HEAD_DIM), lambda b, h, i: (b, h, 0, 0)))
    ins, specs = [q, k, v], [qs, ks, ks]
    if win is not None:
        ins.append(win)
        specs.append(pl.BlockSpec((nh, None, 1, S + tq), lambda b, h, i: (h, i, 0, 0)))
    return pl.pallas_call(body, name=name, grid=(B, H // nh, S // tq), in_specs=specs, out_specs=qs,
                          out_shape=jax.ShapeDtypeStruct((B, H, S, HEAD_DIM), F32),
                          compiler_params=_params(("parallel", "parallel", "parallel")))(*ins)


def _attn_bwd(q, k, v, o, do, win, name, nh=ATTN_HEADS):
    B, H, S, _ = q.shape
    hkv = k.shape[1]
    shared = hkv != H
    assert not shared or H // hkv == nh
    tq = min(ATTN_TQ, S)
    nq = S // tq

    def body(*refs):
        if win is None:
            q_ref, k_ref, v_ref, o_ref, do_ref, dq_ref, dk_ref, dv_ref = refs
        else:
            q_ref, k_ref, v_ref, o_ref, do_ref, w_ref, rev_ref, dq_ref, dk_ref, dv_ref, dw_ref = refs

        @pl.when(pl.program_id(2) == 0)
        def _():
            dk_ref[...] = jnp.zeros_like(dk_ref)
            dv_ref[...] = jnp.zeros_like(dv_ref)

        kvs = [(k_ref[...], v_ref[...])] * nh if shared else [(k_ref[h], v_ref[h]) for h in range(nh)]
        qvs, dobs, scores, dps = [], [], [], []
        for h in range(nh):
            qv, dov = q_ref[h], do_ref[h]
            dob = dov.astype(BF16)
            s = _nt(qv, kvs[h][0])
            if win is not None:
                s = s + _toeplitz(w_ref[h], tq, S)
            dp = _nt(dob, kvs[h][1]) - jnp.sum(dov * o_ref[h], axis=-1, keepdims=True)
            qvs.append(qv)
            dobs.append(dob)
            scores.append(s)
            dps.append(dp)
        pbs, dsbs = [], []
        for s, dp in zip(scores, dps):
            p = jnp.exp(s - jnp.max(s, axis=-1, keepdims=True))
            p = p * (1.0 / jnp.sum(p, axis=-1, keepdims=True))
            pbs.append(p.astype(BF16))
            dsbs.append((p * dp).astype(BF16))
        dk_acc = dv_acc = None
        for h in range(nh):
            dvh, dkh = _tn(pbs[h], dobs[h]), _tn(dsbs[h], qvs[h])
            dq_ref[h] = _nn(dsbs[h], kvs[h][0])
            if shared:
                dv_acc = dvh if dv_acc is None else dv_acc + dvh
                dk_acc = dkh if dk_acc is None else dk_acc + dkh
            else:
                dv_ref[h] += dvh
                dk_ref[h] += dkh
            if win is not None:
                rev = _nn(rev_ref[...], dsbs[h])
                wide = jnp.concatenate([rev, jnp.zeros((tq, tq), F32)], axis=1)
                dw_ref[h] = jnp.sum(pltpu.roll(wide, 0, 1, stride=1, stride_axis=0), axis=0, keepdims=True)
        if shared:
            dv_ref[...] += dv_acc
            dk_ref[...] += dk_acc

    qs = pl.BlockSpec((None, nh, tq, HEAD_DIM), lambda b, h, i: (b, h, i, 0))
    ks = (pl.BlockSpec((None, None, S, HEAD_DIM), lambda b, h, i: (b, h, 0, 0)) if shared
          else pl.BlockSpec((None, nh, S, HEAD_DIM), lambda b, h, i: (b, h, 0, 0)))
    ins, specs = [q, k, v, o, do], [qs, ks, ks, qs, qs]
    outs = [jax.ShapeDtypeStruct((B, H, S, HEAD_DIM), F32), jax.ShapeDtypeStruct((B, hkv, S, HEAD_DIM), F32), jax.ShapeDtypeStruct((B, hkv, S, HEAD_DIM), F32)]
    ospecs = [qs, ks, ks]
    if win is not None:
        ins += [win, jnp.asarray(np.eye(tq, dtype=np.float32)[::-1].copy(), BF16)]
        specs += [pl.BlockSpec((nh, None, 1, S + tq), lambda b, h, i: (h, i, 0, 0)), pl.BlockSpec((tq, tq), lambda b, h, i: (0, 0))]
        outs.append(jax.ShapeDtypeStruct((B, H, nq, 1, S + tq), F32))
        ospecs.append(pl.BlockSpec((None, nh, None, 1, S + tq), lambda b, h, i: (b, h, i, 0, 0)))
    return pl.pallas_call(body, name=name, grid=(B, H // nh, nq), in_specs=specs, out_specs=ospecs, out_shape=outs,
                          compiler_params=_params(("parallel", "parallel", "arbitrary")))(*ins)


def _pattern_count(delta):
    n = jnp.zeros(delta.shape, jnp.int32)
    for window, dil in DIL_PATTERNS:
        n = n + ((delta % dil == 0) & (jnp.abs(delta) <= window // 2)).astype(jnp.int32)
    return n


def _t5_bucket(rel):
    nb = REL_BUCKETS // 2
    max_exact = nb // 2
    ret = jnp.where(rel > 0, nb, 0)
    n = jnp.abs(rel)
    nf = jnp.maximum(n, 1).astype(F32)
    large = max_exact + (jnp.log(nf / max_exact) / math.log(REL_MAX_DIST / max_exact) * (nb - max_exact)).astype(jnp.int32)
    large = jnp.minimum(large, nb - 1)
    return ret + jnp.where(n < max_exact, n, large)


def _bias_windows(rel_bias, S):
    tq = min(ATTN_TQ, S)
    nq = S // tq
    n = nq * (S + tq)
    delta = (jnp.arange(S + tq)[None, :] - (jnp.arange(nq)[:, None] + 1) * tq).reshape(n)
    count = _pattern_count(delta)
    onehot = (_t5_bucket(delta)[None, :] == jnp.arange(REL_BUCKETS)[:, None]).astype(F32)
    extra = jnp.where(count > 0, jnp.log(jnp.maximum(count, 1).astype(F32)), MASKED).reshape(1, n)
    live = (count > 0).astype(F32).reshape(1, n)

    def body(t_ref, oh_ref, live_ref, extra_ref, o_ref):
        o_ref[...] = _nn(t_ref[...], oh_ref[...], HIGHEST) * live_ref[...] + extra_ref[...]

    val = pl.pallas_call(body, name="bias_windows", out_shape=jax.ShapeDtypeStruct((N_HEADS, n), F32),
                         compiler_params=_params())(rel_bias.T, onehot, live, extra)
    return val.reshape(N_HEADS, nq, 1, S + tq)


def _bias_fold(dwin, S, name):
    B, H, nq = dwin.shape[:3]
    tq = min(ATTN_TQ, S)
    n = nq * (S + tq)
    delta = (jnp.arange(S + tq)[None, :] - (tq - 1) - jnp.arange(nq)[:, None] * tq).reshape(n)
    onehot = (_t5_bucket(delta)[:, None] == jnp.arange(128)[None, :]).astype(F32)

    def body(d_ref, oh_ref, o_ref):
        tot = d_ref[0]
        for b in range(1, B):
            tot = tot + d_ref[b]
        o_ref[...] = _nn(tot, oh_ref[...], HIGHEST)

    out = pl.pallas_call(body, name=name, out_shape=jax.ShapeDtypeStruct((H, 128), F32), compiler_params=_params())(dwin.reshape(B, H, n), onehot)
    return out[:, :REL_BUCKETS].T


def _rope_tables(S):
    half = 16
    freqs = ROPE_THETA ** (-jnp.arange(half, dtype=F32) / half)
    t = jnp.arange(S)
    ang_r = (t // GRID_W).astype(F32)[:, None] * freqs[None, :]
    ang_c = (t % GRID_W).astype(F32)[:, None] * freqs[None, :]
    cos = jnp.concatenate([jnp.cos(ang_r)] * 2 + [jnp.cos(ang_c)] * 2, axis=1)
    sin = jnp.concatenate([-jnp.sin(ang_r), jnp.sin(ang_r), -jnp.sin(ang_c), jnp.sin(ang_c)], axis=1)
    return jnp.tile(cos, (1, N_HEADS)), jnp.tile(sin, (1, N_HEADS))


def _mix_fwd(ya, ob, yc, od, gain, B, S, name):
    T = B * S
    tm = min(256, S)
    ns = S // tm

    def body(ya_ref, ob_ref, yc_ref, od_ref, g_ref, o_ref):
        ys = [ya_ref[...], _load_heads(ob_ref, N_HEADS), yc_ref[...], _load_heads(od_ref, N_HEADS)]
        for m, y in enumerate(ys):
            r = lax.rsqrt(jnp.mean(y * y, axis=-1, keepdims=True) + RMS_EPS)
            o_ref[:, 512 * m:512 * (m + 1)] = (y * r * g_ref[:, 512 * m:512 * (m + 1)]).astype(BF16)

    row = pl.BlockSpec((tm, 512), lambda b, i: (b * ns + i, 0))
    heads = pl.BlockSpec((None, N_HEADS, tm, HEAD_DIM), lambda b, i: (b, 0, i, 0))
    return pl.pallas_call(
        body, name=name, grid=(B, ns), in_specs=[row, heads, row, heads, pl.BlockSpec((1, 2048), lambda b, i: (0, 0))],
        out_specs=pl.BlockSpec((tm, 2048), lambda b, i: (b * ns + i, 0)), out_shape=jax.ShapeDtypeStruct((T, 2048), BF16),
        compiler_params=_params(("parallel", "parallel")),
    )(ya, ob, yc, od, gain)


def _mix_bwd(ya, ob, yc, od, dycat, gain, B, S, name):
    T = B * S
    tm = min(256, S)
    ns = S // tm

    def body(ya_ref, ob_ref, yc_ref, od_ref, dy_ref, g_ref, dya_ref, dob_ref, dyc_ref, dod_ref, dg_ref):
        @pl.when((pl.program_id(0) == 0) & (pl.program_id(1) == 0))
        def _():
            dg_ref[...] = jnp.zeros_like(dg_ref)

        ys = [ya_ref[...], _load_heads(ob_ref, N_HEADS), yc_ref[...], _load_heads(od_ref, N_HEADS)]
        outs = [dya_ref, dob_ref, dyc_ref, dod_ref]
        for m, y in enumerate(ys):
            cols = slice(512 * m, 512 * (m + 1))
            r = lax.rsqrt(jnp.mean(y * y, axis=-1, keepdims=True) + RMS_EPS)
            yh = y * r
            dh = dy_ref[:, cols]
            dg_ref[:, cols] += jnp.sum(dh * yh, axis=0, keepdims=True)
            dyh = dh * g_ref[:, cols]
            dyv = r * (dyh - yh * jnp.mean(dyh * yh, axis=-1, keepdims=True))
            if m % 2 == 0:
                outs[m][...] = dyv
            else:
                _store_heads(outs[m], dyv, N_HEADS)

    row = pl.BlockSpec((tm, 512), lambda b, i: (b * ns + i, 0))
    heads = pl.BlockSpec((None, N_HEADS, tm, HEAD_DIM), lambda b, i: (b, 0, i, 0))
    vec = pl.BlockSpec((1, 2048), lambda b, i: (0, 0))
    flat = jax.ShapeDtypeStruct((T, 512), F32)
    hm = jax.ShapeDtypeStruct((B, N_HEADS, S, HEAD_DIM), F32)
    return pl.pallas_call(
        body, name=name, grid=(B, ns), in_specs=[row, heads, row, heads, pl.BlockSpec((tm, 2048), lambda b, i: (b * ns + i, 0)), vec],
        out_specs=[row, heads, row, heads, vec], out_shape=[flat, hm, flat, hm, jax.ShapeDtypeStruct((1, 2048), F32)],
        compiler_params=_params(("arbitrary", "arbitrary")),
    )(ya, ob, yc, od, dycat, gain)


def _ffn_down(gate, up, w_down, res, name):
    J, T, n = gate.shape
    N = w_down.shape[2]
    tm = min(256, T)

    def body(g_ref, u_ref, w_ref, r_ref, o_ref, act_ref):
        acc = None
        for j in range(J):
            g = g_ref[j].astype(F32)
            a = (g * _sigmoid(g) * u_ref[j].astype(F32)).astype(BF16)
            act_ref[j] = a
            d = _nn(a, w_ref[j])
            acc = d if acc is None else acc + d
        o_ref[...] = acc + r_ref[...]

    gu = pl.BlockSpec((J, tm, n), lambda i: (0, i, 0))
    row = pl.BlockSpec((tm, N), lambda i: (i, 0))
    return pl.pallas_call(body, name=name, grid=(T // tm,),
                          in_specs=[gu, gu, pl.BlockSpec((J, n, N), lambda i: (0, 0, 0)), row], out_specs=[row, gu],
                          out_shape=[jax.ShapeDtypeStruct((T, N), F32), jax.ShapeDtypeStruct((J, T, n), BF16)],
                          compiler_params=_params(("parallel",)))(gate, up, w_down, res)


def _ffn_down_dx(dx, w_down, gate, up, name):
    J, n, D = w_down.shape
    T = dx.shape[0]
    tm = min(512, T)

    def body(dx_ref, w_ref, g_ref, u_ref, dg_ref, du_ref):
        d = _nt(dx_ref[...].astype(BF16), w_ref[...])
        g = g_ref[...].astype(F32)
        s = _sigmoid(g)
        dg_ref[...] = (d * u_ref[...].astype(F32) * s * (1.0 + g * (1.0 - s))).astype(BF16)
        du_ref[...] = (d * g * s).astype(BF16)

    blk = pl.BlockSpec((None, tm, n), lambda j, i: (j, i, 0))
    shape = jax.ShapeDtypeStruct((J, T, n), BF16)
    return pl.pallas_call(body, name=name, grid=(J, T // tm),
                          in_specs=[pl.BlockSpec((tm, D), lambda j, i: (i, 0)), pl.BlockSpec((None, n, D), lambda j, i: (j, 0, 0)), blk, blk],
                          out_specs=[blk, blk], out_shape=[shape, shape], compiler_params=_params(("parallel", "parallel")))(dx, w_down, gate, up)


def _loss_grad(y, target, name):
    T, D = y.shape
    tm = min(256, T)
    n = T // tm

    def body(y_ref, t_ref, loss_ref, dy_ref, acc):
        i = pl.program_id(0)

        @pl.when(i == 0)
        def _():
            acc[...] = jnp.zeros_like(acc)

        err = y_ref[...] - t_ref[...]
        dy_ref[...] = err * (1.0 / D)
        acc[...] += jnp.sum((err * err).reshape(tm // 8, 8, D), axis=0)

        @pl.when(i == n - 1)
        def _():
            loss_ref[...] = jnp.full((8, 128), 0.5 / D, F32) * jnp.sum(acc[...])

    row = pl.BlockSpec((tm, D), lambda i: (i, 0))
    return pl.pallas_call(body, name=name, grid=(n,), in_specs=[row, row], out_specs=[pl.BlockSpec((8, 128), lambda i: (0, 0)), row],
                          out_shape=[jax.ShapeDtypeStruct((8, 128), F32), jax.ShapeDtypeStruct((T, D), F32)],
                          scratch_shapes=[pltpu.VMEM((8, D), F32)], compiler_params=_params(("arbitrary",)))(y, target)


def _row_tile(R):
    for cand in (512, 256, 128, 64, 32, 16, 8):
        if R % cand == 0:
            return cand
    return R


def _adamw(w, m, v, stack, name, layer=None, prev=None):
    n, R, C = stack.shape
    tm = _row_tile(R)
    nb = R // tm
    off = 0 if layer is None else layer * nb
    c1 = 1.0 - ADAM_B1 ** ADAM_STEP
    c2 = 1.0 - ADAM_B2 ** ADAM_STEP

    def body(w_ref, m_ref, v_ref, s_ref, *rest):
        g_ref, d_ref, mo_ref, vo_ref = rest[-4:]
        g = s_ref[0].astype(F32)
        for k in range(1, n):
            g = g + s_ref[k].astype(F32)
        mn = ADAM_B1 * m_ref[...] + (1.0 - ADAM_B1) * g
        vn = ADAM_B2 * v_ref[...] + (1.0 - ADAM_B2) * (g * g)
        g_ref[...] = g
        mo_ref[...] = mn
        vo_ref[...] = vn
        d_ref[...] = -ADAM_LR * ((mn / c1) / (jnp.sqrt(vn / c2) + ADAM_EPS) + ADAM_WD * w_ref[...])

    blk = pl.BlockSpec((tm, C), lambda i: (i + off, 0))
    ins = [w, m, v, stack]
    specs = [blk, blk, blk, pl.BlockSpec((n, tm, C), lambda i: (0, i, 0))]
    aliases = {}
    if prev is not None:
        ins += list(prev)
        specs += [pl.BlockSpec(memory_space=pl.ANY)] * 4
        aliases = {4 + t: t for t in range(4)}
    shape = jax.ShapeDtypeStruct(w.shape, F32)
    return pl.pallas_call(body, name=name, grid=(nb,), in_specs=specs, out_specs=[blk] * 4, out_shape=[shape] * 4,
                          input_output_aliases=aliases, compiler_params=_params(("parallel",)))(*ins)


HBM = pl.BlockSpec(memory_space=pltpu.HBM)
SEM = pl.BlockSpec(memory_space=pltpu.SEMAPHORE)
EFFECT = pltpu.SideEffectType.DATAFLOW_SIDE_EFFECTING


PEERS = {"scatter": (1, 2, 3, 4, 5, 6, 7), "gather": (1, 2, 3, 4, 5, 6, 7), "chips": (1, 2, 4, 6), "forward": (2, 4, 6)}


def _spread_copies(srcs, lands, send_sems, recv_sems, local_sems, kind, waiting):
    x, y, c = lax.axis_index("x"), lax.axis_index("y"), lax.axis_index("c")
    me = 4 * x + 2 * y + c

    def peer(bits):
        dev = (1 - x if bits & 4 else x, 1 - y if bits & 2 else y, 1 - c if bits & 1 else c)
        return dev, 4 * dev[0] + 2 * dev[1] + dev[2]

    plan = PEERS[kind]
    remote, local = [], []
    for a, l in enumerate(lands):
        for d, bits in enumerate(plan):
            dev, pid = peer(bits)
            if kind == "forward":
                src, dst, dev = l.at[pid], l.at[peer(bits | 1)[1] if waiting else pid], peer(1)[0]
            else:
                src, dst = (srcs[a].at[pid] if kind == "scatter" else srcs[a]), l.at[pid if waiting else me]
            remote.append(pltpu.make_async_remote_copy(
                src_ref=src, dst_ref=dst, send_sem=send_sems.at[a * len(plan) + d], recv_sem=recv_sems.at[a * len(plan) + d],
                device_id=dev, device_id_type=MESH_ID))
        if kind != "forward":
            local.append(pltpu.make_async_copy(srcs[a].at[me] if kind == "scatter" else srcs[a], l.at[me], local_sems.at[a]))
    return remote, local


def _spread_start(srcs, kind, name, after=None, lands=None):
    if kind == "forward":
        srcs = []
    else:
        shapes = [a.shape if kind == "scatter" else (N_DEV,) + a.shape for a in srcs]
        lands = [lax.empty(shp, a.dtype) for shp, a in zip(shapes, srcs)]
    ns, nl, per = len(srcs), len(lands), len(PEERS[kind])
    extra = [] if after is None else [after]
    sem_shapes = [pltpu.SemaphoreType.DMA((nl * per,))] * 2 + ([pltpu.SemaphoreType.DMA((nl,))] if ns else [])

    def body(*refs):
        src_refs, land_refs = refs[:ns], refs[ns:ns + nl]
        sems = refs[ns + nl + len(extra):ns + nl + len(extra) + len(sem_shapes)]
        remote, local = _spread_copies(src_refs, land_refs, sems[0], sems[1], sems[2] if ns else None, kind, False)
        for cp in remote + local:
            cp.start()
        refs[-1][...] = jnp.zeros((8, 128), F32)

    outs = pl.pallas_call(
        body, name=name,
        out_shape=(*sem_shapes, *[pltpu.HBM(a.shape, a.dtype) for a in srcs + lands], jax.ShapeDtypeStruct((8, 128), F32)),
        in_specs=[HBM] * (ns + nl) + [pl.BlockSpec(memory_space=pl.ANY)] * len(extra),
        out_specs=(*[SEM] * len(sem_shapes), *[HBM] * (ns + nl), pl.BlockSpec(memory_space=pltpu.VMEM)),
        input_output_aliases={i: len(sem_shapes) + i for i in range(ns + nl)},
        compiler_params=pltpu.CompilerParams(has_side_effects=EFFECT),
    )(*[pltpu.with_memory_space_constraint(a, pltpu.HBM) for a in srcs + lands], *extra)
    k = len(sem_shapes)
    return outs[:k], list(outs[k:k + ns]), list(outs[k + ns:k + ns + nl]), outs[-1]


def _spread_wait(sems, srcs, lands, after, kind, name):
    ns, nl = len(srcs), len(lands)
    after = list(after) if isinstance(after, (list, tuple)) else [after]

    def body(*refs):
        src_refs, land_refs = refs[:ns], refs[ns:ns + nl]
        s = refs[ns + nl:ns + nl + len(sems)]
        remote, local = _spread_copies(src_refs, land_refs, s[0], s[1], s[2] if ns else None, kind, True)
        for cp in remote:
            cp.wait_send()
            cp.wait_recv()
        for cp in local:
            cp.wait()

    outs = pl.pallas_call(
        body, name=name, out_shape=tuple(pltpu.HBM(a.shape, a.dtype) for a in srcs + lands),
        in_specs=[HBM] * (ns + nl) + [SEM] * len(sems) + [pl.BlockSpec(memory_space=pl.ANY)] * len(after), out_specs=tuple([HBM] * (ns + nl)),
        input_output_aliases={i: i for i in range(ns + nl)}, compiler_params=pltpu.CompilerParams(has_side_effects=EFFECT),
    )(*srcs, *lands, *sems, *after)
    return list(outs[ns:])


SMALL = ("rel_bias", "norm1_g", "sgu_w", "sgu_b", "dil_qn_g", "dil_kn_g", "conv_w", "conv_b", "conv_ln_g", "conv_ln_b",
         "gqa_qn_g", "gqa_kn_g", "mix_norm_g", "norm2_g")
LARGE = ("w_in", "w_out", "w_gate", "w_up", "w_down")


def _local_step(x, target, p, B, S, fetch, emit, mid):
    T = B * S
    rope = _rope_tables(S)
    win = _bias_windows(p["rel_bias"], S)
    tile8 = lambda g: jnp.tile(g.reshape(1, HEAD_DIM), (1, N_HEADS))
    cols_b = (COL_BQ, COL_BK, COL_BV)
    cols_d = (COL_DQ, COL_DK128, COL_DV128)
    saved = []
    for l in range(DEPTH):
        s = {"x": x}
        s["ws"] = p["sgu_w"][l].astype(BF16)
        s["bias"] = jnp.repeat(p["sgu_b"][l].T, HEAD_DIM, axis=1)
        s["h"] = _rms_fwd(x, p["norm1_g"][l], f"rms1_fwd_{l}")
        s["win"] = fetch(l, "in", s["h"])
        s["cw"] = jnp.pad(s["win"]["conv_w"], ((0, 1), (0, 0))).reshape(32, 1, 512)
        z = s["z"] = _matmul(s["h"], s["win"]["w_in"], "nn", f"in_proj_{l}", tk=D_MODEL)
        s["bias"] = s["bias"] + mid(l, z)
        s["ya"] = _sgu_fwd(z, s["ws"], s["bias"], f"sgu_fwd_{l}")
        s["c"] = _conv_fwd1(z, s["cw"], p["conv_b"][l].reshape(1, 512), B, S, f"conv_fwd_{l}")
        s["yc"] = _conv_fwd2(s["c"], p["conv_ln_g"][l].reshape(1, 512), p["conv_ln_b"][l].reshape(1, 512), f"conv_ln_fwd_{l}")
        s["gb"] = (tile8(p["dil_qn_g"][l]), tile8(p["dil_kn_g"][l]))
        s["gd"] = (tile8(p["gqa_qn_g"][l]), tile8(p["gqa_kn_g"][l])[:, :KV_WIDTH])
        s["qkv_b"] = _prep_fwd(z, *s["gb"], None, B, S, N_HEADS, cols_b, f"prep_b_fwd_{l}")
        s["qkv_d"] = _prep_fwd(z, *s["gd"], rope, B, S, KV_HEADS, cols_d, f"prep_d_fwd_{l}")
        s["ob"] = _attn_fwd(*s["qkv_b"], win, f"attn_b_fwd_{l}")
        s["od"] = _attn_fwd(*s["qkv_d"], None, f"attn_d_fwd_{l}")
        s["gmix"] = p["mix_norm_g"][l].reshape(1, 2048)
        s["ycat"] = _mix_fwd(s["ya"], s["ob"], s["yc"], s["od"], s["gmix"], B, S, f"mix_fwd_{l}")
        s["wout"] = fetch(l, "out", s["ycat"])["w_out"]
        x1 = s["x1"] = _matmul(s["ycat"], s["wout"], "nn", f"out_proj_{l}", res=x, tk=D_MODEL)
        s["h2"] = _rms_fwd(x1, p["norm2_g"][l], f"rms2_fwd_{l}")
        s["ffn"] = fetch(l, "ffn", s["h2"])
        s["gate"] = _mm_shard_out(s["h2"], s["ffn"]["w_gate"], "nn", f"ffn_gate_{l}", out_dtype=BF16)
        s["up"] = _mm_shard_out(s["h2"], s["ffn"]["w_up"], "nn", f"ffn_up_{l}", out_dtype=BF16)
        x, s["act"] = _ffn_down(s["gate"], s["up"], s["ffn"]["w_down"], x1, f"ffn_down_{l}")
        saved.append(s)

    loss_blk, dx = _loss_grad(x, target, "loss")
    g = {k: [None] * DEPTH for k in SMALL if k != "rel_bias"}
    dwin_total = None
    for l in reversed(range(DEPTH)):
        s = saved[l]
        z, ffn = s["z"], s["ffn"]
        dgate, dup = _ffn_down_dx(dx, ffn["w_down"], s["gate"], s["up"], f"ffn_down_dx_{l}")
        tok = emit(l, "w_down", _mm_shard_m(s["act"], dx, f"ffn_down_dw_{l}", out_dtype=BF16, tn=512, tk=T))
        tok += emit(l, "w_gate", _mm_shard_out(s["h2"], dgate, "tn", f"ffn_gate_dw_{l}", out_dtype=BF16, tm=1024, tk=T))
        tok += emit(l, "w_up", _mm_shard_out(s["h2"], dup, "tn", f"ffn_up_dw_{l}", out_dtype=BF16, tm=1024, tk=T))
        dh2 = _mm_shard_k([(dgate, ffn["w_gate"]), (dup, ffn["w_up"])], "nt", f"ffn_up_dx_{l}", tn=512, fold=N_DEV)
        dx1, dg2 = _rms_bwd(dh2, s["x1"], p["norm2_g"][l] + tok, dx, f"rms2_bwd_{l}")
        g["norm2_g"][l] = dg2[0]
        dycat = _matmul(dx1, s["wout"], "nt", f"out_proj_dx_{l}", tk=D_MODEL)
        tok = emit(l, "w_out", _matmul(s["ycat"], dx1, "tn", f"out_proj_dw_{l}", out_dtype=BF16, tn=1024, tk=T))
        dya, dob, dyc, dod, dgm = _mix_bwd(s["ya"], s["ob"], s["yc"], s["od"], dycat, s["gmix"] + tok, B, S, f"mix_bwd_{l}")
        g["mix_norm_g"][l] = dgm[0]
        dz_a, dws, dbias = _sgu_bwd(z, dya, s["ws"], jnp.swapaxes(s["ws"], 1, 2), s["bias"], f"sgu_bwd_{l}")
        g["sgu_w"][l] = dws
        g["sgu_b"][l] = dbias.reshape(128, 8, HEAD_DIM).sum(-1).T
        dc, dlg, dlb, dcb = _conv_bwd1(s["c"], dyc, p["conv_ln_g"][l].reshape(1, 512), p["conv_ln_b"][l].reshape(1, 512), f"conv_ln_bwd_{l}")
        g["conv_ln_g"][l], g["conv_ln_b"][l], g["conv_b"][l] = dlg[0], dlb[0], dcb[0]
        dz_ca, dz_cg, dcw = _conv_bwd2(z, dc, s["cw"], B, S, f"conv_bwd_{l}")
        g["conv_w"][l] = dcw.reshape(32, 512)[:CONV_WIDTH]
        dq, dk, dv, dwin = _attn_bwd(*s["qkv_b"], s["ob"], dob, win, f"attn_b_bwd_{l}")
        dwin_total = dwin if dwin_total is None else dwin_total + dwin
        dz_b, dgq, dgk = _prep_bwd(z, dq, dk, dv, *s["gb"], None, B, S, N_HEADS, cols_b, f"prep_b_bwd_{l}")
        g["dil_qn_g"][l] = dgq.reshape(N_HEADS, HEAD_DIM).sum(0)
        g["dil_kn_g"][l] = dgk.reshape(N_HEADS, HEAD_DIM).sum(0)
        dq, dk, dv = _attn_bwd(*s["qkv_d"], s["od"], dod, None, f"attn_d_bwd_{l}")
        dz_d, dgq, dgk = _prep_bwd(z, dq, dk, dv, *s["gd"], rope, B, S, KV_HEADS, cols_d, f"prep_d_bwd_{l}")
        g["gqa_qn_g"][l] = dgq.reshape(N_HEADS, HEAD_DIM).sum(0)
        g["gqa_kn_g"][l] = dgk.reshape(KV_HEADS, HEAD_DIM).sum(0)
        dz = jnp.concatenate([dz_a, dz_b, dz_ca, dz_cg, dz_d], axis=1)
        tok = emit(l, "w_in", _matmul(s["h"], dz, "tn", f"in_proj_dw_{l}", out_dtype=BF16, tk=T))
        dh = _matmul(dz, s["win"]["w_in"], "nt", f"in_proj_dx_{l}", tn=1024, tk=IN_WIDTH)
        dx, dg1 = _rms_bwd(dh, s["x"], p["norm1_g"][l] + tok, dx1, f"rms1_bwd_{l}")
        g["norm1_g"][l] = dg1[0]

    grads = {k: jnp.stack(v) for k, v in g.items()}
    grads["rel_bias"] = _bias_fold(dwin_total, S, "bias_fold")
    return loss_blk[0, 0], dx, grads


GROUPS = {"in": ("w_in",), "out": ("w_out",), "ffn": ("w_gate", "w_up", "w_down")}
COL_SHARDED = ("w_in", "w_gate", "w_up")


def kernel(x, rel_bias, norm1_g, w_in, sgu_w, sgu_b, dil_qn_g, dil_kn_g, conv_w, conv_b, conv_ln_g, conv_ln_b, gqa_qn_g, gqa_kn_g, mix_norm_g, w_out, norm2_g, w_gate, w_up, w_down, loss_target, m_rel_bias, m_norm1_g, m_w_in, m_sgu_w, m_sgu_b, m_dil_qn_g, m_dil_kn_g, m_conv_w, m_conv_b, m_conv_ln_g, m_conv_ln_b, m_gqa_qn_g, m_gqa_kn_g, m_mix_norm_g, m_w_out, m_norm2_g, m_w_gate, m_w_up, m_w_down, v_rel_bias, v_norm1_g, v_w_in, v_sgu_w, v_sgu_b, v_dil_qn_g, v_dil_kn_g, v_conv_w, v_conv_b, v_conv_ln_g, v_conv_ln_b, v_gqa_qn_g, v_gqa_kn_g, v_mix_norm_g, v_w_out, v_norm2_g, v_w_gate, v_w_up, v_w_down):
    w = dict(rel_bias=rel_bias, norm1_g=norm1_g, w_in=w_in, sgu_w=sgu_w, sgu_b=sgu_b, dil_qn_g=dil_qn_g, dil_kn_g=dil_kn_g, conv_w=conv_w,
             conv_b=conv_b, conv_ln_g=conv_ln_g, conv_ln_b=conv_ln_b, gqa_qn_g=gqa_qn_g, gqa_kn_g=gqa_kn_g, mix_norm_g=mix_norm_g,
             w_out=w_out, norm2_g=norm2_g, w_gate=w_gate, w_up=w_up, w_down=w_down)
    m = dict(rel_bias=m_rel_bias, norm1_g=m_norm1_g, w_in=m_w_in, sgu_w=m_sgu_w, sgu_b=m_sgu_b, dil_qn_g=m_dil_qn_g, dil_kn_g=m_dil_kn_g,
             conv_w=m_conv_w, conv_b=m_conv_b, conv_ln_g=m_conv_ln_g, conv_ln_b=m_conv_ln_b, gqa_qn_g=m_gqa_qn_g, gqa_kn_g=m_gqa_kn_g,
             mix_norm_g=m_mix_norm_g, w_out=m_w_out, norm2_g=m_norm2_g, w_gate=m_w_gate, w_up=m_w_up, w_down=m_w_down)
    v = dict(rel_bias=v_rel_bias, norm1_g=v_norm1_g, w_in=v_w_in, sgu_w=v_sgu_w, sgu_b=v_sgu_b, dil_qn_g=v_dil_qn_g, dil_kn_g=v_dil_kn_g,
             conv_w=v_conv_w, conv_b=v_conv_b, conv_ln_g=v_conv_ln_g, conv_ln_b=v_conv_ln_b, gqa_qn_g=v_gqa_qn_g, gqa_kn_g=v_gqa_kn_g,
             mix_norm_g=v_mix_norm_g, w_out=v_w_out, norm2_g=v_norm2_g, w_gate=v_w_gate, w_up=v_w_up, w_down=v_w_down)
    names = list(w)
    B, S, D = x.shape
    T = B * S
    me = 4 * lax.axis_index("x") + 2 * lax.axis_index("y") + lax.axis_index("c")

    bf = {k: w[k].astype(BF16) for k in LARGE}
    spreads, forwards = {}, {}

    def start_gather(l, group, after=None):
        srcs = [bf[k][l] for k in GROUPS[group]] + ([conv_w[l]] if group == "in" else [])
        spreads[l, group] = _spread_start(srcs, "chips", f"gather_{group}_{l}_start", after)
        return spreads[l, group][3][0, 0]

    def forward(l, group, after):
        sems, srcs, lands, _ = spreads[l, group]
        lands = _spread_wait(sems, srcs, lands, after, "chips", f"gather_{group}_{l}_wait")
        forwards[l, group] = _spread_start(None, "forward", f"forward_{group}_{l}_start", lands=lands)
        return forwards[l, group][3]

    def landed(l, group, after):
        sems, _, lands, _ = forwards[l, group]
        return _spread_wait(sems, [], lands, after, "forward", f"forward_{group}_{l}_wait")

    tok0 = start_gather(0, "in") + start_gather(0, "out") + start_gather(0, "ffn")
    small = {k: w[k] for k in SMALL}
    small["norm1_g"] = norm1_g.at[0].add(tok0)

    def mid(l, z):
        if l > 0:
            return jnp.zeros((), F32)
        return start_gather(1, "in", z) + start_gather(1, "out", z) + start_gather(1, "ffn", z)

    def fetch(l, group, after):
        if group == "in":
            tok = forward(0, "in", after) if l == 0 else after
        elif group == "out":
            tok = forward(l, "ffn", [after, forward(l, "out", after)])
        else:
            tok = forward(1, "in", after) if l == 0 else after
        got = dict(zip(GROUPS[group] + ("conv_w",), landed(l, group, [after, tok])))
        if group == "in":
            got["w_in"] = jnp.transpose(got["w_in"], (1, 0, 2)).reshape(D, IN_WIDTH)
            got["conv_w"] = jnp.transpose(got["conv_w"], (1, 0, 2)).reshape(CONV_WIDTH, 512)
        if group == "out":
            got["w_out"] = got["w_out"].reshape(D, D)
        return got

    scatters = {}

    def emit(l, k, dw):
        if k == "w_in":
            dw = jnp.transpose(dw.reshape(D, N_DEV, IN_WIDTH // N_DEV), (1, 0, 2))
        if k == "w_out":
            dw = dw.reshape(N_DEV, D // N_DEV, D)
        scatters[l, k] = _spread_start([dw], "scatter", f"scatter_{k}_{l}_start")
        return scatters[l, k][3][0, 0]

    loss_part, dx, grads = _local_step(x.reshape(T, D), loss_target.reshape(T, D), small, B, S, fetch, emit, mid)
    loss = lax.psum(loss_part, ("x", "y", "c"))

    out_g, out_d, out_m, out_v = {}, {}, {}, {}

    def update_large(k, after):
        shp = w[k].shape
        two_d = lambda a: a.reshape(-1, shp[-1])
        res = None
        for l in reversed(range(DEPTH)):
            sems, srcs, lands, _ = scatters[l, k]
            stack = _spread_wait(sems, srcs, lands, after, "scatter", f"scatter_{k}_{l}_wait")[0]
            res = _adamw(two_d(w[k]), two_d(m[k]), two_d(v[k]), stack.reshape(N_DEV, -1, shp[-1]), f"adamw_{k}_{l}", layer=l, prev=res)
        out_g[k], out_d[k], out_m[k], out_v[k] = [a.reshape(shp) for a in res]
        return res[0]

    flat2 = lambda a: a.reshape(-1, a.shape[-1])
    small_sems, small_srcs, small_lands, small_tok = _spread_start([flat2(grads[k]) for k in SMALL], "gather", "gather_small_grads_start")
    after = [dx, small_tok]
    for k in ("w_down", "w_gate", "w_up", "w_out"):
        after = update_large(k, after)
    stacks = _spread_wait(small_sems, small_srcs, small_lands, after, "gather", "gather_small_grads_wait")
    for k, stack in zip(SMALL, stacks):
        if k == "conv_w":
            stack = lax.dynamic_slice_in_dim(stack, me * (512 // N_DEV), 512 // N_DEV, axis=2)
        res = _adamw(flat2(w[k]), flat2(m[k]), flat2(v[k]), stack, f"adamw_{k}")
        out_g[k], out_d[k], out_m[k], out_v[k] = [a.reshape(w[k].shape) for a in res]
        after = res[0]
    update_large("w_in", after)

    return (loss, dx.reshape(B, S, D), *[out_g[k] for k in names], *[out_d[k] for k in names],
            *[out_m[k] for k in names], *[out_v[k] for k in names])
```

```python
import functools
import math

import numpy as np
import jax
import jax.numpy as jnp
from jax import lax
from jax.experimental import pallas as pl
from jax.experimental.pallas import tpu as pltpu

F32 = jnp.float32
BF16 = jnp.bfloat16
HIGHEST = lax.Precision.HIGHEST
MESH_ID = pl.DeviceIdType.MESH

D_MODEL = 2048
DEPTH = 2
HEAD_DIM = 64
GROUP_WIDTH = 512
N_HEADS = 8
KV_HEADS = 2
KV_WIDTH = 128
SGU_CHUNK = 128
CONV_WIDTH = 31
CONV_PAD = 16
GRID_W = 64
ROPE_THETA = 10000.0
REL_BUCKETS = 32
REL_MAX_DIST = 1024
DIL_PATTERNS = ((128, 1), (512, 4), (2048, 16))
FFN_HIDDEN = 5632
IN_WIDTH = 4352
RMS_EPS = 1e-6
LN_EPS = 1e-5
MASKED = -1e30
N_DEV = 8

ADAM_LR = 0.001
ADAM_B1 = 0.9
ADAM_B2 = 0.999
ADAM_EPS = 1e-08
ADAM_WD = 0.01
ADAM_STEP = 10

COL_AU, COL_AV, COL_BQ, COL_BK, COL_BV, COL_CA, COL_CG, COL_DQ = range(8)
COL_DK128, COL_DV128 = 32, 33

VMEM_LIMIT = 56 * 1024 * 1024
ATTN_TQ = 256


def _params(sem=None, vmem=VMEM_LIMIT):
    return pltpu.CompilerParams(dimension_semantics=sem, vmem_limit_bytes=vmem)


def _dot(a, b, dims, precision=None):
    return lax.dot_general(a, b, (dims, ((), ())), precision=precision, preferred_element_type=F32)


def _nn(a, b, precision=None):
    return _dot(a, b, ((1,), (0,)), precision)


def _nt(a, b):
    return _dot(a, b, ((1,), (1,)))


def _tn(a, b):
    return _dot(a, b, ((0,), (0,)))


DIMS = {"nn": ((1,), (0,)), "nt": ((1,), (1,)), "tn": ((0,), (0,))}


def _pick(n, cands):
    for c in cands:
        if n % c == 0:
            return c
    return n


def _mm_call(name, mode, pairs, specs, o_spec, out_sds, grid, acc_shape, res=None, fold=None):
    npair, nk, dims = len(pairs), grid[2], DIMS[mode]

    def body(*refs):
        ab = refs[:2 * npair]
        r_ref = refs[2 * npair] if res is not None else None
        o_ref = refs[2 * npair + (res is not None)]
        part = None
        for t in range(npair):
            for s in ([None] if fold is None else range(fold)):
                a_blk = ab[2 * t][...] if s is None else ab[2 * t][s]
                b_blk = ab[2 * t + 1][...] if s is None else ab[2 * t + 1][s]
                d = _dot(a_blk.astype(BF16), b_blk.astype(BF16), dims)
                part = d if part is None else part + d

        def finish(r):
            if r_ref is not None:
                r = r + r_ref[...]
            o_ref[...] = r.astype(o_ref.dtype)

        if nk == 1:
            finish(part)
            return
        acc, k = refs[-1], pl.program_id(2)

        @pl.when(k == 0)
        def _():
            acc[...] = part

        @pl.when(k > 0)
        def _():
            acc[...] += part

        @pl.when(k == nk - 1)
        def _():
            finish(acc[...])

    ins = [t for pair in pairs for t in pair]
    in_specs = [t for pair in specs for t in pair]
    if res is not None:
        ins.append(res)
        in_specs.append(o_spec)
    return pl.pallas_call(
        body, name=name, grid=grid, in_specs=in_specs, out_specs=o_spec, out_shape=out_sds,
        scratch_shapes=[pltpu.VMEM(acc_shape, F32)] if nk > 1 else [],
        compiler_params=_params(("parallel", "parallel", "arbitrary")),
    )(*ins)


def _matmul(a, b, mode, name, res=None, out_dtype=F32, tm=512, tn=None, tk=None):
    if mode == "nn":
        (M, K), N = a.shape, b.shape[1]
    elif mode == "nt":
        (M, K), N = a.shape, b.shape[0]
    else:
        (K, M), N = a.shape, b.shape[1]
    tm = min(tm, M)
    tn = tn or _pick(N, (2176, 2048, 1408, 1024, 512))
    tk = tk or _pick(K, (1024, 2176, 1408, 512))
    assert M % tm == 0 and N % tn == 0 and K % tk == 0, (M, N, K, tm, tn, tk)
    a_spec = pl.BlockSpec((tk, tm), lambda i, j, k: (k, i)) if mode == "tn" else pl.BlockSpec((tm, tk), lambda i, j, k: (i, k))
    b_spec = pl.BlockSpec((tn, tk), lambda i, j, k: (j, k)) if mode == "nt" else pl.BlockSpec((tk, tn), lambda i, j, k: (k, j))
    o_spec = pl.BlockSpec((tm, tn), lambda i, j, k: (i, j))
    return _mm_call(name, mode, [(a, b)], [(a_spec, b_spec)], o_spec, jax.ShapeDtypeStruct((M, N), out_dtype),
                    (M // tm, N // tn, K // tk), (tm, tn), res)


def _mm_shard_out(a, bs, mode, name, out_dtype=F32, tm=512, tk=None):
    J = bs.shape[0]
    n = bs.shape[1] if mode == "nt" else bs.shape[2]
    (K, M) = a.shape if mode == "tn" else a.shape[::-1]
    tm = min(tm, M)
    tk = tk or (K if mode != "tn" else _pick(K, (1024, 512)))
    a_spec = pl.BlockSpec((tk, tm), lambda j, i, k: (k, i)) if mode == "tn" else pl.BlockSpec((tm, tk), lambda j, i, k: (i, k))
    b_spec = pl.BlockSpec((None, n, tk), lambda j, i, k: (j, 0, k)) if mode == "nt" else pl.BlockSpec((None, tk, n), lambda j, i, k: (j, k, 0))
    o_spec = pl.BlockSpec((None, tm, n), lambda j, i, k: (j, i, 0))
    return _mm_call(name, mode, [(a, bs)], [(a_spec, b_spec)], o_spec, jax.ShapeDtypeStruct((J, M, n), out_dtype),
                    (J, M // tm, K // tk), (tm, n))


def _mm_shard_k(pairs, mode, name, res=None, out_dtype=F32, tm=512, tn=None, fold=1):
    J, M, n = pairs[0][0].shape
    N = pairs[0][1].shape[2] if mode == "nn" else pairs[0][1].shape[1]
    tm = min(tm, M)
    tn = tn or _pick(N, (2048, 1024, 512))
    a_spec = pl.BlockSpec((fold, tm, n), lambda i, j, k: (k, i, 0))
    b_spec = pl.BlockSpec((fold, n, tn), lambda i, j, k: (k, 0, j)) if mode == "nn" else pl.BlockSpec((fold, tn, n), lambda i, j, k: (k, j, 0))
    o_spec = pl.BlockSpec((tm, tn), lambda i, j, k: (i, j))
    return _mm_call(name, mode, pairs, [(a_spec, b_spec)] * len(pairs), o_spec, jax.ShapeDtypeStruct((M, N), out_dtype),
                    (M // tm, N // tn, J // fold), (tm, tn), res, fold)


def _mm_shard_m(as_, b, name, out_dtype=F32, tn=None, tk=512):
    J, K, n = as_.shape
    N = b.shape[1]
    tn = tn or _pick(N, (2048, 1024, 512))
    tk = min(tk, K)
    a_spec = pl.BlockSpec((None, tk, n), lambda j, i, k: (j, k, 0))
    b_spec = pl.BlockSpec((tk, tn), lambda j, i, k: (k, i))
    o_spec = pl.BlockSpec((None, n, tn), lambda j, i, k: (j, 0, i))
    return _mm_call(name, "tn", [(as_, b)], [(a_spec, b_spec)], o_spec, jax.ShapeDtypeStruct((J, n, N), out_dtype),
                    (J, N // tn, K // tk), (n, tn))


def _seg_matrix(width):
    return jnp.asarray(np.kron(np.eye(width // HEAD_DIM, dtype=np.float32), np.full((HEAD_DIM, HEAD_DIM), 1.0 / HEAD_DIM, np.float32)), BF16)


def _segmean(v, p):
    hi = v.astype(BF16)
    r = v - hi.astype(F32)
    mid = r.astype(BF16)
    lo = (r - mid.astype(F32)).astype(BF16)
    w = min(256, v.shape[1])
    pw = p[:w, :w]
    halves = []
    for c in range(v.shape[1] // w):
        cols = slice(c * w, (c + 1) * w)
        halves.append(_nn(hi[:, cols], pw) + _nn(mid[:, cols], pw) + _nn(lo[:, cols], pw))
    return halves[0] if len(halves) == 1 else jnp.concatenate(halves, axis=1)


def _gelu(x):
    c0 = math.sqrt(2.0 / math.pi)
    t = jnp.tanh(c0 * (x + 0.044715 * x * x * x))
    return 0.5 * x * (1.0 + t), t


def _gelu_grad(x, t):
    c0 = math.sqrt(2.0 / math.pi)
    return 0.5 * (1.0 + t) + 0.5 * x * (1.0 - t * t) * c0 * (1.0 + 3.0 * 0.044715 * x * x)


def _sigmoid(x):
    return 1.0 / (1.0 + jnp.exp(-x))


def _rms_fwd(x, g, name):
    T, D = x.shape
    tm = min(256, T)

    def body(x_ref, g_ref, o_ref):
        xv = x_ref[...]
        r = lax.rsqrt(jnp.mean(xv * xv, axis=-1, keepdims=True) + RMS_EPS)
        o_ref[...] = (xv * r * g_ref[...]).astype(BF16)

    return pl.pallas_call(
        body, name=name, grid=(T // tm,),
        in_specs=[pl.BlockSpec((tm, D), lambda i: (i, 0)), pl.BlockSpec((1, D), lambda i: (0, 0))],
        out_specs=pl.BlockSpec((tm, D), lambda i: (i, 0)), out_shape=jax.ShapeDtypeStruct((T, D), BF16),
        compiler_params=_params(("parallel",)),
    )(x, g.reshape(1, D))


def _rms_bwd(dh, x, g, dres, name):
    T, D = x.shape
    tm = min(256, T)

    def body(dh_ref, x_ref, g_ref, dres_ref, dx_ref, dg_ref):
        @pl.when(pl.program_id(0) == 0)
        def _():
            dg_ref[...] = jnp.zeros_like(dg_ref)

        xv, dhv = x_ref[...], dh_ref[...]
        r = lax.rsqrt(jnp.mean(xv * xv, axis=-1, keepdims=True) + RMS_EPS)
        y = xv * r
        dy = dhv * g_ref[...]
        dx_ref[...] = dres_ref[...] + r * (dy - y * jnp.mean(dy * y, axis=-1, keepdims=True))
        dg_ref[...] += jnp.sum(dhv * y, axis=0, keepdims=True)

    row = pl.BlockSpec((tm, D), lambda i: (i, 0))
    vec = pl.BlockSpec((1, D), lambda i: (0, 0))
    return pl.pallas_call(
        body, name=name, grid=(T // tm,), in_specs=[row, row, vec, row], out_specs=[row, vec],
        out_shape=[jax.ShapeDtypeStruct((T, D), F32), jax.ShapeDtypeStruct((1, D), F32)],
        compiler_params=_params(("arbitrary",)),
    )(dh, x, g.reshape(1, D), dres)


def _sgu_core(zu, zv, ws_ref, bias, p):
    ug, tu = _gelu(zu)
    vg, tv = _gelu(zv)
    xc = vg - _segmean(vg, p)
    rs = lax.rsqrt(_segmean(xc * xc, p) + LN_EPS)
    vn = xc * rs
    vnb = vn.astype(BF16)
    low = lax.broadcasted_iota(jnp.int32, (SGU_CHUNK, 128), 1) < HEAD_DIM
    parts = []
    for j in range(4):
        vp = vnb[:, 128 * j:128 * (j + 1)]
        parts.append(jnp.where(low, _nn(ws_ref[2 * j], vp), _nn(ws_ref[2 * j + 1], vp)))
    mixed = jnp.concatenate(parts, axis=1) + bias
    return ug, tu, tv, rs, vn, vnb, mixed, low


def _sgu_fwd(z, ws, bias, name):
    T = z.shape[0]

    def body(zu_ref, zv_ref, ws_ref, b_ref, p_ref, y_ref):
        ug, _, _, _, _, _, mixed, _ = _sgu_core(zu_ref[...], zv_ref[...], ws_ref, b_ref[...], p_ref[...])
        y_ref[...] = ug * mixed

    full = lambda shape: pl.BlockSpec(shape, lambda i: (0,) * len(shape))
    return pl.pallas_call(
        body, name=name, grid=(T // SGU_CHUNK,),
        in_specs=[pl.BlockSpec((SGU_CHUNK, 512), lambda i: (i, COL_AU)), pl.BlockSpec((SGU_CHUNK, 512), lambda i: (i, COL_AV)),
                  full((8, 128, 128)), full((128, 512)), full((512, 512))],
        out_specs=pl.BlockSpec((SGU_CHUNK, 512), lambda i: (i, 0)), out_shape=jax.ShapeDtypeStruct((T, 512), F32),
        compiler_params=_params(("parallel",)),
    )(z, z, ws, bias, _seg_matrix(512))


def _sgu_bwd(z, dy, ws, ws_t, bias, name):
    T = z.shape[0]

    def body(zu_ref, zv_ref, dy_ref, ws_ref, wst_ref, b_ref, p_ref, dz_ref, dws_ref, db_ref):
        @pl.when(pl.program_id(0) == 0)
        def _():
            dws_ref[...] = jnp.zeros_like(dws_ref)
            db_ref[...] = jnp.zeros_like(db_ref)

        zu, zv, p = zu_ref[...], zv_ref[...], p_ref[...]
        ug, tu, tv, rs, vn, vnb, mixed, low = _sgu_core(zu, zv, ws_ref, b_ref[...], p)
        dyv = dy_ref[...]
        dmixed = dyv * ug
        db_ref[...] += dmixed
        dmb = dmixed.astype(BF16)
        zero = jnp.zeros((SGU_CHUNK, 128), BF16)
        parts = []
        for j in range(4):
            dmp, vp = dmb[:, 128 * j:128 * (j + 1)], vnb[:, 128 * j:128 * (j + 1)]
            dws_ref[2 * j] += _nt(jnp.where(low, dmp, zero), vp)
            dws_ref[2 * j + 1] += _nt(jnp.where(low, zero, dmp), vp)
            parts.append(jnp.where(low, _nn(wst_ref[2 * j], dmp), _nn(wst_ref[2 * j + 1], dmp)))
        dvn = jnp.concatenate(parts, axis=1)
        dvg = rs * (dvn - _segmean(dvn, p) - vn * _segmean(dvn * vn, p))
        dz_ref[:, 0:512] = (dyv * mixed * _gelu_grad(zu, tu)).astype(BF16)
        dz_ref[:, 512:1024] = (dvg * _gelu_grad(zv, tv)).astype(BF16)

    full = lambda shape: pl.BlockSpec(shape, lambda i: (0,) * len(shape))
    return pl.pallas_call(
        body, name=name, grid=(T // SGU_CHUNK,),
        in_specs=[pl.BlockSpec((SGU_CHUNK, 512), lambda i: (i, COL_AU)), pl.BlockSpec((SGU_CHUNK, 512), lambda i: (i, COL_AV)),
                  pl.BlockSpec((SGU_CHUNK, 512), lambda i: (i, 0)), full((8, 128, 128)), full((8, 128, 128)), full((128, 512)), full((512, 512))],
        out_specs=[pl.BlockSpec((SGU_CHUNK, 1024), lambda i: (i, 0)), full((8, 128, 128)), full((128, 512))],
        out_shape=[jax.ShapeDtypeStruct((T, 1024), BF16), jax.ShapeDtypeStruct((8, 128, 128), F32), jax.ShapeDtypeStruct((128, 512), F32)],
        compiler_params=_params(("arbitrary",)),
    )(z, z, dy, ws, ws_t, bias, _seg_matrix(512))


CONV_ROWS = 256


def _conv_taps(pad_ref, w_ref, base, flip):
    blk = pad_ref[pl.ds(base, CONV_ROWS + 2 * CONV_PAD), :]
    acc = jnp.zeros((CONV_ROWS, blk.shape[1]), F32)
    for k in range(CONV_WIDTH):
        wk = w_ref[CONV_WIDTH - 1 - k if flip else k]
        acc = acc + wk * blk[k + 1:k + 1 + CONV_ROWS, :]
    return acc


def _conv_fwd1(z, w, cb, B, S, name):
    T = B * S
    rows = min(CONV_ROWS, S)
    assert rows == CONV_ROWS

    def body(a_ref, g_ref, w_ref, cb_ref, c_ref, pad):
        pad[0:CONV_PAD, :] = jnp.zeros((CONV_PAD, 128), F32)
        pad[CONV_PAD + S:2 * CONV_PAD + S, :] = jnp.zeros((CONV_PAD, 128), F32)
        pad[CONV_PAD:CONV_PAD + S, :] = a_ref[...] * _sigmoid(g_ref[...])

        def tile(r, carry):
            base = pl.multiple_of(r * CONV_ROWS, CONV_ROWS)
            c_ref[pl.ds(base, CONV_ROWS), :] = _conv_taps(pad, w_ref, base, False) + cb_ref[...]
            return carry

        lax.fori_loop(0, S // CONV_ROWS, tile, 0)

    return pl.pallas_call(
        body, name=name, grid=(4, B),
        in_specs=[pl.BlockSpec((S, 128), lambda j, b: (b, 4 * COL_CA + j)), pl.BlockSpec((S, 128), lambda j, b: (b, 4 * COL_CG + j)),
                  pl.BlockSpec((32, 1, 128), lambda j, b: (0, 0, j)), pl.BlockSpec((1, 128), lambda j, b: (0, j))],
        out_specs=pl.BlockSpec((S, 128), lambda j, b: (b, j)), out_shape=jax.ShapeDtypeStruct((T, 512), F32),
        scratch_shapes=[pltpu.VMEM((S + 2 * CONV_PAD, 128), F32)], compiler_params=_params(("parallel", "parallel")),
    )(z, z, w, cb)


def _ln_rows(c):
    mu = jnp.mean(c, axis=-1, keepdims=True)
    xc = c - mu
    rs = lax.rsqrt(jnp.mean(xc * xc, axis=-1, keepdims=True) + LN_EPS)
    return xc * rs, rs


def _conv_fwd2(c, lng, lnb, name):
    T = c.shape[0]
    tm = min(512, T)

    def body(c_ref, g_ref, b_ref, y_ref):
        n, _ = _ln_rows(c_ref[...])
        t = n * g_ref[...] + b_ref[...]
        y_ref[...] = t * _sigmoid(t)

    row = pl.BlockSpec((tm, 512), lambda i: (i, 0))
    vec = pl.BlockSpec((1, 512), lambda i: (0, 0))
    return pl.pallas_call(body, name=name, grid=(T // tm,), in_specs=[row, vec, vec], out_specs=row,
                          out_shape=jax.ShapeDtypeStruct((T, 512), F32), compiler_params=_params(("parallel",)))(c, lng, lnb)


def _conv_bwd1(c, dy, lng, lnb, name):
    T = c.shape[0]
    tm = min(512, T)

    def body(c_ref, dy_ref, g_ref, b_ref, dc_ref, dg_ref, db_ref, dcb_ref):
        @pl.when(pl.program_id(0) == 0)
        def _():
            dg_ref[...] = jnp.zeros_like(dg_ref)
            db_ref[...] = jnp.zeros_like(db_ref)
            dcb_ref[...] = jnp.zeros_like(dcb_ref)

        n, rs = _ln_rows(c_ref[...])
        t = n * g_ref[...] + b_ref[...]
        s = _sigmoid(t)
        dt = dy_ref[...] * s * (1.0 + t * (1.0 - s))
        dg_ref[...] += jnp.sum(dt * n, axis=0, keepdims=True)
        db_ref[...] += jnp.sum(dt, axis=0, keepdims=True)
        dn = dt * g_ref[...]
        dc = rs * (dn - jnp.mean(dn, axis=-1, keepdims=True) - n * jnp.mean(dn * n, axis=-1, keepdims=True))
        dc_ref[...] = dc
        dcb_ref[...] += jnp.sum(dc, axis=0, keepdims=True)

    row = pl.BlockSpec((tm, 512), lambda i: (i, 0))
    vec = pl.BlockSpec((1, 512), lambda i: (0, 0))
    vshape = jax.ShapeDtypeStruct((1, 512), F32)
    return pl.pallas_call(body, name=name, grid=(T // tm,), in_specs=[row, row, vec, vec], out_specs=[row, vec, vec, vec],
                          out_shape=[jax.ShapeDtypeStruct((T, 512), F32), vshape, vshape, vshape],
                          compiler_params=_params(("arbitrary",)))(c, dy, lng, lnb)


def _conv_bwd2(z, dc, w, B, S, name):
    T = B * S

    def body(a_ref, g_ref, dc_ref, w_ref, da_ref, dg_ref, dw_ref, hpad, dpad, dwacc):
        @pl.when(pl.program_id(1) == 0)
        def _():
            dw_ref[...] = jnp.zeros_like(dw_ref)

        zeros = jnp.zeros((CONV_PAD, 128), F32)
        for ref in (hpad, dpad):
            ref[0:CONV_PAD, :] = zeros
            ref[CONV_PAD + S:2 * CONV_PAD + S, :] = zeros
        hpad[CONV_PAD:CONV_PAD + S, :] = a_ref[...] * _sigmoid(g_ref[...])
        dpad[CONV_PAD:CONV_PAD + S, :] = dc_ref[...]
        dwacc[...] = jnp.zeros_like(dwacc)

        def tile(r, carry):
            base = pl.multiple_of(r * CONV_ROWS, CONV_ROWS)
            dh = _conv_taps(dpad, w_ref, base, True)
            av, gv = a_ref[pl.ds(base, CONV_ROWS), :], g_ref[pl.ds(base, CONV_ROWS), :]
            sg = _sigmoid(gv)
            da_ref[pl.ds(base, CONV_ROWS), :] = (dh * sg).astype(BF16)
            dg_ref[pl.ds(base, CONV_ROWS), :] = (dh * av * sg * (1.0 - sg)).astype(BF16)
            dcv = dc_ref[pl.ds(base, CONV_ROWS), :]
            blk = hpad[pl.ds(base, CONV_ROWS + 2 * CONV_PAD), :]
            for k in range(CONV_WIDTH):
                prod = dcv * blk[k + 1:k + 1 + CONV_ROWS, :]
                dwacc[k] += jnp.sum(prod.reshape(CONV_ROWS // 8, 8, 128), axis=0)
            return carry

        lax.fori_loop(0, S // CONV_ROWS, tile, 0)
        for k in range(CONV_WIDTH):
            dw_ref[k] += jnp.sum(dwacc[k], axis=0, keepdims=True)

    return pl.pallas_call(
        body, name=name, grid=(4, B),
        in_specs=[pl.BlockSpec((S, 128), lambda j, b: (b, 4 * COL_CA + j)), pl.BlockSpec((S, 128), lambda j, b: (b, 4 * COL_CG + j)),
                  pl.BlockSpec((S, 128), lambda j, b: (b, j)), pl.BlockSpec((32, 1, 128), lambda j, b: (0, 0, j))],
        out_specs=[pl.BlockSpec((S, 128), lambda j, b: (b, j)), pl.BlockSpec((S, 128), lambda j, b: (b, j)),
                   pl.BlockSpec((32, 1, 128), lambda j, b: (0, 0, j))],
        out_shape=[jax.ShapeDtypeStruct((T, 512), BF16), jax.ShapeDtypeStruct((T, 512), BF16), jax.ShapeDtypeStruct((32, 1, 512), F32)],
        scratch_shapes=[pltpu.VMEM((S + 2 * CONV_PAD, 128), F32), pltpu.VMEM((S + 2 * CONV_PAD, 128), F32), pltpu.VMEM((32, 8, 128), F32)],
        compiler_params=_params(("parallel", "arbitrary")),
    )(z, z, dc, w)


def _swap16(x):
    n = x.shape[1]
    first = (lax.broadcasted_iota(jnp.int32, x.shape, 1) % 32) < 16
    return jnp.where(first, pltpu.roll(x, n - 16, 1), pltpu.roll(x, 16, 1))


def _rope(x, cos, sin):
    return x * cos + _swap16(x) * sin


def _rope_t(dy, cos, sin):
    return dy * cos + _swap16(dy * sin)


def _qk_norm(x, p):
    r = lax.rsqrt(_segmean(x * x, p) + RMS_EPS)
    return x * r, r


def _store_heads(ref, val, n):
    for h in range(n):
        ref[h] = val[:, HEAD_DIM * h:HEAD_DIM * (h + 1)].astype(ref.dtype)


def _load_heads(ref, n):
    return jnp.concatenate([ref[h] for h in range(n)], axis=1)


def _prep_fwd(z, gq, gk, rope, B, S, kv_heads, cols, name):
    tm = min(256, S)
    ns = S // tm
    kw = kv_heads * HEAD_DIM
    scale = HEAD_DIM ** -0.5
    qc, kc, vc = cols

    def body(*refs):
        if rope is None:
            q_ref, k_ref, v_ref, gq_ref, gk_ref, p_ref, qo, ko, vo = refs
        else:
            q_ref, k_ref, v_ref, gq_ref, gk_ref, p_ref, cos_ref, sin_ref, qo, ko, vo = refs
        p = p_ref[...]
        qn, _ = _qk_norm(q_ref[...], p)
        kn, _ = _qk_norm(k_ref[...], p[:kw, :kw])
        qn, kn = qn * gq_ref[...], kn * gk_ref[...]
        if rope is not None:
            cos, sin = cos_ref[...], sin_ref[...]
            qn, kn = _rope(qn, cos, sin), _rope(kn, cos[:, :kw], sin[:, :kw])
        _store_heads(qo, qn * scale, N_HEADS)
        _store_heads(ko, kn, kv_heads)
        _store_heads(vo, v_ref[...], kv_heads)

    row = lambda w, c: pl.BlockSpec((tm, w), lambda b, i: (b * ns + i, c))
    const = lambda shape: pl.BlockSpec(shape, lambda b, i: (0,) * len(shape))
    heads = lambda n: pl.BlockSpec((None, n, tm, HEAD_DIM), lambda b, i: (b, 0, i, 0))
    ins = [z, z, z, gq, gk, _seg_matrix(512)]
    specs = [row(512, qc), row(kw, kc), row(kw, vc), const((1, 512)), const((1, kw)), const((512, 512))]
    if rope is not None:
        ins += list(rope)
        specs += [pl.BlockSpec((tm, 512), lambda b, i: (i, 0))] * 2
    return pl.pallas_call(
        body, name=name, grid=(B, ns), in_specs=specs, out_specs=[heads(N_HEADS), heads(kv_heads), heads(kv_heads)],
        out_shape=[jax.ShapeDtypeStruct((B, N_HEADS, S, HEAD_DIM), BF16), jax.ShapeDtypeStruct((B, kv_heads, S, HEAD_DIM), BF16),
                   jax.ShapeDtypeStruct((B, kv_heads, S, HEAD_DIM), BF16)],
        compiler_params=_params(("parallel", "parallel")),
    )(*ins)


def _prep_bwd(z, dq, dk, dv, gq, gk, rope, B, S, kv_heads, cols, name):
    T = B * S
    tm = min(256, S)
    ns = S // tm
    kw = kv_heads * HEAD_DIM
    scale = HEAD_DIM ** -0.5
    qc, kc, _ = cols

    def body(*refs):
        if rope is None:
            q_ref, k_ref, dq_ref, dk_ref, dv_ref, gq_ref, gk_ref, p_ref, dz_ref, dgq_ref, dgk_ref = refs
        else:
            q_ref, k_ref, dq_ref, dk_ref, dv_ref, gq_ref, gk_ref, p_ref, cos_ref, sin_ref, dz_ref, dgq_ref, dgk_ref = refs

        @pl.when((pl.program_id(0) == 0) & (pl.program_id(1) == 0))
        def _():
            dgq_ref[...] = jnp.zeros_like(dgq_ref)
            dgk_ref[...] = jnp.zeros_like(dgk_ref)

        p = p_ref[...]
        dqv = _load_heads(dq_ref, N_HEADS) * scale
        dkv = _load_heads(dk_ref, kv_heads)
        if rope is not None:
            cos, sin = cos_ref[...], sin_ref[...]
            dqv, dkv = _rope_t(dqv, cos, sin), _rope_t(dkv, cos[:, :kw], sin[:, :kw])

        def through_norm(xv, dy, g, pm, dg_ref):
            xh, r = _qk_norm(xv, pm)
            dg_ref[...] += jnp.sum(dy * xh, axis=0, keepdims=True)
            dxh = dy * g
            return r * (dxh - xh * _segmean(dxh * xh, pm))

        dz_ref[:, 0:512] = through_norm(q_ref[...], dqv, gq_ref[...], p, dgq_ref).astype(BF16)
        dz_ref[:, 512:512 + kw] = through_norm(k_ref[...], dkv, gk_ref[...], p[:kw, :kw], dgk_ref).astype(BF16)
        dz_ref[:, 512 + kw:512 + 2 * kw] = _load_heads(dv_ref, kv_heads).astype(BF16)

    row = lambda w, c: pl.BlockSpec((tm, w), lambda b, i: (b * ns + i, c))
    const = lambda shape: pl.BlockSpec(shape, lambda b, i: (0,) * len(shape))
    heads = lambda n: pl.BlockSpec((None, n, tm, HEAD_DIM), lambda b, i: (b, 0, i, 0))
    ins = [z, z, dq, dk, dv, gq, gk, _seg_matrix(512)]
    specs = [row(512, qc), row(kw, kc), heads(N_HEADS), heads(kv_heads), heads(kv_heads), const((1, 512)), const((1, kw)), const((512, 512))]
    if rope is not None:
        ins += list(rope)
        specs += [pl.BlockSpec((tm, 512), lambda b, i: (i, 0))] * 2
    return pl.pallas_call(
        body, name=name, grid=(B, ns), in_specs=specs, out_specs=[row(512 + 2 * kw, 0), const((1, 512)), const((1, kw))],
        out_shape=[jax.ShapeDtypeStruct((T, 512 + 2 * kw), BF16), jax.ShapeDtypeStruct((1, 512), F32), jax.ShapeDtypeStruct((1, kw), F32)],
        compiler_params=_params(("arbitrary", "arbitrary")),
    )(*ins)


def _toeplitz(win, tq, S):
    r = pltpu.roll(jnp.broadcast_to(win, (tq, S + tq)), 0, 1, stride=1, stride_axis=0)
    return r[:, tq:tq + S]


ATTN_HEADS = 4


def _attn_fwd(q, k, v, win, name, nh=ATTN_HEADS):
    B, H, S, _ = q.shape
    shared = k.shape[1] != H
    assert not shared or H // k.shape[1] == nh
    tq = min(ATTN_TQ, S)

    def body(*refs):
        if win is None:
            q_ref, k_ref, v_ref, o_ref = refs
        else:
            q_ref, k_ref, v_ref, w_ref, o_ref = refs
        kvs = [(k_ref[...], v_ref[...])] * nh if shared else [(k_ref[h], v_ref[h]) for h in range(nh)]
        scores = []
        for h in range(nh):
            s = _nt(q_ref[h], kvs[h][0])
            if win is not None:
                s = s + _toeplitz(w_ref[h], tq, S)
            scores.append(s)
        probs = []
        for s in scores:
            p = jnp.exp(s - jnp.max(s, axis=-1, keepdims=True))
            probs.append((p.astype(BF16), jnp.sum(p, axis=-1, keepdims=True)))
        for h, (p, l) in enumerate(probs):
            o_ref[h] = _nn(p, kvs[h][1]) / l

    qs = pl.BlockSpec((None, nh, tq, HEAD_DIM), lambda b, h, i: (b, h, i, 0))
    ks = (pl.BlockSpec((None, None, S, HEAD_DIM), lambda b, h, i: (b, h, 0, 0)) if shared
          else pl.BlockSpec((None, nh, S, HEAD_DIM), lambda b, h, i: (b, h, 0, 0)))
    ins, specs = [q, k, v], [qs, ks, ks]
    if win is not None:
        ins.append(win)
        specs.append(pl.BlockSpec((nh, None, 1, S + tq), lambda b, h, i: (h, i, 0, 0)))
    return pl.pallas_call(body, name=name, grid=(B, H // nh, S // tq), in_specs=specs, out_specs=qs,
                          out_shape=jax.ShapeDtypeStruct((B, H, S, HEAD_DIM), F32),
                          compiler_params=_params(("parallel", "parallel", "parallel")))(*ins)


def _attn_bwd(q, k, v, o, do, win, name, nh=ATTN_HEADS):
    B, H, S, _ = q.shape
    hkv = k.shape[1]
    shared = hkv != H
    assert not shared or H // hkv == nh
    tq = min(ATTN_TQ, S)
    nq = S // tq

    def body(*refs):
        if win is None:
            q_ref, k_ref, v_ref, o_ref, do_ref, dq_ref, dk_ref, dv_ref = refs
        else:
            q_ref, k_ref, v_ref, o_ref, do_ref, w_ref, rev_ref, dq_ref, dk_ref, dv_ref, dw_ref = refs

        @pl.when(pl.program_id(2) == 0)
        def _():
            dk_ref[...] = jnp.zeros_like(dk_ref)
            dv_ref[...] = jnp.zeros_like(dv_ref)

        kvs = [(k_ref[...], v_ref[...])] * nh if shared else [(k_ref[h], v_ref[h]) for h in range(nh)]
        qvs, dobs, scores, dps = [], [], [], []
        for h in range(nh):
            qv, dov = q_ref[h], do_ref[h]
            dob = dov.astype(BF16)
            s = _nt(qv, kvs[h][0])
            if win is not None:
                s = s + _toeplitz(w_ref[h], tq, S)
            dp = _nt(dob, kvs[h][1]) - jnp.sum(dov * o_ref[h], axis=-1, keepdims=True)
            qvs.append(qv)
            dobs.append(dob)
            scores.append(s)
            dps.append(dp)
        pbs, dsbs = [], []
        for s, dp in zip(scores, dps):
            p = jnp.exp(s - jnp.max(s, axis=-1, keepdims=True))
            p = p * (1.0 / jnp.sum(p, axis=-1, keepdims=True))
            pbs.append(p.astype(BF16))
            dsbs.append((p * dp).astype(BF16))
        dk_acc = dv_acc = None
        for h in range(nh):
            dvh, dkh = _tn(pbs[h], dobs[h]), _tn(dsbs[h], qvs[h])
            dq_ref[h] = _nn(dsbs[h], kvs[h][0])
            if shared:
                dv_acc = dvh if dv_acc is None else dv_acc + dvh
                dk_acc = dkh if dk_acc is None else dk_acc + dkh
            else:
                dv_ref[h] += dvh
                dk_ref[h] += dkh
            if win is not None:
                rev = _nn(rev_ref[...], dsbs[h])
                wide = jnp.concatenate([rev, jnp.zeros((tq, tq), F32)], axis=1)
                dw_ref[h] = jnp.sum(pltpu.roll(wide, 0, 1, stride=1, stride_axis=0), axis=0, keepdims=True)
        if shared:
            dv_ref[...] += dv_acc
            dk_ref[...] += dk_acc

    qs = pl.BlockSpec((None, nh, tq, HEAD_DIM), lambda b, h, i: (b, h, i, 0))
    ks = (pl.BlockSpec((None, None, S, HEAD_DIM), lambda b, h, i: (b, h, 0, 0)) if shared
          else pl.BlockSpec((None, nh, S, HEAD_DIM), lambda b, h, i: (b, h, 0, 0)))
    ins, specs = [q, k, v, o, do], [qs, ks, ks, qs, qs]
    outs = [jax.ShapeDtypeStruct((B, H, S, HEAD_DIM), F32), jax.ShapeDtypeStruct((B, hkv, S, HEAD_DIM), F32), jax.ShapeDtypeStruct((B, hkv, S, HEAD_DIM), F32)]
    ospecs = [qs, ks, ks]
    if win is not None:
        ins += [win, jnp.asarray(np.eye(tq, dtype=np.float32)[::-1].copy(), BF16)]
        specs += [pl.BlockSpec((nh, None, 1, S + tq), lambda b, h, i: (h, i, 0, 0)), pl.BlockSpec((tq, tq), lambda b, h, i: (0, 0))]
        outs.append(jax.ShapeDtypeStruct((B, H, nq, 1, S + tq), F32))
        ospecs.append(pl.BlockSpec((None, nh, None, 1, S + tq), lambda b, h, i: (b, h, i, 0, 0)))
    return pl.pallas_call(body, name=name, grid=(B, H // nh, nq), in_specs=specs, out_specs=ospecs, out_shape=outs,
                          compiler_params=_params(("parallel", "parallel", "arbitrary")))(*ins)


def _pattern_count(delta):
    n = jnp.zeros(delta.shape, jnp.int32)
    for window, dil in DIL_PATTERNS:
        n = n + ((delta % dil == 0) & (jnp.abs(delta) <= window // 2)).astype(jnp.int32)
    return n


def _t5_bucket(rel):
    nb = REL_BUCKETS // 2
    max_exact = nb // 2
    ret = jnp.where(rel > 0, nb, 0)
    n = jnp.abs(rel)
    nf = jnp.maximum(n, 1).astype(F32)
    large = max_exact + (jnp.log(nf / max_exact) / math.log(REL_MAX_DIST / max_exact) * (nb - max_exact)).astype(jnp.int32)
    large = jnp.minimum(large, nb - 1)
    return ret + jnp.where(n < max_exact, n, large)


def _bias_windows(rel_bias, S):
    tq = min(ATTN_TQ, S)
    nq = S // tq
    n = nq * (S + tq)
    delta = (jnp.arange(S + tq)[None, :] - (jnp.arange(nq)[:, None] + 1) * tq).reshape(n)
    count = _pattern_count(delta)
    onehot = (_t5_bucket(delta)[None, :] == jnp.arange(REL_BUCKETS)[:, None]).astype(F32)
    extra = jnp.where(count > 0, jnp.log(jnp.maximum(count, 1).astype(F32)), MASKED).reshape(1, n)
    live = (count > 0).astype(F32).reshape(1, n)

    def body(t_ref, oh_ref, live_ref, extra_ref, o_ref):
        o_ref[...] = _nn(t_ref[...], oh_ref[...], HIGHEST) * live_ref[...] + extra_ref[...]

    val = pl.pallas_call(body, name="bias_windows", out_shape=jax.ShapeDtypeStruct((N_HEADS, n), F32),
                         compiler_params=_params())(rel_bias.T, onehot, live, extra)
    return val.reshape(N_HEADS, nq, 1, S + tq)


def _bias_fold(dwin, S, name):
    B, H, nq = dwin.shape[:3]
    tq = min(ATTN_TQ, S)
    n = nq * (S + tq)
    delta = (jnp.arange(S + tq)[None, :] - (tq - 1) - jnp.arange(nq)[:, None] * tq).reshape(n)
    onehot = (_t5_bucket(delta)[:, None] == jnp.arange(128)[None, :]).astype(F32)

    def body(d_ref, oh_ref, o_ref):
        tot = d_ref[0]
        for b in range(1, B):
            tot = tot + d_ref[b]
        o_ref[...] = _nn(tot, oh_ref[...], HIGHEST)

    out = pl.pallas_call(body, name=name, out_shape=jax.ShapeDtypeStruct((H, 128), F32), compiler_params=_params())(dwin.reshape(B, H, n), onehot)
    return out[:, :REL_BUCKETS].T


def _rope_tables(S):
    half = 16
    freqs = ROPE_THETA ** (-jnp.arange(half, dtype=F32) / half)
    t = jnp.arange(S)
    ang_r = (t // GRID_W).astype(F32)[:, None] * freqs[None, :]
    ang_c = (t % GRID_W).astype(F32)[:, None] * freqs[None, :]
    cos = jnp.concatenate([jnp.cos(ang_r)] * 2 + [jnp.cos(ang_c)] * 2, axis=1)
    sin = jnp.concatenate([-jnp.sin(ang_r), jnp.sin(ang_r), -jnp.sin(ang_c), jnp.sin(ang_c)], axis=1)
    return jnp.tile(cos, (1, N_HEADS)), jnp.tile(sin, (1, N_HEADS))


def _mix_fwd(ya, ob, yc, od, gain, B, S, name):
    T = B * S
    tm = min(256, S)
    ns = S // tm

    def body(ya_ref, ob_ref, yc_ref, od_ref, g_ref, o_ref):
        ys = [ya_ref[...], _load_heads(ob_ref, N_HEADS), yc_ref[...], _load_heads(od_ref, N_HEADS)]
        for m, y in enumerate(ys):
            r = lax.rsqrt(jnp.mean(y * y, axis=-1, keepdims=True) + RMS_EPS)
            o_ref[:, 512 * m:512 * (m + 1)] = (y * r * g_ref[:, 512 * m:512 * (m + 1)]).astype(BF16)

    row = pl.BlockSpec((tm, 512), lambda b, i: (b * ns + i, 0))
    heads = pl.BlockSpec((None, N_HEADS, tm, HEAD_DIM), lambda b, i: (b, 0, i, 0))
    return pl.pallas_call(
        body, name=name, grid=(B, ns), in_specs=[row, heads, row, heads, pl.BlockSpec((1, 2048), lambda b, i: (0, 0))],
        out_specs=pl.BlockSpec((tm, 2048), lambda b, i: (b * ns + i, 0)), out_shape=jax.ShapeDtypeStruct((T, 2048), BF16),
        compiler_params=_params(("parallel", "parallel")),
    )(ya, ob, yc, od, gain)


def _mix_bwd(ya, ob, yc, od, dycat, gain, B, S, name):
    T = B * S
    tm = min(256, S)
    ns = S // tm

    def body(ya_ref, ob_ref, yc_ref, od_ref, dy_ref, g_ref, dya_ref, dob_ref, dyc_ref, dod_ref, dg_ref):
        @pl.when((pl.program_id(0) == 0) & (pl.program_id(1) == 0))
        def _():
            dg_ref[...] = jnp.zeros_like(dg_ref)

        ys = [ya_ref[...], _load_heads(ob_ref, N_HEADS), yc_ref[...], _load_heads(od_ref, N_HEADS)]
        outs = [dya_ref, dob_ref, dyc_ref, dod_ref]
        for m, y in enumerate(ys):
            cols = slice(512 * m, 512 * (m + 1))
            r = lax.rsqrt(jnp.mean(y * y, axis=-1, keepdims=True) + RMS_EPS)
            yh = y * r
            dh = dy_ref[:, cols]
            dg_ref[:, cols] += jnp.sum(dh * yh, axis=0, keepdims=True)
            dyh = dh * g_ref[:, cols]
            dyv = r * (dyh - yh * jnp.mean(dyh * yh, axis=-1, keepdims=True))
            if m % 2 == 0:
                outs[m][...] = dyv
            else:
                _store_heads(outs[m], dyv, N_HEADS)

    row = pl.BlockSpec((tm, 512), lambda b, i: (b * ns + i, 0))
    heads = pl.BlockSpec((None, N_HEADS, tm, HEAD_DIM), lambda b, i: (b, 0, i, 0))
    vec = pl.BlockSpec((1, 2048), lambda b, i: (0, 0))
    flat = jax.ShapeDtypeStruct((T, 512), F32)
    hm = jax.ShapeDtypeStruct((B, N_HEADS, S, HEAD_DIM), F32)
    return pl.pallas_call(
        body, name=name, grid=(B, ns), in_specs=[row, heads, row, heads, pl.BlockSpec((tm, 2048), lambda b, i: (b * ns + i, 0)), vec],
        out_specs=[row, heads, row, heads, vec], out_shape=[flat, hm, flat, hm, jax.ShapeDtypeStruct((1, 2048), F32)],
        compiler_params=_params(("arbitrary", "arbitrary")),
    )(ya, ob, yc, od, dycat, gain)


def _ffn_down(gate, up, w_down, res, name):
    J, T, n = gate.shape
    N = w_down.shape[2]
    tm = min(256, T)

    def body(g_ref, u_ref, w_ref, r_ref, o_ref, act_ref):
        acc = None
        for j in range(J):
            g = g_ref[j].astype(F32)
            a = (g * _sigmoid(g) * u_ref[j].astype(F32)).astype(BF16)
            act_ref[j] = a
            d = _nn(a, w_ref[j])
            acc = d if acc is None else acc + d
        o_ref[...] = acc + r_ref[...]

    gu = pl.BlockSpec((J, tm, n), lambda i: (0, i, 0))
    row = pl.BlockSpec((tm, N), lambda i: (i, 0))
    return pl.pallas_call(body, name=name, grid=(T // tm,),
                          in_specs=[gu, gu, pl.BlockSpec((J, n, N), lambda i: (0, 0, 0)), row], out_specs=[row, gu],
                          out_shape=[jax.ShapeDtypeStruct((T, N), F32), jax.ShapeDtypeStruct((J, T, n), BF16)],
                          compiler_params=_params(("parallel",)))(gate, up, w_down, res)


def _ffn_down_dx(dx, w_down, gate, up, name):
    J, n, D = w_down.shape
    T = dx.shape[0]
    tm = min(1024, T)

    def body(dx_ref, w_ref, g_ref, u_ref, dg_ref, du_ref):
        d = _nt(dx_ref[...].astype(BF16), w_ref[...])
        g = g_ref[...].astype(F32)
        s = _sigmoid(g)
        dg_ref[...] = (d * u_ref[...].astype(F32) * s * (1.0 + g * (1.0 - s))).astype(BF16)
        du_ref[...] = (d * g * s).astype(BF16)

    blk = pl.BlockSpec((None, tm, n), lambda j, i: (j, i, 0))
    shape = jax.ShapeDtypeStruct((J, T, n), BF16)
    return pl.pallas_call(body, name=name, grid=(J, T // tm),
                          in_specs=[pl.BlockSpec((tm, D), lambda j, i: (i, 0)), pl.BlockSpec((None, n, D), lambda j, i: (j, 0, 0)), blk, blk],
                          out_specs=[blk, blk], out_shape=[shape, shape], compiler_params=_params(("parallel", "parallel")))(dx, w_down, gate, up)


def _loss_grad(y, target, name):
    T, D = y.shape
    tm = min(256, T)
    n = T // tm

    def body(y_ref, t_ref, loss_ref, dy_ref, acc):
        i = pl.program_id(0)

        @pl.when(i == 0)
        def _():
            acc[...] = jnp.zeros_like(acc)

        err = y_ref[...] - t_ref[...]
        dy_ref[...] = err * (1.0 / D)
        acc[...] += jnp.sum((err * err).reshape(tm // 8, 8, D), axis=0)

        @pl.when(i == n - 1)
        def _():
            loss_ref[...] = jnp.full((8, 128), 0.5 / D, F32) * jnp.sum(acc[...])

    row = pl.BlockSpec((tm, D), lambda i: (i, 0))
    return pl.pallas_call(body, name=name, grid=(n,), in_specs=[row, row], out_specs=[pl.BlockSpec((8, 128), lambda i: (0, 0)), row],
                          out_shape=[jax.ShapeDtypeStruct((8, 128), F32), jax.ShapeDtypeStruct((T, D), F32)],
                          scratch_shapes=[pltpu.VMEM((8, D), F32)], compiler_params=_params(("arbitrary",)))(y, target)


def _row_tile(R):
    for cand in (512, 256, 128, 64, 32, 16, 8):
        if R % cand == 0:
            return cand
    return R


def _adamw(w, m, v, stack, name, layer=None, prev=None):
    n, R, C = stack.shape
    tm = _row_tile(R)
    nb = R // tm
    off = 0 if layer is None else layer * nb
    c1 = 1.0 - ADAM_B1 ** ADAM_STEP
    c2 = 1.0 - ADAM_B2 ** ADAM_STEP

    def body(w_ref, m_ref, v_ref, s_ref, *rest):
        g_ref, d_ref, mo_ref, vo_ref = rest[-4:]
        g = s_ref[0].astype(F32)
        for k in range(1, n):
            g = g + s_ref[k].astype(F32)
        mn = ADAM_B1 * m_ref[...] + (1.0 - ADAM_B1) * g
        vn = ADAM_B2 * v_ref[...] + (1.0 - ADAM_B2) * (g * g)
        g_ref[...] = g
        mo_ref[...] = mn
        vo_ref[...] = vn
        d_ref[...] = -ADAM_LR * ((mn / c1) / (jnp.sqrt(vn / c2) + ADAM_EPS) + ADAM_WD * w_ref[...])

    blk = pl.BlockSpec((tm, C), lambda i: (i + off, 0))
    ins = [w, m, v, stack]
    specs = [blk, blk, blk, pl.BlockSpec((n, tm, C), lambda i: (0, i, 0))]
    aliases = {}
    if prev is not None:
        ins += list(prev)
        specs += [pl.BlockSpec(memory_space=pl.ANY)] * 4
        aliases = {4 + t: t for t in range(4)}
    shape = jax.ShapeDtypeStruct(w.shape, F32)
    return pl.pallas_call(body, name=name, grid=(nb,), in_specs=specs, out_specs=[blk] * 4, out_shape=[shape] * 4,
                          input_output_aliases=aliases, compiler_params=_params(("parallel",)))(*ins)


HBM = pl.BlockSpec(memory_space=pltpu.HBM)
SEM = pl.BlockSpec(memory_space=pltpu.SEMAPHORE)
EFFECT = pltpu.SideEffectType.DATAFLOW_SIDE_EFFECTING


PEERS = {"scatter": (1, 2, 3, 4, 5, 6, 7), "gather": (1, 2, 3, 4, 5, 6, 7), "chips": (1, 2, 4, 6), "forward": (2, 4, 6)}


def _spread_copies(srcs, lands, send_sems, recv_sems, local_sems, kind, waiting):
    x, y, c = lax.axis_index("x"), lax.axis_index("y"), lax.axis_index("c")
    me = 4 * x + 2 * y + c

    def peer(bits):
        dev = (1 - x if bits & 4 else x, 1 - y if bits & 2 else y, 1 - c if bits & 1 else c)
        return dev, 4 * dev[0] + 2 * dev[1] + dev[2]

    plan = PEERS[kind]
    remote, local = [], []
    for a, l in enumerate(lands):
        for d, bits in enumerate(plan):
            dev, pid = peer(bits)
            if kind == "forward":
                src, dst, dev = l.at[pid], l.at[peer(bits | 1)[1] if waiting else pid], peer(1)[0]
            else:
                src, dst = (srcs[a].at[pid] if kind == "scatter" else srcs[a]), l.at[pid if waiting else me]
            remote.append(pltpu.make_async_remote_copy(
                src_ref=src, dst_ref=dst, send_sem=send_sems.at[a * len(plan) + d], recv_sem=recv_sems.at[a * len(plan) + d],
                device_id=dev, device_id_type=MESH_ID))
        if kind != "forward":
            local.append(pltpu.make_async_copy(srcs[a].at[me] if kind == "scatter" else srcs[a], l.at[me], local_sems.at[a]))
    return remote, local


def _spread_start(srcs, kind, name, after=None, lands=None):
    if kind == "forward":
        srcs = []
    else:
        shapes = [a.shape if kind == "scatter" else (N_DEV,) + a.shape for a in srcs]
        lands = [lax.empty(shp, a.dtype) for shp, a in zip(shapes, srcs)]
    ns, nl, per = len(srcs), len(lands), len(PEERS[kind])
    extra = [] if after is None else [after]
    sem_shapes = [pltpu.SemaphoreType.DMA((nl * per,))] * 2 + ([pltpu.SemaphoreType.DMA((nl,))] if ns else [])

    def body(*refs):
        src_refs, land_refs = refs[:ns], refs[ns:ns + nl]
        sems = refs[ns + nl + len(extra):ns + nl + len(extra) + len(sem_shapes)]
        remote, local = _spread_copies(src_refs, land_refs, sems[0], sems[1], sems[2] if ns else None, kind, False)
        for cp in remote + local:
            cp.start()
        refs[-1][...] = jnp.zeros((8, 128), F32)

    outs = pl.pallas_call(
        body, name=name,
        out_shape=(*sem_shapes, *[pltpu.HBM(a.shape, a.dtype) for a in srcs + lands], jax.ShapeDtypeStruct((8, 128), F32)),
        in_specs=[HBM] * (ns + nl) + [pl.BlockSpec(memory_space=pl.ANY)] * len(extra),
        out_specs=(*[SEM] * len(sem_shapes), *[HBM] * (ns + nl), pl.BlockSpec(memory_space=pltpu.VMEM)),
        input_output_aliases={i: len(sem_shapes) + i for i in range(ns + nl)},
        compiler_params=pltpu.CompilerParams(has_side_effects=EFFECT),
    )(*[pltpu.with_memory_space_constraint(a, pltpu.HBM) for a in srcs + lands], *extra)
    k = len(sem_shapes)
    return outs[:k], list(outs[k:k + ns]), list(outs[k + ns:k + ns + nl]), outs[-1]


def _spread_wait(sems, srcs, lands, after, kind, name):
    ns, nl = len(srcs), len(lands)
    after = list(after) if isinstance(after, (list, tuple)) else [after]

    def body(*refs):
        src_refs, land_refs = refs[:ns], refs[ns:ns + nl]
        s = refs[ns + nl:ns + nl + len(sems)]
        remote, local = _spread_copies(src_refs, land_refs, s[0], s[1], s[2] if ns else None, kind, True)
        for cp in remote:
            cp.wait_send()
            cp.wait_recv()
        for cp in local:
            cp.wait()

    outs = pl.pallas_call(
        body, name=name, out_shape=tuple(pltpu.HBM(a.shape, a.dtype) for a in srcs + lands),
        in_specs=[HBM] * (ns + nl) + [SEM] * len(sems) + [pl.BlockSpec(memory_space=pl.ANY)] * len(after), out_specs=tuple([HBM] * (ns + nl)),
        input_output_aliases={i: i for i in range(ns + nl)}, compiler_params=pltpu.CompilerParams(has_side_effects=EFFECT),
    )(*srcs, *lands, *sems, *after)
    return list(outs[ns:])


SMALL = ("rel_bias", "norm1_g", "sgu_w", "sgu_b", "dil_qn_g", "dil_kn_g", "conv_w", "conv_b", "conv_ln_g", "conv_ln_b",
         "gqa_qn_g", "gqa_kn_g", "mix_norm_g", "norm2_g")
LARGE = ("w_in", "w_out", "w_gate", "w_up", "w_down")
EARLY = tuple(k for k in SMALL if k != "norm1_g")


def _local_step(x, target, p, B, S, fetch, emit, mid, early):
    T = B * S
    rope = _rope_tables(S)
    win = _bias_windows(p["rel_bias"], S)
    tile8 = lambda g: jnp.tile(g.reshape(1, HEAD_DIM), (1, N_HEADS))
    cols_b = (COL_BQ, COL_BK, COL_BV)
    cols_d = (COL_DQ, COL_DK128, COL_DV128)
    saved = []
    for l in range(DEPTH):
        s = {"x": x}
        s["ws"] = p["sgu_w"][l].astype(BF16)
        s["bias"] = jnp.repeat(p["sgu_b"][l].T, HEAD_DIM, axis=1)
        s["h"] = _rms_fwd(x, p["norm1_g"][l], f"rms1_fwd_{l}")
        s["win"] = fetch(l, "in", s["h"])
        s["cw"] = jnp.pad(s["win"]["conv_w"], ((0, 1), (0, 0))).reshape(32, 1, 512)
        z = s["z"] = _matmul(s["h"], s["win"]["w_in"], "nn", f"in_proj_{l}", tk=D_MODEL)
        s["bias"] = s["bias"] + mid(l, z)
        s["ya"] = _sgu_fwd(z, s["ws"], s["bias"], f"sgu_fwd_{l}")
        s["c"] = _conv_fwd1(z, s["cw"], p["conv_b"][l].reshape(1, 512), B, S, f"conv_fwd_{l}")
        s["yc"] = _conv_fwd2(s["c"], p["conv_ln_g"][l].reshape(1, 512), p["conv_ln_b"][l].reshape(1, 512), f"conv_ln_fwd_{l}")
        s["gb"] = (tile8(p["dil_qn_g"][l]), tile8(p["dil_kn_g"][l]))
        s["gd"] = (tile8(p["gqa_qn_g"][l]), tile8(p["gqa_kn_g"][l])[:, :KV_WIDTH])
        s["qkv_b"] = _prep_fwd(z, *s["gb"], None, B, S, N_HEADS, cols_b, f"prep_b_fwd_{l}")
        s["qkv_d"] = _prep_fwd(z, *s["gd"], rope, B, S, KV_HEADS, cols_d, f"prep_d_fwd_{l}")
        s["ob"] = _attn_fwd(*s["qkv_b"], win, f"attn_b_fwd_{l}")
        s["od"] = _attn_fwd(*s["qkv_d"], None, f"attn_d_fwd_{l}")
        s["gmix"] = p["mix_norm_g"][l].reshape(1, 2048)
        s["ycat"] = _mix_fwd(s["ya"], s["ob"], s["yc"], s["od"], s["gmix"], B, S, f"mix_fwd_{l}")
        s["wout"] = fetch(l, "out", s["ycat"])["w_out"]
        x1 = s["x1"] = _matmul(s["ycat"], s["wout"], "nn", f"out_proj_{l}", res=x, tk=D_MODEL)
        s["h2"] = _rms_fwd(x1, p["norm2_g"][l], f"rms2_fwd_{l}")
        s["ffn"] = fetch(l, "ffn", s["h2"])
        s["gate"] = _mm_shard_out(s["h2"], s["ffn"]["w_gate"], "nn", f"ffn_gate_{l}", out_dtype=BF16, tm=1024)
        s["up"] = _mm_shard_out(s["h2"], s["ffn"]["w_up"], "nn", f"ffn_up_{l}", out_dtype=BF16, tm=1024)
        x, s["act"] = _ffn_down(s["gate"], s["up"], s["ffn"]["w_down"], x1, f"ffn_down_{l}")
        saved.append(s)

    loss_blk, dx = _loss_grad(x, target, "loss")
    g = {k: [None] * DEPTH for k in SMALL if k != "rel_bias"}
    dwin_total = None
    for l in reversed(range(DEPTH)):
        s = saved[l]
        z, ffn = s["z"], s["ffn"]
        dgate, dup = _ffn_down_dx(dx, ffn["w_down"], s["gate"], s["up"], f"ffn_down_dx_{l}")
        tok = emit(l, "w_down", _mm_shard_m(s["act"], dx, f"ffn_down_dw_{l}", out_dtype=BF16, tn=512, tk=T))
        tok += emit(l, "w_gate", _mm_shard_out(s["h2"], dgate, "tn", f"ffn_gate_dw_{l}", out_dtype=BF16, tm=1024, tk=T))
        tok += emit(l, "w_up", _mm_shard_out(s["h2"], dup, "tn", f"ffn_up_dw_{l}", out_dtype=BF16, tm=1024, tk=T))
        dh2 = _mm_shard_k([(dgate, ffn["w_gate"]), (dup, ffn["w_up"])], "nt", f"ffn_up_dx_{l}", tn=512, fold=N_DEV)
        dx1, dg2 = _rms_bwd(dh2, s["x1"], p["norm2_g"][l] + tok, dx, f"rms2_bwd_{l}")
        g["norm2_g"][l] = dg2[0]
        dycat = _matmul(dx1, s["wout"], "nt", f"out_proj_dx_{l}", tk=D_MODEL)
        tok = emit(l, "w_out", _matmul(s["ycat"], dx1, "tn", f"out_proj_dw_{l}", out_dtype=BF16, tn=1024, tk=T))
        dya, dob, dyc, dod, dgm = _mix_bwd(s["ya"], s["ob"], s["yc"], s["od"], dycat, s["gmix"] + tok, B, S, f"mix_bwd_{l}")
        g["mix_norm_g"][l] = dgm[0]
        dz_a, dws, dbias = _sgu_bwd(z, dya, s["ws"], jnp.swapaxes(s["ws"], 1, 2), s["bias"], f"sgu_bwd_{l}")
        g["sgu_w"][l] = dws
        g["sgu_b"][l] = dbias.reshape(128, 8, HEAD_DIM).sum(-1).T
        dc, dlg, dlb, dcb = _conv_bwd1(s["c"], dyc, p["conv_ln_g"][l].reshape(1, 512), p["conv_ln_b"][l].reshape(1, 512), f"conv_ln_bwd_{l}")
        g["conv_ln_g"][l], g["conv_ln_b"][l], g["conv_b"][l] = dlg[0], dlb[0], dcb[0]
        dz_ca, dz_cg, dcw = _conv_bwd2(z, dc, s["cw"], B, S, f"conv_bwd_{l}")
        g["conv_w"][l] = dcw.reshape(32, 512)[:CONV_WIDTH]
        dq, dk, dv, dwin = _attn_bwd(*s["qkv_b"], s["ob"], dob, win, f"attn_b_bwd_{l}")
        dwin_total = dwin if dwin_total is None else dwin_total + dwin
        dz_b, dgq, dgk = _prep_bwd(z, dq, dk, dv, *s["gb"], None, B, S, N_HEADS, cols_b, f"prep_b_bwd_{l}")
        g["dil_qn_g"][l] = dgq.reshape(N_HEADS, HEAD_DIM).sum(0)
        g["dil_kn_g"][l] = dgk.reshape(N_HEADS, HEAD_DIM).sum(0)
        dq, dk, dv = _attn_bwd(*s["qkv_d"], s["od"], dod, None, f"attn_d_bwd_{l}")
        dz_d, dgq, dgk = _prep_bwd(z, dq, dk, dv, *s["gd"], rope, B, S, KV_HEADS, cols_d, f"prep_d_bwd_{l}")
        g["gqa_qn_g"][l] = dgq.reshape(N_HEADS, HEAD_DIM).sum(0)
        g["gqa_kn_g"][l] = dgk.reshape(KV_HEADS, HEAD_DIM).sum(0)
        dz = jnp.concatenate([dz_a, dz_b, dz_ca, dz_cg, dz_d], axis=1)
        tok = jnp.zeros((), F32)
        if l == 0:
            done = {k: jnp.stack(v) for k, v in g.items() if k != "norm1_g"}
            done["rel_bias"] = _bias_fold(dwin_total, S, "bias_fold")
            tok = early(done)
        tok += emit(l, "w_in", _matmul(s["h"], dz, "tn", f"in_proj_dw_{l}", out_dtype=BF16, tk=T))
        dh = _matmul(dz, s["win"]["w_in"], "nt", f"in_proj_dx_{l}", tn=1024, tk=IN_WIDTH)
        dx, dg1 = _rms_bwd(dh, s["x"], p["norm1_g"][l] + tok, dx1, f"rms1_bwd_{l}")
        g["norm1_g"][l] = dg1[0]

    return loss_blk[0, 0], dx, jnp.stack(g["norm1_g"])


GROUPS = {"in": ("w_in",), "out": ("w_out",), "ffn": ("w_gate", "w_up", "w_down")}
COL_SHARDED = ("w_in", "w_gate", "w_up")


def kernel(x, rel_bias, norm1_g, w_in, sgu_w, sgu_b, dil_qn_g, dil_kn_g, conv_w, conv_b, conv_ln_g, conv_ln_b, gqa_qn_g, gqa_kn_g, mix_norm_g, w_out, norm2_g, w_gate, w_up, w_down, loss_target, m_rel_bias, m_norm1_g, m_w_in, m_sgu_w, m_sgu_b, m_dil_qn_g, m_dil_kn_g, m_conv_w, m_conv_b, m_conv_ln_g, m_conv_ln_b, m_gqa_qn_g, m_gqa_kn_g, m_mix_norm_g, m_w_out, m_norm2_g, m_w_gate, m_w_up, m_w_down, v_rel_bias, v_norm1_g, v_w_in, v_sgu_w, v_sgu_b, v_dil_qn_g, v_dil_kn_g, v_conv_w, v_conv_b, v_conv_ln_g, v_conv_ln_b, v_gqa_qn_g, v_gqa_kn_g, v_mix_norm_g, v_w_out, v_norm2_g, v_w_gate, v_w_up, v_w_down):
    w = dict(rel_bias=rel_bias, norm1_g=norm1_g, w_in=w_in, sgu_w=sgu_w, sgu_b=sgu_b, dil_qn_g=dil_qn_g, dil_kn_g=dil_kn_g, conv_w=conv_w,
             conv_b=conv_b, conv_ln_g=conv_ln_g, conv_ln_b=conv_ln_b, gqa_qn_g=gqa_qn_g, gqa_kn_g=gqa_kn_g, mix_norm_g=mix_norm_g,
             w_out=w_out, norm2_g=norm2_g, w_gate=w_gate, w_up=w_up, w_down=w_down)
    m = dict(rel_bias=m_rel_bias, norm1_g=m_norm1_g, w_in=m_w_in, sgu_w=m_sgu_w, sgu_b=m_sgu_b, dil_qn_g=m_dil_qn_g, dil_kn_g=m_dil_kn_g,
             conv_w=m_conv_w, conv_b=m_conv_b, conv_ln_g=m_conv_ln_g, conv_ln_b=m_conv_ln_b, gqa_qn_g=m_gqa_qn_g, gqa_kn_g=m_gqa_kn_g,
             mix_norm_g=m_mix_norm_g, w_out=m_w_out, norm2_g=m_norm2_g, w_gate=m_w_gate, w_up=m_w_up, w_down=m_w_down)
    v = dict(rel_bias=v_rel_bias, norm1_g=v_norm1_g, w_in=v_w_in, sgu_w=v_sgu_w, sgu_b=v_sgu_b, dil_qn_g=v_dil_qn_g, dil_kn_g=v_dil_kn_g,
             conv_w=v_conv_w, conv_b=v_conv_b, conv_ln_g=v_conv_ln_g, conv_ln_b=v_conv_ln_b, gqa_qn_g=v_gqa_qn_g, gqa_kn_g=v_gqa_kn_g,
             mix_norm_g=v_mix_norm_g, w_out=v_w_out, norm2_g=v_norm2_g, w_gate=v_w_gate, w_up=v_w_up, w_down=v_w_down)
    names = list(w)
    B, S, D = x.shape
    T = B * S
    me = 4 * lax.axis_index("x") + 2 * lax.axis_index("y") + lax.axis_index("c")

    bf = {k: w[k].astype(BF16) for k in LARGE}
    spreads, forwards = {}, {}

    def start_gather(l, group, after=None):
        srcs = [bf[k][l] for k in GROUPS[group]] + ([conv_w[l]] if group == "in" else [])
        spreads[l, group] = _spread_start(srcs, "chips", f"gather_{group}_{l}_start", after)
        return spreads[l, group][3][0, 0]

    def forward(l, group, after):
        sems, srcs, lands, _ = spreads[l, group]
        lands = _spread_wait(sems, srcs, lands, after, "chips", f"gather_{group}_{l}_wait")
        forwards[l, group] = _spread_start(None, "forward", f"forward_{group}_{l}_start", lands=lands)
        return forwards[l, group][3]

    def landed(l, group, after):
        sems, _, lands, _ = forwards[l, group]
        return _spread_wait(sems, [], lands, after, "forward", f"forward_{group}_{l}_wait")

    tok0 = start_gather(0, "in") + start_gather(0, "out") + start_gather(0, "ffn")
    small = {k: w[k] for k in SMALL}
    small["norm1_g"] = norm1_g.at[0].add(tok0)

    def mid(l, z):
        if l > 0:
            return jnp.zeros((), F32)
        return start_gather(1, "in", z) + start_gather(1, "out", z) + start_gather(1, "ffn", z)

    def fetch(l, group, after):
        if group == "in":
            tok = forward(0, "in", after) if l == 0 else after
        elif group == "out":
            tok = forward(l, "ffn", [after, forward(l, "out", after)])
        else:
            tok = forward(1, "in", after) if l == 0 else after
        got = dict(zip(GROUPS[group] + ("conv_w",), landed(l, group, [after, tok])))
        if group == "in":
            got["w_in"] = jnp.transpose(got["w_in"], (1, 0, 2)).reshape(D, IN_WIDTH)
            got["conv_w"] = jnp.transpose(got["conv_w"], (1, 0, 2)).reshape(CONV_WIDTH, 512)
        if group == "out":
            got["w_out"] = got["w_out"].reshape(D, D)
        return got

    scatters = {}

    def emit(l, k, dw):
        if k == "w_in":
            dw = jnp.transpose(dw.reshape(D, N_DEV, IN_WIDTH // N_DEV), (1, 0, 2))
        if k == "w_out":
            dw = dw.reshape(N_DEV, D // N_DEV, D)
        scatters[l, k] = _spread_start([dw], "scatter", f"scatter_{k}_{l}_start")
        return scatters[l, k][3][0, 0]

    flat2 = lambda a: a.reshape(-1, a.shape[-1])
    small_spread = []

    def early(done):
        small_spread.append(_spread_start([flat2(done[k]) for k in EARLY], "gather", "gather_small_grads_start"))
        return small_spread[0][3][0, 0]

    loss_part, dx, dnorm1 = _local_step(x.reshape(T, D), loss_target.reshape(T, D), small, B, S, fetch, emit, mid, early)
    loss = lax.psum(loss_part, ("x", "y", "c"))

    out_g, out_d, out_m, out_v = {}, {}, {}, {}

    def update_large(k, after):
        shp = w[k].shape
        two_d = lambda a: a.reshape(-1, shp[-1])
        res = None
        for l in reversed(range(DEPTH)):
            sems, srcs, lands, _ = scatters[l, k]
            stack = _spread_wait(sems, srcs, lands, after, "scatter", f"scatter_{k}_{l}_wait")[0]
            res = _adamw(two_d(w[k]), two_d(m[k]), two_d(v[k]), stack.reshape(N_DEV, -1, shp[-1]), f"adamw_{k}_{l}", layer=l, prev=res)
        out_g[k], out_d[k], out_m[k], out_v[k] = [a.reshape(shp) for a in res]
        return res[0]

    late_sems, late_srcs, late_lands, late_tok = _spread_start([flat2(dnorm1)], "gather", "gather_norm1_grad_start")
    after = [dx, late_tok]
    for k in ("w_down", "w_gate", "w_up", "w_out"):
        after = update_large(k, after)
    sems, srcs, lands, _ = small_spread[0]
    stacks = dict(zip(EARLY, _spread_wait(sems, srcs, lands, after, "gather", "gather_small_grads_wait")))
    stacks["norm1_g"] = _spread_wait(late_sems, late_srcs, late_lands, after, "gather", "gather_norm1_grad_wait")[0]
    for k in SMALL:
        stack = stacks[k]
        if k == "conv_w":
            stack = lax.dynamic_slice_in_dim(stack, me * (512 // N_DEV), 512 // N_DEV, axis=2)
        res = _adamw(flat2(w[k]), flat2(m[k]), flat2(v[k]), stack, f"adamw_{k}")
        out_g[k], out_d[k], out_m[k], out_v[k] = [a.reshape(w[k].shape) for a in res]
        after = res[0]
    update_large("w_in", after)

    return (loss, dx.reshape(B, S, D), *[out_g[k] for k in names], *[out_d[k] for k in names],
            *[out_m[k] for k in names], *[out_v[k] for k in names])
```

```python
import functools
import math

import numpy as np
import jax
import jax.numpy as jnp
from jax import lax
from jax.experimental import pallas as pl
from jax.experimental.pallas import tpu as pltpu

F32 = jnp.float32
BF16 = jnp.bfloat16
HIGHEST = lax.Precision.HIGHEST
MESH_ID = pl.DeviceIdType.MESH

D_MODEL = 2048
DEPTH = 2
HEAD_DIM = 64
GROUP_WIDTH = 512
N_HEADS = 8
KV_HEADS = 2
KV_WIDTH = 128
SGU_CHUNK = 128
CONV_WIDTH = 31
CONV_PAD = 16
GRID_W = 64
ROPE_THETA = 10000.0
REL_BUCKETS = 32
REL_MAX_DIST = 1024
DIL_PATTERNS = ((128, 1), (512, 4), (2048, 16))
FFN_HIDDEN = 5632
IN_WIDTH = 4352
RMS_EPS = 1e-6
LN_EPS = 1e-5
MASKED = -1e30
N_DEV = 8

ADAM_LR = 0.001
ADAM_B1 = 0.9
ADAM_B2 = 0.999
ADAM_EPS = 1e-08
ADAM_WD = 0.01
ADAM_STEP = 10

COL_AU, COL_AV, COL_BQ, COL_BK, COL_BV, COL_CA, COL_CG, COL_DQ = range(8)
COL_DK128, COL_DV128 = 32, 33

VMEM_LIMIT = 56 * 1024 * 1024
ATTN_TQ = 256
GQA_TQ = 512


def _params(sem=None, vmem=VMEM_LIMIT):
    return pltpu.CompilerParams(dimension_semantics=sem, vmem_limit_bytes=vmem)


def _dot(a, b, dims, precision=None):
    return lax.dot_general(a, b, (dims, ((), ())), precision=precision, preferred_element_type=F32)


def _nn(a, b, precision=None):
    return _dot(a, b, ((1,), (0,)), precision)


def _nt(a, b):
    return _dot(a, b, ((1,), (1,)))


def _tn(a, b):
    return _dot(a, b, ((0,), (0,)))


DIMS = {"nn": ((1,), (0,)), "nt": ((1,), (1,)), "tn": ((0,), (0,))}


def _pick(n, cands):
    for c in cands:
        if n % c == 0:
            return c
    return n


def _mm_call(name, mode, pairs, specs, o_spec, out_sds, grid, acc_shape, res=None, fold=None):
    npair, nk, dims = len(pairs), grid[2], DIMS[mode]

    def body(*refs):
        ab = refs[:2 * npair]
        r_ref = refs[2 * npair] if res is not None else None
        o_ref = refs[2 * npair + (res is not None)]
        part = None
        for t in range(npair):
            for s in ([None] if fold is None else range(fold)):
                a_blk = ab[2 * t][...] if s is None else ab[2 * t][s]
                b_blk = ab[2 * t + 1][...] if s is None else ab[2 * t + 1][s]
                d = _dot(a_blk.astype(BF16), b_blk.astype(BF16), dims)
                part = d if part is None else part + d

        def finish(r):
            if r_ref is not None:
                r = r + r_ref[...]
            o_ref[...] = r.astype(o_ref.dtype)

        if nk == 1:
            finish(part)
            return
        acc, k = refs[-1], pl.program_id(2)

        @pl.when(k == 0)
        def _():
            acc[...] = part

        @pl.when(k > 0)
        def _():
            acc[...] += part

        @pl.when(k == nk - 1)
        def _():
            finish(acc[...])

    ins = [t for pair in pairs for t in pair]
    in_specs = [t for pair in specs for t in pair]
    if res is not None:
        ins.append(res)
        in_specs.append(o_spec)
    return pl.pallas_call(
        body, name=name, grid=grid, in_specs=in_specs, out_specs=o_spec, out_shape=out_sds,
        scratch_shapes=[pltpu.VMEM(acc_shape, F32)] if nk > 1 else [],
        compiler_params=_params(("parallel", "parallel", "arbitrary")),
    )(*ins)


def _matmul(a, b, mode, name, res=None, out_dtype=F32, tm=512, tn=None, tk=None):
    if mode == "nn":
        (M, K), N = a.shape, b.shape[1]
    elif mode == "nt":
        (M, K), N = a.shape, b.shape[0]
    else:
        (K, M), N = a.shape, b.shape[1]
    tm = min(tm, M)
    tn = tn or _pick(N, (2176, 2048, 1408, 1024, 512))
    tk = tk or _pick(K, (1024, 2176, 1408, 512))
    assert M % tm == 0 and N % tn == 0 and K % tk == 0, (M, N, K, tm, tn, tk)
    a_spec = pl.BlockSpec((tk, tm), lambda i, j, k: (k, i)) if mode == "tn" else pl.BlockSpec((tm, tk), lambda i, j, k: (i, k))
    b_spec = pl.BlockSpec((tn, tk), lambda i, j, k: (j, k)) if mode == "nt" else pl.BlockSpec((tk, tn), lambda i, j, k: (k, j))
    o_spec = pl.BlockSpec((tm, tn), lambda i, j, k: (i, j))
    return _mm_call(name, mode, [(a, b)], [(a_spec, b_spec)], o_spec, jax.ShapeDtypeStruct((M, N), out_dtype),
                    (M // tm, N // tn, K // tk), (tm, tn), res)


def _mm_shard_out(a, bs, mode, name, out_dtype=F32, tm=512, tk=None):
    J = bs.shape[0]
    n = bs.shape[1] if mode == "nt" else bs.shape[2]
    (K, M) = a.shape if mode == "tn" else a.shape[::-1]
    tm = min(tm, M)
    tk = tk or (K if mode != "tn" else _pick(K, (1024, 512)))
    a_spec = pl.BlockSpec((tk, tm), lambda j, i, k: (k, i)) if mode == "tn" else pl.BlockSpec((tm, tk), lambda j, i, k: (i, k))
    b_spec = pl.BlockSpec((None, n, tk), lambda j, i, k: (j, 0, k)) if mode == "nt" else pl.BlockSpec((None, tk, n), lambda j, i, k: (j, k, 0))
    o_spec = pl.BlockSpec((None, tm, n), lambda j, i, k: (j, i, 0))
    return _mm_call(name, mode, [(a, bs)], [(a_spec, b_spec)], o_spec, jax.ShapeDtypeStruct((J, M, n), out_dtype),
                    (J, M // tm, K // tk), (tm, n))


def _mm_shard_k(pairs, mode, name, res=None, out_dtype=F32, tm=512, tn=None, fold=1):
    J, M, n = pairs[0][0].shape
    N = pairs[0][1].shape[2] if mode == "nn" else pairs[0][1].shape[1]
    tm = min(tm, M)
    tn = tn or _pick(N, (2048, 1024, 512))
    a_spec = pl.BlockSpec((fold, tm, n), lambda i, j, k: (k, i, 0))
    b_spec = pl.BlockSpec((fold, n, tn), lambda i, j, k: (k, 0, j)) if mode == "nn" else pl.BlockSpec((fold, tn, n), lambda i, j, k: (k, j, 0))
    o_spec = pl.BlockSpec((tm, tn), lambda i, j, k: (i, j))
    return _mm_call(name, mode, pairs, [(a_spec, b_spec)] * len(pairs), o_spec, jax.ShapeDtypeStruct((M, N), out_dtype),
                    (M // tm, N // tn, J // fold), (tm, tn), res, fold)


def _mm_shard_m(as_, b, name, out_dtype=F32, tn=None, tk=512):
    J, K, n = as_.shape
    N = b.shape[1]
    tn = tn or _pick(N, (2048, 1024, 512))
    tk = min(tk, K)
    a_spec = pl.BlockSpec((None, tk, n), lambda j, i, k: (j, k, 0))
    b_spec = pl.BlockSpec((tk, tn), lambda j, i, k: (k, i))
    o_spec = pl.BlockSpec((None, n, tn), lambda j, i, k: (j, 0, i))
    return _mm_call(name, "tn", [(as_, b)], [(a_spec, b_spec)], o_spec, jax.ShapeDtypeStruct((J, n, N), out_dtype),
                    (J, N // tn, K // tk), (n, tn))


def _seg_matrix(width):
    return jnp.asarray(np.kron(np.eye(width // HEAD_DIM, dtype=np.float32), np.full((HEAD_DIM, HEAD_DIM), 1.0 / HEAD_DIM, np.float32)), BF16)


def _segmean(v, p):
    hi = v.astype(BF16)
    r = v - hi.astype(F32)
    mid = r.astype(BF16)
    lo = (r - mid.astype(F32)).astype(BF16)
    w = min(256, v.shape[1])
    pw = p[:w, :w]
    halves = []
    for c in range(v.shape[1] // w):
        cols = slice(c * w, (c + 1) * w)
        halves.append(_nn(hi[:, cols], pw) + _nn(mid[:, cols], pw) + _nn(lo[:, cols], pw))
    return halves[0] if len(halves) == 1 else jnp.concatenate(halves, axis=1)


def _gelu(x):
    c0 = math.sqrt(2.0 / math.pi)
    t = jnp.tanh(c0 * (x + 0.044715 * x * x * x))
    return 0.5 * x * (1.0 + t), t


def _gelu_grad(x, t):
    c0 = math.sqrt(2.0 / math.pi)
    return 0.5 * (1.0 + t) + 0.5 * x * (1.0 - t * t) * c0 * (1.0 + 3.0 * 0.044715 * x * x)


def _sigmoid(x):
    return 1.0 / (1.0 + jnp.exp(-x))


def _rms_fwd(x, g, name):
    T, D = x.shape
    tm = min(256, T)

    def body(x_ref, g_ref, o_ref):
        xv = x_ref[...]
        r = lax.rsqrt(jnp.mean(xv * xv, axis=-1, keepdims=True) + RMS_EPS)
        o_ref[...] = (xv * r * g_ref[...]).astype(BF16)

    return pl.pallas_call(
        body, name=name, grid=(T // tm,),
        in_specs=[pl.BlockSpec((tm, D), lambda i: (i, 0)), pl.BlockSpec((1, D), lambda i: (0, 0))],
        out_specs=pl.BlockSpec((tm, D), lambda i: (i, 0)), out_shape=jax.ShapeDtypeStruct((T, D), BF16),
        compiler_params=_params(("parallel",)),
    )(x, g.reshape(1, D))


def _rms_bwd(dh, x, g, dres, name):
    T, D = x.shape
    tm = min(256, T)

    def body(dh_ref, x_ref, g_ref, dres_ref, dx_ref, dg_ref):
        @pl.when(pl.program_id(0) == 0)
        def _():
            dg_ref[...] = jnp.zeros_like(dg_ref)

        xv, dhv = x_ref[...], dh_ref[...]
        r = lax.rsqrt(jnp.mean(xv * xv, axis=-1, keepdims=True) + RMS_EPS)
        y = xv * r
        dy = dhv * g_ref[...]
        dx_ref[...] = dres_ref[...] + r * (dy - y * jnp.mean(dy * y, axis=-1, keepdims=True))
        dg_ref[...] += jnp.sum(dhv * y, axis=0, keepdims=True)

    row = pl.BlockSpec((tm, D), lambda i: (i, 0))
    vec = pl.BlockSpec((1, D), lambda i: (0, 0))
    return pl.pallas_call(
        body, name=name, grid=(T // tm,), in_specs=[row, row, vec, row], out_specs=[row, vec],
        out_shape=[jax.ShapeDtypeStruct((T, D), F32), jax.ShapeDtypeStruct((1, D), F32)],
        compiler_params=_params(("arbitrary",)),
    )(dh, x, g.reshape(1, D), dres)


def _sgu_core(zu, zv, ws_ref, bias, p):
    ug, tu = _gelu(zu)
    vg, tv = _gelu(zv)
    xc = vg - _segmean(vg, p)
    rs = lax.rsqrt(_segmean(xc * xc, p) + LN_EPS)
    vn = xc * rs
    vnb = vn.astype(BF16)
    low = lax.broadcasted_iota(jnp.int32, (SGU_CHUNK, 128), 1) < HEAD_DIM
    parts = []
    for j in range(4):
        vp = vnb[:, 128 * j:128 * (j + 1)]
        parts.append(jnp.where(low, _nn(ws_ref[2 * j], vp), _nn(ws_ref[2 * j + 1], vp)))
    mixed = jnp.concatenate(parts, axis=1) + bias
    return ug, tu, tv, rs, vn, vnb, mixed, low


def _sgu_fwd(z, ws, bias, name):
    T = z.shape[0]

    def body(zu_ref, zv_ref, ws_ref, b_ref, p_ref, y_ref):
        ug, _, _, _, _, _, mixed, _ = _sgu_core(zu_ref[...], zv_ref[...], ws_ref, b_ref[...], p_ref[...])
        y_ref[...] = ug * mixed

    full = lambda shape: pl.BlockSpec(shape, lambda i: (0,) * len(shape))
    return pl.pallas_call(
        body, name=name, grid=(T // SGU_CHUNK,),
        in_specs=[pl.BlockSpec((SGU_CHUNK, 512), lambda i: (i, COL_AU)), pl.BlockSpec((SGU_CHUNK, 512), lambda i: (i, COL_AV)),
                  full((8, 128, 128)), full((128, 512)), full((512, 512))],
        out_specs=pl.BlockSpec((SGU_CHUNK, 512), lambda i: (i, 0)), out_shape=jax.ShapeDtypeStruct((T, 512), F32),
        compiler_params=_params(("parallel",)),
    )(z, z, ws, bias, _seg_matrix(512))


def _sgu_bwd(z, dy, ws, ws_t, bias, name):
    T = z.shape[0]

    def body(zu_ref, zv_ref, dy_ref, ws_ref, wst_ref, b_ref, p_ref, dz_ref, dws_ref, db_ref):
        @pl.when(pl.program_id(0) == 0)
        def _():
            dws_ref[...] = jnp.zeros_like(dws_ref)
            db_ref[...] = jnp.zeros_like(db_ref)

        zu, zv, p = zu_ref[...], zv_ref[...], p_ref[...]
        ug, tu, tv, rs, vn, vnb, mixed, low = _sgu_core(zu, zv, ws_ref, b_ref[...], p)
        dyv = dy_ref[...]
        dmixed = dyv * ug
        db_ref[...] += dmixed
        dmb = dmixed.astype(BF16)
        zero = jnp.zeros((SGU_CHUNK, 128), BF16)
        parts = []
        for j in range(4):
            dmp, vp = dmb[:, 128 * j:128 * (j + 1)], vnb[:, 128 * j:128 * (j + 1)]
            dws_ref[2 * j] += _nt(jnp.where(low, dmp, zero), vp)
            dws_ref[2 * j + 1] += _nt(jnp.where(low, zero, dmp), vp)
            parts.append(jnp.where(low, _nn(wst_ref[2 * j], dmp), _nn(wst_ref[2 * j + 1], dmp)))
        dvn = jnp.concatenate(parts, axis=1)
        dvg = rs * (dvn - _segmean(dvn, p) - vn * _segmean(dvn * vn, p))
        dz_ref[:, 0:512] = (dyv * mixed * _gelu_grad(zu, tu)).astype(BF16)
        dz_ref[:, 512:1024] = (dvg * _gelu_grad(zv, tv)).astype(BF16)

    full = lambda shape: pl.BlockSpec(shape, lambda i: (0,) * len(shape))
    return pl.pallas_call(
        body, name=name, grid=(T // SGU_CHUNK,),
        in_specs=[pl.BlockSpec((SGU_CHUNK, 512), lambda i: (i, COL_AU)), pl.BlockSpec((SGU_CHUNK, 512), lambda i: (i, COL_AV)),
                  pl.BlockSpec((SGU_CHUNK, 512), lambda i: (i, 0)), full((8, 128, 128)), full((8, 128, 128)), full((128, 512)), full((512, 512))],
        out_specs=[pl.BlockSpec((SGU_CHUNK, 1024), lambda i: (i, 0)), full((8, 128, 128)), full((128, 512))],
        out_shape=[jax.ShapeDtypeStruct((T, 1024), BF16), jax.ShapeDtypeStruct((8, 128, 128), F32), jax.ShapeDtypeStruct((128, 512), F32)],
        compiler_params=_params(("arbitrary",)),
    )(z, z, dy, ws, ws_t, bias, _seg_matrix(512))


CONV_ROWS = 256


def _conv_taps(pad_ref, w_ref, base, flip):
    acc = None
    for k in range(CONV_WIDTH):
        wk = w_ref[CONV_WIDTH - 1 - k if flip else k]
        t = wk * pad_ref[base + k + 1:base + k + 1 + CONV_ROWS, :]
        acc = t if acc is None else acc + t
    return acc


def _conv_fwd1(z, w, cb, B, S, name):
    T = B * S
    rows = min(CONV_ROWS, S)
    assert rows == CONV_ROWS

    def body(a_ref, g_ref, w_ref, cb_ref, c_ref, pad):
        pad[0:CONV_PAD, :] = jnp.zeros((CONV_PAD, 128), F32)
        pad[CONV_PAD + S:2 * CONV_PAD + S, :] = jnp.zeros((CONV_PAD, 128), F32)
        pad[CONV_PAD:CONV_PAD + S, :] = a_ref[...] * _sigmoid(g_ref[...])

        for base in range(0, S, CONV_ROWS):
            c_ref[base:base + CONV_ROWS, :] = _conv_taps(pad, w_ref, base, False) + cb_ref[...]

    return pl.pallas_call(
        body, name=name, grid=(4, B),
        in_specs=[pl.BlockSpec((S, 128), lambda j, b: (b, 4 * COL_CA + j)), pl.BlockSpec((S, 128), lambda j, b: (b, 4 * COL_CG + j)),
                  pl.BlockSpec((32, 1, 128), lambda j, b: (0, 0, j)), pl.BlockSpec((1, 128), lambda j, b: (0, j))],
        out_specs=pl.BlockSpec((S, 128), lambda j, b: (b, j)), out_shape=jax.ShapeDtypeStruct((T, 512), F32),
        scratch_shapes=[pltpu.VMEM((S + 2 * CONV_PAD, 128), F32)], compiler_params=_params(("parallel", "parallel")),
    )(z, z, w, cb)


def _ln_rows(c):
    mu = jnp.mean(c, axis=-1, keepdims=True)
    xc = c - mu
    rs = lax.rsqrt(jnp.mean(xc * xc, axis=-1, keepdims=True) + LN_EPS)
    return xc * rs, rs


def _conv_fwd2(c, lng, lnb, name):
    T = c.shape[0]
    tm = min(512, T)

    def body(c_ref, g_ref, b_ref, y_ref):
        n, _ = _ln_rows(c_ref[...])
        t = n * g_ref[...] + b_ref[...]
        y_ref[...] = t * _sigmoid(t)

    row = pl.BlockSpec((tm, 512), lambda i: (i, 0))
    vec = pl.BlockSpec((1, 512), lambda i: (0, 0))
    return pl.pallas_call(body, name=name, grid=(T // tm,), in_specs=[row, vec, vec], out_specs=row,
                          out_shape=jax.ShapeDtypeStruct((T, 512), F32), compiler_params=_params(("parallel",)))(c, lng, lnb)


def _conv_bwd1(c, dy, lng, lnb, name):
    T = c.shape[0]
    tm = min(512, T)

    def body(c_ref, dy_ref, g_ref, b_ref, dc_ref, dg_ref, db_ref, dcb_ref):
        @pl.when(pl.program_id(0) == 0)
        def _():
            dg_ref[...] = jnp.zeros_like(dg_ref)
            db_ref[...] = jnp.zeros_like(db_ref)
            dcb_ref[...] = jnp.zeros_like(dcb_ref)

        n, rs = _ln_rows(c_ref[...])
        t = n * g_ref[...] + b_ref[...]
        s = _sigmoid(t)
        dt = dy_ref[...] * s * (1.0 + t * (1.0 - s))
        dg_ref[...] += jnp.sum(dt * n, axis=0, keepdims=True)
        db_ref[...] += jnp.sum(dt, axis=0, keepdims=True)
        dn = dt * g_ref[...]
        dc = rs * (dn - jnp.mean(dn, axis=-1, keepdims=True) - n * jnp.mean(dn * n, axis=-1, keepdims=True))
        dc_ref[...] = dc
        dcb_ref[...] += jnp.sum(dc, axis=0, keepdims=True)

    row = pl.BlockSpec((tm, 512), lambda i: (i, 0))
    vec = pl.BlockSpec((1, 512), lambda i: (0, 0))
    vshape = jax.ShapeDtypeStruct((1, 512), F32)
    return pl.pallas_call(body, name=name, grid=(T // tm,), in_specs=[row, row, vec, vec], out_specs=[row, vec, vec, vec],
                          out_shape=[jax.ShapeDtypeStruct((T, 512), F32), vshape, vshape, vshape],
                          compiler_params=_params(("arbitrary",)))(c, dy, lng, lnb)


def _conv_bwd2(z, dc, w, B, S, name):
    T = B * S

    def body(a_ref, g_ref, dc_ref, w_ref, da_ref, dg_ref, dw_ref, hpad, dpad):
        @pl.when(pl.program_id(1) == 0)
        def _():
            dw_ref[...] = jnp.zeros_like(dw_ref)

        zeros = jnp.zeros((CONV_PAD, 128), F32)
        for ref in (hpad, dpad):
            ref[0:CONV_PAD, :] = zeros
            ref[CONV_PAD + S:2 * CONV_PAD + S, :] = zeros
        hpad[CONV_PAD:CONV_PAD + S, :] = a_ref[...] * _sigmoid(g_ref[...])
        dpad[CONV_PAD:CONV_PAD + S, :] = dc_ref[...]
        dws = [None] * CONV_WIDTH
        for base in range(0, S, CONV_ROWS):
            rows = slice(base, base + CONV_ROWS)
            dh = _conv_taps(dpad, w_ref, base, True)
            sg = _sigmoid(g_ref[rows, :])
            da_ref[rows, :] = (dh * sg).astype(BF16)
            dg_ref[rows, :] = (dh * a_ref[rows, :] * sg * (1.0 - sg)).astype(BF16)
            dcv = dc_ref[rows, :]
            for k in range(CONV_WIDTH):
                prod = dcv * hpad[base + k + 1:base + k + 1 + CONV_ROWS, :]
                part = jnp.sum(prod.reshape(CONV_ROWS // 8, 8, 128), axis=0)
                dws[k] = part if dws[k] is None else dws[k] + part
        for k in range(CONV_WIDTH):
            dw_ref[k] += jnp.sum(dws[k], axis=0, keepdims=True)

    return pl.pallas_call(
        body, name=name, grid=(4, B),
        in_specs=[pl.BlockSpec((S, 128), lambda j, b: (b, 4 * COL_CA + j)), pl.BlockSpec((S, 128), lambda j, b: (b, 4 * COL_CG + j)),
                  pl.BlockSpec((S, 128), lambda j, b: (b, j)), pl.BlockSpec((32, 1, 128), lambda j, b: (0, 0, j))],
        out_specs=[pl.BlockSpec((S, 128), lambda j, b: (b, j)), pl.BlockSpec((S, 128), lambda j, b: (b, j)),
                   pl.BlockSpec((32, 1, 128), lambda j, b: (0, 0, j))],
        out_shape=[jax.ShapeDtypeStruct((T, 512), BF16), jax.ShapeDtypeStruct((T, 512), BF16), jax.ShapeDtypeStruct((32, 1, 512), F32)],
        scratch_shapes=[pltpu.VMEM((S + 2 * CONV_PAD, 128), F32), pltpu.VMEM((S + 2 * CONV_PAD, 128), F32)],
        compiler_params=_params(("parallel", "arbitrary")),
    )(z, z, dc, w)


def _swap16(x):
    n = x.shape[1]
    first = (lax.broadcasted_iota(jnp.int32, x.shape, 1) % 32) < 16
    return jnp.where(first, pltpu.roll(x, n - 16, 1), pltpu.roll(x, 16, 1))


def _rope(x, cos, sin):
    return x * cos + _swap16(x) * sin


def _rope_t(dy, cos, sin):
    return dy * cos + _swap16(dy * sin)


def _qk_norm(x, p):
    r = lax.rsqrt(_segmean(x * x, p) + RMS_EPS)
    return x * r, r


def _store_heads(ref, val, n):
    for h in range(n):
        ref[h] = val[:, HEAD_DIM * h:HEAD_DIM * (h + 1)].astype(ref.dtype)


def _load_heads(ref, n):
    return jnp.concatenate([ref[h] for h in range(n)], axis=1)


def _prep_fwd(z, gq, gk, rope, B, S, kv_heads, cols, name):
    tm = min(256, S)
    ns = S // tm
    kw = kv_heads * HEAD_DIM
    scale = HEAD_DIM ** -0.5
    qc, kc, vc = cols

    def body(*refs):
        if rope is None:
            q_ref, k_ref, v_ref, gq_ref, gk_ref, p_ref, qo, ko, vo = refs
        else:
            q_ref, k_ref, v_ref, gq_ref, gk_ref, p_ref, cos_ref, sin_ref, qo, ko, vo = refs
        p = p_ref[...]
        qn, _ = _qk_norm(q_ref[...], p)
        kn, _ = _qk_norm(k_ref[...], p[:kw, :kw])
        qn, kn = qn * gq_ref[...], kn * gk_ref[...]
        if rope is not None:
            cos, sin = cos_ref[...], sin_ref[...]
            qn, kn = _rope(qn, cos, sin), _rope(kn, cos[:, :kw], sin[:, :kw])
        _store_heads(qo, qn * scale, N_HEADS)
        _store_heads(ko, kn, kv_heads)
        _store_heads(vo, v_ref[...], kv_heads)

    row = lambda w, c: pl.BlockSpec((tm, w), lambda b, i: (b * ns + i, c))
    const = lambda shape: pl.BlockSpec(shape, lambda b, i: (0,) * len(shape))
    heads = lambda n: pl.BlockSpec((None, n, tm, HEAD_DIM), lambda b, i: (b, 0, i, 0))
    ins = [z, z, z, gq, gk, _seg_matrix(512)]
    specs = [row(512, qc), row(kw, kc), row(kw, vc), const((1, 512)), const((1, kw)), const((512, 512))]
    if rope is not None:
        ins += list(rope)
        specs += [pl.BlockSpec((tm, 512), lambda b, i: (i, 0))] * 2
    return pl.pallas_call(
        body, name=name, grid=(B, ns), in_specs=specs, out_specs=[heads(N_HEADS), heads(kv_heads), heads(kv_heads)],
        out_shape=[jax.ShapeDtypeStruct((B, N_HEADS, S, HEAD_DIM), BF16), jax.ShapeDtypeStruct((B, kv_heads, S, HEAD_DIM), BF16),
                   jax.ShapeDtypeStruct((B, kv_heads, S, HEAD_DIM), BF16)],
        compiler_params=_params(("parallel", "parallel")),
    )(*ins)


def _prep_bwd(z, dq, dk, dv, gq, gk, rope, B, S, kv_heads, cols, name):
    T = B * S
    tm = min(256, S)
    ns = S // tm
    kw = kv_heads * HEAD_DIM
    scale = HEAD_DIM ** -0.5
    qc, kc, _ = cols

    def body(*refs):
        if rope is None:
            q_ref, k_ref, dq_ref, dk_ref, dv_ref, gq_ref, gk_ref, p_ref, dz_ref, dgq_ref, dgk_ref = refs
        else:
            q_ref, k_ref, dq_ref, dk_ref, dv_ref, gq_ref, gk_ref, p_ref, cos_ref, sin_ref, dz_ref, dgq_ref, dgk_ref = refs

        @pl.when((pl.program_id(0) == 0) & (pl.program_id(1) == 0))
        def _():
            dgq_ref[...] = jnp.zeros_like(dgq_ref)
            dgk_ref[...] = jnp.zeros_like(dgk_ref)

        p = p_ref[...]
        dqv = _load_heads(dq_ref, N_HEADS) * scale
        dkv = _load_heads(dk_ref, kv_heads)
        if rope is not None:
            cos, sin = cos_ref[...], sin_ref[...]
            dqv, dkv = _rope_t(dqv, cos, sin), _rope_t(dkv, cos[:, :kw], sin[:, :kw])

        def through_norm(xv, dy, g, pm, dg_ref):
            xh, r = _qk_norm(xv, pm)
            dg_ref[...] += jnp.sum(dy * xh, axis=0, keepdims=True)
            dxh = dy * g
            return r * (dxh - xh * _segmean(dxh * xh, pm))

        dz_ref[:, 0:512] = through_norm(q_ref[...], dqv, gq_ref[...], p, dgq_ref).astype(BF16)
        dz_ref[:, 512:512 + kw] = through_norm(k_ref[...], dkv, gk_ref[...], p[:kw, :kw], dgk_ref).astype(BF16)
        dz_ref[:, 512 + kw:512 + 2 * kw] = _load_heads(dv_ref, kv_heads).astype(BF16)

    row = lambda w, c: pl.BlockSpec((tm, w), lambda b, i: (b * ns + i, c))
    const = lambda shape: pl.BlockSpec(shape, lambda b, i: (0,) * len(shape))
    heads = lambda n: pl.BlockSpec((None, n, tm, HEAD_DIM), lambda b, i: (b, 0, i, 0))
    ins = [z, z, dq, dk, dv, gq, gk, _seg_matrix(512)]
    specs = [row(512, qc), row(kw, kc), heads(N_HEADS), heads(kv_heads), heads(kv_heads), const((1, 512)), const((1, kw)), const((512, 512))]
    if rope is not None:
        ins += list(rope)
        specs += [pl.BlockSpec((tm, 512), lambda b, i: (i, 0))] * 2
    return pl.pallas_call(
        body, name=name, grid=(B, ns), in_specs=specs, out_specs=[row(512 + 2 * kw, 0), const((1, 512)), const((1, kw))],
        out_shape=[jax.ShapeDtypeStruct((T, 512 + 2 * kw), BF16), jax.ShapeDtypeStruct((1, 512), F32), jax.ShapeDtypeStruct((1, kw), F32)],
        compiler_params=_params(("arbitrary", "arbitrary")),
    )(*ins)


def _toeplitz(win, tq, S):
    r = pltpu.roll(jnp.broadcast_to(win, (tq, S + tq)), 0, 1, stride=1, stride_axis=0)
    return r[:, tq:tq + S]


ATTN_HEADS = 4


def _attn_fwd(q, k, v, win, name, nh=ATTN_HEADS, tq=ATTN_TQ):
    B, H, S, _ = q.shape
    shared = k.shape[1] != H
    assert not shared or H // k.shape[1] == nh
    tq = min(tq, S)

    def body(*refs):
        if win is None:
            q_ref, k_ref, v_ref, o_ref = refs
        else:
            q_ref, k_ref, v_ref, w_ref, o_ref = refs
        kvs = [(k_ref[...], v_ref[...])] * nh if shared else [(k_ref[h], v_ref[h]) for h in range(nh)]
        scores = []
        for h in range(nh):
            s = _nt(q_ref[h], kvs[h][0])
            if win is not None:
                s = s + _toeplitz(w_ref[h], tq, S)
            scores.append(s)
        probs = []
        for s in scores:
            p = jnp.exp(s - jnp.max(s, axis=-1, keepdims=True))
            probs.append((p.astype(BF16), jnp.sum(p, axis=-1, keepdims=True)))
        for h, (p, l) in enumerate(probs):
            o_ref[h] = _nn(p, kvs[h][1]) / l

    qs = pl.BlockSpec((None, nh, tq, HEAD_DIM), lambda b, h, i: (b, h, i, 0))
    ks = (pl.BlockSpec((None, None, S, HEAD_DIM), lambda b, h, i: (b, h, 0, 0)) if shared
          else pl.BlockSpec((None, nh, S, HEAD_DIM), lambda b, h, i: (b, h, 0, 0)))
    ins, specs = [q, k, v], [qs, ks, ks]
    if win is not None:
        ins.append(win)
        specs.append(pl.BlockSpec((nh, None, 1, S + tq), lambda b, h, i: (h, i, 0, 0)))
    return pl.pallas_call(body, name=name, grid=(B, H // nh, S // tq), in_specs=specs, out_specs=qs,
                          out_shape=jax.ShapeDtypeStruct((B, H, S, HEAD_DIM), F32),
                          compiler_params=_params(("parallel", "parallel", "parallel")))(*ins)


def _attn_bwd(q, k, v, o, do, win, name, nh=ATTN_HEADS, tq=ATTN_TQ):
    B, H, S, _ = q.shape
    hkv = k.shape[1]
    shared = hkv != H
    assert not shared or H // hkv == nh
    tq = min(tq, S)
    nq = S // tq

    def body(*refs):
        if win is None:
            q_ref, k_ref, v_ref, o_ref, do_ref, dq_ref, dk_ref, dv_ref = refs
        else:
            q_ref, k_ref, v_ref, o_ref, do_ref, w_ref, rev_ref, dq_ref, dk_ref, dv_ref, dw_ref = refs

        @pl.when(pl.program_id(2) == 0)
        def _():
            dk_ref[...] = jnp.zeros_like(dk_ref)
            dv_ref[...] = jnp.zeros_like(dv_ref)

        kvs = [(k_ref[...], v_ref[...])] * nh if shared else [(k_ref[h], v_ref[h]) for h in range(nh)]
        qvs, dobs, scores, dps = [], [], [], []
        for h in range(nh):
            qv, dov = q_ref[h], do_ref[h]
            dob = dov.astype(BF16)
            s = _nt(qv, kvs[h][0])
            if win is not None:
                s = s + _toeplitz(w_ref[h], tq, S)
            dp = _nt(dob, kvs[h][1]) - jnp.sum(dov * o_ref[h], axis=-1, keepdims=True)
            qvs.append(qv)
            dobs.append(dob)
            scores.append(s)
            dps.append(dp)
        pbs, dsbs = [], []
        for s, dp in zip(scores, dps):
            p = jnp.exp(s - jnp.max(s, axis=-1, keepdims=True))
            p = p * (1.0 / jnp.sum(p, axis=-1, keepdims=True))
            pbs.append(p.astype(BF16))
            dsbs.append((p * dp).astype(BF16))
        dk_acc = dv_acc = None
        for h in range(nh):
            dvh, dkh = _tn(pbs[h], dobs[h]), _tn(dsbs[h], qvs[h])
            dq_ref[h] = _nn(dsbs[h], kvs[h][0])
            if shared:
                dv_acc = dvh if dv_acc is None else dv_acc + dvh
                dk_acc = dkh if dk_acc is None else dk_acc + dkh
            else:
                dv_ref[h] += dvh
                dk_ref[h] += dkh
            if win is not None:
                rev = _nn(rev_ref[...], dsbs[h])
                wide = jnp.concatenate([rev, jnp.zeros((tq, tq), F32)], axis=1)
                dw_ref[h] = jnp.sum(pltpu.roll(wide, 0, 1, stride=1, stride_axis=0), axis=0, keepdims=True)
        if shared:
            dv_ref[...] += dv_acc
            dk_ref[...] += dk_acc

    qs = pl.BlockSpec((None, nh, tq, HEAD_DIM), lambda b, h, i: (b, h, i, 0))
    ks = (pl.BlockSpec((None, None, S, HEAD_DIM), lambda b, h, i: (b, h, 0, 0)) if shared
          else pl.BlockSpec((None, nh, S, HEAD_DIM), lambda b, h, i: (b, h, 0, 0)))
    ins, specs = [q, k, v, o, do], [qs, ks, ks, qs, qs]
    outs = [jax.ShapeDtypeStruct((B, H, S, HEAD_DIM), F32), jax.ShapeDtypeStruct((B, hkv, S, HEAD_DIM), F32), jax.ShapeDtypeStruct((B, hkv, S, HEAD_DIM), F32)]
    ospecs = [qs, ks, ks]
    if win is not None:
        ins += [win, jnp.asarray(np.eye(tq, dtype=np.float32)[::-1].copy(), BF16)]
        specs += [pl.BlockSpec((nh, None, 1, S + tq), lambda b, h, i: (h, i, 0, 0)), pl.BlockSpec((tq, tq), lambda b, h, i: (0, 0))]
        outs.append(jax.ShapeDtypeStruct((B, H, nq, 1, S + tq), F32))
        ospecs.append(pl.BlockSpec((None, nh, None, 1, S + tq), lambda b, h, i: (b, h, i, 0, 0)))
    return pl.pallas_call(body, name=name, grid=(B, H // nh, nq), in_specs=specs, out_specs=ospecs, out_shape=outs,
                          compiler_params=_params(("parallel", "parallel", "arbitrary")))(*ins)


def _pattern_count(delta):
    n = jnp.zeros(delta.shape, jnp.int32)
    for window, dil in DIL_PATTERNS:
        n = n + ((delta % dil == 0) & (jnp.abs(delta) <= window // 2)).astype(jnp.int32)
    return n


def _t5_bucket(rel):
    nb = REL_BUCKETS // 2
    max_exact = nb // 2
    ret = jnp.where(rel > 0, nb, 0)
    n = jnp.abs(rel)
    nf = jnp.maximum(n, 1).astype(F32)
    large = max_exact + (jnp.log(nf / max_exact) / math.log(REL_MAX_DIST / max_exact) * (nb - max_exact)).astype(jnp.int32)
    large = jnp.minimum(large, nb - 1)
    return ret + jnp.where(n < max_exact, n, large)


def _bias_windows(rel_bias, S):
    tq = min(ATTN_TQ, S)
    nq = S // tq
    n = nq * (S + tq)
    delta = (jnp.arange(S + tq)[None, :] - (jnp.arange(nq)[:, None] + 1) * tq).reshape(n)
    count = _pattern_count(delta)
    onehot = (_t5_bucket(delta)[None, :] == jnp.arange(REL_BUCKETS)[:, None]).astype(F32)
    extra = jnp.where(count > 0, jnp.log(jnp.maximum(count, 1).astype(F32)), MASKED).reshape(1, n)
    live = (count > 0).astype(F32).reshape(1, n)

    def body(t_ref, oh_ref, live_ref, extra_ref, o_ref):
        o_ref[...] = _nn(t_ref[...], oh_ref[...], HIGHEST) * live_ref[...] + extra_ref[...]

    val = pl.pallas_call(body, name="bias_windows", out_shape=jax.ShapeDtypeStruct((N_HEADS, n), F32),
                         compiler_params=_params())(rel_bias.T, onehot, live, extra)
    return val.reshape(N_HEADS, nq, 1, S + tq)


def _bias_fold(dwin, S, name):
    B, H, nq = dwin.shape[:3]
    tq = min(ATTN_TQ, S)
    n = nq * (S + tq)
    delta = (jnp.arange(S + tq)[None, :] - (tq - 1) - jnp.arange(nq)[:, None] * tq).reshape(n)
    onehot = (_t5_bucket(delta)[:, None] == jnp.arange(128)[None, :]).astype(F32)

    def body(d_ref, oh_ref, o_ref):
        tot = d_ref[0]
        for b in range(1, B):
            tot = tot + d_ref[b]
        o_ref[...] = _nn(tot, oh_ref[...], HIGHEST)

    out = pl.pallas_call(body, name=name, out_shape=jax.ShapeDtypeStruct((H, 128), F32), compiler_params=_params())(dwin.reshape(B, H, n), onehot)
    return out[:, :REL_BUCKETS].T


def _rope_tables(S):
    half = 16
    freqs = ROPE_THETA ** (-jnp.arange(half, dtype=F32) / half)
    t = jnp.arange(S)
    ang_r = (t // GRID_W).astype(F32)[:, None] * freqs[None, :]
    ang_c = (t % GRID_W).astype(F32)[:, None] * freqs[None, :]
    cos = jnp.concatenate([jnp.cos(ang_r)] * 2 + [jnp.cos(ang_c)] * 2, axis=1)
    sin = jnp.concatenate([-jnp.sin(ang_r), jnp.sin(ang_r), -jnp.sin(ang_c), jnp.sin(ang_c)], axis=1)
    return jnp.tile(cos, (1, N_HEADS)), jnp.tile(sin, (1, N_HEADS))


def _mix_fwd(ya, ob, yc, od, gain, B, S, name):
    T = B * S
    tm = min(256, S)
    ns = S // tm

    def body(ya_ref, ob_ref, yc_ref, od_ref, g_ref, o_ref):
        ys = [ya_ref[...], _load_heads(ob_ref, N_HEADS), yc_ref[...], _load_heads(od_ref, N_HEADS)]
        for m, y in enumerate(ys):
            r = lax.rsqrt(jnp.mean(y * y, axis=-1, keepdims=True) + RMS_EPS)
            o_ref[:, 512 * m:512 * (m + 1)] = (y * r * g_ref[:, 512 * m:512 * (m + 1)]).astype(BF16)

    row = pl.BlockSpec((tm, 512), lambda b, i: (b * ns + i, 0))
    heads = pl.BlockSpec((None, N_HEADS, tm, HEAD_DIM), lambda b, i: (b, 0, i, 0))
    return pl.pallas_call(
        body, name=name, grid=(B, ns), in_specs=[row, heads, row, heads, pl.BlockSpec((1, 2048), lambda b, i: (0, 0))],
        out_specs=pl.BlockSpec((tm, 2048), lambda b, i: (b * ns + i, 0)), out_shape=jax.ShapeDtypeStruct((T, 2048), BF16),
        compiler_params=_params(("parallel", "parallel")),
    )(ya, ob, yc, od, gain)


def _mix_bwd(ya, ob, yc, od, dycat, gain, B, S, name):
    T = B * S
    tm = min(256, S)
    ns = S // tm

    def body(ya_ref, ob_ref, yc_ref, od_ref, dy_ref, g_ref, dya_ref, dob_ref, dyc_ref, dod_ref, dg_ref):
        @pl.when((pl.program_id(0) == 0) & (pl.program_id(1) == 0))
        def _():
            dg_ref[...] = jnp.zeros_like(dg_ref)

        ys = [ya_ref[...], _load_heads(ob_ref, N_HEADS), yc_ref[...], _load_heads(od_ref, N_HEADS)]
        outs = [dya_ref, dob_ref, dyc_ref, dod_ref]
        for m, y in enumerate(ys):
            cols = slice(512 * m, 512 * (m + 1))
            r = lax.rsqrt(jnp.mean(y * y, axis=-1, keepdims=True) + RMS_EPS)
            yh = y * r
            dh = dy_ref[:, cols]
            dg_ref[:, cols] += jnp.sum(dh * yh, axis=0, keepdims=True)
            dyh = dh * g_ref[:, cols]
            dyv = r * (dyh - yh * jnp.mean(dyh * yh, axis=-1, keepdims=True))
            if m % 2 == 0:
                outs[m][...] = dyv
            else:
                _store_heads(outs[m], dyv, N_HEADS)

    row = pl.BlockSpec((tm, 512), lambda b, i: (b * ns + i, 0))
    heads = pl.BlockSpec((None, N_HEADS, tm, HEAD_DIM), lambda b, i: (b, 0, i, 0))
    vec = pl.BlockSpec((1, 2048), lambda b, i: (0, 0))
    flat = jax.ShapeDtypeStruct((T, 512), F32)
    hm = jax.ShapeDtypeStruct((B, N_HEADS, S, HEAD_DIM), F32)
    return pl.pallas_call(
        body, name=name, grid=(B, ns), in_specs=[row, heads, row, heads, pl.BlockSpec((tm, 2048), lambda b, i: (b * ns + i, 0)), vec],
        out_specs=[row, heads, row, heads, vec], out_shape=[flat, hm, flat, hm, jax.ShapeDtypeStruct((1, 2048), F32)],
        compiler_params=_params(("arbitrary", "arbitrary")),
    )(ya, ob, yc, od, dycat, gain)


def _ffn_down(gate, up, w_down, res, name):
    J, T, n = gate.shape
    N = w_down.shape[2]
    tm = min(256, T)

    def body(g_ref, u_ref, w_ref, r_ref, o_ref, act_ref):
        acc = None
        for j in range(J):
            g = g_ref[j].astype(F32)
            a = (g * _sigmoid(g) * u_ref[j].astype(F32)).astype(BF16)
            act_ref[j] = a
            d = _nn(a, w_ref[j])
            acc = d if acc is None else acc + d
        o_ref[...] = acc + r_ref[...]

    gu = pl.BlockSpec((J, tm, n), lambda i: (0, i, 0))
    row = pl.BlockSpec((tm, N), lambda i: (i, 0))
    return pl.pallas_call(body, name=name, grid=(T // tm,),
                          in_specs=[gu, gu, pl.BlockSpec((J, n, N), lambda i: (0, 0, 0)), row], out_specs=[row, gu],
                          out_shape=[jax.ShapeDtypeStruct((T, N), F32), jax.ShapeDtypeStruct((J, T, n), BF16)],
                          compiler_params=_params(("parallel",)))(gate, up, w_down, res)


def _ffn_down_dx(dx, w_down, gate, up, name):
    J, n, D = w_down.shape
    T = dx.shape[0]
    tm = min(1024, T)

    def body(dx_ref, w_ref, g_ref, u_ref, dg_ref, du_ref):
        d = _nt(dx_ref[...].astype(BF16), w_ref[...])
        g = g_ref[...].astype(F32)
        s = _sigmoid(g)
        dg_ref[...] = (d * u_ref[...].astype(F32) * s * (1.0 + g * (1.0 - s))).astype(BF16)
        du_ref[...] = (d * g * s).astype(BF16)

    blk = pl.BlockSpec((None, tm, n), lambda j, i: (j, i, 0))
    shape = jax.ShapeDtypeStruct((J, T, n), BF16)
    return pl.pallas_call(body, name=name, grid=(J, T // tm),
                          in_specs=[pl.BlockSpec((tm, D), lambda j, i: (i, 0)), pl.BlockSpec((None, n, D), lambda j, i: (j, 0, 0)), blk, blk],
                          out_specs=[blk, blk], out_shape=[shape, shape], compiler_params=_params(("parallel", "parallel")))(dx, w_down, gate, up)


def _loss_grad(y, target, name):
    T, D = y.shape
    tm = min(256, T)
    n = T // tm

    def body(y_ref, t_ref, loss_ref, dy_ref, acc):
        i = pl.program_id(0)

        @pl.when(i == 0)
        def _():
            acc[...] = jnp.zeros_like(acc)

        err = y_ref[...] - t_ref[...]
        dy_ref[...] = err * (1.0 / D)
        acc[...] += jnp.sum((err * err).reshape(tm // 8, 8, D), axis=0)

        @pl.when(i == n - 1)
        def _():
            loss_ref[...] = jnp.full((8, 128), 0.5 / D, F32) * jnp.sum(acc[...])

    row = pl.BlockSpec((tm, D), lambda i: (i, 0))
    return pl.pallas_call(body, name=name, grid=(n,), in_specs=[row, row], out_specs=[pl.BlockSpec((8, 128), lambda i: (0, 0)), row],
                          out_shape=[jax.ShapeDtypeStruct((8, 128), F32), jax.ShapeDtypeStruct((T, D), F32)],
                          scratch_shapes=[pltpu.VMEM((8, D), F32)], compiler_params=_params(("arbitrary",)))(y, target)


def _row_tile(R):
    for cand in (512, 256, 128, 64, 32, 16, 8):
        if R % cand == 0:
            return cand
    return R


def _adamw(w, m, v, stack, name, layer=None, prev=None):
    n, R, C = stack.shape
    tm = _row_tile(R)
    nb = R // tm
    off = 0 if layer is None else layer * nb
    c1 = 1.0 - ADAM_B1 ** ADAM_STEP
    c2 = 1.0 - ADAM_B2 ** ADAM_STEP

    def body(w_ref, m_ref, v_ref, s_ref, *rest):
        g_ref, d_ref, mo_ref, vo_ref = rest[-4:]
        g = s_ref[0].astype(F32)
        for k in range(1, n):
            g = g + s_ref[k].astype(F32)
        mn = ADAM_B1 * m_ref[...] + (1.0 - ADAM_B1) * g
        vn = ADAM_B2 * v_ref[...] + (1.0 - ADAM_B2) * (g * g)
        g_ref[...] = g
        mo_ref[...] = mn
        vo_ref[...] = vn
        d_ref[...] = -ADAM_LR * ((mn / c1) / (jnp.sqrt(vn / c2) + ADAM_EPS) + ADAM_WD * w_ref[...])

    blk = pl.BlockSpec((tm, C), lambda i: (i + off, 0))
    ins = [w, m, v, stack]
    specs = [blk, blk, blk, pl.BlockSpec((n, tm, C), lambda i: (0, i, 0))]
    aliases = {}
    if prev is not None:
        ins += list(prev)
        specs += [pl.BlockSpec(memory_space=pl.ANY)] * 4
        aliases = {4 + t: t for t in range(4)}
    shape = jax.ShapeDtypeStruct(w.shape, F32)
    return pl.pallas_call(body, name=name, grid=(nb,), in_specs=specs, out_specs=[blk] * 4, out_shape=[shape] * 4,
                          input_output_aliases=aliases, compiler_params=_params(("parallel",)))(*ins)


HBM = pl.BlockSpec(memory_space=pltpu.HBM)
SEM = pl.BlockSpec(memory_space=pltpu.SEMAPHORE)
EFFECT = pltpu.SideEffectType.DATAFLOW_SIDE_EFFECTING


PEERS = {"scatter": (1, 2, 3, 4, 5, 6, 7), "gather": (1, 2, 3, 4, 5, 6, 7), "chips": (1, 2, 4, 6), "forward": (2, 4, 6)}


def _spread_copies(srcs, lands, send_sems, recv_sems, local_sems, kind, waiting):
    x, y, c = lax.axis_index("x"), lax.axis_index("y"), lax.axis_index("c")
    me = 4 * x + 2 * y + c

    def peer(bits):
        dev = (1 - x if bits & 4 else x, 1 - y if bits & 2 else y, 1 - c if bits & 1 else c)
        return dev, 4 * dev[0] + 2 * dev[1] + dev[2]

    plan = PEERS[kind]
    remote, local = [], []
    for a, l in enumerate(lands):
        for d, bits in enumerate(plan):
            dev, pid = peer(bits)
            if kind == "forward":
                src, dst, dev = l.at[pid], l.at[peer(bits | 1)[1] if waiting else pid], peer(1)[0]
            else:
                src, dst = (srcs[a].at[pid] if kind == "scatter" else srcs[a]), l.at[pid if waiting else me]
            remote.append(pltpu.make_async_remote_copy(
                src_ref=src, dst_ref=dst, send_sem=send_sems.at[a * len(plan) + d], recv_sem=recv_sems.at[a * len(plan) + d],
                device_id=dev, device_id_type=MESH_ID))
        if kind != "forward":
            local.append(pltpu.make_async_copy(srcs[a].at[me] if kind == "scatter" else srcs[a], l.at[me], local_sems.at[a]))
    return remote, local


def _spread_start(srcs, kind, name, after=None, lands=None):
    if kind == "forward":
        srcs = []
    else:
        shapes = [a.shape if kind == "scatter" else (N_DEV,) + a.shape for a in srcs]
        lands = [lax.empty(shp, a.dtype) for shp, a in zip(shapes, srcs)]
    ns, nl, per = len(srcs), len(lands), len(PEERS[kind])
    extra = [] if after is None else [after]
    sem_shapes = [pltpu.SemaphoreType.DMA((nl * per,))] * 2 + ([pltpu.SemaphoreType.DMA((nl,))] if ns else [])

    def body(*refs):
        src_refs, land_refs = refs[:ns], refs[ns:ns + nl]
        sems = refs[ns + nl + len(extra):ns + nl + len(extra) + len(sem_shapes)]
        remote, local = _spread_copies(src_refs, land_refs, sems[0], sems[1], sems[2] if ns else None, kind, False)
        for cp in remote + local:
            cp.start()
        refs[-1][...] = jnp.zeros((8, 128), F32)

    outs = pl.pallas_call(
        body, name=name,
        out_shape=(*sem_shapes, *[pltpu.HBM(a.shape, a.dtype) for a in srcs + lands], jax.ShapeDtypeStruct((8, 128), F32)),
        in_specs=[HBM] * (ns + nl) + [pl.BlockSpec(memory_space=pl.ANY)] * len(extra),
        out_specs=(*[SEM] * len(sem_shapes), *[HBM] * (ns + nl), pl.BlockSpec(memory_space=pltpu.VMEM)),
        input_output_aliases={i: len(sem_shapes) + i for i in range(ns + nl)},
        compiler_params=pltpu.CompilerParams(has_side_effects=EFFECT),
    )(*[pltpu.with_memory_space_constraint(a, pltpu.HBM) for a in srcs + lands], *extra)
    k = len(sem_shapes)
    return outs[:k], list(outs[k:k + ns]), list(outs[k + ns:k + ns + nl]), outs[-1]


def _spread_wait(sems, srcs, lands, after, kind, name):
    ns, nl = len(srcs), len(lands)
    after = list(after) if isinstance(after, (list, tuple)) else [after]

    def body(*refs):
        src_refs, land_refs = refs[:ns], refs[ns:ns + nl]
        s = refs[ns + nl:ns + nl + len(sems)]
        remote, local = _spread_copies(src_refs, land_refs, s[0], s[1], s[2] if ns else None, kind, True)
        for cp in remote:
            cp.wait_send()
            cp.wait_recv()
        for cp in local:
            cp.wait()

    outs = pl.pallas_call(
        body, name=name, out_shape=tuple(pltpu.HBM(a.shape, a.dtype) for a in srcs + lands),
        in_specs=[HBM] * (ns + nl) + [SEM] * len(sems) + [pl.BlockSpec(memory_space=pl.ANY)] * len(after), out_specs=tuple([HBM] * (ns + nl)),
        input_output_aliases={i: i for i in range(ns + nl)}, compiler_params=pltpu.CompilerParams(has_side_effects=EFFECT),
    )(*srcs, *lands, *sems, *after)
    return list(outs[ns:])


SMALL = ("rel_bias", "norm1_g", "sgu_w", "sgu_b", "dil_qn_g", "dil_kn_g", "conv_w", "conv_b", "conv_ln_g", "conv_ln_b",
         "gqa_qn_g", "gqa_kn_g", "mix_norm_g", "norm2_g")
LARGE = ("w_in", "w_out", "w_gate", "w_up", "w_down")
EARLY = tuple(k for k in SMALL if k != "norm1_g")


def _local_step(x, target, p, B, S, fetch, emit, mid, early):
    T = B * S
    rope = _rope_tables(S)
    win = _bias_windows(p["rel_bias"], S)
    tile8 = lambda g: jnp.tile(g.reshape(1, HEAD_DIM), (1, N_HEADS))
    cols_b = (COL_BQ, COL_BK, COL_BV)
    cols_d = (COL_DQ, COL_DK128, COL_DV128)
    saved = []
    for l in range(DEPTH):
        s = {"x": x}
        s["ws"] = p["sgu_w"][l].astype(BF16)
        s["bias"] = jnp.repeat(p["sgu_b"][l].T, HEAD_DIM, axis=1)
        s["h"] = _rms_fwd(x, p["norm1_g"][l], f"rms1_fwd_{l}")
        s["win"] = fetch(l, "in", s["h"])
        s["cw"] = jnp.pad(s["win"]["conv_w"], ((0, 1), (0, 0))).reshape(32, 1, 512)
        z = s["z"] = _matmul(s["h"], s["win"]["w_in"], "nn", f"in_proj_{l}", tk=D_MODEL)
        s["bias"] = s["bias"] + mid(l, z)
        s["ya"] = _sgu_fwd(z, s["ws"], s["bias"], f"sgu_fwd_{l}")
        s["c"] = _conv_fwd1(z, s["cw"], p["conv_b"][l].reshape(1, 512), B, S, f"conv_fwd_{l}")
        s["yc"] = _conv_fwd2(s["c"], p["conv_ln_g"][l].reshape(1, 512), p["conv_ln_b"][l].reshape(1, 512), f"conv_ln_fwd_{l}")
        s["gb"] = (tile8(p["dil_qn_g"][l]), tile8(p["dil_kn_g"][l]))
        s["gd"] = (tile8(p["gqa_qn_g"][l]), tile8(p["gqa_kn_g"][l])[:, :KV_WIDTH])
        s["qkv_b"] = _prep_fwd(z, *s["gb"], None, B, S, N_HEADS, cols_b, f"prep_b_fwd_{l}")
        s["qkv_d"] = _prep_fwd(z, *s["gd"], rope, B, S, KV_HEADS, cols_d, f"prep_d_fwd_{l}")
        s["ob"] = _attn_fwd(*s["qkv_b"], win, f"attn_b_fwd_{l}")
        s["od"] = _attn_fwd(*s["qkv_d"], None, f"attn_d_fwd_{l}", tq=GQA_TQ)
        s["gmix"] = p["mix_norm_g"][l].reshape(1, 2048)
        s["ycat"] = _mix_fwd(s["ya"], s["ob"], s["yc"], s["od"], s["gmix"], B, S, f"mix_fwd_{l}")
        s["wout"] = fetch(l, "out", s["ycat"])["w_out"]
        x1 = s["x1"] = _matmul(s["ycat"], s["wout"], "nn", f"out_proj_{l}", res=x, tk=D_MODEL)
        s["h2"] = _rms_fwd(x1, p["norm2_g"][l], f"rms2_fwd_{l}")
        s["ffn"] = fetch(l, "ffn", s["h2"])
        s["gate"] = _mm_shard_out(s["h2"], s["ffn"]["w_gate"], "nn", f"ffn_gate_{l}", out_dtype=BF16, tm=1024)
        s["up"] = _mm_shard_out(s["h2"], s["ffn"]["w_up"], "nn", f"ffn_up_{l}", out_dtype=BF16, tm=1024)
        x, s["act"] = _ffn_down(s["gate"], s["up"], s["ffn"]["w_down"], x1, f"ffn_down_{l}")
        saved.append(s)

    loss_blk, dx = _loss_grad(x, target, "loss")
    g = {k: [None] * DEPTH for k in SMALL if k != "rel_bias"}
    dwin_total = None
    for l in reversed(range(DEPTH)):
        s = saved[l]
        z, ffn = s["z"], s["ffn"]
        dgate, dup = _ffn_down_dx(dx, ffn["w_down"], s["gate"], s["up"], f"ffn_down_dx_{l}")
        tok = emit(l, "w_down", _mm_shard_m(s["act"], dx, f"ffn_down_dw_{l}", out_dtype=BF16, tn=512, tk=T))
        tok += emit(l, "w_gate", _mm_shard_out(s["h2"], dgate, "tn", f"ffn_gate_dw_{l}", out_dtype=BF16, tm=1024, tk=T))
        tok += emit(l, "w_up", _mm_shard_out(s["h2"], dup, "tn", f"ffn_up_dw_{l}", out_dtype=BF16, tm=1024, tk=T))
        dh2 = _mm_shard_k([(dgate, ffn["w_gate"]), (dup, ffn["w_up"])], "nt", f"ffn_up_dx_{l}", tn=512, fold=N_DEV)
        dx1, dg2 = _rms_bwd(dh2, s["x1"], p["norm2_g"][l] + tok, dx, f"rms2_bwd_{l}")
        g["norm2_g"][l] = dg2[0]
        dycat = _matmul(dx1, s["wout"], "nt", f"out_proj_dx_{l}", tk=D_MODEL)
        tok = emit(l, "w_out", _matmul(s["ycat"], dx1, "tn", f"out_proj_dw_{l}", out_dtype=BF16, tn=1024, tk=T))
        dya, dob, dyc, dod, dgm = _mix_bwd(s["ya"], s["ob"], s["yc"], s["od"], dycat, s["gmix"] + tok, B, S, f"mix_bwd_{l}")
        g["mix_norm_g"][l] = dgm[0]
        dz_a, dws, dbias = _sgu_bwd(z, dya, s["ws"], jnp.swapaxes(s["ws"], 1, 2), s["bias"], f"sgu_bwd_{l}")
        g["sgu_w"][l] = dws
        g["sgu_b"][l] = dbias.reshape(128, 8, HEAD_DIM).sum(-1).T
        dc, dlg, dlb, dcb = _conv_bwd1(s["c"], dyc, p["conv_ln_g"][l].reshape(1, 512), p["conv_ln_b"][l].reshape(1, 512), f"conv_ln_bwd_{l}")
        g["conv_ln_g"][l], g["conv_ln_b"][l], g["conv_b"][l] = dlg[0], dlb[0], dcb[0]
        dz_ca, dz_cg, dcw = _conv_bwd2(z, dc, s["cw"], B, S, f"conv_bwd_{l}")
        g["conv_w"][l] = dcw.reshape(32, 512)[:CONV_WIDTH]
        dq, dk, dv, dwin = _attn_bwd(*s["qkv_b"], s["ob"], dob, win, f"attn_b_bwd_{l}")
        dwin_total = dwin if dwin_total is None else dwin_total + dwin
        dz_b, dgq, dgk = _prep_bwd(z, dq, dk, dv, *s["gb"], None, B, S, N_HEADS, cols_b, f"prep_b_bwd_{l}")
        g["dil_qn_g"][l] = dgq.reshape(N_HEADS, HEAD_DIM).sum(0)
        g["dil_kn_g"][l] = dgk.reshape(N_HEADS, HEAD_DIM).sum(0)
        dq, dk, dv = _attn_bwd(*s["qkv_d"], s["od"], dod, None, f"attn_d_bwd_{l}", tq=GQA_TQ)
        dz_d, dgq, dgk = _prep_bwd(z, dq, dk, dv, *s["gd"], rope, B, S, KV_HEADS, cols_d, f"prep_d_bwd_{l}")
        g["gqa_qn_g"][l] = dgq.reshape(N_HEADS, HEAD_DIM).sum(0)
        g["gqa_kn_g"][l] = dgk.reshape(KV_HEADS, HEAD_DIM).sum(0)
        dz = jnp.concatenate([dz_a, dz_b, dz_ca, dz_cg, dz_d], axis=1)
        tok = jnp.zeros((), F32)
        if l == 0:
            done = {k: jnp.stack(v) for k, v in g.items() if k != "norm1_g"}
            done["rel_bias"] = _bias_fold(dwin_total, S, "bias_fold")
            tok = early(done)
        tok += emit(l, "w_in", _matmul(s["h"], dz, "tn", f"in_proj_dw_{l}", out_dtype=BF16, tk=T))
        dh = _matmul(dz, s["win"]["w_in"], "nt", f"in_proj_dx_{l}", tn=1024, tk=IN_WIDTH)
        dx, dg1 = _rms_bwd(dh, s["x"], p["norm1_g"][l] + tok, dx1, f"rms1_bwd_{l}")
        g["norm1_g"][l] = dg1[0]

    return loss_blk[0, 0], dx, jnp.stack(g["norm1_g"])


GROUPS = {"in": ("w_in",), "out": ("w_out",), "ffn": ("w_gate", "w_up", "w_down")}
COL_SHARDED = ("w_in", "w_gate", "w_up")


def kernel(x, rel_bias, norm1_g, w_in, sgu_w, sgu_b, dil_qn_g, dil_kn_g, conv_w, conv_b, conv_ln_g, conv_ln_b, gqa_qn_g, gqa_kn_g, mix_norm_g, w_out, norm2_g, w_gate, w_up, w_down, loss_target, m_rel_bias, m_norm1_g, m_w_in, m_sgu_w, m_sgu_b, m_dil_qn_g, m_dil_kn_g, m_conv_w, m_conv_b, m_conv_ln_g, m_conv_ln_b, m_gqa_qn_g, m_gqa_kn_g, m_mix_norm_g, m_w_out, m_norm2_g, m_w_gate, m_w_up, m_w_down, v_rel_bias, v_norm1_g, v_w_in, v_sgu_w, v_sgu_b, v_dil_qn_g, v_dil_kn_g, v_conv_w, v_conv_b, v_conv_ln_g, v_conv_ln_b, v_gqa_qn_g, v_gqa_kn_g, v_mix_norm_g, v_w_out, v_norm2_g, v_w_gate, v_w_up, v_w_down):
    w = dict(rel_bias=rel_bias, norm1_g=norm1_g, w_in=w_in, sgu_w=sgu_w, sgu_b=sgu_b, dil_qn_g=dil_qn_g, dil_kn_g=dil_kn_g, conv_w=conv_w,
             conv_b=conv_b, conv_ln_g=conv_ln_g, conv_ln_b=conv_ln_b, gqa_qn_g=gqa_qn_g, gqa_kn_g=gqa_kn_g, mix_norm_g=mix_norm_g,
             w_out=w_out, norm2_g=norm2_g, w_gate=w_gate, w_up=w_up, w_down=w_down)
    m = dict(rel_bias=m_rel_bias, norm1_g=m_norm1_g, w_in=m_w_in, sgu_w=m_sgu_w, sgu_b=m_sgu_b, dil_qn_g=m_dil_qn_g, dil_kn_g=m_dil_kn_g,
             conv_w=m_conv_w, conv_b=m_conv_b, conv_ln_g=m_conv_ln_g, conv_ln_b=m_conv_ln_b, gqa_qn_g=m_gqa_qn_g, gqa_kn_g=m_gqa_kn_g,
             mix_norm_g=m_mix_norm_g, w_out=m_w_out, norm2_g=m_norm2_g, w_gate=m_w_gate, w_up=m_w_up, w_down=m_w_down)
    v = dict(rel_bias=v_rel_bias, norm1_g=v_norm1_g, w_in=v_w_in, sgu_w=v_sgu_w, sgu_b=v_sgu_b, dil_qn_g=v_dil_qn_g, dil_kn_g=v_dil_kn_g,
             conv_w=v_conv_w, conv_b=v_conv_b, conv_ln_g=v_conv_ln_g, conv_ln_b=v_conv_ln_b, gqa_qn_g=v_gqa_qn_g, gqa_kn_g=v_gqa_kn_g,
             mix_norm_g=v_mix_norm_g, w_out=v_w_out, norm2_g=v_norm2_g, w_gate=v_w_gate, w_up=v_w_up, w_down=v_w_down)
    names = list(w)
    B, S, D = x.shape
    T = B * S
    me = 4 * lax.axis_index("x") + 2 * lax.axis_index("y") + lax.axis_index("c")

    bf = {k: w[k].astype(BF16) for k in LARGE}
    spreads, forwards = {}, {}

    def start_gather(l, group, after=None):
        srcs = [bf[k][l] for k in GROUPS[group]] + ([conv_w[l]] if group == "in" else [])
        spreads[l, group] = _spread_start(srcs, "chips", f"gather_{group}_{l}_start", after)
        return spreads[l, group][3][0, 0]

    def forward(l, group, after):
        sems, srcs, lands, _ = spreads[l, group]
        lands = _spread_wait(sems, srcs, lands, after, "chips", f"gather_{group}_{l}_wait")
        forwards[l, group] = _spread_start(None, "forward", f"forward_{group}_{l}_start", lands=lands)
        return forwards[l, group][3]

    def landed(l, group, after):
        sems, _, lands, _ = forwards[l, group]
        return _spread_wait(sems, [], lands, after, "forward", f"forward_{group}_{l}_wait")

    tok0 = start_gather(0, "in") + start_gather(0, "out") + start_gather(0, "ffn")
    small = {k: w[k] for k in SMALL}
    small["norm1_g"] = norm1_g.at[0].add(tok0)

    def mid(l, z):
        if l > 0:
            return jnp.zeros((), F32)
        return start_gather(1, "in", z) + start_gather(1, "out", z) + start_gather(1, "ffn", z)

    def fetch(l, group, after):
        if group == "in":
            tok = forward(0, "in", after) if l == 0 else after
        elif group == "out":
            tok = forward(l, "ffn", [after, forward(l, "out", after)])
        else:
            tok = forward(1, "in", after) if l == 0 else after
        got = dict(zip(GROUPS[group] + ("conv_w",), landed(l, group, [after, tok])))
        if group == "in":
            got["w_in"] = jnp.transpose(got["w_in"], (1, 0, 2)).reshape(D, IN_WIDTH)
            got["conv_w"] = jnp.transpose(got["conv_w"], (1, 0, 2)).reshape(CONV_WIDTH, 512)
        if group == "out":
            got["w_out"] = got["w_out"].reshape(D, D)
        return got

    scatters = {}

    def emit(l, k, dw):
        if k == "w_in":
            dw = jnp.transpose(dw.reshape(D, N_DEV, IN_WIDTH // N_DEV), (1, 0, 2))
        if k == "w_out":
            dw = dw.reshape(N_DEV, D // N_DEV, D)
        scatters[l, k] = _spread_start([dw], "scatter", f"scatter_{k}_{l}_start")
        return scatters[l, k][3][0, 0]

    flat2 = lambda a: a.reshape(-1, a.shape[-1])
    small_spread = []

    def early(done):
        small_spread.append(_spread_start([flat2(done[k]) for k in EARLY], "gather", "gather_small_grads_start"))
        return small_spread[0][3][0, 0]

    loss_part, dx, dnorm1 = _local_step(x.reshape(T, D), loss_target.reshape(T, D), small, B, S, fetch, emit, mid, early)
    loss = lax.psum(loss_part, ("x", "y", "c"))

    out_g, out_d, out_m, out_v = {}, {}, {}, {}

    def update_large(k, after):
        shp = w[k].shape
        two_d = lambda a: a.reshape(-1, shp[-1])
        res = None
        for l in reversed(range(DEPTH)):
            sems, srcs, lands, _ = scatters[l, k]
            stack = _spread_wait(sems, srcs, lands, after, "scatter", f"scatter_{k}_{l}_wait")[0]
            res = _adamw(two_d(w[k]), two_d(m[k]), two_d(v[k]), stack.reshape(N_DEV, -1, shp[-1]), f"adamw_{k}_{l}", layer=l, prev=res)
        out_g[k], out_d[k], out_m[k], out_v[k] = [a.reshape(shp) for a in res]
        return res[0]

    late_sems, late_srcs, late_lands, late_tok = _spread_start([flat2(dnorm1)], "gather", "gather_norm1_grad_start")
    after = [dx, late_tok]
    for k in ("w_down", "w_gate", "w_up", "w_out"):
        after = update_large(k, after)
    sems, srcs, lands, _ = small_spread[0]
    stacks = dict(zip(EARLY, _spread_wait(sems, srcs, lands, after, "gather", "gather_small_grads_wait")))
    stacks["norm1_g"] = _spread_wait(late_sems, late_srcs, late_lands, after, "gather", "gather_norm1_grad_wait")[0]
    for k in SMALL:
        stack = stacks[k]
        if k == "conv_w":
            stack = lax.dynamic_slice_in_dim(stack, me * (512 // N_DEV), 512 // N_DEV, axis=2)
        res = _adamw(flat2(w[k]), flat2(m[k]), flat2(v[k]), stack, f"adamw_{k}")
        out_g[k], out_d[k], out_m[k], out_v[k] = [a.reshape(w[k].shape) for a in res]
        after = res[0]
    update_large("w_in", after)

    return (loss, dx.reshape(B, S, D), *[out_g[k] for k in names], *[out_d[k] for k in names],
            *[out_m[k] for k in names], *[out_v[k] for k in names])
```

```python
import functools
import math

import numpy as np
import jax
import jax.numpy as jnp
from jax import lax
from jax.experimental import pallas as pl
from jax.experimental.pallas import tpu as pltpu

F32 = jnp.float32
BF16 = jnp.bfloat16
HIGHEST = lax.Precision.HIGHEST
MESH_ID = pl.DeviceIdType.MESH

D_MODEL = 2048
DEPTH = 2
HEAD_DIM = 64
GROUP_WIDTH = 512
N_HEADS = 8
KV_HEADS = 2
KV_WIDTH = 128
SGU_CHUNK = 128
CONV_WIDTH = 31
CONV_PAD = 16
GRID_W = 64
ROPE_THETA = 10000.0
REL_BUCKETS = 32
REL_MAX_DIST = 1024
DIL_PATTERNS = ((128, 1), (512, 4), (2048, 16))
FFN_HIDDEN = 5632
IN_WIDTH = 4352
RMS_EPS = 1e-6
LN_EPS = 1e-5
MASKED = -1e30
N_DEV = 8

ADAM_LR = 0.001
ADAM_B1 = 0.9
ADAM_B2 = 0.999
ADAM_EPS = 1e-08
ADAM_WD = 0.01
ADAM_STEP = 10

COL_AU, COL_AV, COL_BQ, COL_BK, COL_BV, COL_CA, COL_CG, COL_DQ = range(8)
COL_DK128, COL_DV128 = 32, 33

VMEM_LIMIT = 56 * 1024 * 1024
ATTN_TQ = 256
GQA_TQ = 512


def _params(sem=None, vmem=VMEM_LIMIT):
    return pltpu.CompilerParams(dimension_semantics=sem, vmem_limit_bytes=vmem)


def _dot(a, b, dims, precision=None):
    return lax.dot_general(a, b, (dims, ((), ())), precision=precision, preferred_element_type=F32)


def _nn(a, b, precision=None):
    return _dot(a, b, ((1,), (0,)), precision)


def _nt(a, b):
    return _dot(a, b, ((1,), (1,)))


def _tn(a, b):
    return _dot(a, b, ((0,), (0,)))


DIMS = {"nn": ((1,), (0,)), "nt": ((1,), (1,)), "tn": ((0,), (0,))}


def _pick(n, cands):
    for c in cands:
        if n % c == 0:
            return c
    return n


def _mm_call(name, mode, pairs, specs, o_spec, out_sds, grid, acc_shape, res=None, fold=None):
    npair, nk, dims = len(pairs), grid[2], DIMS[mode]

    def body(*refs):
        ab = refs[:2 * npair]
        r_ref = refs[2 * npair] if res is not None else None
        o_ref = refs[2 * npair + (res is not None)]
        part = None
        for t in range(npair):
            for s in ([None] if fold is None else range(fold)):
                a_blk = ab[2 * t][...] if s is None else ab[2 * t][s]
                b_blk = ab[2 * t + 1][...] if s is None else ab[2 * t + 1][s]
                d = _dot(a_blk.astype(BF16), b_blk.astype(BF16), dims)
                part = d if part is None else part + d

        def finish(r):
            if r_ref is not None:
                r = r + r_ref[...]
            o_ref[...] = r.astype(o_ref.dtype)

        if nk == 1:
            finish(part)
            return
        acc, k = refs[-1], pl.program_id(2)

        @pl.when(k == 0)
        def _():
            acc[...] = part

        @pl.when(k > 0)
        def _():
            acc[...] += part

        @pl.when(k == nk - 1)
        def _():
            finish(acc[...])

    ins = [t for pair in pairs for t in pair]
    in_specs = [t for pair in specs for t in pair]
    if res is not None:
        ins.append(res)
        in_specs.append(o_spec)
    return pl.pallas_call(
        body, name=name, grid=grid, in_specs=in_specs, out_specs=o_spec, out_shape=out_sds,
        scratch_shapes=[pltpu.VMEM(acc_shape, F32)] if nk > 1 else [],
        compiler_params=_params(("parallel", "parallel", "arbitrary")),
    )(*ins)


def _matmul(a, b, mode, name, res=None, out_dtype=F32, tm=512, tn=None, tk=None):
    if mode == "nn":
        (M, K), N = a.shape, b.shape[1]
    elif mode == "nt":
        (M, K), N = a.shape, b.shape[0]
    else:
        (K, M), N = a.shape, b.shape[1]
    tm = min(tm, M)
    tn = tn or _pick(N, (2176, 2048, 1408, 1024, 512))
    tk = tk or _pick(K, (1024, 2176, 1408, 512))
    assert M % tm == 0 and N % tn == 0 and K % tk == 0, (M, N, K, tm, tn, tk)
    a_spec = pl.BlockSpec((tk, tm), lambda i, j, k: (k, i)) if mode == "tn" else pl.BlockSpec((tm, tk), lambda i, j, k: (i, k))
    b_spec = pl.BlockSpec((tn, tk), lambda i, j, k: (j, k)) if mode == "nt" else pl.BlockSpec((tk, tn), lambda i, j, k: (k, j))
    o_spec = pl.BlockSpec((tm, tn), lambda i, j, k: (i, j))
    return _mm_call(name, mode, [(a, b)], [(a_spec, b_spec)], o_spec, jax.ShapeDtypeStruct((M, N), out_dtype),
                    (M // tm, N // tn, K // tk), (tm, tn), res)


def _mm_shard_out(a, bs, mode, name, out_dtype=F32, tm=512, tk=None):
    J = bs.shape[0]
    n = bs.shape[1] if mode == "nt" else bs.shape[2]
    (K, M) = a.shape if mode == "tn" else a.shape[::-1]
    tm = min(tm, M)
    tk = tk or (K if mode != "tn" else _pick(K, (1024, 512)))
    a_spec = pl.BlockSpec((tk, tm), lambda j, i, k: (k, i)) if mode == "tn" else pl.BlockSpec((tm, tk), lambda j, i, k: (i, k))
    b_spec = pl.BlockSpec((None, n, tk), lambda j, i, k: (j, 0, k)) if mode == "nt" else pl.BlockSpec((None, tk, n), lambda j, i, k: (j, k, 0))
    o_spec = pl.BlockSpec((None, tm, n), lambda j, i, k: (j, i, 0))
    return _mm_call(name, mode, [(a, bs)], [(a_spec, b_spec)], o_spec, jax.ShapeDtypeStruct((J, M, n), out_dtype),
                    (J, M // tm, K // tk), (tm, n))


def _mm_shard_k(pairs, mode, name, res=None, out_dtype=F32, tm=512, tn=None, fold=1):
    J, M, n = pairs[0][0].shape
    N = pairs[0][1].shape[2] if mode == "nn" else pairs[0][1].shape[1]
    tm = min(tm, M)
    tn = tn or _pick(N, (2048, 1024, 512))
    a_spec = pl.BlockSpec((fold, tm, n), lambda i, j, k: (k, i, 0))
    b_spec = pl.BlockSpec((fold, n, tn), lambda i, j, k: (k, 0, j)) if mode == "nn" else pl.BlockSpec((fold, tn, n), lambda i, j, k: (k, j, 0))
    o_spec = pl.BlockSpec((tm, tn), lambda i, j, k: (i, j))
    return _mm_call(name, mode, pairs, [(a_spec, b_spec)] * len(pairs), o_spec, jax.ShapeDtypeStruct((M, N), out_dtype),
                    (M // tm, N // tn, J // fold), (tm, tn), res, fold)


def _mm_shard_m(as_, b, name, out_dtype=F32, tn=None, tk=512):
    J, K, n = as_.shape
    N = b.shape[1]
    tn = tn or _pick(N, (2048, 1024, 512))
    tk = min(tk, K)
    a_spec = pl.BlockSpec((None, tk, n), lambda j, i, k: (j, k, 0))
    b_spec = pl.BlockSpec((tk, tn), lambda j, i, k: (k, i))
    o_spec = pl.BlockSpec((None, n, tn), lambda j, i, k: (j, 0, i))
    return _mm_call(name, "tn", [(as_, b)], [(a_spec, b_spec)], o_spec, jax.ShapeDtypeStruct((J, n, N), out_dtype),
                    (J, N // tn, K // tk), (n, tn))


def _seg_matrix(width):
    return jnp.asarray(np.kron(np.eye(width // HEAD_DIM, dtype=np.float32), np.full((HEAD_DIM, HEAD_DIM), 1.0 / HEAD_DIM, np.float32)), BF16)


def _segmean(v, p):
    hi = v.astype(BF16)
    r = v - hi.astype(F32)
    mid = r.astype(BF16)
    lo = (r - mid.astype(F32)).astype(BF16)
    w = min(256, v.shape[1])
    pw = p[:w, :w]
    halves = []
    for c in range(v.shape[1] // w):
        cols = slice(c * w, (c + 1) * w)
        halves.append(_nn(hi[:, cols], pw) + _nn(mid[:, cols], pw) + _nn(lo[:, cols], pw))
    return halves[0] if len(halves) == 1 else jnp.concatenate(halves, axis=1)


def _gelu(x):
    c0 = math.sqrt(2.0 / math.pi)
    t = jnp.tanh(c0 * (x + 0.044715 * x * x * x))
    return 0.5 * x * (1.0 + t), t


def _gelu_grad(x, t):
    c0 = math.sqrt(2.0 / math.pi)
    return 0.5 * (1.0 + t) + 0.5 * x * (1.0 - t * t) * c0 * (1.0 + 3.0 * 0.044715 * x * x)


def _sigmoid(x):
    return 1.0 / (1.0 + jnp.exp(-x))


def _rms_fwd(x, g, name):
    T, D = x.shape
    tm = min(256, T)

    def body(x_ref, g_ref, o_ref):
        xv = x_ref[...]
        r = lax.rsqrt(jnp.mean(xv * xv, axis=-1, keepdims=True) + RMS_EPS)
        o_ref[...] = (xv * r * g_ref[...]).astype(BF16)

    return pl.pallas_call(
        body, name=name, grid=(T // tm,),
        in_specs=[pl.BlockSpec((tm, D), lambda i: (i, 0)), pl.BlockSpec((1, D), lambda i: (0, 0))],
        out_specs=pl.BlockSpec((tm, D), lambda i: (i, 0)), out_shape=jax.ShapeDtypeStruct((T, D), BF16),
        compiler_params=_params(("parallel",)),
    )(x, g.reshape(1, D))


def _rms_bwd(dh, x, g, dres, name):
    T, D = x.shape
    tm = min(256, T)

    def body(dh_ref, x_ref, g_ref, dres_ref, dx_ref, dg_ref):
        @pl.when(pl.program_id(0) == 0)
        def _():
            dg_ref[...] = jnp.zeros_like(dg_ref)

        xv, dhv = x_ref[...], dh_ref[...]
        r = lax.rsqrt(jnp.mean(xv * xv, axis=-1, keepdims=True) + RMS_EPS)
        y = xv * r
        dy = dhv * g_ref[...]
        dx_ref[...] = dres_ref[...] + r * (dy - y * jnp.mean(dy * y, axis=-1, keepdims=True))
        dg_ref[...] += jnp.sum(dhv * y, axis=0, keepdims=True)

    row = pl.BlockSpec((tm, D), lambda i: (i, 0))
    vec = pl.BlockSpec((1, D), lambda i: (0, 0))
    return pl.pallas_call(
        body, name=name, grid=(T // tm,), in_specs=[row, row, vec, row], out_specs=[row, vec],
        out_shape=[jax.ShapeDtypeStruct((T, D), F32), jax.ShapeDtypeStruct((1, D), F32)],
        compiler_params=_params(("arbitrary",)),
    )(dh, x, g.reshape(1, D), dres)


def _sgu_core(zu, zv, ws_ref, bias, p):
    ug, tu = _gelu(zu)
    vg, tv = _gelu(zv)
    xc = vg - _segmean(vg, p)
    rs = lax.rsqrt(_segmean(xc * xc, p) + LN_EPS)
    vn = xc * rs
    vnb = vn.astype(BF16)
    low = lax.broadcasted_iota(jnp.int32, (SGU_CHUNK, 128), 1) < HEAD_DIM
    parts = []
    for j in range(4):
        vp = vnb[:, 128 * j:128 * (j + 1)]
        parts.append(jnp.where(low, _nn(ws_ref[2 * j], vp), _nn(ws_ref[2 * j + 1], vp)))
    mixed = jnp.concatenate(parts, axis=1) + bias
    return ug, tu, tv, rs, vn, vnb, mixed, low


def _sgu_fwd(z, ws, bias, name):
    T = z.shape[0]

    def body(zu_ref, zv_ref, ws_ref, b_ref, p_ref, y_ref):
        ug, _, _, _, _, _, mixed, _ = _sgu_core(zu_ref[...], zv_ref[...], ws_ref, b_ref[...], p_ref[...])
        y_ref[...] = ug * mixed

    full = lambda shape: pl.BlockSpec(shape, lambda i: (0,) * len(shape))
    return pl.pallas_call(
        body, name=name, grid=(T // SGU_CHUNK,),
        in_specs=[pl.BlockSpec((SGU_CHUNK, 512), lambda i: (i, COL_AU)), pl.BlockSpec((SGU_CHUNK, 512), lambda i: (i, COL_AV)),
                  full((8, 128, 128)), full((128, 512)), full((512, 512))],
        out_specs=pl.BlockSpec((SGU_CHUNK, 512), lambda i: (i, 0)), out_shape=jax.ShapeDtypeStruct((T, 512), F32),
        compiler_params=_params(("parallel",)),
    )(z, z, ws, bias, _seg_matrix(512))


def _sgu_bwd(z, dy, ws, ws_t, bias, name):
    T = z.shape[0]

    def body(zu_ref, zv_ref, dy_ref, ws_ref, wst_ref, b_ref, p_ref, dz_ref, dws_ref, db_ref):
        @pl.when(pl.program_id(0) == 0)
        def _():
            dws_ref[...] = jnp.zeros_like(dws_ref)
            db_ref[...] = jnp.zeros_like(db_ref)

        zu, zv, p = zu_ref[...], zv_ref[...], p_ref[...]
        ug, tu, tv, rs, vn, vnb, mixed, low = _sgu_core(zu, zv, ws_ref, b_ref[...], p)
        dyv = dy_ref[...]
        dmixed = dyv * ug
        db_ref[...] += dmixed
        dmb = dmixed.astype(BF16)
        zero = jnp.zeros((SGU_CHUNK, 128), BF16)
        parts = []
        for j in range(4):
            dmp, vp = dmb[:, 128 * j:128 * (j + 1)], vnb[:, 128 * j:128 * (j + 1)]
            dws_ref[2 * j] += _nt(jnp.where(low, dmp, zero), vp)
            dws_ref[2 * j + 1] += _nt(jnp.where(low, zero, dmp), vp)
            parts.append(jnp.where(low, _nn(wst_ref[2 * j], dmp), _nn(wst_ref[2 * j + 1], dmp)))
        dvn = jnp.concatenate(parts, axis=1)
        dvg = rs * (dvn - _segmean(dvn, p) - vn * _segmean(dvn * vn, p))
        dz_ref[:, 0:512] = (dyv * mixed * _gelu_grad(zu, tu)).astype(BF16)
        dz_ref[:, 512:1024] = (dvg * _gelu_grad(zv, tv)).astype(BF16)

    full = lambda shape: pl.BlockSpec(shape, lambda i: (0,) * len(shape))
    return pl.pallas_call(
        body, name=name, grid=(T // SGU_CHUNK,),
        in_specs=[pl.BlockSpec((SGU_CHUNK, 512), lambda i: (i, COL_AU)), pl.BlockSpec((SGU_CHUNK, 512), lambda i: (i, COL_AV)),
                  pl.BlockSpec((SGU_CHUNK, 512), lambda i: (i, 0)), full((8, 128, 128)), full((8, 128, 128)), full((128, 512)), full((512, 512))],
        out_specs=[pl.BlockSpec((SGU_CHUNK, 1024), lambda i: (i, 0)), full((8, 128, 128)), full((128, 512))],
        out_shape=[jax.ShapeDtypeStruct((T, 1024), BF16), jax.ShapeDtypeStruct((8, 128, 128), F32), jax.ShapeDtypeStruct((128, 512), F32)],
        compiler_params=_params(("arbitrary",)),
    )(z, z, dy, ws, ws_t, bias, _seg_matrix(512))


CONV_ROWS = 256


def _conv_taps(pad_ref, w_ref, base, flip):
    acc = None
    for k in range(CONV_WIDTH):
        wk = w_ref[CONV_WIDTH - 1 - k if flip else k]
        t = wk * pad_ref[base + k + 1:base + k + 1 + CONV_ROWS, :]
        acc = t if acc is None else acc + t
    return acc


def _conv_fwd1(z, w, cb, B, S, name):
    T = B * S
    rows = min(CONV_ROWS, S)
    assert rows == CONV_ROWS

    def body(a_ref, g_ref, w_ref, cb_ref, c_ref, pad):
        pad[0:CONV_PAD, :] = jnp.zeros((CONV_PAD, 128), F32)
        pad[CONV_PAD + S:2 * CONV_PAD + S, :] = jnp.zeros((CONV_PAD, 128), F32)
        pad[CONV_PAD:CONV_PAD + S, :] = a_ref[...] * _sigmoid(g_ref[...])

        for base in range(0, S, CONV_ROWS):
            c_ref[base:base + CONV_ROWS, :] = _conv_taps(pad, w_ref, base, False) + cb_ref[...]

    return pl.pallas_call(
        body, name=name, grid=(4, B),
        in_specs=[pl.BlockSpec((S, 128), lambda j, b: (b, 4 * COL_CA + j)), pl.BlockSpec((S, 128), lambda j, b: (b, 4 * COL_CG + j)),
                  pl.BlockSpec((32, 1, 128), lambda j, b: (0, 0, j)), pl.BlockSpec((1, 128), lambda j, b: (0, j))],
        out_specs=pl.BlockSpec((S, 128), lambda j, b: (b, j)), out_shape=jax.ShapeDtypeStruct((T, 512), F32),
        scratch_shapes=[pltpu.VMEM((S + 2 * CONV_PAD, 128), F32)], compiler_params=_params(("parallel", "parallel")),
    )(z, z, w, cb)


def _ln_rows(c):
    mu = jnp.mean(c, axis=-1, keepdims=True)
    xc = c - mu
    rs = lax.rsqrt(jnp.mean(xc * xc, axis=-1, keepdims=True) + LN_EPS)
    return xc * rs, rs


def _conv_fwd2(c, lng, lnb, name):
    T = c.shape[0]
    tm = min(512, T)

    def body(c_ref, g_ref, b_ref, y_ref):
        n, _ = _ln_rows(c_ref[...])
        t = n * g_ref[...] + b_ref[...]
        y_ref[...] = t * _sigmoid(t)

    row = pl.BlockSpec((tm, 512), lambda i: (i, 0))
    vec = pl.BlockSpec((1, 512), lambda i: (0, 0))
    return pl.pallas_call(body, name=name, grid=(T // tm,), in_specs=[row, vec, vec], out_specs=row,
                          out_shape=jax.ShapeDtypeStruct((T, 512), F32), compiler_params=_params(("parallel",)))(c, lng, lnb)


def _conv_bwd1(c, dy, lng, lnb, name):
    T = c.shape[0]
    tm = min(512, T)

    def body(c_ref, dy_ref, g_ref, b_ref, dc_ref, dg_ref, db_ref, dcb_ref):
        @pl.when(pl.program_id(0) == 0)
        def _():
            dg_ref[...] = jnp.zeros_like(dg_ref)
            db_ref[...] = jnp.zeros_like(db_ref)
            dcb_ref[...] = jnp.zeros_like(dcb_ref)

        n, rs = _ln_rows(c_ref[...])
        t = n * g_ref[...] + b_ref[...]
        s = _sigmoid(t)
        dt = dy_ref[...] * s * (1.0 + t * (1.0 - s))
        dg_ref[...] += jnp.sum(dt * n, axis=0, keepdims=True)
        db_ref[...] += jnp.sum(dt, axis=0, keepdims=True)
        dn = dt * g_ref[...]
        dc = rs * (dn - jnp.mean(dn, axis=-1, keepdims=True) - n * jnp.mean(dn * n, axis=-1, keepdims=True))
        dc_ref[...] = dc
        dcb_ref[...] += jnp.sum(dc, axis=0, keepdims=True)

    row = pl.BlockSpec((tm, 512), lambda i: (i, 0))
    vec = pl.BlockSpec((1, 512), lambda i: (0, 0))
    vshape = jax.ShapeDtypeStruct((1, 512), F32)
    return pl.pallas_call(body, name=name, grid=(T // tm,), in_specs=[row, row, vec, vec], out_specs=[row, vec, vec, vec],
                          out_shape=[jax.ShapeDtypeStruct((T, 512), F32), vshape, vshape, vshape],
                          compiler_params=_params(("arbitrary",)))(c, dy, lng, lnb)


def _conv_bwd2(z, dc, w, B, S, name):
    T = B * S

    def body(a_ref, g_ref, dc_ref, w_ref, da_ref, dg_ref, dw_ref, hpad, dpad):
        @pl.when(pl.program_id(1) == 0)
        def _():
            dw_ref[...] = jnp.zeros_like(dw_ref)

        zeros = jnp.zeros((CONV_PAD, 128), F32)
        for ref in (hpad, dpad):
            ref[0:CONV_PAD, :] = zeros
            ref[CONV_PAD + S:2 * CONV_PAD + S, :] = zeros
        hpad[CONV_PAD:CONV_PAD + S, :] = a_ref[...] * _sigmoid(g_ref[...])
        dpad[CONV_PAD:CONV_PAD + S, :] = dc_ref[...]
        dws = [None] * CONV_WIDTH
        for base in range(0, S, CONV_ROWS):
            rows = slice(base, base + CONV_ROWS)
            dh = _conv_taps(dpad, w_ref, base, True)
            sg = _sigmoid(g_ref[rows, :])
            da_ref[rows, :] = (dh * sg).astype(BF16)
            dg_ref[rows, :] = (dh * a_ref[rows, :] * sg * (1.0 - sg)).astype(BF16)
            dcv = dc_ref[rows, :]
            for k in range(CONV_WIDTH):
                prod = dcv * hpad[base + k + 1:base + k + 1 + CONV_ROWS, :]
                part = jnp.sum(prod.reshape(CONV_ROWS // 8, 8, 128), axis=0)
                dws[k] = part if dws[k] is None else dws[k] + part
        for k in range(CONV_WIDTH):
            dw_ref[k] += jnp.sum(dws[k], axis=0, keepdims=True)

    return pl.pallas_call(
        body, name=name, grid=(4, B),
        in_specs=[pl.BlockSpec((S, 128), lambda j, b: (b, 4 * COL_CA + j)), pl.BlockSpec((S, 128), lambda j, b: (b, 4 * COL_CG + j)),
                  pl.BlockSpec((S, 128), lambda j, b: (b, j)), pl.BlockSpec((32, 1, 128), lambda j, b: (0, 0, j))],
        out_specs=[pl.BlockSpec((S, 128), lambda j, b: (b, j)), pl.BlockSpec((S, 128), lambda j, b: (b, j)),
                   pl.BlockSpec((32, 1, 128), lambda j, b: (0, 0, j))],
        out_shape=[jax.ShapeDtypeStruct((T, 512), BF16), jax.ShapeDtypeStruct((T, 512), BF16), jax.ShapeDtypeStruct((32, 1, 512), F32)],
        scratch_shapes=[pltpu.VMEM((S + 2 * CONV_PAD, 128), F32), pltpu.VMEM((S + 2 * CONV_PAD, 128), F32)],
        compiler_params=_params(("parallel", "arbitrary")),
    )(z, z, dc, w)


def _swap16(x):
    n = x.shape[1]
    first = (lax.broadcasted_iota(jnp.int32, x.shape, 1) % 32) < 16
    return jnp.where(first, pltpu.roll(x, n - 16, 1), pltpu.roll(x, 16, 1))


def _rope(x, cos, sin):
    return x * cos + _swap16(x) * sin


def _rope_t(dy, cos, sin):
    return dy * cos + _swap16(dy * sin)


def _qk_norm(x, p):
    r = lax.rsqrt(_segmean(x * x, p) + RMS_EPS)
    return x * r, r


def _store_heads(ref, val, n):
    for h in range(n):
        ref[h] = val[:, HEAD_DIM * h:HEAD_DIM * (h + 1)].astype(ref.dtype)


def _load_heads(ref, n):
    return jnp.concatenate([ref[h] for h in range(n)], axis=1)


def _prep_fwd(z, gq, gk, rope, B, S, kv_heads, cols, name):
    tm = min(256, S)
    ns = S // tm
    kw = kv_heads * HEAD_DIM
    scale = HEAD_DIM ** -0.5
    qc, kc, vc = cols

    def body(*refs):
        if rope is None:
            q_ref, k_ref, v_ref, gq_ref, gk_ref, p_ref, qo, ko, vo = refs
        else:
            q_ref, k_ref, v_ref, gq_ref, gk_ref, p_ref, cos_ref, sin_ref, qo, ko, vo = refs
        p = p_ref[...]
        qn, _ = _qk_norm(q_ref[...], p)
        kn, _ = _qk_norm(k_ref[...], p[:kw, :kw])
        qn, kn = qn * gq_ref[...], kn * gk_ref[...]
        if rope is not None:
            cos, sin = cos_ref[...], sin_ref[...]
            qn, kn = _rope(qn, cos, sin), _rope(kn, cos[:, :kw], sin[:, :kw])
        _store_heads(qo, qn * scale, N_HEADS)
        _store_heads(ko, kn, kv_heads)
        _store_heads(vo, v_ref[...], kv_heads)

    row = lambda w, c: pl.BlockSpec((tm, w), lambda b, i: (b * ns + i, c))
    const = lambda shape: pl.BlockSpec(shape, lambda b, i: (0,) * len(shape))
    heads = lambda n: pl.BlockSpec((None, n, tm, HEAD_DIM), lambda b, i: (b, 0, i, 0))
    ins = [z, z, z, gq, gk, _seg_matrix(512)]
    specs = [row(512, qc), row(kw, kc), row(kw, vc), const((1, 512)), const((1, kw)), const((512, 512))]
    if rope is not None:
        ins += list(rope)
        specs += [pl.BlockSpec((tm, 512), lambda b, i: (i, 0))] * 2
    return pl.pallas_call(
        body, name=name, grid=(B, ns), in_specs=specs, out_specs=[heads(N_HEADS), heads(kv_heads), heads(kv_heads)],
        out_shape=[jax.ShapeDtypeStruct((B, N_HEADS, S, HEAD_DIM), BF16), jax.ShapeDtypeStruct((B, kv_heads, S, HEAD_DIM), BF16),
                   jax.ShapeDtypeStruct((B, kv_heads, S, HEAD_DIM), BF16)],
        compiler_params=_params(("parallel", "parallel")),
    )(*ins)


def _prep_bwd(z, dq, dk, dv, gq, gk, rope, B, S, kv_heads, cols, name):
    T = B * S
    tm = min(256, S)
    ns = S // tm
    kw = kv_heads * HEAD_DIM
    scale = HEAD_DIM ** -0.5
    qc, kc, _ = cols

    def body(*refs):
        if rope is None:
            q_ref, k_ref, dq_ref, dk_ref, dv_ref, gq_ref, gk_ref, p_ref, dz_ref, dgq_ref, dgk_ref = refs
        else:
            q_ref, k_ref, dq_ref, dk_ref, dv_ref, gq_ref, gk_ref, p_ref, cos_ref, sin_ref, dz_ref, dgq_ref, dgk_ref = refs

        @pl.when((pl.program_id(0) == 0) & (pl.program_id(1) == 0))
        def _():
            dgq_ref[...] = jnp.zeros_like(dgq_ref)
            dgk_ref[...] = jnp.zeros_like(dgk_ref)

        p = p_ref[...]
        dqv = _load_heads(dq_ref, N_HEADS) * scale
        dkv = _load_heads(dk_ref, kv_heads)
        if rope is not None:
            cos, sin = cos_ref[...], sin_ref[...]
            dqv, dkv = _rope_t(dqv, cos, sin), _rope_t(dkv, cos[:, :kw], sin[:, :kw])

        def through_norm(xv, dy, g, pm, dg_ref):
            xh, r = _qk_norm(xv, pm)
            dg_ref[...] += jnp.sum(dy * xh, axis=0, keepdims=True)
            dxh = dy * g
            return r * (dxh - xh * _segmean(dxh * xh, pm))

        dz_ref[:, 0:512] = through_norm(q_ref[...], dqv, gq_ref[...], p, dgq_ref).astype(BF16)
        dz_ref[:, 512:512 + kw] = through_norm(k_ref[...], dkv, gk_ref[...], p[:kw, :kw], dgk_ref).astype(BF16)
        dz_ref[:, 512 + kw:512 + 2 * kw] = _load_heads(dv_ref, kv_heads).astype(BF16)

    row = lambda w, c: pl.BlockSpec((tm, w), lambda b, i: (b * ns + i, c))
    const = lambda shape: pl.BlockSpec(shape, lambda b, i: (0,) * len(shape))
    heads = lambda n: pl.BlockSpec((None, n, tm, HEAD_DIM), lambda b, i: (b, 0, i, 0))
    ins = [z, z, dq, dk, dv, gq, gk, _seg_matrix(512)]
    specs = [row(512, qc), row(kw, kc), heads(N_HEADS), heads(kv_heads), heads(kv_heads), const((1, 512)), const((1, kw)), const((512, 512))]
    if rope is not None:
        ins += list(rope)
        specs += [pl.BlockSpec((tm, 512), lambda b, i: (i, 0))] * 2
    return pl.pallas_call(
        body, name=name, grid=(B, ns), in_specs=specs, out_specs=[row(512 + 2 * kw, 0), const((1, 512)), const((1, kw))],
        out_shape=[jax.ShapeDtypeStruct((T, 512 + 2 * kw), BF16), jax.ShapeDtypeStruct((1, 512), F32), jax.ShapeDtypeStruct((1, kw), F32)],
        compiler_params=_params(("arbitrary", "arbitrary")),
    )(*ins)


def _toeplitz(win, tq, S):
    r = pltpu.roll(jnp.broadcast_to(win, (tq, S + tq)), 0, 1, stride=1, stride_axis=0)
    return r[:, tq:tq + S]


ATTN_HEADS = 4


def _attn_fwd(q, k, v, win, name, nh=ATTN_HEADS, tq=ATTN_TQ):
    B, H, S, _ = q.shape
    shared = k.shape[1] != H
    assert not shared or H // k.shape[1] == nh
    tq = min(tq, S)

    def body(*refs):
        if win is None:
            q_ref, k_ref, v_ref, o_ref = refs
        else:
            q_ref, k_ref, v_ref, w_ref, o_ref = refs
        kvs = [(k_ref[...], v_ref[...])] * nh if shared else [(k_ref[h], v_ref[h]) for h in range(nh)]
        scores = []
        for h in range(nh):
            s = _nt(q_ref[h], kvs[h][0])
            if win is not None:
                s = s + _toeplitz(w_ref[h], tq, S)
            scores.append(s)
        probs = []
        for s in scores:
            p = jnp.exp(s - jnp.max(s, axis=-1, keepdims=True))
            probs.append((p.astype(BF16), jnp.sum(p, axis=-1, keepdims=True)))
        for h, (p, l) in enumerate(probs):
            o_ref[h] = _nn(p, kvs[h][1]) / l

    qs = pl.BlockSpec((None, nh, tq, HEAD_DIM), lambda b, h, i: (b, h, i, 0))
    ks = (pl.BlockSpec((None, None, S, HEAD_DIM), lambda b, h, i: (b, h, 0, 0)) if shared
          else pl.BlockSpec((None, nh, S, HEAD_DIM), lambda b, h, i: (b, h, 0, 0)))
    ins, specs = [q, k, v], [qs, ks, ks]
    if win is not None:
        ins.append(win)
        specs.append(pl.BlockSpec((nh, None, 1, S + tq), lambda b, h, i: (h, i, 0, 0)))
    return pl.pallas_call(body, name=name, grid=(B, H // nh, S // tq), in_specs=specs, out_specs=qs,
                          out_shape=jax.ShapeDtypeStruct((B, H, S, HEAD_DIM), F32),
                          compiler_params=_params(("parallel", "parallel", "parallel")))(*ins)


def _attn_bwd(q, k, v, o, do, win, name, nh=ATTN_HEADS, tq=ATTN_TQ):
    B, H, S, _ = q.shape
    hkv = k.shape[1]
    shared = hkv != H
    assert not shared or H // hkv == nh
    tq = min(tq, S)
    nq = S // tq

    def body(*refs):
        if win is None:
            q_ref, k_ref, v_ref, o_ref, do_ref, dq_ref, dk_ref, dv_ref = refs
        else:
            q_ref, k_ref, v_ref, o_ref, do_ref, w_ref, rev_ref, dq_ref, dk_ref, dv_ref, dw_ref = refs

        @pl.when(pl.program_id(2) == 0)
        def _():
            dk_ref[...] = jnp.zeros_like(dk_ref)
            dv_ref[...] = jnp.zeros_like(dv_ref)

        kvs = [(k_ref[...], v_ref[...])] * nh if shared else [(k_ref[h], v_ref[h]) for h in range(nh)]
        qvs, dobs, scores, dps = [], [], [], []
        for h in range(nh):
            qv, dov = q_ref[h], do_ref[h]
            dob = dov.astype(BF16)
            s = _nt(qv, kvs[h][0])
            if win is not None:
                s = s + _toeplitz(w_ref[h], tq, S)
            dp = _nt(dob, kvs[h][1]) - jnp.sum(dov * o_ref[h], axis=-1, keepdims=True)
            qvs.append(qv)
            dobs.append(dob)
            scores.append(s)
            dps.append(dp)
        pbs, dsbs = [], []
        for s, dp in zip(scores, dps):
            p = jnp.exp(s - jnp.max(s, axis=-1, keepdims=True))
            p = p * (1.0 / jnp.sum(p, axis=-1, keepdims=True))
            pbs.append(p.astype(BF16))
            dsbs.append((p * dp).astype(BF16))
        dk_acc = dv_acc = None
        for h in range(nh):
            dvh, dkh = _tn(pbs[h], dobs[h]), _tn(dsbs[h], qvs[h])
            dq_ref[h] = _nn(dsbs[h], kvs[h][0])
            if shared:
                dv_acc = dvh if dv_acc is None else dv_acc + dvh
                dk_acc = dkh if dk_acc is None else dk_acc + dkh
            else:
                dv_ref[h] += dvh
                dk_ref[h] += dkh
            if win is not None:
                rev = _nn(rev_ref[...], dsbs[h])
                wide = jnp.concatenate([rev, jnp.zeros((tq, tq), F32)], axis=1)
                dw_ref[h] = jnp.sum(pltpu.roll(wide, 0, 1, stride=1, stride_axis=0), axis=0, keepdims=True)
        if shared:
            dv_ref[...] += dv_acc
            dk_ref[...] += dk_acc

    qs = pl.BlockSpec((None, nh, tq, HEAD_DIM), lambda b, h, i: (b, h, i, 0))
    ks = (pl.BlockSpec((None, None, S, HEAD_DIM), lambda b, h, i: (b, h, 0, 0)) if shared
          else pl.BlockSpec((None, nh, S, HEAD_DIM), lambda b, h, i: (b, h, 0, 0)))
    ins, specs = [q, k, v, o, do], [qs, ks, ks, qs, qs]
    outs = [jax.ShapeDtypeStruct((B, H, S, HEAD_DIM), F32), jax.ShapeDtypeStruct((B, hkv, S, HEAD_DIM), F32), jax.ShapeDtypeStruct((B, hkv, S, HEAD_DIM), F32)]
    ospecs = [qs, ks, ks]
    if win is not None:
        ins += [win, jnp.asarray(np.eye(tq, dtype=np.float32)[::-1].copy(), BF16)]
        specs += [pl.BlockSpec((nh, None, 1, S + tq), lambda b, h, i: (h, i, 0, 0)), pl.BlockSpec((tq, tq), lambda b, h, i: (0, 0))]
        outs.append(jax.ShapeDtypeStruct((B, H, nq, 1, S + tq), F32))
        ospecs.append(pl.BlockSpec((None, nh, None, 1, S + tq), lambda b, h, i: (b, h, i, 0, 0)))
    return pl.pallas_call(body, name=name, grid=(B, H // nh, nq), in_specs=specs, out_specs=ospecs, out_shape=outs,
                          compiler_params=_params(("parallel", "parallel", "arbitrary")))(*ins)


def _pattern_count(delta):
    n = jnp.zeros(delta.shape, jnp.int32)
    for window, dil in DIL_PATTERNS:
        n = n + ((delta % dil == 0) & (jnp.abs(delta) <= window // 2)).astype(jnp.int32)
    return n


def _t5_bucket(rel):
    nb = REL_BUCKETS // 2
    max_exact = nb // 2
    ret = jnp.where(rel > 0, nb, 0)
    n = jnp.abs(rel)
    nf = jnp.maximum(n, 1).astype(F32)
    large = max_exact + (jnp.log(nf / max_exact) / math.log(REL_MAX_DIST / max_exact) * (nb - max_exact)).astype(jnp.int32)
    large = jnp.minimum(large, nb - 1)
    return ret + jnp.where(n < max_exact, n, large)


def _bias_windows(rel_bias, S):
    tq = min(ATTN_TQ, S)
    nq = S // tq
    n = nq * (S + tq)
    delta = (jnp.arange(S + tq)[None, :] - (jnp.arange(nq)[:, None] + 1) * tq).reshape(n)
    count = _pattern_count(delta)
    onehot = (_t5_bucket(delta)[None, :] == jnp.arange(REL_BUCKETS)[:, None]).astype(F32)
    extra = jnp.where(count > 0, jnp.log(jnp.maximum(count, 1).astype(F32)), MASKED).reshape(1, n)
    live = (count > 0).astype(F32).reshape(1, n)

    def body(t_ref, oh_ref, live_ref, extra_ref, o_ref):
        o_ref[...] = _nn(t_ref[...], oh_ref[...], HIGHEST) * live_ref[...] + extra_ref[...]

    val = pl.pallas_call(body, name="bias_windows", out_shape=jax.ShapeDtypeStruct((N_HEADS, n), F32),
                         compiler_params=_params())(rel_bias.T, onehot, live, extra)
    return val.reshape(N_HEADS, nq, 1, S + tq)


def _bias_fold(dwin, S, name):
    B, H, nq = dwin.shape[:3]
    tq = min(ATTN_TQ, S)
    n = nq * (S + tq)
    delta = (jnp.arange(S + tq)[None, :] - (tq - 1) - jnp.arange(nq)[:, None] * tq).reshape(n)
    onehot = (_t5_bucket(delta)[:, None] == jnp.arange(128)[None, :]).astype(F32)

    def body(d_ref, oh_ref, o_ref):
        tot = d_ref[0]
        for b in range(1, B):
            tot = tot + d_ref[b]
        o_ref[...] = _nn(tot, oh_ref[...], HIGHEST)

    out = pl.pallas_call(body, name=name, out_shape=jax.ShapeDtypeStruct((H, 128), F32), compiler_params=_params())(dwin.reshape(B, H, n), onehot)
    return out[:, :REL_BUCKETS].T


def _rope_tables(S):
    half = 16
    freqs = ROPE_THETA ** (-jnp.arange(half, dtype=F32) / half)
    t = jnp.arange(S)
    ang_r = (t // GRID_W).astype(F32)[:, None] * freqs[None, :]
    ang_c = (t % GRID_W).astype(F32)[:, None] * freqs[None, :]
    cos = jnp.concatenate([jnp.cos(ang_r)] * 2 + [jnp.cos(ang_c)] * 2, axis=1)
    sin = jnp.concatenate([-jnp.sin(ang_r), jnp.sin(ang_r), -jnp.sin(ang_c), jnp.sin(ang_c)], axis=1)
    return jnp.tile(cos, (1, N_HEADS)), jnp.tile(sin, (1, N_HEADS))


def _mix_fwd(ya, ob, yc, od, gain, B, S, name):
    T = B * S
    tm = min(256, S)
    ns = S // tm

    def body(ya_ref, ob_ref, yc_ref, od_ref, g_ref, o_ref):
        ys = [ya_ref[...], _load_heads(ob_ref, N_HEADS), yc_ref[...], _load_heads(od_ref, N_HEADS)]
        for m, y in enumerate(ys):
            r = lax.rsqrt(jnp.mean(y * y, axis=-1, keepdims=True) + RMS_EPS)
            o_ref[:, 512 * m:512 * (m + 1)] = (y * r * g_ref[:, 512 * m:512 * (m + 1)]).astype(BF16)

    row = pl.BlockSpec((tm, 512), lambda b, i: (b * ns + i, 0))
    heads = pl.BlockSpec((None, N_HEADS, tm, HEAD_DIM), lambda b, i: (b, 0, i, 0))
    return pl.pallas_call(
        body, name=name, grid=(B, ns), in_specs=[row, heads, row, heads, pl.BlockSpec((1, 2048), lambda b, i: (0, 0))],
        out_specs=pl.BlockSpec((tm, 2048), lambda b, i: (b * ns + i, 0)), out_shape=jax.ShapeDtypeStruct((T, 2048), BF16),
        compiler_params=_params(("parallel", "parallel")),
    )(ya, ob, yc, od, gain)


def _mix_bwd(ya, ob, yc, od, dycat, gain, B, S, name):
    T = B * S
    tm = min(256, S)
    ns = S // tm

    def body(ya_ref, ob_ref, yc_ref, od_ref, dy_ref, g_ref, dya_ref, dob_ref, dyc_ref, dod_ref, dg_ref):
        @pl.when((pl.program_id(0) == 0) & (pl.program_id(1) == 0))
        def _():
            dg_ref[...] = jnp.zeros_like(dg_ref)

        ys = [ya_ref[...], _load_heads(ob_ref, N_HEADS), yc_ref[...], _load_heads(od_ref, N_HEADS)]
        outs = [dya_ref, dob_ref, dyc_ref, dod_ref]
        for m, y in enumerate(ys):
            cols = slice(512 * m, 512 * (m + 1))
            r = lax.rsqrt(jnp.mean(y * y, axis=-1, keepdims=True) + RMS_EPS)
            yh = y * r
            dh = dy_ref[:, cols]
            dg_ref[:, cols] += jnp.sum(dh * yh, axis=0, keepdims=True)
            dyh = dh * g_ref[:, cols]
            dyv = r * (dyh - yh * jnp.mean(dyh * yh, axis=-1, keepdims=True))
            if m % 2 == 0:
                outs[m][...] = dyv
            else:
                _store_heads(outs[m], dyv, N_HEADS)

    row = pl.BlockSpec((tm, 512), lambda b, i: (b * ns + i, 0))
    heads = pl.BlockSpec((None, N_HEADS, tm, HEAD_DIM), lambda b, i: (b, 0, i, 0))
    vec = pl.BlockSpec((1, 2048), lambda b, i: (0, 0))
    flat = jax.ShapeDtypeStruct((T, 512), F32)
    hm = jax.ShapeDtypeStruct((B, N_HEADS, S, HEAD_DIM), F32)
    return pl.pallas_call(
        body, name=name, grid=(B, ns), in_specs=[row, heads, row, heads, pl.BlockSpec((tm, 2048), lambda b, i: (b * ns + i, 0)), vec],
        out_specs=[row, heads, row, heads, vec], out_shape=[flat, hm, flat, hm, jax.ShapeDtypeStruct((1, 2048), F32)],
        compiler_params=_params(("arbitrary", "arbitrary")),
    )(ya, ob, yc, od, dycat, gain)


def _ffn_down(gate, up, w_down, res, name):
    J, T, n = gate.shape
    N = w_down.shape[2]
    tm = min(256, T)

    def body(g_ref, u_ref, w_ref, r_ref, o_ref, act_ref):
        acc = None
        for j in range(J):
            g = g_ref[j].astype(F32)
            a = (g * _sigmoid(g) * u_ref[j].astype(F32)).astype(BF16)
            act_ref[j] = a
            d = _nn(a, w_ref[j])
            acc = d if acc is None else acc + d
        o_ref[...] = acc + r_ref[...]

    gu = pl.BlockSpec((J, tm, n), lambda i: (0, i, 0))
    row = pl.BlockSpec((tm, N), lambda i: (i, 0))
    return pl.pallas_call(body, name=name, grid=(T // tm,),
                          in_specs=[gu, gu, pl.BlockSpec((J, n, N), lambda i: (0, 0, 0)), row], out_specs=[row, gu],
                          out_shape=[jax.ShapeDtypeStruct((T, N), F32), jax.ShapeDtypeStruct((J, T, n), BF16)],
                          compiler_params=_params(("parallel",)))(gate, up, w_down, res)


def _ffn_down_dx(dx, w_down, gate, up, name):
    J, n, D = w_down.shape
    T = dx.shape[0]
    tm = min(1024, T)

    def body(dx_ref, w_ref, g_ref, u_ref, dg_ref, du_ref):
        d = _nt(dx_ref[...].astype(BF16), w_ref[...])
        g = g_ref[...].astype(F32)
        s = _sigmoid(g)
        dg_ref[...] = (d * u_ref[...].astype(F32) * s * (1.0 + g * (1.0 - s))).astype(BF16)
        du_ref[...] = (d * g * s).astype(BF16)

    blk = pl.BlockSpec((None, tm, n), lambda j, i: (j, i, 0))
    shape = jax.ShapeDtypeStruct((J, T, n), BF16)
    return pl.pallas_call(body, name=name, grid=(J, T // tm),
                          in_specs=[pl.BlockSpec((tm, D), lambda j, i: (i, 0)), pl.BlockSpec((None, n, D), lambda j, i: (j, 0, 0)), blk, blk],
                          out_specs=[blk, blk], out_shape=[shape, shape], compiler_params=_params(("parallel", "parallel")))(dx, w_down, gate, up)


def _loss_grad(y, target, name):
    T, D = y.shape
    tm = min(256, T)
    n = T // tm

    def body(y_ref, t_ref, loss_ref, dy_ref, acc):
        i = pl.program_id(0)

        @pl.when(i == 0)
        def _():
            acc[...] = jnp.zeros_like(acc)

        err = y_ref[...] - t_ref[...]
        dy_ref[...] = err * (1.0 / D)
        acc[...] += jnp.sum((err * err).reshape(tm // 8, 8, D), axis=0)

        @pl.when(i == n - 1)
        def _():
            loss_ref[...] = jnp.full((8, 128), 0.5 / D, F32) * jnp.sum(acc[...])

    row = pl.BlockSpec((tm, D), lambda i: (i, 0))
    return pl.pallas_call(body, name=name, grid=(n,), in_specs=[row, row], out_specs=[pl.BlockSpec((8, 128), lambda i: (0, 0)), row],
                          out_shape=[jax.ShapeDtypeStruct((8, 128), F32), jax.ShapeDtypeStruct((T, D), F32)],
                          scratch_shapes=[pltpu.VMEM((8, D), F32)], compiler_params=_params(("arbitrary",)))(y, target)


def _row_tile(R):
    for cand in (512, 256, 128, 64, 32, 16, 8):
        if R % cand == 0:
            return cand
    return R


def _adamw(w, m, v, stack, name, layer=None, prev=None):
    n, R, C = stack.shape
    tm = _row_tile(R)
    nb = R // tm
    off = 0 if layer is None else layer * nb
    c1 = 1.0 - ADAM_B1 ** ADAM_STEP
    c2 = 1.0 - ADAM_B2 ** ADAM_STEP

    def body(w_ref, m_ref, v_ref, s_ref, *rest):
        g_ref, d_ref, mo_ref, vo_ref = rest[-4:]
        g = s_ref[0].astype(F32)
        for k in range(1, n):
            g = g + s_ref[k].astype(F32)
        mn = ADAM_B1 * m_ref[...] + (1.0 - ADAM_B1) * g
        vn = ADAM_B2 * v_ref[...] + (1.0 - ADAM_B2) * (g * g)
        g_ref[...] = g
        mo_ref[...] = mn
        vo_ref[...] = vn
        d_ref[...] = -ADAM_LR * ((mn / c1) / (jnp.sqrt(vn / c2) + ADAM_EPS) + ADAM_WD * w_ref[...])

    blk = pl.BlockSpec((tm, C), lambda i: (i + off, 0))
    ins = [w, m, v, stack]
    specs = [blk, blk, blk, pl.BlockSpec((n, tm, C), lambda i: (0, i, 0))]
    aliases = {}
    if prev is not None:
        ins += list(prev)
        specs += [pl.BlockSpec(memory_space=pl.ANY)] * 4
        aliases = {4 + t: t for t in range(4)}
    shape = jax.ShapeDtypeStruct(w.shape, F32)
    return pl.pallas_call(body, name=name, grid=(nb,), in_specs=specs, out_specs=[blk] * 4, out_shape=[shape] * 4,
                          input_output_aliases=aliases, compiler_params=_params(("parallel",)))(*ins)


HBM = pl.BlockSpec(memory_space=pltpu.HBM)
SEM = pl.BlockSpec(memory_space=pltpu.SEMAPHORE)
EFFECT = pltpu.SideEffectType.DATAFLOW_SIDE_EFFECTING


PEERS = {"scatter": (1, 2, 3, 4, 5, 6, 7), "gather": (1, 2, 3, 4, 5, 6, 7), "chips": (1, 2, 4, 6), "forward": (2, 4, 6)}


def _spread_copies(srcs, lands, send_sems, recv_sems, local_sems, kind, waiting):
    x, y, c = lax.axis_index("x"), lax.axis_index("y"), lax.axis_index("c")
    me = 4 * x + 2 * y + c

    def peer(bits):
        dev = (1 - x if bits & 4 else x, 1 - y if bits & 2 else y, 1 - c if bits & 1 else c)
        return dev, 4 * dev[0] + 2 * dev[1] + dev[2]

    plan = PEERS[kind]
    remote, local = [], []
    for a, l in enumerate(lands):
        for d, bits in enumerate(plan):
            dev, pid = peer(bits)
            if kind == "forward":
                src, dst, dev = l.at[pid], l.at[peer(bits | 1)[1] if waiting else pid], peer(1)[0]
            else:
                src, dst = (srcs[a].at[pid] if kind == "scatter" else srcs[a]), l.at[pid if waiting else me]
            remote.append(pltpu.make_async_remote_copy(
                src_ref=src, dst_ref=dst, send_sem=send_sems.at[a * len(plan) + d], recv_sem=recv_sems.at[a * len(plan) + d],
                device_id=dev, device_id_type=MESH_ID))
        if kind != "forward":
            local.append(pltpu.make_async_copy(srcs[a].at[me] if kind == "scatter" else srcs[a], l.at[me], local_sems.at[a]))
    return remote, local


def _spread_start(srcs, kind, name, after=None, lands=None):
    if kind == "forward":
        srcs = []
    else:
        shapes = [a.shape if kind == "scatter" else (N_DEV,) + a.shape for a in srcs]
        lands = [lax.empty(shp, a.dtype) for shp, a in zip(shapes, srcs)]
    ns, nl, per = len(srcs), len(lands), len(PEERS[kind])
    extra = [] if after is None else [after]
    sem_shapes = [pltpu.SemaphoreType.DMA((nl * per,))] * 2 + ([pltpu.SemaphoreType.DMA((nl,))] if ns else [])

    def body(*refs):
        src_refs, land_refs = refs[:ns], refs[ns:ns + nl]
        sems = refs[ns + nl + len(extra):ns + nl + len(extra) + len(sem_shapes)]
        remote, local = _spread_copies(src_refs, land_refs, sems[0], sems[1], sems[2] if ns else None, kind, False)
        for cp in remote + local:
            cp.start()
        refs[-1][...] = jnp.zeros((8, 128), F32)

    outs = pl.pallas_call(
        body, name=name,
        out_shape=(*sem_shapes, *[pltpu.HBM(a.shape, a.dtype) for a in srcs + lands], jax.ShapeDtypeStruct((8, 128), F32)),
        in_specs=[HBM] * (ns + nl) + [pl.BlockSpec(memory_space=pl.ANY)] * len(extra),
        out_specs=(*[SEM] * len(sem_shapes), *[HBM] * (ns + nl), pl.BlockSpec(memory_space=pltpu.VMEM)),
        input_output_aliases={i: len(sem_shapes) + i for i in range(ns + nl)},
        compiler_params=pltpu.CompilerParams(has_side_effects=EFFECT),
    )(*[pltpu.with_memory_space_constraint(a, pltpu.HBM) for a in srcs + lands], *extra)
    k = len(sem_shapes)
    return outs[:k], list(outs[k:k + ns]), list(outs[k + ns:k + ns + nl]), outs[-1]


def _spread_wait(sems, srcs, lands, after, kind, name):
    ns, nl = len(srcs), len(lands)
    after = list(after) if isinstance(after, (list, tuple)) else [after]

    def body(*refs):
        src_refs, land_refs = refs[:ns], refs[ns:ns + nl]
        s = refs[ns + nl:ns + nl + len(sems)]
        remote, local = _spread_copies(src_refs, land_refs, s[0], s[1], s[2] if ns else None, kind, True)
        for cp in remote:
            cp.wait_send()
            cp.wait_recv()
        for cp in local:
            cp.wait()

    outs = pl.pallas_call(
        body, name=name, out_shape=tuple(pltpu.HBM(a.shape, a.dtype) for a in srcs + lands),
        in_specs=[HBM] * (ns + nl) + [SEM] * len(sems) + [pl.BlockSpec(memory_space=pl.ANY)] * len(after), out_specs=tuple([HBM] * (ns + nl)),
        input_output_aliases={i: i for i in range(ns + nl)}, compiler_params=pltpu.CompilerParams(has_side_effects=EFFECT),
    )(*srcs, *lands, *sems, *after)
    return list(outs[ns:])


SMALL = ("rel_bias", "norm1_g", "sgu_w", "sgu_b", "dil_qn_g", "dil_kn_g", "conv_w", "conv_b", "conv_ln_g", "conv_ln_b",
         "gqa_qn_g", "gqa_kn_g", "mix_norm_g", "norm2_g")
LARGE = ("w_in", "w_out", "w_gate", "w_up", "w_down")
EARLY = tuple(k for k in SMALL if k != "norm1_g")


def _local_step(x, target, p, B, S, fetch, emit, mid, early):
    T = B * S
    rope = _rope_tables(S)
    win = _bias_windows(p["rel_bias"], S)
    tile8 = lambda g: jnp.tile(g.reshape(1, HEAD_DIM), (1, N_HEADS))
    cols_b = (COL_BQ, COL_BK, COL_BV)
    cols_d = (COL_DQ, COL_DK128, COL_DV128)
    saved = []
    for l in range(DEPTH):
        s = {"x": x}
        s["ws"] = p["sgu_w"][l].astype(BF16)
        s["bias"] = jnp.repeat(p["sgu_b"][l].T, HEAD_DIM, axis=1)
        s["h"] = _rms_fwd(x, p["norm1_g"][l], f"rms1_fwd_{l}")
        s["win"] = fetch(l, "in", s["h"])
        s["cw"] = jnp.pad(s["win"]["conv_w"], ((0, 1), (0, 0))).reshape(32, 1, 512)
        z = s["z"] = _matmul(s["h"], s["win"]["w_in"], "nt", f"in_proj_{l}", tk=D_MODEL)
        s["bias"] = s["bias"] + mid(l, z)
        s["ya"] = _sgu_fwd(z, s["ws"], s["bias"], f"sgu_fwd_{l}")
        s["c"] = _conv_fwd1(z, s["cw"], p["conv_b"][l].reshape(1, 512), B, S, f"conv_fwd_{l}")
        s["yc"] = _conv_fwd2(s["c"], p["conv_ln_g"][l].reshape(1, 512), p["conv_ln_b"][l].reshape(1, 512), f"conv_ln_fwd_{l}")
        s["gb"] = (tile8(p["dil_qn_g"][l]), tile8(p["dil_kn_g"][l]))
        s["gd"] = (tile8(p["gqa_qn_g"][l]), tile8(p["gqa_kn_g"][l])[:, :KV_WIDTH])
        s["qkv_b"] = _prep_fwd(z, *s["gb"], None, B, S, N_HEADS, cols_b, f"prep_b_fwd_{l}")
        s["qkv_d"] = _prep_fwd(z, *s["gd"], rope, B, S, KV_HEADS, cols_d, f"prep_d_fwd_{l}")
        s["ob"] = _attn_fwd(*s["qkv_b"], win, f"attn_b_fwd_{l}")
        s["od"] = _attn_fwd(*s["qkv_d"], None, f"attn_d_fwd_{l}", tq=GQA_TQ)
        s["gmix"] = p["mix_norm_g"][l].reshape(1, 2048)
        s["ycat"] = _mix_fwd(s["ya"], s["ob"], s["yc"], s["od"], s["gmix"], B, S, f"mix_fwd_{l}")
        s["wout"] = fetch(l, "out", s["ycat"])["w_out"]
        x1 = s["x1"] = _matmul(s["ycat"], s["wout"], "nn", f"out_proj_{l}", res=x, tk=D_MODEL)
        s["h2"] = _rms_fwd(x1, p["norm2_g"][l], f"rms2_fwd_{l}")
        s["ffn"] = fetch(l, "ffn", s["h2"])
        s["gate"] = _mm_shard_out(s["h2"], s["ffn"]["w_gate"], "nt", f"ffn_gate_{l}", out_dtype=BF16, tm=1024)
        s["up"] = _mm_shard_out(s["h2"], s["ffn"]["w_up"], "nt", f"ffn_up_{l}", out_dtype=BF16, tm=1024)
        x, s["act"] = _ffn_down(s["gate"], s["up"], s["ffn"]["w_down"], x1, f"ffn_down_{l}")
        saved.append(s)

    loss_blk, dx = _loss_grad(x, target, "loss")
    g = {k: [None] * DEPTH for k in SMALL if k != "rel_bias"}
    dwin_total = None
    for l in reversed(range(DEPTH)):
        s = saved[l]
        z, ffn = s["z"], s["ffn"]
        dgate, dup = _ffn_down_dx(dx, ffn["w_down"], s["gate"], s["up"], f"ffn_down_dx_{l}")
        tok = emit(l, "w_down", _mm_shard_m(s["act"], dx, f"ffn_down_dw_{l}", out_dtype=BF16, tn=512, tk=T))
        tok += emit(l, "w_gate", _mm_shard_m(dgate, s["h2"], f"ffn_gate_dw_{l}", out_dtype=BF16, tn=1024, tk=T))
        tok += emit(l, "w_up", _mm_shard_m(dup, s["h2"], f"ffn_up_dw_{l}", out_dtype=BF16, tn=1024, tk=T))
        dh2 = _mm_shard_k([(dgate, ffn["w_gate"]), (dup, ffn["w_up"])], "nn", f"ffn_up_dx_{l}", tn=512, fold=N_DEV)
        dx1, dg2 = _rms_bwd(dh2, s["x1"], p["norm2_g"][l] + tok, dx, f"rms2_bwd_{l}")
        g["norm2_g"][l] = dg2[0]
        dycat = _matmul(dx1, s["wout"], "nt", f"out_proj_dx_{l}", tk=D_MODEL)
        tok = emit(l, "w_out", _matmul(s["ycat"], dx1, "tn", f"out_proj_dw_{l}", out_dtype=BF16, tn=1024, tk=T))
        dya, dob, dyc, dod, dgm = _mix_bwd(s["ya"], s["ob"], s["yc"], s["od"], dycat, s["gmix"] + tok, B, S, f"mix_bwd_{l}")
        g["mix_norm_g"][l] = dgm[0]
        dz_a, dws, dbias = _sgu_bwd(z, dya, s["ws"], jnp.swapaxes(s["ws"], 1, 2), s["bias"], f"sgu_bwd_{l}")
        g["sgu_w"][l] = dws
        g["sgu_b"][l] = dbias.reshape(128, 8, HEAD_DIM).sum(-1).T
        dc, dlg, dlb, dcb = _conv_bwd1(s["c"], dyc, p["conv_ln_g"][l].reshape(1, 512), p["conv_ln_b"][l].reshape(1, 512), f"conv_ln_bwd_{l}")
        g["conv_ln_g"][l], g["conv_ln_b"][l], g["conv_b"][l] = dlg[0], dlb[0], dcb[0]
        dz_ca, dz_cg, dcw = _conv_bwd2(z, dc, s["cw"], B, S, f"conv_bwd_{l}")
        g["conv_w"][l] = dcw.reshape(32, 512)[:CONV_WIDTH]
        dq, dk, dv, dwin = _attn_bwd(*s["qkv_b"], s["ob"], dob, win, f"attn_b_bwd_{l}")
        dwin_total = dwin if dwin_total is None else dwin_total + dwin
        dz_b, dgq, dgk = _prep_bwd(z, dq, dk, dv, *s["gb"], None, B, S, N_HEADS, cols_b, f"prep_b_bwd_{l}")
        g["dil_qn_g"][l] = dgq.reshape(N_HEADS, HEAD_DIM).sum(0)
        g["dil_kn_g"][l] = dgk.reshape(N_HEADS, HEAD_DIM).sum(0)
        dq, dk, dv = _attn_bwd(*s["qkv_d"], s["od"], dod, None, f"attn_d_bwd_{l}", tq=GQA_TQ)
        dz_d, dgq, dgk = _prep_bwd(z, dq, dk, dv, *s["gd"], rope, B, S, KV_HEADS, cols_d, f"prep_d_bwd_{l}")
        g["gqa_qn_g"][l] = dgq.reshape(N_HEADS, HEAD_DIM).sum(0)
        g["gqa_kn_g"][l] = dgk.reshape(KV_HEADS, HEAD_DIM).sum(0)
        dz = jnp.concatenate([dz_a, dz_b, dz_ca, dz_cg, dz_d], axis=1)
        tok = jnp.zeros((), F32)
        if l == 0:
            done = {k: jnp.stack(v) for k, v in g.items() if k != "norm1_g"}
            done["rel_bias"] = _bias_fold(dwin_total, S, "bias_fold")
            tok = early(done)
        tok += emit(l, "w_in", _matmul(dz, s["h"], "tn", f"in_proj_dw_{l}", out_dtype=BF16, tm=256, tn=D_MODEL, tk=T))
        dh = _matmul(dz, s["win"]["w_in"], "nn", f"in_proj_dx_{l}", tn=1024, tk=IN_WIDTH)
        dx, dg1 = _rms_bwd(dh, s["x"], p["norm1_g"][l] + tok, dx1, f"rms1_bwd_{l}")
        g["norm1_g"][l] = dg1[0]

    return loss_blk[0, 0], dx, jnp.stack(g["norm1_g"])


GROUPS = {"in": ("w_in",), "out": ("w_out",), "ffn": ("w_gate", "w_up", "w_down")}
COL_SHARDED = ("w_in", "w_gate", "w_up")


def kernel(x, rel_bias, norm1_g, w_in, sgu_w, sgu_b, dil_qn_g, dil_kn_g, conv_w, conv_b, conv_ln_g, conv_ln_b, gqa_qn_g, gqa_kn_g, mix_norm_g, w_out, norm2_g, w_gate, w_up, w_down, loss_target, m_rel_bias, m_norm1_g, m_w_in, m_sgu_w, m_sgu_b, m_dil_qn_g, m_dil_kn_g, m_conv_w, m_conv_b, m_conv_ln_g, m_conv_ln_b, m_gqa_qn_g, m_gqa_kn_g, m_mix_norm_g, m_w_out, m_norm2_g, m_w_gate, m_w_up, m_w_down, v_rel_bias, v_norm1_g, v_w_in, v_sgu_w, v_sgu_b, v_dil_qn_g, v_dil_kn_g, v_conv_w, v_conv_b, v_conv_ln_g, v_conv_ln_b, v_gqa_qn_g, v_gqa_kn_g, v_mix_norm_g, v_w_out, v_norm2_g, v_w_gate, v_w_up, v_w_down):
    w = dict(rel_bias=rel_bias, norm1_g=norm1_g, w_in=w_in, sgu_w=sgu_w, sgu_b=sgu_b, dil_qn_g=dil_qn_g, dil_kn_g=dil_kn_g, conv_w=conv_w,
             conv_b=conv_b, conv_ln_g=conv_ln_g, conv_ln_b=conv_ln_b, gqa_qn_g=gqa_qn_g, gqa_kn_g=gqa_kn_g, mix_norm_g=mix_norm_g,
             w_out=w_out, norm2_g=norm2_g, w_gate=w_gate, w_up=w_up, w_down=w_down)
    m = dict(rel_bias=m_rel_bias, norm1_g=m_norm1_g, w_in=m_w_in, sgu_w=m_sgu_w, sgu_b=m_sgu_b, dil_qn_g=m_dil_qn_g, dil_kn_g=m_dil_kn_g,
             conv_w=m_conv_w, conv_b=m_conv_b, conv_ln_g=m_conv_ln_g, conv_ln_b=m_conv_ln_b, gqa_qn_g=m_gqa_qn_g, gqa_kn_g=m_gqa_kn_g,
             mix_norm_g=m_mix_norm_g, w_out=m_w_out, norm2_g=m_norm2_g, w_gate=m_w_gate, w_up=m_w_up, w_down=m_w_down)
    v = dict(rel_bias=v_rel_bias, norm1_g=v_norm1_g, w_in=v_w_in, sgu_w=v_sgu_w, sgu_b=v_sgu_b, dil_qn_g=v_dil_qn_g, dil_kn_g=v_dil_kn_g,
             conv_w=v_conv_w, conv_b=v_conv_b, conv_ln_g=v_conv_ln_g, conv_ln_b=v_conv_ln_b, gqa_qn_g=v_gqa_qn_g, gqa_kn_g=v_gqa_kn_g,
             mix_norm_g=v_mix_norm_g, w_out=v_w_out, norm2_g=v_norm2_g, w_gate=v_w_gate, w_up=v_w_up, w_down=v_w_down)
    names = list(w)
    B, S, D = x.shape
    T = B * S
    me = 4 * lax.axis_index("x") + 2 * lax.axis_index("y") + lax.axis_index("c")

    view = lambda a, k: jnp.swapaxes(a, 1, 2) if k in COL_SHARDED else a
    bf = {k: view(w[k], k).astype(BF16) for k in LARGE}
    spreads, forwards = {}, {}

    def start_gather(l, group, after=None):
        srcs = [bf[k][l] for k in GROUPS[group]] + ([conv_w[l]] if group == "in" else [])
        spreads[l, group] = _spread_start(srcs, "chips", f"gather_{group}_{l}_start", after)
        return spreads[l, group][3][0, 0]

    def forward(l, group, after):
        sems, srcs, lands, _ = spreads[l, group]
        lands = _spread_wait(sems, srcs, lands, after, "chips", f"gather_{group}_{l}_wait")
        forwards[l, group] = _spread_start(None, "forward", f"forward_{group}_{l}_start", lands=lands)
        return forwards[l, group][3]

    def landed(l, group, after):
        sems, _, lands, _ = forwards[l, group]
        return _spread_wait(sems, [], lands, after, "forward", f"forward_{group}_{l}_wait")

    tok0 = start_gather(0, "in") + start_gather(0, "out") + start_gather(0, "ffn")
    small = {k: w[k] for k in SMALL}
    small["norm1_g"] = norm1_g.at[0].add(tok0)

    def mid(l, z):
        if l > 0:
            return jnp.zeros((), F32)
        return start_gather(1, "in", z) + start_gather(1, "out", z) + start_gather(1, "ffn", z)

    def fetch(l, group, after):
        if group == "in":
            tok = forward(0, "in", after) if l == 0 else after
        elif group == "out":
            tok = forward(l, "ffn", [after, forward(l, "out", after)])
        else:
            tok = forward(1, "in", after) if l == 0 else after
        got = dict(zip(GROUPS[group] + ("conv_w",), landed(l, group, [after, tok])))
        if group == "in":
            got["w_in"] = got["w_in"].reshape(IN_WIDTH, D)
            got["conv_w"] = jnp.transpose(got["conv_w"], (1, 0, 2)).reshape(CONV_WIDTH, 512)
        if group == "out":
            got["w_out"] = got["w_out"].reshape(D, D)
        return got

    scatters = {}

    def emit(l, k, dw):
        if k in ("w_in", "w_out"):
            dw = dw.reshape(N_DEV, dw.shape[0] // N_DEV, D)
        scatters[l, k] = _spread_start([dw], "scatter", f"scatter_{k}_{l}_start")
        return scatters[l, k][3][0, 0]

    flat2 = lambda a: a.reshape(-1, a.shape[-1])
    small_spread = []

    def early(done):
        small_spread.append(_spread_start([flat2(done[k]) for k in EARLY], "gather", "gather_small_grads_start"))
        return small_spread[0][3][0, 0]

    loss_part, dx, dnorm1 = _local_step(x.reshape(T, D), loss_target.reshape(T, D), small, B, S, fetch, emit, mid, early)
    loss = lax.psum(loss_part, ("x", "y", "c"))

    out_g, out_d, out_m, out_v = {}, {}, {}, {}

    def update_large(k, after):
        shp = view(w[k], k).shape
        two_d = lambda a: view(a, k).reshape(-1, shp[-1])
        res = None
        for l in reversed(range(DEPTH)):
            sems, srcs, lands, _ = scatters[l, k]
            stack = _spread_wait(sems, srcs, lands, after, "scatter", f"scatter_{k}_{l}_wait")[0]
            res = _adamw(two_d(w[k]), two_d(m[k]), two_d(v[k]), stack.reshape(N_DEV, -1, shp[-1]), f"adamw_{k}_{l}", layer=l, prev=res)
        out_g[k], out_d[k], out_m[k], out_v[k] = [view(a.reshape(shp), k) for a in res]
        return res[0]

    late_sems, late_srcs, late_lands, late_tok = _spread_start([flat2(dnorm1)], "gather", "gather_norm1_grad_start")
    after = [dx, late_tok]
    for k in ("w_down", "w_gate", "w_up", "w_out"):
        after = update_large(k, after)
    sems, srcs, lands, _ = small_spread[0]
    stacks = dict(zip(EARLY, _spread_wait(sems, srcs, lands, after, "gather", "gather_small_grads_wait")))
    stacks["norm1_g"] = _spread_wait(late_sems, late_srcs, late_lands, after, "gather", "gather_norm1_grad_wait")[0]
    for k in SMALL:
        stack = stacks[k]
        if k == "conv_w":
            stack = lax.dynamic_slice_in_dim(stack, me * (512 // N_DEV), 512 // N_DEV, axis=2)
        res = _adamw(flat2(w[k]), flat2(m[k]), flat2(v[k]), stack, f"adamw_{k}")
        out_g[k], out_d[k], out_m[k], out_v[k] = [a.reshape(w[k].shape) for a in res]
        after = res[0]
    update_large("w_in", after)

    return (loss, dx.reshape(B, S, D), *[out_g[k] for k in names], *[out_d[k] for k in names],
            *[out_m[k] for k in names], *[out_v[k] for k in names])
```

```python
import functools
import math

import numpy as np
import jax
import jax.numpy as jnp
from jax import lax
from jax.experimental import pallas as pl
from jax.experimental.pallas import tpu as pltpu

F32 = jnp.float32
BF16 = jnp.bfloat16
HIGHEST = lax.Precision.HIGHEST
MESH_ID = pl.DeviceIdType.MESH

D_MODEL = 2048
DEPTH = 2
HEAD_DIM = 64
GROUP_WIDTH = 512
N_HEADS = 8
KV_HEADS = 2
KV_WIDTH = 128
SGU_CHUNK = 128
CONV_WIDTH = 31
CONV_PAD = 16
GRID_W = 64
ROPE_THETA = 10000.0
REL_BUCKETS = 32
REL_MAX_DIST = 1024
DIL_PATTERNS = ((128, 1), (512, 4), (2048, 16))
FFN_HIDDEN = 5632
IN_WIDTH = 4352
RMS_EPS = 1e-6
LN_EPS = 1e-5
MASKED = -1e30
N_DEV = 8

ADAM_LR = 0.001
ADAM_B1 = 0.9
ADAM_B2 = 0.999
ADAM_EPS = 1e-08
ADAM_WD = 0.01
ADAM_STEP = 10

COL_AU, COL_AV, COL_BQ, COL_BK, COL_BV, COL_CA, COL_CG, COL_DQ = range(8)
COL_DK128, COL_DV128 = 32, 33

VMEM_LIMIT = 56 * 1024 * 1024
ATTN_TQ = 256
GQA_TQ = 512


def _params(sem=None, vmem=VMEM_LIMIT):
    return pltpu.CompilerParams(dimension_semantics=sem, vmem_limit_bytes=vmem)


def _dot(a, b, dims, precision=None):
    return lax.dot_general(a, b, (dims, ((), ())), precision=precision, preferred_element_type=F32)


def _nn(a, b, precision=None):
    return _dot(a, b, ((1,), (0,)), precision)


def _nt(a, b):
    return _dot(a, b, ((1,), (1,)))


def _tn(a, b):
    return _dot(a, b, ((0,), (0,)))


DIMS = {"nn": ((1,), (0,)), "nt": ((1,), (1,)), "tn": ((0,), (0,))}


def _pick(n, cands):
    for c in cands:
        if n % c == 0:
            return c
    return n


def _mm_call(name, mode, pairs, specs, o_spec, out_sds, grid, acc_shape, res=None, fold=None, norm=None):
    npair, nk, dims = len(pairs), grid[2], DIMS[mode]

    def body(*refs):
        ab = refs[:2 * npair]
        at = 2 * npair
        r_ref = refs[at] if res is not None else None
        at += res is not None
        g_ref = refs[at] if norm is not None else None
        at += norm is not None
        o_ref = refs[at]
        h_ref = refs[at + 1] if norm is not None else None
        part = None
        for t in range(npair):
            for s in ([None] if fold is None else range(fold)):
                a_blk = ab[2 * t][...] if s is None else ab[2 * t][s]
                b_blk = ab[2 * t + 1][...] if s is None else ab[2 * t + 1][s]
                d = _dot(a_blk.astype(BF16), b_blk.astype(BF16), dims)
                part = d if part is None else part + d

        def finish(r):
            if r_ref is not None:
                r = r + r_ref[...]
            o_ref[...] = r.astype(o_ref.dtype)
            if h_ref is not None:
                h_ref[...] = _rms_rows(r, g_ref[...])

        if nk == 1:
            finish(part)
            return
        acc, k = refs[-1], pl.program_id(2)

        @pl.when(k == 0)
        def _():
            acc[...] = part

        @pl.when(k > 0)
        def _():
            acc[...] += part

        @pl.when(k == nk - 1)
        def _():
            finish(acc[...])

    ins = [t for pair in pairs for t in pair]
    in_specs = [t for pair in specs for t in pair]
    if res is not None:
        ins.append(res)
        in_specs.append(o_spec)
    out_specs = o_spec
    if norm is not None:
        ins.append(norm)
        in_specs.append(pl.BlockSpec(norm.shape, lambda *_: (0, 0)))
        out_specs, out_sds = [o_spec, o_spec], [out_sds, jax.ShapeDtypeStruct(out_sds.shape, BF16)]
    return pl.pallas_call(
        body, name=name, grid=grid, in_specs=in_specs, out_specs=out_specs, out_shape=out_sds,
        scratch_shapes=[pltpu.VMEM(acc_shape, F32)] if nk > 1 else [],
        compiler_params=_params(("parallel", "parallel", "arbitrary")),
    )(*ins)


def _rms_rows(x, g):
    return (x * lax.rsqrt(jnp.mean(x * x, axis=-1, keepdims=True) + RMS_EPS) * g).astype(BF16)


def _matmul(a, b, mode, name, res=None, out_dtype=F32, tm=512, tn=None, tk=None, norm=None):
    if mode == "nn":
        (M, K), N = a.shape, b.shape[1]
    elif mode == "nt":
        (M, K), N = a.shape, b.shape[0]
    else:
        (K, M), N = a.shape, b.shape[1]
    tm = min(tm, M)
    tn = tn or _pick(N, (2176, 2048, 1408, 1024, 512))
    tk = tk or _pick(K, (1024, 2176, 1408, 512))
    assert M % tm == 0 and N % tn == 0 and K % tk == 0, (M, N, K, tm, tn, tk)
    a_spec = pl.BlockSpec((tk, tm), lambda i, j, k: (k, i)) if mode == "tn" else pl.BlockSpec((tm, tk), lambda i, j, k: (i, k))
    b_spec = pl.BlockSpec((tn, tk), lambda i, j, k: (j, k)) if mode == "nt" else pl.BlockSpec((tk, tn), lambda i, j, k: (k, j))
    o_spec = pl.BlockSpec((tm, tn), lambda i, j, k: (i, j))
    assert norm is None or tn == N
    return _mm_call(name, mode, [(a, b)], [(a_spec, b_spec)], o_spec, jax.ShapeDtypeStruct((M, N), out_dtype),
                    (M // tm, N // tn, K // tk), (tm, tn), res, norm=None if norm is None else norm.reshape(1, N))


def _mm_shard_out(a, bs, mode, name, out_dtype=F32, tm=512, tk=None):
    J = bs.shape[0]
    n = bs.shape[1] if mode == "nt" else bs.shape[2]
    (K, M) = a.shape if mode == "tn" else a.shape[::-1]
    tm = min(tm, M)
    tk = tk or (K if mode != "tn" else _pick(K, (1024, 512)))
    a_spec = pl.BlockSpec((tk, tm), lambda j, i, k: (k, i)) if mode == "tn" else pl.BlockSpec((tm, tk), lambda j, i, k: (i, k))
    b_spec = pl.BlockSpec((None, n, tk), lambda j, i, k: (j, 0, k)) if mode == "nt" else pl.BlockSpec((None, tk, n), lambda j, i, k: (j, k, 0))
    o_spec = pl.BlockSpec((None, tm, n), lambda j, i, k: (j, i, 0))
    return _mm_call(name, mode, [(a, bs)], [(a_spec, b_spec)], o_spec, jax.ShapeDtypeStruct((J, M, n), out_dtype),
                    (J, M // tm, K // tk), (tm, n))


def _mm_shard_k(pairs, mode, name, res=None, out_dtype=F32, tm=512, tn=None, fold=1):
    J, M, n = pairs[0][0].shape
    N = pairs[0][1].shape[2] if mode == "nn" else pairs[0][1].shape[1]
    tm = min(tm, M)
    tn = tn or _pick(N, (2048, 1024, 512))
    a_spec = pl.BlockSpec((fold, tm, n), lambda i, j, k: (k, i, 0))
    b_spec = pl.BlockSpec((fold, n, tn), lambda i, j, k: (k, 0, j)) if mode == "nn" else pl.BlockSpec((fold, tn, n), lambda i, j, k: (k, j, 0))
    o_spec = pl.BlockSpec((tm, tn), lambda i, j, k: (i, j))
    return _mm_call(name, mode, pairs, [(a_spec, b_spec)] * len(pairs), o_spec, jax.ShapeDtypeStruct((M, N), out_dtype),
                    (M // tm, N // tn, J // fold), (tm, tn), res, fold)


def _mm_shard_m(as_, b, name, out_dtype=F32, tn=None, tk=512):
    J, K, n = as_.shape
    N = b.shape[1]
    tn = tn or _pick(N, (2048, 1024, 512))
    tk = min(tk, K)
    a_spec = pl.BlockSpec((None, tk, n), lambda j, i, k: (j, k, 0))
    b_spec = pl.BlockSpec((tk, tn), lambda j, i, k: (k, i))
    o_spec = pl.BlockSpec((None, n, tn), lambda j, i, k: (j, 0, i))
    return _mm_call(name, "tn", [(as_, b)], [(a_spec, b_spec)], o_spec, jax.ShapeDtypeStruct((J, n, N), out_dtype),
                    (J, N // tn, K // tk), (n, tn))


def _in_proj_dw(pieces, h, name):
    T, D = h.shape
    tm = 256
    nbs = [p.shape[1] // tm for p in pieces]
    los = [sum(nbs[:t]) for t in range(len(pieces))]

    def body(*refs):
        h_ref, o_ref = refs[-2:]
        i = pl.program_id(0)
        for p_ref, lo, nb in zip(refs[:-2], los, nbs):
            @pl.when((i >= lo) & (i < lo + nb))
            def _():
                o_ref[...] = _tn(p_ref[...], h_ref[...]).astype(BF16)

    specs = [pl.BlockSpec((T, tm), (lambda lo, nb: lambda i: (0, jnp.clip(i - lo, 0, nb - 1)))(lo, nb)) for lo, nb in zip(los, nbs)]
    return pl.pallas_call(body, name=name, grid=(sum(nbs),), in_specs=specs + [pl.BlockSpec((T, D), lambda i: (0, 0))],
                          out_specs=pl.BlockSpec((tm, D), lambda i: (i, 0)), out_shape=jax.ShapeDtypeStruct((sum(nbs) * tm, D), BF16),
                          compiler_params=_params(("parallel",)))(*pieces, h)


def _in_proj_dx(pieces, w, name, tm=512, tn=1024):
    T = pieces[0].shape[0]
    K, D = w.shape
    tm = min(tm, T)
    widths = [p.shape[1] for p in pieces]
    offs = [sum(widths[:t]) for t in range(len(pieces))]

    def body(*refs):
        w_ref, o_ref = refs[-2:]
        acc = None
        for p_ref, off, wd in zip(refs[:-2], offs, widths):
            d = _nn(p_ref[...], w_ref[off:off + wd, :])
            acc = d if acc is None else acc + d
        o_ref[...] = acc

    specs = [pl.BlockSpec((tm, wd), lambda i, j: (i, 0)) for wd in widths]
    return pl.pallas_call(body, name=name, grid=(T // tm, D // tn), in_specs=specs + [pl.BlockSpec((K, tn), lambda i, j: (0, j))],
                          out_specs=pl.BlockSpec((tm, tn), lambda i, j: (i, j)), out_shape=jax.ShapeDtypeStruct((T, D), F32),
                          compiler_params=_params(("parallel", "parallel")))(*pieces, w)


def _seg_matrix(width):
    return jnp.asarray(np.kron(np.eye(width // HEAD_DIM, dtype=np.float32), np.full((HEAD_DIM, HEAD_DIM), 1.0 / HEAD_DIM, np.float32)), BF16)


def _segmean(v, p):
    hi = v.astype(BF16)
    r = v - hi.astype(F32)
    mid = r.astype(BF16)
    lo = (r - mid.astype(F32)).astype(BF16)
    w = min(256, v.shape[1])
    pw = p[:w, :w]
    halves = []
    for c in range(v.shape[1] // w):
        cols = slice(c * w, (c + 1) * w)
        halves.append(_nn(hi[:, cols], pw) + _nn(mid[:, cols], pw) + _nn(lo[:, cols], pw))
    return halves[0] if len(halves) == 1 else jnp.concatenate(halves, axis=1)


def _gelu(x):
    c0 = math.sqrt(2.0 / math.pi)
    t = jnp.tanh(c0 * (x + 0.044715 * x * x * x))
    return 0.5 * x * (1.0 + t), t


def _gelu_grad(x, t):
    c0 = math.sqrt(2.0 / math.pi)
    return 0.5 * (1.0 + t) + 0.5 * x * (1.0 - t * t) * c0 * (1.0 + 3.0 * 0.044715 * x * x)


def _sigmoid(x):
    return 1.0 / (1.0 + jnp.exp(-x))


def _rms_fwd(x, g, name):
    T, D = x.shape
    tm = min(256, T)

    def body(x_ref, g_ref, o_ref):
        o_ref[...] = _rms_rows(x_ref[...], g_ref[...])

    return pl.pallas_call(
        body, name=name, grid=(T // tm,),
        in_specs=[pl.BlockSpec((tm, D), lambda i: (i, 0)), pl.BlockSpec((1, D), lambda i: (0, 0))],
        out_specs=pl.BlockSpec((tm, D), lambda i: (i, 0)), out_shape=jax.ShapeDtypeStruct((T, D), BF16),
        compiler_params=_params(("parallel",)),
    )(x, g.reshape(1, D))


def _rms_bwd(dh, x, g, dres, name):
    T, D = x.shape
    tm = min(256, T)

    def body(dh_ref, x_ref, g_ref, dres_ref, dx_ref, dg_ref):
        @pl.when(pl.program_id(0) == 0)
        def _():
            dg_ref[...] = jnp.zeros_like(dg_ref)

        xv, dhv = x_ref[...], dh_ref[...]
        r = lax.rsqrt(jnp.mean(xv * xv, axis=-1, keepdims=True) + RMS_EPS)
        y = xv * r
        dy = dhv * g_ref[...]
        dx_ref[...] = dres_ref[...] + r * (dy - y * jnp.mean(dy * y, axis=-1, keepdims=True))
        dg_ref[...] += jnp.sum(dhv * y, axis=0, keepdims=True)

    row = pl.BlockSpec((tm, D), lambda i: (i, 0))
    vec = pl.BlockSpec((1, D), lambda i: (0, 0))
    return pl.pallas_call(
        body, name=name, grid=(T // tm,), in_specs=[row, row, vec, row], out_specs=[row, vec],
        out_shape=[jax.ShapeDtypeStruct((T, D), F32), jax.ShapeDtypeStruct((1, D), F32)],
        compiler_params=_params(("arbitrary",)),
    )(dh, x, g.reshape(1, D), dres)


def _sgu_core(zu, zv, ws_ref, bias, p):
    ug, tu = _gelu(zu)
    vg, tv = _gelu(zv)
    xc = vg - _segmean(vg, p)
    rs = lax.rsqrt(_segmean(xc * xc, p) + LN_EPS)
    vn = xc * rs
    vnb = vn.astype(BF16)
    low = lax.broadcasted_iota(jnp.int32, (SGU_CHUNK, 128), 1) < HEAD_DIM
    parts = []
    for j in range(4):
        vp = vnb[:, 128 * j:128 * (j + 1)]
        parts.append(jnp.where(low, _nn(ws_ref[2 * j], vp), _nn(ws_ref[2 * j + 1], vp)))
    mixed = jnp.concatenate(parts, axis=1) + bias
    return ug, tu, tv, rs, vn, vnb, mixed, low


def _sgu_fwd(z, ws, bias, name):
    T = z.shape[0]

    def body(zu_ref, zv_ref, ws_ref, b_ref, p_ref, y_ref):
        ug, _, _, _, _, _, mixed, _ = _sgu_core(zu_ref[...], zv_ref[...], ws_ref, b_ref[...], p_ref[...])
        y_ref[...] = ug * mixed

    full = lambda shape: pl.BlockSpec(shape, lambda i: (0,) * len(shape))
    return pl.pallas_call(
        body, name=name, grid=(T // SGU_CHUNK,),
        in_specs=[pl.BlockSpec((SGU_CHUNK, 512), lambda i: (i, COL_AU)), pl.BlockSpec((SGU_CHUNK, 512), lambda i: (i, COL_AV)),
                  full((8, 128, 128)), full((128, 512)), full((512, 512))],
        out_specs=pl.BlockSpec((SGU_CHUNK, 512), lambda i: (i, 0)), out_shape=jax.ShapeDtypeStruct((T, 512), F32),
        compiler_params=_params(("parallel",)),
    )(z, z, ws, bias, _seg_matrix(512))


def _sgu_bwd(z, dy, ws, ws_t, bias, name):
    T = z.shape[0]

    def body(zu_ref, zv_ref, dy_ref, ws_ref, wst_ref, b_ref, p_ref, dz_ref, dws_ref, db_ref):
        @pl.when(pl.program_id(0) == 0)
        def _():
            dws_ref[...] = jnp.zeros_like(dws_ref)
            db_ref[...] = jnp.zeros_like(db_ref)

        zu, zv, p = zu_ref[...], zv_ref[...], p_ref[...]
        ug, tu, tv, rs, vn, vnb, mixed, low = _sgu_core(zu, zv, ws_ref, b_ref[...], p)
        dyv = dy_ref[...]
        dmixed = dyv * ug
        db_ref[...] += dmixed
        dmb = dmixed.astype(BF16)
        zero = jnp.zeros((SGU_CHUNK, 128), BF16)
        parts = []
        for j in range(4):
            dmp, vp = dmb[:, 128 * j:128 * (j + 1)], vnb[:, 128 * j:128 * (j + 1)]
            dws_ref[2 * j] += _nt(jnp.where(low, dmp, zero), vp)
            dws_ref[2 * j + 1] += _nt(jnp.where(low, zero, dmp), vp)
            parts.append(jnp.where(low, _nn(wst_ref[2 * j], dmp), _nn(wst_ref[2 * j + 1], dmp)))
        dvn = jnp.concatenate(parts, axis=1)
        dvg = rs * (dvn - _segmean(dvn, p) - vn * _segmean(dvn * vn, p))
        dz_ref[:, 0:512] = (dyv * mixed * _gelu_grad(zu, tu)).astype(BF16)
        dz_ref[:, 512:1024] = (dvg * _gelu_grad(zv, tv)).astype(BF16)

    full = lambda shape: pl.BlockSpec(shape, lambda i: (0,) * len(shape))
    return pl.pallas_call(
        body, name=name, grid=(T // SGU_CHUNK,),
        in_specs=[pl.BlockSpec((SGU_CHUNK, 512), lambda i: (i, COL_AU)), pl.BlockSpec((SGU_CHUNK, 512), lambda i: (i, COL_AV)),
                  pl.BlockSpec((SGU_CHUNK, 512), lambda i: (i, 0)), full((8, 128, 128)), full((8, 128, 128)), full((128, 512)), full((512, 512))],
        out_specs=[pl.BlockSpec((SGU_CHUNK, 1024), lambda i: (i, 0)), full((8, 128, 128)), full((128, 512))],
        out_shape=[jax.ShapeDtypeStruct((T, 1024), BF16), jax.ShapeDtypeStruct((8, 128, 128), F32), jax.ShapeDtypeStruct((128, 512), F32)],
        compiler_params=_params(("arbitrary",)),
    )(z, z, dy, ws, ws_t, bias, _seg_matrix(512))


CONV_ROWS = 256


def _conv_taps(pad_ref, w_ref, base, flip):
    acc = None
    for k in range(CONV_WIDTH):
        wk = w_ref[CONV_WIDTH - 1 - k if flip else k]
        t = wk * pad_ref[base + k + 1:base + k + 1 + CONV_ROWS, :]
        acc = t if acc is None else acc + t
    return acc


def _conv_fwd1(z, w, cb, B, S, name):
    T = B * S
    rows = min(CONV_ROWS, S)
    assert rows == CONV_ROWS

    def body(a_ref, g_ref, w_ref, cb_ref, c_ref, pad):
        pad[0:CONV_PAD, :] = jnp.zeros((CONV_PAD, 128), F32)
        pad[CONV_PAD + S:2 * CONV_PAD + S, :] = jnp.zeros((CONV_PAD, 128), F32)
        pad[CONV_PAD:CONV_PAD + S, :] = a_ref[...] * _sigmoid(g_ref[...])

        for base in range(0, S, CONV_ROWS):
            c_ref[base:base + CONV_ROWS, :] = _conv_taps(pad, w_ref, base, False) + cb_ref[...]

    return pl.pallas_call(
        body, name=name, grid=(4, B),
        in_specs=[pl.BlockSpec((S, 128), lambda j, b: (b, 4 * COL_CA + j)), pl.BlockSpec((S, 128), lambda j, b: (b, 4 * COL_CG + j)),
                  pl.BlockSpec((32, 1, 128), lambda j, b: (0, 0, j)), pl.BlockSpec((1, 128), lambda j, b: (0, j))],
        out_specs=pl.BlockSpec((S, 128), lambda j, b: (b, j)), out_shape=jax.ShapeDtypeStruct((T, 512), F32),
        scratch_shapes=[pltpu.VMEM((S + 2 * CONV_PAD, 128), F32)], compiler_params=_params(("parallel", "parallel")),
    )(z, z, w, cb)


def _ln_rows(c):
    mu = jnp.mean(c, axis=-1, keepdims=True)
    xc = c - mu
    rs = lax.rsqrt(jnp.mean(xc * xc, axis=-1, keepdims=True) + LN_EPS)
    return xc * rs, rs


def _conv_fwd2(c, lng, lnb, name):
    T = c.shape[0]
    tm = min(512, T)

    def body(c_ref, g_ref, b_ref, y_ref):
        n, _ = _ln_rows(c_ref[...])
        t = n * g_ref[...] + b_ref[...]
        y_ref[...] = t * _sigmoid(t)

    row = pl.BlockSpec((tm, 512), lambda i: (i, 0))
    vec = pl.BlockSpec((1, 512), lambda i: (0, 0))
    return pl.pallas_call(body, name=name, grid=(T // tm,), in_specs=[row, vec, vec], out_specs=row,
                          out_shape=jax.ShapeDtypeStruct((T, 512), F32), compiler_params=_params(("parallel",)))(c, lng, lnb)


def _conv_bwd1(c, dy, lng, lnb, name):
    T = c.shape[0]
    tm = min(512, T)

    def body(c_ref, dy_ref, g_ref, b_ref, dc_ref, dg_ref, db_ref, dcb_ref):
        @pl.when(pl.program_id(0) == 0)
        def _():
            dg_ref[...] = jnp.zeros_like(dg_ref)
            db_ref[...] = jnp.zeros_like(db_ref)
            dcb_ref[...] = jnp.zeros_like(dcb_ref)

        n, rs = _ln_rows(c_ref[...])
        t = n * g_ref[...] + b_ref[...]
        s = _sigmoid(t)
        dt = dy_ref[...] * s * (1.0 + t * (1.0 - s))
        dg_ref[...] += jnp.sum(dt * n, axis=0, keepdims=True)
        db_ref[...] += jnp.sum(dt, axis=0, keepdims=True)
        dn = dt * g_ref[...]
        dc = rs * (dn - jnp.mean(dn, axis=-1, keepdims=True) - n * jnp.mean(dn * n, axis=-1, keepdims=True))
        dc_ref[...] = dc
        dcb_ref[...] += jnp.sum(dc, axis=0, keepdims=True)

    row = pl.BlockSpec((tm, 512), lambda i: (i, 0))
    vec = pl.BlockSpec((1, 512), lambda i: (0, 0))
    vshape = jax.ShapeDtypeStruct((1, 512), F32)
    return pl.pallas_call(body, name=name, grid=(T // tm,), in_specs=[row, row, vec, vec], out_specs=[row, vec, vec, vec],
                          out_shape=[jax.ShapeDtypeStruct((T, 512), F32), vshape, vshape, vshape],
                          compiler_params=_params(("arbitrary",)))(c, dy, lng, lnb)


def _conv_bwd2(z, dc, w, B, S, name):
    T = B * S

    def body(a_ref, g_ref, dc_ref, w_ref, da_ref, dg_ref, dw_ref, hpad, dpad):
        @pl.when(pl.program_id(1) == 0)
        def _():
            dw_ref[...] = jnp.zeros_like(dw_ref)

        zeros = jnp.zeros((CONV_PAD, 128), F32)
        for ref in (hpad, dpad):
            ref[0:CONV_PAD, :] = zeros
            ref[CONV_PAD + S:2 * CONV_PAD + S, :] = zeros
        hpad[CONV_PAD:CONV_PAD + S, :] = a_ref[...] * _sigmoid(g_ref[...])
        dpad[CONV_PAD:CONV_PAD + S, :] = dc_ref[...]
        dws = [None] * CONV_WIDTH
        for base in range(0, S, CONV_ROWS):
            rows = slice(base, base + CONV_ROWS)
            dh = _conv_taps(dpad, w_ref, base, True)
            sg = _sigmoid(g_ref[rows, :])
            da_ref[rows, :] = (dh * sg).astype(BF16)
            dg_ref[rows, :] = (dh * a_ref[rows, :] * sg * (1.0 - sg)).astype(BF16)
            dcv = dc_ref[rows, :]
            for k in range(CONV_WIDTH):
                prod = dcv * hpad[base + k + 1:base + k + 1 + CONV_ROWS, :]
                part = jnp.sum(prod.reshape(CONV_ROWS // 8, 8, 128), axis=0)
                dws[k] = part if dws[k] is None else dws[k] + part
        for k in range(CONV_WIDTH):
            dw_ref[k] += jnp.sum(dws[k], axis=0, keepdims=True)

    return pl.pallas_call(
        body, name=name, grid=(4, B),
        in_specs=[pl.BlockSpec((S, 128), lambda j, b: (b, 4 * COL_CA + j)), pl.BlockSpec((S, 128), lambda j, b: (b, 4 * COL_CG + j)),
                  pl.BlockSpec((S, 128), lambda j, b: (b, j)), pl.BlockSpec((32, 1, 128), lambda j, b: (0, 0, j))],
        out_specs=[pl.BlockSpec((S, 128), lambda j, b: (b, j)), pl.BlockSpec((S, 128), lambda j, b: (b, j)),
                   pl.BlockSpec((32, 1, 128), lambda j, b: (0, 0, j))],
        out_shape=[jax.ShapeDtypeStruct((T, 512), BF16), jax.ShapeDtypeStruct((T, 512), BF16), jax.ShapeDtypeStruct((32, 1, 512), F32)],
        scratch_shapes=[pltpu.VMEM((S + 2 * CONV_PAD, 128), F32), pltpu.VMEM((S + 2 * CONV_PAD, 128), F32)],
        compiler_params=_params(("parallel", "arbitrary")),
    )(z, z, dc, w)


def _swap16(x):
    n = x.shape[1]
    first = (lax.broadcasted_iota(jnp.int32, x.shape, 1) % 32) < 16
    return jnp.where(first, pltpu.roll(x, n - 16, 1), pltpu.roll(x, 16, 1))


def _rope(x, cos, sin):
    return x * cos + _swap16(x) * sin


def _rope_t(dy, cos, sin):
    return dy * cos + _swap16(dy * sin)


def _qk_norm(x, p):
    r = lax.rsqrt(_segmean(x * x, p) + RMS_EPS)
    return x * r, r


def _store_heads(ref, val, n):
    for h in range(n):
        ref[h] = val[:, HEAD_DIM * h:HEAD_DIM * (h + 1)].astype(ref.dtype)


def _load_heads(ref, n):
    return jnp.concatenate([ref[h] for h in range(n)], axis=1)


def _prep_fwd(z, gq, gk, rope, B, S, kv_heads, cols, name):
    tm = min(256, S)
    ns = S // tm
    kw = kv_heads * HEAD_DIM
    scale = HEAD_DIM ** -0.5
    qc, kc, vc = cols

    def body(*refs):
        if rope is None:
            q_ref, k_ref, v_ref, gq_ref, gk_ref, p_ref, qo, ko, vo = refs
        else:
            q_ref, k_ref, v_ref, gq_ref, gk_ref, p_ref, cos_ref, sin_ref, qo, ko, vo = refs
        p = p_ref[...]
        qn, _ = _qk_norm(q_ref[...], p)
        kn, _ = _qk_norm(k_ref[...], p[:kw, :kw])
        qn, kn = qn * gq_ref[...], kn * gk_ref[...]
        if rope is not None:
            cos, sin = cos_ref[...], sin_ref[...]
            qn, kn = _rope(qn, cos, sin), _rope(kn, cos[:, :kw], sin[:, :kw])
        _store_heads(qo, qn * scale, N_HEADS)
        _store_heads(ko, kn, kv_heads)
        _store_heads(vo, v_ref[...], kv_heads)

    row = lambda w, c: pl.BlockSpec((tm, w), lambda b, i: (b * ns + i, c))
    const = lambda shape: pl.BlockSpec(shape, lambda b, i: (0,) * len(shape))
    heads = lambda n: pl.BlockSpec((None, n, tm, HEAD_DIM), lambda b, i: (b, 0, i, 0))
    ins = [z, z, z, gq, gk, _seg_matrix(512)]
    specs = [row(512, qc), row(kw, kc), row(kw, vc), const((1, 512)), const((1, kw)), const((512, 512))]
    if rope is not None:
        ins += list(rope)
        specs += [pl.BlockSpec((tm, 512), lambda b, i: (i, 0))] * 2
    return pl.pallas_call(
        body, name=name, grid=(B, ns), in_specs=specs, out_specs=[heads(N_HEADS), heads(kv_heads), heads(kv_heads)],
        out_shape=[jax.ShapeDtypeStruct((B, N_HEADS, S, HEAD_DIM), BF16), jax.ShapeDtypeStruct((B, kv_heads, S, HEAD_DIM), BF16),
                   jax.ShapeDtypeStruct((B, kv_heads, S, HEAD_DIM), BF16)],
        compiler_params=_params(("parallel", "parallel")),
    )(*ins)


def _prep_bwd(z, dq, dk, dv, gq, gk, rope, B, S, kv_heads, cols, name):
    T = B * S
    tm = min(256, S)
    ns = S // tm
    kw = kv_heads * HEAD_DIM
    scale = HEAD_DIM ** -0.5
    qc, kc, _ = cols

    def body(*refs):
        if rope is None:
            q_ref, k_ref, dq_ref, dk_ref, dv_ref, gq_ref, gk_ref, p_ref, dz_ref, dgq_ref, dgk_ref = refs
        else:
            q_ref, k_ref, dq_ref, dk_ref, dv_ref, gq_ref, gk_ref, p_ref, cos_ref, sin_ref, dz_ref, dgq_ref, dgk_ref = refs

        @pl.when((pl.program_id(0) == 0) & (pl.program_id(1) == 0))
        def _():
            dgq_ref[...] = jnp.zeros_like(dgq_ref)
            dgk_ref[...] = jnp.zeros_like(dgk_ref)

        p = p_ref[...]
        dqv = _load_heads(dq_ref, N_HEADS) * scale
        dkv = _load_heads(dk_ref, kv_heads)
        if rope is not None:
            cos, sin = cos_ref[...], sin_ref[...]
            dqv, dkv = _rope_t(dqv, cos, sin), _rope_t(dkv, cos[:, :kw], sin[:, :kw])

        def through_norm(xv, dy, g, pm, dg_ref):
            xh, r = _qk_norm(xv, pm)
            dg_ref[...] += jnp.sum(dy * xh, axis=0, keepdims=True)
            dxh = dy * g
            return r * (dxh - xh * _segmean(dxh * xh, pm))

        dz_ref[:, 0:512] = through_norm(q_ref[...], dqv, gq_ref[...], p, dgq_ref).astype(BF16)
        dz_ref[:, 512:512 + kw] = through_norm(k_ref[...], dkv, gk_ref[...], p[:kw, :kw], dgk_ref).astype(BF16)
        dz_ref[:, 512 + kw:512 + 2 * kw] = _load_heads(dv_ref, kv_heads).astype(BF16)

    row = lambda w, c: pl.BlockSpec((tm, w), lambda b, i: (b * ns + i, c))
    const = lambda shape: pl.BlockSpec(shape, lambda b, i: (0,) * len(shape))
    heads = lambda n: pl.BlockSpec((None, n, tm, HEAD_DIM), lambda b, i: (b, 0, i, 0))
    ins = [z, z, dq, dk, dv, gq, gk, _seg_matrix(512)]
    specs = [row(512, qc), row(kw, kc), heads(N_HEADS), heads(kv_heads), heads(kv_heads), const((1, 512)), const((1, kw)), const((512, 512))]
    if rope is not None:
        ins += list(rope)
        specs += [pl.BlockSpec((tm, 512), lambda b, i: (i, 0))] * 2
    return pl.pallas_call(
        body, name=name, grid=(B, ns), in_specs=specs, out_specs=[row(512 + 2 * kw, 0), const((1, 512)), const((1, kw))],
        out_shape=[jax.ShapeDtypeStruct((T, 512 + 2 * kw), BF16), jax.ShapeDtypeStruct((1, 512), F32), jax.ShapeDtypeStruct((1, kw), F32)],
        compiler_params=_params(("arbitrary", "arbitrary")),
    )(*ins)


def _toeplitz(win, tq, S):
    r = pltpu.roll(jnp.broadcast_to(win, (tq, S + tq)), 0, 1, stride=1, stride_axis=0)
    return r[:, tq:tq + S]


ATTN_HEADS = 4


def _attn_fwd(q, k, v, win, name, nh=ATTN_HEADS, tq=ATTN_TQ):
    B, H, S, _ = q.shape
    shared = k.shape[1] != H
    assert not shared or H // k.shape[1] == nh
    tq = min(tq, S)

    def body(*refs):
        if win is None:
            q_ref, k_ref, v_ref, o_ref = refs
        else:
            q_ref, k_ref, v_ref, w_ref, o_ref = refs
        kvs = [(k_ref[...], v_ref[...])] * nh if shared else [(k_ref[h], v_ref[h]) for h in range(nh)]
        scores = []
        for h in range(nh):
            s = _nt(q_ref[h], kvs[h][0])
            if win is not None:
                s = s + _toeplitz(w_ref[h], tq, S)
            scores.append(s)
        probs = []
        for s in scores:
            p = jnp.exp(s - jnp.max(s, axis=-1, keepdims=True))
            probs.append((p.astype(BF16), jnp.sum(p, axis=-1, keepdims=True)))
        for h, (p, l) in enumerate(probs):
            o_ref[h] = _nn(p, kvs[h][1]) / l

    qs = pl.BlockSpec((None, nh, tq, HEAD_DIM), lambda b, h, i: (b, h, i, 0))
    ks = (pl.BlockSpec((None, None, S, HEAD_DIM), lambda b, h, i: (b, h, 0, 0)) if shared
          else pl.BlockSpec((None, nh, S, HEAD_DIM), lambda b, h, i: (b, h, 0, 0)))
    ins, specs = [q, k, v], [qs, ks, ks]
    if win is not None:
        ins.append(win)
        specs.append(pl.BlockSpec((nh, None, 1, S + tq), lambda b, h, i: (h, i, 0, 0)))
    return pl.pallas_call(body, name=name, grid=(B, H // nh, S // tq), in_specs=specs, out_specs=qs,
                          out_shape=jax.ShapeDtypeStruct((B, H, S, HEAD_DIM), F32),
                          compiler_params=_params(("parallel", "parallel", "parallel")))(*ins)


def _attn_bwd(q, k, v, o, do, win, name, nh=ATTN_HEADS, tq=ATTN_TQ):
    B, H, S, _ = q.shape
    hkv = k.shape[1]
    shared = hkv != H
    assert not shared or H // hkv == nh
    tq = min(tq, S)
    nq = S // tq

    def body(*refs):
        if win is None:
            q_ref, k_ref, v_ref, o_ref, do_ref, dq_ref, dk_ref, dv_ref = refs
        else:
            q_ref, k_ref, v_ref, o_ref, do_ref, w_ref, rev_ref, dq_ref, dk_ref, dv_ref, dw_ref = refs

        @pl.when(pl.program_id(2) == 0)
        def _():
            dk_ref[...] = jnp.zeros_like(dk_ref)
            dv_ref[...] = jnp.zeros_like(dv_ref)

        kvs = [(k_ref[...], v_ref[...])] * nh if shared else [(k_ref[h], v_ref[h]) for h in range(nh)]
        qvs, dobs, scores, dps = [], [], [], []
        for h in range(nh):
            qv, dov = q_ref[h], do_ref[h]
            dob = dov.astype(BF16)
            s = _nt(qv, kvs[h][0])
            if win is not None:
                s = s + _toeplitz(w_ref[h], tq, S)
            dp = _nt(dob, kvs[h][1]) - jnp.sum(dov * o_ref[h], axis=-1, keepdims=True)
            qvs.append(qv)
            dobs.append(dob)
            scores.append(s)
            dps.append(dp)
        pbs, dsbs = [], []
        for s, dp in zip(scores, dps):
            p = jnp.exp(s - jnp.max(s, axis=-1, keepdims=True))
            p = p * (1.0 / jnp.sum(p, axis=-1, keepdims=True))
            pbs.append(p.astype(BF16))
            dsbs.append((p * dp).astype(BF16))
        dk_acc = dv_acc = None
        for h in range(nh):
            dvh, dkh = _tn(pbs[h], dobs[h]), _tn(dsbs[h], qvs[h])
            dq_ref[h] = _nn(dsbs[h], kvs[h][0])
            if shared:
                dv_acc = dvh if dv_acc is None else dv_acc + dvh
                dk_acc = dkh if dk_acc is None else dk_acc + dkh
            else:
                dv_ref[h] += dvh
                dk_ref[h] += dkh
            if win is not None:
                rev = _nn(rev_ref[...], dsbs[h])
                wide = jnp.concatenate([rev, jnp.zeros((tq, tq), F32)], axis=1)
                dw_ref[h] = jnp.sum(pltpu.roll(wide, 0, 1, stride=1, stride_axis=0), axis=0, keepdims=True)
        if shared:
            dv_ref[...] += dv_acc
            dk_ref[...] += dk_acc

    qs = pl.BlockSpec((None, nh, tq, HEAD_DIM), lambda b, h, i: (b, h, i, 0))
    ks = (pl.BlockSpec((None, None, S, HEAD_DIM), lambda b, h, i: (b, h, 0, 0)) if shared
          else pl.BlockSpec((None, nh, S, HEAD_DIM), lambda b, h, i: (b, h, 0, 0)))
    ins, specs = [q, k, v, o, do], [qs, ks, ks, qs, qs]
    outs = [jax.ShapeDtypeStruct((B, H, S, HEAD_DIM), F32), jax.ShapeDtypeStruct((B, hkv, S, HEAD_DIM), F32), jax.ShapeDtypeStruct((B, hkv, S, HEAD_DIM), F32)]
    ospecs = [qs, ks, ks]
    if win is not None:
        ins += [win, jnp.asarray(np.eye(tq, dtype=np.float32)[::-1].copy(), BF16)]
        specs += [pl.BlockSpec((nh, None, 1, S + tq), lambda b, h, i: (h, i, 0, 0)), pl.BlockSpec((tq, tq), lambda b, h, i: (0, 0))]
        outs.append(jax.ShapeDtypeStruct((B, H, nq, 1, S + tq), F32))
        ospecs.append(pl.BlockSpec((None, nh, None, 1, S + tq), lambda b, h, i: (b, h, i, 0, 0)))
    return pl.pallas_call(body, name=name, grid=(B, H // nh, nq), in_specs=specs, out_specs=ospecs, out_shape=outs,
                          compiler_params=_params(("parallel", "parallel", "arbitrary")))(*ins)


def _pattern_count(delta):
    n = jnp.zeros(delta.shape, jnp.int32)
    for window, dil in DIL_PATTERNS:
        n = n + ((delta % dil == 0) & (jnp.abs(delta) <= window // 2)).astype(jnp.int32)
    return n


def _t5_bucket(rel):
    nb = REL_BUCKETS // 2
    max_exact = nb // 2
    ret = jnp.where(rel > 0, nb, 0)
    n = jnp.abs(rel)
    nf = jnp.maximum(n, 1).astype(F32)
    large = max_exact + (jnp.log(nf / max_exact) / math.log(REL_MAX_DIST / max_exact) * (nb - max_exact)).astype(jnp.int32)
    large = jnp.minimum(large, nb - 1)
    return ret + jnp.where(n < max_exact, n, large)


def _bias_windows(rel_bias, S):
    tq = min(ATTN_TQ, S)
    nq = S // tq
    n = nq * (S + tq)
    delta = (jnp.arange(S + tq)[None, :] - (jnp.arange(nq)[:, None] + 1) * tq).reshape(n)
    count = _pattern_count(delta)
    onehot = (_t5_bucket(delta)[None, :] == jnp.arange(REL_BUCKETS)[:, None]).astype(F32)
    extra = jnp.where(count > 0, jnp.log(jnp.maximum(count, 1).astype(F32)), MASKED).reshape(1, n)
    live = (count > 0).astype(F32).reshape(1, n)

    def body(t_ref, oh_ref, live_ref, extra_ref, o_ref):
        o_ref[...] = _nn(t_ref[...], oh_ref[...], HIGHEST) * live_ref[...] + extra_ref[...]

    val = pl.pallas_call(body, name="bias_windows", out_shape=jax.ShapeDtypeStruct((N_HEADS, n), F32),
                         compiler_params=_params())(rel_bias.T, onehot, live, extra)
    return val.reshape(N_HEADS, nq, 1, S + tq)


def _bias_fold(dwin, S, name):
    B, H, nq = dwin.shape[:3]
    tq = min(ATTN_TQ, S)
    n = nq * (S + tq)
    delta = (jnp.arange(S + tq)[None, :] - (tq - 1) - jnp.arange(nq)[:, None] * tq).reshape(n)
    onehot = (_t5_bucket(delta)[:, None] == jnp.arange(128)[None, :]).astype(F32)

    def body(d_ref, oh_ref, o_ref):
        tot = d_ref[0]
        for b in range(1, B):
            tot = tot + d_ref[b]
        o_ref[...] = _nn(tot, oh_ref[...], HIGHEST)

    out = pl.pallas_call(body, name=name, out_shape=jax.ShapeDtypeStruct((H, 128), F32), compiler_params=_params())(dwin.reshape(B, H, n), onehot)
    return out[:, :REL_BUCKETS].T


def _rope_tables(S):
    half = 16
    freqs = ROPE_THETA ** (-jnp.arange(half, dtype=F32) / half)
    t = jnp.arange(S)
    ang_r = (t // GRID_W).astype(F32)[:, None] * freqs[None, :]
    ang_c = (t % GRID_W).astype(F32)[:, None] * freqs[None, :]
    cos = jnp.concatenate([jnp.cos(ang_r)] * 2 + [jnp.cos(ang_c)] * 2, axis=1)
    sin = jnp.concatenate([-jnp.sin(ang_r), jnp.sin(ang_r), -jnp.sin(ang_c), jnp.sin(ang_c)], axis=1)
    return jnp.tile(cos, (1, N_HEADS)), jnp.tile(sin, (1, N_HEADS))


def _mix_fwd(ya, ob, yc, od, gain, B, S, name):
    T = B * S
    tm = min(256, S)
    ns = S // tm

    def body(ya_ref, ob_ref, yc_ref, od_ref, g_ref, o_ref):
        ys = [ya_ref[...], _load_heads(ob_ref, N_HEADS), yc_ref[...], _load_heads(od_ref, N_HEADS)]
        for m, y in enumerate(ys):
            r = lax.rsqrt(jnp.mean(y * y, axis=-1, keepdims=True) + RMS_EPS)
            o_ref[:, 512 * m:512 * (m + 1)] = (y * r * g_ref[:, 512 * m:512 * (m + 1)]).astype(BF16)

    row = pl.BlockSpec((tm, 512), lambda b, i: (b * ns + i, 0))
    heads = pl.BlockSpec((None, N_HEADS, tm, HEAD_DIM), lambda b, i: (b, 0, i, 0))
    return pl.pallas_call(
        body, name=name, grid=(B, ns), in_specs=[row, heads, row, heads, pl.BlockSpec((1, 2048), lambda b, i: (0, 0))],
        out_specs=pl.BlockSpec((tm, 2048), lambda b, i: (b * ns + i, 0)), out_shape=jax.ShapeDtypeStruct((T, 2048), BF16),
        compiler_params=_params(("parallel", "parallel")),
    )(ya, ob, yc, od, gain)


def _mix_bwd(ya, ob, yc, od, dycat, gain, B, S, name):
    T = B * S
    tm = min(256, S)
    ns = S // tm

    def body(ya_ref, ob_ref, yc_ref, od_ref, dy_ref, g_ref, dya_ref, dob_ref, dyc_ref, dod_ref, dg_ref):
        @pl.when((pl.program_id(0) == 0) & (pl.program_id(1) == 0))
        def _():
            dg_ref[...] = jnp.zeros_like(dg_ref)

        ys = [ya_ref[...], _load_heads(ob_ref, N_HEADS), yc_ref[...], _load_heads(od_ref, N_HEADS)]
        outs = [dya_ref, dob_ref, dyc_ref, dod_ref]
        for m, y in enumerate(ys):
            cols = slice(512 * m, 512 * (m + 1))
            r = lax.rsqrt(jnp.mean(y * y, axis=-1, keepdims=True) + RMS_EPS)
            yh = y * r
            dh = dy_ref[:, cols]
            dg_ref[:, cols] += jnp.sum(dh * yh, axis=0, keepdims=True)
            dyh = dh * g_ref[:, cols]
            dyv = r * (dyh - yh * jnp.mean(dyh * yh, axis=-1, keepdims=True))
            if m % 2 == 0:
                outs[m][...] = dyv
            else:
                _store_heads(outs[m], dyv, N_HEADS)

    row = pl.BlockSpec((tm, 512), lambda b, i: (b * ns + i, 0))
    heads = pl.BlockSpec((None, N_HEADS, tm, HEAD_DIM), lambda b, i: (b, 0, i, 0))
    vec = pl.BlockSpec((1, 2048), lambda b, i: (0, 0))
    flat = jax.ShapeDtypeStruct((T, 512), F32)
    hm = jax.ShapeDtypeStruct((B, N_HEADS, S, HEAD_DIM), F32)
    return pl.pallas_call(
        body, name=name, grid=(B, ns), in_specs=[row, heads, row, heads, pl.BlockSpec((tm, 2048), lambda b, i: (b * ns + i, 0)), vec],
        out_specs=[row, heads, row, heads, vec], out_shape=[flat, hm, flat, hm, jax.ShapeDtypeStruct((1, 2048), F32)],
        compiler_params=_params(("arbitrary", "arbitrary")),
    )(ya, ob, yc, od, dycat, gain)


def _ffn_down(gate, up, w_down, res, name, norm=None, target=None):
    J, T, n = gate.shape
    N = w_down.shape[2]
    tm = min(256, T)
    steps = T // tm

    def body(g_ref, u_ref, w_ref, r_ref, *rest):
        x_ref = rest[0] if (norm is not None or target is not None) else None
        o_ref, act_ref = rest[-3:-1] if x_ref is not None else rest[-2:]
        acc = None
        for j in range(J):
            g = g_ref[j].astype(F32)
            a = (g * _sigmoid(g) * u_ref[j].astype(F32)).astype(BF16)
            act_ref[j] = a
            d = _nn(a, w_ref[j])
            acc = d if acc is None else acc + d
        y = acc + r_ref[...]
        if target is None:
            o_ref[...] = y
            if norm is not None:
                rest[-1][...] = _rms_rows(y, x_ref[...])
            return
        err = y - x_ref[...]
        o_ref[...] = err * (1.0 / N)
        loss_ref, i = rest[-1], pl.program_id(0)

        @pl.when(i == 0)
        def _():
            loss_ref[...] = jnp.zeros_like(loss_ref)

        loss_ref[...] += jnp.sum(err * err)

        @pl.when(i == steps - 1)
        def _():
            loss_ref[...] = loss_ref[...] * (0.5 / N)

    gu = pl.BlockSpec((J, tm, n), lambda i: (0, i, 0))
    row = pl.BlockSpec((tm, N), lambda i: (i, 0))
    ins, specs = [gate, up, w_down, res], [gu, gu, pl.BlockSpec((J, n, N), lambda i: (0, 0, 0)), row]
    outs, ospecs = [jax.ShapeDtypeStruct((T, N), F32), jax.ShapeDtypeStruct((J, T, n), BF16)], [row, gu]
    if target is not None:
        ins, specs = ins + [target], specs + [row]
        outs, ospecs = outs + [jax.ShapeDtypeStruct((8, 128), F32)], ospecs + [pl.BlockSpec((8, 128), lambda i: (0, 0))]
    elif norm is not None:
        ins, specs = ins + [norm.reshape(1, N)], specs + [pl.BlockSpec((1, N), lambda i: (0, 0))]
        outs, ospecs = outs + [jax.ShapeDtypeStruct((T, N), BF16)], ospecs + [row]
    return pl.pallas_call(body, name=name, grid=(steps,), in_specs=specs, out_specs=ospecs, out_shape=outs,
                          compiler_params=_params(("arbitrary" if target is not None else "parallel",)))(*ins)


def _ffn_down_dx(dx, w_down, gate, up, name):
    J, n, D = w_down.shape
    T = dx.shape[0]
    tm = min(1024, T)

    def body(dx_ref, w_ref, g_ref, u_ref, dg_ref, du_ref):
        d = _nt(dx_ref[...].astype(BF16), w_ref[...])
        g = g_ref[...].astype(F32)
        s = _sigmoid(g)
        dg_ref[...] = (d * u_ref[...].astype(F32) * s * (1.0 + g * (1.0 - s))).astype(BF16)
        du_ref[...] = (d * g * s).astype(BF16)

    blk = pl.BlockSpec((None, tm, n), lambda j, i: (j, i, 0))
    shape = jax.ShapeDtypeStruct((J, T, n), BF16)
    return pl.pallas_call(body, name=name, grid=(J, T // tm),
                          in_specs=[pl.BlockSpec((tm, D), lambda j, i: (i, 0)), pl.BlockSpec((None, n, D), lambda j, i: (j, 0, 0)), blk, blk],
                          out_specs=[blk, blk], out_shape=[shape, shape], compiler_params=_params(("parallel", "parallel")))(dx, w_down, gate, up)


def _row_tile(R):
    best = R
    for cand in range(16, min(R, 272) + 1, 16):
        if R % cand == 0:
            best = cand
    return best


def _adamw(w, m, v, stack, name, layer=None, prev=None):
    n, R, C = stack.shape
    tm = _row_tile(R)
    nb = R // tm
    off = 0 if layer is None else layer * nb
    c1 = 1.0 - ADAM_B1 ** ADAM_STEP
    c2 = 1.0 - ADAM_B2 ** ADAM_STEP

    def body(w_ref, m_ref, v_ref, s_ref, *rest):
        g_ref, d_ref, mo_ref, vo_ref = rest[-4:]
        g = s_ref[0].astype(F32)
        for k in range(1, n):
            g = g + s_ref[k].astype(F32)
        mn = ADAM_B1 * m_ref[...] + (1.0 - ADAM_B1) * g
        vn = ADAM_B2 * v_ref[...] + (1.0 - ADAM_B2) * (g * g)
        g_ref[...] = g
        mo_ref[...] = mn
        vo_ref[...] = vn
        d_ref[...] = -ADAM_LR * ((mn / c1) / (jnp.sqrt(vn / c2) + ADAM_EPS) + ADAM_WD * w_ref[...])

    blk = pl.BlockSpec((tm, C), lambda i: (i + off, 0))
    ins = [w, m, v, stack]
    specs = [blk, blk, blk, pl.BlockSpec((n, tm, C), lambda i: (0, i, 0))]
    aliases = {}
    if prev is not None:
        ins += list(prev)
        specs += [pl.BlockSpec(memory_space=pl.ANY)] * 4
        aliases = {4 + t: t for t in range(4)}
    shape = jax.ShapeDtypeStruct(w.shape, F32)
    return pl.pallas_call(body, name=name, grid=(nb,), in_specs=specs, out_specs=[blk] * 4, out_shape=[shape] * 4,
                          input_output_aliases=aliases, compiler_params=_params(("parallel",)))(*ins)


HBM = pl.BlockSpec(memory_space=pltpu.HBM)
SEM = pl.BlockSpec(memory_space=pltpu.SEMAPHORE)
EFFECT = pltpu.SideEffectType.DATAFLOW_SIDE_EFFECTING


PEERS = {"scatter": (1, 2, 3, 4, 5, 6, 7), "gather": (1, 2, 3, 4, 5, 6, 7), "chips": (1, 2, 4, 6), "forward": (2, 4, 6)}


def _spread_copies(srcs, lands, send_sems, recv_sems, local_sems, kind, waiting):
    x, y, c = lax.axis_index("x"), lax.axis_index("y"), lax.axis_index("c")
    me = 4 * x + 2 * y + c

    def peer(bits):
        dev = (1 - x if bits & 4 else x, 1 - y if bits & 2 else y, 1 - c if bits & 1 else c)
        return dev, 4 * dev[0] + 2 * dev[1] + dev[2]

    plan = PEERS[kind]
    remote, local = [], []
    for a, l in enumerate(lands):
        for d, bits in enumerate(plan):
            dev, pid = peer(bits)
            if kind == "forward":
                src, dst, dev = l.at[pid], l.at[peer(bits | 1)[1] if waiting else pid], peer(1)[0]
            else:
                src, dst = (srcs[a].at[pid] if kind == "scatter" else srcs[a]), l.at[pid if waiting else me]
            remote.append(pltpu.make_async_remote_copy(
                src_ref=src, dst_ref=dst, send_sem=send_sems.at[a * len(plan) + d], recv_sem=recv_sems.at[a * len(plan) + d],
                device_id=dev, device_id_type=MESH_ID))
        if kind != "forward":
            local.append(pltpu.make_async_copy(srcs[a].at[me] if kind == "scatter" else srcs[a], l.at[me], local_sems.at[a]))
    return remote, local


def _spread_start(srcs, kind, name, after=None, lands=None):
    if kind == "forward":
        srcs = []
    else:
        shapes = [a.shape if kind == "scatter" else (N_DEV,) + a.shape for a in srcs]
        lands = [lax.empty(shp, a.dtype) for shp, a in zip(shapes, srcs)]
    ns, nl, per = len(srcs), len(lands), len(PEERS[kind])
    extra = [] if after is None else [after]
    sem_shapes = [pltpu.SemaphoreType.DMA((nl * per,))] * 2 + ([pltpu.SemaphoreType.DMA((nl,))] if ns else [])

    def body(*refs):
        src_refs, land_refs = refs[:ns], refs[ns:ns + nl]
        sems = refs[ns + nl + len(extra):ns + nl + len(extra) + len(sem_shapes)]
        remote, local = _spread_copies(src_refs, land_refs, sems[0], sems[1], sems[2] if ns else None, kind, False)
        for cp in remote + local:
            cp.start()
        refs[-1][...] = jnp.zeros((8, 128), F32)

    outs = pl.pallas_call(
        body, name=name,
        out_shape=(*sem_shapes, *[pltpu.HBM(a.shape, a.dtype) for a in srcs + lands], jax.ShapeDtypeStruct((8, 128), F32)),
        in_specs=[HBM] * (ns + nl) + [pl.BlockSpec(memory_space=pl.ANY)] * len(extra),
        out_specs=(*[SEM] * len(sem_shapes), *[HBM] * (ns + nl), pl.BlockSpec(memory_space=pltpu.VMEM)),
        input_output_aliases={i: len(sem_shapes) + i for i in range(ns + nl)},
        compiler_params=pltpu.CompilerParams(has_side_effects=EFFECT),
    )(*[pltpu.with_memory_space_constraint(a, pltpu.HBM) for a in srcs + lands], *extra)
    k = len(sem_shapes)
    return outs[:k], list(outs[k:k + ns]), list(outs[k + ns:k + ns + nl]), outs[-1]


def _spread_wait(sems, srcs, lands, after, kind, name):
    ns, nl = len(srcs), len(lands)
    after = list(after) if isinstance(after, (list, tuple)) else [after]

    def body(*refs):
        src_refs, land_refs = refs[:ns], refs[ns:ns + nl]
        s = refs[ns + nl:ns + nl + len(sems)]
        remote, local = _spread_copies(src_refs, land_refs, s[0], s[1], s[2] if ns else None, kind, True)
        for cp in remote:
            cp.wait_send()
            cp.wait_recv()
        for cp in local:
            cp.wait()

    outs = pl.pallas_call(
        body, name=name, out_shape=tuple(pltpu.HBM(a.shape, a.dtype) for a in srcs + lands),
        in_specs=[HBM] * (ns + nl) + [SEM] * len(sems) + [pl.BlockSpec(memory_space=pl.ANY)] * len(after), out_specs=tuple([HBM] * (ns + nl)),
        input_output_aliases={i: i for i in range(ns + nl)}, compiler_params=pltpu.CompilerParams(has_side_effects=EFFECT),
    )(*srcs, *lands, *sems, *after)
    return list(outs[ns:])


SMALL = ("rel_bias", "norm1_g", "sgu_w", "sgu_b", "dil_qn_g", "dil_kn_g", "conv_w", "conv_b", "conv_ln_g", "conv_ln_b",
         "gqa_qn_g", "gqa_kn_g", "mix_norm_g", "norm2_g")
LARGE = ("w_in", "w_out", "w_gate", "w_up", "w_down")
EARLY = tuple(k for k in SMALL if k != "norm1_g")


def _local_step(x, target, p, B, S, fetch, emit, mid, early):
    T = B * S
    rope = _rope_tables(S)
    win = _bias_windows(p["rel_bias"], S)
    tile8 = lambda g: jnp.tile(g.reshape(1, HEAD_DIM), (1, N_HEADS))
    cols_b = (COL_BQ, COL_BK, COL_BV)
    cols_d = (COL_DQ, COL_DK128, COL_DV128)
    saved = []
    for l in range(DEPTH):
        s = {"x": x}
        s["ws"] = p["sgu_w"][l].astype(BF16)
        s["bias"] = jnp.repeat(p["sgu_b"][l].T, HEAD_DIM, axis=1)
        s["h"] = _rms_fwd(x, p["norm1_g"][l], f"rms1_fwd_{l}") if l == 0 else h_next
        s["win"] = fetch(l, "in", s["h"])
        s["cw"] = jnp.pad(s["win"]["conv_w"], ((0, 1), (0, 0))).reshape(32, 1, 512)
        z = s["z"] = _matmul(s["h"], s["win"]["w_in"], "nt", f"in_proj_{l}", tk=D_MODEL)
        s["bias"] = s["bias"] + mid(l, z)
        s["ya"] = _sgu_fwd(z, s["ws"], s["bias"], f"sgu_fwd_{l}")
        s["c"] = _conv_fwd1(z, s["cw"], p["conv_b"][l].reshape(1, 512), B, S, f"conv_fwd_{l}")
        s["yc"] = _conv_fwd2(s["c"], p["conv_ln_g"][l].reshape(1, 512), p["conv_ln_b"][l].reshape(1, 512), f"conv_ln_fwd_{l}")
        s["gb"] = (tile8(p["dil_qn_g"][l]), tile8(p["dil_kn_g"][l]))
        s["gd"] = (tile8(p["gqa_qn_g"][l]), tile8(p["gqa_kn_g"][l])[:, :KV_WIDTH])
        s["qkv_b"] = _prep_fwd(z, *s["gb"], None, B, S, N_HEADS, cols_b, f"prep_b_fwd_{l}")
        s["qkv_d"] = _prep_fwd(z, *s["gd"], rope, B, S, KV_HEADS, cols_d, f"prep_d_fwd_{l}")
        s["ob"] = _attn_fwd(*s["qkv_b"], win, f"attn_b_fwd_{l}")
        s["od"] = _attn_fwd(*s["qkv_d"], None, f"attn_d_fwd_{l}", tq=GQA_TQ)
        s["gmix"] = p["mix_norm_g"][l].reshape(1, 2048)
        s["ycat"] = _mix_fwd(s["ya"], s["ob"], s["yc"], s["od"], s["gmix"], B, S, f"mix_fwd_{l}")
        s["wout"] = fetch(l, "out", s["ycat"])["w_out"]
        x1, s["h2"] = _matmul(s["ycat"], s["wout"], "nn", f"out_proj_{l}", res=x, tk=D_MODEL, norm=p["norm2_g"][l])
        s["x1"] = x1
        s["ffn"] = fetch(l, "ffn", s["h2"])
        s["gate"] = _mm_shard_out(s["h2"], s["ffn"]["w_gate"], "nt", f"ffn_gate_{l}", out_dtype=BF16, tm=1024)
        s["up"] = _mm_shard_out(s["h2"], s["ffn"]["w_up"], "nt", f"ffn_up_{l}", out_dtype=BF16, tm=1024)
        if l + 1 < DEPTH:
            x, s["act"], h_next = _ffn_down(s["gate"], s["up"], s["ffn"]["w_down"], x1, f"ffn_down_{l}", norm=p["norm1_g"][l + 1])
        else:
            dx, s["act"], loss_blk = _ffn_down(s["gate"], s["up"], s["ffn"]["w_down"], x1, f"ffn_down_{l}", target=target)
        saved.append(s)

    g = {k: [None] * DEPTH for k in SMALL if k != "rel_bias"}
    dwin_total = None
    for l in reversed(range(DEPTH)):
        s = saved[l]
        z, ffn = s["z"], s["ffn"]
        dgate, dup = _ffn_down_dx(dx, ffn["w_down"], s["gate"], s["up"], f"ffn_down_dx_{l}")
        tok = emit(l, "w_down", _mm_shard_m(s["act"], dx, f"ffn_down_dw_{l}", out_dtype=BF16, tn=512, tk=T))
        tok += emit(l, "w_gate", _mm_shard_m(dgate, s["h2"], f"ffn_gate_dw_{l}", out_dtype=BF16, tn=1024, tk=T))
        tok += emit(l, "w_up", _mm_shard_m(dup, s["h2"], f"ffn_up_dw_{l}", out_dtype=BF16, tn=1024, tk=T))
        dh2 = _mm_shard_k([(dgate, ffn["w_gate"]), (dup, ffn["w_up"])], "nn", f"ffn_up_dx_{l}", tn=512, fold=N_DEV)
        dx1, dg2 = _rms_bwd(dh2, s["x1"], p["norm2_g"][l] + tok, dx, f"rms2_bwd_{l}")
        g["norm2_g"][l] = dg2[0]
        dycat = _matmul(dx1, s["wout"], "nt", f"out_proj_dx_{l}", tk=D_MODEL)
        tok = emit(l, "w_out", _matmul(s["ycat"], dx1, "tn", f"out_proj_dw_{l}", out_dtype=BF16, tn=1024, tk=T))
        dya, dob, dyc, dod, dgm = _mix_bwd(s["ya"], s["ob"], s["yc"], s["od"], dycat, s["gmix"] + tok, B, S, f"mix_bwd_{l}")
        g["mix_norm_g"][l] = dgm[0]
        dz_a, dws, dbias = _sgu_bwd(z, dya, s["ws"], jnp.swapaxes(s["ws"], 1, 2), s["bias"], f"sgu_bwd_{l}")
        g["sgu_w"][l] = dws
        g["sgu_b"][l] = dbias.reshape(128, 8, HEAD_DIM).sum(-1).T
        dc, dlg, dlb, dcb = _conv_bwd1(s["c"], dyc, p["conv_ln_g"][l].reshape(1, 512), p["conv_ln_b"][l].reshape(1, 512), f"conv_ln_bwd_{l}")
        g["conv_ln_g"][l], g["conv_ln_b"][l], g["conv_b"][l] = dlg[0], dlb[0], dcb[0]
        dz_ca, dz_cg, dcw = _conv_bwd2(z, dc, s["cw"], B, S, f"conv_bwd_{l}")
        g["conv_w"][l] = dcw.reshape(32, 512)[:CONV_WIDTH]
        dq, dk, dv, dwin = _attn_bwd(*s["qkv_b"], s["ob"], dob, win, f"attn_b_bwd_{l}")
        dwin_total = dwin if dwin_total is None else dwin_total + dwin
        dz_b, dgq, dgk = _prep_bwd(z, dq, dk, dv, *s["gb"], None, B, S, N_HEADS, cols_b, f"prep_b_bwd_{l}")
        g["dil_qn_g"][l] = dgq.reshape(N_HEADS, HEAD_DIM).sum(0)
        g["dil_kn_g"][l] = dgk.reshape(N_HEADS, HEAD_DIM).sum(0)
        dq, dk, dv = _attn_bwd(*s["qkv_d"], s["od"], dod, None, f"attn_d_bwd_{l}", tq=GQA_TQ)
        dz_d, dgq, dgk = _prep_bwd(z, dq, dk, dv, *s["gd"], rope, B, S, KV_HEADS, cols_d, f"prep_d_bwd_{l}")
        g["gqa_qn_g"][l] = dgq.reshape(N_HEADS, HEAD_DIM).sum(0)
        g["gqa_kn_g"][l] = dgk.reshape(KV_HEADS, HEAD_DIM).sum(0)
        dz = [dz_a, dz_b, dz_ca, dz_cg, dz_d]
        tok = jnp.zeros((), F32)
        if l == 0:
            done = {k: jnp.stack(v) for k, v in g.items() if k != "norm1_g"}
            done["rel_bias"] = _bias_fold(dwin_total, S, "bias_fold")
            tok = early(done)
        tok += emit(l, "w_in", _in_proj_dw(dz, s["h"], f"in_proj_dw_{l}"))
        dh = _in_proj_dx(dz, s["win"]["w_in"], f"in_proj_dx_{l}")
        dx, dg1 = _rms_bwd(dh, s["x"], p["norm1_g"][l] + tok, dx1, f"rms1_bwd_{l}")
        g["norm1_g"][l] = dg1[0]

    return loss_blk[0, 0], dx, jnp.stack(g["norm1_g"])


GROUPS = {"in": ("w_in",), "out": ("w_out",), "ffn": ("w_gate", "w_up", "w_down")}
COL_SHARDED = ("w_in", "w_gate", "w_up")


def kernel(x, rel_bias, norm1_g, w_in, sgu_w, sgu_b, dil_qn_g, dil_kn_g, conv_w, conv_b, conv_ln_g, conv_ln_b, gqa_qn_g, gqa_kn_g, mix_norm_g, w_out, norm2_g, w_gate, w_up, w_down, loss_target, m_rel_bias, m_norm1_g, m_w_in, m_sgu_w, m_sgu_b, m_dil_qn_g, m_dil_kn_g, m_conv_w, m_conv_b, m_conv_ln_g, m_conv_ln_b, m_gqa_qn_g, m_gqa_kn_g, m_mix_norm_g, m_w_out, m_norm2_g, m_w_gate, m_w_up, m_w_down, v_rel_bias, v_norm1_g, v_w_in, v_sgu_w, v_sgu_b, v_dil_qn_g, v_dil_kn_g, v_conv_w, v_conv_b, v_conv_ln_g, v_conv_ln_b, v_gqa_qn_g, v_gqa_kn_g, v_mix_norm_g, v_w_out, v_norm2_g, v_w_gate, v_w_up, v_w_down):
    w = dict(rel_bias=rel_bias, norm1_g=norm1_g, w_in=w_in, sgu_w=sgu_w, sgu_b=sgu_b, dil_qn_g=dil_qn_g, dil_kn_g=dil_kn_g, conv_w=conv_w,
             conv_b=conv_b, conv_ln_g=conv_ln_g, conv_ln_b=conv_ln_b, gqa_qn_g=gqa_qn_g, gqa_kn_g=gqa_kn_g, mix_norm_g=mix_norm_g,
             w_out=w_out, norm2_g=norm2_g, w_gate=w_gate, w_up=w_up, w_down=w_down)
    m = dict(rel_bias=m_rel_bias, norm1_g=m_norm1_g, w_in=m_w_in, sgu_w=m_sgu_w, sgu_b=m_sgu_b, dil_qn_g=m_dil_qn_g, dil_kn_g=m_dil_kn_g,
             conv_w=m_conv_w, conv_b=m_conv_b, conv_ln_g=m_conv_ln_g, conv_ln_b=m_conv_ln_b, gqa_qn_g=m_gqa_qn_g, gqa_kn_g=m_gqa_kn_g,
             mix_norm_g=m_mix_norm_g, w_out=m_w_out, norm2_g=m_norm2_g, w_gate=m_w_gate, w_up=m_w_up, w_down=m_w_down)
    v = dict(rel_bias=v_rel_bias, norm1_g=v_norm1_g, w_in=v_w_in, sgu_w=v_sgu_w, sgu_b=v_sgu_b, dil_qn_g=v_dil_qn_g, dil_kn_g=v_dil_kn_g,
             conv_w=v_conv_w, conv_b=v_conv_b, conv_ln_g=v_conv_ln_g, conv_ln_b=v_conv_ln_b, gqa_qn_g=v_gqa_qn_g, gqa_kn_g=v_gqa_kn_g,
             mix_norm_g=v_mix_norm_g, w_out=v_w_out, norm2_g=v_norm2_g, w_gate=v_w_gate, w_up=v_w_up, w_down=v_w_down)
    names = list(w)
    B, S, D = x.shape
    T = B * S
    me = 4 * lax.axis_index("x") + 2 * lax.axis_index("y") + lax.axis_index("c")

    view = lambda a, k: jnp.swapaxes(a, 1, 2) if k in COL_SHARDED else a
    bf = {k: view(w[k], k).astype(BF16) for k in LARGE}
    spreads, forwards = {}, {}

    def start_gather(l, group, after=None):
        srcs = [bf[k][l] for k in GROUPS[group]] + ([conv_w[l]] if group == "in" else [])
        spreads[l, group] = _spread_start(srcs, "chips", f"gather_{group}_{l}_start", after)
        return spreads[l, group][3][0, 0]

    def forward(l, group, after):
        sems, srcs, lands, _ = spreads[l, group]
        lands = _spread_wait(sems, srcs, lands, after, "chips", f"gather_{group}_{l}_wait")
        forwards[l, group] = _spread_start(None, "forward", f"forward_{group}_{l}_start", lands=lands)
        return forwards[l, group][3]

    def landed(l, group, after):
        sems, _, lands, _ = forwards[l, group]
        return _spread_wait(sems, [], lands, after, "forward", f"forward_{group}_{l}_wait")

    tok0 = start_gather(0, "in") + start_gather(0, "out") + start_gather(0, "ffn")
    small = {k: w[k] for k in SMALL}
    small["norm1_g"] = norm1_g.at[0].add(tok0)

    def mid(l, z):
        if l > 0:
            return jnp.zeros((), F32)
        return start_gather(1, "in", z) + start_gather(1, "out", z) + start_gather(1, "ffn", z)

    def fetch(l, group, after):
        if group == "in":
            tok = forward(0, "in", after) if l == 0 else after
        elif group == "out":
            tok = forward(l, "ffn", [after, forward(l, "out", after)])
        else:
            tok = forward(1, "in", after) if l == 0 else after
        got = dict(zip(GROUPS[group] + ("conv_w",), landed(l, group, [after, tok])))
        if group == "in":
            got["w_in"] = got["w_in"].reshape(IN_WIDTH, D)
            got["conv_w"] = jnp.transpose(got["conv_w"], (1, 0, 2)).reshape(CONV_WIDTH, 512)
        if group == "out":
            got["w_out"] = got["w_out"].reshape(D, D)
        return got

    scatters = {}

    def emit(l, k, dw):
        if k in ("w_in", "w_out"):
            dw = dw.reshape(N_DEV, dw.shape[0] // N_DEV, D)
        scatters[l, k] = _spread_start([dw], "scatter", f"scatter_{k}_{l}_start")
        return scatters[l, k][3][0, 0]

    flat2 = lambda a: a.reshape(-1, a.shape[-1])
    small_spread = []

    def early(done):
        small_spread.append(_spread_start([flat2(done[k]) for k in EARLY], "gather", "gather_small_grads_start"))
        return small_spread[0][3][0, 0]

    loss_part, dx, dnorm1 = _local_step(x.reshape(T, D), loss_target.reshape(T, D), small, B, S, fetch, emit, mid, early)
    loss = lax.psum(loss_part, ("x", "y", "c"))

    out_g, out_d, out_m, out_v = {}, {}, {}, {}

    def update_large(k, after):
        shp = view(w[k], k).shape
        two_d = lambda a: view(a, k).reshape(-1, shp[-1])
        res = None
        for l in reversed(range(DEPTH)):
            sems, srcs, lands, _ = scatters[l, k]
            stack = _spread_wait(sems, srcs, lands, after, "scatter", f"scatter_{k}_{l}_wait")[0]
            res = _adamw(two_d(w[k]), two_d(m[k]), two_d(v[k]), stack.reshape(N_DEV, -1, shp[-1]), f"adamw_{k}_{l}", layer=l, prev=res)
        out_g[k], out_d[k], out_m[k], out_v[k] = [view(a.reshape(shp), k) for a in res]
        return res[0]

    late_sems, late_srcs, late_lands, late_tok = _spread_start([flat2(dnorm1)], "gather", "gather_norm1_grad_start")
    after = [dx, late_tok]
    for k in ("w_down", "w_gate", "w_up", "w_out"):
        after = update_large(k, after)
    sems, srcs, lands, _ = small_spread[0]
    stacks = dict(zip(EARLY, _spread_wait(sems, srcs, lands, after, "gather", "gather_small_grads_wait")))
    stacks["norm1_g"] = _spread_wait(late_sems, late_srcs, late_lands, after, "gather", "gather_norm1_grad_wait")[0]
    for k in SMALL:
        stack = stacks[k]
        if k == "conv_w":
            stack = lax.dynamic_slice_in_dim(stack, me * (512 // N_DEV), 512 // N_DEV, axis=2)
        res = _adamw(flat2(w[k]), flat2(m[k]), flat2(v[k]), stack, f"adamw_{k}")
        out_g[k], out_d[k], out_m[k], out_v[k] = [a.reshape(w[k].shape) for a in res]
        after = res[0]
    update_large("w_in", after)

    return (loss, dx.reshape(B, S, D), *[out_g[k] for k in names], *[out_d[k] for k in names],
            *[out_m[k] for k in names], *[out_v[k] for k in names])
```

```python
import functools
import math

import numpy as np
import jax
import jax.numpy as jnp
from jax import lax
from jax.experimental import pallas as pl
from jax.experimental.pallas import tpu as pltpu

F32 = jnp.float32
BF16 = jnp.bfloat16
HIGHEST = lax.Precision.HIGHEST
MESH_ID = pl.DeviceIdType.MESH

D_MODEL = 2048
DEPTH = 2
HEAD_DIM = 64
GROUP_WIDTH = 512
N_HEADS = 8
KV_HEADS = 2
KV_WIDTH = 128
SGU_CHUNK = 128
CONV_WIDTH = 31
CONV_PAD = 16
GRID_W = 64
ROPE_THETA = 10000.0
REL_BUCKETS = 32
REL_MAX_DIST = 1024
DIL_PATTERNS = ((128, 1), (512, 4), (2048, 16))
FFN_HIDDEN = 5632
IN_WIDTH = 4352
RMS_EPS = 1e-6
LN_EPS = 1e-5
MASKED = -1e30
N_DEV = 8

ADAM_LR = 0.001
ADAM_B1 = 0.9
ADAM_B2 = 0.999
ADAM_EPS = 1e-08
ADAM_WD = 0.01
ADAM_STEP = 10

COL_AU, COL_AV, COL_BQ, COL_BK, COL_BV, COL_CA, COL_CG, COL_DQ = range(8)
COL_DK128, COL_DV128 = 32, 33

VMEM_LIMIT = 56 * 1024 * 1024
FFN_GROUPS = 2
FFN_BLOCK = 1408
ATTN_TQ = 256
GQA_TQ = 512


def _params(sem=None, vmem=VMEM_LIMIT):
    return pltpu.CompilerParams(dimension_semantics=sem, vmem_limit_bytes=vmem)


def _dot(a, b, dims, precision=None):
    return lax.dot_general(a, b, (dims, ((), ())), precision=precision, preferred_element_type=F32)


def _nn(a, b, precision=None):
    return _dot(a, b, ((1,), (0,)), precision)


def _nt(a, b):
    return _dot(a, b, ((1,), (1,)))


def _tn(a, b):
    return _dot(a, b, ((0,), (0,)))


DIMS = {"nn": ((1,), (0,)), "nt": ((1,), (1,)), "tn": ((0,), (0,))}


def _pick(n, cands):
    for c in cands:
        if n % c == 0:
            return c
    return n


def _mm_call(name, mode, pairs, specs, o_spec, out_sds, grid, acc_shape, res=None, fold=None, norm=None):
    npair, nk, dims = len(pairs), grid[2], DIMS[mode]

    def body(*refs):
        ab = refs[:2 * npair]
        at = 2 * npair
        r_ref = refs[at] if res is not None else None
        at += res is not None
        g_ref = refs[at] if norm is not None else None
        at += norm is not None
        o_ref = refs[at]
        h_ref = refs[at + 1] if norm is not None else None
        part = None
        for t in range(npair):
            for s in ([None] if fold is None else range(fold)):
                a_blk = ab[2 * t][...] if s is None else ab[2 * t][s]
                b_blk = ab[2 * t + 1][...] if s is None else ab[2 * t + 1][s]
                d = _dot(a_blk.astype(BF16), b_blk.astype(BF16), dims)
                part = d if part is None else part + d

        def finish(r):
            if r_ref is not None:
                r = r + r_ref[...]
            o_ref[...] = r.astype(o_ref.dtype)
            if h_ref is not None:
                h_ref[...] = _rms_rows(r, g_ref[...])

        if nk == 1:
            finish(part)
            return
        acc, k = refs[-1], pl.program_id(2)

        @pl.when(k == 0)
        def _():
            acc[...] = part

        @pl.when(k > 0)
        def _():
            acc[...] += part

        @pl.when(k == nk - 1)
        def _():
            finish(acc[...])

    ins = [t for pair in pairs for t in pair]
    in_specs = [t for pair in specs for t in pair]
    if res is not None:
        ins.append(res)
        in_specs.append(o_spec)
    out_specs = o_spec
    if norm is not None:
        ins.append(norm)
        in_specs.append(pl.BlockSpec(norm.shape, lambda *_: (0, 0)))
        out_specs, out_sds = [o_spec, o_spec], [out_sds, jax.ShapeDtypeStruct(out_sds.shape, BF16)]
    return pl.pallas_call(
        body, name=name, grid=grid, in_specs=in_specs, out_specs=out_specs, out_shape=out_sds,
        scratch_shapes=[pltpu.VMEM(acc_shape, F32)] if nk > 1 else [],
        compiler_params=_params(("parallel", "parallel", "arbitrary")),
    )(*ins)


def _rms_rows(x, g):
    return (x * lax.rsqrt(jnp.mean(x * x, axis=-1, keepdims=True) + RMS_EPS) * g).astype(BF16)


def _matmul(a, b, mode, name, res=None, out_dtype=F32, tm=512, tn=None, tk=None, norm=None):
    if mode == "nn":
        (M, K), N = a.shape, b.shape[1]
    elif mode == "nt":
        (M, K), N = a.shape, b.shape[0]
    else:
        (K, M), N = a.shape, b.shape[1]
    tm = min(tm, M)
    tn = tn or _pick(N, (2176, 2048, 1408, 1024, 512))
    tk = tk or _pick(K, (1024, 2176, 1408, 512))
    assert M % tm == 0 and N % tn == 0 and K % tk == 0, (M, N, K, tm, tn, tk)
    a_spec = pl.BlockSpec((tk, tm), lambda i, j, k: (k, i)) if mode == "tn" else pl.BlockSpec((tm, tk), lambda i, j, k: (i, k))
    b_spec = pl.BlockSpec((tn, tk), lambda i, j, k: (j, k)) if mode == "nt" else pl.BlockSpec((tk, tn), lambda i, j, k: (k, j))
    o_spec = pl.BlockSpec((tm, tn), lambda i, j, k: (i, j))
    assert norm is None or tn == N
    return _mm_call(name, mode, [(a, b)], [(a_spec, b_spec)], o_spec, jax.ShapeDtypeStruct((M, N), out_dtype),
                    (M // tm, N // tn, K // tk), (tm, tn), res, norm=None if norm is None else norm.reshape(1, N))


def _mm_shard_out(a, bs, mode, name, out_dtype=F32, tm=512, tk=None):
    J = bs.shape[0]
    n = bs.shape[1] if mode == "nt" else bs.shape[2]
    (K, M) = a.shape if mode == "tn" else a.shape[::-1]
    tm = min(tm, M)
    tk = tk or (K if mode != "tn" else _pick(K, (1024, 512)))
    a_spec = pl.BlockSpec((tk, tm), lambda j, i, k: (k, i)) if mode == "tn" else pl.BlockSpec((tm, tk), lambda j, i, k: (i, k))
    b_spec = pl.BlockSpec((None, n, tk), lambda j, i, k: (j, 0, k)) if mode == "nt" else pl.BlockSpec((None, tk, n), lambda j, i, k: (j, k, 0))
    o_spec = pl.BlockSpec((None, tm, n), lambda j, i, k: (j, i, 0))
    return _mm_call(name, mode, [(a, bs)], [(a_spec, b_spec)], o_spec, jax.ShapeDtypeStruct((J, M, n), out_dtype),
                    (J, M // tm, K // tk), (tm, n))


def _mm_shard_k(pairs, mode, name, res=None, out_dtype=F32, tm=512, tn=None, fold=1):
    J, M, n = pairs[0][0].shape
    N = pairs[0][1].shape[2] if mode == "nn" else pairs[0][1].shape[1]
    tm = min(tm, M)
    tn = tn or _pick(N, (2048, 1024, 512))
    a_spec = pl.BlockSpec((fold, tm, n), lambda i, j, k: (k, i, 0))
    b_spec = pl.BlockSpec((fold, n, tn), lambda i, j, k: (k, 0, j)) if mode == "nn" else pl.BlockSpec((fold, tn, n), lambda i, j, k: (k, j, 0))
    o_spec = pl.BlockSpec((tm, tn), lambda i, j, k: (i, j))
    return _mm_call(name, mode, pairs, [(a_spec, b_spec)] * len(pairs), o_spec, jax.ShapeDtypeStruct((M, N), out_dtype),
                    (M // tm, N // tn, J // fold), (tm, tn), res, fold)


def _mm_shard_m(as_, b, name, out_dtype=F32, tm=None, tn=None, tk=512):
    J, K, n = as_.shape
    N = b.shape[1]
    tm = tm or n
    tn = tn or _pick(N, (2048, 1024, 512))
    tk = min(tk, K)
    nn = N // tn
    a_spec = pl.BlockSpec((None, tk, tm), lambda j, i, k: (j, k, i // nn))
    b_spec = pl.BlockSpec((tk, tn), lambda j, i, k: (k, i % nn))
    o_spec = pl.BlockSpec((None, tm, tn), lambda j, i, k: (j, i // nn, i % nn))
    return _mm_call(name, "tn", [(as_, b)], [(a_spec, b_spec)], o_spec, jax.ShapeDtypeStruct((J, n, N), out_dtype),
                    (J, (n // tm) * nn, K // tk), (tm, tn))


def _in_proj_dw(pieces, h, name):
    T, D = h.shape
    tm = 256
    nbs = [p.shape[1] // tm for p in pieces]
    los = [sum(nbs[:t]) for t in range(len(pieces))]

    def body(*refs):
        h_ref, o_ref = refs[-2:]
        i = pl.program_id(0)
        for p_ref, lo, nb in zip(refs[:-2], los, nbs):
            @pl.when((i >= lo) & (i < lo + nb))
            def _():
                o_ref[...] = _tn(p_ref[...], h_ref[...]).astype(BF16)

    specs = [pl.BlockSpec((T, tm), (lambda lo, nb: lambda i: (0, jnp.clip(i - lo, 0, nb - 1)))(lo, nb)) for lo, nb in zip(los, nbs)]
    return pl.pallas_call(body, name=name, grid=(sum(nbs),), in_specs=specs + [pl.BlockSpec((T, D), lambda i: (0, 0))],
                          out_specs=pl.BlockSpec((tm, D), lambda i: (i, 0)), out_shape=jax.ShapeDtypeStruct((sum(nbs) * tm, D), BF16),
                          compiler_params=_params(("parallel",)))(*pieces, h)


def _in_proj_dx(pieces, w, name, tm=512, tn=1024):
    T = pieces[0].shape[0]
    K, D = w.shape
    tm = min(tm, T)
    widths = [p.shape[1] for p in pieces]
    offs = [sum(widths[:t]) for t in range(len(pieces))]

    def body(*refs):
        w_ref, o_ref = refs[-2:]
        acc = None
        for p_ref, off, wd in zip(refs[:-2], offs, widths):
            d = _nn(p_ref[...], w_ref[off:off + wd, :])
            acc = d if acc is None else acc + d
        o_ref[...] = acc

    specs = [pl.BlockSpec((tm, wd), lambda i, j: (i, 0)) for wd in widths]
    return pl.pallas_call(body, name=name, grid=(T // tm, D // tn), in_specs=specs + [pl.BlockSpec((K, tn), lambda i, j: (0, j))],
                          out_specs=pl.BlockSpec((tm, tn), lambda i, j: (i, j)), out_shape=jax.ShapeDtypeStruct((T, D), F32),
                          compiler_params=_params(("parallel", "parallel")))(*pieces, w)


def _seg_matrix(width):
    return jnp.asarray(np.kron(np.eye(width // HEAD_DIM, dtype=np.float32), np.full((HEAD_DIM, HEAD_DIM), 1.0 / HEAD_DIM, np.float32)), BF16)


def _segmean(v, p):
    hi = v.astype(BF16)
    r = v - hi.astype(F32)
    mid = r.astype(BF16)
    lo = (r - mid.astype(F32)).astype(BF16)
    w = min(256, v.shape[1])
    pw = p[:w, :w]
    halves = []
    for c in range(v.shape[1] // w):
        cols = slice(c * w, (c + 1) * w)
        halves.append(_nn(hi[:, cols], pw) + _nn(mid[:, cols], pw) + _nn(lo[:, cols], pw))
    return halves[0] if len(halves) == 1 else jnp.concatenate(halves, axis=1)


def _gelu(x):
    c0 = math.sqrt(2.0 / math.pi)
    t = jnp.tanh(c0 * (x + 0.044715 * x * x * x))
    return 0.5 * x * (1.0 + t), t


def _gelu_grad(x, t):
    c0 = math.sqrt(2.0 / math.pi)
    return 0.5 * (1.0 + t) + 0.5 * x * (1.0 - t * t) * c0 * (1.0 + 3.0 * 0.044715 * x * x)


def _sigmoid(x):
    return 1.0 / (1.0 + jnp.exp(-x))


def _rms_fwd(x, g, name):
    T, D = x.shape
    tm = min(256, T)

    def body(x_ref, g_ref, o_ref):
        o_ref[...] = _rms_rows(x_ref[...], g_ref[...])

    return pl.pallas_call(
        body, name=name, grid=(T // tm,),
        in_specs=[pl.BlockSpec((tm, D), lambda i: (i, 0)), pl.BlockSpec((1, D), lambda i: (0, 0))],
        out_specs=pl.BlockSpec((tm, D), lambda i: (i, 0)), out_shape=jax.ShapeDtypeStruct((T, D), BF16),
        compiler_params=_params(("parallel",)),
    )(x, g.reshape(1, D))


def _rms_bwd(dh, x, g, dres, name):
    T, D = x.shape
    tm = min(256, T)

    def body(dh_ref, x_ref, g_ref, dres_ref, dx_ref, dg_ref):
        @pl.when(pl.program_id(0) == 0)
        def _():
            dg_ref[...] = jnp.zeros_like(dg_ref)

        xv, dhv = x_ref[...], dh_ref[...]
        r = lax.rsqrt(jnp.mean(xv * xv, axis=-1, keepdims=True) + RMS_EPS)
        y = xv * r
        dy = dhv * g_ref[...]
        dx_ref[...] = dres_ref[...] + r * (dy - y * jnp.mean(dy * y, axis=-1, keepdims=True))
        dg_ref[...] += jnp.sum(dhv * y, axis=0, keepdims=True)

    row = pl.BlockSpec((tm, D), lambda i: (i, 0))
    vec = pl.BlockSpec((1, D), lambda i: (0, 0))
    return pl.pallas_call(
        body, name=name, grid=(T // tm,), in_specs=[row, row, vec, row], out_specs=[row, vec],
        out_shape=[jax.ShapeDtypeStruct((T, D), F32), jax.ShapeDtypeStruct((1, D), F32)],
        compiler_params=_params(("arbitrary",)),
    )(dh, x, g.reshape(1, D), dres)


def _sgu_core(zu, zv, ws_ref, bias, p):
    ug, tu = _gelu(zu)
    vg, tv = _gelu(zv)
    xc = vg - _segmean(vg, p)
    rs = lax.rsqrt(_segmean(xc * xc, p) + LN_EPS)
    vn = xc * rs
    vnb = vn.astype(BF16)
    low = lax.broadcasted_iota(jnp.int32, (SGU_CHUNK, 128), 1) < HEAD_DIM
    parts = []
    for j in range(4):
        vp = vnb[:, 128 * j:128 * (j + 1)]
        parts.append(jnp.where(low, _nn(ws_ref[2 * j], vp), _nn(ws_ref[2 * j + 1], vp)))
    mixed = jnp.concatenate(parts, axis=1) + bias
    return ug, tu, tv, rs, vn, vnb, mixed, low


def _sgu_fwd(z, ws, bias, name):
    T = z.shape[0]

    def body(zu_ref, zv_ref, ws_ref, b_ref, p_ref, y_ref):
        ug, _, _, _, _, _, mixed, _ = _sgu_core(zu_ref[...], zv_ref[...], ws_ref, b_ref[...], p_ref[...])
        y_ref[...] = ug * mixed

    full = lambda shape: pl.BlockSpec(shape, lambda i: (0,) * len(shape))
    return pl.pallas_call(
        body, name=name, grid=(T // SGU_CHUNK,),
        in_specs=[pl.BlockSpec((SGU_CHUNK, 512), lambda i: (i, COL_AU)), pl.BlockSpec((SGU_CHUNK, 512), lambda i: (i, COL_AV)),
                  full((8, 128, 128)), full((128, 512)), full((512, 512))],
        out_specs=pl.BlockSpec((SGU_CHUNK, 512), lambda i: (i, 0)), out_shape=jax.ShapeDtypeStruct((T, 512), F32),
        compiler_params=_params(("parallel",)),
    )(z, z, ws, bias, _seg_matrix(512))


def _sgu_bwd(z, dy, ws, ws_t, bias, name):
    T = z.shape[0]

    def body(zu_ref, zv_ref, dy_ref, ws_ref, wst_ref, b_ref, p_ref, dz_ref, dws_ref, db_ref):
        @pl.when(pl.program_id(0) == 0)
        def _():
            dws_ref[...] = jnp.zeros_like(dws_ref)
            db_ref[...] = jnp.zeros_like(db_ref)

        zu, zv, p = zu_ref[...], zv_ref[...], p_ref[...]
        ug, tu, tv, rs, vn, vnb, mixed, low = _sgu_core(zu, zv, ws_ref, b_ref[...], p)
        dyv = dy_ref[...]
        dmixed = dyv * ug
        db_ref[...] += dmixed
        dmb = dmixed.astype(BF16)
        zero = jnp.zeros((SGU_CHUNK, 128), BF16)
        parts = []
        for j in range(4):
            dmp, vp = dmb[:, 128 * j:128 * (j + 1)], vnb[:, 128 * j:128 * (j + 1)]
            dws_ref[2 * j] += _nt(jnp.where(low, dmp, zero), vp)
            dws_ref[2 * j + 1] += _nt(jnp.where(low, zero, dmp), vp)
            parts.append(jnp.where(low, _nn(wst_ref[2 * j], dmp), _nn(wst_ref[2 * j + 1], dmp)))
        dvn = jnp.concatenate(parts, axis=1)
        dvg = rs * (dvn - _segmean(dvn, p) - vn * _segmean(dvn * vn, p))
        dz_ref[:, 0:512] = (dyv * mixed * _gelu_grad(zu, tu)).astype(BF16)
        dz_ref[:, 512:1024] = (dvg * _gelu_grad(zv, tv)).astype(BF16)

    full = lambda shape: pl.BlockSpec(shape, lambda i: (0,) * len(shape))
    return pl.pallas_call(
        body, name=name, grid=(T // SGU_CHUNK,),
        in_specs=[pl.BlockSpec((SGU_CHUNK, 512), lambda i: (i, COL_AU)), pl.BlockSpec((SGU_CHUNK, 512), lambda i: (i, COL_AV)),
                  pl.BlockSpec((SGU_CHUNK, 512), lambda i: (i, 0)), full((8, 128, 128)), full((8, 128, 128)), full((128, 512)), full((512, 512))],
        out_specs=[pl.BlockSpec((SGU_CHUNK, 1024), lambda i: (i, 0)), full((8, 128, 128)), full((128, 512))],
        out_shape=[jax.ShapeDtypeStruct((T, 1024), BF16), jax.ShapeDtypeStruct((8, 128, 128), F32), jax.ShapeDtypeStruct((128, 512), F32)],
        compiler_params=_params(("arbitrary",)),
    )(z, z, dy, ws, ws_t, bias, _seg_matrix(512))


CONV_ROWS = 256


def _conv_taps(pad_ref, w_ref, base, flip):
    acc = None
    for k in range(CONV_WIDTH):
        wk = w_ref[CONV_WIDTH - 1 - k if flip else k]
        t = wk * pad_ref[base + k + 1:base + k + 1 + CONV_ROWS, :]
        acc = t if acc is None else acc + t
    return acc


def _conv_fwd1(z, w, cb, B, S, name):
    T = B * S
    rows = min(CONV_ROWS, S)
    assert rows == CONV_ROWS

    def body(a_ref, g_ref, w_ref, cb_ref, c_ref, pad):
        pad[0:CONV_PAD, :] = jnp.zeros((CONV_PAD, 128), F32)
        pad[CONV_PAD + S:2 * CONV_PAD + S, :] = jnp.zeros((CONV_PAD, 128), F32)
        pad[CONV_PAD:CONV_PAD + S, :] = a_ref[...] * _sigmoid(g_ref[...])

        for base in range(0, S, CONV_ROWS):
            c_ref[base:base + CONV_ROWS, :] = _conv_taps(pad, w_ref, base, False) + cb_ref[...]

    return pl.pallas_call(
        body, name=name, grid=(4, B),
        in_specs=[pl.BlockSpec((S, 128), lambda j, b: (b, 4 * COL_CA + j)), pl.BlockSpec((S, 128), lambda j, b: (b, 4 * COL_CG + j)),
                  pl.BlockSpec((32, 1, 128), lambda j, b: (0, 0, j)), pl.BlockSpec((1, 128), lambda j, b: (0, j))],
        out_specs=pl.BlockSpec((S, 128), lambda j, b: (b, j)), out_shape=jax.ShapeDtypeStruct((T, 512), F32),
        scratch_shapes=[pltpu.VMEM((S + 2 * CONV_PAD, 128), F32)], compiler_params=_params(("parallel", "parallel")),
    )(z, z, w, cb)


def _ln_rows(c):
    mu = jnp.mean(c, axis=-1, keepdims=True)
    xc = c - mu
    rs = lax.rsqrt(jnp.mean(xc * xc, axis=-1, keepdims=True) + LN_EPS)
    return xc * rs, rs


def _conv_fwd2(c, lng, lnb, name):
    T = c.shape[0]
    tm = min(512, T)

    def body(c_ref, g_ref, b_ref, y_ref):
        n, _ = _ln_rows(c_ref[...])
        t = n * g_ref[...] + b_ref[...]
        y_ref[...] = t * _sigmoid(t)

    row = pl.BlockSpec((tm, 512), lambda i: (i, 0))
    vec = pl.BlockSpec((1, 512), lambda i: (0, 0))
    return pl.pallas_call(body, name=name, grid=(T // tm,), in_specs=[row, vec, vec], out_specs=row,
                          out_shape=jax.ShapeDtypeStruct((T, 512), F32), compiler_params=_params(("parallel",)))(c, lng, lnb)


def _conv_bwd1(c, dy, lng, lnb, name):
    T = c.shape[0]
    tm = min(512, T)

    def body(c_ref, dy_ref, g_ref, b_ref, dc_ref, dg_ref, db_ref, dcb_ref):
        @pl.when(pl.program_id(0) == 0)
        def _():
            dg_ref[...] = jnp.zeros_like(dg_ref)
            db_ref[...] = jnp.zeros_like(db_ref)
            dcb_ref[...] = jnp.zeros_like(dcb_ref)

        n, rs = _ln_rows(c_ref[...])
        t = n * g_ref[...] + b_ref[...]
        s = _sigmoid(t)
        dt = dy_ref[...] * s * (1.0 + t * (1.0 - s))
        dg_ref[...] += jnp.sum(dt * n, axis=0, keepdims=True)
        db_ref[...] += jnp.sum(dt, axis=0, keepdims=True)
        dn = dt * g_ref[...]
        dc = rs * (dn - jnp.mean(dn, axis=-1, keepdims=True) - n * jnp.mean(dn * n, axis=-1, keepdims=True))
        dc_ref[...] = dc
        dcb_ref[...] += jnp.sum(dc, axis=0, keepdims=True)

    row = pl.BlockSpec((tm, 512), lambda i: (i, 0))
    vec = pl.BlockSpec((1, 512), lambda i: (0, 0))
    vshape = jax.ShapeDtypeStruct((1, 512), F32)
    return pl.pallas_call(body, name=name, grid=(T // tm,), in_specs=[row, row, vec, vec], out_specs=[row, vec, vec, vec],
                          out_shape=[jax.ShapeDtypeStruct((T, 512), F32), vshape, vshape, vshape],
                          compiler_params=_params(("arbitrary",)))(c, dy, lng, lnb)


def _conv_bwd2(z, dc, w, B, S, name):
    T = B * S

    def body(a_ref, g_ref, dc_ref, w_ref, da_ref, dg_ref, dw_ref, hpad, dpad):
        @pl.when(pl.program_id(1) == 0)
        def _():
            dw_ref[...] = jnp.zeros_like(dw_ref)

        zeros = jnp.zeros((CONV_PAD, 128), F32)
        for ref in (hpad, dpad):
            ref[0:CONV_PAD, :] = zeros
            ref[CONV_PAD + S:2 * CONV_PAD + S, :] = zeros
        hpad[CONV_PAD:CONV_PAD + S, :] = a_ref[...] * _sigmoid(g_ref[...])
        dpad[CONV_PAD:CONV_PAD + S, :] = dc_ref[...]
        dws = [None] * CONV_WIDTH
        for base in range(0, S, CONV_ROWS):
            rows = slice(base, base + CONV_ROWS)
            dh = _conv_taps(dpad, w_ref, base, True)
            sg = _sigmoid(g_ref[rows, :])
            da_ref[rows, :] = (dh * sg).astype(BF16)
            dg_ref[rows, :] = (dh * a_ref[rows, :] * sg * (1.0 - sg)).astype(BF16)
            dcv = dc_ref[rows, :]
            for k in range(CONV_WIDTH):
                prod = dcv * hpad[base + k + 1:base + k + 1 + CONV_ROWS, :]
                part = jnp.sum(prod.reshape(CONV_ROWS // 8, 8, 128), axis=0)
                dws[k] = part if dws[k] is None else dws[k] + part
        for k in range(CONV_WIDTH):
            dw_ref[k] += jnp.sum(dws[k], axis=0, keepdims=True)

    return pl.pallas_call(
        body, name=name, grid=(4, B),
        in_specs=[pl.BlockSpec((S, 128), lambda j, b: (b, 4 * COL_CA + j)), pl.BlockSpec((S, 128), lambda j, b: (b, 4 * COL_CG + j)),
                  pl.BlockSpec((S, 128), lambda j, b: (b, j)), pl.BlockSpec((32, 1, 128), lambda j, b: (0, 0, j))],
        out_specs=[pl.BlockSpec((S, 128), lambda j, b: (b, j)), pl.BlockSpec((S, 128), lambda j, b: (b, j)),
                   pl.BlockSpec((32, 1, 128), lambda j, b: (0, 0, j))],
        out_shape=[jax.ShapeDtypeStruct((T, 512), BF16), jax.ShapeDtypeStruct((T, 512), BF16), jax.ShapeDtypeStruct((32, 1, 512), F32)],
        scratch_shapes=[pltpu.VMEM((S + 2 * CONV_PAD, 128), F32), pltpu.VMEM((S + 2 * CONV_PAD, 128), F32)],
        compiler_params=_params(("parallel", "arbitrary")),
    )(z, z, dc, w)


def _swap16(x):
    n = x.shape[1]
    first = (lax.broadcasted_iota(jnp.int32, x.shape, 1) % 32) < 16
    return jnp.where(first, pltpu.roll(x, n - 16, 1), pltpu.roll(x, 16, 1))


def _rope(x, cos, sin):
    return x * cos + _swap16(x) * sin


def _rope_t(dy, cos, sin):
    return dy * cos + _swap16(dy * sin)


def _qk_norm(x, p):
    r = lax.rsqrt(_segmean(x * x, p) + RMS_EPS)
    return x * r, r


def _store_heads(ref, val, n):
    for h in range(n):
        ref[h] = val[:, HEAD_DIM * h:HEAD_DIM * (h + 1)].astype(ref.dtype)


def _load_heads(ref, n):
    return jnp.concatenate([ref[h] for h in range(n)], axis=1)


def _prep_fwd(z, gq, gk, rope, B, S, kv_heads, cols, name):
    tm = min(256, S)
    ns = S // tm
    kw = kv_heads * HEAD_DIM
    scale = HEAD_DIM ** -0.5
    qc, kc, vc = cols

    def body(*refs):
        if rope is None:
            q_ref, k_ref, v_ref, gq_ref, gk_ref, p_ref, qo, ko, vo = refs
        else:
            q_ref, k_ref, v_ref, gq_ref, gk_ref, p_ref, cos_ref, sin_ref, qo, ko, vo = refs
        p = p_ref[...]
        qn, _ = _qk_norm(q_ref[...], p)
        kn, _ = _qk_norm(k_ref[...], p[:kw, :kw])
        qn, kn = qn * gq_ref[...], kn * gk_ref[...]
        if rope is not None:
            cos, sin = cos_ref[...], sin_ref[...]
            qn, kn = _rope(qn, cos, sin), _rope(kn, cos[:, :kw], sin[:, :kw])
        _store_heads(qo, qn * scale, N_HEADS)
        _store_heads(ko, kn, kv_heads)
        _store_heads(vo, v_ref[...], kv_heads)

    row = lambda w, c: pl.BlockSpec((tm, w), lambda b, i: (b * ns + i, c))
    const = lambda shape: pl.BlockSpec(shape, lambda b, i: (0,) * len(shape))
    heads = lambda n: pl.BlockSpec((None, n, tm, HEAD_DIM), lambda b, i: (b, 0, i, 0))
    ins = [z, z, z, gq, gk, _seg_matrix(512)]
    specs = [row(512, qc), row(kw, kc), row(kw, vc), const((1, 512)), const((1, kw)), const((512, 512))]
    if rope is not None:
        ins += list(rope)
        specs += [pl.BlockSpec((tm, 512), lambda b, i: (i, 0))] * 2
    return pl.pallas_call(
        body, name=name, grid=(B, ns), in_specs=specs, out_specs=[heads(N_HEADS), heads(kv_heads), heads(kv_heads)],
        out_shape=[jax.ShapeDtypeStruct((B, N_HEADS, S, HEAD_DIM), BF16), jax.ShapeDtypeStruct((B, kv_heads, S, HEAD_DIM), BF16),
                   jax.ShapeDtypeStruct((B, kv_heads, S, HEAD_DIM), BF16)],
        compiler_params=_params(("parallel", "parallel")),
    )(*ins)


def _prep_bwd(z, dq, dk, dv, gq, gk, rope, B, S, kv_heads, cols, name):
    T = B * S
    tm = min(256, S)
    ns = S // tm
    kw = kv_heads * HEAD_DIM
    scale = HEAD_DIM ** -0.5
    qc, kc, _ = cols

    def body(*refs):
        if rope is None:
            q_ref, k_ref, dq_ref, dk_ref, dv_ref, gq_ref, gk_ref, p_ref, dz_ref, dgq_ref, dgk_ref = refs
        else:
            q_ref, k_ref, dq_ref, dk_ref, dv_ref, gq_ref, gk_ref, p_ref, cos_ref, sin_ref, dz_ref, dgq_ref, dgk_ref = refs

        @pl.when((pl.program_id(0) == 0) & (pl.program_id(1) == 0))
        def _():
            dgq_ref[...] = jnp.zeros_like(dgq_ref)
            dgk_ref[...] = jnp.zeros_like(dgk_ref)

        p = p_ref[...]
        dqv = _load_heads(dq_ref, N_HEADS) * scale
        dkv = _load_heads(dk_ref, kv_heads)
        if rope is not None:
            cos, sin = cos_ref[...], sin_ref[...]
            dqv, dkv = _rope_t(dqv, cos, sin), _rope_t(dkv, cos[:, :kw], sin[:, :kw])

        def through_norm(xv, dy, g, pm, dg_ref):
            xh, r = _qk_norm(xv, pm)
            dg_ref[...] += jnp.sum(dy * xh, axis=0, keepdims=True)
            dxh = dy * g
            return r * (dxh - xh * _segmean(dxh * xh, pm))

        dz_ref[:, 0:512] = through_norm(q_ref[...], dqv, gq_ref[...], p, dgq_ref).astype(BF16)
        dz_ref[:, 512:512 + kw] = through_norm(k_ref[...], dkv, gk_ref[...], p[:kw, :kw], dgk_ref).astype(BF16)
        dz_ref[:, 512 + kw:512 + 2 * kw] = _load_heads(dv_ref, kv_heads).astype(BF16)

    row = lambda w, c: pl.BlockSpec((tm, w), lambda b, i: (b * ns + i, c))
    const = lambda shape: pl.BlockSpec(shape, lambda b, i: (0,) * len(shape))
    heads = lambda n: pl.BlockSpec((None, n, tm, HEAD_DIM), lambda b, i: (b, 0, i, 0))
    ins = [z, z, dq, dk, dv, gq, gk, _seg_matrix(512)]
    specs = [row(512, qc), row(kw, kc), heads(N_HEADS), heads(kv_heads), heads(kv_heads), const((1, 512)), const((1, kw)), const((512, 512))]
    if rope is not None:
        ins += list(rope)
        specs += [pl.BlockSpec((tm, 512), lambda b, i: (i, 0))] * 2
    return pl.pallas_call(
        body, name=name, grid=(B, ns), in_specs=specs, out_specs=[row(512 + 2 * kw, 0), const((1, 512)), const((1, kw))],
        out_shape=[jax.ShapeDtypeStruct((T, 512 + 2 * kw), BF16), jax.ShapeDtypeStruct((1, 512), F32), jax.ShapeDtypeStruct((1, kw), F32)],
        compiler_params=_params(("arbitrary", "arbitrary")),
    )(*ins)


def _toeplitz(win, tq, S):
    r = pltpu.roll(jnp.broadcast_to(win, (tq, S + tq)), 0, 1, stride=1, stride_axis=0)
    return r[:, tq:tq + S]


ATTN_HEADS = 4


def _attn_fwd(q, k, v, win, name, nh=ATTN_HEADS, tq=ATTN_TQ):
    B, H, S, _ = q.shape
    shared = k.shape[1] != H
    assert not shared or H // k.shape[1] == nh
    tq = min(tq, S)

    def body(*refs):
        if win is None:
            q_ref, k_ref, v_ref, o_ref = refs
        else:
            q_ref, k_ref, v_ref, w_ref, o_ref = refs
        kvs = [(k_ref[...], v_ref[...])] * nh if shared else [(k_ref[h], v_ref[h]) for h in range(nh)]
        scores = []
        for h in range(nh):
            s = _nt(q_ref[h], kvs[h][0])
            if win is not None:
                s = s + _toeplitz(w_ref[h], tq, S)
            scores.append(s)
        probs = []
        for s in scores:
            p = jnp.exp(s - jnp.max(s, axis=-1, keepdims=True))
            probs.append((p.astype(BF16), jnp.sum(p, axis=-1, keepdims=True)))
        for h, (p, l) in enumerate(probs):
            o_ref[h] = _nn(p, kvs[h][1]) / l

    qs = pl.BlockSpec((None, nh, tq, HEAD_DIM), lambda b, h, i: (b, h, i, 0))
    ks = (pl.BlockSpec((None, None, S, HEAD_DIM), lambda b, h, i: (b, h, 0, 0)) if shared
          else pl.BlockSpec((None, nh, S, HEAD_DIM), lambda b, h, i: (b, h, 0, 0)))
    ins, specs = [q, k, v], [qs, ks, ks]
    if win is not None:
        ins.append(win)
        specs.append(pl.BlockSpec((nh, None, 1, S + tq), lambda b, h, i: (h, i, 0, 0)))
    return pl.pallas_call(body, name=name, grid=(B, H // nh, S // tq), in_specs=specs, out_specs=qs,
                          out_shape=jax.ShapeDtypeStruct((B, H, S, HEAD_DIM), F32),
                          compiler_params=_params(("parallel", "parallel", "parallel")))(*ins)


def _attn_bwd(q, k, v, o, do, win, name, nh=ATTN_HEADS, tq=ATTN_TQ):
    B, H, S, _ = q.shape
    hkv = k.shape[1]
    shared = hkv != H
    assert not shared or H // hkv == nh
    tq = min(tq, S)
    nq = S // tq

    def body(*refs):
        if win is None:
            q_ref, k_ref, v_ref, o_ref, do_ref, dq_ref, dk_ref, dv_ref = refs
        else:
            q_ref, k_ref, v_ref, o_ref, do_ref, w_ref, rev_ref, dq_ref, dk_ref, dv_ref, dw_ref = refs

        @pl.when(pl.program_id(2) == 0)
        def _():
            dk_ref[...] = jnp.zeros_like(dk_ref)
            dv_ref[...] = jnp.zeros_like(dv_ref)

        kvs = [(k_ref[...], v_ref[...])] * nh if shared else [(k_ref[h], v_ref[h]) for h in range(nh)]
        qvs, dobs, scores, dps = [], [], [], []
        for h in range(nh):
            qv, dov = q_ref[h], do_ref[h]
            dob = dov.astype(BF16)
            s = _nt(qv, kvs[h][0])
            if win is not None:
                s = s + _toeplitz(w_ref[h], tq, S)
            dp = _nt(dob, kvs[h][1]) - jnp.sum(dov * o_ref[h], axis=-1, keepdims=True)
            qvs.append(qv)
            dobs.append(dob)
            scores.append(s)
            dps.append(dp)
        pbs, dsbs = [], []
        for s, dp in zip(scores, dps):
            p = jnp.exp(s - jnp.max(s, axis=-1, keepdims=True))
            p = p * (1.0 / jnp.sum(p, axis=-1, keepdims=True))
            pbs.append(p.astype(BF16))
            dsbs.append((p * dp).astype(BF16))
        dk_acc = dv_acc = None
        for h in range(nh):
            dvh, dkh = _tn(pbs[h], dobs[h]), _tn(dsbs[h], qvs[h])
            dq_ref[h] = _nn(dsbs[h], kvs[h][0])
            if shared:
                dv_acc = dvh if dv_acc is None else dv_acc + dvh
                dk_acc = dkh if dk_acc is None else dk_acc + dkh
            else:
                dv_ref[h] += dvh
                dk_ref[h] += dkh
            if win is not None:
                rev = _nn(rev_ref[...], dsbs[h])
                wide = jnp.concatenate([rev, jnp.zeros((tq, tq), F32)], axis=1)
                dw_ref[h] = jnp.sum(pltpu.roll(wide, 0, 1, stride=1, stride_axis=0), axis=0, keepdims=True)
        if shared:
            dv_ref[...] += dv_acc
            dk_ref[...] += dk_acc

    qs = pl.BlockSpec((None, nh, tq, HEAD_DIM), lambda b, h, i: (b, h, i, 0))
    ks = (pl.BlockSpec((None, None, S, HEAD_DIM), lambda b, h, i: (b, h, 0, 0)) if shared
          else pl.BlockSpec((None, nh, S, HEAD_DIM), lambda b, h, i: (b, h, 0, 0)))
    ins, specs = [q, k, v, o, do], [qs, ks, ks, qs, qs]
    outs = [jax.ShapeDtypeStruct((B, H, S, HEAD_DIM), F32), jax.ShapeDtypeStruct((B, hkv, S, HEAD_DIM), F32), jax.ShapeDtypeStruct((B, hkv, S, HEAD_DIM), F32)]
    ospecs = [qs, ks, ks]
    if win is not None:
        ins += [win, jnp.asarray(np.eye(tq, dtype=np.float32)[::-1].copy(), BF16)]
        specs += [pl.BlockSpec((nh, None, 1, S + tq), lambda b, h, i: (h, i, 0, 0)), pl.BlockSpec((tq, tq), lambda b, h, i: (0, 0))]
        outs.append(jax.ShapeDtypeStruct((B, H, nq, 1, S + tq), F32))
        ospecs.append(pl.BlockSpec((None, nh, None, 1, S + tq), lambda b, h, i: (b, h, i, 0, 0)))
    return pl.pallas_call(body, name=name, grid=(B, H // nh, nq), in_specs=specs, out_specs=ospecs, out_shape=outs,
                          compiler_params=_params(("parallel", "parallel", "arbitrary")))(*ins)


def _pattern_count(delta):
    n = jnp.zeros(delta.shape, jnp.int32)
    for window, dil in DIL_PATTERNS:
        n = n + ((delta % dil == 0) & (jnp.abs(delta) <= window // 2)).astype(jnp.int32)
    return n


def _t5_bucket(rel):
    nb = REL_BUCKETS // 2
    max_exact = nb // 2
    ret = jnp.where(rel > 0, nb, 0)
    n = jnp.abs(rel)
    nf = jnp.maximum(n, 1).astype(F32)
    large = max_exact + (jnp.log(nf / max_exact) / math.log(REL_MAX_DIST / max_exact) * (nb - max_exact)).astype(jnp.int32)
    large = jnp.minimum(large, nb - 1)
    return ret + jnp.where(n < max_exact, n, large)


def _bias_windows(rel_bias, S):
    tq = min(ATTN_TQ, S)
    nq = S // tq
    n = nq * (S + tq)
    delta = (jnp.arange(S + tq)[None, :] - (jnp.arange(nq)[:, None] + 1) * tq).reshape(n)
    count = _pattern_count(delta)
    onehot = (_t5_bucket(delta)[None, :] == jnp.arange(REL_BUCKETS)[:, None]).astype(F32)
    extra = jnp.where(count > 0, jnp.log(jnp.maximum(count, 1).astype(F32)), MASKED).reshape(1, n)
    live = (count > 0).astype(F32).reshape(1, n)

    def body(t_ref, oh_ref, live_ref, extra_ref, o_ref):
        o_ref[...] = _nn(t_ref[...], oh_ref[...], HIGHEST) * live_ref[...] + extra_ref[...]

    val = pl.pallas_call(body, name="bias_windows", out_shape=jax.ShapeDtypeStruct((N_HEADS, n), F32),
                         compiler_params=_params())(rel_bias.T, onehot, live, extra)
    return val.reshape(N_HEADS, nq, 1, S + tq)


def _bias_fold(dwin, S, name):
    B, H, nq = dwin.shape[:3]
    tq = min(ATTN_TQ, S)
    n = nq * (S + tq)
    delta = (jnp.arange(S + tq)[None, :] - (tq - 1) - jnp.arange(nq)[:, None] * tq).reshape(n)
    onehot = (_t5_bucket(delta)[:, None] == jnp.arange(128)[None, :]).astype(F32)

    def body(d_ref, oh_ref, o_ref):
        tot = d_ref[0]
        for b in range(1, B):
            tot = tot + d_ref[b]
        o_ref[...] = _nn(tot, oh_ref[...], HIGHEST)

    out = pl.pallas_call(body, name=name, out_shape=jax.ShapeDtypeStruct((H, 128), F32), compiler_params=_params())(dwin.reshape(B, H, n), onehot)
    return out[:, :REL_BUCKETS].T


def _rope_tables(S):
    half = 16
    freqs = ROPE_THETA ** (-jnp.arange(half, dtype=F32) / half)
    t = jnp.arange(S)
    ang_r = (t // GRID_W).astype(F32)[:, None] * freqs[None, :]
    ang_c = (t % GRID_W).astype(F32)[:, None] * freqs[None, :]
    cos = jnp.concatenate([jnp.cos(ang_r)] * 2 + [jnp.cos(ang_c)] * 2, axis=1)
    sin = jnp.concatenate([-jnp.sin(ang_r), jnp.sin(ang_r), -jnp.sin(ang_c), jnp.sin(ang_c)], axis=1)
    return jnp.tile(cos, (1, N_HEADS)), jnp.tile(sin, (1, N_HEADS))


def _mix_fwd(ya, ob, yc, od, gain, B, S, name):
    T = B * S
    tm = min(256, S)
    ns = S // tm

    def body(ya_ref, ob_ref, yc_ref, od_ref, g_ref, o_ref):
        ys = [ya_ref[...], _load_heads(ob_ref, N_HEADS), yc_ref[...], _load_heads(od_ref, N_HEADS)]
        for m, y in enumerate(ys):
            r = lax.rsqrt(jnp.mean(y * y, axis=-1, keepdims=True) + RMS_EPS)
            o_ref[:, 512 * m:512 * (m + 1)] = (y * r * g_ref[:, 512 * m:512 * (m + 1)]).astype(BF16)

    row = pl.BlockSpec((tm, 512), lambda b, i: (b * ns + i, 0))
    heads = pl.BlockSpec((None, N_HEADS, tm, HEAD_DIM), lambda b, i: (b, 0, i, 0))
    return pl.pallas_call(
        body, name=name, grid=(B, ns), in_specs=[row, heads, row, heads, pl.BlockSpec((1, 2048), lambda b, i: (0, 0))],
        out_specs=pl.BlockSpec((tm, 2048), lambda b, i: (b * ns + i, 0)), out_shape=jax.ShapeDtypeStruct((T, 2048), BF16),
        compiler_params=_params(("parallel", "parallel")),
    )(ya, ob, yc, od, gain)


def _mix_bwd(ya, ob, yc, od, dycat, gain, B, S, name):
    T = B * S
    tm = min(256, S)
    ns = S // tm

    def body(ya_ref, ob_ref, yc_ref, od_ref, dy_ref, g_ref, dya_ref, dob_ref, dyc_ref, dod_ref, dg_ref):
        @pl.when((pl.program_id(0) == 0) & (pl.program_id(1) == 0))
        def _():
            dg_ref[...] = jnp.zeros_like(dg_ref)

        ys = [ya_ref[...], _load_heads(ob_ref, N_HEADS), yc_ref[...], _load_heads(od_ref, N_HEADS)]
        outs = [dya_ref, dob_ref, dyc_ref, dod_ref]
        for m, y in enumerate(ys):
            cols = slice(512 * m, 512 * (m + 1))
            r = lax.rsqrt(jnp.mean(y * y, axis=-1, keepdims=True) + RMS_EPS)
            yh = y * r
            dh = dy_ref[:, cols]
            dg_ref[:, cols] += jnp.sum(dh * yh, axis=0, keepdims=True)
            dyh = dh * g_ref[:, cols]
            dyv = r * (dyh - yh * jnp.mean(dyh * yh, axis=-1, keepdims=True))
            if m % 2 == 0:
                outs[m][...] = dyv
            else:
                _store_heads(outs[m], dyv, N_HEADS)

    row = pl.BlockSpec((tm, 512), lambda b, i: (b * ns + i, 0))
    heads = pl.BlockSpec((None, N_HEADS, tm, HEAD_DIM), lambda b, i: (b, 0, i, 0))
    vec = pl.BlockSpec((1, 2048), lambda b, i: (0, 0))
    flat = jax.ShapeDtypeStruct((T, 512), F32)
    hm = jax.ShapeDtypeStruct((B, N_HEADS, S, HEAD_DIM), F32)
    return pl.pallas_call(
        body, name=name, grid=(B, ns), in_specs=[row, heads, row, heads, pl.BlockSpec((tm, 2048), lambda b, i: (b * ns + i, 0)), vec],
        out_specs=[row, heads, row, heads, vec], out_shape=[flat, hm, flat, hm, jax.ShapeDtypeStruct((1, 2048), F32)],
        compiler_params=_params(("arbitrary", "arbitrary")),
    )(ya, ob, yc, od, dycat, gain)


def _ffn_down(gate, up, w_down, res, name, norm=None, target=None):
    J, T, n = gate.shape
    N = w_down.shape[2]
    tm = min(256, T)
    steps = T // tm

    def body(g_ref, u_ref, w_ref, r_ref, *rest):
        x_ref = rest[0] if (norm is not None or target is not None) else None
        o_ref, act_ref = rest[-3:-1] if x_ref is not None else rest[-2:]
        acc = None
        for j in range(J):
            g = g_ref[j].astype(F32)
            a = (g * _sigmoid(g) * u_ref[j].astype(F32)).astype(BF16)
            act_ref[j] = a
            d = _nn(a, w_ref[j])
            acc = d if acc is None else acc + d
        y = acc + r_ref[...]
        if target is None:
            o_ref[...] = y
            if norm is not None:
                rest[-1][...] = _rms_rows(y, x_ref[...])
            return
        err = y - x_ref[...]
        o_ref[...] = err * (1.0 / N)
        loss_ref, i = rest[-1], pl.program_id(0)

        @pl.when(i == 0)
        def _():
            loss_ref[...] = jnp.zeros_like(loss_ref)

        loss_ref[...] += jnp.sum(err * err)

        @pl.when(i == steps - 1)
        def _():
            loss_ref[...] = loss_ref[...] * (0.5 / N)

    gu = pl.BlockSpec((J, tm, n), lambda i: (0, i, 0))
    row = pl.BlockSpec((tm, N), lambda i: (i, 0))
    ins, specs = [gate, up, w_down, res], [gu, gu, pl.BlockSpec((J, n, N), lambda i: (0, 0, 0)), row]
    outs, ospecs = [jax.ShapeDtypeStruct((T, N), F32), jax.ShapeDtypeStruct((J, T, n), BF16)], [row, gu]
    if target is not None:
        ins, specs = ins + [target], specs + [row]
        outs, ospecs = outs + [jax.ShapeDtypeStruct((8, 128), F32)], ospecs + [pl.BlockSpec((8, 128), lambda i: (0, 0))]
    elif norm is not None:
        ins, specs = ins + [norm.reshape(1, N)], specs + [pl.BlockSpec((1, N), lambda i: (0, 0))]
        outs, ospecs = outs + [jax.ShapeDtypeStruct((T, N), BF16)], ospecs + [row]
    return pl.pallas_call(body, name=name, grid=(steps,), in_specs=specs, out_specs=ospecs, out_shape=outs,
                          compiler_params=_params(("arbitrary" if target is not None else "parallel",)))(*ins)


def _ffn_down_dx(dx, w_down, gate, up, name):
    J, n, D = w_down.shape
    T = dx.shape[0]
    tm = min(512, T)

    def body(dx_ref, w_ref, g_ref, u_ref, dg_ref, du_ref):
        d = _nt(dx_ref[...].astype(BF16), w_ref[...])
        g = g_ref[...].astype(F32)
        s = _sigmoid(g)
        dg_ref[...] = (d * u_ref[...].astype(F32) * s * (1.0 + g * (1.0 - s))).astype(BF16)
        du_ref[...] = (d * g * s).astype(BF16)

    blk = pl.BlockSpec((None, tm, n), lambda j, i: (j, i, 0))
    shape = jax.ShapeDtypeStruct((J, T, n), BF16)
    return pl.pallas_call(body, name=name, grid=(J, T // tm),
                          in_specs=[pl.BlockSpec((tm, D), lambda j, i: (i, 0)), pl.BlockSpec((None, n, D), lambda j, i: (j, 0, 0)), blk, blk],
                          out_specs=[blk, blk], out_shape=[shape, shape], compiler_params=_params(("parallel", "parallel")))(dx, w_down, gate, up)


def _row_tile(R):
    best = R
    for cand in range(16, min(R, 272) + 1, 16):
        if R % cand == 0:
            best = cand
    return best


def _adamw(w, m, v, stack, name, layer=None, prev=None):
    n, R, C = stack.shape
    tm = _row_tile(R)
    nb = R // tm
    off = 0 if layer is None else layer * nb
    c1 = 1.0 - ADAM_B1 ** ADAM_STEP
    c2 = 1.0 - ADAM_B2 ** ADAM_STEP

    def body(w_ref, m_ref, v_ref, s_ref, *rest):
        g_ref, d_ref, mo_ref, vo_ref = rest[-4:]
        g = s_ref[0].astype(F32)
        for k in range(1, n):
            g = g + s_ref[k].astype(F32)
        mn = ADAM_B1 * m_ref[...] + (1.0 - ADAM_B1) * g
        vn = ADAM_B2 * v_ref[...] + (1.0 - ADAM_B2) * (g * g)
        g_ref[...] = g
        mo_ref[...] = mn
        vo_ref[...] = vn
        d_ref[...] = -ADAM_LR * ((mn / c1) / (jnp.sqrt(vn / c2) + ADAM_EPS) + ADAM_WD * w_ref[...])

    blk = pl.BlockSpec((tm, C), lambda i: (i + off, 0))
    ins = [w, m, v, stack]
    specs = [blk, blk, blk, pl.BlockSpec((n, tm, C), lambda i: (0, i, 0))]
    aliases = {}
    if prev is not None:
        ins += list(prev)
        specs += [pl.BlockSpec(memory_space=pl.ANY)] * 4
        aliases = {4 + t: t for t in range(4)}
    shape = jax.ShapeDtypeStruct(w.shape, F32)
    return pl.pallas_call(body, name=name, grid=(nb,), in_specs=specs, out_specs=[blk] * 4, out_shape=[shape] * 4,
                          input_output_aliases=aliases, compiler_params=_params(("parallel",)))(*ins)


HBM = pl.BlockSpec(memory_space=pltpu.HBM)
SEM = pl.BlockSpec(memory_space=pltpu.SEMAPHORE)
EFFECT = pltpu.SideEffectType.DATAFLOW_SIDE_EFFECTING


PEERS = {"scatter": (1, 2, 3, 4, 5, 6, 7), "gather": (1, 2, 3, 4, 5, 6, 7), "chips": (1, 2, 4, 6), "forward": (2, 4, 6)}


def _spread_copies(srcs, lands, send_sems, recv_sems, local_sems, kind, waiting):
    x, y, c = lax.axis_index("x"), lax.axis_index("y"), lax.axis_index("c")
    me = 4 * x + 2 * y + c

    def peer(bits):
        dev = (1 - x if bits & 4 else x, 1 - y if bits & 2 else y, 1 - c if bits & 1 else c)
        return dev, 4 * dev[0] + 2 * dev[1] + dev[2]

    plan = PEERS[kind]
    remote, local = [], []
    for a, l in enumerate(lands):
        for d, bits in enumerate(plan):
            dev, pid = peer(bits)
            if kind == "forward":
                src, dst, dev = l.at[pid], l.at[peer(bits | 1)[1] if waiting else pid], peer(1)[0]
            else:
                src, dst = (srcs[a].at[pid] if kind == "scatter" else srcs[a]), l.at[pid if waiting else me]
            remote.append(pltpu.make_async_remote_copy(
                src_ref=src, dst_ref=dst, send_sem=send_sems.at[a * len(plan) + d], recv_sem=recv_sems.at[a * len(plan) + d],
                device_id=dev, device_id_type=MESH_ID))
        if kind != "forward":
            local.append(pltpu.make_async_copy(srcs[a].at[me] if kind == "scatter" else srcs[a], l.at[me], local_sems.at[a]))
    return remote, local


def _spread_start(srcs, kind, name, after=None, lands=None):
    if kind == "forward":
        srcs = []
    else:
        shapes = [a.shape if kind == "scatter" else (N_DEV,) + a.shape for a in srcs]
        lands = [lax.empty(shp, a.dtype) for shp, a in zip(shapes, srcs)]
    ns, nl, per = len(srcs), len(lands), len(PEERS[kind])
    extra = [] if after is None else [after]
    sem_shapes = [pltpu.SemaphoreType.DMA((nl * per,))] * 2 + ([pltpu.SemaphoreType.DMA((nl,))] if ns else [])

    def body(*refs):
        src_refs, land_refs = refs[:ns], refs[ns:ns + nl]
        sems = refs[ns + nl + len(extra):ns + nl + len(extra) + len(sem_shapes)]
        remote, local = _spread_copies(src_refs, land_refs, sems[0], sems[1], sems[2] if ns else None, kind, False)
        for cp in remote + local:
            cp.start()
        refs[-1][...] = jnp.zeros((8, 128), F32)

    outs = pl.pallas_call(
        body, name=name,
        out_shape=(*sem_shapes, *[pltpu.HBM(a.shape, a.dtype) for a in srcs + lands], jax.ShapeDtypeStruct((8, 128), F32)),
        in_specs=[HBM] * (ns + nl) + [pl.BlockSpec(memory_space=pl.ANY)] * len(extra),
        out_specs=(*[SEM] * len(sem_shapes), *[HBM] * (ns + nl), pl.BlockSpec(memory_space=pltpu.VMEM)),
        input_output_aliases={i: len(sem_shapes) + i for i in range(ns + nl)},
        compiler_params=pltpu.CompilerParams(has_side_effects=EFFECT),
    )(*[pltpu.with_memory_space_constraint(a, pltpu.HBM) for a in srcs + lands], *extra)
    k = len(sem_shapes)
    return outs[:k], list(outs[k:k + ns]), list(outs[k + ns:k + ns + nl]), outs[-1]


def _spread_wait(sems, srcs, lands, after, kind, name):
    ns, nl = len(srcs), len(lands)
    after = list(after) if isinstance(after, (list, tuple)) else [after]

    def body(*refs):
        src_refs, land_refs = refs[:ns], refs[ns:ns + nl]
        s = refs[ns + nl:ns + nl + len(sems)]
        remote, local = _spread_copies(src_refs, land_refs, s[0], s[1], s[2] if ns else None, kind, True)
        for cp in remote:
            cp.wait_send()
            cp.wait_recv()
        for cp in local:
            cp.wait()

    outs = pl.pallas_call(
        body, name=name, out_shape=tuple(pltpu.HBM(a.shape, a.dtype) for a in srcs + lands),
        in_specs=[HBM] * (ns + nl) + [SEM] * len(sems) + [pl.BlockSpec(memory_space=pl.ANY)] * len(after), out_specs=tuple([HBM] * (ns + nl)),
        input_output_aliases={i: i for i in range(ns + nl)}, compiler_params=pltpu.CompilerParams(has_side_effects=EFFECT),
    )(*srcs, *lands, *sems, *after)
    return list(outs[ns:])


SMALL = ("rel_bias", "norm1_g", "sgu_w", "sgu_b", "dil_qn_g", "dil_kn_g", "conv_w", "conv_b", "conv_ln_g", "conv_ln_b",
         "gqa_qn_g", "gqa_kn_g", "mix_norm_g", "norm2_g")
LARGE = ("w_in", "w_out", "w_gate", "w_up", "w_down")
EARLY = tuple(k for k in SMALL if k != "norm1_g")


def _local_step(x, target, p, B, S, fetch, emit, mid, early):
    T = B * S
    rope = _rope_tables(S)
    win = _bias_windows(p["rel_bias"], S)
    tile8 = lambda g: jnp.tile(g.reshape(1, HEAD_DIM), (1, N_HEADS))
    cols_b = (COL_BQ, COL_BK, COL_BV)
    cols_d = (COL_DQ, COL_DK128, COL_DV128)
    saved = []
    for l in range(DEPTH):
        s = {"x": x}
        s["ws"] = p["sgu_w"][l].astype(BF16)
        s["bias"] = jnp.repeat(p["sgu_b"][l].T, HEAD_DIM, axis=1)
        s["h"] = _rms_fwd(x, p["norm1_g"][l], f"rms1_fwd_{l}") if l == 0 else h_next
        s["win"] = fetch(l, "in", s["h"])
        s["cw"] = jnp.pad(s["win"]["conv_w"], ((0, 1), (0, 0))).reshape(32, 1, 512)
        z = s["z"] = _matmul(s["h"], s["win"]["w_in"], "nt", f"in_proj_{l}", tk=D_MODEL)
        s["bias"] = s["bias"] + mid(l, z)
        s["ya"] = _sgu_fwd(z, s["ws"], s["bias"], f"sgu_fwd_{l}")
        s["c"] = _conv_fwd1(z, s["cw"], p["conv_b"][l].reshape(1, 512), B, S, f"conv_fwd_{l}")
        s["yc"] = _conv_fwd2(s["c"], p["conv_ln_g"][l].reshape(1, 512), p["conv_ln_b"][l].reshape(1, 512), f"conv_ln_fwd_{l}")
        s["gb"] = (tile8(p["dil_qn_g"][l]), tile8(p["dil_kn_g"][l]))
        s["gd"] = (tile8(p["gqa_qn_g"][l]), tile8(p["gqa_kn_g"][l])[:, :KV_WIDTH])
        s["qkv_b"] = _prep_fwd(z, *s["gb"], None, B, S, N_HEADS, cols_b, f"prep_b_fwd_{l}")
        s["qkv_d"] = _prep_fwd(z, *s["gd"], rope, B, S, KV_HEADS, cols_d, f"prep_d_fwd_{l}")
        s["ob"] = _attn_fwd(*s["qkv_b"], win, f"attn_b_fwd_{l}")
        s["od"] = _attn_fwd(*s["qkv_d"], None, f"attn_d_fwd_{l}", tq=GQA_TQ)
        s["gmix"] = p["mix_norm_g"][l].reshape(1, 2048)
        s["ycat"] = _mix_fwd(s["ya"], s["ob"], s["yc"], s["od"], s["gmix"], B, S, f"mix_fwd_{l}")
        s["wout"] = fetch(l, "out", s["ycat"])["w_out"]
        x1, s["h2"] = _matmul(s["ycat"], s["wout"], "nn", f"out_proj_{l}", res=x, tk=D_MODEL, norm=p["norm2_g"][l])
        s["x1"] = x1
        s["ffn"] = fetch(l, "ffn", s["h2"])
        s["gate"] = _mm_shard_out(s["h2"], s["ffn"]["w_gate"], "nt", f"ffn_gate_{l}", out_dtype=BF16, tm=1024)
        s["up"] = _mm_shard_out(s["h2"], s["ffn"]["w_up"], "nt", f"ffn_up_{l}", out_dtype=BF16, tm=1024)
        if l + 1 < DEPTH:
            x, s["act"], h_next = _ffn_down(s["gate"], s["up"], s["ffn"]["w_down"], x1, f"ffn_down_{l}", norm=p["norm1_g"][l + 1])
        else:
            dx, s["act"], loss_blk = _ffn_down(s["gate"], s["up"], s["ffn"]["w_down"], x1, f"ffn_down_{l}", target=target)
        saved.append(s)

    g = {k: [None] * DEPTH for k in SMALL if k != "rel_bias"}
    dwin_total = None
    for l in reversed(range(DEPTH)):
        s = saved[l]
        z, ffn = s["z"], s["ffn"]
        dgate, dup = _ffn_down_dx(dx, ffn["w_down"], s["gate"], s["up"], f"ffn_down_dx_{l}")
        tok = emit(l, "w_down", _mm_shard_m(s["act"], dx, f"ffn_down_dw_{l}", out_dtype=BF16, tm=FFN_BLOCK, tn=512, tk=T))
        tok += emit(l, "w_gate", _mm_shard_m(dgate, s["h2"], f"ffn_gate_dw_{l}", out_dtype=BF16, tm=FFN_BLOCK, tn=512, tk=T))
        tok += emit(l, "w_up", _mm_shard_m(dup, s["h2"], f"ffn_up_dw_{l}", out_dtype=BF16, tm=FFN_BLOCK, tn=512, tk=T))
        dh2 = _mm_shard_k([(dgate, ffn["w_gate"]), (dup, ffn["w_up"])], "nn", f"ffn_up_dx_{l}", tn=512, fold=FFN_GROUPS)
        dx1, dg2 = _rms_bwd(dh2, s["x1"], p["norm2_g"][l] + tok, dx, f"rms2_bwd_{l}")
        g["norm2_g"][l] = dg2[0]
        dycat = _matmul(dx1, s["wout"], "nt", f"out_proj_dx_{l}", tk=D_MODEL)
        tok = emit(l, "w_out", _matmul(s["ycat"], dx1, "tn", f"out_proj_dw_{l}", out_dtype=BF16, tn=1024, tk=T))
        dya, dob, dyc, dod, dgm = _mix_bwd(s["ya"], s["ob"], s["yc"], s["od"], dycat, s["gmix"] + tok, B, S, f"mix_bwd_{l}")
        g["mix_norm_g"][l] = dgm[0]
        dz_a, dws, dbias = _sgu_bwd(z, dya, s["ws"], jnp.swapaxes(s["ws"], 1, 2), s["bias"], f"sgu_bwd_{l}")
        g["sgu_w"][l] = dws
        g["sgu_b"][l] = dbias.reshape(128, 8, HEAD_DIM).sum(-1).T
        dc, dlg, dlb, dcb = _conv_bwd1(s["c"], dyc, p["conv_ln_g"][l].reshape(1, 512), p["conv_ln_b"][l].reshape(1, 512), f"conv_ln_bwd_{l}")
        g["conv_ln_g"][l], g["conv_ln_b"][l], g["conv_b"][l] = dlg[0], dlb[0], dcb[0]
        dz_ca, dz_cg, dcw = _conv_bwd2(z, dc, s["cw"], B, S, f"conv_bwd_{l}")
        g["conv_w"][l] = dcw.reshape(32, 512)[:CONV_WIDTH]
        dq, dk, dv, dwin = _attn_bwd(*s["qkv_b"], s["ob"], dob, win, f"attn_b_bwd_{l}")
        dwin_total = dwin if dwin_total is None else dwin_total + dwin
        dz_b, dgq, dgk = _prep_bwd(z, dq, dk, dv, *s["gb"], None, B, S, N_HEADS, cols_b, f"prep_b_bwd_{l}")
        g["dil_qn_g"][l] = dgq.reshape(N_HEADS, HEAD_DIM).sum(0)
        g["dil_kn_g"][l] = dgk.reshape(N_HEADS, HEAD_DIM).sum(0)
        dq, dk, dv = _attn_bwd(*s["qkv_d"], s["od"], dod, None, f"attn_d_bwd_{l}", tq=GQA_TQ)
        dz_d, dgq, dgk = _prep_bwd(z, dq, dk, dv, *s["gd"], rope, B, S, KV_HEADS, cols_d, f"prep_d_bwd_{l}")
        g["gqa_qn_g"][l] = dgq.reshape(N_HEADS, HEAD_DIM).sum(0)
        g["gqa_kn_g"][l] = dgk.reshape(KV_HEADS, HEAD_DIM).sum(0)
        dz = [dz_a, dz_b, dz_ca, dz_cg, dz_d]
        tok = jnp.zeros((), F32)
        if l == 0:
            done = {k: jnp.stack(v) for k, v in g.items() if k != "norm1_g"}
            done["rel_bias"] = _bias_fold(dwin_total, S, "bias_fold")
            tok = early(done)
        tok += emit(l, "w_in", _in_proj_dw(dz, s["h"], f"in_proj_dw_{l}"))
        dh = _in_proj_dx(dz, s["win"]["w_in"], f"in_proj_dx_{l}")
        dx, dg1 = _rms_bwd(dh, s["x"], p["norm1_g"][l] + tok, dx1, f"rms1_bwd_{l}")
        g["norm1_g"][l] = dg1[0]

    return loss_blk[0, 0], dx, jnp.stack(g["norm1_g"])


GROUPS = {"in": ("w_in",), "out": ("w_out",), "ffn": ("w_gate", "w_up", "w_down")}
COL_SHARDED = ("w_in", "w_gate", "w_up")


def kernel(x, rel_bias, norm1_g, w_in, sgu_w, sgu_b, dil_qn_g, dil_kn_g, conv_w, conv_b, conv_ln_g, conv_ln_b, gqa_qn_g, gqa_kn_g, mix_norm_g, w_out, norm2_g, w_gate, w_up, w_down, loss_target, m_rel_bias, m_norm1_g, m_w_in, m_sgu_w, m_sgu_b, m_dil_qn_g, m_dil_kn_g, m_conv_w, m_conv_b, m_conv_ln_g, m_conv_ln_b, m_gqa_qn_g, m_gqa_kn_g, m_mix_norm_g, m_w_out, m_norm2_g, m_w_gate, m_w_up, m_w_down, v_rel_bias, v_norm1_g, v_w_in, v_sgu_w, v_sgu_b, v_dil_qn_g, v_dil_kn_g, v_conv_w, v_conv_b, v_conv_ln_g, v_conv_ln_b, v_gqa_qn_g, v_gqa_kn_g, v_mix_norm_g, v_w_out, v_norm2_g, v_w_gate, v_w_up, v_w_down):
    w = dict(rel_bias=rel_bias, norm1_g=norm1_g, w_in=w_in, sgu_w=sgu_w, sgu_b=sgu_b, dil_qn_g=dil_qn_g, dil_kn_g=dil_kn_g, conv_w=conv_w,
             conv_b=conv_b, conv_ln_g=conv_ln_g, conv_ln_b=conv_ln_b, gqa_qn_g=gqa_qn_g, gqa_kn_g=gqa_kn_g, mix_norm_g=mix_norm_g,
             w_out=w_out, norm2_g=norm2_g, w_gate=w_gate, w_up=w_up, w_down=w_down)
    m = dict(rel_bias=m_rel_bias, norm1_g=m_norm1_g, w_in=m_w_in, sgu_w=m_sgu_w, sgu_b=m_sgu_b, dil_qn_g=m_dil_qn_g, dil_kn_g=m_dil_kn_g,
             conv_w=m_conv_w, conv_b=m_conv_b, conv_ln_g=m_conv_ln_g, conv_ln_b=m_conv_ln_b, gqa_qn_g=m_gqa_qn_g, gqa_kn_g=m_gqa_kn_g,
             mix_norm_g=m_mix_norm_g, w_out=m_w_out, norm2_g=m_norm2_g, w_gate=m_w_gate, w_up=m_w_up, w_down=m_w_down)
    v = dict(rel_bias=v_rel_bias, norm1_g=v_norm1_g, w_in=v_w_in, sgu_w=v_sgu_w, sgu_b=v_sgu_b, dil_qn_g=v_dil_qn_g, dil_kn_g=v_dil_kn_g,
             conv_w=v_conv_w, conv_b=v_conv_b, conv_ln_g=v_conv_ln_g, conv_ln_b=v_conv_ln_b, gqa_qn_g=v_gqa_qn_g, gqa_kn_g=v_gqa_kn_g,
             mix_norm_g=v_mix_norm_g, w_out=v_w_out, norm2_g=v_norm2_g, w_gate=v_w_gate, w_up=v_w_up, w_down=v_w_down)
    names = list(w)
    B, S, D = x.shape
    T = B * S
    me = 4 * lax.axis_index("x") + 2 * lax.axis_index("y") + lax.axis_index("c")

    view = lambda a, k: jnp.swapaxes(a, 1, 2) if k in COL_SHARDED else a
    bf = {k: view(w[k], k).astype(BF16) for k in LARGE}
    spreads, forwards = {}, {}

    def start_gather(l, group, after=None):
        srcs = [bf[k][l] for k in GROUPS[group]] + ([conv_w[l]] if group == "in" else [])
        spreads[l, group] = _spread_start(srcs, "chips", f"gather_{group}_{l}_start", after)
        return spreads[l, group][3][0, 0]

    def forward(l, group, after):
        sems, srcs, lands, _ = spreads[l, group]
        lands = _spread_wait(sems, srcs, lands, after, "chips", f"gather_{group}_{l}_wait")
        forwards[l, group] = _spread_start(None, "forward", f"forward_{group}_{l}_start", lands=lands)
        return forwards[l, group][3]

    def landed(l, group, after):
        sems, _, lands, _ = forwards[l, group]
        return _spread_wait(sems, [], lands, after, "forward", f"forward_{group}_{l}_wait")

    tok0 = start_gather(0, "in") + start_gather(0, "out") + start_gather(0, "ffn")
    small = {k: w[k] for k in SMALL}
    small["norm1_g"] = norm1_g.at[0].add(tok0)

    def mid(l, z):
        if l > 0:
            return jnp.zeros((), F32)
        return start_gather(1, "in", z) + start_gather(1, "out", z) + start_gather(1, "ffn", z)

    def fetch(l, group, after):
        if group == "in":
            tok = forward(0, "in", after) if l == 0 else after
        elif group == "out":
            tok = forward(l, "ffn", [after, forward(l, "out", after)])
        else:
            tok = forward(1, "in", after) if l == 0 else after
        got = dict(zip(GROUPS[group] + ("conv_w",), landed(l, group, [after, tok])))
        if group == "in":
            got["w_in"] = got["w_in"].reshape(IN_WIDTH, D)
            got["conv_w"] = jnp.transpose(got["conv_w"], (1, 0, 2)).reshape(CONV_WIDTH, 512)
        if group == "out":
            got["w_out"] = got["w_out"].reshape(D, D)
        if group == "ffn":
            got = {k: a.reshape(FFN_GROUPS, -1, D) for k, a in got.items()}
        return got

    scatters = {}

    def emit(l, k, dw):
        dw = dw.reshape(N_DEV, -1, D)
        scatters[l, k] = _spread_start([dw], "scatter", f"scatter_{k}_{l}_start")
        return scatters[l, k][3][0, 0]

    flat2 = lambda a: a.reshape(-1, a.shape[-1])
    small_spread = []

    def early(done):
        small_spread.append(_spread_start([flat2(done[k]) for k in EARLY], "gather", "gather_small_grads_start"))
        return small_spread[0][3][0, 0]

    loss_part, dx, dnorm1 = _local_step(x.reshape(T, D), loss_target.reshape(T, D), small, B, S, fetch, emit, mid, early)
    loss = lax.psum(loss_part, ("x", "y", "c"))

    out_g, out_d, out_m, out_v = {}, {}, {}, {}

    def update_large(k, after):
        shp = view(w[k], k).shape
        two_d = lambda a: view(a, k).reshape(-1, shp[-1])
        res = None
        for l in reversed(range(DEPTH)):
            sems, srcs, lands, _ = scatters[l, k]
            stack = _spread_wait(sems, srcs, lands, after, "scatter", f"scatter_{k}_{l}_wait")[0]
            res = _adamw(two_d(w[k]), two_d(m[k]), two_d(v[k]), stack.reshape(N_DEV, -1, shp[-1]), f"adamw_{k}_{l}", layer=l, prev=res)
        out_g[k], out_d[k], out_m[k], out_v[k] = [view(a.reshape(shp), k) for a in res]
        return res[0]

    late_sems, late_srcs, late_lands, late_tok = _spread_start([flat2(dnorm1)], "gather", "gather_norm1_grad_start")
    after = [dx, late_tok]
    for k in ("w_down", "w_gate", "w_up", "w_out"):
        after = update_large(k, after)
    sems, srcs, lands, _ = small_spread[0]
    stacks = dict(zip(EARLY, _spread_wait(sems, srcs, lands, after, "gather", "gather_small_grads_wait")))
    stacks["norm1_g"] = _spread_wait(late_sems, late_srcs, late_lands, after, "gather", "gather_norm1_grad_wait")[0]
    for k in SMALL:
        stack = stacks[k]
        if k == "conv_w":
            stack = lax.dynamic_slice_in_dim(stack, me * (512 // N_DEV), 512 // N_DEV, axis=2)
        res = _adamw(flat2(w[k]), flat2(m[k]), flat2(v[k]), stack, f"adamw_{k}")
        out_g[k], out_d[k], out_m[k], out_v[k] = [a.reshape(w[k].shape) for a in res]
        after = res[0]
    update_large("w_in", after)

    return (loss, dx.reshape(B, S, D), *[out_g[k] for k in names], *[out_d[k] for k in names],
            *[out_m[k] for k in names], *[out_v[k] for k in names])
```

```python
import functools
import math

import numpy as np
import jax
import jax.numpy as jnp
from jax import lax
from jax.experimental import pallas as pl
from jax.experimental.pallas import tpu as pltpu

F32 = jnp.float32
BF16 = jnp.bfloat16
HIGHEST = lax.Precision.HIGHEST
MESH_ID = pl.DeviceIdType.MESH

D_MODEL = 2048
DEPTH = 2
HEAD_DIM = 64
GROUP_WIDTH = 512
N_HEADS = 8
KV_HEADS = 2
KV_WIDTH = 128
SGU_CHUNK = 128
CONV_WIDTH = 31
CONV_PAD = 16
GRID_W = 64
ROPE_THETA = 10000.0
REL_BUCKETS = 32
REL_MAX_DIST = 1024
DIL_PATTERNS = ((128, 1), (512, 4), (2048, 16))
FFN_HIDDEN = 5632
IN_WIDTH = 4352
RMS_EPS = 1e-6
LN_EPS = 1e-5
MASKED = -1e30
N_DEV = 8

ADAM_LR = 0.001
ADAM_B1 = 0.9
ADAM_B2 = 0.999
ADAM_EPS = 1e-08
ADAM_WD = 0.01
ADAM_STEP = 10

COL_AU, COL_AV, COL_BQ, COL_BK, COL_BV, COL_CA, COL_CG, COL_DQ = range(8)
COL_DK128, COL_DV128 = 32, 33

VMEM_LIMIT = 56 * 1024 * 1024
FFN_GROUPS = 2
FFN_BLOCK = 1408
ATTN_TQ = 256
GQA_TQ = 512


def _params(sem=None, vmem=VMEM_LIMIT):
    return pltpu.CompilerParams(dimension_semantics=sem, vmem_limit_bytes=vmem)


def _dot(a, b, dims, precision=None):
    return lax.dot_general(a, b, (dims, ((), ())), precision=precision, preferred_element_type=F32)


def _nn(a, b, precision=None):
    return _dot(a, b, ((1,), (0,)), precision)


def _nt(a, b):
    return _dot(a, b, ((1,), (1,)))


def _tn(a, b):
    return _dot(a, b, ((0,), (0,)))


DIMS = {"nn": ((1,), (0,)), "nt": ((1,), (1,)), "tn": ((0,), (0,))}


def _pick(n, cands):
    for c in cands:
        if n % c == 0:
            return c
    return n


def _mm_call(name, mode, pairs, specs, o_spec, out_sds, grid, acc_shape, res=None, fold=None, norm=None):
    npair, nk, dims = len(pairs), grid[2], DIMS[mode]

    def body(*refs):
        ab = refs[:2 * npair]
        at = 2 * npair
        r_ref = refs[at] if res is not None else None
        at += res is not None
        g_ref = refs[at] if norm is not None else None
        at += norm is not None
        o_ref = refs[at]
        h_ref = refs[at + 1] if norm is not None else None
        part = None
        for t in range(npair):
            for s in ([None] if fold is None else range(fold)):
                a_blk = ab[2 * t][...] if s is None else ab[2 * t][s]
                b_blk = ab[2 * t + 1][...] if s is None else ab[2 * t + 1][s]
                d = _dot(a_blk.astype(BF16), b_blk.astype(BF16), dims)
                part = d if part is None else part + d

        def finish(r):
            if r_ref is not None:
                r = r + r_ref[...]
            o_ref[...] = r.astype(o_ref.dtype)
            if h_ref is not None:
                h_ref[...] = _rms_rows(r, g_ref[...])

        if nk == 1:
            finish(part)
            return
        acc, k = refs[-1], pl.program_id(2)

        @pl.when(k == 0)
        def _():
            acc[...] = part

        @pl.when(k > 0)
        def _():
            acc[...] += part

        @pl.when(k == nk - 1)
        def _():
            finish(acc[...])

    ins = [t for pair in pairs for t in pair]
    in_specs = [t for pair in specs for t in pair]
    if res is not None:
        ins.append(res)
        in_specs.append(o_spec)
    out_specs = o_spec
    if norm is not None:
        ins.append(norm)
        in_specs.append(pl.BlockSpec(norm.shape, lambda *_: (0, 0)))
        out_specs, out_sds = [o_spec, o_spec], [out_sds, jax.ShapeDtypeStruct(out_sds.shape, BF16)]
    return pl.pallas_call(
        body, name=name, grid=grid, in_specs=in_specs, out_specs=out_specs, out_shape=out_sds,
        scratch_shapes=[pltpu.VMEM(acc_shape, F32)] if nk > 1 else [],
        compiler_params=_params(("parallel", "parallel", "arbitrary")),
    )(*ins)


def _rms_rows(x, g):
    return (x * lax.rsqrt(jnp.mean(x * x, axis=-1, keepdims=True) + RMS_EPS) * g).astype(BF16)


def _matmul(a, b, mode, name, res=None, out_dtype=F32, tm=512, tn=None, tk=None, norm=None):
    if mode == "nn":
        (M, K), N = a.shape, b.shape[1]
    elif mode == "nt":
        (M, K), N = a.shape, b.shape[0]
    else:
        (K, M), N = a.shape, b.shape[1]
    tm = min(tm, M)
    tn = tn or _pick(N, (2176, 2048, 1408, 1024, 512))
    tk = tk or _pick(K, (1024, 2176, 1408, 512))
    assert M % tm == 0 and N % tn == 0 and K % tk == 0, (M, N, K, tm, tn, tk)
    a_spec = pl.BlockSpec((tk, tm), lambda i, j, k: (k, i)) if mode == "tn" else pl.BlockSpec((tm, tk), lambda i, j, k: (i, k))
    b_spec = pl.BlockSpec((tn, tk), lambda i, j, k: (j, k)) if mode == "nt" else pl.BlockSpec((tk, tn), lambda i, j, k: (k, j))
    o_spec = pl.BlockSpec((tm, tn), lambda i, j, k: (i, j))
    assert norm is None or tn == N
    return _mm_call(name, mode, [(a, b)], [(a_spec, b_spec)], o_spec, jax.ShapeDtypeStruct((M, N), out_dtype),
                    (M // tm, N // tn, K // tk), (tm, tn), res, norm=None if norm is None else norm.reshape(1, N))


def _mm_shard_out(a, bs, mode, name, out_dtype=F32, tm=512, tk=None):
    J = bs.shape[0]
    n = bs.shape[1] if mode == "nt" else bs.shape[2]
    (K, M) = a.shape if mode == "tn" else a.shape[::-1]
    tm = min(tm, M)
    tk = tk or (K if mode != "tn" else _pick(K, (1024, 512)))
    a_spec = pl.BlockSpec((tk, tm), lambda j, i, k: (k, i)) if mode == "tn" else pl.BlockSpec((tm, tk), lambda j, i, k: (i, k))
    b_spec = pl.BlockSpec((None, n, tk), lambda j, i, k: (j, 0, k)) if mode == "nt" else pl.BlockSpec((None, tk, n), lambda j, i, k: (j, k, 0))
    o_spec = pl.BlockSpec((None, tm, n), lambda j, i, k: (j, i, 0))
    return _mm_call(name, mode, [(a, bs)], [(a_spec, b_spec)], o_spec, jax.ShapeDtypeStruct((J, M, n), out_dtype),
                    (J, M // tm, K // tk), (tm, n))


def _mm_shard_k(pairs, mode, name, res=None, out_dtype=F32, tm=512, tn=None, fold=1):
    J, M, n = pairs[0][0].shape
    N = pairs[0][1].shape[2] if mode == "nn" else pairs[0][1].shape[1]
    tm = min(tm, M)
    tn = tn or _pick(N, (2048, 1024, 512))
    a_spec = pl.BlockSpec((fold, tm, n), lambda i, j, k: (k, i, 0))
    b_spec = pl.BlockSpec((fold, n, tn), lambda i, j, k: (k, 0, j)) if mode == "nn" else pl.BlockSpec((fold, tn, n), lambda i, j, k: (k, j, 0))
    o_spec = pl.BlockSpec((tm, tn), lambda i, j, k: (i, j))
    return _mm_call(name, mode, pairs, [(a_spec, b_spec)] * len(pairs), o_spec, jax.ShapeDtypeStruct((M, N), out_dtype),
                    (M // tm, N // tn, J // fold), (tm, tn), res, fold)


def _mm_shard_m(as_, b, name, out_dtype=F32, tm=None, tn=None, tk=512):
    J, K, n = as_.shape
    N = b.shape[1]
    tm = tm or n
    tn = tn or _pick(N, (2048, 1024, 512))
    tk = min(tk, K)
    nn = N // tn
    a_spec = pl.BlockSpec((None, tk, tm), lambda j, i, k: (j, k, i // nn))
    b_spec = pl.BlockSpec((tk, tn), lambda j, i, k: (k, i % nn))
    o_spec = pl.BlockSpec((None, tm, tn), lambda j, i, k: (j, i // nn, i % nn))
    return _mm_call(name, "tn", [(as_, b)], [(a_spec, b_spec)], o_spec, jax.ShapeDtypeStruct((J, n, N), out_dtype),
                    (J, (n // tm) * nn, K // tk), (tm, tn))


def _out_proj_dx(dh, x, g, dres, w, name):
    T, D = x.shape
    N = w.shape[0]
    tm = min(256, T)

    def body(dh_ref, x_ref, g_ref, dres_ref, w_ref, o_ref, dx_ref, dg_ref):
        @pl.when(pl.program_id(0) == 0)
        def _():
            dg_ref[...] = jnp.zeros_like(dg_ref)

        xv, dhv = x_ref[...], dh_ref[...]
        r = lax.rsqrt(jnp.mean(xv * xv, axis=-1, keepdims=True) + RMS_EPS)
        y = xv * r
        dy = dhv * g_ref[...]
        dx = dres_ref[...] + r * (dy - y * jnp.mean(dy * y, axis=-1, keepdims=True))
        dx_ref[...] = dx
        dg_ref[...] += jnp.sum(dhv * y, axis=0, keepdims=True)
        o_ref[...] = _nt(dx.astype(BF16), w_ref[...])

    row = pl.BlockSpec((tm, D), lambda i: (i, 0))
    vec = pl.BlockSpec((1, D), lambda i: (0, 0))
    return pl.pallas_call(
        body, name=name, grid=(T // tm,), in_specs=[row, row, vec, row, pl.BlockSpec((N, D), lambda i: (0, 0))],
        out_specs=[pl.BlockSpec((tm, N), lambda i: (i, 0)), row, vec],
        out_shape=[jax.ShapeDtypeStruct((T, N), F32), jax.ShapeDtypeStruct((T, D), F32), jax.ShapeDtypeStruct((1, D), F32)],
        compiler_params=_params(("arbitrary",)),
    )(dh, x, g.reshape(1, D), dres, w)


def _in_proj_dw(pieces, h, name):
    T, D = h.shape
    tm = 256
    nbs = [p.shape[1] // tm for p in pieces]
    los = [sum(nbs[:t]) for t in range(len(pieces))]

    def body(*refs):
        h_ref, o_ref = refs[-2:]
        i = pl.program_id(0)
        for p_ref, lo, nb in zip(refs[:-2], los, nbs):
            @pl.when((i >= lo) & (i < lo + nb))
            def _():
                o_ref[...] = _tn(p_ref[...], h_ref[...]).astype(BF16)

    specs = [pl.BlockSpec((T, tm), (lambda lo, nb: lambda i: (0, jnp.clip(i - lo, 0, nb - 1)))(lo, nb)) for lo, nb in zip(los, nbs)]
    return pl.pallas_call(body, name=name, grid=(sum(nbs),), in_specs=specs + [pl.BlockSpec((T, D), lambda i: (0, 0))],
                          out_specs=pl.BlockSpec((tm, D), lambda i: (i, 0)), out_shape=jax.ShapeDtypeStruct((sum(nbs) * tm, D), BF16),
                          compiler_params=_params(("parallel",)))(*pieces, h)


def _in_proj_dx(pieces, w, name, tm=512, tn=1024):
    T = pieces[0].shape[0]
    K, D = w.shape
    tm = min(tm, T)
    widths = [p.shape[1] for p in pieces]
    offs = [sum(widths[:t]) for t in range(len(pieces))]

    def body(*refs):
        w_ref, o_ref = refs[-2:]
        acc = None
        for p_ref, off, wd in zip(refs[:-2], offs, widths):
            d = _nn(p_ref[...], w_ref[off:off + wd, :])
            acc = d if acc is None else acc + d
        o_ref[...] = acc

    specs = [pl.BlockSpec((tm, wd), lambda i, j: (i, 0)) for wd in widths]
    return pl.pallas_call(body, name=name, grid=(T // tm, D // tn), in_specs=specs + [pl.BlockSpec((K, tn), lambda i, j: (0, j))],
                          out_specs=pl.BlockSpec((tm, tn), lambda i, j: (i, j)), out_shape=jax.ShapeDtypeStruct((T, D), F32),
                          compiler_params=_params(("parallel", "parallel")))(*pieces, w)


def _seg_matrix(width):
    return jnp.asarray(np.kron(np.eye(width // HEAD_DIM, dtype=np.float32), np.full((HEAD_DIM, HEAD_DIM), 1.0 / HEAD_DIM, np.float32)), BF16)


def _segmean(v, p):
    hi = v.astype(BF16)
    r = v - hi.astype(F32)
    mid = r.astype(BF16)
    lo = (r - mid.astype(F32)).astype(BF16)
    w = min(256, v.shape[1])
    pw = p[:w, :w]
    halves = []
    for c in range(v.shape[1] // w):
        cols = slice(c * w, (c + 1) * w)
        halves.append(_nn(hi[:, cols], pw) + _nn(mid[:, cols], pw) + _nn(lo[:, cols], pw))
    return halves[0] if len(halves) == 1 else jnp.concatenate(halves, axis=1)


def _gelu(x):
    c0 = math.sqrt(2.0 / math.pi)
    t = jnp.tanh(c0 * (x + 0.044715 * x * x * x))
    return 0.5 * x * (1.0 + t), t


def _gelu_grad(x, t):
    c0 = math.sqrt(2.0 / math.pi)
    return 0.5 * (1.0 + t) + 0.5 * x * (1.0 - t * t) * c0 * (1.0 + 3.0 * 0.044715 * x * x)


def _sigmoid(x):
    return 1.0 / (1.0 + jnp.exp(-x))


def _rms_fwd(x, g, name):
    T, D = x.shape
    tm = min(256, T)

    def body(x_ref, g_ref, o_ref):
        o_ref[...] = _rms_rows(x_ref[...], g_ref[...])

    return pl.pallas_call(
        body, name=name, grid=(T // tm,),
        in_specs=[pl.BlockSpec((tm, D), lambda i: (i, 0)), pl.BlockSpec((1, D), lambda i: (0, 0))],
        out_specs=pl.BlockSpec((tm, D), lambda i: (i, 0)), out_shape=jax.ShapeDtypeStruct((T, D), BF16),
        compiler_params=_params(("parallel",)),
    )(x, g.reshape(1, D))


def _rms_bwd(dh, x, g, dres, name):
    T, D = x.shape
    tm = min(256, T)

    def body(dh_ref, x_ref, g_ref, dres_ref, dx_ref, dg_ref):
        @pl.when(pl.program_id(0) == 0)
        def _():
            dg_ref[...] = jnp.zeros_like(dg_ref)

        xv, dhv = x_ref[...], dh_ref[...]
        r = lax.rsqrt(jnp.mean(xv * xv, axis=-1, keepdims=True) + RMS_EPS)
        y = xv * r
        dy = dhv * g_ref[...]
        dx_ref[...] = dres_ref[...] + r * (dy - y * jnp.mean(dy * y, axis=-1, keepdims=True))
        dg_ref[...] += jnp.sum(dhv * y, axis=0, keepdims=True)

    row = pl.BlockSpec((tm, D), lambda i: (i, 0))
    vec = pl.BlockSpec((1, D), lambda i: (0, 0))
    return pl.pallas_call(
        body, name=name, grid=(T // tm,), in_specs=[row, row, vec, row], out_specs=[row, vec],
        out_shape=[jax.ShapeDtypeStruct((T, D), F32), jax.ShapeDtypeStruct((1, D), F32)],
        compiler_params=_params(("arbitrary",)),
    )(dh, x, g.reshape(1, D), dres)


def _sgu_core(zu, zv, ws_ref, bias, p):
    ug, tu = _gelu(zu)
    vg, tv = _gelu(zv)
    xc = vg - _segmean(vg, p)
    rs = lax.rsqrt(_segmean(xc * xc, p) + LN_EPS)
    vn = xc * rs
    vnb = vn.astype(BF16)
    low = lax.broadcasted_iota(jnp.int32, (SGU_CHUNK, 128), 1) < HEAD_DIM
    parts = []
    for j in range(4):
        vp = vnb[:, 128 * j:128 * (j + 1)]
        parts.append(jnp.where(low, _nn(ws_ref[2 * j], vp), _nn(ws_ref[2 * j + 1], vp)))
    mixed = jnp.concatenate(parts, axis=1) + bias
    return ug, tu, tv, rs, vn, vnb, mixed, low


def _sgu_fwd(z, ws, bias, name):
    T = z.shape[0]

    def body(zu_ref, zv_ref, ws_ref, b_ref, p_ref, y_ref):
        ug, _, _, _, _, _, mixed, _ = _sgu_core(zu_ref[...], zv_ref[...], ws_ref, b_ref[...], p_ref[...])
        y_ref[...] = ug * mixed

    full = lambda shape: pl.BlockSpec(shape, lambda i: (0,) * len(shape))
    return pl.pallas_call(
        body, name=name, grid=(T // SGU_CHUNK,),
        in_specs=[pl.BlockSpec((SGU_CHUNK, 512), lambda i: (i, COL_AU)), pl.BlockSpec((SGU_CHUNK, 512), lambda i: (i, COL_AV)),
                  full((8, 128, 128)), full((128, 512)), full((512, 512))],
        out_specs=pl.BlockSpec((SGU_CHUNK, 512), lambda i: (i, 0)), out_shape=jax.ShapeDtypeStruct((T, 512), F32),
        compiler_params=_params(("parallel",)),
    )(z, z, ws, bias, _seg_matrix(512))


def _sgu_bwd(z, dy, ws, ws_t, bias, name):
    T = z.shape[0]

    def body(zu_ref, zv_ref, dy_ref, ws_ref, wst_ref, b_ref, p_ref, dz_ref, dws_ref, db_ref):
        @pl.when(pl.program_id(0) == 0)
        def _():
            dws_ref[...] = jnp.zeros_like(dws_ref)
            db_ref[...] = jnp.zeros_like(db_ref)

        zu, zv, p = zu_ref[...], zv_ref[...], p_ref[...]
        ug, tu, tv, rs, vn, vnb, mixed, low = _sgu_core(zu, zv, ws_ref, b_ref[...], p)
        dyv = dy_ref[...]
        dmixed = dyv * ug
        db_ref[...] += dmixed
        dmb = dmixed.astype(BF16)
        zero = jnp.zeros((SGU_CHUNK, 128), BF16)
        parts = []
        for j in range(4):
            dmp, vp = dmb[:, 128 * j:128 * (j + 1)], vnb[:, 128 * j:128 * (j + 1)]
            dws_ref[2 * j] += _nt(jnp.where(low, dmp, zero), vp)
            dws_ref[2 * j + 1] += _nt(jnp.where(low, zero, dmp), vp)
            parts.append(jnp.where(low, _nn(wst_ref[2 * j], dmp), _nn(wst_ref[2 * j + 1], dmp)))
        dvn = jnp.concatenate(parts, axis=1)
        dvg = rs * (dvn - _segmean(dvn, p) - vn * _segmean(dvn * vn, p))
        dz_ref[:, 0:512] = (dyv * mixed * _gelu_grad(zu, tu)).astype(BF16)
        dz_ref[:, 512:1024] = (dvg * _gelu_grad(zv, tv)).astype(BF16)

    full = lambda shape: pl.BlockSpec(shape, lambda i: (0,) * len(shape))
    return pl.pallas_call(
        body, name=name, grid=(T // SGU_CHUNK,),
        in_specs=[pl.BlockSpec((SGU_CHUNK, 512), lambda i: (i, COL_AU)), pl.BlockSpec((SGU_CHUNK, 512), lambda i: (i, COL_AV)),
                  pl.BlockSpec((SGU_CHUNK, 512), lambda i: (i, 0)), full((8, 128, 128)), full((8, 128, 128)), full((128, 512)), full((512, 512))],
        out_specs=[pl.BlockSpec((SGU_CHUNK, 1024), lambda i: (i, 0)), full((8, 128, 128)), full((128, 512))],
        out_shape=[jax.ShapeDtypeStruct((T, 1024), BF16), jax.ShapeDtypeStruct((8, 128, 128), F32), jax.ShapeDtypeStruct((128, 512), F32)],
        compiler_params=_params(("arbitrary",)),
    )(z, z, dy, ws, ws_t, bias, _seg_matrix(512))


CONV_ROWS = 256


def _conv_taps(pad_ref, w_ref, base, flip):
    acc = None
    for k in range(CONV_WIDTH):
        wk = w_ref[CONV_WIDTH - 1 - k if flip else k]
        t = wk * pad_ref[base + k + 1:base + k + 1 + CONV_ROWS, :]
        acc = t if acc is None else acc + t
    return acc


def _conv_fwd1(z, w, cb, B, S, name):
    T = B * S
    rows = min(CONV_ROWS, S)
    assert rows == CONV_ROWS

    def body(a_ref, g_ref, w_ref, cb_ref, c_ref, pad):
        pad[0:CONV_PAD, :] = jnp.zeros((CONV_PAD, 128), F32)
        pad[CONV_PAD + S:2 * CONV_PAD + S, :] = jnp.zeros((CONV_PAD, 128), F32)
        pad[CONV_PAD:CONV_PAD + S, :] = a_ref[...] * _sigmoid(g_ref[...])

        for base in range(0, S, CONV_ROWS):
            c_ref[base:base + CONV_ROWS, :] = _conv_taps(pad, w_ref, base, False) + cb_ref[...]

    return pl.pallas_call(
        body, name=name, grid=(4, B),
        in_specs=[pl.BlockSpec((S, 128), lambda j, b: (b, 4 * COL_CA + j)), pl.BlockSpec((S, 128), lambda j, b: (b, 4 * COL_CG + j)),
                  pl.BlockSpec((32, 1, 128), lambda j, b: (0, 0, j)), pl.BlockSpec((1, 128), lambda j, b: (0, j))],
        out_specs=pl.BlockSpec((S, 128), lambda j, b: (b, j)), out_shape=jax.ShapeDtypeStruct((T, 512), F32),
        scratch_shapes=[pltpu.VMEM((S + 2 * CONV_PAD, 128), F32)], compiler_params=_params(("parallel", "parallel")),
    )(z, z, w, cb)


def _ln_rows(c):
    mu = jnp.mean(c, axis=-1, keepdims=True)
    xc = c - mu
    rs = lax.rsqrt(jnp.mean(xc * xc, axis=-1, keepdims=True) + LN_EPS)
    return xc * rs, rs


def _conv_bwd2(z, dc, w, B, S, name):
    T = B * S

    def body(a_ref, g_ref, dc_ref, w_ref, da_ref, dg_ref, dw_ref, hpad, dpad):
        @pl.when(pl.program_id(1) == 0)
        def _():
            dw_ref[...] = jnp.zeros_like(dw_ref)

        zeros = jnp.zeros((CONV_PAD, 128), F32)
        for ref in (hpad, dpad):
            ref[0:CONV_PAD, :] = zeros
            ref[CONV_PAD + S:2 * CONV_PAD + S, :] = zeros
        hpad[CONV_PAD:CONV_PAD + S, :] = a_ref[...] * _sigmoid(g_ref[...])
        dpad[CONV_PAD:CONV_PAD + S, :] = dc_ref[...]
        dws = [None] * CONV_WIDTH
        for base in range(0, S, CONV_ROWS):
            rows = slice(base, base + CONV_ROWS)
            dh = _conv_taps(dpad, w_ref, base, True)
            sg = _sigmoid(g_ref[rows, :])
            da_ref[rows, :] = (dh * sg).astype(BF16)
            dg_ref[rows, :] = (dh * a_ref[rows, :] * sg * (1.0 - sg)).astype(BF16)
            dcv = dc_ref[rows, :]
            for k in range(CONV_WIDTH):
                prod = dcv * hpad[base + k + 1:base + k + 1 + CONV_ROWS, :]
                part = jnp.sum(prod.reshape(CONV_ROWS // 8, 8, 128), axis=0)
                dws[k] = part if dws[k] is None else dws[k] + part
        for k in range(CONV_WIDTH):
            dw_ref[k] += jnp.sum(dws[k], axis=0, keepdims=True)

    return pl.pallas_call(
        body, name=name, grid=(4, B),
        in_specs=[pl.BlockSpec((S, 128), lambda j, b: (b, 4 * COL_CA + j)), pl.BlockSpec((S, 128), lambda j, b: (b, 4 * COL_CG + j)),
                  pl.BlockSpec((S, 128), lambda j, b: (b, j)), pl.BlockSpec((32, 1, 128), lambda j, b: (0, 0, j))],
        out_specs=[pl.BlockSpec((S, 128), lambda j, b: (b, j)), pl.BlockSpec((S, 128), lambda j, b: (b, j)),
                   pl.BlockSpec((32, 1, 128), lambda j, b: (0, 0, j))],
        out_shape=[jax.ShapeDtypeStruct((T, 512), BF16), jax.ShapeDtypeStruct((T, 512), BF16), jax.ShapeDtypeStruct((32, 1, 512), F32)],
        scratch_shapes=[pltpu.VMEM((S + 2 * CONV_PAD, 128), F32), pltpu.VMEM((S + 2 * CONV_PAD, 128), F32)],
        compiler_params=_params(("parallel", "arbitrary")),
    )(z, z, dc, w)


def _swap16(x):
    n = x.shape[1]
    first = (lax.broadcasted_iota(jnp.int32, x.shape, 1) % 32) < 16
    return jnp.where(first, pltpu.roll(x, n - 16, 1), pltpu.roll(x, 16, 1))


def _rope(x, cos, sin):
    return x * cos + _swap16(x) * sin


def _rope_t(dy, cos, sin):
    return dy * cos + _swap16(dy * sin)


def _qk_norm(x, p):
    r = lax.rsqrt(_segmean(x * x, p) + RMS_EPS)
    return x * r, r


def _store_heads(ref, val, n):
    for h in range(n):
        ref[h] = val[:, HEAD_DIM * h:HEAD_DIM * (h + 1)].astype(ref.dtype)


def _load_heads(ref, n):
    return jnp.concatenate([ref[h] for h in range(n)], axis=1)


def _prep_fwd(z, gq, gk, rope, B, S, kv_heads, cols, name):
    tm = min(256, S)
    ns = S // tm
    kw = kv_heads * HEAD_DIM
    scale = HEAD_DIM ** -0.5
    qc, kc, vc = cols

    def body(*refs):
        if rope is None:
            q_ref, k_ref, v_ref, gq_ref, gk_ref, p_ref, qo, ko, vo = refs
        else:
            q_ref, k_ref, v_ref, gq_ref, gk_ref, p_ref, cos_ref, sin_ref, qo, ko, vo = refs
        p = p_ref[...]
        qn, _ = _qk_norm(q_ref[...], p)
        kn, _ = _qk_norm(k_ref[...], p[:kw, :kw])
        qn, kn = qn * gq_ref[...], kn * gk_ref[...]
        if rope is not None:
            cos, sin = cos_ref[...], sin_ref[...]
            qn, kn = _rope(qn, cos, sin), _rope(kn, cos[:, :kw], sin[:, :kw])
        _store_heads(qo, qn * scale, N_HEADS)
        _store_heads(ko, kn, kv_heads)
        _store_heads(vo, v_ref[...], kv_heads)

    row = lambda w, c: pl.BlockSpec((tm, w), lambda b, i: (b * ns + i, c))
    const = lambda shape: pl.BlockSpec(shape, lambda b, i: (0,) * len(shape))
    heads = lambda n: pl.BlockSpec((None, n, tm, HEAD_DIM), lambda b, i: (b, 0, i, 0))
    ins = [z, z, z, gq, gk, _seg_matrix(512)]
    specs = [row(512, qc), row(kw, kc), row(kw, vc), const((1, 512)), const((1, kw)), const((512, 512))]
    if rope is not None:
        ins += list(rope)
        specs += [pl.BlockSpec((tm, 512), lambda b, i: (i, 0))] * 2
    return pl.pallas_call(
        body, name=name, grid=(B, ns), in_specs=specs, out_specs=[heads(N_HEADS), heads(kv_heads), heads(kv_heads)],
        out_shape=[jax.ShapeDtypeStruct((B, N_HEADS, S, HEAD_DIM), BF16), jax.ShapeDtypeStruct((B, kv_heads, S, HEAD_DIM), BF16),
                   jax.ShapeDtypeStruct((B, kv_heads, S, HEAD_DIM), BF16)],
        compiler_params=_params(("parallel", "parallel")),
    )(*ins)


def _prep_bwd(z, dq, dk, dv, gq, gk, rope, B, S, kv_heads, cols, name):
    T = B * S
    tm = min(256, S)
    ns = S // tm
    kw = kv_heads * HEAD_DIM
    scale = HEAD_DIM ** -0.5
    qc, kc, _ = cols

    def body(*refs):
        if rope is None:
            q_ref, k_ref, dq_ref, dk_ref, dv_ref, gq_ref, gk_ref, p_ref, dz_ref, dgq_ref, dgk_ref = refs
        else:
            q_ref, k_ref, dq_ref, dk_ref, dv_ref, gq_ref, gk_ref, p_ref, cos_ref, sin_ref, dz_ref, dgq_ref, dgk_ref = refs

        @pl.when((pl.program_id(0) == 0) & (pl.program_id(1) == 0))
        def _():
            dgq_ref[...] = jnp.zeros_like(dgq_ref)
            dgk_ref[...] = jnp.zeros_like(dgk_ref)

        p = p_ref[...]
        dqv = _load_heads(dq_ref, N_HEADS) * scale
        dkv = _load_heads(dk_ref, kv_heads)
        if rope is not None:
            cos, sin = cos_ref[...], sin_ref[...]
            dqv, dkv = _rope_t(dqv, cos, sin), _rope_t(dkv, cos[:, :kw], sin[:, :kw])

        def through_norm(xv, dy, g, pm, dg_ref):
            xh, r = _qk_norm(xv, pm)
            dg_ref[...] += jnp.sum(dy * xh, axis=0, keepdims=True)
            dxh = dy * g
            return r * (dxh - xh * _segmean(dxh * xh, pm))

        dz_ref[:, 0:512] = through_norm(q_ref[...], dqv, gq_ref[...], p, dgq_ref).astype(BF16)
        dz_ref[:, 512:512 + kw] = through_norm(k_ref[...], dkv, gk_ref[...], p[:kw, :kw], dgk_ref).astype(BF16)
        dz_ref[:, 512 + kw:512 + 2 * kw] = _load_heads(dv_ref, kv_heads).astype(BF16)

    row = lambda w, c: pl.BlockSpec((tm, w), lambda b, i: (b * ns + i, c))
    const = lambda shape: pl.BlockSpec(shape, lambda b, i: (0,) * len(shape))
    heads = lambda n: pl.BlockSpec((None, n, tm, HEAD_DIM), lambda b, i: (b, 0, i, 0))
    ins = [z, z, dq, dk, dv, gq, gk, _seg_matrix(512)]
    specs = [row(512, qc), row(kw, kc), heads(N_HEADS), heads(kv_heads), heads(kv_heads), const((1, 512)), const((1, kw)), const((512, 512))]
    if rope is not None:
        ins += list(rope)
        specs += [pl.BlockSpec((tm, 512), lambda b, i: (i, 0))] * 2
    return pl.pallas_call(
        body, name=name, grid=(B, ns), in_specs=specs, out_specs=[row(512 + 2 * kw, 0), const((1, 512)), const((1, kw))],
        out_shape=[jax.ShapeDtypeStruct((T, 512 + 2 * kw), BF16), jax.ShapeDtypeStruct((1, 512), F32), jax.ShapeDtypeStruct((1, kw), F32)],
        compiler_params=_params(("arbitrary", "arbitrary")),
    )(*ins)


def _toeplitz(win, tq, S):
    r = pltpu.roll(jnp.broadcast_to(win, (tq, S + tq)), 0, 1, stride=1, stride_axis=0)
    return r[:, tq:tq + S]


ATTN_HEADS = 4


def _attn_fwd(q, k, v, win, name, nh=ATTN_HEADS, tq=ATTN_TQ):
    B, H, S, _ = q.shape
    shared = k.shape[1] != H
    assert not shared or H // k.shape[1] == nh
    tq = min(tq, S)

    def body(*refs):
        if win is None:
            q_ref, k_ref, v_ref, o_ref = refs
        else:
            q_ref, k_ref, v_ref, w_ref, o_ref = refs
        kvs = [(k_ref[...], v_ref[...])] * nh if shared else [(k_ref[h], v_ref[h]) for h in range(nh)]
        scores = []
        for h in range(nh):
            s = _nt(q_ref[h], kvs[h][0])
            if win is not None:
                s = s + _toeplitz(w_ref[h], tq, S)
            scores.append(s)
        probs = []
        for s in scores:
            p = jnp.exp(s - jnp.max(s, axis=-1, keepdims=True))
            probs.append((p.astype(BF16), jnp.sum(p, axis=-1, keepdims=True)))
        for h, (p, l) in enumerate(probs):
            o_ref[h] = _nn(p, kvs[h][1]) / l

    qs = pl.BlockSpec((None, nh, tq, HEAD_DIM), lambda b, h, i: (b, h, i, 0))
    ks = (pl.BlockSpec((None, None, S, HEAD_DIM), lambda b, h, i: (b, h, 0, 0)) if shared
          else pl.BlockSpec((None, nh, S, HEAD_DIM), lambda b, h, i: (b, h, 0, 0)))
    ins, specs = [q, k, v], [qs, ks, ks]
    if win is not None:
        ins.append(win)
        specs.append(pl.BlockSpec((nh, None, 1, S + tq), lambda b, h, i: (h, i, 0, 0)))
    return pl.pallas_call(body, name=name, grid=(B, H // nh, S // tq), in_specs=specs, out_specs=qs,
                          out_shape=jax.ShapeDtypeStruct((B, H, S, HEAD_DIM), F32),
                          compiler_params=_params(("parallel", "parallel", "parallel")))(*ins)


def _attn_bwd(q, k, v, o, do, win, name, nh=ATTN_HEADS, tq=ATTN_TQ):
    B, H, S, _ = q.shape
    hkv = k.shape[1]
    shared = hkv != H
    assert not shared or H // hkv == nh
    tq = min(tq, S)
    nq = S // tq

    def body(*refs):
        if win is None:
            q_ref, k_ref, v_ref, o_ref, do_ref, dq_ref, dk_ref, dv_ref = refs
        else:
            q_ref, k_ref, v_ref, o_ref, do_ref, w_ref, rev_ref, dq_ref, dk_ref, dv_ref, dw_ref = refs

        @pl.when(pl.program_id(2) == 0)
        def _():
            dk_ref[...] = jnp.zeros_like(dk_ref)
            dv_ref[...] = jnp.zeros_like(dv_ref)

        kvs = [(k_ref[...], v_ref[...])] * nh if shared else [(k_ref[h], v_ref[h]) for h in range(nh)]
        qvs, dobs, scores, dps = [], [], [], []
        for h in range(nh):
            qv, dov = q_ref[h], do_ref[h]
            dob = dov.astype(BF16)
            s = _nt(qv, kvs[h][0])
            if win is not None:
                s = s + _toeplitz(w_ref[h], tq, S)
            dp = _nt(dob, kvs[h][1]) - jnp.sum(dov * o_ref[h], axis=-1, keepdims=True)
            qvs.append(qv)
            dobs.append(dob)
            scores.append(s)
            dps.append(dp)
        pbs, dsbs = [], []
        for s, dp in zip(scores, dps):
            p = jnp.exp(s - jnp.max(s, axis=-1, keepdims=True))
            p = p * (1.0 / jnp.sum(p, axis=-1, keepdims=True))
            pbs.append(p.astype(BF16))
            dsbs.append((p * dp).astype(BF16))
        dk_acc = dv_acc = None
        for h in range(nh):
            dvh, dkh = _tn(pbs[h], dobs[h]), _tn(dsbs[h], qvs[h])
            dq_ref[h] = _nn(dsbs[h], kvs[h][0])
            if shared:
                dv_acc = dvh if dv_acc is None else dv_acc + dvh
                dk_acc = dkh if dk_acc is None else dk_acc + dkh
            else:
                dv_ref[h] += dvh
                dk_ref[h] += dkh
            if win is not None:
                rev = _nn(rev_ref[...], dsbs[h])
                wide = jnp.concatenate([rev, jnp.zeros((tq, tq), F32)], axis=1)
                dw_ref[h] = jnp.sum(pltpu.roll(wide, 0, 1, stride=1, stride_axis=0), axis=0, keepdims=True)
        if shared:
            dv_ref[...] += dv_acc
            dk_ref[...] += dk_acc

    qs = pl.BlockSpec((None, nh, tq, HEAD_DIM), lambda b, h, i: (b, h, i, 0))
    ks = (pl.BlockSpec((None, None, S, HEAD_DIM), lambda b, h, i: (b, h, 0, 0)) if shared
          else pl.BlockSpec((None, nh, S, HEAD_DIM), lambda b, h, i: (b, h, 0, 0)))
    ins, specs = [q, k, v, o, do], [qs, ks, ks, qs, qs]
    outs = [jax.ShapeDtypeStruct((B, H, S, HEAD_DIM), F32), jax.ShapeDtypeStruct((B, hkv, S, HEAD_DIM), F32), jax.ShapeDtypeStruct((B, hkv, S, HEAD_DIM), F32)]
    ospecs = [qs, ks, ks]
    if win is not None:
        ins += [win, jnp.asarray(np.eye(tq, dtype=np.float32)[::-1].copy(), BF16)]
        specs += [pl.BlockSpec((nh, None, 1, S + tq), lambda b, h, i: (h, i, 0, 0)), pl.BlockSpec((tq, tq), lambda b, h, i: (0, 0))]
        outs.append(jax.ShapeDtypeStruct((B, H, nq, 1, S + tq), F32))
        ospecs.append(pl.BlockSpec((None, nh, None, 1, S + tq), lambda b, h, i: (b, h, i, 0, 0)))
    return pl.pallas_call(body, name=name, grid=(B, H // nh, nq), in_specs=specs, out_specs=ospecs, out_shape=outs,
                          compiler_params=_params(("parallel", "parallel", "arbitrary")))(*ins)


def _pattern_count(delta):
    n = jnp.zeros(delta.shape, jnp.int32)
    for window, dil in DIL_PATTERNS:
        n = n + ((delta % dil == 0) & (jnp.abs(delta) <= window // 2)).astype(jnp.int32)
    return n


def _t5_bucket(rel):
    nb = REL_BUCKETS // 2
    max_exact = nb // 2
    ret = jnp.where(rel > 0, nb, 0)
    n = jnp.abs(rel)
    nf = jnp.maximum(n, 1).astype(F32)
    large = max_exact + (jnp.log(nf / max_exact) / math.log(REL_MAX_DIST / max_exact) * (nb - max_exact)).astype(jnp.int32)
    large = jnp.minimum(large, nb - 1)
    return ret + jnp.where(n < max_exact, n, large)


def _bias_windows(rel_bias, S):
    tq = min(ATTN_TQ, S)
    nq = S // tq
    n = nq * (S + tq)
    delta = (jnp.arange(S + tq)[None, :] - (jnp.arange(nq)[:, None] + 1) * tq).reshape(n)
    count = _pattern_count(delta)
    onehot = (_t5_bucket(delta)[None, :] == jnp.arange(REL_BUCKETS)[:, None]).astype(F32)
    extra = jnp.where(count > 0, jnp.log(jnp.maximum(count, 1).astype(F32)), MASKED).reshape(1, n)
    live = (count > 0).astype(F32).reshape(1, n)

    def body(t_ref, oh_ref, live_ref, extra_ref, o_ref):
        o_ref[...] = _nn(t_ref[...], oh_ref[...], HIGHEST) * live_ref[...] + extra_ref[...]

    val = pl.pallas_call(body, name="bias_windows", out_shape=jax.ShapeDtypeStruct((N_HEADS, n), F32),
                         compiler_params=_params())(rel_bias.T, onehot, live, extra)
    return val.reshape(N_HEADS, nq, 1, S + tq)


def _bias_fold(dwin, S, name):
    B, H, nq = dwin.shape[:3]
    tq = min(ATTN_TQ, S)
    n = nq * (S + tq)
    delta = (jnp.arange(S + tq)[None, :] - (tq - 1) - jnp.arange(nq)[:, None] * tq).reshape(n)
    onehot = (_t5_bucket(delta)[:, None] == jnp.arange(128)[None, :]).astype(F32)

    def body(d_ref, oh_ref, o_ref):
        tot = d_ref[0]
        for b in range(1, B):
            tot = tot + d_ref[b]
        o_ref[...] = _nn(tot, oh_ref[...], HIGHEST)

    out = pl.pallas_call(body, name=name, out_shape=jax.ShapeDtypeStruct((H, 128), F32), compiler_params=_params())(dwin.reshape(B, H, n), onehot)
    return out[:, :REL_BUCKETS].T


def _rope_tables(S):
    half = 16
    freqs = ROPE_THETA ** (-jnp.arange(half, dtype=F32) / half)
    t = jnp.arange(S)
    ang_r = (t // GRID_W).astype(F32)[:, None] * freqs[None, :]
    ang_c = (t % GRID_W).astype(F32)[:, None] * freqs[None, :]
    cos = jnp.concatenate([jnp.cos(ang_r)] * 2 + [jnp.cos(ang_c)] * 2, axis=1)
    sin = jnp.concatenate([-jnp.sin(ang_r), jnp.sin(ang_r), -jnp.sin(ang_c), jnp.sin(ang_c)], axis=1)
    return jnp.tile(cos, (1, N_HEADS)), jnp.tile(sin, (1, N_HEADS))


def _conv_act(c, g, b):
    n, rs = _ln_rows(c)
    t = n * g + b
    return t * _sigmoid(t), n, rs, t


def _mix_fwd(ya, ob, c, od, gain, lng, lnb, B, S, name):
    T = B * S
    tm = min(256, S)
    ns = S // tm

    def body(ya_ref, ob_ref, c_ref, od_ref, g_ref, lg_ref, lb_ref, o_ref):
        yc = _conv_act(c_ref[...], lg_ref[...], lb_ref[...])[0]
        ys = [ya_ref[...], _load_heads(ob_ref, N_HEADS), yc, _load_heads(od_ref, N_HEADS)]
        for m, y in enumerate(ys):
            r = lax.rsqrt(jnp.mean(y * y, axis=-1, keepdims=True) + RMS_EPS)
            o_ref[:, 512 * m:512 * (m + 1)] = (y * r * g_ref[:, 512 * m:512 * (m + 1)]).astype(BF16)

    row = pl.BlockSpec((tm, 512), lambda b, i: (b * ns + i, 0))
    heads = pl.BlockSpec((None, N_HEADS, tm, HEAD_DIM), lambda b, i: (b, 0, i, 0))
    vec = pl.BlockSpec((1, 512), lambda b, i: (0, 0))
    return pl.pallas_call(
        body, name=name, grid=(B, ns), in_specs=[row, heads, row, heads, pl.BlockSpec((1, 2048), lambda b, i: (0, 0)), vec, vec],
        out_specs=pl.BlockSpec((tm, 2048), lambda b, i: (b * ns + i, 0)), out_shape=jax.ShapeDtypeStruct((T, 2048), BF16),
        compiler_params=_params(("parallel", "parallel")),
    )(ya, ob, c, od, gain, lng, lnb)


def _mix_bwd(ya, ob, c, od, dycat, gain, lng, lnb, B, S, name):
    T = B * S
    tm = min(256, S)
    ns = S // tm

    def body(ya_ref, ob_ref, c_ref, od_ref, dy_ref, g_ref, lg_ref, lb_ref, dya_ref, dob_ref, dc_ref, dod_ref, dg_ref, dlg_ref, dlb_ref, dcb_ref):
        @pl.when((pl.program_id(0) == 0) & (pl.program_id(1) == 0))
        def _():
            for ref in (dg_ref, dlg_ref, dlb_ref, dcb_ref):
                ref[...] = jnp.zeros_like(ref)

        yc, n, rs, t = _conv_act(c_ref[...], lg_ref[...], lb_ref[...])
        ys = [ya_ref[...], _load_heads(ob_ref, N_HEADS), yc, _load_heads(od_ref, N_HEADS)]
        outs = [dya_ref, dob_ref, None, dod_ref]
        for m, y in enumerate(ys):
            cols = slice(512 * m, 512 * (m + 1))
            r = lax.rsqrt(jnp.mean(y * y, axis=-1, keepdims=True) + RMS_EPS)
            yh = y * r
            dh = dy_ref[:, cols]
            dg_ref[:, cols] += jnp.sum(dh * yh, axis=0, keepdims=True)
            dyh = dh * g_ref[:, cols]
            dyv = r * (dyh - yh * jnp.mean(dyh * yh, axis=-1, keepdims=True))
            if m == 0:
                outs[m][...] = dyv
            elif m == 2:
                sg = _sigmoid(t)
                dt = dyv * sg * (1.0 + t * (1.0 - sg))
                dlg_ref[...] += jnp.sum(dt * n, axis=0, keepdims=True)
                dlb_ref[...] += jnp.sum(dt, axis=0, keepdims=True)
                dn = dt * lg_ref[...]
                dc = rs * (dn - jnp.mean(dn, axis=-1, keepdims=True) - n * jnp.mean(dn * n, axis=-1, keepdims=True))
                dc_ref[...] = dc
                dcb_ref[...] += jnp.sum(dc, axis=0, keepdims=True)
            else:
                _store_heads(outs[m], dyv, N_HEADS)

    row = pl.BlockSpec((tm, 512), lambda b, i: (b * ns + i, 0))
    heads = pl.BlockSpec((None, N_HEADS, tm, HEAD_DIM), lambda b, i: (b, 0, i, 0))
    vec = pl.BlockSpec((1, 2048), lambda b, i: (0, 0))
    flat = jax.ShapeDtypeStruct((T, 512), F32)
    hm = jax.ShapeDtypeStruct((B, N_HEADS, S, HEAD_DIM), F32)
    v512 = pl.BlockSpec((1, 512), lambda b, i: (0, 0))
    s512 = jax.ShapeDtypeStruct((1, 512), F32)
    return pl.pallas_call(
        body, name=name, grid=(B, ns), in_specs=[row, heads, row, heads, pl.BlockSpec((tm, 2048), lambda b, i: (b * ns + i, 0)), vec, v512, v512],
        out_specs=[row, heads, row, heads, vec, v512, v512, v512],
        out_shape=[flat, hm, flat, hm, jax.ShapeDtypeStruct((1, 2048), F32), s512, s512, s512],
        compiler_params=_params(("arbitrary", "arbitrary")),
    )(ya, ob, c, od, dycat, gain, lng, lnb)


def _ffn_down(gate, up, w_down, res, name, norm=None, target=None):
    J, T, n = gate.shape
    N = w_down.shape[2]
    tm = min(256, T)
    steps = T // tm

    def body(g_ref, u_ref, w_ref, r_ref, *rest):
        x_ref = rest[0] if (norm is not None or target is not None) else None
        o_ref, act_ref = rest[-3:-1] if x_ref is not None else rest[-2:]
        acc = None
        for j in range(J):
            g = g_ref[j].astype(F32)
            a = (g * _sigmoid(g) * u_ref[j].astype(F32)).astype(BF16)
            act_ref[j] = a
            d = _nn(a, w_ref[j])
            acc = d if acc is None else acc + d
        y = acc + r_ref[...]
        if target is None:
            o_ref[...] = y
            if norm is not None:
                rest[-1][...] = _rms_rows(y, x_ref[...])
            return
        err = y - x_ref[...]
        o_ref[...] = err * (1.0 / N)
        loss_ref, i = rest[-1], pl.program_id(0)

        @pl.when(i == 0)
        def _():
            loss_ref[...] = jnp.zeros_like(loss_ref)

        loss_ref[...] += jnp.sum(err * err)

        @pl.when(i == steps - 1)
        def _():
            loss_ref[...] = loss_ref[...] * (0.5 / N)

    gu = pl.BlockSpec((J, tm, n), lambda i: (0, i, 0))
    row = pl.BlockSpec((tm, N), lambda i: (i, 0))
    ins, specs = [gate, up, w_down, res], [gu, gu, pl.BlockSpec((J, n, N), lambda i: (0, 0, 0)), row]
    outs, ospecs = [jax.ShapeDtypeStruct((T, N), F32), jax.ShapeDtypeStruct((J, T, n), BF16)], [row, gu]
    if target is not None:
        ins, specs = ins + [target], specs + [row]
        outs, ospecs = outs + [jax.ShapeDtypeStruct((8, 128), F32)], ospecs + [pl.BlockSpec((8, 128), lambda i: (0, 0))]
    elif norm is not None:
        ins, specs = ins + [norm.reshape(1, N)], specs + [pl.BlockSpec((1, N), lambda i: (0, 0))]
        outs, ospecs = outs + [jax.ShapeDtypeStruct((T, N), BF16)], ospecs + [row]
    return pl.pallas_call(body, name=name, grid=(steps,), in_specs=specs, out_specs=ospecs, out_shape=outs,
                          compiler_params=_params(("arbitrary" if target is not None else "parallel",)))(*ins)


def _ffn_down_dx(dx, w_down, gate, up, name):
    J, n, D = w_down.shape
    T = dx.shape[0]
    tm = min(512, T)

    def body(dx_ref, w_ref, g_ref, u_ref, dg_ref, du_ref):
        d = _nt(dx_ref[...].astype(BF16), w_ref[...])
        g = g_ref[...].astype(F32)
        s = _sigmoid(g)
        dg_ref[...] = (d * u_ref[...].astype(F32) * s * (1.0 + g * (1.0 - s))).astype(BF16)
        du_ref[...] = (d * g * s).astype(BF16)

    blk = pl.BlockSpec((None, tm, n), lambda j, i: (j, i, 0))
    shape = jax.ShapeDtypeStruct((J, T, n), BF16)
    return pl.pallas_call(body, name=name, grid=(J, T // tm),
                          in_specs=[pl.BlockSpec((tm, D), lambda j, i: (i, 0)), pl.BlockSpec((None, n, D), lambda j, i: (j, 0, 0)), blk, blk],
                          out_specs=[blk, blk], out_shape=[shape, shape], compiler_params=_params(("parallel", "parallel")))(dx, w_down, gate, up)


def _row_tile(R):
    best = R
    for cand in range(16, min(R, 272) + 1, 16):
        if R % cand == 0:
            best = cand
    return best


def _adamw(w, m, v, stack, name, layer=None, prev=None):
    n, R, C = stack.shape
    tm = _row_tile(R)
    nb = R // tm
    off = 0 if layer is None else layer * nb
    c1 = 1.0 - ADAM_B1 ** ADAM_STEP
    c2 = 1.0 - ADAM_B2 ** ADAM_STEP

    def body(w_ref, m_ref, v_ref, s_ref, *rest):
        g_ref, d_ref, mo_ref, vo_ref = rest[-4:]
        g = s_ref[0].astype(F32)
        for k in range(1, n):
            g = g + s_ref[k].astype(F32)
        mn = ADAM_B1 * m_ref[...] + (1.0 - ADAM_B1) * g
        vn = ADAM_B2 * v_ref[...] + (1.0 - ADAM_B2) * (g * g)
        g_ref[...] = g
        mo_ref[...] = mn
        vo_ref[...] = vn
        d_ref[...] = -ADAM_LR * ((mn / c1) / (jnp.sqrt(vn / c2) + ADAM_EPS) + ADAM_WD * w_ref[...])

    blk = pl.BlockSpec((tm, C), lambda i: (i + off, 0))
    ins = [w, m, v, stack]
    specs = [blk, blk, blk, pl.BlockSpec((n, tm, C), lambda i: (0, i, 0))]
    aliases = {}
    if prev is not None:
        ins += list(prev)
        specs += [pl.BlockSpec(memory_space=pl.ANY)] * 4
        aliases = {4 + t: t for t in range(4)}
    shape = jax.ShapeDtypeStruct(w.shape, F32)
    return pl.pallas_call(body, name=name, grid=(nb,), in_specs=specs, out_specs=[blk] * 4, out_shape=[shape] * 4,
                          input_output_aliases=aliases, compiler_params=_params(("parallel",)))(*ins)


HBM = pl.BlockSpec(memory_space=pltpu.HBM)
SEM = pl.BlockSpec(memory_space=pltpu.SEMAPHORE)
EFFECT = pltpu.SideEffectType.DATAFLOW_SIDE_EFFECTING


PEERS = {"scatter": (1, 2, 3, 4, 5, 6, 7), "gather": (1, 2, 3, 4, 5, 6, 7), "chips": (1, 2, 4, 6), "forward": (2, 4, 6)}


def _spread_copies(srcs, lands, send_sems, recv_sems, local_sems, kind, waiting):
    x, y, c = lax.axis_index("x"), lax.axis_index("y"), lax.axis_index("c")
    me = 4 * x + 2 * y + c

    def peer(bits):
        dev = (1 - x if bits & 4 else x, 1 - y if bits & 2 else y, 1 - c if bits & 1 else c)
        return dev, 4 * dev[0] + 2 * dev[1] + dev[2]

    plan = PEERS[kind]
    remote, local = [], []
    for a, l in enumerate(lands):
        for d, bits in enumerate(plan):
            dev, pid = peer(bits)
            if kind == "forward":
                src, dst, dev = l.at[pid], l.at[peer(bits | 1)[1] if waiting else pid], peer(1)[0]
            else:
                src, dst = (srcs[a].at[pid] if kind == "scatter" else srcs[a]), l.at[pid if waiting else me]
            remote.append(pltpu.make_async_remote_copy(
                src_ref=src, dst_ref=dst, send_sem=send_sems.at[a * len(plan) + d], recv_sem=recv_sems.at[a * len(plan) + d],
                device_id=dev, device_id_type=MESH_ID))
        if kind != "forward":
            local.append(pltpu.make_async_copy(srcs[a].at[me] if kind == "scatter" else srcs[a], l.at[me], local_sems.at[a]))
    return remote, local


def _spread_start(srcs, kind, name, after=None, lands=None):
    if kind == "forward":
        srcs = []
    else:
        shapes = [a.shape if kind == "scatter" else (N_DEV,) + a.shape for a in srcs]
        lands = [lax.empty(shp, a.dtype) for shp, a in zip(shapes, srcs)]
    ns, nl, per = len(srcs), len(lands), len(PEERS[kind])
    extra = [] if after is None else [after]
    sem_shapes = [pltpu.SemaphoreType.DMA((nl * per,))] * 2 + ([pltpu.SemaphoreType.DMA((nl,))] if ns else [])

    def body(*refs):
        src_refs, land_refs = refs[:ns], refs[ns:ns + nl]
        sems = refs[ns + nl + len(extra):ns + nl + len(extra) + len(sem_shapes)]
        remote, local = _spread_copies(src_refs, land_refs, sems[0], sems[1], sems[2] if ns else None, kind, False)
        for cp in remote + local:
            cp.start()
        refs[-1][...] = jnp.zeros((8, 128), F32)

    outs = pl.pallas_call(
        body, name=name,
        out_shape=(*sem_shapes, *[pltpu.HBM(a.shape, a.dtype) for a in srcs + lands], jax.ShapeDtypeStruct((8, 128), F32)),
        in_specs=[HBM] * (ns + nl) + [pl.BlockSpec(memory_space=pl.ANY)] * len(extra),
        out_specs=(*[SEM] * len(sem_shapes), *[HBM] * (ns + nl), pl.BlockSpec(memory_space=pltpu.VMEM)),
        input_output_aliases={i: len(sem_shapes) + i for i in range(ns + nl)},
        compiler_params=pltpu.CompilerParams(has_side_effects=EFFECT),
    )(*[pltpu.with_memory_space_constraint(a, pltpu.HBM) for a in srcs + lands], *extra)
    k = len(sem_shapes)
    return outs[:k], list(outs[k:k + ns]), list(outs[k + ns:k + ns + nl]), outs[-1]


def _spread_wait(sems, srcs, lands, after, kind, name):
    ns, nl = len(srcs), len(lands)
    after = list(after) if isinstance(after, (list, tuple)) else [after]

    def body(*refs):
        src_refs, land_refs = refs[:ns], refs[ns:ns + nl]
        s = refs[ns + nl:ns + nl + len(sems)]
        remote, local = _spread_copies(src_refs, land_refs, s[0], s[1], s[2] if ns else None, kind, True)
        for cp in remote:
            cp.wait_send()
            cp.wait_recv()
        for cp in local:
            cp.wait()

    outs = pl.pallas_call(
        body, name=name, out_shape=tuple(pltpu.HBM(a.shape, a.dtype) for a in srcs + lands),
        in_specs=[HBM] * (ns + nl) + [SEM] * len(sems) + [pl.BlockSpec(memory_space=pl.ANY)] * len(after), out_specs=tuple([HBM] * (ns + nl)),
        input_output_aliases={i: i for i in range(ns + nl)}, compiler_params=pltpu.CompilerParams(has_side_effects=EFFECT),
    )(*srcs, *lands, *sems, *after)
    return list(outs[ns:])


SMALL = ("rel_bias", "norm1_g", "sgu_w", "sgu_b", "dil_qn_g", "dil_kn_g", "conv_w", "conv_b", "conv_ln_g", "conv_ln_b",
         "gqa_qn_g", "gqa_kn_g", "mix_norm_g", "norm2_g")
LARGE = ("w_in", "w_out", "w_gate", "w_up", "w_down")
EARLY = tuple(k for k in SMALL if k != "norm1_g")


def _local_step(x, target, p, B, S, fetch, emit, mid, early):
    T = B * S
    rope = _rope_tables(S)
    win = _bias_windows(p["rel_bias"], S)
    tile8 = lambda g: jnp.tile(g.reshape(1, HEAD_DIM), (1, N_HEADS))
    cols_b = (COL_BQ, COL_BK, COL_BV)
    cols_d = (COL_DQ, COL_DK128, COL_DV128)
    saved = []
    for l in range(DEPTH):
        s = {"x": x}
        s["ws"] = p["sgu_w"][l].astype(BF16)
        s["bias"] = jnp.repeat(p["sgu_b"][l].T, HEAD_DIM, axis=1)
        s["h"] = _rms_fwd(x, p["norm1_g"][l], f"rms1_fwd_{l}") if l == 0 else h_next
        s["win"] = fetch(l, "in", s["h"])
        s["cw"] = jnp.pad(s["win"]["conv_w"], ((0, 1), (0, 0))).reshape(32, 1, 512)
        z = s["z"] = _matmul(s["h"], s["win"]["w_in"], "nt", f"in_proj_{l}", tk=D_MODEL)
        s["bias"] = s["bias"] + mid(l, z)
        s["ya"] = _sgu_fwd(z, s["ws"], s["bias"], f"sgu_fwd_{l}")
        s["c"] = _conv_fwd1(z, s["cw"], p["conv_b"][l].reshape(1, 512), B, S, f"conv_fwd_{l}")
        s["ln"] = (p["conv_ln_g"][l].reshape(1, 512), p["conv_ln_b"][l].reshape(1, 512))
        s["gb"] = (tile8(p["dil_qn_g"][l]), tile8(p["dil_kn_g"][l]))
        s["gd"] = (tile8(p["gqa_qn_g"][l]), tile8(p["gqa_kn_g"][l])[:, :KV_WIDTH])
        s["qkv_b"] = _prep_fwd(z, *s["gb"], None, B, S, N_HEADS, cols_b, f"prep_b_fwd_{l}")
        s["qkv_d"] = _prep_fwd(z, *s["gd"], rope, B, S, KV_HEADS, cols_d, f"prep_d_fwd_{l}")
        s["ob"] = _attn_fwd(*s["qkv_b"], win, f"attn_b_fwd_{l}")
        s["od"] = _attn_fwd(*s["qkv_d"], None, f"attn_d_fwd_{l}", tq=GQA_TQ)
        s["gmix"] = p["mix_norm_g"][l].reshape(1, 2048)
        s["ycat"] = _mix_fwd(s["ya"], s["ob"], s["c"], s["od"], s["gmix"], *s["ln"], B, S, f"mix_fwd_{l}")
        s["wout"] = fetch(l, "out", s["ycat"])["w_out"]
        x1, s["h2"] = _matmul(s["ycat"], s["wout"], "nn", f"out_proj_{l}", res=x, tk=D_MODEL, norm=p["norm2_g"][l])
        s["x1"] = x1
        s["ffn"] = fetch(l, "ffn", s["h2"])
        s["gate"] = _mm_shard_out(s["h2"], s["ffn"]["w_gate"], "nt", f"ffn_gate_{l}", out_dtype=BF16, tm=1024)
        s["up"] = _mm_shard_out(s["h2"], s["ffn"]["w_up"], "nt", f"ffn_up_{l}", out_dtype=BF16, tm=1024)
        if l + 1 < DEPTH:
            x, s["act"], h_next = _ffn_down(s["gate"], s["up"], s["ffn"]["w_down"], x1, f"ffn_down_{l}", norm=p["norm1_g"][l + 1])
        else:
            dx, s["act"], loss_blk = _ffn_down(s["gate"], s["up"], s["ffn"]["w_down"], x1, f"ffn_down_{l}", target=target)
        saved.append(s)

    g = {k: [None] * DEPTH for k in SMALL if k != "rel_bias"}
    dwin_total = None
    for l in reversed(range(DEPTH)):
        s = saved[l]
        z, ffn = s["z"], s["ffn"]
        dgate, dup = _ffn_down_dx(dx, ffn["w_down"], s["gate"], s["up"], f"ffn_down_dx_{l}")
        tok = emit(l, "w_down", _mm_shard_m(s["act"], dx, f"ffn_down_dw_{l}", out_dtype=BF16, tm=FFN_BLOCK, tn=512, tk=T))
        tok += emit(l, "w_gate", _mm_shard_m(dgate, s["h2"], f"ffn_gate_dw_{l}", out_dtype=BF16, tm=FFN_BLOCK, tn=512, tk=T))
        tok += emit(l, "w_up", _mm_shard_m(dup, s["h2"], f"ffn_up_dw_{l}", out_dtype=BF16, tm=FFN_BLOCK, tn=512, tk=T))
        dh2 = _mm_shard_k([(dgate, ffn["w_gate"]), (dup, ffn["w_up"])], "nn", f"ffn_up_dx_{l}", tn=512, fold=FFN_GROUPS)
        dycat, dx1, dg2 = _out_proj_dx(dh2, s["x1"], p["norm2_g"][l] + tok, dx, s["wout"], f"out_proj_dx_{l}")
        g["norm2_g"][l] = dg2[0]
        tok = emit(l, "w_out", _matmul(s["ycat"], dx1, "tn", f"out_proj_dw_{l}", out_dtype=BF16, tn=1024, tk=T))
        dya, dob, dc, dod, dgm, dlg, dlb, dcb = _mix_bwd(s["ya"], s["ob"], s["c"], s["od"], dycat, s["gmix"] + tok, *s["ln"], B, S, f"mix_bwd_{l}")
        g["mix_norm_g"][l] = dgm[0]
        dz_a, dws, dbias = _sgu_bwd(z, dya, s["ws"], jnp.swapaxes(s["ws"], 1, 2), s["bias"], f"sgu_bwd_{l}")
        g["sgu_w"][l] = dws
        g["sgu_b"][l] = dbias.reshape(128, 8, HEAD_DIM).sum(-1).T
        g["conv_ln_g"][l], g["conv_ln_b"][l], g["conv_b"][l] = dlg[0], dlb[0], dcb[0]
        dz_ca, dz_cg, dcw = _conv_bwd2(z, dc, s["cw"], B, S, f"conv_bwd_{l}")
        g["conv_w"][l] = dcw.reshape(32, 512)[:CONV_WIDTH]
        dq, dk, dv, dwin = _attn_bwd(*s["qkv_b"], s["ob"], dob, win, f"attn_b_bwd_{l}")
        dwin_total = dwin if dwin_total is None else dwin_total + dwin
        dz_b, dgq, dgk = _prep_bwd(z, dq, dk, dv, *s["gb"], None, B, S, N_HEADS, cols_b, f"prep_b_bwd_{l}")
        g["dil_qn_g"][l] = dgq.reshape(N_HEADS, HEAD_DIM).sum(0)
        g["dil_kn_g"][l] = dgk.reshape(N_HEADS, HEAD_DIM).sum(0)
        dq, dk, dv = _attn_bwd(*s["qkv_d"], s["od"], dod, None, f"attn_d_bwd_{l}", tq=GQA_TQ)
        dz_d, dgq, dgk = _prep_bwd(z, dq, dk, dv, *s["gd"], rope, B, S, KV_HEADS, cols_d, f"prep_d_bwd_{l}")
        g["gqa_qn_g"][l] = dgq.reshape(N_HEADS, HEAD_DIM).sum(0)
        g["gqa_kn_g"][l] = dgk.reshape(KV_HEADS, HEAD_DIM).sum(0)
        dz = [dz_a, dz_b, dz_ca, dz_cg, dz_d]
        tok = jnp.zeros((), F32)
        if l == 0:
            done = {k: jnp.stack(v) for k, v in g.items() if k != "norm1_g"}
            done["rel_bias"] = _bias_fold(dwin_total, S, "bias_fold")
            tok = early(done)
        tok += emit(l, "w_in", _in_proj_dw(dz, s["h"], f"in_proj_dw_{l}"))
        dh = _in_proj_dx(dz, s["win"]["w_in"], f"in_proj_dx_{l}")
        dx, dg1 = _rms_bwd(dh, s["x"], p["norm1_g"][l] + tok, dx1, f"rms1_bwd_{l}")
        g["norm1_g"][l] = dg1[0]

    return loss_blk[0, 0], dx, jnp.stack(g["norm1_g"])


GROUPS = {"in": ("w_in",), "out": ("w_out",), "ffn": ("w_gate", "w_up", "w_down")}
COL_SHARDED = ("w_in", "w_gate", "w_up")


def kernel(x, rel_bias, norm1_g, w_in, sgu_w, sgu_b, dil_qn_g, dil_kn_g, conv_w, conv_b, conv_ln_g, conv_ln_b, gqa_qn_g, gqa_kn_g, mix_norm_g, w_out, norm2_g, w_gate, w_up, w_down, loss_target, m_rel_bias, m_norm1_g, m_w_in, m_sgu_w, m_sgu_b, m_dil_qn_g, m_dil_kn_g, m_conv_w, m_conv_b, m_conv_ln_g, m_conv_ln_b, m_gqa_qn_g, m_gqa_kn_g, m_mix_norm_g, m_w_out, m_norm2_g, m_w_gate, m_w_up, m_w_down, v_rel_bias, v_norm1_g, v_w_in, v_sgu_w, v_sgu_b, v_dil_qn_g, v_dil_kn_g, v_conv_w, v_conv_b, v_conv_ln_g, v_conv_ln_b, v_gqa_qn_g, v_gqa_kn_g, v_mix_norm_g, v_w_out, v_norm2_g, v_w_gate, v_w_up, v_w_down):
    w = dict(rel_bias=rel_bias, norm1_g=norm1_g, w_in=w_in, sgu_w=sgu_w, sgu_b=sgu_b, dil_qn_g=dil_qn_g, dil_kn_g=dil_kn_g, conv_w=conv_w,
             conv_b=conv_b, conv_ln_g=conv_ln_g, conv_ln_b=conv_ln_b, gqa_qn_g=gqa_qn_g, gqa_kn_g=gqa_kn_g, mix_norm_g=mix_norm_g,
             w_out=w_out, norm2_g=norm2_g, w_gate=w_gate, w_up=w_up, w_down=w_down)
    m = dict(rel_bias=m_rel_bias, norm1_g=m_norm1_g, w_in=m_w_in, sgu_w=m_sgu_w, sgu_b=m_sgu_b, dil_qn_g=m_dil_qn_g, dil_kn_g=m_dil_kn_g,
             conv_w=m_conv_w, conv_b=m_conv_b, conv_ln_g=m_conv_ln_g, conv_ln_b=m_conv_ln_b, gqa_qn_g=m_gqa_qn_g, gqa_kn_g=m_gqa_kn_g,
             mix_norm_g=m_mix_norm_g, w_out=m_w_out, norm2_g=m_norm2_g, w_gate=m_w_gate, w_up=m_w_up, w_down=m_w_down)
    v = dict(rel_bias=v_rel_bias, norm1_g=v_norm1_g, w_in=v_w_in, sgu_w=v_sgu_w, sgu_b=v_sgu_b, dil_qn_g=v_dil_qn_g, dil_kn_g=v_dil_kn_g,
             conv_w=v_conv_w, conv_b=v_conv_b, conv_ln_g=v_conv_ln_g, conv_ln_b=v_conv_ln_b, gqa_qn_g=v_gqa_qn_g, gqa_kn_g=v_gqa_kn_g,
             mix_norm_g=v_mix_norm_g, w_out=v_w_out, norm2_g=v_norm2_g, w_gate=v_w_gate, w_up=v_w_up, w_down=v_w_down)
    names = list(w)
    B, S, D = x.shape
    T = B * S
    me = 4 * lax.axis_index("x") + 2 * lax.axis_index("y") + lax.axis_index("c")

    view = lambda a, k: jnp.swapaxes(a, 1, 2) if k in COL_SHARDED else a
    bf = {k: view(w[k], k).astype(BF16) for k in LARGE}
    spreads, forwards = {}, {}

    def start_gather(l, group, after=None):
        srcs = [bf[k][l] for k in GROUPS[group]] + ([conv_w[l]] if group == "in" else [])
        spreads[l, group] = _spread_start(srcs, "chips", f"gather_{group}_{l}_start", after)
        return spreads[l, group][3][0, 0]

    def forward(l, group, after):
        sems, srcs, lands, _ = spreads[l, group]
        lands = _spread_wait(sems, srcs, lands, after, "chips", f"gather_{group}_{l}_wait")
        forwards[l, group] = _spread_start(None, "forward", f"forward_{group}_{l}_start", lands=lands)
        return forwards[l, group][3]

    def landed(l, group, after):
        sems, _, lands, _ = forwards[l, group]
        return _spread_wait(sems, [], lands, after, "forward", f"forward_{group}_{l}_wait")

    tok0 = start_gather(0, "in") + start_gather(0, "out") + start_gather(0, "ffn")
    small = {k: w[k] for k in SMALL}
    small["norm1_g"] = norm1_g.at[0].add(tok0)

    def mid(l, z):
        if l > 0:
            return jnp.zeros((), F32)
        return start_gather(1, "in", z) + start_gather(1, "out", z) + start_gather(1, "ffn", z)

    def fetch(l, group, after):
        if group == "in":
            tok = forward(0, "in", after) if l == 0 else after
        elif group == "out":
            tok = forward(l, "ffn", [after, forward(l, "out", after)])
        else:
            tok = forward(1, "in", after) if l == 0 else after
        got = dict(zip(GROUPS[group] + ("conv_w",), landed(l, group, [after, tok])))
        if group == "in":
            got["w_in"] = got["w_in"].reshape(IN_WIDTH, D)
            got["conv_w"] = jnp.transpose(got["conv_w"], (1, 0, 2)).reshape(CONV_WIDTH, 512)
        if group == "out":
            got["w_out"] = got["w_out"].reshape(D, D)
        if group == "ffn":
            got = {k: a.reshape(FFN_GROUPS, -1, D) for k, a in got.items()}
        return got

    scatters = {}

    def emit(l, k, dw):
        dw = dw.reshape(N_DEV, -1, D)
        scatters[l, k] = _spread_start([dw], "scatter", f"scatter_{k}_{l}_start")
        return scatters[l, k][3][0, 0]

    flat2 = lambda a: a.reshape(-1, a.shape[-1])
    small_spread = []

    def early(done):
        small_spread.append(_spread_start([flat2(done[k]) for k in EARLY], "gather", "gather_small_grads_start"))
        return small_spread[0][3][0, 0]

    loss_part, dx, dnorm1 = _local_step(x.reshape(T, D), loss_target.reshape(T, D), small, B, S, fetch, emit, mid, early)
    loss = lax.psum(loss_part, ("x", "y", "c"))

    out_g, out_d, out_m, out_v = {}, {}, {}, {}

    def update_large(k, after):
        shp = view(w[k], k).shape
        two_d = lambda a: view(a, k).reshape(-1, shp[-1])
        res = None
        for l in reversed(range(DEPTH)):
            sems, srcs, lands, _ = scatters[l, k]
            stack = _spread_wait(sems, srcs, lands, after, "scatter", f"scatter_{k}_{l}_wait")[0]
            res = _adamw(two_d(w[k]), two_d(m[k]), two_d(v[k]), stack.reshape(N_DEV, -1, shp[-1]), f"adamw_{k}_{l}", layer=l, prev=res)
        out_g[k], out_d[k], out_m[k], out_v[k] = [view(a.reshape(shp), k) for a in res]
        return res[0]

    late_sems, late_srcs, late_lands, late_tok = _spread_start([flat2(dnorm1)], "gather", "gather_norm1_grad_start")
    after = [dx, late_tok]
    for k in ("w_down", "w_gate", "w_up", "w_out"):
        after = update_large(k, after)
    sems, srcs, lands, _ = small_spread[0]
    stacks = dict(zip(EARLY, _spread_wait(sems, srcs, lands, after, "gather", "gather_small_grads_wait")))
    stacks["norm1_g"] = _spread_wait(late_sems, late_srcs, late_lands, after, "gather", "gather_norm1_grad_wait")[0]
    for k in SMALL:
        stack = stacks[k]
        if k == "conv_w":
            stack = lax.dynamic_slice_in_dim(stack, me * (512 // N_DEV), 512 // N_DEV, axis=2)
        res = _adamw(flat2(w[k]), flat2(m[k]), flat2(v[k]), stack, f"adamw_{k}")
        out_g[k], out_d[k], out_m[k], out_v[k] = [a.reshape(w[k].shape) for a in res]
        after = res[0]
    update_large("w_in", after)

    return (loss, dx.reshape(B, S, D), *[out_g[k] for k in names], *[out_d[k] for k in names],
            *[out_m[k] for k in names], *[out_v[k] for k in names])
```

```python
import functools
import math

import numpy as np
import jax
import jax.numpy as jnp
from jax import lax
from jax.experimental import pallas as pl
from jax.experimental.pallas import tpu as pltpu

F32 = jnp.float32
BF16 = jnp.bfloat16
HIGHEST = lax.Precision.HIGHEST
MESH_ID = pl.DeviceIdType.MESH

D_MODEL = 2048
DEPTH = 2
HEAD_DIM = 64
GROUP_WIDTH = 512
N_HEADS = 8
KV_HEADS = 2
KV_WIDTH = 128
SGU_CHUNK = 128
CONV_WIDTH = 31
CONV_PAD = 16
GRID_W = 64
ROPE_THETA = 10000.0
REL_BUCKETS = 32
REL_MAX_DIST = 1024
DIL_PATTERNS = ((128, 1), (512, 4), (2048, 16))
FFN_HIDDEN = 5632
IN_WIDTH = 4352
RMS_EPS = 1e-6
LN_EPS = 1e-5
MASKED = -1e30
N_DEV = 8

ADAM_LR = 0.001
ADAM_B1 = 0.9
ADAM_B2 = 0.999
ADAM_EPS = 1e-08
ADAM_WD = 0.01
ADAM_STEP = 10

COL_AU, COL_AV, COL_BQ, COL_BK, COL_BV, COL_CA, COL_CG, COL_DQ = range(8)
COL_DK128, COL_DV128 = 32, 33

VMEM_LIMIT = 56 * 1024 * 1024
FFN_GROUPS = 2
FFN_BLOCK = 1408
ATTN_TQ = 256
GQA_TQ = 512
ROW_TILE = 512


def _params(sem=None, vmem=VMEM_LIMIT):
    return pltpu.CompilerParams(dimension_semantics=sem, vmem_limit_bytes=vmem)


def _dot(a, b, dims, precision=None):
    return lax.dot_general(a, b, (dims, ((), ())), precision=precision, preferred_element_type=F32)


def _nn(a, b, precision=None):
    return _dot(a, b, ((1,), (0,)), precision)


def _nt(a, b):
    return _dot(a, b, ((1,), (1,)))


def _tn(a, b):
    return _dot(a, b, ((0,), (0,)))


DIMS = {"nn": ((1,), (0,)), "nt": ((1,), (1,)), "tn": ((0,), (0,))}


def _pick(n, cands):
    for c in cands:
        if n % c == 0:
            return c
    return n


def _mm_call(name, mode, pairs, specs, o_spec, out_sds, grid, acc_shape, res=None, fold=None, norm=None):
    npair, nk, dims = len(pairs), grid[2], DIMS[mode]

    def body(*refs):
        ab = refs[:2 * npair]
        at = 2 * npair
        r_ref = refs[at] if res is not None else None
        at += res is not None
        g_ref = refs[at] if norm is not None else None
        at += norm is not None
        o_ref = refs[at]
        h_ref = refs[at + 1] if norm is not None else None
        part = None
        for t in range(npair):
            for s in ([None] if fold is None else range(fold)):
                a_blk = ab[2 * t][...] if s is None else ab[2 * t][s]
                b_blk = ab[2 * t + 1][...] if s is None else ab[2 * t + 1][s]
                d = _dot(a_blk.astype(BF16), b_blk.astype(BF16), dims)
                part = d if part is None else part + d

        def finish(r):
            if r_ref is not None:
                r = r + r_ref[...]
            o_ref[...] = r.astype(o_ref.dtype)
            if h_ref is not None:
                h_ref[...] = _rms_rows(r, g_ref[...])

        if nk == 1:
            finish(part)
            return
        acc, k = refs[-1], pl.program_id(2)

        @pl.when(k == 0)
        def _():
            acc[...] = part

        @pl.when(k > 0)
        def _():
            acc[...] += part

        @pl.when(k == nk - 1)
        def _():
            finish(acc[...])

    ins = [t for pair in pairs for t in pair]
    in_specs = [t for pair in specs for t in pair]
    if res is not None:
        ins.append(res)
        in_specs.append(o_spec)
    out_specs = o_spec
    if norm is not None:
        ins.append(norm)
        in_specs.append(pl.BlockSpec(norm.shape, lambda *_: (0, 0)))
        out_specs, out_sds = [o_spec, o_spec], [out_sds, jax.ShapeDtypeStruct(out_sds.shape, BF16)]
    return pl.pallas_call(
        body, name=name, grid=grid, in_specs=in_specs, out_specs=out_specs, out_shape=out_sds,
        scratch_shapes=[pltpu.VMEM(acc_shape, F32)] if nk > 1 else [],
        compiler_params=_params(("parallel", "parallel", "arbitrary")),
    )(*ins)


def _rms_rows(x, g):
    return (x * lax.rsqrt(jnp.mean(x * x, axis=-1, keepdims=True) + RMS_EPS) * g).astype(BF16)


def _matmul(a, b, mode, name, res=None, out_dtype=F32, tm=512, tn=None, tk=None, norm=None):
    if mode == "nn":
        (M, K), N = a.shape, b.shape[1]
    elif mode == "nt":
        (M, K), N = a.shape, b.shape[0]
    else:
        (K, M), N = a.shape, b.shape[1]
    tm = min(tm, M)
    tn = tn or _pick(N, (2176, 2048, 1408, 1024, 512))
    tk = tk or _pick(K, (1024, 2176, 1408, 512))
    assert M % tm == 0 and N % tn == 0 and K % tk == 0, (M, N, K, tm, tn, tk)
    a_spec = pl.BlockSpec((tk, tm), lambda i, j, k: (k, i)) if mode == "tn" else pl.BlockSpec((tm, tk), lambda i, j, k: (i, k))
    b_spec = pl.BlockSpec((tn, tk), lambda i, j, k: (j, k)) if mode == "nt" else pl.BlockSpec((tk, tn), lambda i, j, k: (k, j))
    o_spec = pl.BlockSpec((tm, tn), lambda i, j, k: (i, j))
    assert norm is None or tn == N
    return _mm_call(name, mode, [(a, b)], [(a_spec, b_spec)], o_spec, jax.ShapeDtypeStruct((M, N), out_dtype),
                    (M // tm, N // tn, K // tk), (tm, tn), res, norm=None if norm is None else norm.reshape(1, N))


def _mm_shard_out(a, bs, mode, name, out_dtype=F32, tm=512, tk=None):
    J = bs.shape[0]
    n = bs.shape[1] if mode == "nt" else bs.shape[2]
    (K, M) = a.shape if mode == "tn" else a.shape[::-1]
    tm = min(tm, M)
    tk = tk or (K if mode != "tn" else _pick(K, (1024, 512)))
    a_spec = pl.BlockSpec((tk, tm), lambda j, i, k: (k, i)) if mode == "tn" else pl.BlockSpec((tm, tk), lambda j, i, k: (i, k))
    b_spec = pl.BlockSpec((None, n, tk), lambda j, i, k: (j, 0, k)) if mode == "nt" else pl.BlockSpec((None, tk, n), lambda j, i, k: (j, k, 0))
    o_spec = pl.BlockSpec((None, tm, n), lambda j, i, k: (j, i, 0))
    return _mm_call(name, mode, [(a, bs)], [(a_spec, b_spec)], o_spec, jax.ShapeDtypeStruct((J, M, n), out_dtype),
                    (J, M // tm, K // tk), (tm, n))


def _mm_shard_k(pairs, mode, name, res=None, out_dtype=F32, tm=512, tn=None, fold=1):
    J, M, n = pairs[0][0].shape
    N = pairs[0][1].shape[2] if mode == "nn" else pairs[0][1].shape[1]
    tm = min(tm, M)
    tn = tn or _pick(N, (2048, 1024, 512))
    a_spec = pl.BlockSpec((fold, tm, n), lambda i, j, k: (k, i, 0))
    b_spec = pl.BlockSpec((fold, n, tn), lambda i, j, k: (k, 0, j)) if mode == "nn" else pl.BlockSpec((fold, tn, n), lambda i, j, k: (k, j, 0))
    o_spec = pl.BlockSpec((tm, tn), lambda i, j, k: (i, j))
    return _mm_call(name, mode, pairs, [(a_spec, b_spec)] * len(pairs), o_spec, jax.ShapeDtypeStruct((M, N), out_dtype),
                    (M // tm, N // tn, J // fold), (tm, tn), res, fold)


def _mm_shard_m(as_, b, name, out_dtype=F32, tm=None, tn=None, tk=512):
    J, K, n = as_.shape
    N = b.shape[1]
    tm = tm or n
    tn = tn or _pick(N, (2048, 1024, 512))
    tk = min(tk, K)
    nn = N // tn
    a_spec = pl.BlockSpec((None, tk, tm), lambda j, i, k: (j, k, i // nn))
    b_spec = pl.BlockSpec((tk, tn), lambda j, i, k: (k, i % nn))
    o_spec = pl.BlockSpec((None, tm, tn), lambda j, i, k: (j, i // nn, i % nn))
    return _mm_call(name, "tn", [(as_, b)], [(a_spec, b_spec)], o_spec, jax.ShapeDtypeStruct((J, n, N), out_dtype),
                    (J, (n // tm) * nn, K // tk), (tm, tn))


def _out_proj_dx(dh, x, g, dres, w, name):
    T, D = x.shape
    N = w.shape[0]
    tm = min(256, T)

    def body(dh_ref, x_ref, g_ref, dres_ref, w_ref, o_ref, dx_ref, dg_ref):
        @pl.when(pl.program_id(0) == 0)
        def _():
            dg_ref[...] = jnp.zeros_like(dg_ref)

        xv, dhv = x_ref[...], dh_ref[...]
        r = lax.rsqrt(jnp.mean(xv * xv, axis=-1, keepdims=True) + RMS_EPS)
        y = xv * r
        dy = dhv * g_ref[...]
        dx = dres_ref[...] + r * (dy - y * jnp.mean(dy * y, axis=-1, keepdims=True))
        dx_ref[...] = dx
        dg_ref[...] += jnp.sum(dhv * y, axis=0, keepdims=True)
        o_ref[...] = _nt(dx.astype(BF16), w_ref[...])

    row = pl.BlockSpec((tm, D), lambda i: (i, 0))
    vec = pl.BlockSpec((1, D), lambda i: (0, 0))
    return pl.pallas_call(
        body, name=name, grid=(T // tm,), in_specs=[row, row, vec, row, pl.BlockSpec((N, D), lambda i: (0, 0))],
        out_specs=[pl.BlockSpec((tm, N), lambda i: (i, 0)), row, vec],
        out_shape=[jax.ShapeDtypeStruct((T, N), F32), jax.ShapeDtypeStruct((T, D), F32), jax.ShapeDtypeStruct((1, D), F32)],
        compiler_params=_params(("arbitrary",)),
    )(dh, x, g.reshape(1, D), dres, w)


def _in_proj_dw(pieces, h, name):
    T, D = h.shape
    tm = 256
    nbs = [p.shape[1] // tm for p in pieces]
    los = [sum(nbs[:t]) for t in range(len(pieces))]

    def body(*refs):
        h_ref, o_ref = refs[-2:]
        i = pl.program_id(0)
        for p_ref, lo, nb in zip(refs[:-2], los, nbs):
            @pl.when((i >= lo) & (i < lo + nb))
            def _():
                o_ref[...] = _tn(p_ref[...], h_ref[...]).astype(BF16)

    specs = [pl.BlockSpec((T, tm), (lambda lo, nb: lambda i: (0, jnp.clip(i - lo, 0, nb - 1)))(lo, nb)) for lo, nb in zip(los, nbs)]
    return pl.pallas_call(body, name=name, grid=(sum(nbs),), in_specs=specs + [pl.BlockSpec((T, D), lambda i: (0, 0))],
                          out_specs=pl.BlockSpec((tm, D), lambda i: (i, 0)), out_shape=jax.ShapeDtypeStruct((sum(nbs) * tm, D), BF16),
                          compiler_params=_params(("parallel",)))(*pieces, h)


def _in_proj_dx(pieces, w, name, tm=512, tn=1024):
    T = pieces[0].shape[0]
    K, D = w.shape
    tm = min(tm, T)
    widths = [p.shape[1] for p in pieces]
    offs = [sum(widths[:t]) for t in range(len(pieces))]

    def body(*refs):
        w_ref, o_ref = refs[-2:]
        acc = None
        for p_ref, off, wd in zip(refs[:-2], offs, widths):
            d = _nn(p_ref[...], w_ref[off:off + wd, :])
            acc = d if acc is None else acc + d
        o_ref[...] = acc

    specs = [pl.BlockSpec((tm, wd), lambda i, j: (i, 0)) for wd in widths]
    return pl.pallas_call(body, name=name, grid=(T // tm, D // tn), in_specs=specs + [pl.BlockSpec((K, tn), lambda i, j: (0, j))],
                          out_specs=pl.BlockSpec((tm, tn), lambda i, j: (i, j)), out_shape=jax.ShapeDtypeStruct((T, D), F32),
                          compiler_params=_params(("parallel", "parallel")))(*pieces, w)


def _seg_matrix(width):
    return jnp.asarray(np.kron(np.eye(width // HEAD_DIM, dtype=np.float32), np.full((HEAD_DIM, HEAD_DIM), 1.0 / HEAD_DIM, np.float32)), BF16)


def _segmean(v, p):
    hi = v.astype(BF16)
    r = v - hi.astype(F32)
    mid = r.astype(BF16)
    lo = (r - mid.astype(F32)).astype(BF16)
    w = min(256, v.shape[1])
    pw = p[:w, :w]
    halves = []
    for c in range(v.shape[1] // w):
        cols = slice(c * w, (c + 1) * w)
        halves.append(_nn(hi[:, cols], pw) + _nn(mid[:, cols], pw) + _nn(lo[:, cols], pw))
    return halves[0] if len(halves) == 1 else jnp.concatenate(halves, axis=1)


def _gelu(x):
    c0 = math.sqrt(2.0 / math.pi)
    t = jnp.tanh(c0 * (x + 0.044715 * x * x * x))
    return 0.5 * x * (1.0 + t), t


def _gelu_grad(x, t):
    c0 = math.sqrt(2.0 / math.pi)
    return 0.5 * (1.0 + t) + 0.5 * x * (1.0 - t * t) * c0 * (1.0 + 3.0 * 0.044715 * x * x)


def _sigmoid(x):
    return 1.0 / (1.0 + jnp.exp(-x))


def _rms_fwd(x, g, name):
    T, D = x.shape
    tm = min(256, T)

    def body(x_ref, g_ref, o_ref):
        o_ref[...] = _rms_rows(x_ref[...], g_ref[...])

    return pl.pallas_call(
        body, name=name, grid=(T // tm,),
        in_specs=[pl.BlockSpec((tm, D), lambda i: (i, 0)), pl.BlockSpec((1, D), lambda i: (0, 0))],
        out_specs=pl.BlockSpec((tm, D), lambda i: (i, 0)), out_shape=jax.ShapeDtypeStruct((T, D), BF16),
        compiler_params=_params(("parallel",)),
    )(x, g.reshape(1, D))


def _rms_bwd(dh, x, g, dres, name):
    T, D = x.shape
    tm = min(256, T)

    def body(dh_ref, x_ref, g_ref, dres_ref, dx_ref, dg_ref):
        @pl.when(pl.program_id(0) == 0)
        def _():
            dg_ref[...] = jnp.zeros_like(dg_ref)

        xv, dhv = x_ref[...], dh_ref[...]
        r = lax.rsqrt(jnp.mean(xv * xv, axis=-1, keepdims=True) + RMS_EPS)
        y = xv * r
        dy = dhv * g_ref[...]
        dx_ref[...] = dres_ref[...] + r * (dy - y * jnp.mean(dy * y, axis=-1, keepdims=True))
        dg_ref[...] += jnp.sum(dhv * y, axis=0, keepdims=True)

    row = pl.BlockSpec((tm, D), lambda i: (i, 0))
    vec = pl.BlockSpec((1, D), lambda i: (0, 0))
    return pl.pallas_call(
        body, name=name, grid=(T // tm,), in_specs=[row, row, vec, row], out_specs=[row, vec],
        out_shape=[jax.ShapeDtypeStruct((T, D), F32), jax.ShapeDtypeStruct((1, D), F32)],
        compiler_params=_params(("arbitrary",)),
    )(dh, x, g.reshape(1, D), dres)


def _sgu_core(zu, zv, ws_ref, bias, p):
    ug, tu = _gelu(zu)
    vg, tv = _gelu(zv)
    xc = vg - _segmean(vg, p)
    rs = lax.rsqrt(_segmean(xc * xc, p) + LN_EPS)
    vn = xc * rs
    vnb = vn.astype(BF16)
    low = lax.broadcasted_iota(jnp.int32, (SGU_CHUNK, 128), 1) < HEAD_DIM
    parts = []
    for j in range(4):
        vp = vnb[:, 128 * j:128 * (j + 1)]
        parts.append(jnp.where(low, _nn(ws_ref[2 * j], vp), _nn(ws_ref[2 * j + 1], vp)))
    mixed = jnp.concatenate(parts, axis=1) + bias
    return ug, tu, tv, rs, vn, vnb, mixed, low


SGU_ROWS = 4 * SGU_CHUNK


def _sgu_fwd(z, ws, bias, name):
    T = z.shape[0]

    def body(zu_ref, zv_ref, ws_ref, b_ref, p_ref, y_ref):
        for r in range(0, SGU_ROWS, SGU_CHUNK):
            rows = slice(r, r + SGU_CHUNK)
            ug, _, _, _, _, _, mixed, _ = _sgu_core(zu_ref[rows, :], zv_ref[rows, :], ws_ref, b_ref[...], p_ref[...])
            y_ref[rows, :] = ug * mixed

    full = lambda shape: pl.BlockSpec(shape, lambda i: (0,) * len(shape))
    return pl.pallas_call(
        body, name=name, grid=(T // SGU_ROWS,),
        in_specs=[pl.BlockSpec((SGU_ROWS, 512), lambda i: (i, COL_AU)), pl.BlockSpec((SGU_ROWS, 512), lambda i: (i, COL_AV)),
                  full((8, 128, 128)), full((128, 512)), full((512, 512))],
        out_specs=pl.BlockSpec((SGU_ROWS, 512), lambda i: (i, 0)), out_shape=jax.ShapeDtypeStruct((T, 512), F32),
        compiler_params=_params(("parallel",)),
    )(z, z, ws, bias, _seg_matrix(512))


def _sgu_bwd(z, dy, ws, ws_t, bias, name):
    T = z.shape[0]

    def body(zu_ref, zv_ref, dy_ref, ws_ref, wst_ref, b_ref, p_ref, dz_ref, dws_ref, db_ref):
        @pl.when(pl.program_id(0) == 0)
        def _():
            dws_ref[...] = jnp.zeros_like(dws_ref)
            db_ref[...] = jnp.zeros_like(db_ref)

        p = p_ref[...]
        zero = jnp.zeros((SGU_CHUNK, 128), BF16)
        dws = [None] * 8
        db = None
        for r in range(0, SGU_ROWS, SGU_CHUNK):
            rows = slice(r, r + SGU_CHUNK)
            zu, zv = zu_ref[rows, :], zv_ref[rows, :]
            ug, tu, tv, rs, vn, vnb, mixed, low = _sgu_core(zu, zv, ws_ref, b_ref[...], p)
            dyv = dy_ref[rows, :]
            dmixed = dyv * ug
            db = dmixed if db is None else db + dmixed
            dmb = dmixed.astype(BF16)
            parts = []
            for j in range(4):
                dmp, vp = dmb[:, 128 * j:128 * (j + 1)], vnb[:, 128 * j:128 * (j + 1)]
                for g, d in ((2 * j, _nt(jnp.where(low, dmp, zero), vp)), (2 * j + 1, _nt(jnp.where(low, zero, dmp), vp))):
                    dws[g] = d if dws[g] is None else dws[g] + d
                parts.append(jnp.where(low, _nn(wst_ref[2 * j], dmp), _nn(wst_ref[2 * j + 1], dmp)))
            dvn = jnp.concatenate(parts, axis=1)
            dvg = rs * (dvn - _segmean(dvn, p) - vn * _segmean(dvn * vn, p))
            dz_ref[rows, 0:512] = (dyv * mixed * _gelu_grad(zu, tu)).astype(BF16)
            dz_ref[rows, 512:1024] = (dvg * _gelu_grad(zv, tv)).astype(BF16)
        db_ref[...] += db
        for g in range(8):
            dws_ref[g] += dws[g]

    full = lambda shape: pl.BlockSpec(shape, lambda i: (0,) * len(shape))
    return pl.pallas_call(
        body, name=name, grid=(T // SGU_ROWS,),
        in_specs=[pl.BlockSpec((SGU_ROWS, 512), lambda i: (i, COL_AU)), pl.BlockSpec((SGU_ROWS, 512), lambda i: (i, COL_AV)),
                  pl.BlockSpec((SGU_ROWS, 512), lambda i: (i, 0)), full((8, 128, 128)), full((8, 128, 128)), full((128, 512)), full((512, 512))],
        out_specs=[pl.BlockSpec((SGU_ROWS, 1024), lambda i: (i, 0)), full((8, 128, 128)), full((128, 512))],
        out_shape=[jax.ShapeDtypeStruct((T, 1024), BF16), jax.ShapeDtypeStruct((8, 128, 128), F32), jax.ShapeDtypeStruct((128, 512), F32)],
        compiler_params=_params(("arbitrary",)),
    )(z, z, dy, ws, ws_t, bias, _seg_matrix(512))


CONV_ROWS = 256


def _conv_taps(pad_ref, w_ref, base, flip):
    acc = None
    for k in range(CONV_WIDTH):
        wk = w_ref[CONV_WIDTH - 1 - k if flip else k]
        t = wk * pad_ref[base + k + 1:base + k + 1 + CONV_ROWS, :]
        acc = t if acc is None else acc + t
    return acc


def _conv_fwd1(z, w, cb, B, S, name):
    T = B * S
    rows = min(CONV_ROWS, S)
    assert rows == CONV_ROWS

    def body(a_ref, g_ref, w_ref, cb_ref, c_ref, pad):
        pad[0:CONV_PAD, :] = jnp.zeros((CONV_PAD, 128), F32)
        pad[CONV_PAD + S:2 * CONV_PAD + S, :] = jnp.zeros((CONV_PAD, 128), F32)
        pad[CONV_PAD:CONV_PAD + S, :] = a_ref[...] * _sigmoid(g_ref[...])

        for base in range(0, S, CONV_ROWS):
            c_ref[base:base + CONV_ROWS, :] = _conv_taps(pad, w_ref, base, False) + cb_ref[...]

    return pl.pallas_call(
        body, name=name, grid=(4, B),
        in_specs=[pl.BlockSpec((S, 128), lambda j, b: (b, 4 * COL_CA + j)), pl.BlockSpec((S, 128), lambda j, b: (b, 4 * COL_CG + j)),
                  pl.BlockSpec((32, 1, 128), lambda j, b: (0, 0, j)), pl.BlockSpec((1, 128), lambda j, b: (0, j))],
        out_specs=pl.BlockSpec((S, 128), lambda j, b: (b, j)), out_shape=jax.ShapeDtypeStruct((T, 512), F32),
        scratch_shapes=[pltpu.VMEM((S + 2 * CONV_PAD, 128), F32)], compiler_params=_params(("parallel", "parallel")),
    )(z, z, w, cb)


def _ln_rows(c):
    mu = jnp.mean(c, axis=-1, keepdims=True)
    xc = c - mu
    rs = lax.rsqrt(jnp.mean(xc * xc, axis=-1, keepdims=True) + LN_EPS)
    return xc * rs, rs


def _conv_bwd2(z, dc, w, B, S, name):
    T = B * S

    def body(a_ref, g_ref, dc_ref, w_ref, da_ref, dg_ref, dw_ref, hpad, dpad):
        @pl.when(pl.program_id(1) == 0)
        def _():
            dw_ref[...] = jnp.zeros_like(dw_ref)

        zeros = jnp.zeros((CONV_PAD, 128), F32)
        for ref in (hpad, dpad):
            ref[0:CONV_PAD, :] = zeros
            ref[CONV_PAD + S:2 * CONV_PAD + S, :] = zeros
        hpad[CONV_PAD:CONV_PAD + S, :] = a_ref[...] * _sigmoid(g_ref[...])
        dpad[CONV_PAD:CONV_PAD + S, :] = dc_ref[...]
        dws = [None] * CONV_WIDTH
        for base in range(0, S, CONV_ROWS):
            rows = slice(base, base + CONV_ROWS)
            dh = _conv_taps(dpad, w_ref, base, True)
            sg = _sigmoid(g_ref[rows, :])
            da_ref[rows, :] = (dh * sg).astype(BF16)
            dg_ref[rows, :] = (dh * a_ref[rows, :] * sg * (1.0 - sg)).astype(BF16)
            dcv = dc_ref[rows, :]
            for k in range(CONV_WIDTH):
                prod = dcv * hpad[base + k + 1:base + k + 1 + CONV_ROWS, :]
                part = jnp.sum(prod.reshape(CONV_ROWS // 8, 8, 128), axis=0)
                dws[k] = part if dws[k] is None else dws[k] + part
        for k in range(CONV_WIDTH):
            dw_ref[k] += jnp.sum(dws[k], axis=0, keepdims=True)

    return pl.pallas_call(
        body, name=name, grid=(4, B),
        in_specs=[pl.BlockSpec((S, 128), lambda j, b: (b, 4 * COL_CA + j)), pl.BlockSpec((S, 128), lambda j, b: (b, 4 * COL_CG + j)),
                  pl.BlockSpec((S, 128), lambda j, b: (b, j)), pl.BlockSpec((32, 1, 128), lambda j, b: (0, 0, j))],
        out_specs=[pl.BlockSpec((S, 128), lambda j, b: (b, j)), pl.BlockSpec((S, 128), lambda j, b: (b, j)),
                   pl.BlockSpec((32, 1, 128), lambda j, b: (0, 0, j))],
        out_shape=[jax.ShapeDtypeStruct((T, 512), BF16), jax.ShapeDtypeStruct((T, 512), BF16), jax.ShapeDtypeStruct((32, 1, 512), F32)],
        scratch_shapes=[pltpu.VMEM((S + 2 * CONV_PAD, 128), F32), pltpu.VMEM((S + 2 * CONV_PAD, 128), F32)],
        compiler_params=_params(("parallel", "arbitrary")),
    )(z, z, dc, w)


def _swap16(x):
    n = x.shape[1]
    first = (lax.broadcasted_iota(jnp.int32, x.shape, 1) % 32) < 16
    return jnp.where(first, pltpu.roll(x, n - 16, 1), pltpu.roll(x, 16, 1))


def _rope(x, cos, sin):
    return x * cos + _swap16(x) * sin


def _rope_t(dy, cos, sin):
    return dy * cos + _swap16(dy * sin)


def _qk_norm(x, p):
    r = lax.rsqrt(_segmean(x * x, p) + RMS_EPS)
    return x * r, r


def _store_heads(ref, val, n):
    for h in range(n):
        ref[h] = val[:, HEAD_DIM * h:HEAD_DIM * (h + 1)].astype(ref.dtype)


def _load_heads(ref, n):
    return jnp.concatenate([ref[h] for h in range(n)], axis=1)


def _prep_fwd(z, gq, gk, rope, B, S, kv_heads, cols, name):
    tm = min(ROW_TILE, S)
    ns = S // tm
    kw = kv_heads * HEAD_DIM
    scale = HEAD_DIM ** -0.5
    qc, kc, vc = cols

    def body(*refs):
        if rope is None:
            q_ref, k_ref, v_ref, gq_ref, gk_ref, p_ref, qo, ko, vo = refs
        else:
            q_ref, k_ref, v_ref, gq_ref, gk_ref, p_ref, cos_ref, sin_ref, qo, ko, vo = refs
        p = p_ref[...]
        qn, _ = _qk_norm(q_ref[...], p)
        kn, _ = _qk_norm(k_ref[...], p[:kw, :kw])
        qn, kn = qn * gq_ref[...], kn * gk_ref[...]
        if rope is not None:
            cos, sin = cos_ref[...], sin_ref[...]
            qn, kn = _rope(qn, cos, sin), _rope(kn, cos[:, :kw], sin[:, :kw])
        _store_heads(qo, qn * scale, N_HEADS)
        _store_heads(ko, kn, kv_heads)
        _store_heads(vo, v_ref[...], kv_heads)

    row = lambda w, c: pl.BlockSpec((tm, w), lambda b, i: (b * ns + i, c))
    const = lambda shape: pl.BlockSpec(shape, lambda b, i: (0,) * len(shape))
    heads = lambda n: pl.BlockSpec((None, n, tm, HEAD_DIM), lambda b, i: (b, 0, i, 0))
    ins = [z, z, z, gq, gk, _seg_matrix(512)]
    specs = [row(512, qc), row(kw, kc), row(kw, vc), const((1, 512)), const((1, kw)), const((512, 512))]
    if rope is not None:
        ins += list(rope)
        specs += [pl.BlockSpec((tm, 512), lambda b, i: (i, 0))] * 2
    return pl.pallas_call(
        body, name=name, grid=(B, ns), in_specs=specs, out_specs=[heads(N_HEADS), heads(kv_heads), heads(kv_heads)],
        out_shape=[jax.ShapeDtypeStruct((B, N_HEADS, S, HEAD_DIM), BF16), jax.ShapeDtypeStruct((B, kv_heads, S, HEAD_DIM), BF16),
                   jax.ShapeDtypeStruct((B, kv_heads, S, HEAD_DIM), BF16)],
        compiler_params=_params(("parallel", "parallel")),
    )(*ins)


def _prep_bwd(z, dq, dk, dv, gq, gk, rope, B, S, kv_heads, cols, name):
    T = B * S
    tm = min(ROW_TILE, S)
    ns = S // tm
    kw = kv_heads * HEAD_DIM
    scale = HEAD_DIM ** -0.5
    qc, kc, _ = cols

    def body(*refs):
        if rope is None:
            q_ref, k_ref, dq_ref, dk_ref, dv_ref, gq_ref, gk_ref, p_ref, dz_ref, dgq_ref, dgk_ref = refs
        else:
            q_ref, k_ref, dq_ref, dk_ref, dv_ref, gq_ref, gk_ref, p_ref, cos_ref, sin_ref, dz_ref, dgq_ref, dgk_ref = refs

        @pl.when((pl.program_id(0) == 0) & (pl.program_id(1) == 0))
        def _():
            dgq_ref[...] = jnp.zeros_like(dgq_ref)
            dgk_ref[...] = jnp.zeros_like(dgk_ref)

        p = p_ref[...]
        dqv = _load_heads(dq_ref, N_HEADS) * scale
        dkv = _load_heads(dk_ref, kv_heads)
        if rope is not None:
            cos, sin = cos_ref[...], sin_ref[...]
            dqv, dkv = _rope_t(dqv, cos, sin), _rope_t(dkv, cos[:, :kw], sin[:, :kw])

        def through_norm(xv, dy, g, pm, dg_ref):
            xh, r = _qk_norm(xv, pm)
            dg_ref[...] += jnp.sum(dy * xh, axis=0, keepdims=True)
            dxh = dy * g
            return r * (dxh - xh * _segmean(dxh * xh, pm))

        dz_ref[:, 0:512] = through_norm(q_ref[...], dqv, gq_ref[...], p, dgq_ref).astype(BF16)
        dz_ref[:, 512:512 + kw] = through_norm(k_ref[...], dkv, gk_ref[...], p[:kw, :kw], dgk_ref).astype(BF16)
        dz_ref[:, 512 + kw:512 + 2 * kw] = _load_heads(dv_ref, kv_heads).astype(BF16)

    row = lambda w, c: pl.BlockSpec((tm, w), lambda b, i: (b * ns + i, c))
    const = lambda shape: pl.BlockSpec(shape, lambda b, i: (0,) * len(shape))
    heads = lambda n: pl.BlockSpec((None, n, tm, HEAD_DIM), lambda b, i: (b, 0, i, 0))
    ins = [z, z, dq, dk, dv, gq, gk, _seg_matrix(512)]
    specs = [row(512, qc), row(kw, kc), heads(N_HEADS), heads(kv_heads), heads(kv_heads), const((1, 512)), const((1, kw)), const((512, 512))]
    if rope is not None:
        ins += list(rope)
        specs += [pl.BlockSpec((tm, 512), lambda b, i: (i, 0))] * 2
    return pl.pallas_call(
        body, name=name, grid=(B, ns), in_specs=specs, out_specs=[row(512 + 2 * kw, 0), const((1, 512)), const((1, kw))],
        out_shape=[jax.ShapeDtypeStruct((T, 512 + 2 * kw), BF16), jax.ShapeDtypeStruct((1, 512), F32), jax.ShapeDtypeStruct((1, kw), F32)],
        compiler_params=_params(("arbitrary", "arbitrary")),
    )(*ins)


def _toeplitz(win, tq, S):
    r = pltpu.roll(jnp.broadcast_to(win, (tq, S + tq)), 0, 1, stride=1, stride_axis=0)
    return r[:, tq:tq + S]


ATTN_HEADS = 4


def _attn_fwd(q, k, v, win, name, nh=ATTN_HEADS, tq=ATTN_TQ):
    B, H, S, _ = q.shape
    shared = k.shape[1] != H
    assert not shared or H // k.shape[1] == nh
    tq = min(tq, S)

    def body(*refs):
        if win is None:
            q_ref, k_ref, v_ref, o_ref = refs
        else:
            q_ref, k_ref, v_ref, w_ref, o_ref = refs
        kvs = [(k_ref[...], v_ref[...])] * nh if shared else [(k_ref[h], v_ref[h]) for h in range(nh)]
        scores = []
        for h in range(nh):
            s = _nt(q_ref[h], kvs[h][0])
            if win is not None:
                s = s + _toeplitz(w_ref[h], tq, S)
            scores.append(s)
        probs = []
        for s in scores:
            p = jnp.exp(s - jnp.max(s, axis=-1, keepdims=True))
            probs.append((p.astype(BF16), jnp.sum(p, axis=-1, keepdims=True)))
        for h, (p, l) in enumerate(probs):
            o_ref[h] = _nn(p, kvs[h][1]) / l

    qs = pl.BlockSpec((None, nh, tq, HEAD_DIM), lambda b, h, i: (b, h, i, 0))
    ks = (pl.BlockSpec((None, None, S, HEAD_DIM), lambda b, h, i: (b, h, 0, 0)) if shared
          else pl.BlockSpec((None, nh, S, HEAD_DIM), lambda b, h, i: (b, h, 0, 0)))
    ins, specs = [q, k, v], [qs, ks, ks]
    if win is not None:
        ins.append(win)
        specs.append(pl.BlockSpec((nh, None, 1, S + tq), lambda b, h, i: (h, i, 0, 0)))
    return pl.pallas_call(body, name=name, grid=(B, H // nh, S // tq), in_specs=specs, out_specs=qs,
                          out_shape=jax.ShapeDtypeStruct((B, H, S, HEAD_DIM), F32),
                          compiler_params=_params(("parallel", "parallel", "parallel")))(*ins)


def _attn_bwd(q, k, v, o, do, win, name, nh=ATTN_HEADS, tq=ATTN_TQ):
    B, H, S, _ = q.shape
    hkv = k.shape[1]
    shared = hkv != H
    assert not shared or H // hkv == nh
    tq = min(tq, S)
    nq = S // tq

    def body(*refs):
        if win is None:
            q_ref, k_ref, v_ref, o_ref, do_ref, dq_ref, dk_ref, dv_ref = refs
        else:
            q_ref, k_ref, v_ref, o_ref, do_ref, w_ref, rev_ref, dq_ref, dk_ref, dv_ref, dw_ref = refs

        @pl.when(pl.program_id(2) == 0)
        def _():
            dk_ref[...] = jnp.zeros_like(dk_ref)
            dv_ref[...] = jnp.zeros_like(dv_ref)

        kvs = [(k_ref[...], v_ref[...])] * nh if shared else [(k_ref[h], v_ref[h]) for h in range(nh)]
        qvs, dobs, scores, dps = [], [], [], []
        for h in range(nh):
            qv, dov = q_ref[h], do_ref[h]
            dob = dov.astype(BF16)
            s = _nt(qv, kvs[h][0])
            if win is not None:
                s = s + _toeplitz(w_ref[h], tq, S)
            dp = _nt(dob, kvs[h][1]) - jnp.sum(dov * o_ref[h], axis=-1, keepdims=True)
            qvs.append(qv)
            dobs.append(dob)
            scores.append(s)
            dps.append(dp)
        pbs, dsbs = [], []
        for s, dp in zip(scores, dps):
            p = jnp.exp(s - jnp.max(s, axis=-1, keepdims=True))
            p = p * (1.0 / jnp.sum(p, axis=-1, keepdims=True))
            pbs.append(p.astype(BF16))
            dsbs.append((p * dp).astype(BF16))
        dk_acc = dv_acc = None
        for h in range(nh):
            dvh, dkh = _tn(pbs[h], dobs[h]), _tn(dsbs[h], qvs[h])
            dq_ref[h] = _nn(dsbs[h], kvs[h][0])
            if shared:
                dv_acc = dvh if dv_acc is None else dv_acc + dvh
                dk_acc = dkh if dk_acc is None else dk_acc + dkh
            else:
                dv_ref[h] += dvh
                dk_ref[h] += dkh
            if win is not None:
                rev = _nn(rev_ref[...], dsbs[h])
                wide = jnp.concatenate([rev, jnp.zeros((tq, tq), F32)], axis=1)
                dw_ref[h] = jnp.sum(pltpu.roll(wide, 0, 1, stride=1, stride_axis=0), axis=0, keepdims=True)
        if shared:
            dv_ref[...] += dv_acc
            dk_ref[...] += dk_acc

    qs = pl.BlockSpec((None, nh, tq, HEAD_DIM), lambda b, h, i: (b, h, i, 0))
    ks = (pl.BlockSpec((None, None, S, HEAD_DIM), lambda b, h, i: (b, h, 0, 0)) if shared
          else pl.BlockSpec((None, nh, S, HEAD_DIM), lambda b, h, i: (b, h, 0, 0)))
    ins, specs = [q, k, v, o, do], [qs, ks, ks, qs, qs]
    outs = [jax.ShapeDtypeStruct((B, H, S, HEAD_DIM), F32), jax.ShapeDtypeStruct((B, hkv, S, HEAD_DIM), F32), jax.ShapeDtypeStruct((B, hkv, S, HEAD_DIM), F32)]
    ospecs = [qs, ks, ks]
    if win is not None:
        ins += [win, jnp.asarray(np.eye(tq, dtype=np.float32)[::-1].copy(), BF16)]
        specs += [pl.BlockSpec((nh, None, 1, S + tq), lambda b, h, i: (h, i, 0, 0)), pl.BlockSpec((tq, tq), lambda b, h, i: (0, 0))]
        outs.append(jax.ShapeDtypeStruct((B, H, nq, 1, S + tq), F32))
        ospecs.append(pl.BlockSpec((None, nh, None, 1, S + tq), lambda b, h, i: (b, h, i, 0, 0)))
    return pl.pallas_call(body, name=name, grid=(B, H // nh, nq), in_specs=specs, out_specs=ospecs, out_shape=outs,
                          compiler_params=_params(("parallel", "parallel", "arbitrary")))(*ins)


def _pattern_count(delta):
    n = jnp.zeros(delta.shape, jnp.int32)
    for window, dil in DIL_PATTERNS:
        n = n + ((delta % dil == 0) & (jnp.abs(delta) <= window // 2)).astype(jnp.int32)
    return n


def _t5_bucket(rel):
    nb = REL_BUCKETS // 2
    max_exact = nb // 2
    ret = jnp.where(rel > 0, nb, 0)
    n = jnp.abs(rel)
    nf = jnp.maximum(n, 1).astype(F32)
    large = max_exact + (jnp.log(nf / max_exact) / math.log(REL_MAX_DIST / max_exact) * (nb - max_exact)).astype(jnp.int32)
    large = jnp.minimum(large, nb - 1)
    return ret + jnp.where(n < max_exact, n, large)


def _bias_windows(rel_bias, S):
    tq = min(ATTN_TQ, S)
    nq = S // tq
    n = nq * (S + tq)
    delta = (jnp.arange(S + tq)[None, :] - (jnp.arange(nq)[:, None] + 1) * tq).reshape(n)
    count = _pattern_count(delta)
    onehot = (_t5_bucket(delta)[None, :] == jnp.arange(REL_BUCKETS)[:, None]).astype(F32)
    extra = jnp.where(count > 0, jnp.log(jnp.maximum(count, 1).astype(F32)), MASKED).reshape(1, n)
    live = (count > 0).astype(F32).reshape(1, n)

    def body(t_ref, oh_ref, live_ref, extra_ref, o_ref):
        o_ref[...] = _nn(t_ref[...], oh_ref[...], HIGHEST) * live_ref[...] + extra_ref[...]

    val = pl.pallas_call(body, name="bias_windows", out_shape=jax.ShapeDtypeStruct((N_HEADS, n), F32),
                         compiler_params=_params())(rel_bias.T, onehot, live, extra)
    return val.reshape(N_HEADS, nq, 1, S + tq)


def _bias_fold(dwin, S, name):
    B, H, nq = dwin.shape[:3]
    tq = min(ATTN_TQ, S)
    n = nq * (S + tq)
    delta = (jnp.arange(S + tq)[None, :] - (tq - 1) - jnp.arange(nq)[:, None] * tq).reshape(n)
    onehot = (_t5_bucket(delta)[:, None] == jnp.arange(128)[None, :]).astype(F32)

    def body(d_ref, oh_ref, o_ref):
        tot = d_ref[0]
        for b in range(1, B):
            tot = tot + d_ref[b]
        o_ref[...] = _nn(tot, oh_ref[...], HIGHEST)

    out = pl.pallas_call(body, name=name, out_shape=jax.ShapeDtypeStruct((H, 128), F32), compiler_params=_params())(dwin.reshape(B, H, n), onehot)
    return out[:, :REL_BUCKETS].T


def _rope_tables(S):
    half = 16
    freqs = ROPE_THETA ** (-jnp.arange(half, dtype=F32) / half)
    t = jnp.arange(S)
    ang_r = (t // GRID_W).astype(F32)[:, None] * freqs[None, :]
    ang_c = (t % GRID_W).astype(F32)[:, None] * freqs[None, :]
    cos = jnp.concatenate([jnp.cos(ang_r)] * 2 + [jnp.cos(ang_c)] * 2, axis=1)
    sin = jnp.concatenate([-jnp.sin(ang_r), jnp.sin(ang_r), -jnp.sin(ang_c), jnp.sin(ang_c)], axis=1)
    return jnp.tile(cos, (1, N_HEADS)), jnp.tile(sin, (1, N_HEADS))


def _conv_act(c, g, b):
    n, rs = _ln_rows(c)
    t = n * g + b
    return t * _sigmoid(t), n, rs, t


def _mix_fwd(ya, ob, c, od, gain, lng, lnb, B, S, name):
    T = B * S
    tm = min(ROW_TILE, S)
    ns = S // tm

    def body(ya_ref, ob_ref, c_ref, od_ref, g_ref, lg_ref, lb_ref, o_ref):
        yc = _conv_act(c_ref[...], lg_ref[...], lb_ref[...])[0]
        ys = [ya_ref[...], _load_heads(ob_ref, N_HEADS), yc, _load_heads(od_ref, N_HEADS)]
        for m, y in enumerate(ys):
            r = lax.rsqrt(jnp.mean(y * y, axis=-1, keepdims=True) + RMS_EPS)
            o_ref[:, 512 * m:512 * (m + 1)] = (y * r * g_ref[:, 512 * m:512 * (m + 1)]).astype(BF16)

    row = pl.BlockSpec((tm, 512), lambda b, i: (b * ns + i, 0))
    heads = pl.BlockSpec((None, N_HEADS, tm, HEAD_DIM), lambda b, i: (b, 0, i, 0))
    vec = pl.BlockSpec((1, 512), lambda b, i: (0, 0))
    return pl.pallas_call(
        body, name=name, grid=(B, ns), in_specs=[row, heads, row, heads, pl.BlockSpec((1, 2048), lambda b, i: (0, 0)), vec, vec],
        out_specs=pl.BlockSpec((tm, 2048), lambda b, i: (b * ns + i, 0)), out_shape=jax.ShapeDtypeStruct((T, 2048), BF16),
        compiler_params=_params(("parallel", "parallel")),
    )(ya, ob, c, od, gain, lng, lnb)


def _mix_bwd(ya, ob, c, od, dycat, gain, lng, lnb, B, S, name):
    T = B * S
    tm = min(ROW_TILE, S)
    ns = S // tm

    def body(ya_ref, ob_ref, c_ref, od_ref, dy_ref, g_ref, lg_ref, lb_ref, dya_ref, dob_ref, dc_ref, dod_ref, dg_ref, dlg_ref, dlb_ref, dcb_ref):
        @pl.when((pl.program_id(0) == 0) & (pl.program_id(1) == 0))
        def _():
            for ref in (dg_ref, dlg_ref, dlb_ref, dcb_ref):
                ref[...] = jnp.zeros_like(ref)

        yc, n, rs, t = _conv_act(c_ref[...], lg_ref[...], lb_ref[...])
        ys = [ya_ref[...], _load_heads(ob_ref, N_HEADS), yc, _load_heads(od_ref, N_HEADS)]
        outs = [dya_ref, dob_ref, None, dod_ref]
        for m, y in enumerate(ys):
            cols = slice(512 * m, 512 * (m + 1))
            r = lax.rsqrt(jnp.mean(y * y, axis=-1, keepdims=True) + RMS_EPS)
            yh = y * r
            dh = dy_ref[:, cols]
            dg_ref[:, cols] += jnp.sum(dh * yh, axis=0, keepdims=True)
            dyh = dh * g_ref[:, cols]
            dyv = r * (dyh - yh * jnp.mean(dyh * yh, axis=-1, keepdims=True))
            if m == 0:
                outs[m][...] = dyv
            elif m == 2:
                sg = _sigmoid(t)
                dt = dyv * sg * (1.0 + t * (1.0 - sg))
                dlg_ref[...] += jnp.sum(dt * n, axis=0, keepdims=True)
                dlb_ref[...] += jnp.sum(dt, axis=0, keepdims=True)
                dn = dt * lg_ref[...]
                dc = rs * (dn - jnp.mean(dn, axis=-1, keepdims=True) - n * jnp.mean(dn * n, axis=-1, keepdims=True))
                dc_ref[...] = dc
                dcb_ref[...] += jnp.sum(dc, axis=0, keepdims=True)
            else:
                _store_heads(outs[m], dyv, N_HEADS)

    row = pl.BlockSpec((tm, 512), lambda b, i: (b * ns + i, 0))
    heads = pl.BlockSpec((None, N_HEADS, tm, HEAD_DIM), lambda b, i: (b, 0, i, 0))
    vec = pl.BlockSpec((1, 2048), lambda b, i: (0, 0))
    flat = jax.ShapeDtypeStruct((T, 512), F32)
    hm = jax.ShapeDtypeStruct((B, N_HEADS, S, HEAD_DIM), F32)
    v512 = pl.BlockSpec((1, 512), lambda b, i: (0, 0))
    s512 = jax.ShapeDtypeStruct((1, 512), F32)
    return pl.pallas_call(
        body, name=name, grid=(B, ns), in_specs=[row, heads, row, heads, pl.BlockSpec((tm, 2048), lambda b, i: (b * ns + i, 0)), vec, v512, v512],
        out_specs=[row, heads, row, heads, vec, v512, v512, v512],
        out_shape=[flat, hm, flat, hm, jax.ShapeDtypeStruct((1, 2048), F32), s512, s512, s512],
        compiler_params=_params(("arbitrary", "arbitrary")),
    )(ya, ob, c, od, dycat, gain, lng, lnb)


def _ffn_down(gate, up, w_down, res, name, norm=None, target=None):
    J, T, n = gate.shape
    N = w_down.shape[2]
    tm = min(256, T)
    steps = T // tm

    def body(g_ref, u_ref, w_ref, r_ref, *rest):
        x_ref = rest[0] if (norm is not None or target is not None) else None
        o_ref, act_ref = rest[-3:-1] if x_ref is not None else rest[-2:]
        acc = None
        for j in range(J):
            g = g_ref[j].astype(F32)
            a = (g * _sigmoid(g) * u_ref[j].astype(F32)).astype(BF16)
            act_ref[j] = a
            d = _nn(a, w_ref[j])
            acc = d if acc is None else acc + d
        y = acc + r_ref[...]
        if target is None:
            o_ref[...] = y
            if norm is not None:
                rest[-1][...] = _rms_rows(y, x_ref[...])
            return
        err = y - x_ref[...]
        o_ref[...] = err * (1.0 / N)
        loss_ref, i = rest[-1], pl.program_id(0)

        @pl.when(i == 0)
        def _():
            loss_ref[...] = jnp.zeros_like(loss_ref)

        loss_ref[...] += jnp.sum(err * err)

        @pl.when(i == steps - 1)
        def _():
            loss_ref[...] = loss_ref[...] * (0.5 / N)

    gu = pl.BlockSpec((J, tm, n), lambda i: (0, i, 0))
    row = pl.BlockSpec((tm, N), lambda i: (i, 0))
    ins, specs = [gate, up, w_down, res], [gu, gu, pl.BlockSpec((J, n, N), lambda i: (0, 0, 0)), row]
    outs, ospecs = [jax.ShapeDtypeStruct((T, N), F32), jax.ShapeDtypeStruct((J, T, n), BF16)], [row, gu]
    if target is not None:
        ins, specs = ins + [target], specs + [row]
        outs, ospecs = outs + [jax.ShapeDtypeStruct((8, 128), F32)], ospecs + [pl.BlockSpec((8, 128), lambda i: (0, 0))]
    elif norm is not None:
        ins, specs = ins + [norm.reshape(1, N)], specs + [pl.BlockSpec((1, N), lambda i: (0, 0))]
        outs, ospecs = outs + [jax.ShapeDtypeStruct((T, N), BF16)], ospecs + [row]
    return pl.pallas_call(body, name=name, grid=(steps,), in_specs=specs, out_specs=ospecs, out_shape=outs,
                          compiler_params=_params(("arbitrary" if target is not None else "parallel",)))(*ins)


def _ffn_down_dx(dx, w_down, gate, up, name):
    J, n, D = w_down.shape
    T = dx.shape[0]
    tm = min(512, T)

    def body(dx_ref, w_ref, g_ref, u_ref, dg_ref, du_ref):
        d = _nt(dx_ref[...].astype(BF16), w_ref[...])
        g = g_ref[...].astype(F32)
        s = _sigmoid(g)
        dg_ref[...] = (d * u_ref[...].astype(F32) * s * (1.0 + g * (1.0 - s))).astype(BF16)
        du_ref[...] = (d * g * s).astype(BF16)

    blk = pl.BlockSpec((None, tm, n), lambda j, i: (j, i, 0))
    shape = jax.ShapeDtypeStruct((J, T, n), BF16)
    return pl.pallas_call(body, name=name, grid=(J, T // tm),
                          in_specs=[pl.BlockSpec((tm, D), lambda j, i: (i, 0)), pl.BlockSpec((None, n, D), lambda j, i: (j, 0, 0)), blk, blk],
                          out_specs=[blk, blk], out_shape=[shape, shape], compiler_params=_params(("parallel", "parallel")))(dx, w_down, gate, up)


def _row_tile(R):
    best = R
    for cand in range(16, min(R, 272) + 1, 16):
        if R % cand == 0:
            best = cand
    return best


def _adamw(w, m, v, stack, name, layer=None, prev=None):
    n, R, C = stack.shape
    tm = _row_tile(R)
    nb = R // tm
    off = 0 if layer is None else layer * nb
    c1 = 1.0 - ADAM_B1 ** ADAM_STEP
    c2 = 1.0 - ADAM_B2 ** ADAM_STEP

    def body(w_ref, m_ref, v_ref, s_ref, *rest):
        g_ref, d_ref, mo_ref, vo_ref = rest[-4:]
        g = s_ref[0].astype(F32)
        for k in range(1, n):
            g = g + s_ref[k].astype(F32)
        mn = ADAM_B1 * m_ref[...] + (1.0 - ADAM_B1) * g
        vn = ADAM_B2 * v_ref[...] + (1.0 - ADAM_B2) * (g * g)
        g_ref[...] = g
        mo_ref[...] = mn
        vo_ref[...] = vn
        d_ref[...] = -ADAM_LR * ((mn / c1) / (jnp.sqrt(vn / c2) + ADAM_EPS) + ADAM_WD * w_ref[...])

    blk = pl.BlockSpec((tm, C), lambda i: (i + off, 0))
    ins = [w, m, v, stack]
    specs = [blk, blk, blk, pl.BlockSpec((n, tm, C), lambda i: (0, i, 0))]
    aliases = {}
    if prev is not None:
        ins += list(prev)
        specs += [pl.BlockSpec(memory_space=pl.ANY)] * 4
        aliases = {4 + t: t for t in range(4)}
    shape = jax.ShapeDtypeStruct(w.shape, F32)
    return pl.pallas_call(body, name=name, grid=(nb,), in_specs=specs, out_specs=[blk] * 4, out_shape=[shape] * 4,
                          input_output_aliases=aliases, compiler_params=_params(("parallel",)))(*ins)


HBM = pl.BlockSpec(memory_space=pltpu.HBM)
SEM = pl.BlockSpec(memory_space=pltpu.SEMAPHORE)
EFFECT = pltpu.SideEffectType.DATAFLOW_SIDE_EFFECTING


PEERS = {"scatter": (1, 2, 3, 4, 5, 6, 7), "gather": (1, 2, 3, 4, 5, 6, 7), "chips": (1, 2, 4, 6), "forward": (2, 4, 6)}


def _spread_copies(srcs, lands, send_sems, recv_sems, local_sems, kind, waiting):
    x, y, c = lax.axis_index("x"), lax.axis_index("y"), lax.axis_index("c")
    me = 4 * x + 2 * y + c

    def peer(bits):
        dev = (1 - x if bits & 4 else x, 1 - y if bits & 2 else y, 1 - c if bits & 1 else c)
        return dev, 4 * dev[0] + 2 * dev[1] + dev[2]

    plan = PEERS[kind]
    remote, local = [], []
    for a, l in enumerate(lands):
        for d, bits in enumerate(plan):
            dev, pid = peer(bits)
            if kind == "forward":
                src, dst, dev = l.at[pid], l.at[peer(bits | 1)[1] if waiting else pid], peer(1)[0]
            else:
                src, dst = (srcs[a].at[pid] if kind == "scatter" else srcs[a]), l.at[pid if waiting else me]
            remote.append(pltpu.make_async_remote_copy(
                src_ref=src, dst_ref=dst, send_sem=send_sems.at[a * len(plan) + d], recv_sem=recv_sems.at[a * len(plan) + d],
                device_id=dev, device_id_type=MESH_ID))
        if kind != "forward":
            local.append(pltpu.make_async_copy(srcs[a].at[me] if kind == "scatter" else srcs[a], l.at[me], local_sems.at[a]))
    return remote, local


def _spread_start(srcs, kind, name, after=None, lands=None):
    if kind == "forward":
        srcs = []
    else:
        shapes = [a.shape if kind == "scatter" else (N_DEV,) + a.shape for a in srcs]
        lands = [lax.empty(shp, a.dtype) for shp, a in zip(shapes, srcs)]
    ns, nl, per = len(srcs), len(lands), len(PEERS[kind])
    extra = [] if after is None else [after]
    sem_shapes = [pltpu.SemaphoreType.DMA((nl * per,))] * 2 + ([pltpu.SemaphoreType.DMA((nl,))] if ns else [])

    def body(*refs):
        src_refs, land_refs = refs[:ns], refs[ns:ns + nl]
        sems = refs[ns + nl + len(extra):ns + nl + len(extra) + len(sem_shapes)]
        remote, local = _spread_copies(src_refs, land_refs, sems[0], sems[1], sems[2] if ns else None, kind, False)
        for cp in remote + local:
            cp.start()
        refs[-1][...] = jnp.zeros((8, 128), F32)

    outs = pl.pallas_call(
        body, name=name,
        out_shape=(*sem_shapes, *[pltpu.HBM(a.shape, a.dtype) for a in srcs + lands], jax.ShapeDtypeStruct((8, 128), F32)),
        in_specs=[HBM] * (ns + nl) + [pl.BlockSpec(memory_space=pl.ANY)] * len(extra),
        out_specs=(*[SEM] * len(sem_shapes), *[HBM] * (ns + nl), pl.BlockSpec(memory_space=pltpu.VMEM)),
        input_output_aliases={i: len(sem_shapes) + i for i in range(ns + nl)},
        compiler_params=pltpu.CompilerParams(has_side_effects=EFFECT),
    )(*[pltpu.with_memory_space_constraint(a, pltpu.HBM) for a in srcs + lands], *extra)
    k = len(sem_shapes)
    return outs[:k], list(outs[k:k + ns]), list(outs[k + ns:k + ns + nl]), outs[-1]


def _spread_wait(sems, srcs, lands, after, kind, name):
    ns, nl = len(srcs), len(lands)
    after = list(after) if isinstance(after, (list, tuple)) else [after]

    def body(*refs):
        src_refs, land_refs = refs[:ns], refs[ns:ns + nl]
        s = refs[ns + nl:ns + nl + len(sems)]
        remote, local = _spread_copies(src_refs, land_refs, s[0], s[1], s[2] if ns else None, kind, True)
        for cp in remote:
            cp.wait_send()
            cp.wait_recv()
        for cp in local:
            cp.wait()

    outs = pl.pallas_call(
        body, name=name, out_shape=tuple(pltpu.HBM(a.shape, a.dtype) for a in srcs + lands),
        in_specs=[HBM] * (ns + nl) + [SEM] * len(sems) + [pl.BlockSpec(memory_space=pl.ANY)] * len(after), out_specs=tuple([HBM] * (ns + nl)),
        input_output_aliases={i: i for i in range(ns + nl)}, compiler_params=pltpu.CompilerParams(has_side_effects=EFFECT),
    )(*srcs, *lands, *sems, *after)
    return list(outs[ns:])


SMALL = ("rel_bias", "norm1_g", "sgu_w", "sgu_b", "dil_qn_g", "dil_kn_g", "conv_w", "conv_b", "conv_ln_g", "conv_ln_b",
         "gqa_qn_g", "gqa_kn_g", "mix_norm_g", "norm2_g")
LARGE = ("w_in", "w_out", "w_gate", "w_up", "w_down")
EARLY = tuple(k for k in SMALL if k != "norm1_g")


def _local_step(x, target, p, B, S, fetch, emit, mid, early):
    T = B * S
    rope = _rope_tables(S)
    win = _bias_windows(p["rel_bias"], S)
    tile8 = lambda g: jnp.tile(g.reshape(1, HEAD_DIM), (1, N_HEADS))
    cols_b = (COL_BQ, COL_BK, COL_BV)
    cols_d = (COL_DQ, COL_DK128, COL_DV128)
    saved = []
    for l in range(DEPTH):
        s = {"x": x}
        s["ws"] = p["sgu_w"][l].astype(BF16)
        s["bias"] = jnp.repeat(p["sgu_b"][l].T, HEAD_DIM, axis=1)
        s["h"] = _rms_fwd(x, p["norm1_g"][l], f"rms1_fwd_{l}") if l == 0 else h_next
        s["win"] = fetch(l, "in", s["h"])
        s["cw"] = jnp.pad(s["win"]["conv_w"], ((0, 1), (0, 0))).reshape(32, 1, 512)
        z = s["z"] = _matmul(s["h"], s["win"]["w_in"], "nt", f"in_proj_{l}", tk=D_MODEL)
        s["bias"] = s["bias"] + mid(l, z)
        s["ya"] = _sgu_fwd(z, s["ws"], s["bias"], f"sgu_fwd_{l}")
        s["c"] = _conv_fwd1(z, s["cw"], p["conv_b"][l].reshape(1, 512), B, S, f"conv_fwd_{l}")
        s["ln"] = (p["conv_ln_g"][l].reshape(1, 512), p["conv_ln_b"][l].reshape(1, 512))
        s["gb"] = (tile8(p["dil_qn_g"][l]), tile8(p["dil_kn_g"][l]))
        s["gd"] = (tile8(p["gqa_qn_g"][l]), tile8(p["gqa_kn_g"][l])[:, :KV_WIDTH])
        s["qkv_b"] = _prep_fwd(z, *s["gb"], None, B, S, N_HEADS, cols_b, f"prep_b_fwd_{l}")
        s["qkv_d"] = _prep_fwd(z, *s["gd"], rope, B, S, KV_HEADS, cols_d, f"prep_d_fwd_{l}")
        s["ob"] = _attn_fwd(*s["qkv_b"], win, f"attn_b_fwd_{l}")
        s["od"] = _attn_fwd(*s["qkv_d"], None, f"attn_d_fwd_{l}", tq=GQA_TQ)
        s["gmix"] = p["mix_norm_g"][l].reshape(1, 2048)
        s["ycat"] = _mix_fwd(s["ya"], s["ob"], s["c"], s["od"], s["gmix"], *s["ln"], B, S, f"mix_fwd_{l}")
        s["wout"] = fetch(l, "out", s["ycat"])["w_out"]
        x1, s["h2"] = _matmul(s["ycat"], s["wout"], "nn", f"out_proj_{l}", res=x, tk=D_MODEL, norm=p["norm2_g"][l])
        s["x1"] = x1
        s["ffn"] = fetch(l, "ffn", s["h2"])
        s["gate"] = _mm_shard_out(s["h2"], s["ffn"]["w_gate"], "nt", f"ffn_gate_{l}", out_dtype=BF16, tm=1024)
        s["up"] = _mm_shard_out(s["h2"], s["ffn"]["w_up"], "nt", f"ffn_up_{l}", out_dtype=BF16, tm=1024)
        if l + 1 < DEPTH:
            x, s["act"], h_next = _ffn_down(s["gate"], s["up"], s["ffn"]["w_down"], x1, f"ffn_down_{l}", norm=p["norm1_g"][l + 1])
        else:
            dx, s["act"], loss_blk = _ffn_down(s["gate"], s["up"], s["ffn"]["w_down"], x1, f"ffn_down_{l}", target=target)
        saved.append(s)

    g = {k: [None] * DEPTH for k in SMALL if k != "rel_bias"}
    dwin_total = None
    for l in reversed(range(DEPTH)):
        s = saved[l]
        z, ffn = s["z"], s["ffn"]
        dgate, dup = _ffn_down_dx(dx, ffn["w_down"], s["gate"], s["up"], f"ffn_down_dx_{l}")
        tok = emit(l, "w_down", _mm_shard_m(s["act"], dx, f"ffn_down_dw_{l}", out_dtype=BF16, tm=FFN_BLOCK, tn=512, tk=T))
        tok += emit(l, "w_gate", _mm_shard_m(dgate, s["h2"], f"ffn_gate_dw_{l}", out_dtype=BF16, tm=FFN_BLOCK, tn=512, tk=T))
        tok += emit(l, "w_up", _mm_shard_m(dup, s["h2"], f"ffn_up_dw_{l}", out_dtype=BF16, tm=FFN_BLOCK, tn=512, tk=T))
        dh2 = _mm_shard_k([(dgate, ffn["w_gate"]), (dup, ffn["w_up"])], "nn", f"ffn_up_dx_{l}", tn=512, fold=FFN_GROUPS)
        dycat, dx1, dg2 = _out_proj_dx(dh2, s["x1"], p["norm2_g"][l] + tok, dx, s["wout"], f"out_proj_dx_{l}")
        g["norm2_g"][l] = dg2[0]
        tok = emit(l, "w_out", _matmul(s["ycat"], dx1, "tn", f"out_proj_dw_{l}", out_dtype=BF16, tn=1024, tk=T))
        dya, dob, dc, dod, dgm, dlg, dlb, dcb = _mix_bwd(s["ya"], s["ob"], s["c"], s["od"], dycat, s["gmix"] + tok, *s["ln"], B, S, f"mix_bwd_{l}")
        g["mix_norm_g"][l] = dgm[0]
        dz_a, dws, dbias = _sgu_bwd(z, dya, s["ws"], jnp.swapaxes(s["ws"], 1, 2), s["bias"], f"sgu_bwd_{l}")
        g["sgu_w"][l] = dws
        g["sgu_b"][l] = dbias.reshape(128, 8, HEAD_DIM).sum(-1).T
        g["conv_ln_g"][l], g["conv_ln_b"][l], g["conv_b"][l] = dlg[0], dlb[0], dcb[0]
        dz_ca, dz_cg, dcw = _conv_bwd2(z, dc, s["cw"], B, S, f"conv_bwd_{l}")
        g["conv_w"][l] = dcw.reshape(32, 512)[:CONV_WIDTH]
        dq, dk, dv, dwin = _attn_bwd(*s["qkv_b"], s["ob"], dob, win, f"attn_b_bwd_{l}")
        dwin_total = dwin if dwin_total is None else dwin_total + dwin
        dz_b, dgq, dgk = _prep_bwd(z, dq, dk, dv, *s["gb"], None, B, S, N_HEADS, cols_b, f"prep_b_bwd_{l}")
        g["dil_qn_g"][l] = dgq.reshape(N_HEADS, HEAD_DIM).sum(0)
        g["dil_kn_g"][l] = dgk.reshape(N_HEADS, HEAD_DIM).sum(0)
        dq, dk, dv = _attn_bwd(*s["qkv_d"], s["od"], dod, None, f"attn_d_bwd_{l}", tq=GQA_TQ)
        dz_d, dgq, dgk = _prep_bwd(z, dq, dk, dv, *s["gd"], rope, B, S, KV_HEADS, cols_d, f"prep_d_bwd_{l}")
        g["gqa_qn_g"][l] = dgq.reshape(N_HEADS, HEAD_DIM).sum(0)
        g["gqa_kn_g"][l] = dgk.reshape(KV_HEADS, HEAD_DIM).sum(0)
        dz = [dz_a, dz_b, dz_ca, dz_cg, dz_d]
        tok = jnp.zeros((), F32)
        if l == 0:
            done = {k: jnp.stack(v) for k, v in g.items() if k != "norm1_g"}
            done["rel_bias"] = _bias_fold(dwin_total, S, "bias_fold")
            tok = early(done)
        tok += emit(l, "w_in", _in_proj_dw(dz, s["h"], f"in_proj_dw_{l}"))
        dh = _in_proj_dx(dz, s["win"]["w_in"], f"in_proj_dx_{l}")
        dx, dg1 = _rms_bwd(dh, s["x"], p["norm1_g"][l] + tok, dx1, f"rms1_bwd_{l}")
        g["norm1_g"][l] = dg1[0]

    return loss_blk[0, 0], dx, jnp.stack(g["norm1_g"])


GROUPS = {"in": ("w_in",), "out": ("w_out",), "ffn": ("w_gate", "w_up", "w_down")}
COL_SHARDED = ("w_in", "w_gate", "w_up")


def kernel(x, rel_bias, norm1_g, w_in, sgu_w, sgu_b, dil_qn_g, dil_kn_g, conv_w, conv_b, conv_ln_g, conv_ln_b, gqa_qn_g, gqa_kn_g, mix_norm_g, w_out, norm2_g, w_gate, w_up, w_down, loss_target, m_rel_bias, m_norm1_g, m_w_in, m_sgu_w, m_sgu_b, m_dil_qn_g, m_dil_kn_g, m_conv_w, m_conv_b, m_conv_ln_g, m_conv_ln_b, m_gqa_qn_g, m_gqa_kn_g, m_mix_norm_g, m_w_out, m_norm2_g, m_w_gate, m_w_up, m_w_down, v_rel_bias, v_norm1_g, v_w_in, v_sgu_w, v_sgu_b, v_dil_qn_g, v_dil_kn_g, v_conv_w, v_conv_b, v_conv_ln_g, v_conv_ln_b, v_gqa_qn_g, v_gqa_kn_g, v_mix_norm_g, v_w_out, v_norm2_g, v_w_gate, v_w_up, v_w_down):
    w = dict(rel_bias=rel_bias, norm1_g=norm1_g, w_in=w_in, sgu_w=sgu_w, sgu_b=sgu_b, dil_qn_g=dil_qn_g, dil_kn_g=dil_kn_g, conv_w=conv_w,
             conv_b=conv_b, conv_ln_g=conv_ln_g, conv_ln_b=conv_ln_b, gqa_qn_g=gqa_qn_g, gqa_kn_g=gqa_kn_g, mix_norm_g=mix_norm_g,
             w_out=w_out, norm2_g=norm2_g, w_gate=w_gate, w_up=w_up, w_down=w_down)
    m = dict(rel_bias=m_rel_bias, norm1_g=m_norm1_g, w_in=m_w_in, sgu_w=m_sgu_w, sgu_b=m_sgu_b, dil_qn_g=m_dil_qn_g, dil_kn_g=m_dil_kn_g,
             conv_w=m_conv_w, conv_b=m_conv_b, conv_ln_g=m_conv_ln_g, conv_ln_b=m_conv_ln_b, gqa_qn_g=m_gqa_qn_g, gqa_kn_g=m_gqa_kn_g,
             mix_norm_g=m_mix_norm_g, w_out=m_w_out, norm2_g=m_norm2_g, w_gate=m_w_gate, w_up=m_w_up, w_down=m_w_down)
    v = dict(rel_bias=v_rel_bias, norm1_g=v_norm1_g, w_in=v_w_in, sgu_w=v_sgu_w, sgu_b=v_sgu_b, dil_qn_g=v_dil_qn_g, dil_kn_g=v_dil_kn_g,
             conv_w=v_conv_w, conv_b=v_conv_b, conv_ln_g=v_conv_ln_g, conv_ln_b=v_conv_ln_b, gqa_qn_g=v_gqa_qn_g, gqa_kn_g=v_gqa_kn_g,
             mix_norm_g=v_mix_norm_g, w_out=v_w_out, norm2_g=v_norm2_g, w_gate=v_w_gate, w_up=v_w_up, w_down=v_w_down)
    names = list(w)
    B, S, D = x.shape
    T = B * S
    me = 4 * lax.axis_index("x") + 2 * lax.axis_index("y") + lax.axis_index("c")

    view = lambda a, k: jnp.swapaxes(a, 1, 2) if k in COL_SHARDED else a
    bf = {k: view(w[k], k).astype(BF16) for k in LARGE}
    spreads, forwards = {}, {}

    def start_gather(l, group, after=None):
        srcs = [bf[k][l] for k in GROUPS[group]] + ([conv_w[l]] if group == "in" else [])
        spreads[l, group] = _spread_start(srcs, "chips", f"gather_{group}_{l}_start", after)
        return spreads[l, group][3][0, 0]

    def forward(l, group, after):
        sems, srcs, lands, _ = spreads[l, group]
        lands = _spread_wait(sems, srcs, lands, after, "chips", f"gather_{group}_{l}_wait")
        forwards[l, group] = _spread_start(None, "forward", f"forward_{group}_{l}_start", lands=lands)
        return forwards[l, group][3]

    def landed(l, group, after):
        sems, _, lands, _ = forwards[l, group]
        return _spread_wait(sems, [], lands, after, "forward", f"forward_{group}_{l}_wait")

    tok0 = start_gather(0, "in") + start_gather(0, "out") + start_gather(0, "ffn")
    small = {k: w[k] for k in SMALL}
    small["norm1_g"] = norm1_g.at[0].add(tok0)

    def mid(l, z):
        if l > 0:
            return jnp.zeros((), F32)
        return start_gather(1, "in", z) + start_gather(1, "out", z) + start_gather(1, "ffn", z)

    def fetch(l, group, after):
        if group == "in":
            tok = forward(0, "in", after) if l == 0 else after
        elif group == "out":
            tok = forward(l, "ffn", [after, forward(l, "out", after)])
        else:
            tok = forward(1, "in", after) if l == 0 else after
        got = dict(zip(GROUPS[group] + ("conv_w",), landed(l, group, [after, tok])))
        if group == "in":
            got["w_in"] = got["w_in"].reshape(IN_WIDTH, D)
            got["conv_w"] = jnp.transpose(got["conv_w"], (1, 0, 2)).reshape(CONV_WIDTH, 512)
        if group == "out":
            got["w_out"] = got["w_out"].reshape(D, D)
        if group == "ffn":
            got = {k: a.reshape(FFN_GROUPS, -1, D) for k, a in got.items()}
        return got

    scatters = {}

    def emit(l, k, dw):
        dw = dw.reshape(N_DEV, -1, D)
        scatters[l, k] = _spread_start([dw], "scatter", f"scatter_{k}_{l}_start")
        return scatters[l, k][3][0, 0]

    flat2 = lambda a: a.reshape(-1, a.shape[-1])
    small_spread = []

    def early(done):
        small_spread.append(_spread_start([flat2(done[k]) for k in EARLY], "gather", "gather_small_grads_start"))
        return small_spread[0][3][0, 0]

    loss_part, dx, dnorm1 = _local_step(x.reshape(T, D), loss_target.reshape(T, D), small, B, S, fetch, emit, mid, early)
    loss = lax.psum(loss_part, ("x", "y", "c"))

    out_g, out_d, out_m, out_v = {}, {}, {}, {}

    def update_large(k, after):
        shp = view(w[k], k).shape
        two_d = lambda a: view(a, k).reshape(-1, shp[-1])
        res = None
        for l in reversed(range(DEPTH)):
            sems, srcs, lands, _ = scatters[l, k]
            stack = _spread_wait(sems, srcs, lands, after, "scatter", f"scatter_{k}_{l}_wait")[0]
            res = _adamw(two_d(w[k]), two_d(m[k]), two_d(v[k]), stack.reshape(N_DEV, -1, shp[-1]), f"adamw_{k}_{l}", layer=l, prev=res)
        out_g[k], out_d[k], out_m[k], out_v[k] = [view(a.reshape(shp), k) for a in res]
        return res[0]

    late_sems, late_srcs, late_lands, late_tok = _spread_start([flat2(dnorm1)], "gather", "gather_norm1_grad_start")
    after = [dx, late_tok]
    for k in ("w_down", "w_gate", "w_up", "w_out"):
        after = update_large(k, after)
    sems, srcs, lands, _ = small_spread[0]
    stacks = dict(zip(EARLY, _spread_wait(sems, srcs, lands, after, "gather", "gather_small_grads_wait")))
    stacks["norm1_g"] = _spread_wait(late_sems, late_srcs, late_lands, after, "gather", "gather_norm1_grad_wait")[0]
    for k in SMALL:
        stack = stacks[k]
        if k == "conv_w":
            stack = lax.dynamic_slice_in_dim(stack, me * (512 // N_DEV), 512 // N_DEV, axis=2)
        res = _adamw(flat2(w[k]), flat2(m[k]), flat2(v[k]), stack, f"adamw_{k}")
        out_g[k], out_d[k], out_m[k], out_v[k] = [a.reshape(w[k].shape) for a in res]
        after = res[0]
    update_large("w_in", after)

    return (loss, dx.reshape(B, S, D), *[out_g[k] for k in names], *[out_d[k] for k in names],
            *[out_m[k] for k in names], *[out_v[k] for k in names])
```

```python
import functools
import math

import numpy as np
import jax
import jax.numpy as jnp
from jax import lax
from jax.experimental import pallas as pl
from jax.experimental.pallas import tpu as pltpu

F32 = jnp.float32
BF16 = jnp.bfloat16
HIGHEST = lax.Precision.HIGHEST
MESH_ID = pl.DeviceIdType.MESH

D_MODEL = 2048
DEPTH = 2
HEAD_DIM = 64
GROUP_WIDTH = 512
N_HEADS = 8
KV_HEADS = 2
KV_WIDTH = 128
SGU_CHUNK = 128
CONV_WIDTH = 31
CONV_PAD = 16
GRID_W = 64
ROPE_THETA = 10000.0
REL_BUCKETS = 32
REL_MAX_DIST = 1024
DIL_PATTERNS = ((128, 1), (512, 4), (2048, 16))
FFN_HIDDEN = 5632
IN_WIDTH = 4352
RMS_EPS = 1e-6
LN_EPS = 1e-5
MASKED = -1e30
N_DEV = 8

ADAM_LR = 0.001
ADAM_B1 = 0.9
ADAM_B2 = 0.999
ADAM_EPS = 1e-08
ADAM_WD = 0.01
ADAM_STEP = 10

COL_AU, COL_AV, COL_BQ, COL_BK, COL_BV, COL_CA, COL_CG, COL_DQ = range(8)
COL_DK128, COL_DV128 = 32, 33

VMEM_LIMIT = 56 * 1024 * 1024
FFN_GROUPS = 2
FFN_BLOCK = 1408
ATTN_TQ = 256
GQA_TQ = 512
ROW_TILE = 512


def _params(sem=None, vmem=VMEM_LIMIT):
    return pltpu.CompilerParams(dimension_semantics=sem, vmem_limit_bytes=vmem)


def _dot(a, b, dims, precision=None):
    return lax.dot_general(a, b, (dims, ((), ())), precision=precision, preferred_element_type=F32)


def _nn(a, b, precision=None):
    return _dot(a, b, ((1,), (0,)), precision)


def _nt(a, b):
    return _dot(a, b, ((1,), (1,)))


def _tn(a, b):
    return _dot(a, b, ((0,), (0,)))


DIMS = {"nn": ((1,), (0,)), "nt": ((1,), (1,)), "tn": ((0,), (0,))}


def _pick(n, cands):
    for c in cands:
        if n % c == 0:
            return c
    return n


def _mm_call(name, mode, pairs, specs, o_spec, out_sds, grid, acc_shape, res=None, fold=None, norm=None):
    npair, nk, dims = len(pairs), grid[2], DIMS[mode]

    def body(*refs):
        ab = refs[:2 * npair]
        at = 2 * npair
        r_ref = refs[at] if res is not None else None
        at += res is not None
        g_ref = refs[at] if norm is not None else None
        at += norm is not None
        o_ref = refs[at]
        h_ref = refs[at + 1] if norm is not None else None
        part = None
        for t in range(npair):
            for s in ([None] if fold is None else range(fold)):
                a_blk = ab[2 * t][...] if s is None else ab[2 * t][s]
                b_blk = ab[2 * t + 1][...] if s is None else ab[2 * t + 1][s]
                d = _dot(a_blk.astype(BF16), b_blk.astype(BF16), dims)
                part = d if part is None else part + d

        def finish(r):
            if r_ref is not None:
                r = r + r_ref[...]
            o_ref[...] = r.astype(o_ref.dtype)
            if h_ref is not None:
                h_ref[...] = _rms_rows(r, g_ref[...])

        if nk == 1:
            finish(part)
            return
        acc, k = refs[-1], pl.program_id(2)

        @pl.when(k == 0)
        def _():
            acc[...] = part

        @pl.when(k > 0)
        def _():
            acc[...] += part

        @pl.when(k == nk - 1)
        def _():
            finish(acc[...])

    ins = [t for pair in pairs for t in pair]
    in_specs = [t for pair in specs for t in pair]
    if res is not None:
        ins.append(res)
        in_specs.append(o_spec)
    out_specs = o_spec
    if norm is not None:
        ins.append(norm)
        in_specs.append(pl.BlockSpec(norm.shape, lambda *_: (0, 0)))
        out_specs, out_sds = [o_spec, o_spec], [out_sds, jax.ShapeDtypeStruct(out_sds.shape, BF16)]
    return pl.pallas_call(
        body, name=name, grid=grid, in_specs=in_specs, out_specs=out_specs, out_shape=out_sds,
        scratch_shapes=[pltpu.VMEM(acc_shape, F32)] if nk > 1 else [],
        compiler_params=_params(("parallel", "parallel", "arbitrary")),
    )(*ins)


def _rms_rows(x, g):
    return (x * lax.rsqrt(jnp.mean(x * x, axis=-1, keepdims=True) + RMS_EPS) * g).astype(BF16)


def _matmul(a, b, mode, name, res=None, out_dtype=F32, tm=512, tn=None, tk=None, norm=None):
    if mode == "nn":
        (M, K), N = a.shape, b.shape[1]
    elif mode == "nt":
        (M, K), N = a.shape, b.shape[0]
    else:
        (K, M), N = a.shape, b.shape[1]
    tm = min(tm, M)
    tn = tn or _pick(N, (2176, 2048, 1408, 1024, 512))
    tk = tk or _pick(K, (1024, 2176, 1408, 512))
    assert M % tm == 0 and N % tn == 0 and K % tk == 0, (M, N, K, tm, tn, tk)
    a_spec = pl.BlockSpec((tk, tm), lambda i, j, k: (k, i)) if mode == "tn" else pl.BlockSpec((tm, tk), lambda i, j, k: (i, k))
    b_spec = pl.BlockSpec((tn, tk), lambda i, j, k: (j, k)) if mode == "nt" else pl.BlockSpec((tk, tn), lambda i, j, k: (k, j))
    o_spec = pl.BlockSpec((tm, tn), lambda i, j, k: (i, j))
    assert norm is None or tn == N
    return _mm_call(name, mode, [(a, b)], [(a_spec, b_spec)], o_spec, jax.ShapeDtypeStruct((M, N), out_dtype),
                    (M // tm, N // tn, K // tk), (tm, tn), res, norm=None if norm is None else norm.reshape(1, N))


def _mm_shard_out(a, bs, mode, name, out_dtype=F32, tm=512, tk=None):
    J = bs.shape[0]
    n = bs.shape[1] if mode == "nt" else bs.shape[2]
    (K, M) = a.shape if mode == "tn" else a.shape[::-1]
    tm = min(tm, M)
    tk = tk or (K if mode != "tn" else _pick(K, (1024, 512)))
    a_spec = pl.BlockSpec((tk, tm), lambda j, i, k: (k, i)) if mode == "tn" else pl.BlockSpec((tm, tk), lambda j, i, k: (i, k))
    b_spec = pl.BlockSpec((None, n, tk), lambda j, i, k: (j, 0, k)) if mode == "nt" else pl.BlockSpec((None, tk, n), lambda j, i, k: (j, k, 0))
    o_spec = pl.BlockSpec((None, tm, n), lambda j, i, k: (j, i, 0))
    return _mm_call(name, mode, [(a, bs)], [(a_spec, b_spec)], o_spec, jax.ShapeDtypeStruct((J, M, n), out_dtype),
                    (J, M // tm, K // tk), (tm, n))


def _mm_shard_k(pairs, mode, name, res=None, out_dtype=F32, tm=512, tn=None, fold=1):
    J, M, n = pairs[0][0].shape
    N = pairs[0][1].shape[2] if mode == "nn" else pairs[0][1].shape[1]
    tm = min(tm, M)
    tn = tn or _pick(N, (2048, 1024, 512))
    a_spec = pl.BlockSpec((fold, tm, n), lambda i, j, k: (k, i, 0))
    b_spec = pl.BlockSpec((fold, n, tn), lambda i, j, k: (k, 0, j)) if mode == "nn" else pl.BlockSpec((fold, tn, n), lambda i, j, k: (k, j, 0))
    o_spec = pl.BlockSpec((tm, tn), lambda i, j, k: (i, j))
    return _mm_call(name, mode, pairs, [(a_spec, b_spec)] * len(pairs), o_spec, jax.ShapeDtypeStruct((M, N), out_dtype),
                    (M // tm, N // tn, J // fold), (tm, tn), res, fold)


def _mm_shard_m(as_, b, name, out_dtype=F32, tm=None, tn=None, tk=512):
    J, K, n = as_.shape
    N = b.shape[1]
    tm = tm or n
    tn = tn or _pick(N, (2048, 1024, 512))
    tk = min(tk, K)
    nn = N // tn
    a_spec = pl.BlockSpec((None, tk, tm), lambda j, i, k: (j, k, i // nn))
    b_spec = pl.BlockSpec((tk, tn), lambda j, i, k: (k, i % nn))
    o_spec = pl.BlockSpec((None, tm, tn), lambda j, i, k: (j, i // nn, i % nn))
    return _mm_call(name, "tn", [(as_, b)], [(a_spec, b_spec)], o_spec, jax.ShapeDtypeStruct((J, n, N), out_dtype),
                    (J, (n // tm) * nn, K // tk), (tm, tn))


def _out_proj_dx(dh, x, g, dres, w, name):
    T, D = x.shape
    N = w.shape[0]
    tm = min(256, T)

    def body(dh_ref, x_ref, g_ref, dres_ref, w_ref, o_ref, dx_ref, dg_ref):
        @pl.when(pl.program_id(0) == 0)
        def _():
            dg_ref[...] = jnp.zeros_like(dg_ref)

        xv, dhv = x_ref[...], dh_ref[...]
        r = lax.rsqrt(jnp.mean(xv * xv, axis=-1, keepdims=True) + RMS_EPS)
        y = xv * r
        dy = dhv * g_ref[...]
        dx = dres_ref[...] + r * (dy - y * jnp.mean(dy * y, axis=-1, keepdims=True))
        dx_ref[...] = dx
        dg_ref[...] += jnp.sum(dhv * y, axis=0, keepdims=True)
        o_ref[...] = _nt(dx.astype(BF16), w_ref[...])

    row = pl.BlockSpec((tm, D), lambda i: (i, 0))
    vec = pl.BlockSpec((1, D), lambda i: (0, 0))
    return pl.pallas_call(
        body, name=name, grid=(T // tm,), in_specs=[row, row, vec, row, pl.BlockSpec((N, D), lambda i: (0, 0))],
        out_specs=[pl.BlockSpec((tm, N), lambda i: (i, 0)), row, vec],
        out_shape=[jax.ShapeDtypeStruct((T, N), F32), jax.ShapeDtypeStruct((T, D), F32), jax.ShapeDtypeStruct((1, D), F32)],
        compiler_params=_params(("arbitrary",)),
    )(dh, x, g.reshape(1, D), dres, w)


def _in_proj_dw(pieces, h, name):
    T, D = h.shape
    tm = 256
    nbs = [p.shape[1] // tm for p in pieces]
    los = [sum(nbs[:t]) for t in range(len(pieces))]

    def body(*refs):
        h_ref, o_ref = refs[-2:]
        i = pl.program_id(0)
        for p_ref, lo, nb in zip(refs[:-2], los, nbs):
            @pl.when((i >= lo) & (i < lo + nb))
            def _():
                o_ref[...] = _tn(p_ref[...], h_ref[...]).astype(BF16)

    specs = [pl.BlockSpec((T, tm), (lambda lo, nb: lambda i: (0, jnp.clip(i - lo, 0, nb - 1)))(lo, nb)) for lo, nb in zip(los, nbs)]
    return pl.pallas_call(body, name=name, grid=(sum(nbs),), in_specs=specs + [pl.BlockSpec((T, D), lambda i: (0, 0))],
                          out_specs=pl.BlockSpec((tm, D), lambda i: (i, 0)), out_shape=jax.ShapeDtypeStruct((sum(nbs) * tm, D), BF16),
                          compiler_params=_params(("parallel",)))(*pieces, h)


def _in_proj_dx(pieces, w, name, tm=512, tn=1024):
    T = pieces[0].shape[0]
    K, D = w.shape
    tm = min(tm, T)
    widths = [p.shape[1] for p in pieces]
    offs = [sum(widths[:t]) for t in range(len(pieces))]

    def body(*refs):
        w_ref, o_ref = refs[-2:]
        acc = None
        for p_ref, off, wd in zip(refs[:-2], offs, widths):
            d = _nn(p_ref[...], w_ref[off:off + wd, :])
            acc = d if acc is None else acc + d
        o_ref[...] = acc

    specs = [pl.BlockSpec((tm, wd), lambda i, j: (i, 0)) for wd in widths]
    return pl.pallas_call(body, name=name, grid=(T // tm, D // tn), in_specs=specs + [pl.BlockSpec((K, tn), lambda i, j: (0, j))],
                          out_specs=pl.BlockSpec((tm, tn), lambda i, j: (i, j)), out_shape=jax.ShapeDtypeStruct((T, D), F32),
                          compiler_params=_params(("parallel", "parallel")))(*pieces, w)


def _seg_matrix(width):
    return jnp.asarray(np.kron(np.eye(width // HEAD_DIM, dtype=np.float32), np.full((HEAD_DIM, HEAD_DIM), 1.0 / HEAD_DIM, np.float32)), BF16)


def _segmean(v, p):
    hi = v.astype(BF16)
    r = v - hi.astype(F32)
    mid = r.astype(BF16)
    lo = (r - mid.astype(F32)).astype(BF16)
    w = min(256, v.shape[1])
    pw = p[:w, :w]
    halves = []
    for c in range(v.shape[1] // w):
        cols = slice(c * w, (c + 1) * w)
        halves.append(_nn(hi[:, cols], pw) + _nn(mid[:, cols], pw) + _nn(lo[:, cols], pw))
    return halves[0] if len(halves) == 1 else jnp.concatenate(halves, axis=1)


def _gelu(x):
    c0 = math.sqrt(2.0 / math.pi)
    t = jnp.tanh(c0 * (x + 0.044715 * x * x * x))
    return 0.5 * x * (1.0 + t), t


def _gelu_grad(x, t):
    c0 = math.sqrt(2.0 / math.pi)
    return 0.5 * (1.0 + t) + 0.5 * x * (1.0 - t * t) * c0 * (1.0 + 3.0 * 0.044715 * x * x)


def _sigmoid(x):
    return 1.0 / (1.0 + jnp.exp(-x))


def _rms_fwd(x, g, name):
    T, D = x.shape
    tm = min(256, T)

    def body(x_ref, g_ref, o_ref):
        o_ref[...] = _rms_rows(x_ref[...], g_ref[...])

    return pl.pallas_call(
        body, name=name, grid=(T // tm,),
        in_specs=[pl.BlockSpec((tm, D), lambda i: (i, 0)), pl.BlockSpec((1, D), lambda i: (0, 0))],
        out_specs=pl.BlockSpec((tm, D), lambda i: (i, 0)), out_shape=jax.ShapeDtypeStruct((T, D), BF16),
        compiler_params=_params(("parallel",)),
    )(x, g.reshape(1, D))


def _rms_bwd(dh, x, g, dres, name):
    T, D = x.shape
    tm = min(256, T)

    def body(dh_ref, x_ref, g_ref, dres_ref, dx_ref, dg_ref):
        @pl.when(pl.program_id(0) == 0)
        def _():
            dg_ref[...] = jnp.zeros_like(dg_ref)

        xv, dhv = x_ref[...], dh_ref[...]
        r = lax.rsqrt(jnp.mean(xv * xv, axis=-1, keepdims=True) + RMS_EPS)
        y = xv * r
        dy = dhv * g_ref[...]
        dx_ref[...] = dres_ref[...] + r * (dy - y * jnp.mean(dy * y, axis=-1, keepdims=True))
        dg_ref[...] += jnp.sum(dhv * y, axis=0, keepdims=True)

    row = pl.BlockSpec((tm, D), lambda i: (i, 0))
    vec = pl.BlockSpec((1, D), lambda i: (0, 0))
    return pl.pallas_call(
        body, name=name, grid=(T // tm,), in_specs=[row, row, vec, row], out_specs=[row, vec],
        out_shape=[jax.ShapeDtypeStruct((T, D), F32), jax.ShapeDtypeStruct((1, D), F32)],
        compiler_params=_params(("arbitrary",)),
    )(dh, x, g.reshape(1, D), dres)


def _sgu_core(zu, zv, ws_ref, bias, p):
    ug, tu = _gelu(zu)
    vg, tv = _gelu(zv)
    xc = vg - _segmean(vg, p)
    rs = lax.rsqrt(_segmean(xc * xc, p) + LN_EPS)
    vn = xc * rs
    vnb = vn.astype(BF16)
    low = lax.broadcasted_iota(jnp.int32, (SGU_CHUNK, 128), 1) < HEAD_DIM
    parts = []
    for j in range(4):
        vp = vnb[:, 128 * j:128 * (j + 1)]
        parts.append(jnp.where(low, _nn(ws_ref[2 * j], vp), _nn(ws_ref[2 * j + 1], vp)))
    mixed = jnp.concatenate(parts, axis=1) + bias
    return ug, tu, tv, rs, vn, vnb, mixed, low


SGU_ROWS = 4 * SGU_CHUNK


def _sgu_fwd(z, ws, bias, name):
    T = z.shape[0]

    def body(zu_ref, zv_ref, ws_ref, b_ref, p_ref, y_ref):
        for r in range(0, SGU_ROWS, SGU_CHUNK):
            rows = slice(r, r + SGU_CHUNK)
            ug, _, _, _, _, _, mixed, _ = _sgu_core(zu_ref[rows, :].astype(F32), zv_ref[rows, :].astype(F32), ws_ref, b_ref[...], p_ref[...])
            y_ref[rows, :] = ug * mixed

    full = lambda shape: pl.BlockSpec(shape, lambda i: (0,) * len(shape))
    return pl.pallas_call(
        body, name=name, grid=(T // SGU_ROWS,),
        in_specs=[pl.BlockSpec((SGU_ROWS, 512), lambda i: (i, COL_AU)), pl.BlockSpec((SGU_ROWS, 512), lambda i: (i, COL_AV)),
                  full((8, 128, 128)), full((128, 512)), full((512, 512))],
        out_specs=pl.BlockSpec((SGU_ROWS, 512), lambda i: (i, 0)), out_shape=jax.ShapeDtypeStruct((T, 512), F32),
        compiler_params=_params(("parallel",)),
    )(z, z, ws, bias, _seg_matrix(512))


def _sgu_bwd(z, dy, ws, ws_t, bias, name):
    T = z.shape[0]

    def body(zu_ref, zv_ref, dy_ref, ws_ref, wst_ref, b_ref, p_ref, dz_ref, dws_ref, db_ref):
        @pl.when(pl.program_id(0) == 0)
        def _():
            dws_ref[...] = jnp.zeros_like(dws_ref)
            db_ref[...] = jnp.zeros_like(db_ref)

        p = p_ref[...]
        zero = jnp.zeros((SGU_CHUNK, 128), BF16)
        dws = [None] * 8
        db = None
        for r in range(0, SGU_ROWS, SGU_CHUNK):
            rows = slice(r, r + SGU_CHUNK)
            zu, zv = zu_ref[rows, :].astype(F32), zv_ref[rows, :].astype(F32)
            ug, tu, tv, rs, vn, vnb, mixed, low = _sgu_core(zu, zv, ws_ref, b_ref[...], p)
            dyv = dy_ref[rows, :]
            dmixed = dyv * ug
            db = dmixed if db is None else db + dmixed
            dmb = dmixed.astype(BF16)
            parts = []
            for j in range(4):
                dmp, vp = dmb[:, 128 * j:128 * (j + 1)], vnb[:, 128 * j:128 * (j + 1)]
                for g, d in ((2 * j, _nt(jnp.where(low, dmp, zero), vp)), (2 * j + 1, _nt(jnp.where(low, zero, dmp), vp))):
                    dws[g] = d if dws[g] is None else dws[g] + d
                parts.append(jnp.where(low, _nn(wst_ref[2 * j], dmp), _nn(wst_ref[2 * j + 1], dmp)))
            dvn = jnp.concatenate(parts, axis=1)
            dvg = rs * (dvn - _segmean(dvn, p) - vn * _segmean(dvn * vn, p))
            dz_ref[rows, 0:512] = (dyv * mixed * _gelu_grad(zu, tu)).astype(BF16)
            dz_ref[rows, 512:1024] = (dvg * _gelu_grad(zv, tv)).astype(BF16)
        db_ref[...] += db
        for g in range(8):
            dws_ref[g] += dws[g]

    full = lambda shape: pl.BlockSpec(shape, lambda i: (0,) * len(shape))
    return pl.pallas_call(
        body, name=name, grid=(T // SGU_ROWS,),
        in_specs=[pl.BlockSpec((SGU_ROWS, 512), lambda i: (i, COL_AU)), pl.BlockSpec((SGU_ROWS, 512), lambda i: (i, COL_AV)),
                  pl.BlockSpec((SGU_ROWS, 512), lambda i: (i, 0)), full((8, 128, 128)), full((8, 128, 128)), full((128, 512)), full((512, 512))],
        out_specs=[pl.BlockSpec((SGU_ROWS, 1024), lambda i: (i, 0)), full((8, 128, 128)), full((128, 512))],
        out_shape=[jax.ShapeDtypeStruct((T, 1024), BF16), jax.ShapeDtypeStruct((8, 128, 128), F32), jax.ShapeDtypeStruct((128, 512), F32)],
        compiler_params=_params(("arbitrary",)),
    )(z, z, dy, ws, ws_t, bias, _seg_matrix(512))


CONV_ROWS = 256


def _conv_taps(pad_ref, w_ref, base, flip):
    acc = None
    for k in range(CONV_WIDTH):
        wk = w_ref[CONV_WIDTH - 1 - k if flip else k]
        t = wk * pad_ref[base + k + 1:base + k + 1 + CONV_ROWS, :]
        acc = t if acc is None else acc + t
    return acc


def _conv_fwd1(z, w, cb, B, S, name):
    T = B * S
    rows = min(CONV_ROWS, S)
    assert rows == CONV_ROWS

    def body(a_ref, g_ref, w_ref, cb_ref, c_ref, pad):
        pad[0:CONV_PAD, :] = jnp.zeros((CONV_PAD, 128), F32)
        pad[CONV_PAD + S:2 * CONV_PAD + S, :] = jnp.zeros((CONV_PAD, 128), F32)
        pad[CONV_PAD:CONV_PAD + S, :] = a_ref[...].astype(F32) * _sigmoid(g_ref[...].astype(F32))

        for base in range(0, S, CONV_ROWS):
            c_ref[base:base + CONV_ROWS, :] = _conv_taps(pad, w_ref, base, False) + cb_ref[...]

    return pl.pallas_call(
        body, name=name, grid=(4, B),
        in_specs=[pl.BlockSpec((S, 128), lambda j, b: (b, 4 * COL_CA + j)), pl.BlockSpec((S, 128), lambda j, b: (b, 4 * COL_CG + j)),
                  pl.BlockSpec((32, 1, 128), lambda j, b: (0, 0, j)), pl.BlockSpec((1, 128), lambda j, b: (0, j))],
        out_specs=pl.BlockSpec((S, 128), lambda j, b: (b, j)), out_shape=jax.ShapeDtypeStruct((T, 512), F32),
        scratch_shapes=[pltpu.VMEM((S + 2 * CONV_PAD, 128), F32)], compiler_params=_params(("parallel", "parallel")),
    )(z, z, w, cb)


def _ln_rows(c):
    mu = jnp.mean(c, axis=-1, keepdims=True)
    xc = c - mu
    rs = lax.rsqrt(jnp.mean(xc * xc, axis=-1, keepdims=True) + LN_EPS)
    return xc * rs, rs


def _conv_bwd2(z, dc, w, B, S, name):
    T = B * S

    def body(a_ref, g_ref, dc_ref, w_ref, da_ref, dg_ref, dw_ref, hpad, dpad):
        @pl.when(pl.program_id(1) == 0)
        def _():
            dw_ref[...] = jnp.zeros_like(dw_ref)

        zeros = jnp.zeros((CONV_PAD, 128), F32)
        for ref in (hpad, dpad):
            ref[0:CONV_PAD, :] = zeros
            ref[CONV_PAD + S:2 * CONV_PAD + S, :] = zeros
        hpad[CONV_PAD:CONV_PAD + S, :] = a_ref[...].astype(F32) * _sigmoid(g_ref[...].astype(F32))
        dpad[CONV_PAD:CONV_PAD + S, :] = dc_ref[...]
        dws = [None] * CONV_WIDTH
        for base in range(0, S, CONV_ROWS):
            rows = slice(base, base + CONV_ROWS)
            dh = _conv_taps(dpad, w_ref, base, True)
            sg = _sigmoid(g_ref[rows, :].astype(F32))
            da_ref[rows, :] = (dh * sg).astype(BF16)
            dg_ref[rows, :] = (dh * a_ref[rows, :].astype(F32) * sg * (1.0 - sg)).astype(BF16)
            dcv = dc_ref[rows, :]
            for k in range(CONV_WIDTH):
                prod = dcv * hpad[base + k + 1:base + k + 1 + CONV_ROWS, :]
                part = jnp.sum(prod.reshape(CONV_ROWS // 8, 8, 128), axis=0)
                dws[k] = part if dws[k] is None else dws[k] + part
        for k in range(CONV_WIDTH):
            dw_ref[k] += jnp.sum(dws[k], axis=0, keepdims=True)

    return pl.pallas_call(
        body, name=name, grid=(4, B),
        in_specs=[pl.BlockSpec((S, 128), lambda j, b: (b, 4 * COL_CA + j)), pl.BlockSpec((S, 128), lambda j, b: (b, 4 * COL_CG + j)),
                  pl.BlockSpec((S, 128), lambda j, b: (b, j)), pl.BlockSpec((32, 1, 128), lambda j, b: (0, 0, j))],
        out_specs=[pl.BlockSpec((S, 128), lambda j, b: (b, j)), pl.BlockSpec((S, 128), lambda j, b: (b, j)),
                   pl.BlockSpec((32, 1, 128), lambda j, b: (0, 0, j))],
        out_shape=[jax.ShapeDtypeStruct((T, 512), BF16), jax.ShapeDtypeStruct((T, 512), BF16), jax.ShapeDtypeStruct((32, 1, 512), F32)],
        scratch_shapes=[pltpu.VMEM((S + 2 * CONV_PAD, 128), F32), pltpu.VMEM((S + 2 * CONV_PAD, 128), F32)],
        compiler_params=_params(("parallel", "arbitrary")),
    )(z, z, dc, w)


def _swap16(x):
    n = x.shape[1]
    first = (lax.broadcasted_iota(jnp.int32, x.shape, 1) % 32) < 16
    return jnp.where(first, pltpu.roll(x, n - 16, 1), pltpu.roll(x, 16, 1))


def _rope(x, cos, sin):
    return x * cos + _swap16(x) * sin


def _rope_t(dy, cos, sin):
    return dy * cos + _swap16(dy * sin)


def _qk_norm(x, p):
    r = lax.rsqrt(_segmean(x * x, p) + RMS_EPS)
    return x * r, r


def _store_heads(ref, val, n):
    for h in range(n):
        ref[h] = val[:, HEAD_DIM * h:HEAD_DIM * (h + 1)].astype(ref.dtype)


def _load_heads(ref, n):
    return jnp.concatenate([ref[h] for h in range(n)], axis=1)


def _prep_fwd(z, gq, gk, rope, B, S, kv_heads, cols, name):
    tm = min(ROW_TILE, S)
    ns = S // tm
    kw = kv_heads * HEAD_DIM
    scale = HEAD_DIM ** -0.5
    qc, kc, vc = cols

    def body(*refs):
        if rope is None:
            q_ref, k_ref, v_ref, gq_ref, gk_ref, p_ref, qo, ko, vo = refs
        else:
            q_ref, k_ref, v_ref, gq_ref, gk_ref, p_ref, cos_ref, sin_ref, qo, ko, vo = refs
        p = p_ref[...]
        qn, _ = _qk_norm(q_ref[...].astype(F32), p)
        kn, _ = _qk_norm(k_ref[...].astype(F32), p[:kw, :kw])
        qn, kn = qn * gq_ref[...], kn * gk_ref[...]
        if rope is not None:
            cos, sin = cos_ref[...], sin_ref[...]
            qn, kn = _rope(qn, cos, sin), _rope(kn, cos[:, :kw], sin[:, :kw])
        _store_heads(qo, qn * scale, N_HEADS)
        _store_heads(ko, kn, kv_heads)
        _store_heads(vo, v_ref[...], kv_heads)

    row = lambda w, c: pl.BlockSpec((tm, w), lambda b, i: (b * ns + i, c))
    const = lambda shape: pl.BlockSpec(shape, lambda b, i: (0,) * len(shape))
    heads = lambda n: pl.BlockSpec((None, n, tm, HEAD_DIM), lambda b, i: (b, 0, i, 0))
    ins = [z, z, z, gq, gk, _seg_matrix(512)]
    specs = [row(512, qc), row(kw, kc), row(kw, vc), const((1, 512)), const((1, kw)), const((512, 512))]
    if rope is not None:
        ins += list(rope)
        specs += [pl.BlockSpec((tm, 512), lambda b, i: (i, 0))] * 2
    return pl.pallas_call(
        body, name=name, grid=(B, ns), in_specs=specs, out_specs=[heads(N_HEADS), heads(kv_heads), heads(kv_heads)],
        out_shape=[jax.ShapeDtypeStruct((B, N_HEADS, S, HEAD_DIM), BF16), jax.ShapeDtypeStruct((B, kv_heads, S, HEAD_DIM), BF16),
                   jax.ShapeDtypeStruct((B, kv_heads, S, HEAD_DIM), BF16)],
        compiler_params=_params(("parallel", "parallel")),
    )(*ins)


def _prep_bwd(z, dq, dk, dv, gq, gk, rope, B, S, kv_heads, cols, name):
    T = B * S
    tm = min(ROW_TILE, S)
    ns = S // tm
    kw = kv_heads * HEAD_DIM
    scale = HEAD_DIM ** -0.5
    qc, kc, _ = cols

    def body(*refs):
        if rope is None:
            q_ref, k_ref, dq_ref, dk_ref, dv_ref, gq_ref, gk_ref, p_ref, dz_ref, dgq_ref, dgk_ref = refs
        else:
            q_ref, k_ref, dq_ref, dk_ref, dv_ref, gq_ref, gk_ref, p_ref, cos_ref, sin_ref, dz_ref, dgq_ref, dgk_ref = refs

        @pl.when((pl.program_id(0) == 0) & (pl.program_id(1) == 0))
        def _():
            dgq_ref[...] = jnp.zeros_like(dgq_ref)
            dgk_ref[...] = jnp.zeros_like(dgk_ref)

        p = p_ref[...]
        dqv = _load_heads(dq_ref, N_HEADS) * scale
        dkv = _load_heads(dk_ref, kv_heads)
        if rope is not None:
            cos, sin = cos_ref[...], sin_ref[...]
            dqv, dkv = _rope_t(dqv, cos, sin), _rope_t(dkv, cos[:, :kw], sin[:, :kw])

        def through_norm(xv, dy, g, pm, dg_ref):
            xh, r = _qk_norm(xv, pm)
            dg_ref[...] += jnp.sum(dy * xh, axis=0, keepdims=True)
            dxh = dy * g
            return r * (dxh - xh * _segmean(dxh * xh, pm))

        dz_ref[:, 0:512] = through_norm(q_ref[...].astype(F32), dqv, gq_ref[...], p, dgq_ref).astype(BF16)
        dz_ref[:, 512:512 + kw] = through_norm(k_ref[...].astype(F32), dkv, gk_ref[...], p[:kw, :kw], dgk_ref).astype(BF16)
        dz_ref[:, 512 + kw:512 + 2 * kw] = _load_heads(dv_ref, kv_heads).astype(BF16)

    row = lambda w, c: pl.BlockSpec((tm, w), lambda b, i: (b * ns + i, c))
    const = lambda shape: pl.BlockSpec(shape, lambda b, i: (0,) * len(shape))
    heads = lambda n: pl.BlockSpec((None, n, tm, HEAD_DIM), lambda b, i: (b, 0, i, 0))
    ins = [z, z, dq, dk, dv, gq, gk, _seg_matrix(512)]
    specs = [row(512, qc), row(kw, kc), heads(N_HEADS), heads(kv_heads), heads(kv_heads), const((1, 512)), const((1, kw)), const((512, 512))]
    if rope is not None:
        ins += list(rope)
        specs += [pl.BlockSpec((tm, 512), lambda b, i: (i, 0))] * 2
    return pl.pallas_call(
        body, name=name, grid=(B, ns), in_specs=specs, out_specs=[row(512 + 2 * kw, 0), const((1, 512)), const((1, kw))],
        out_shape=[jax.ShapeDtypeStruct((T, 512 + 2 * kw), BF16), jax.ShapeDtypeStruct((1, 512), F32), jax.ShapeDtypeStruct((1, kw), F32)],
        compiler_params=_params(("arbitrary", "arbitrary")),
    )(*ins)


def _toeplitz(win, tq, S):
    r = pltpu.roll(jnp.broadcast_to(win, (tq, S + tq)), 0, 1, stride=1, stride_axis=0)
    return r[:, tq:tq + S]


ATTN_HEADS = 4


def _attn_fwd(q, k, v, win, name, nh=ATTN_HEADS, tq=ATTN_TQ):
    B, H, S, _ = q.shape
    shared = k.shape[1] != H
    assert not shared or H // k.shape[1] == nh
    tq = min(tq, S)

    def body(*refs):
        if win is None:
            q_ref, k_ref, v_ref, o_ref = refs
        else:
            q_ref, k_ref, v_ref, w_ref, o_ref = refs
        kvs = [(k_ref[...], v_ref[...])] * nh if shared else [(k_ref[h], v_ref[h]) for h in range(nh)]
        scores = []
        for h in range(nh):
            s = _nt(q_ref[h], kvs[h][0])
            if win is not None:
                s = s + _toeplitz(w_ref[h], tq, S)
            scores.append(s)
        probs = []
        for s in scores:
            p = jnp.exp(s - jnp.max(s, axis=-1, keepdims=True))
            probs.append((p.astype(BF16), jnp.sum(p, axis=-1, keepdims=True)))
        for h, (p, l) in enumerate(probs):
            o_ref[h] = _nn(p, kvs[h][1]) / l

    qs = pl.BlockSpec((None, nh, tq, HEAD_DIM), lambda b, h, i: (b, h, i, 0))
    ks = (pl.BlockSpec((None, None, S, HEAD_DIM), lambda b, h, i: (b, h, 0, 0)) if shared
          else pl.BlockSpec((None, nh, S, HEAD_DIM), lambda b, h, i: (b, h, 0, 0)))
    ins, specs = [q, k, v], [qs, ks, ks]
    if win is not None:
        ins.append(win)
        specs.append(pl.BlockSpec((nh, None, 1, S + tq), lambda b, h, i: (h, i, 0, 0)))
    return pl.pallas_call(body, name=name, grid=(B, H // nh, S // tq), in_specs=specs, out_specs=qs,
                          out_shape=jax.ShapeDtypeStruct((B, H, S, HEAD_DIM), F32),
                          compiler_params=_params(("parallel", "parallel", "parallel")))(*ins)


def _attn_bwd(q, k, v, o, do, win, name, nh=ATTN_HEADS, tq=ATTN_TQ):
    B, H, S, _ = q.shape
    hkv = k.shape[1]
    shared = hkv != H
    assert not shared or H // hkv == nh
    tq = min(tq, S)
    nq = S // tq

    def body(*refs):
        if win is None:
            q_ref, k_ref, v_ref, o_ref, do_ref, dq_ref, dk_ref, dv_ref = refs
        else:
            q_ref, k_ref, v_ref, o_ref, do_ref, w_ref, rev_ref, dq_ref, dk_ref, dv_ref, dw_ref = refs

        @pl.when(pl.program_id(2) == 0)
        def _():
            dk_ref[...] = jnp.zeros_like(dk_ref)
            dv_ref[...] = jnp.zeros_like(dv_ref)

        kvs = [(k_ref[...], v_ref[...])] * nh if shared else [(k_ref[h], v_ref[h]) for h in range(nh)]
        qvs, dobs, scores, dps = [], [], [], []
        for h in range(nh):
            qv, dov = q_ref[h], do_ref[h]
            dob = dov.astype(BF16)
            s = _nt(qv, kvs[h][0])
            if win is not None:
                s = s + _toeplitz(w_ref[h], tq, S)
            dp = _nt(dob, kvs[h][1]) - jnp.sum(dov * o_ref[h], axis=-1, keepdims=True)
            qvs.append(qv)
            dobs.append(dob)
            scores.append(s)
            dps.append(dp)
        pbs, dsbs = [], []
        for s, dp in zip(scores, dps):
            p = jnp.exp(s - jnp.max(s, axis=-1, keepdims=True))
            p = p * (1.0 / jnp.sum(p, axis=-1, keepdims=True))
            pbs.append(p.astype(BF16))
            dsbs.append((p * dp).astype(BF16))
        dk_acc = dv_acc = None
        for h in range(nh):
            dvh, dkh = _tn(pbs[h], dobs[h]), _tn(dsbs[h], qvs[h])
            dq_ref[h] = _nn(dsbs[h], kvs[h][0])
            if shared:
                dv_acc = dvh if dv_acc is None else dv_acc + dvh
                dk_acc = dkh if dk_acc is None else dk_acc + dkh
            else:
                dv_ref[h] += dvh
                dk_ref[h] += dkh
            if win is not None:
                rev = _nn(rev_ref[...], dsbs[h])
                wide = jnp.concatenate([rev, jnp.zeros((tq, tq), F32)], axis=1)
                dw_ref[h] = jnp.sum(pltpu.roll(wide, 0, 1, stride=1, stride_axis=0), axis=0, keepdims=True)
        if shared:
            dv_ref[...] += dv_acc
            dk_ref[...] += dk_acc

    qs = pl.BlockSpec((None, nh, tq, HEAD_DIM), lambda b, h, i: (b, h, i, 0))
    ks = (pl.BlockSpec((None, None, S, HEAD_DIM), lambda b, h, i: (b, h, 0, 0)) if shared
          else pl.BlockSpec((None, nh, S, HEAD_DIM), lambda b, h, i: (b, h, 0, 0)))
    ins, specs = [q, k, v, o, do], [qs, ks, ks, qs, qs]
    outs = [jax.ShapeDtypeStruct((B, H, S, HEAD_DIM), F32), jax.ShapeDtypeStruct((B, hkv, S, HEAD_DIM), F32), jax.ShapeDtypeStruct((B, hkv, S, HEAD_DIM), F32)]
    ospecs = [qs, ks, ks]
    if win is not None:
        ins += [win, jnp.asarray(np.eye(tq, dtype=np.float32)[::-1].copy(), BF16)]
        specs += [pl.BlockSpec((nh, None, 1, S + tq), lambda b, h, i: (h, i, 0, 0)), pl.BlockSpec((tq, tq), lambda b, h, i: (0, 0))]
        outs.append(jax.ShapeDtypeStruct((B, H, nq, 1, S + tq), F32))
        ospecs.append(pl.BlockSpec((None, nh, None, 1, S + tq), lambda b, h, i: (b, h, i, 0, 0)))
    return pl.pallas_call(body, name=name, grid=(B, H // nh, nq), in_specs=specs, out_specs=ospecs, out_shape=outs,
                          compiler_params=_params(("parallel", "parallel", "arbitrary")))(*ins)


def _pattern_count(delta):
    n = jnp.zeros(delta.shape, jnp.int32)
    for window, dil in DIL_PATTERNS:
        n = n + ((delta % dil == 0) & (jnp.abs(delta) <= window // 2)).astype(jnp.int32)
    return n


def _t5_bucket(rel):
    nb = REL_BUCKETS // 2
    max_exact = nb // 2
    ret = jnp.where(rel > 0, nb, 0)
    n = jnp.abs(rel)
    nf = jnp.maximum(n, 1).astype(F32)
    large = max_exact + (jnp.log(nf / max_exact) / math.log(REL_MAX_DIST / max_exact) * (nb - max_exact)).astype(jnp.int32)
    large = jnp.minimum(large, nb - 1)
    return ret + jnp.where(n < max_exact, n, large)


def _bias_windows(rel_bias, S):
    tq = min(ATTN_TQ, S)
    nq = S // tq
    n = nq * (S + tq)
    delta = (jnp.arange(S + tq)[None, :] - (jnp.arange(nq)[:, None] + 1) * tq).reshape(n)
    count = _pattern_count(delta)
    onehot = (_t5_bucket(delta)[None, :] == jnp.arange(REL_BUCKETS)[:, None]).astype(F32)
    extra = jnp.where(count > 0, jnp.log(jnp.maximum(count, 1).astype(F32)), MASKED).reshape(1, n)
    live = (count > 0).astype(F32).reshape(1, n)

    def body(t_ref, oh_ref, live_ref, extra_ref, o_ref):
        o_ref[...] = _nn(t_ref[...], oh_ref[...], HIGHEST) * live_ref[...] + extra_ref[...]

    val = pl.pallas_call(body, name="bias_windows", out_shape=jax.ShapeDtypeStruct((N_HEADS, n), F32),
                         compiler_params=_params())(rel_bias.T, onehot, live, extra)
    return val.reshape(N_HEADS, nq, 1, S + tq)


def _bias_fold(dwin, S, name):
    B, H, nq = dwin.shape[:3]
    tq = min(ATTN_TQ, S)
    n = nq * (S + tq)
    delta = (jnp.arange(S + tq)[None, :] - (tq - 1) - jnp.arange(nq)[:, None] * tq).reshape(n)
    onehot = (_t5_bucket(delta)[:, None] == jnp.arange(128)[None, :]).astype(F32)

    def body(d_ref, oh_ref, o_ref):
        tot = d_ref[0]
        for b in range(1, B):
            tot = tot + d_ref[b]
        o_ref[...] = _nn(tot, oh_ref[...], HIGHEST)

    out = pl.pallas_call(body, name=name, out_shape=jax.ShapeDtypeStruct((H, 128), F32), compiler_params=_params())(dwin.reshape(B, H, n), onehot)
    return out[:, :REL_BUCKETS].T


def _rope_tables(S):
    half = 16
    freqs = ROPE_THETA ** (-jnp.arange(half, dtype=F32) / half)
    t = jnp.arange(S)
    ang_r = (t // GRID_W).astype(F32)[:, None] * freqs[None, :]
    ang_c = (t % GRID_W).astype(F32)[:, None] * freqs[None, :]
    cos = jnp.concatenate([jnp.cos(ang_r)] * 2 + [jnp.cos(ang_c)] * 2, axis=1)
    sin = jnp.concatenate([-jnp.sin(ang_r), jnp.sin(ang_r), -jnp.sin(ang_c), jnp.sin(ang_c)], axis=1)
    return jnp.tile(cos, (1, N_HEADS)), jnp.tile(sin, (1, N_HEADS))


def _conv_act(c, g, b):
    n, rs = _ln_rows(c)
    t = n * g + b
    return t * _sigmoid(t), n, rs, t


def _mix_fwd(ya, ob, c, od, gain, lng, lnb, B, S, name):
    T = B * S
    tm = min(ROW_TILE, S)
    ns = S // tm

    def body(ya_ref, ob_ref, c_ref, od_ref, g_ref, lg_ref, lb_ref, o_ref):
        yc = _conv_act(c_ref[...], lg_ref[...], lb_ref[...])[0]
        ys = [ya_ref[...], _load_heads(ob_ref, N_HEADS), yc, _load_heads(od_ref, N_HEADS)]
        for m, y in enumerate(ys):
            r = lax.rsqrt(jnp.mean(y * y, axis=-1, keepdims=True) + RMS_EPS)
            o_ref[:, 512 * m:512 * (m + 1)] = (y * r * g_ref[:, 512 * m:512 * (m + 1)]).astype(BF16)

    row = pl.BlockSpec((tm, 512), lambda b, i: (b * ns + i, 0))
    heads = pl.BlockSpec((None, N_HEADS, tm, HEAD_DIM), lambda b, i: (b, 0, i, 0))
    vec = pl.BlockSpec((1, 512), lambda b, i: (0, 0))
    return pl.pallas_call(
        body, name=name, grid=(B, ns), in_specs=[row, heads, row, heads, pl.BlockSpec((1, 2048), lambda b, i: (0, 0)), vec, vec],
        out_specs=pl.BlockSpec((tm, 2048), lambda b, i: (b * ns + i, 0)), out_shape=jax.ShapeDtypeStruct((T, 2048), BF16),
        compiler_params=_params(("parallel", "parallel")),
    )(ya, ob, c, od, gain, lng, lnb)


def _mix_bwd(ya, ob, c, od, dycat, gain, lng, lnb, B, S, name):
    T = B * S
    tm = min(ROW_TILE, S)
    ns = S // tm

    def body(ya_ref, ob_ref, c_ref, od_ref, dy_ref, g_ref, lg_ref, lb_ref, dya_ref, dob_ref, dc_ref, dod_ref, dg_ref, dlg_ref, dlb_ref, dcb_ref):
        @pl.when((pl.program_id(0) == 0) & (pl.program_id(1) == 0))
        def _():
            for ref in (dg_ref, dlg_ref, dlb_ref, dcb_ref):
                ref[...] = jnp.zeros_like(ref)

        yc, n, rs, t = _conv_act(c_ref[...], lg_ref[...], lb_ref[...])
        ys = [ya_ref[...], _load_heads(ob_ref, N_HEADS), yc, _load_heads(od_ref, N_HEADS)]
        outs = [dya_ref, dob_ref, None, dod_ref]
        for m, y in enumerate(ys):
            cols = slice(512 * m, 512 * (m + 1))
            r = lax.rsqrt(jnp.mean(y * y, axis=-1, keepdims=True) + RMS_EPS)
            yh = y * r
            dh = dy_ref[:, cols]
            dg_ref[:, cols] += jnp.sum(dh * yh, axis=0, keepdims=True)
            dyh = dh * g_ref[:, cols]
            dyv = r * (dyh - yh * jnp.mean(dyh * yh, axis=-1, keepdims=True))
            if m == 0:
                outs[m][...] = dyv
            elif m == 2:
                sg = _sigmoid(t)
                dt = dyv * sg * (1.0 + t * (1.0 - sg))
                dlg_ref[...] += jnp.sum(dt * n, axis=0, keepdims=True)
                dlb_ref[...] += jnp.sum(dt, axis=0, keepdims=True)
                dn = dt * lg_ref[...]
                dc = rs * (dn - jnp.mean(dn, axis=-1, keepdims=True) - n * jnp.mean(dn * n, axis=-1, keepdims=True))
                dc_ref[...] = dc
                dcb_ref[...] += jnp.sum(dc, axis=0, keepdims=True)
            else:
                _store_heads(outs[m], dyv, N_HEADS)

    row = pl.BlockSpec((tm, 512), lambda b, i: (b * ns + i, 0))
    heads = pl.BlockSpec((None, N_HEADS, tm, HEAD_DIM), lambda b, i: (b, 0, i, 0))
    vec = pl.BlockSpec((1, 2048), lambda b, i: (0, 0))
    flat = jax.ShapeDtypeStruct((T, 512), F32)
    hm = jax.ShapeDtypeStruct((B, N_HEADS, S, HEAD_DIM), F32)
    v512 = pl.BlockSpec((1, 512), lambda b, i: (0, 0))
    s512 = jax.ShapeDtypeStruct((1, 512), F32)
    return pl.pallas_call(
        body, name=name, grid=(B, ns), in_specs=[row, heads, row, heads, pl.BlockSpec((tm, 2048), lambda b, i: (b * ns + i, 0)), vec, v512, v512],
        out_specs=[row, heads, row, heads, vec, v512, v512, v512],
        out_shape=[flat, hm, flat, hm, jax.ShapeDtypeStruct((1, 2048), F32), s512, s512, s512],
        compiler_params=_params(("arbitrary", "arbitrary")),
    )(ya, ob, c, od, dycat, gain, lng, lnb)


def _ffn_down(gate, up, w_down, res, name, norm=None, target=None):
    J, T, n = gate.shape
    N = w_down.shape[2]
    tm = min(256, T)
    steps = T // tm

    def body(g_ref, u_ref, w_ref, r_ref, *rest):
        x_ref = rest[0] if (norm is not None or target is not None) else None
        o_ref, act_ref = rest[-3:-1] if x_ref is not None else rest[-2:]
        acc = None
        for j in range(J):
            g = g_ref[j].astype(F32)
            a = (g * _sigmoid(g) * u_ref[j].astype(F32)).astype(BF16)
            act_ref[j] = a
            d = _nn(a, w_ref[j])
            acc = d if acc is None else acc + d
        y = acc + r_ref[...]
        if target is None:
            o_ref[...] = y
            if norm is not None:
                rest[-1][...] = _rms_rows(y, x_ref[...])
            return
        err = y - x_ref[...]
        o_ref[...] = err * (1.0 / N)
        loss_ref, i = rest[-1], pl.program_id(0)

        @pl.when(i == 0)
        def _():
            loss_ref[...] = jnp.zeros_like(loss_ref)

        loss_ref[...] += jnp.sum(err * err)

        @pl.when(i == steps - 1)
        def _():
            loss_ref[...] = loss_ref[...] * (0.5 / N)

    gu = pl.BlockSpec((J, tm, n), lambda i: (0, i, 0))
    row = pl.BlockSpec((tm, N), lambda i: (i, 0))
    ins, specs = [gate, up, w_down, res], [gu, gu, pl.BlockSpec((J, n, N), lambda i: (0, 0, 0)), row]
    outs, ospecs = [jax.ShapeDtypeStruct((T, N), F32), jax.ShapeDtypeStruct((J, T, n), BF16)], [row, gu]
    if target is not None:
        ins, specs = ins + [target], specs + [row]
        outs, ospecs = outs + [jax.ShapeDtypeStruct((8, 128), F32)], ospecs + [pl.BlockSpec((8, 128), lambda i: (0, 0))]
    elif norm is not None:
        ins, specs = ins + [norm.reshape(1, N)], specs + [pl.BlockSpec((1, N), lambda i: (0, 0))]
        outs, ospecs = outs + [jax.ShapeDtypeStruct((T, N), BF16)], ospecs + [row]
    return pl.pallas_call(body, name=name, grid=(steps,), in_specs=specs, out_specs=ospecs, out_shape=outs,
                          compiler_params=_params(("arbitrary" if target is not None else "parallel",)))(*ins)


def _ffn_down_dx(dx, w_down, gate, up, name):
    J, n, D = w_down.shape
    T = dx.shape[0]
    tm = min(512, T)

    def body(dx_ref, w_ref, g_ref, u_ref, dg_ref, du_ref):
        d = _nt(dx_ref[...].astype(BF16), w_ref[...])
        g = g_ref[...].astype(F32)
        s = _sigmoid(g)
        dg_ref[...] = (d * u_ref[...].astype(F32) * s * (1.0 + g * (1.0 - s))).astype(BF16)
        du_ref[...] = (d * g * s).astype(BF16)

    blk = pl.BlockSpec((None, tm, n), lambda j, i: (j, i, 0))
    shape = jax.ShapeDtypeStruct((J, T, n), BF16)
    return pl.pallas_call(body, name=name, grid=(J, T // tm),
                          in_specs=[pl.BlockSpec((tm, D), lambda j, i: (i, 0)), pl.BlockSpec((None, n, D), lambda j, i: (j, 0, 0)), blk, blk],
                          out_specs=[blk, blk], out_shape=[shape, shape], compiler_params=_params(("parallel", "parallel")))(dx, w_down, gate, up)


def _row_tile(R):
    best = R
    for cand in range(16, min(R, 272) + 1, 16):
        if R % cand == 0:
            best = cand
    return best


def _adamw(w, m, v, stack, name, layer=None, prev=None):
    n, R, C = stack.shape
    tm = _row_tile(R)
    nb = R // tm
    off = 0 if layer is None else layer * nb
    c1 = 1.0 - ADAM_B1 ** ADAM_STEP
    c2 = 1.0 - ADAM_B2 ** ADAM_STEP

    def body(w_ref, m_ref, v_ref, s_ref, *rest):
        g_ref, d_ref, mo_ref, vo_ref = rest[-4:]
        g = s_ref[0].astype(F32)
        for k in range(1, n):
            g = g + s_ref[k].astype(F32)
        mn = ADAM_B1 * m_ref[...] + (1.0 - ADAM_B1) * g
        vn = ADAM_B2 * v_ref[...] + (1.0 - ADAM_B2) * (g * g)
        g_ref[...] = g
        mo_ref[...] = mn
        vo_ref[...] = vn
        d_ref[...] = -ADAM_LR * ((mn / c1) / (jnp.sqrt(vn / c2) + ADAM_EPS) + ADAM_WD * w_ref[...])

    blk = pl.BlockSpec((tm, C), lambda i: (i + off, 0))
    ins = [w, m, v, stack]
    specs = [blk, blk, blk, pl.BlockSpec((n, tm, C), lambda i: (0, i, 0))]
    aliases = {}
    if prev is not None:
        ins += list(prev)
        specs += [pl.BlockSpec(memory_space=pl.ANY)] * 4
        aliases = {4 + t: t for t in range(4)}
    shape = jax.ShapeDtypeStruct(w.shape, F32)
    return pl.pallas_call(body, name=name, grid=(nb,), in_specs=specs, out_specs=[blk] * 4, out_shape=[shape] * 4,
                          input_output_aliases=aliases, compiler_params=_params(("parallel",)))(*ins)


HBM = pl.BlockSpec(memory_space=pltpu.HBM)
SEM = pl.BlockSpec(memory_space=pltpu.SEMAPHORE)
EFFECT = pltpu.SideEffectType.DATAFLOW_SIDE_EFFECTING


PEERS = {"scatter": (1, 2, 3, 4, 5, 6, 7), "gather": (1, 2, 3, 4, 5, 6, 7), "chips": (1, 2, 4, 6), "forward": (2, 4, 6)}


def _spread_copies(srcs, lands, send_sems, recv_sems, local_sems, kind, waiting):
    x, y, c = lax.axis_index("x"), lax.axis_index("y"), lax.axis_index("c")
    me = 4 * x + 2 * y + c

    def peer(bits):
        dev = (1 - x if bits & 4 else x, 1 - y if bits & 2 else y, 1 - c if bits & 1 else c)
        return dev, 4 * dev[0] + 2 * dev[1] + dev[2]

    plan = PEERS[kind]
    remote, local = [], []
    for a, l in enumerate(lands):
        for d, bits in enumerate(plan):
            dev, pid = peer(bits)
            if kind == "forward":
                src, dst, dev = l.at[pid], l.at[peer(bits | 1)[1] if waiting else pid], peer(1)[0]
            else:
                src, dst = (srcs[a].at[pid] if kind == "scatter" else srcs[a]), l.at[pid if waiting else me]
            remote.append(pltpu.make_async_remote_copy(
                src_ref=src, dst_ref=dst, send_sem=send_sems.at[a * len(plan) + d], recv_sem=recv_sems.at[a * len(plan) + d],
                device_id=dev, device_id_type=MESH_ID))
        if kind != "forward":
            local.append(pltpu.make_async_copy(srcs[a].at[me] if kind == "scatter" else srcs[a], l.at[me], local_sems.at[a]))
    return remote, local


def _spread_start(srcs, kind, name, after=None, lands=None):
    if kind == "forward":
        srcs = []
    else:
        shapes = [a.shape if kind == "scatter" else (N_DEV,) + a.shape for a in srcs]
        lands = [lax.empty(shp, a.dtype) for shp, a in zip(shapes, srcs)]
    ns, nl, per = len(srcs), len(lands), len(PEERS[kind])
    extra = [] if after is None else [after]
    sem_shapes = [pltpu.SemaphoreType.DMA((nl * per,))] * 2 + ([pltpu.SemaphoreType.DMA((nl,))] if ns else [])

    def body(*refs):
        src_refs, land_refs = refs[:ns], refs[ns:ns + nl]
        sems = refs[ns + nl + len(extra):ns + nl + len(extra) + len(sem_shapes)]
        remote, local = _spread_copies(src_refs, land_refs, sems[0], sems[1], sems[2] if ns else None, kind, False)
        for cp in remote + local:
            cp.start()
        refs[-1][...] = jnp.zeros((8, 128), F32)

    outs = pl.pallas_call(
        body, name=name,
        out_shape=(*sem_shapes, *[pltpu.HBM(a.shape, a.dtype) for a in srcs + lands], jax.ShapeDtypeStruct((8, 128), F32)),
        in_specs=[HBM] * (ns + nl) + [pl.BlockSpec(memory_space=pl.ANY)] * len(extra),
        out_specs=(*[SEM] * len(sem_shapes), *[HBM] * (ns + nl), pl.BlockSpec(memory_space=pltpu.VMEM)),
        input_output_aliases={i: len(sem_shapes) + i for i in range(ns + nl)},
        compiler_params=pltpu.CompilerParams(has_side_effects=EFFECT),
    )(*[pltpu.with_memory_space_constraint(a, pltpu.HBM) for a in srcs + lands], *extra)
    k = len(sem_shapes)
    return outs[:k], list(outs[k:k + ns]), list(outs[k + ns:k + ns + nl]), outs[-1]


def _spread_wait(sems, srcs, lands, after, kind, name):
    ns, nl = len(srcs), len(lands)
    after = list(after) if isinstance(after, (list, tuple)) else [after]

    def body(*refs):
        src_refs, land_refs = refs[:ns], refs[ns:ns + nl]
        s = refs[ns + nl:ns + nl + len(sems)]
        remote, local = _spread_copies(src_refs, land_refs, s[0], s[1], s[2] if ns else None, kind, True)
        for cp in remote:
            cp.wait_send()
            cp.wait_recv()
        for cp in local:
            cp.wait()

    outs = pl.pallas_call(
        body, name=name, out_shape=tuple(pltpu.HBM(a.shape, a.dtype) for a in srcs + lands),
        in_specs=[HBM] * (ns + nl) + [SEM] * len(sems) + [pl.BlockSpec(memory_space=pl.ANY)] * len(after), out_specs=tuple([HBM] * (ns + nl)),
        input_output_aliases={i: i for i in range(ns + nl)}, compiler_params=pltpu.CompilerParams(has_side_effects=EFFECT),
    )(*srcs, *lands, *sems, *after)
    return list(outs[ns:])


SMALL = ("rel_bias", "norm1_g", "sgu_w", "sgu_b", "dil_qn_g", "dil_kn_g", "conv_w", "conv_b", "conv_ln_g", "conv_ln_b",
         "gqa_qn_g", "gqa_kn_g", "mix_norm_g", "norm2_g")
LARGE = ("w_in", "w_out", "w_gate", "w_up", "w_down")
EARLY = tuple(k for k in SMALL if k != "norm1_g")


def _local_step(x, target, p, B, S, fetch, emit, mid, early):
    T = B * S
    rope = _rope_tables(S)
    win = _bias_windows(p["rel_bias"], S)
    tile8 = lambda g: jnp.tile(g.reshape(1, HEAD_DIM), (1, N_HEADS))
    cols_b = (COL_BQ, COL_BK, COL_BV)
    cols_d = (COL_DQ, COL_DK128, COL_DV128)
    saved = []
    for l in range(DEPTH):
        s = {"x": x}
        s["ws"] = p["sgu_w"][l].astype(BF16)
        s["bias"] = jnp.repeat(p["sgu_b"][l].T, HEAD_DIM, axis=1)
        s["h"] = _rms_fwd(x, p["norm1_g"][l], f"rms1_fwd_{l}") if l == 0 else h_next
        s["win"] = fetch(l, "in", s["h"])
        s["cw"] = jnp.pad(s["win"]["conv_w"], ((0, 1), (0, 0))).reshape(32, 1, 512)
        z = s["z"] = _matmul(s["h"], s["win"]["w_in"], "nt", f"in_proj_{l}", tk=D_MODEL, out_dtype=BF16)
        s["bias"] = s["bias"] + mid(l, z)
        s["ya"] = _sgu_fwd(z, s["ws"], s["bias"], f"sgu_fwd_{l}")
        s["c"] = _conv_fwd1(z, s["cw"], p["conv_b"][l].reshape(1, 512), B, S, f"conv_fwd_{l}")
        s["ln"] = (p["conv_ln_g"][l].reshape(1, 512), p["conv_ln_b"][l].reshape(1, 512))
        s["gb"] = (tile8(p["dil_qn_g"][l]), tile8(p["dil_kn_g"][l]))
        s["gd"] = (tile8(p["gqa_qn_g"][l]), tile8(p["gqa_kn_g"][l])[:, :KV_WIDTH])
        s["qkv_b"] = _prep_fwd(z, *s["gb"], None, B, S, N_HEADS, cols_b, f"prep_b_fwd_{l}")
        s["qkv_d"] = _prep_fwd(z, *s["gd"], rope, B, S, KV_HEADS, cols_d, f"prep_d_fwd_{l}")
        s["ob"] = _attn_fwd(*s["qkv_b"], win, f"attn_b_fwd_{l}")
        s["od"] = _attn_fwd(*s["qkv_d"], None, f"attn_d_fwd_{l}", tq=GQA_TQ)
        s["gmix"] = p["mix_norm_g"][l].reshape(1, 2048)
        s["ycat"] = _mix_fwd(s["ya"], s["ob"], s["c"], s["od"], s["gmix"], *s["ln"], B, S, f"mix_fwd_{l}")
        s["wout"] = fetch(l, "out", s["ycat"])["w_out"]
        x1, s["h2"] = _matmul(s["ycat"], s["wout"], "nn", f"out_proj_{l}", res=x, tk=D_MODEL, norm=p["norm2_g"][l])
        s["x1"] = x1
        s["ffn"] = fetch(l, "ffn", s["h2"])
        s["gate"] = _mm_shard_out(s["h2"], s["ffn"]["w_gate"], "nt", f"ffn_gate_{l}", out_dtype=BF16, tm=1024)
        s["up"] = _mm_shard_out(s["h2"], s["ffn"]["w_up"], "nt", f"ffn_up_{l}", out_dtype=BF16, tm=1024)
        if l + 1 < DEPTH:
            x, s["act"], h_next = _ffn_down(s["gate"], s["up"], s["ffn"]["w_down"], x1, f"ffn_down_{l}", norm=p["norm1_g"][l + 1])
        else:
            dx, s["act"], loss_blk = _ffn_down(s["gate"], s["up"], s["ffn"]["w_down"], x1, f"ffn_down_{l}", target=target)
        saved.append(s)

    g = {k: [None] * DEPTH for k in SMALL if k != "rel_bias"}
    dwin_total = None
    for l in reversed(range(DEPTH)):
        s = saved[l]
        z, ffn = s["z"], s["ffn"]
        dgate, dup = _ffn_down_dx(dx, ffn["w_down"], s["gate"], s["up"], f"ffn_down_dx_{l}")
        tok = emit(l, "w_down", _mm_shard_m(s["act"], dx, f"ffn_down_dw_{l}", out_dtype=BF16, tm=FFN_BLOCK, tn=512, tk=T))
        tok += emit(l, "w_gate", _mm_shard_m(dgate, s["h2"], f"ffn_gate_dw_{l}", out_dtype=BF16, tm=FFN_BLOCK, tn=512, tk=T))
        tok += emit(l, "w_up", _mm_shard_m(dup, s["h2"], f"ffn_up_dw_{l}", out_dtype=BF16, tm=FFN_BLOCK, tn=512, tk=T))
        dh2 = _mm_shard_k([(dgate, ffn["w_gate"]), (dup, ffn["w_up"])], "nn", f"ffn_up_dx_{l}", tn=512, fold=FFN_GROUPS)
        dycat, dx1, dg2 = _out_proj_dx(dh2, s["x1"], p["norm2_g"][l] + tok, dx, s["wout"], f"out_proj_dx_{l}")
        g["norm2_g"][l] = dg2[0]
        tok = emit(l, "w_out", _matmul(s["ycat"], dx1, "tn", f"out_proj_dw_{l}", out_dtype=BF16, tn=1024, tk=T))
        dya, dob, dc, dod, dgm, dlg, dlb, dcb = _mix_bwd(s["ya"], s["ob"], s["c"], s["od"], dycat, s["gmix"] + tok, *s["ln"], B, S, f"mix_bwd_{l}")
        g["mix_norm_g"][l] = dgm[0]
        dz_a, dws, dbias = _sgu_bwd(z, dya, s["ws"], jnp.swapaxes(s["ws"], 1, 2), s["bias"], f"sgu_bwd_{l}")
        g["sgu_w"][l] = dws
        g["sgu_b"][l] = dbias.reshape(128, 8, HEAD_DIM).sum(-1).T
        g["conv_ln_g"][l], g["conv_ln_b"][l], g["conv_b"][l] = dlg[0], dlb[0], dcb[0]
        dz_ca, dz_cg, dcw = _conv_bwd2(z, dc, s["cw"], B, S, f"conv_bwd_{l}")
        g["conv_w"][l] = dcw.reshape(32, 512)[:CONV_WIDTH]
        dq, dk, dv, dwin = _attn_bwd(*s["qkv_b"], s["ob"], dob, win, f"attn_b_bwd_{l}")
        dwin_total = dwin if dwin_total is None else dwin_total + dwin
        dz_b, dgq, dgk = _prep_bwd(z, dq, dk, dv, *s["gb"], None, B, S, N_HEADS, cols_b, f"prep_b_bwd_{l}")
        g["dil_qn_g"][l] = dgq.reshape(N_HEADS, HEAD_DIM).sum(0)
        g["dil_kn_g"][l] = dgk.reshape(N_HEADS, HEAD_DIM).sum(0)
        dq, dk, dv = _attn_bwd(*s["qkv_d"], s["od"], dod, None, f"attn_d_bwd_{l}", tq=GQA_TQ)
        dz_d, dgq, dgk = _prep_bwd(z, dq, dk, dv, *s["gd"], rope, B, S, KV_HEADS, cols_d, f"prep_d_bwd_{l}")
        g["gqa_qn_g"][l] = dgq.reshape(N_HEADS, HEAD_DIM).sum(0)
        g["gqa_kn_g"][l] = dgk.reshape(KV_HEADS, HEAD_DIM).sum(0)
        dz = [dz_a, dz_b, dz_ca, dz_cg, dz_d]
        tok = jnp.zeros((), F32)
        if l == 0:
            done = {k: jnp.stack(v) for k, v in g.items() if k != "norm1_g"}
            done["rel_bias"] = _bias_fold(dwin_total, S, "bias_fold")
            tok = early(done)
        tok += emit(l, "w_in", _in_proj_dw(dz, s["h"], f"in_proj_dw_{l}"))
        dh = _in_proj_dx(dz, s["win"]["w_in"], f"in_proj_dx_{l}")
        dx, dg1 = _rms_bwd(dh, s["x"], p["norm1_g"][l] + tok, dx1, f"rms1_bwd_{l}")
        g["norm1_g"][l] = dg1[0]

    return loss_blk[0, 0], dx, jnp.stack(g["norm1_g"])


GROUPS = {"in": ("w_in",), "out": ("w_out",), "ffn": ("w_gate", "w_up", "w_down")}
COL_SHARDED = ("w_in", "w_gate", "w_up")


def kernel(x, rel_bias, norm1_g, w_in, sgu_w, sgu_b, dil_qn_g, dil_kn_g, conv_w, conv_b, conv_ln_g, conv_ln_b, gqa_qn_g, gqa_kn_g, mix_norm_g, w_out, norm2_g, w_gate, w_up, w_down, loss_target, m_rel_bias, m_norm1_g, m_w_in, m_sgu_w, m_sgu_b, m_dil_qn_g, m_dil_kn_g, m_conv_w, m_conv_b, m_conv_ln_g, m_conv_ln_b, m_gqa_qn_g, m_gqa_kn_g, m_mix_norm_g, m_w_out, m_norm2_g, m_w_gate, m_w_up, m_w_down, v_rel_bias, v_norm1_g, v_w_in, v_sgu_w, v_sgu_b, v_dil_qn_g, v_dil_kn_g, v_conv_w, v_conv_b, v_conv_ln_g, v_conv_ln_b, v_gqa_qn_g, v_gqa_kn_g, v_mix_norm_g, v_w_out, v_norm2_g, v_w_gate, v_w_up, v_w_down):
    w = dict(rel_bias=rel_bias, norm1_g=norm1_g, w_in=w_in, sgu_w=sgu_w, sgu_b=sgu_b, dil_qn_g=dil_qn_g, dil_kn_g=dil_kn_g, conv_w=conv_w,
             conv_b=conv_b, conv_ln_g=conv_ln_g, conv_ln_b=conv_ln_b, gqa_qn_g=gqa_qn_g, gqa_kn_g=gqa_kn_g, mix_norm_g=mix_norm_g,
             w_out=w_out, norm2_g=norm2_g, w_gate=w_gate, w_up=w_up, w_down=w_down)
    m = dict(rel_bias=m_rel_bias, norm1_g=m_norm1_g, w_in=m_w_in, sgu_w=m_sgu_w, sgu_b=m_sgu_b, dil_qn_g=m_dil_qn_g, dil_kn_g=m_dil_kn_g,
             conv_w=m_conv_w, conv_b=m_conv_b, conv_ln_g=m_conv_ln_g, conv_ln_b=m_conv_ln_b, gqa_qn_g=m_gqa_qn_g, gqa_kn_g=m_gqa_kn_g,
             mix_norm_g=m_mix_norm_g, w_out=m_w_out, norm2_g=m_norm2_g, w_gate=m_w_gate, w_up=m_w_up, w_down=m_w_down)
    v = dict(rel_bias=v_rel_bias, norm1_g=v_norm1_g, w_in=v_w_in, sgu_w=v_sgu_w, sgu_b=v_sgu_b, dil_qn_g=v_dil_qn_g, dil_kn_g=v_dil_kn_g,
             conv_w=v_conv_w, conv_b=v_conv_b, conv_ln_g=v_conv_ln_g, conv_ln_b=v_conv_ln_b, gqa_qn_g=v_gqa_qn_g, gqa_kn_g=v_gqa_kn_g,
             mix_norm_g=v_mix_norm_g, w_out=v_w_out, norm2_g=v_norm2_g, w_gate=v_w_gate, w_up=v_w_up, w_down=v_w_down)
    names = list(w)
    B, S, D = x.shape
    T = B * S
    me = 4 * lax.axis_index("x") + 2 * lax.axis_index("y") + lax.axis_index("c")

    view = lambda a, k: jnp.swapaxes(a, 1, 2) if k in COL_SHARDED else a
    bf = {k: view(w[k], k).astype(BF16) for k in LARGE}
    spreads, forwards = {}, {}

    def start_gather(l, group, after=None):
        srcs = [bf[k][l] for k in GROUPS[group]] + ([conv_w[l]] if group == "in" else [])
        spreads[l, group] = _spread_start(srcs, "chips", f"gather_{group}_{l}_start", after)
        return spreads[l, group][3][0, 0]

    def forward(l, group, after):
        sems, srcs, lands, _ = spreads[l, group]
        lands = _spread_wait(sems, srcs, lands, after, "chips", f"gather_{group}_{l}_wait")
        forwards[l, group] = _spread_start(None, "forward", f"forward_{group}_{l}_start", lands=lands)
        return forwards[l, group][3]

    def landed(l, group, after):
        sems, _, lands, _ = forwards[l, group]
        return _spread_wait(sems, [], lands, after, "forward", f"forward_{group}_{l}_wait")

    tok0 = start_gather(0, "in") + start_gather(0, "out") + start_gather(0, "ffn")
    small = {k: w[k] for k in SMALL}
    small["norm1_g"] = norm1_g.at[0].add(tok0)

    def mid(l, z):
        if l > 0:
            return jnp.zeros((), F32)
        return start_gather(1, "in", z) + start_gather(1, "out", z) + start_gather(1, "ffn", z)

    def fetch(l, group, after):
        if group == "in":
            tok = forward(0, "in", after) if l == 0 else after
        elif group == "out":
            tok = forward(l, "ffn", [after, forward(l, "out", after)])
        else:
            tok = forward(1, "in", after) if l == 0 else after
        got = dict(zip(GROUPS[group] + ("conv_w",), landed(l, group, [after, tok])))
        if group == "in":
            got["w_in"] = got["w_in"].reshape(IN_WIDTH, D)
            got["conv_w"] = jnp.transpose(got["conv_w"], (1, 0, 2)).reshape(CONV_WIDTH, 512)
        if group == "out":
            got["w_out"] = got["w_out"].reshape(D, D)
        if group == "ffn":
            got = {k: a.reshape(FFN_GROUPS, -1, D) for k, a in got.items()}
        return got

    scatters = {}

    def emit(l, k, dw):
        dw = dw.reshape(N_DEV, -1, D)
        scatters[l, k] = _spread_start([dw], "scatter", f"scatter_{k}_{l}_start")
        return scatters[l, k][3][0, 0]

    flat2 = lambda a: a.reshape(-1, a.shape[-1])
    small_spread = []

    def early(done):
        small_spread.append(_spread_start([flat2(done[k]) for k in EARLY], "gather", "gather_small_grads_start"))
        return small_spread[0][3][0, 0]

    loss_part, dx, dnorm1 = _local_step(x.reshape(T, D), loss_target.reshape(T, D), small, B, S, fetch, emit, mid, early)
    loss = lax.psum(loss_part, ("x", "y", "c"))

    out_g, out_d, out_m, out_v = {}, {}, {}, {}

    def update_large(k, after):
        shp = view(w[k], k).shape
        two_d = lambda a: view(a, k).reshape(-1, shp[-1])
        res = None
        for l in reversed(range(DEPTH)):
            sems, srcs, lands, _ = scatters[l, k]
            stack = _spread_wait(sems, srcs, lands, after, "scatter", f"scatter_{k}_{l}_wait")[0]
            res = _adamw(two_d(w[k]), two_d(m[k]), two_d(v[k]), stack.reshape(N_DEV, -1, shp[-1]), f"adamw_{k}_{l}", layer=l, prev=res)
        out_g[k], out_d[k], out_m[k], out_v[k] = [view(a.reshape(shp), k) for a in res]
        return res[0]

    late_sems, late_srcs, late_lands, late_tok = _spread_start([flat2(dnorm1)], "gather", "gather_norm1_grad_start")
    after = [dx, late_tok]
    for k in ("w_down", "w_gate", "w_up", "w_out"):
        after = update_large(k, after)
    sems, srcs, lands, _ = small_spread[0]
    stacks = dict(zip(EARLY, _spread_wait(sems, srcs, lands, after, "gather", "gather_small_grads_wait")))
    stacks["norm1_g"] = _spread_wait(late_sems, late_srcs, late_lands, after, "gather", "gather_norm1_grad_wait")[0]
    for k in SMALL:
        stack = stacks[k]
        if k == "conv_w":
            stack = lax.dynamic_slice_in_dim(stack, me * (512 // N_DEV), 512 // N_DEV, axis=2)
        res = _adamw(flat2(w[k]), flat2(m[k]), flat2(v[k]), stack, f"adamw_{k}")
        out_g[k], out_d[k], out_m[k], out_v[k] = [a.reshape(w[k].shape) for a in res]
        after = res[0]
    update_large("w_in", after)

    return (loss, dx.reshape(B, S, D), *[out_g[k] for k in names], *[out_d[k] for k in names],
            *[out_m[k] for k in names], *[out_v[k] for k in names])
```

```python
import functools
import math

import numpy as np
import jax
import jax.numpy as jnp
from jax import lax
from jax.experimental import pallas as pl
from jax.experimental.pallas import tpu as pltpu

F32 = jnp.float32
BF16 = jnp.bfloat16
HIGHEST = lax.Precision.HIGHEST
MESH_ID = pl.DeviceIdType.MESH

D_MODEL = 2048
DEPTH = 2
HEAD_DIM = 64
GROUP_WIDTH = 512
N_HEADS = 8
KV_HEADS = 2
KV_WIDTH = 128
SGU_CHUNK = 128
CONV_WIDTH = 31
CONV_PAD = 16
GRID_W = 64
ROPE_THETA = 10000.0
REL_BUCKETS = 32
REL_MAX_DIST = 1024
DIL_PATTERNS = ((128, 1), (512, 4), (2048, 16))
FFN_HIDDEN = 5632
IN_WIDTH = 4352
RMS_EPS = 1e-6
LN_EPS = 1e-5
MASKED = -1e30
N_DEV = 8

ADAM_LR = 0.001
ADAM_B1 = 0.9
ADAM_B2 = 0.999
ADAM_EPS = 1e-08
ADAM_WD = 0.01
ADAM_STEP = 10

COL_AU, COL_AV, COL_BQ, COL_BK, COL_BV, COL_CA, COL_CG, COL_DQ = range(8)
COL_DK128, COL_DV128 = 32, 33

VMEM_LIMIT = 56 * 1024 * 1024
FFN_GROUPS = 2
FFN_BLOCK = 1408
ATTN_TQ = 256
GQA_TQ = 512
ROW_TILE = 512


def _params(sem=None, vmem=VMEM_LIMIT):
    return pltpu.CompilerParams(dimension_semantics=sem, vmem_limit_bytes=vmem)


def _dot(a, b, dims, precision=None):
    return lax.dot_general(a, b, (dims, ((), ())), precision=precision, preferred_element_type=F32)


def _nn(a, b, precision=None):
    return _dot(a, b, ((1,), (0,)), precision)


def _nt(a, b):
    return _dot(a, b, ((1,), (1,)))


def _tn(a, b):
    return _dot(a, b, ((0,), (0,)))


DIMS = {"nn": ((1,), (0,)), "nt": ((1,), (1,)), "tn": ((0,), (0,))}


def _pick(n, cands):
    for c in cands:
        if n % c == 0:
            return c
    return n


def _mm_call(name, mode, pairs, specs, o_spec, out_sds, grid, acc_shape, res=None, fold=None, norm=None):
    npair, nk, dims = len(pairs), grid[2], DIMS[mode]

    def body(*refs):
        ab = refs[:2 * npair]
        at = 2 * npair
        r_ref = refs[at] if res is not None else None
        at += res is not None
        g_ref = refs[at] if norm is not None else None
        at += norm is not None
        o_ref = refs[at]
        h_ref = refs[at + 1] if norm is not None else None
        part = None
        for t in range(npair):
            for s in ([None] if fold is None else range(fold)):
                a_blk = ab[2 * t][...] if s is None else ab[2 * t][s]
                b_blk = ab[2 * t + 1][...] if s is None else ab[2 * t + 1][s]
                d = _dot(a_blk.astype(BF16), b_blk.astype(BF16), dims)
                part = d if part is None else part + d

        def finish(r):
            if r_ref is not None:
                r = r + r_ref[...]
            o_ref[...] = r.astype(o_ref.dtype)
            if h_ref is not None:
                h_ref[...] = _rms_rows(r, g_ref[...])

        if nk == 1:
            finish(part)
            return
        acc, k = refs[-1], pl.program_id(2)

        @pl.when(k == 0)
        def _():
            acc[...] = part

        @pl.when(k > 0)
        def _():
            acc[...] += part

        @pl.when(k == nk - 1)
        def _():
            finish(acc[...])

    ins = [t for pair in pairs for t in pair]
    in_specs = [t for pair in specs for t in pair]
    if res is not None:
        ins.append(res)
        in_specs.append(o_spec)
    out_specs = o_spec
    if norm is not None:
        ins.append(norm)
        in_specs.append(pl.BlockSpec(norm.shape, lambda *_: (0, 0)))
        out_specs, out_sds = [o_spec, o_spec], [out_sds, jax.ShapeDtypeStruct(out_sds.shape, BF16)]
    return pl.pallas_call(
        body, name=name, grid=grid, in_specs=in_specs, out_specs=out_specs, out_shape=out_sds,
        scratch_shapes=[pltpu.VMEM(acc_shape, F32)] if nk > 1 else [],
        compiler_params=_params(("parallel", "parallel", "arbitrary")),
    )(*ins)


def _rms_rows(x, g):
    return (x * lax.rsqrt(jnp.mean(x * x, axis=-1, keepdims=True) + RMS_EPS) * g).astype(BF16)


def _matmul(a, b, mode, name, res=None, out_dtype=F32, tm=512, tn=None, tk=None, norm=None):
    if mode == "nn":
        (M, K), N = a.shape, b.shape[1]
    elif mode == "nt":
        (M, K), N = a.shape, b.shape[0]
    else:
        (K, M), N = a.shape, b.shape[1]
    tm = min(tm, M)
    tn = tn or _pick(N, (2176, 2048, 1408, 1024, 512))
    tk = tk or _pick(K, (1024, 2176, 1408, 512))
    assert M % tm == 0 and N % tn == 0 and K % tk == 0, (M, N, K, tm, tn, tk)
    a_spec = pl.BlockSpec((tk, tm), lambda i, j, k: (k, i)) if mode == "tn" else pl.BlockSpec((tm, tk), lambda i, j, k: (i, k))
    b_spec = pl.BlockSpec((tn, tk), lambda i, j, k: (j, k)) if mode == "nt" else pl.BlockSpec((tk, tn), lambda i, j, k: (k, j))
    o_spec = pl.BlockSpec((tm, tn), lambda i, j, k: (i, j))
    assert norm is None or tn == N
    return _mm_call(name, mode, [(a, b)], [(a_spec, b_spec)], o_spec, jax.ShapeDtypeStruct((M, N), out_dtype),
                    (M // tm, N // tn, K // tk), (tm, tn), res, norm=None if norm is None else norm.reshape(1, N))


def _mm_shard_out(a, bs, mode, name, out_dtype=F32, tm=512, tk=None):
    J = bs.shape[0]
    n = bs.shape[1] if mode == "nt" else bs.shape[2]
    (K, M) = a.shape if mode == "tn" else a.shape[::-1]
    tm = min(tm, M)
    tk = tk or (K if mode != "tn" else _pick(K, (1024, 512)))
    a_spec = pl.BlockSpec((tk, tm), lambda j, i, k: (k, i)) if mode == "tn" else pl.BlockSpec((tm, tk), lambda j, i, k: (i, k))
    b_spec = pl.BlockSpec((None, n, tk), lambda j, i, k: (j, 0, k)) if mode == "nt" else pl.BlockSpec((None, tk, n), lambda j, i, k: (j, k, 0))
    o_spec = pl.BlockSpec((None, tm, n), lambda j, i, k: (j, i, 0))
    return _mm_call(name, mode, [(a, bs)], [(a_spec, b_spec)], o_spec, jax.ShapeDtypeStruct((J, M, n), out_dtype),
                    (J, M // tm, K // tk), (tm, n))


def _mm_shard_k(pairs, mode, name, res=None, out_dtype=F32, tm=512, tn=None, fold=1):
    J, M, n = pairs[0][0].shape
    N = pairs[0][1].shape[2] if mode == "nn" else pairs[0][1].shape[1]
    tm = min(tm, M)
    tn = tn or _pick(N, (2048, 1024, 512))
    a_spec = pl.BlockSpec((fold, tm, n), lambda i, j, k: (k, i, 0))
    b_spec = pl.BlockSpec((fold, n, tn), lambda i, j, k: (k, 0, j)) if mode == "nn" else pl.BlockSpec((fold, tn, n), lambda i, j, k: (k, j, 0))
    o_spec = pl.BlockSpec((tm, tn), lambda i, j, k: (i, j))
    return _mm_call(name, mode, pairs, [(a_spec, b_spec)] * len(pairs), o_spec, jax.ShapeDtypeStruct((M, N), out_dtype),
                    (M // tm, N // tn, J // fold), (tm, tn), res, fold)


def _mm_shard_m(as_, b, name, out_dtype=F32, tm=None, tn=None, tk=512):
    J, K, n = as_.shape
    N = b.shape[1]
    tm = tm or n
    tn = tn or _pick(N, (2048, 1024, 512))
    tk = min(tk, K)
    nn = N // tn
    a_spec = pl.BlockSpec((None, tk, tm), lambda j, i, k: (j, k, i // nn))
    b_spec = pl.BlockSpec((tk, tn), lambda j, i, k: (k, i % nn))
    o_spec = pl.BlockSpec((None, tm, tn), lambda j, i, k: (j, i // nn, i % nn))
    return _mm_call(name, "tn", [(as_, b)], [(a_spec, b_spec)], o_spec, jax.ShapeDtypeStruct((J, n, N), out_dtype),
                    (J, (n // tm) * nn, K // tk), (tm, tn))


def _out_proj_dx(dh, x, g, dres, w, name):
    T, D = x.shape
    N = w.shape[0]
    tm = min(256, T)

    def body(dh_ref, x_ref, g_ref, dres_ref, w_ref, o_ref, dx_ref, dg_ref):
        @pl.when(pl.program_id(0) == 0)
        def _():
            dg_ref[...] = jnp.zeros_like(dg_ref)

        xv, dhv = x_ref[...], dh_ref[...]
        r = lax.rsqrt(jnp.mean(xv * xv, axis=-1, keepdims=True) + RMS_EPS)
        y = xv * r
        dy = dhv * g_ref[...]
        dx = dres_ref[...] + r * (dy - y * jnp.mean(dy * y, axis=-1, keepdims=True))
        dx_ref[...] = dx
        dg_ref[...] += jnp.sum(dhv * y, axis=0, keepdims=True)
        o_ref[...] = _nt(dx.astype(BF16), w_ref[...])

    row = pl.BlockSpec((tm, D), lambda i: (i, 0))
    vec = pl.BlockSpec((1, D), lambda i: (0, 0))
    return pl.pallas_call(
        body, name=name, grid=(T // tm,), in_specs=[row, row, vec, row, pl.BlockSpec((N, D), lambda i: (0, 0))],
        out_specs=[pl.BlockSpec((tm, N), lambda i: (i, 0)), row, vec],
        out_shape=[jax.ShapeDtypeStruct((T, N), F32), jax.ShapeDtypeStruct((T, D), F32), jax.ShapeDtypeStruct((1, D), F32)],
        compiler_params=_params(("arbitrary",)),
    )(dh, x, g.reshape(1, D), dres, w)


def _in_proj_dw(pieces, h, name):
    T, D = h.shape
    tm = 256
    nbs = [p.shape[1] // tm for p in pieces]
    los = [sum(nbs[:t]) for t in range(len(pieces))]

    def body(*refs):
        h_ref, o_ref = refs[-2:]
        i = pl.program_id(0)
        for p_ref, lo, nb in zip(refs[:-2], los, nbs):
            @pl.when((i >= lo) & (i < lo + nb))
            def _():
                o_ref[...] = _tn(p_ref[...], h_ref[...]).astype(BF16)

    specs = [pl.BlockSpec((T, tm), (lambda lo, nb: lambda i: (0, jnp.clip(i - lo, 0, nb - 1)))(lo, nb)) for lo, nb in zip(los, nbs)]
    return pl.pallas_call(body, name=name, grid=(sum(nbs),), in_specs=specs + [pl.BlockSpec((T, D), lambda i: (0, 0))],
                          out_specs=pl.BlockSpec((tm, D), lambda i: (i, 0)), out_shape=jax.ShapeDtypeStruct((sum(nbs) * tm, D), BF16),
                          compiler_params=_params(("parallel",)))(*pieces, h)


def _in_proj_dx(pieces, w, x, g, dres, name):
    T, D = x.shape
    K = w.shape[0]
    tm = min(256, T)
    n = len(pieces)
    widths = [p.shape[1] for p in pieces]
    offs = [sum(widths[:t]) for t in range(n)]

    def body(*refs):
        w_ref, x_ref, g_ref, dres_ref, dx_ref, dg_ref = refs[n:]

        @pl.when(pl.program_id(0) == 0)
        def _():
            dg_ref[...] = jnp.zeros_like(dg_ref)

        dh = None
        for p_ref, off, wd in zip(refs[:n], offs, widths):
            d = _nn(p_ref[...], w_ref[off:off + wd, :])
            dh = d if dh is None else dh + d
        xv = x_ref[...]
        r = lax.rsqrt(jnp.mean(xv * xv, axis=-1, keepdims=True) + RMS_EPS)
        y = xv * r
        dy = dh * g_ref[...]
        dx_ref[...] = dres_ref[...] + r * (dy - y * jnp.mean(dy * y, axis=-1, keepdims=True))
        dg_ref[...] += jnp.sum(dh * y, axis=0, keepdims=True)

    specs = [pl.BlockSpec((tm, wd), lambda i: (i, 0)) for wd in widths]
    row = pl.BlockSpec((tm, D), lambda i: (i, 0))
    vec = pl.BlockSpec((1, D), lambda i: (0, 0))
    return pl.pallas_call(body, name=name, grid=(T // tm,), in_specs=specs + [pl.BlockSpec((K, D), lambda i: (0, 0)), row, vec, row],
                          out_specs=[row, vec], out_shape=[jax.ShapeDtypeStruct((T, D), F32), jax.ShapeDtypeStruct((1, D), F32)],
                          compiler_params=_params(("arbitrary",)))(*pieces, w, x, g.reshape(1, D), dres)


def _seg_matrix(width):
    return jnp.asarray(np.kron(np.eye(width // HEAD_DIM, dtype=np.float32), np.full((HEAD_DIM, HEAD_DIM), 1.0 / HEAD_DIM, np.float32)), BF16)


def _segmean(v, p):
    hi = v.astype(BF16)
    r = v - hi.astype(F32)
    mid = r.astype(BF16)
    lo = (r - mid.astype(F32)).astype(BF16)
    w = min(256, v.shape[1])
    pw = p[:w, :w]
    halves = []
    for c in range(v.shape[1] // w):
        cols = slice(c * w, (c + 1) * w)
        halves.append(_nn(hi[:, cols], pw) + _nn(mid[:, cols], pw) + _nn(lo[:, cols], pw))
    return halves[0] if len(halves) == 1 else jnp.concatenate(halves, axis=1)


def _gelu(x):
    c0 = math.sqrt(2.0 / math.pi)
    t = jnp.tanh(c0 * (x + 0.044715 * x * x * x))
    return 0.5 * x * (1.0 + t), t


def _gelu_grad(x, t):
    c0 = math.sqrt(2.0 / math.pi)
    return 0.5 * (1.0 + t) + 0.5 * x * (1.0 - t * t) * c0 * (1.0 + 3.0 * 0.044715 * x * x)


def _sigmoid(x):
    return 1.0 / (1.0 + jnp.exp(-x))


def _rms_fwd(x, g, name):
    T, D = x.shape
    tm = min(256, T)

    def body(x_ref, g_ref, o_ref):
        o_ref[...] = _rms_rows(x_ref[...], g_ref[...])

    return pl.pallas_call(
        body, name=name, grid=(T // tm,),
        in_specs=[pl.BlockSpec((tm, D), lambda i: (i, 0)), pl.BlockSpec((1, D), lambda i: (0, 0))],
        out_specs=pl.BlockSpec((tm, D), lambda i: (i, 0)), out_shape=jax.ShapeDtypeStruct((T, D), BF16),
        compiler_params=_params(("parallel",)),
    )(x, g.reshape(1, D))


def _sgu_core(zu, zv, ws_ref, bias, p):
    ug, tu = _gelu(zu)
    vg, tv = _gelu(zv)
    xc = vg - _segmean(vg, p)
    rs = lax.rsqrt(_segmean(xc * xc, p) + LN_EPS)
    vn = xc * rs
    vnb = vn.astype(BF16)
    low = lax.broadcasted_iota(jnp.int32, (SGU_CHUNK, 128), 1) < HEAD_DIM
    parts = []
    for j in range(4):
        vp = vnb[:, 128 * j:128 * (j + 1)]
        parts.append(jnp.where(low, _nn(ws_ref[2 * j], vp), _nn(ws_ref[2 * j + 1], vp)))
    mixed = jnp.concatenate(parts, axis=1) + bias
    return ug, tu, tv, rs, vn, vnb, mixed, low


SGU_ROWS = 4 * SGU_CHUNK


def _sgu_fwd(z, ws, bias, name):
    T = z.shape[0]

    def body(zu_ref, zv_ref, ws_ref, b_ref, p_ref, y_ref):
        for r in range(0, SGU_ROWS, SGU_CHUNK):
            rows = slice(r, r + SGU_CHUNK)
            ug, _, _, _, _, _, mixed, _ = _sgu_core(zu_ref[rows, :], zv_ref[rows, :], ws_ref, b_ref[...], p_ref[...])
            y_ref[rows, :] = ug * mixed

    full = lambda shape: pl.BlockSpec(shape, lambda i: (0,) * len(shape))
    return pl.pallas_call(
        body, name=name, grid=(T // SGU_ROWS,),
        in_specs=[pl.BlockSpec((SGU_ROWS, 512), lambda i: (i, COL_AU)), pl.BlockSpec((SGU_ROWS, 512), lambda i: (i, COL_AV)),
                  full((8, 128, 128)), full((128, 512)), full((512, 512))],
        out_specs=pl.BlockSpec((SGU_ROWS, 512), lambda i: (i, 0)), out_shape=jax.ShapeDtypeStruct((T, 512), F32),
        compiler_params=_params(("parallel",)),
    )(z, z, ws, bias, _seg_matrix(512))


def _sgu_bwd(z, dy, ws, ws_t, bias, name):
    T = z.shape[0]

    def body(zu_ref, zv_ref, dy_ref, ws_ref, wst_ref, b_ref, p_ref, dz_ref, dws_ref, db_ref):
        @pl.when(pl.program_id(0) == 0)
        def _():
            dws_ref[...] = jnp.zeros_like(dws_ref)
            db_ref[...] = jnp.zeros_like(db_ref)

        p = p_ref[...]
        zero = jnp.zeros((SGU_CHUNK, 128), BF16)
        dws = [None] * 8
        db = None
        for r in range(0, SGU_ROWS, SGU_CHUNK):
            rows = slice(r, r + SGU_CHUNK)
            zu, zv = zu_ref[rows, :], zv_ref[rows, :]
            ug, tu, tv, rs, vn, vnb, mixed, low = _sgu_core(zu, zv, ws_ref, b_ref[...], p)
            dyv = dy_ref[rows, :]
            dmixed = dyv * ug
            db = dmixed if db is None else db + dmixed
            dmb = dmixed.astype(BF16)
            parts = []
            for j in range(4):
                dmp, vp = dmb[:, 128 * j:128 * (j + 1)], vnb[:, 128 * j:128 * (j + 1)]
                for g, d in ((2 * j, _nt(jnp.where(low, dmp, zero), vp)), (2 * j + 1, _nt(jnp.where(low, zero, dmp), vp))):
                    dws[g] = d if dws[g] is None else dws[g] + d
                parts.append(jnp.where(low, _nn(wst_ref[2 * j], dmp), _nn(wst_ref[2 * j + 1], dmp)))
            dvn = jnp.concatenate(parts, axis=1)
            dvg = rs * (dvn - _segmean(dvn, p) - vn * _segmean(dvn * vn, p))
            dz_ref[rows, 0:512] = (dyv * mixed * _gelu_grad(zu, tu)).astype(BF16)
            dz_ref[rows, 512:1024] = (dvg * _gelu_grad(zv, tv)).astype(BF16)
        db_ref[...] += db
        for g in range(8):
            dws_ref[g] += dws[g]

    full = lambda shape: pl.BlockSpec(shape, lambda i: (0,) * len(shape))
    return pl.pallas_call(
        body, name=name, grid=(T // SGU_ROWS,),
        in_specs=[pl.BlockSpec((SGU_ROWS, 512), lambda i: (i, COL_AU)), pl.BlockSpec((SGU_ROWS, 512), lambda i: (i, COL_AV)),
                  pl.BlockSpec((SGU_ROWS, 512), lambda i: (i, 0)), full((8, 128, 128)), full((8, 128, 128)), full((128, 512)), full((512, 512))],
        out_specs=[pl.BlockSpec((SGU_ROWS, 1024), lambda i: (i, 0)), full((8, 128, 128)), full((128, 512))],
        out_shape=[jax.ShapeDtypeStruct((T, 1024), BF16), jax.ShapeDtypeStruct((8, 128, 128), F32), jax.ShapeDtypeStruct((128, 512), F32)],
        compiler_params=_params(("arbitrary",)),
    )(z, z, dy, ws, ws_t, bias, _seg_matrix(512))


CONV_ROWS = 256


def _conv_taps(pad_ref, w_ref, base, flip):
    acc = None
    for k in range(CONV_WIDTH):
        wk = w_ref[CONV_WIDTH - 1 - k if flip else k]
        t = wk * pad_ref[base + k + 1:base + k + 1 + CONV_ROWS, :]
        acc = t if acc is None else acc + t
    return acc


def _conv_fwd1(z, w, cb, B, S, name):
    T = B * S
    rows = min(CONV_ROWS, S)
    assert rows == CONV_ROWS

    def body(a_ref, g_ref, w_ref, cb_ref, c_ref, pad):
        pad[0:CONV_PAD, :] = jnp.zeros((CONV_PAD, 128), F32)
        pad[CONV_PAD + S:2 * CONV_PAD + S, :] = jnp.zeros((CONV_PAD, 128), F32)
        pad[CONV_PAD:CONV_PAD + S, :] = a_ref[...] * _sigmoid(g_ref[...])

        for base in range(0, S, CONV_ROWS):
            c_ref[base:base + CONV_ROWS, :] = _conv_taps(pad, w_ref, base, False) + cb_ref[...]

    return pl.pallas_call(
        body, name=name, grid=(4, B),
        in_specs=[pl.BlockSpec((S, 128), lambda j, b: (b, 4 * COL_CA + j)), pl.BlockSpec((S, 128), lambda j, b: (b, 4 * COL_CG + j)),
                  pl.BlockSpec((32, 1, 128), lambda j, b: (0, 0, j)), pl.BlockSpec((1, 128), lambda j, b: (0, j))],
        out_specs=pl.BlockSpec((S, 128), lambda j, b: (b, j)), out_shape=jax.ShapeDtypeStruct((T, 512), F32),
        scratch_shapes=[pltpu.VMEM((S + 2 * CONV_PAD, 128), F32)], compiler_params=_params(("parallel", "parallel")),
    )(z, z, w, cb)


def _ln_rows(c):
    mu = jnp.mean(c, axis=-1, keepdims=True)
    xc = c - mu
    rs = lax.rsqrt(jnp.mean(xc * xc, axis=-1, keepdims=True) + LN_EPS)
    return xc * rs, rs


def _conv_bwd2(z, dc, w, B, S, name):
    T = B * S

    def body(a_ref, g_ref, dc_ref, w_ref, da_ref, dg_ref, dw_ref, hpad, dpad):
        @pl.when(pl.program_id(1) == 0)
        def _():
            dw_ref[...] = jnp.zeros_like(dw_ref)

        zeros = jnp.zeros((CONV_PAD, 128), F32)
        for ref in (hpad, dpad):
            ref[0:CONV_PAD, :] = zeros
            ref[CONV_PAD + S:2 * CONV_PAD + S, :] = zeros
        hpad[CONV_PAD:CONV_PAD + S, :] = a_ref[...] * _sigmoid(g_ref[...])
        dpad[CONV_PAD:CONV_PAD + S, :] = dc_ref[...]
        dws = [None] * CONV_WIDTH
        for base in range(0, S, CONV_ROWS):
            rows = slice(base, base + CONV_ROWS)
            dh = _conv_taps(dpad, w_ref, base, True)
            sg = _sigmoid(g_ref[rows, :])
            da_ref[rows, :] = (dh * sg).astype(BF16)
            dg_ref[rows, :] = (dh * a_ref[rows, :] * sg * (1.0 - sg)).astype(BF16)
            dcv = dc_ref[rows, :]
            for k in range(CONV_WIDTH):
                prod = dcv * hpad[base + k + 1:base + k + 1 + CONV_ROWS, :]
                part = jnp.sum(prod.reshape(CONV_ROWS // 8, 8, 128), axis=0)
                dws[k] = part if dws[k] is None else dws[k] + part
        for k in range(CONV_WIDTH):
            dw_ref[k] += jnp.sum(dws[k], axis=0, keepdims=True)

    return pl.pallas_call(
        body, name=name, grid=(4, B),
        in_specs=[pl.BlockSpec((S, 128), lambda j, b: (b, 4 * COL_CA + j)), pl.BlockSpec((S, 128), lambda j, b: (b, 4 * COL_CG + j)),
                  pl.BlockSpec((S, 128), lambda j, b: (b, j)), pl.BlockSpec((32, 1, 128), lambda j, b: (0, 0, j))],
        out_specs=[pl.BlockSpec((S, 128), lambda j, b: (b, j)), pl.BlockSpec((S, 128), lambda j, b: (b, j)),
                   pl.BlockSpec((32, 1, 128), lambda j, b: (0, 0, j))],
        out_shape=[jax.ShapeDtypeStruct((T, 512), BF16), jax.ShapeDtypeStruct((T, 512), BF16), jax.ShapeDtypeStruct((32, 1, 512), F32)],
        scratch_shapes=[pltpu.VMEM((S + 2 * CONV_PAD, 128), F32), pltpu.VMEM((S + 2 * CONV_PAD, 128), F32)],
        compiler_params=_params(("parallel", "arbitrary")),
    )(z, z, dc, w)


def _swap16(x):
    n = x.shape[1]
    first = (lax.broadcasted_iota(jnp.int32, x.shape, 1) % 32) < 16
    return jnp.where(first, pltpu.roll(x, n - 16, 1), pltpu.roll(x, 16, 1))


def _rope(x, cos, sin):
    return x * cos + _swap16(x) * sin


def _rope_t(dy, cos, sin):
    return dy * cos + _swap16(dy * sin)


def _qk_norm(x, p):
    r = lax.rsqrt(_segmean(x * x, p) + RMS_EPS)
    return x * r, r


def _store_heads(ref, val, n):
    for h in range(n):
        ref[h] = val[:, HEAD_DIM * h:HEAD_DIM * (h + 1)].astype(ref.dtype)


def _load_heads(ref, n):
    return jnp.concatenate([ref[h] for h in range(n)], axis=1)


def _prep_fwd(z, gq, gk, rope, B, S, kv_heads, cols, name):
    tm = min(ROW_TILE, S)
    ns = S // tm
    kw = kv_heads * HEAD_DIM
    scale = HEAD_DIM ** -0.5
    qc, kc, vc = cols

    def body(*refs):
        if rope is None:
            q_ref, k_ref, v_ref, gq_ref, gk_ref, p_ref, qo, ko, vo = refs
        else:
            q_ref, k_ref, v_ref, gq_ref, gk_ref, p_ref, cos_ref, sin_ref, qo, ko, vo = refs
        p = p_ref[...]
        qn, _ = _qk_norm(q_ref[...], p)
        kn, _ = _qk_norm(k_ref[...], p[:kw, :kw])
        qn, kn = qn * gq_ref[...], kn * gk_ref[...]
        if rope is not None:
            cos, sin = cos_ref[...], sin_ref[...]
            qn, kn = _rope(qn, cos, sin), _rope(kn, cos[:, :kw], sin[:, :kw])
        _store_heads(qo, qn * scale, N_HEADS)
        _store_heads(ko, kn, kv_heads)
        _store_heads(vo, v_ref[...], kv_heads)

    row = lambda w, c: pl.BlockSpec((tm, w), lambda b, i: (b * ns + i, c))
    const = lambda shape: pl.BlockSpec(shape, lambda b, i: (0,) * len(shape))
    heads = lambda n: pl.BlockSpec((None, n, tm, HEAD_DIM), lambda b, i: (b, 0, i, 0))
    ins = [z, z, z, gq, gk, _seg_matrix(512)]
    specs = [row(512, qc), row(kw, kc), row(kw, vc), const((1, 512)), const((1, kw)), const((512, 512))]
    if rope is not None:
        ins += list(rope)
        specs += [pl.BlockSpec((tm, 512), lambda b, i: (i, 0))] * 2
    return pl.pallas_call(
        body, name=name, grid=(B, ns), in_specs=specs, out_specs=[heads(N_HEADS), heads(kv_heads), heads(kv_heads)],
        out_shape=[jax.ShapeDtypeStruct((B, N_HEADS, S, HEAD_DIM), BF16), jax.ShapeDtypeStruct((B, kv_heads, S, HEAD_DIM), BF16),
                   jax.ShapeDtypeStruct((B, kv_heads, S, HEAD_DIM), BF16)],
        compiler_params=_params(("parallel", "parallel")),
    )(*ins)


def _prep_bwd(z, dq, dk, dv, gq, gk, rope, B, S, kv_heads, cols, name):
    T = B * S
    tm = min(ROW_TILE, S)
    ns = S // tm
    kw = kv_heads * HEAD_DIM
    scale = HEAD_DIM ** -0.5
    qc, kc, _ = cols

    def body(*refs):
        if rope is None:
            q_ref, k_ref, dq_ref, dk_ref, dv_ref, gq_ref, gk_ref, p_ref, dz_ref, dgq_ref, dgk_ref = refs
        else:
            q_ref, k_ref, dq_ref, dk_ref, dv_ref, gq_ref, gk_ref, p_ref, cos_ref, sin_ref, dz_ref, dgq_ref, dgk_ref = refs

        @pl.when((pl.program_id(0) == 0) & (pl.program_id(1) == 0))
        def _():
            dgq_ref[...] = jnp.zeros_like(dgq_ref)
            dgk_ref[...] = jnp.zeros_like(dgk_ref)

        p = p_ref[...]
        dqv = _load_heads(dq_ref, N_HEADS) * scale
        dkv = _load_heads(dk_ref, kv_heads)
        if rope is not None:
            cos, sin = cos_ref[...], sin_ref[...]
            dqv, dkv = _rope_t(dqv, cos, sin), _rope_t(dkv, cos[:, :kw], sin[:, :kw])

        def through_norm(xv, dy, g, pm, dg_ref):
            xh, r = _qk_norm(xv, pm)
            dg_ref[...] += jnp.sum(dy * xh, axis=0, keepdims=True)
            dxh = dy * g
            return r * (dxh - xh * _segmean(dxh * xh, pm))

        dz_ref[:, 0:512] = through_norm(q_ref[...], dqv, gq_ref[...], p, dgq_ref).astype(BF16)
        dz_ref[:, 512:512 + kw] = through_norm(k_ref[...], dkv, gk_ref[...], p[:kw, :kw], dgk_ref).astype(BF16)
        dz_ref[:, 512 + kw:512 + 2 * kw] = _load_heads(dv_ref, kv_heads).astype(BF16)

    row = lambda w, c: pl.BlockSpec((tm, w), lambda b, i: (b * ns + i, c))
    const = lambda shape: pl.BlockSpec(shape, lambda b, i: (0,) * len(shape))
    heads = lambda n: pl.BlockSpec((None, n, tm, HEAD_DIM), lambda b, i: (b, 0, i, 0))
    ins = [z, z, dq, dk, dv, gq, gk, _seg_matrix(512)]
    specs = [row(512, qc), row(kw, kc), heads(N_HEADS), heads(kv_heads), heads(kv_heads), const((1, 512)), const((1, kw)), const((512, 512))]
    if rope is not None:
        ins += list(rope)
        specs += [pl.BlockSpec((tm, 512), lambda b, i: (i, 0))] * 2
    return pl.pallas_call(
        body, name=name, grid=(B, ns), in_specs=specs, out_specs=[row(512 + 2 * kw, 0), const((1, 512)), const((1, kw))],
        out_shape=[jax.ShapeDtypeStruct((T, 512 + 2 * kw), BF16), jax.ShapeDtypeStruct((1, 512), F32), jax.ShapeDtypeStruct((1, kw), F32)],
        compiler_params=_params(("arbitrary", "arbitrary")),
    )(*ins)


def _toeplitz(win, tq, S):
    r = pltpu.roll(jnp.broadcast_to(win, (tq, S + tq)), 0, 1, stride=1, stride_axis=0)
    return r[:, tq:tq + S]


ATTN_HEADS = 4


def _attn_fwd(q, k, v, win, name, nh=ATTN_HEADS, tq=ATTN_TQ):
    B, H, S, _ = q.shape
    shared = k.shape[1] != H
    assert not shared or H // k.shape[1] == nh
    tq = min(tq, S)

    def body(*refs):
        if win is None:
            q_ref, k_ref, v_ref, o_ref = refs
        else:
            q_ref, k_ref, v_ref, w_ref, o_ref = refs
        kvs = [(k_ref[...], v_ref[...])] * nh if shared else [(k_ref[h], v_ref[h]) for h in range(nh)]
        scores = []
        for h in range(nh):
            s = _nt(q_ref[h], kvs[h][0])
            if win is not None:
                s = s + _toeplitz(w_ref[h], tq, S)
            scores.append(s)
        probs = []
        for s in scores:
            p = jnp.exp(s - jnp.max(s, axis=-1, keepdims=True))
            probs.append((p.astype(BF16), jnp.sum(p, axis=-1, keepdims=True)))
        for h, (p, l) in enumerate(probs):
            o_ref[h] = _nn(p, kvs[h][1]) / l

    qs = pl.BlockSpec((None, nh, tq, HEAD_DIM), lambda b, h, i: (b, h, i, 0))
    ks = (pl.BlockSpec((None, None, S, HEAD_DIM), lambda b, h, i: (b, h, 0, 0)) if shared
          else pl.BlockSpec((None, nh, S, HEAD_DIM), lambda b, h, i: (b, h, 0, 0)))
    ins, specs = [q, k, v], [qs, ks, ks]
    if win is not None:
        ins.append(win)
        specs.append(pl.BlockSpec((nh, None, 1, S + tq), lambda b, h, i: (h, i, 0, 0)))
    return pl.pallas_call(body, name=name, grid=(B, H // nh, S // tq), in_specs=specs, out_specs=qs,
                          out_shape=jax.ShapeDtypeStruct((B, H, S, HEAD_DIM), F32),
                          compiler_params=_params(("parallel", "parallel", "parallel")))(*ins)


def _attn_bwd(q, k, v, o, do, win, name, nh=ATTN_HEADS, tq=ATTN_TQ):
    B, H, S, _ = q.shape
    hkv = k.shape[1]
    shared = hkv != H
    assert not shared or H // hkv == nh
    tq = min(tq, S)
    nq = S // tq

    def body(*refs):
        if win is None:
            q_ref, k_ref, v_ref, o_ref, do_ref, dq_ref, dk_ref, dv_ref = refs
        else:
            q_ref, k_ref, v_ref, o_ref, do_ref, w_ref, rev_ref, dq_ref, dk_ref, dv_ref, dw_ref = refs

        @pl.when(pl.program_id(2) == 0)
        def _():
            dk_ref[...] = jnp.zeros_like(dk_ref)
            dv_ref[...] = jnp.zeros_like(dv_ref)

        kvs = [(k_ref[...], v_ref[...])] * nh if shared else [(k_ref[h], v_ref[h]) for h in range(nh)]
        qvs, dobs, scores, dps = [], [], [], []
        for h in range(nh):
            qv, dov = q_ref[h], do_ref[h]
            dob = dov.astype(BF16)
            s = _nt(qv, kvs[h][0])
            if win is not None:
                s = s + _toeplitz(w_ref[h], tq, S)
            dp = _nt(dob, kvs[h][1]) - jnp.sum(dov * o_ref[h], axis=-1, keepdims=True)
            qvs.append(qv)
            dobs.append(dob)
            scores.append(s)
            dps.append(dp)
        pbs, dsbs = [], []
        for s, dp in zip(scores, dps):
            p = jnp.exp(s - jnp.max(s, axis=-1, keepdims=True))
            p = p * (1.0 / jnp.sum(p, axis=-1, keepdims=True))
            pbs.append(p.astype(BF16))
            dsbs.append((p * dp).astype(BF16))
        dk_acc = dv_acc = None
        for h in range(nh):
            dvh, dkh = _tn(pbs[h], dobs[h]), _tn(dsbs[h], qvs[h])
            dq_ref[h] = _nn(dsbs[h], kvs[h][0])
            if shared:
                dv_acc = dvh if dv_acc is None else dv_acc + dvh
                dk_acc = dkh if dk_acc is None else dk_acc + dkh
            else:
                dv_ref[h] += dvh
                dk_ref[h] += dkh
            if win is not None:
                rev = _nn(rev_ref[...], dsbs[h])
                wide = jnp.concatenate([rev, jnp.zeros((tq, tq), F32)], axis=1)
                dw_ref[h] = jnp.sum(pltpu.roll(wide, 0, 1, stride=1, stride_axis=0), axis=0, keepdims=True)
        if shared:
            dv_ref[...] += dv_acc
            dk_ref[...] += dk_acc

    qs = pl.BlockSpec((None, nh, tq, HEAD_DIM), lambda b, h, i: (b, h, i, 0))
    ks = (pl.BlockSpec((None, None, S, HEAD_DIM), lambda b, h, i: (b, h, 0, 0)) if shared
          else pl.BlockSpec((None, nh, S, HEAD_DIM), lambda b, h, i: (b, h, 0, 0)))
    ins, specs = [q, k, v, o, do], [qs, ks, ks, qs, qs]
    outs = [jax.ShapeDtypeStruct((B, H, S, HEAD_DIM), F32), jax.ShapeDtypeStruct((B, hkv, S, HEAD_DIM), F32), jax.ShapeDtypeStruct((B, hkv, S, HEAD_DIM), F32)]
    ospecs = [qs, ks, ks]
    if win is not None:
        ins += [win, jnp.asarray(np.eye(tq, dtype=np.float32)[::-1].copy(), BF16)]
        specs += [pl.BlockSpec((nh, None, 1, S + tq), lambda b, h, i: (h, i, 0, 0)), pl.BlockSpec((tq, tq), lambda b, h, i: (0, 0))]
        outs.append(jax.ShapeDtypeStruct((B, H, nq, 1, S + tq), F32))
        ospecs.append(pl.BlockSpec((None, nh, None, 1, S + tq), lambda b, h, i: (b, h, i, 0, 0)))
    return pl.pallas_call(body, name=name, grid=(B, H // nh, nq), in_specs=specs, out_specs=ospecs, out_shape=outs,
                          compiler_params=_params(("parallel", "parallel", "arbitrary")))(*ins)


def _pattern_count(delta):
    n = jnp.zeros(delta.shape, jnp.int32)
    for window, dil in DIL_PATTERNS:
        n = n + ((delta % dil == 0) & (jnp.abs(delta) <= window // 2)).astype(jnp.int32)
    return n


def _t5_bucket(rel):
    nb = REL_BUCKETS // 2
    max_exact = nb // 2
    ret = jnp.where(rel > 0, nb, 0)
    n = jnp.abs(rel)
    nf = jnp.maximum(n, 1).astype(F32)
    large = max_exact + (jnp.log(nf / max_exact) / math.log(REL_MAX_DIST / max_exact) * (nb - max_exact)).astype(jnp.int32)
    large = jnp.minimum(large, nb - 1)
    return ret + jnp.where(n < max_exact, n, large)


def _bias_windows(rel_bias, S):
    tq = min(ATTN_TQ, S)
    nq = S // tq
    n = nq * (S + tq)
    delta = (jnp.arange(S + tq)[None, :] - (jnp.arange(nq)[:, None] + 1) * tq).reshape(n)
    count = _pattern_count(delta)
    onehot = (_t5_bucket(delta)[None, :] == jnp.arange(REL_BUCKETS)[:, None]).astype(F32)
    extra = jnp.where(count > 0, jnp.log(jnp.maximum(count, 1).astype(F32)), MASKED).reshape(1, n)
    live = (count > 0).astype(F32).reshape(1, n)

    def body(t_ref, oh_ref, live_ref, extra_ref, o_ref):
        o_ref[...] = _nn(t_ref[...], oh_ref[...], HIGHEST) * live_ref[...] + extra_ref[...]

    val = pl.pallas_call(body, name="bias_windows", out_shape=jax.ShapeDtypeStruct((N_HEADS, n), F32),
                         compiler_params=_params())(rel_bias.T, onehot, live, extra)
    return val.reshape(N_HEADS, nq, 1, S + tq)


def _bias_fold(dwin, S, name):
    B, H, nq = dwin.shape[:3]
    tq = min(ATTN_TQ, S)
    n = nq * (S + tq)
    delta = (jnp.arange(S + tq)[None, :] - (tq - 1) - jnp.arange(nq)[:, None] * tq).reshape(n)
    onehot = (_t5_bucket(delta)[:, None] == jnp.arange(128)[None, :]).astype(F32)

    def body(d_ref, oh_ref, o_ref):
        tot = d_ref[0]
        for b in range(1, B):
            tot = tot + d_ref[b]
        o_ref[...] = _nn(tot, oh_ref[...], HIGHEST)

    out = pl.pallas_call(body, name=name, out_shape=jax.ShapeDtypeStruct((H, 128), F32), compiler_params=_params())(dwin.reshape(B, H, n), onehot)
    return out[:, :REL_BUCKETS].T


def _rope_tables(S):
    half = 16
    freqs = ROPE_THETA ** (-jnp.arange(half, dtype=F32) / half)
    t = jnp.arange(S)
    ang_r = (t // GRID_W).astype(F32)[:, None] * freqs[None, :]
    ang_c = (t % GRID_W).astype(F32)[:, None] * freqs[None, :]
    cos = jnp.concatenate([jnp.cos(ang_r)] * 2 + [jnp.cos(ang_c)] * 2, axis=1)
    sin = jnp.concatenate([-jnp.sin(ang_r), jnp.sin(ang_r), -jnp.sin(ang_c), jnp.sin(ang_c)], axis=1)
    return jnp.tile(cos, (1, N_HEADS)), jnp.tile(sin, (1, N_HEADS))


def _conv_act(c, g, b):
    n, rs = _ln_rows(c)
    t = n * g + b
    return t * _sigmoid(t), n, rs, t


def _mix_fwd(ya, ob, c, od, gain, lng, lnb, B, S, name):
    T = B * S
    tm = min(ROW_TILE, S)
    ns = S // tm

    def body(ya_ref, ob_ref, c_ref, od_ref, g_ref, lg_ref, lb_ref, o_ref):
        yc = _conv_act(c_ref[...], lg_ref[...], lb_ref[...])[0]
        ys = [ya_ref[...], _load_heads(ob_ref, N_HEADS), yc, _load_heads(od_ref, N_HEADS)]
        for m, y in enumerate(ys):
            r = lax.rsqrt(jnp.mean(y * y, axis=-1, keepdims=True) + RMS_EPS)
            o_ref[:, 512 * m:512 * (m + 1)] = (y * r * g_ref[:, 512 * m:512 * (m + 1)]).astype(BF16)

    row = pl.BlockSpec((tm, 512), lambda b, i: (b * ns + i, 0))
    heads = pl.BlockSpec((None, N_HEADS, tm, HEAD_DIM), lambda b, i: (b, 0, i, 0))
    vec = pl.BlockSpec((1, 512), lambda b, i: (0, 0))
    return pl.pallas_call(
        body, name=name, grid=(B, ns), in_specs=[row, heads, row, heads, pl.BlockSpec((1, 2048), lambda b, i: (0, 0)), vec, vec],
        out_specs=pl.BlockSpec((tm, 2048), lambda b, i: (b * ns + i, 0)), out_shape=jax.ShapeDtypeStruct((T, 2048), BF16),
        compiler_params=_params(("parallel", "parallel")),
    )(ya, ob, c, od, gain, lng, lnb)


def _mix_bwd(ya, ob, c, od, dycat, gain, lng, lnb, B, S, name):
    T = B * S
    tm = min(ROW_TILE, S)
    ns = S // tm

    def body(ya_ref, ob_ref, c_ref, od_ref, dy_ref, g_ref, lg_ref, lb_ref, dya_ref, dob_ref, dc_ref, dod_ref, dg_ref, dlg_ref, dlb_ref, dcb_ref):
        @pl.when((pl.program_id(0) == 0) & (pl.program_id(1) == 0))
        def _():
            for ref in (dg_ref, dlg_ref, dlb_ref, dcb_ref):
                ref[...] = jnp.zeros_like(ref)

        yc, n, rs, t = _conv_act(c_ref[...], lg_ref[...], lb_ref[...])
        ys = [ya_ref[...], _load_heads(ob_ref, N_HEADS), yc, _load_heads(od_ref, N_HEADS)]
        outs = [dya_ref, dob_ref, None, dod_ref]
        for m, y in enumerate(ys):
            cols = slice(512 * m, 512 * (m + 1))
            r = lax.rsqrt(jnp.mean(y * y, axis=-1, keepdims=True) + RMS_EPS)
            yh = y * r
            dh = dy_ref[:, cols]
            dg_ref[:, cols] += jnp.sum(dh * yh, axis=0, keepdims=True)
            dyh = dh * g_ref[:, cols]
            dyv = r * (dyh - yh * jnp.mean(dyh * yh, axis=-1, keepdims=True))
            if m == 0:
                outs[m][...] = dyv
            elif m == 2:
                sg = _sigmoid(t)
                dt = dyv * sg * (1.0 + t * (1.0 - sg))
                dlg_ref[...] += jnp.sum(dt * n, axis=0, keepdims=True)
                dlb_ref[...] += jnp.sum(dt, axis=0, keepdims=True)
                dn = dt * lg_ref[...]
                dc = rs * (dn - jnp.mean(dn, axis=-1, keepdims=True) - n * jnp.mean(dn * n, axis=-1, keepdims=True))
                dc_ref[...] = dc
                dcb_ref[...] += jnp.sum(dc, axis=0, keepdims=True)
            else:
                _store_heads(outs[m], dyv, N_HEADS)

    row = pl.BlockSpec((tm, 512), lambda b, i: (b * ns + i, 0))
    heads = pl.BlockSpec((None, N_HEADS, tm, HEAD_DIM), lambda b, i: (b, 0, i, 0))
    vec = pl.BlockSpec((1, 2048), lambda b, i: (0, 0))
    flat = jax.ShapeDtypeStruct((T, 512), F32)
    hm = jax.ShapeDtypeStruct((B, N_HEADS, S, HEAD_DIM), F32)
    v512 = pl.BlockSpec((1, 512), lambda b, i: (0, 0))
    s512 = jax.ShapeDtypeStruct((1, 512), F32)
    return pl.pallas_call(
        body, name=name, grid=(B, ns), in_specs=[row, heads, row, heads, pl.BlockSpec((tm, 2048), lambda b, i: (b * ns + i, 0)), vec, v512, v512],
        out_specs=[row, heads, row, heads, vec, v512, v512, v512],
        out_shape=[flat, hm, flat, hm, jax.ShapeDtypeStruct((1, 2048), F32), s512, s512, s512],
        compiler_params=_params(("arbitrary", "arbitrary")),
    )(ya, ob, c, od, dycat, gain, lng, lnb)


def _ffn_down(gate, up, w_down, res, name, norm=None, target=None):
    J, T, n = gate.shape
    N = w_down.shape[2]
    tm = min(256, T)
    steps = T // tm

    def body(g_ref, u_ref, w_ref, r_ref, *rest):
        x_ref = rest[0] if (norm is not None or target is not None) else None
        o_ref, act_ref = rest[-3:-1] if x_ref is not None else rest[-2:]
        acc = None
        for j in range(J):
            g = g_ref[j].astype(F32)
            a = (g * _sigmoid(g) * u_ref[j].astype(F32)).astype(BF16)
            act_ref[j] = a
            d = _nn(a, w_ref[j])
            acc = d if acc is None else acc + d
        y = acc + r_ref[...]
        if target is None:
            o_ref[...] = y
            if norm is not None:
                rest[-1][...] = _rms_rows(y, x_ref[...])
            return
        err = y - x_ref[...]
        o_ref[...] = err * (1.0 / N)
        loss_ref, i = rest[-1], pl.program_id(0)

        @pl.when(i == 0)
        def _():
            loss_ref[...] = jnp.zeros_like(loss_ref)

        loss_ref[...] += jnp.sum(err * err)

        @pl.when(i == steps - 1)
        def _():
            loss_ref[...] = loss_ref[...] * (0.5 / N)

    gu = pl.BlockSpec((J, tm, n), lambda i: (0, i, 0))
    row = pl.BlockSpec((tm, N), lambda i: (i, 0))
    ins, specs = [gate, up, w_down, res], [gu, gu, pl.BlockSpec((J, n, N), lambda i: (0, 0, 0)), row]
    outs, ospecs = [jax.ShapeDtypeStruct((T, N), F32), jax.ShapeDtypeStruct((J, T, n), BF16)], [row, gu]
    if target is not None:
        ins, specs = ins + [target], specs + [row]
        outs, ospecs = outs + [jax.ShapeDtypeStruct((8, 128), F32)], ospecs + [pl.BlockSpec((8, 128), lambda i: (0, 0))]
    elif norm is not None:
        ins, specs = ins + [norm.reshape(1, N)], specs + [pl.BlockSpec((1, N), lambda i: (0, 0))]
        outs, ospecs = outs + [jax.ShapeDtypeStruct((T, N), BF16)], ospecs + [row]
    return pl.pallas_call(body, name=name, grid=(steps,), in_specs=specs, out_specs=ospecs, out_shape=outs,
                          compiler_params=_params(("arbitrary" if target is not None else "parallel",)))(*ins)


def _ffn_down_dx(dx, w_down, gate, up, name):
    J, n, D = w_down.shape
    T = dx.shape[0]
    tm = min(512, T)

    def body(dx_ref, w_ref, g_ref, u_ref, dg_ref, du_ref):
        d = _nt(dx_ref[...].astype(BF16), w_ref[...])
        g = g_ref[...].astype(F32)
        s = _sigmoid(g)
        dg_ref[...] = (d * u_ref[...].astype(F32) * s * (1.0 + g * (1.0 - s))).astype(BF16)
        du_ref[...] = (d * g * s).astype(BF16)

    blk = pl.BlockSpec((None, tm, n), lambda j, i: (j, i, 0))
    shape = jax.ShapeDtypeStruct((J, T, n), BF16)
    return pl.pallas_call(body, name=name, grid=(J, T // tm),
                          in_specs=[pl.BlockSpec((tm, D), lambda j, i: (i, 0)), pl.BlockSpec((None, n, D), lambda j, i: (j, 0, 0)), blk, blk],
                          out_specs=[blk, blk], out_shape=[shape, shape], compiler_params=_params(("parallel", "parallel")))(dx, w_down, gate, up)


def _row_tile(R):
    best = R
    for cand in range(16, min(R, 272) + 1, 16):
        if R % cand == 0:
            best = cand
    return best


def _adamw(w, m, v, stack, name, layer=None, prev=None):
    n, R, C = stack.shape
    tm = _row_tile(R)
    nb = R // tm
    off = 0 if layer is None else layer * nb
    c1 = 1.0 - ADAM_B1 ** ADAM_STEP
    c2 = 1.0 - ADAM_B2 ** ADAM_STEP

    def body(w_ref, m_ref, v_ref, s_ref, *rest):
        g_ref, d_ref, mo_ref, vo_ref = rest[-4:]
        g = s_ref[0].astype(F32)
        for k in range(1, n):
            g = g + s_ref[k].astype(F32)
        mn = ADAM_B1 * m_ref[...] + (1.0 - ADAM_B1) * g
        vn = ADAM_B2 * v_ref[...] + (1.0 - ADAM_B2) * (g * g)
        g_ref[...] = g
        mo_ref[...] = mn
        vo_ref[...] = vn
        d_ref[...] = -ADAM_LR * ((mn / c1) / (jnp.sqrt(vn / c2) + ADAM_EPS) + ADAM_WD * w_ref[...])

    blk = pl.BlockSpec((tm, C), lambda i: (i + off, 0))
    ins = [w, m, v, stack]
    specs = [blk, blk, blk, pl.BlockSpec((n, tm, C), lambda i: (0, i, 0))]
    aliases = {}
    if prev is not None:
        ins += list(prev)
        specs += [pl.BlockSpec(memory_space=pl.ANY)] * 4
        aliases = {4 + t: t for t in range(4)}
    shape = jax.ShapeDtypeStruct(w.shape, F32)
    return pl.pallas_call(body, name=name, grid=(nb,), in_specs=specs, out_specs=[blk] * 4, out_shape=[shape] * 4,
                          input_output_aliases=aliases, compiler_params=_params(("parallel",)))(*ins)


HBM = pl.BlockSpec(memory_space=pltpu.HBM)
SEM = pl.BlockSpec(memory_space=pltpu.SEMAPHORE)
EFFECT = pltpu.SideEffectType.DATAFLOW_SIDE_EFFECTING


PEERS = {"scatter": (1, 2, 3, 4, 5, 6, 7), "gather": (1, 2, 3, 4, 5, 6, 7), "chips": (1, 2, 4, 6), "forward": (2, 4, 6)}


def _spread_copies(srcs, lands, send_sems, recv_sems, local_sems, kind, waiting):
    x, y, c = lax.axis_index("x"), lax.axis_index("y"), lax.axis_index("c")
    me = 4 * x + 2 * y + c

    def peer(bits):
        dev = (1 - x if bits & 4 else x, 1 - y if bits & 2 else y, 1 - c if bits & 1 else c)
        return dev, 4 * dev[0] + 2 * dev[1] + dev[2]

    plan = PEERS[kind]
    remote, local = [], []
    for a, l in enumerate(lands):
        for d, bits in enumerate(plan):
            dev, pid = peer(bits)
            if kind == "forward":
                src, dst, dev = l.at[pid], l.at[peer(bits | 1)[1] if waiting else pid], peer(1)[0]
            else:
                src, dst = (srcs[a].at[pid] if kind == "scatter" else srcs[a]), l.at[pid if waiting else me]
            remote.append(pltpu.make_async_remote_copy(
                src_ref=src, dst_ref=dst, send_sem=send_sems.at[a * len(plan) + d], recv_sem=recv_sems.at[a * len(plan) + d],
                device_id=dev, device_id_type=MESH_ID))
        if kind != "forward":
            local.append(pltpu.make_async_copy(srcs[a].at[me] if kind == "scatter" else srcs[a], l.at[me], local_sems.at[a]))
    return remote, local


def _spread_start(srcs, kind, name, after=None, lands=None):
    if kind == "forward":
        srcs = []
    else:
        shapes = [a.shape if kind == "scatter" else (N_DEV,) + a.shape for a in srcs]
        lands = [lax.empty(shp, a.dtype) for shp, a in zip(shapes, srcs)]
    ns, nl, per = len(srcs), len(lands), len(PEERS[kind])
    extra = [] if after is None else [after]
    sem_shapes = [pltpu.SemaphoreType.DMA((nl * per,))] * 2 + ([pltpu.SemaphoreType.DMA((nl,))] if ns else [])

    def body(*refs):
        src_refs, land_refs = refs[:ns], refs[ns:ns + nl]
        sems = refs[ns + nl + len(extra):ns + nl + len(extra) + len(sem_shapes)]
        remote, local = _spread_copies(src_refs, land_refs, sems[0], sems[1], sems[2] if ns else None, kind, False)
        for cp in remote + local:
            cp.start()
        refs[-1][...] = jnp.zeros((8, 128), F32)

    outs = pl.pallas_call(
        body, name=name,
        out_shape=(*sem_shapes, *[pltpu.HBM(a.shape, a.dtype) for a in srcs + lands], jax.ShapeDtypeStruct((8, 128), F32)),
        in_specs=[HBM] * (ns + nl) + [pl.BlockSpec(memory_space=pl.ANY)] * len(extra),
        out_specs=(*[SEM] * len(sem_shapes), *[HBM] * (ns + nl), pl.BlockSpec(memory_space=pltpu.VMEM)),
        input_output_aliases={i: len(sem_shapes) + i for i in range(ns + nl)},
        compiler_params=pltpu.CompilerParams(has_side_effects=EFFECT),
    )(*[pltpu.with_memory_space_constraint(a, pltpu.HBM) for a in srcs + lands], *extra)
    k = len(sem_shapes)
    return outs[:k], list(outs[k:k + ns]), list(outs[k + ns:k + ns + nl]), outs[-1]


def _spread_wait(sems, srcs, lands, after, kind, name):
    ns, nl = len(srcs), len(lands)
    after = list(after) if isinstance(after, (list, tuple)) else [after]

    def body(*refs):
        src_refs, land_refs = refs[:ns], refs[ns:ns + nl]
        s = refs[ns + nl:ns + nl + len(sems)]
        remote, local = _spread_copies(src_refs, land_refs, s[0], s[1], s[2] if ns else None, kind, True)
        for cp in remote:
            cp.wait_send()
            cp.wait_recv()
        for cp in local:
            cp.wait()

    outs = pl.pallas_call(
        body, name=name, out_shape=tuple(pltpu.HBM(a.shape, a.dtype) for a in srcs + lands),
        in_specs=[HBM] * (ns + nl) + [SEM] * len(sems) + [pl.BlockSpec(memory_space=pl.ANY)] * len(after), out_specs=tuple([HBM] * (ns + nl)),
        input_output_aliases={i: i for i in range(ns + nl)}, compiler_params=pltpu.CompilerParams(has_side_effects=EFFECT),
    )(*srcs, *lands, *sems, *after)
    return list(outs[ns:])


SMALL = ("rel_bias", "norm1_g", "sgu_w", "sgu_b", "dil_qn_g", "dil_kn_g", "conv_w", "conv_b", "conv_ln_g", "conv_ln_b",
         "gqa_qn_g", "gqa_kn_g", "mix_norm_g", "norm2_g")
LARGE = ("w_in", "w_out", "w_gate", "w_up", "w_down")
EARLY = tuple(k for k in SMALL if k != "norm1_g")


def _local_step(x, target, p, B, S, fetch, emit, mid, early):
    T = B * S
    rope = _rope_tables(S)
    win = _bias_windows(p["rel_bias"], S)
    tile8 = lambda g: jnp.tile(g.reshape(1, HEAD_DIM), (1, N_HEADS))
    cols_b = (COL_BQ, COL_BK, COL_BV)
    cols_d = (COL_DQ, COL_DK128, COL_DV128)
    saved = []
    for l in range(DEPTH):
        s = {"x": x}
        s["ws"] = p["sgu_w"][l].astype(BF16)
        s["bias"] = jnp.repeat(p["sgu_b"][l].T, HEAD_DIM, axis=1)
        s["h"] = _rms_fwd(x, p["norm1_g"][l], f"rms1_fwd_{l}") if l == 0 else h_next
        s["win"] = fetch(l, "in", s["h"])
        s["cw"] = jnp.pad(s["win"]["conv_w"], ((0, 1), (0, 0))).reshape(32, 1, 512)
        z = s["z"] = _matmul(s["h"], s["win"]["w_in"], "nt", f"in_proj_{l}", tk=D_MODEL)
        s["bias"] = s["bias"] + mid(l, z)
        s["ya"] = _sgu_fwd(z, s["ws"], s["bias"], f"sgu_fwd_{l}")
        s["c"] = _conv_fwd1(z, s["cw"], p["conv_b"][l].reshape(1, 512), B, S, f"conv_fwd_{l}")
        s["ln"] = (p["conv_ln_g"][l].reshape(1, 512), p["conv_ln_b"][l].reshape(1, 512))
        s["gb"] = (tile8(p["dil_qn_g"][l]), tile8(p["dil_kn_g"][l]))
        s["gd"] = (tile8(p["gqa_qn_g"][l]), tile8(p["gqa_kn_g"][l])[:, :KV_WIDTH])
        s["qkv_b"] = _prep_fwd(z, *s["gb"], None, B, S, N_HEADS, cols_b, f"prep_b_fwd_{l}")
        s["qkv_d"] = _prep_fwd(z, *s["gd"], rope, B, S, KV_HEADS, cols_d, f"prep_d_fwd_{l}")
        s["ob"] = _attn_fwd(*s["qkv_b"], win, f"attn_b_fwd_{l}")
        s["od"] = _attn_fwd(*s["qkv_d"], None, f"attn_d_fwd_{l}", tq=GQA_TQ)
        s["gmix"] = p["mix_norm_g"][l].reshape(1, 2048)
        s["ycat"] = _mix_fwd(s["ya"], s["ob"], s["c"], s["od"], s["gmix"], *s["ln"], B, S, f"mix_fwd_{l}")
        s["wout"] = fetch(l, "out", s["ycat"])["w_out"]
        x1, s["h2"] = _matmul(s["ycat"], s["wout"], "nn", f"out_proj_{l}", res=x, tk=D_MODEL, norm=p["norm2_g"][l])
        s["x1"] = x1
        s["ffn"] = fetch(l, "ffn", s["h2"])
        s["gate"] = _mm_shard_out(s["h2"], s["ffn"]["w_gate"], "nt", f"ffn_gate_{l}", out_dtype=BF16, tm=1024)
        s["up"] = _mm_shard_out(s["h2"], s["ffn"]["w_up"], "nt", f"ffn_up_{l}", out_dtype=BF16, tm=1024)
        if l + 1 < DEPTH:
            x, s["act"], h_next = _ffn_down(s["gate"], s["up"], s["ffn"]["w_down"], x1, f"ffn_down_{l}", norm=p["norm1_g"][l + 1])
        else:
            dx, s["act"], loss_blk = _ffn_down(s["gate"], s["up"], s["ffn"]["w_down"], x1, f"ffn_down_{l}", target=target)
        saved.append(s)

    g = {k: [None] * DEPTH for k in SMALL if k != "rel_bias"}
    dwin_total = None
    for l in reversed(range(DEPTH)):
        s = saved[l]
        z, ffn = s["z"], s["ffn"]
        dgate, dup = _ffn_down_dx(dx, ffn["w_down"], s["gate"], s["up"], f"ffn_down_dx_{l}")
        tok = emit(l, "w_down", _mm_shard_m(s["act"], dx, f"ffn_down_dw_{l}", out_dtype=BF16, tm=FFN_BLOCK, tn=512, tk=T))
        tok += emit(l, "w_gate", _mm_shard_m(dgate, s["h2"], f"ffn_gate_dw_{l}", out_dtype=BF16, tm=FFN_BLOCK, tn=512, tk=T))
        tok += emit(l, "w_up", _mm_shard_m(dup, s["h2"], f"ffn_up_dw_{l}", out_dtype=BF16, tm=FFN_BLOCK, tn=512, tk=T))
        dh2 = _mm_shard_k([(dgate, ffn["w_gate"]), (dup, ffn["w_up"])], "nn", f"ffn_up_dx_{l}", tn=512, fold=FFN_GROUPS)
        dycat, dx1, dg2 = _out_proj_dx(dh2, s["x1"], p["norm2_g"][l] + tok, dx, s["wout"], f"out_proj_dx_{l}")
        g["norm2_g"][l] = dg2[0]
        tok = emit(l, "w_out", _matmul(s["ycat"], dx1, "tn", f"out_proj_dw_{l}", out_dtype=BF16, tn=1024, tk=T))
        dya, dob, dc, dod, dgm, dlg, dlb, dcb = _mix_bwd(s["ya"], s["ob"], s["c"], s["od"], dycat, s["gmix"] + tok, *s["ln"], B, S, f"mix_bwd_{l}")
        g["mix_norm_g"][l] = dgm[0]
        dz_a, dws, dbias = _sgu_bwd(z, dya, s["ws"], jnp.swapaxes(s["ws"], 1, 2), s["bias"], f"sgu_bwd_{l}")
        g["sgu_w"][l] = dws
        g["sgu_b"][l] = dbias.reshape(128, 8, HEAD_DIM).sum(-1).T
        g["conv_ln_g"][l], g["conv_ln_b"][l], g["conv_b"][l] = dlg[0], dlb[0], dcb[0]
        dz_ca, dz_cg, dcw = _conv_bwd2(z, dc, s["cw"], B, S, f"conv_bwd_{l}")
        g["conv_w"][l] = dcw.reshape(32, 512)[:CONV_WIDTH]
        dq, dk, dv, dwin = _attn_bwd(*s["qkv_b"], s["ob"], dob, win, f"attn_b_bwd_{l}")
        dwin_total = dwin if dwin_total is None else dwin_total + dwin
        dz_b, dgq, dgk = _prep_bwd(z, dq, dk, dv, *s["gb"], None, B, S, N_HEADS, cols_b, f"prep_b_bwd_{l}")
        g["dil_qn_g"][l] = dgq.reshape(N_HEADS, HEAD_DIM).sum(0)
        g["dil_kn_g"][l] = dgk.reshape(N_HEADS, HEAD_DIM).sum(0)
        dq, dk, dv = _attn_bwd(*s["qkv_d"], s["od"], dod, None, f"attn_d_bwd_{l}", tq=GQA_TQ)
        dz_d, dgq, dgk = _prep_bwd(z, dq, dk, dv, *s["gd"], rope, B, S, KV_HEADS, cols_d, f"prep_d_bwd_{l}")
        g["gqa_qn_g"][l] = dgq.reshape(N_HEADS, HEAD_DIM).sum(0)
        g["gqa_kn_g"][l] = dgk.reshape(KV_HEADS, HEAD_DIM).sum(0)
        dz = [dz_a, dz_b, dz_ca, dz_cg, dz_d]
        tok = jnp.zeros((), F32)
        if l == 0:
            done = {k: jnp.stack(v) for k, v in g.items() if k != "norm1_g"}
            done["rel_bias"] = _bias_fold(dwin_total, S, "bias_fold")
            tok = early(done)
        tok += emit(l, "w_in", _in_proj_dw(dz, s["h"], f"in_proj_dw_{l}"))
        dx, dg1 = _in_proj_dx(dz, s["win"]["w_in"], s["x"], p["norm1_g"][l] + tok, dx1, f"in_proj_dx_{l}")
        g["norm1_g"][l] = dg1[0]

    return loss_blk[0, 0], dx, jnp.stack(g["norm1_g"])


GROUPS = {"in": ("w_in",), "out": ("w_out",), "ffn": ("w_gate", "w_up", "w_down")}
COL_SHARDED = ("w_in", "w_gate", "w_up")


def kernel(x, rel_bias, norm1_g, w_in, sgu_w, sgu_b, dil_qn_g, dil_kn_g, conv_w, conv_b, conv_ln_g, conv_ln_b, gqa_qn_g, gqa_kn_g, mix_norm_g, w_out, norm2_g, w_gate, w_up, w_down, loss_target, m_rel_bias, m_norm1_g, m_w_in, m_sgu_w, m_sgu_b, m_dil_qn_g, m_dil_kn_g, m_conv_w, m_conv_b, m_conv_ln_g, m_conv_ln_b, m_gqa_qn_g, m_gqa_kn_g, m_mix_norm_g, m_w_out, m_norm2_g, m_w_gate, m_w_up, m_w_down, v_rel_bias, v_norm1_g, v_w_in, v_sgu_w, v_sgu_b, v_dil_qn_g, v_dil_kn_g, v_conv_w, v_conv_b, v_conv_ln_g, v_conv_ln_b, v_gqa_qn_g, v_gqa_kn_g, v_mix_norm_g, v_w_out, v_norm2_g, v_w_gate, v_w_up, v_w_down):
    w = dict(rel_bias=rel_bias, norm1_g=norm1_g, w_in=w_in, sgu_w=sgu_w, sgu_b=sgu_b, dil_qn_g=dil_qn_g, dil_kn_g=dil_kn_g, conv_w=conv_w,
             conv_b=conv_b, conv_ln_g=conv_ln_g, conv_ln_b=conv_ln_b, gqa_qn_g=gqa_qn_g, gqa_kn_g=gqa_kn_g, mix_norm_g=mix_norm_g,
             w_out=w_out, norm2_g=norm2_g, w_gate=w_gate, w_up=w_up, w_down=w_down)
    m = dict(rel_bias=m_rel_bias, norm1_g=m_norm1_g, w_in=m_w_in, sgu_w=m_sgu_w, sgu_b=m_sgu_b, dil_qn_g=m_dil_qn_g, dil_kn_g=m_dil_kn_g,
             conv_w=m_conv_w, conv_b=m_conv_b, conv_ln_g=m_conv_ln_g, conv_ln_b=m_conv_ln_b, gqa_qn_g=m_gqa_qn_g, gqa_kn_g=m_gqa_kn_g,
             mix_norm_g=m_mix_norm_g, w_out=m_w_out, norm2_g=m_norm2_g, w_gate=m_w_gate, w_up=m_w_up, w_down=m_w_down)
    v = dict(rel_bias=v_rel_bias, norm1_g=v_norm1_g, w_in=v_w_in, sgu_w=v_sgu_w, sgu_b=v_sgu_b, dil_qn_g=v_dil_qn_g, dil_kn_g=v_dil_kn_g,
             conv_w=v_conv_w, conv_b=v_conv_b, conv_ln_g=v_conv_ln_g, conv_ln_b=v_conv_ln_b, gqa_qn_g=v_gqa_qn_g, gqa_kn_g=v_gqa_kn_g,
             mix_norm_g=v_mix_norm_g, w_out=v_w_out, norm2_g=v_norm2_g, w_gate=v_w_gate, w_up=v_w_up, w_down=v_w_down)
    names = list(w)
    B, S, D = x.shape
    T = B * S
    me = 4 * lax.axis_index("x") + 2 * lax.axis_index("y") + lax.axis_index("c")

    view = lambda a, k: jnp.swapaxes(a, 1, 2) if k in COL_SHARDED else a
    bf = {k: view(w[k], k).astype(BF16) for k in LARGE}
    spreads, forwards = {}, {}

    def start_gather(l, group, after=None):
        srcs = [bf[k][l] for k in GROUPS[group]] + ([conv_w[l]] if group == "in" else [])
        spreads[l, group] = _spread_start(srcs, "chips", f"gather_{group}_{l}_start", after)
        return spreads[l, group][3][0, 0]

    def forward(l, group, after):
        sems, srcs, lands, _ = spreads[l, group]
        lands = _spread_wait(sems, srcs, lands, after, "chips", f"gather_{group}_{l}_wait")
        forwards[l, group] = _spread_start(None, "forward", f"forward_{group}_{l}_start", lands=lands)
        return forwards[l, group][3]

    def landed(l, group, after):
        sems, _, lands, _ = forwards[l, group]
        return _spread_wait(sems, [], lands, after, "forward", f"forward_{group}_{l}_wait")

    tok0 = start_gather(0, "in") + start_gather(0, "out") + start_gather(0, "ffn")
    small = {k: w[k] for k in SMALL}
    small["norm1_g"] = norm1_g.at[0].add(tok0)

    def mid(l, z):
        if l > 0:
            return jnp.zeros((), F32)
        return start_gather(1, "in", z) + start_gather(1, "out", z) + start_gather(1, "ffn", z)

    def fetch(l, group, after):
        if group == "in":
            tok = forward(0, "in", after) if l == 0 else after
        elif group == "out":
            tok = forward(l, "ffn", [after, forward(l, "out", after)])
        else:
            tok = forward(1, "in", after) if l == 0 else after
        got = dict(zip(GROUPS[group] + ("conv_w",), landed(l, group, [after, tok])))
        if group == "in":
            got["w_in"] = got["w_in"].reshape(IN_WIDTH, D)
            got["conv_w"] = jnp.transpose(got["conv_w"], (1, 0, 2)).reshape(CONV_WIDTH, 512)
        if group == "out":
            got["w_out"] = got["w_out"].reshape(D, D)
        if group == "ffn":
            got = {k: a.reshape(FFN_GROUPS, -1, D) for k, a in got.items()}
        return got

    scatters = {}

    def emit(l, k, dw):
        dw = dw.reshape(N_DEV, -1, D)
        scatters[l, k] = _spread_start([dw], "scatter", f"scatter_{k}_{l}_start")
        return scatters[l, k][3][0, 0]

    flat2 = lambda a: a.reshape(-1, a.shape[-1])
    small_spread = []

    def early(done):
        small_spread.append(_spread_start([flat2(done[k]) for k in EARLY], "gather", "gather_small_grads_start"))
        return small_spread[0][3][0, 0]

    loss_part, dx, dnorm1 = _local_step(x.reshape(T, D), loss_target.reshape(T, D), small, B, S, fetch, emit, mid, early)
    loss = lax.psum(loss_part, ("x", "y", "c"))

    out_g, out_d, out_m, out_v = {}, {}, {}, {}

    def update_large(k, after):
        shp = view(w[k], k).shape
        two_d = lambda a: view(a, k).reshape(-1, shp[-1])
        res = None
        for l in reversed(range(DEPTH)):
            sems, srcs, lands, _ = scatters[l, k]
            stack = _spread_wait(sems, srcs, lands, after, "scatter", f"scatter_{k}_{l}_wait")[0]
            res = _adamw(two_d(w[k]), two_d(m[k]), two_d(v[k]), stack.reshape(N_DEV, -1, shp[-1]), f"adamw_{k}_{l}", layer=l, prev=res)
        out_g[k], out_d[k], out_m[k], out_v[k] = [view(a.reshape(shp), k) for a in res]
        return res[0]

    late_sems, late_srcs, late_lands, late_tok = _spread_start([flat2(dnorm1)], "gather", "gather_norm1_grad_start")
    after = [dx, late_tok]
    for k in ("w_down", "w_gate", "w_up", "w_out"):
        after = update_large(k, after)
    sems, srcs, lands, _ = small_spread[0]
    stacks = dict(zip(EARLY, _spread_wait(sems, srcs, lands, after, "gather", "gather_small_grads_wait")))
    stacks["norm1_g"] = _spread_wait(late_sems, late_srcs, late_lands, after, "gather", "gather_norm1_grad_wait")[0]
    for k in SMALL:
        stack = stacks[k]
        if k == "conv_w":
            stack = lax.dynamic_slice_in_dim(stack, me * (512 // N_DEV), 512 // N_DEV, axis=2)
        res = _adamw(flat2(w[k]), flat2(m[k]), flat2(v[k]), stack, f"adamw_{k}")
        out_g[k], out_d[k], out_m[k], out_v[k] = [a.reshape(w[k].shape) for a in res]
        after = res[0]
    update_large("w_in", after)

    return (loss, dx.reshape(B, S, D), *[out_g[k] for k in names], *[out_d[k] for k in names],
            *[out_m[k] for k in names], *[out_v[k] for k in names])
```

```python
import functools
import math

import numpy as np
import jax
import jax.numpy as jnp
from jax import lax
from jax.experimental import pallas as pl
from jax.experimental.pallas import tpu as pltpu

F32 = jnp.float32
BF16 = jnp.bfloat16
HIGHEST = lax.Precision.HIGHEST
MESH_ID = pl.DeviceIdType.MESH

D_MODEL = 2048
DEPTH = 2
HEAD_DIM = 64
GROUP_WIDTH = 512
N_HEADS = 8
KV_HEADS = 2
KV_WIDTH = 128
SGU_CHUNK = 128
CONV_WIDTH = 31
CONV_PAD = 16
GRID_W = 64
ROPE_THETA = 10000.0
REL_BUCKETS = 32
REL_MAX_DIST = 1024
DIL_PATTERNS = ((128, 1), (512, 4), (2048, 16))
FFN_HIDDEN = 5632
IN_WIDTH = 4352
RMS_EPS = 1e-6
LN_EPS = 1e-5
MASKED = -1e30
N_DEV = 8

ADAM_LR = 0.001
ADAM_B1 = 0.9
ADAM_B2 = 0.999
ADAM_EPS = 1e-08
ADAM_WD = 0.01
ADAM_STEP = 10

COL_AU, COL_AV, COL_BQ, COL_BK, COL_BV, COL_CA, COL_CG, COL_DQ = range(8)
COL_DK128, COL_DV128 = 32, 33

VMEM_LIMIT = 56 * 1024 * 1024
FFN_GROUPS = 2
FFN_BLOCK = 1408
ATTN_TQ = 256
GQA_TQ = 512
ROW_TILE = 512


def _params(sem=None, vmem=VMEM_LIMIT):
    return pltpu.CompilerParams(dimension_semantics=sem, vmem_limit_bytes=vmem)


def _dot(a, b, dims, precision=None):
    return lax.dot_general(a, b, (dims, ((), ())), precision=precision, preferred_element_type=F32)


def _nn(a, b, precision=None):
    return _dot(a, b, ((1,), (0,)), precision)


def _nt(a, b):
    return _dot(a, b, ((1,), (1,)))


def _tn(a, b):
    return _dot(a, b, ((0,), (0,)))


DIMS = {"nn": ((1,), (0,)), "nt": ((1,), (1,)), "tn": ((0,), (0,))}


def _pick(n, cands):
    for c in cands:
        if n % c == 0:
            return c
    return n


def _mm_call(name, mode, pairs, specs, o_spec, out_sds, grid, acc_shape, res=None, fold=None, norm=None):
    npair, nk, dims = len(pairs), grid[2], DIMS[mode]

    def body(*refs):
        ab = refs[:2 * npair]
        at = 2 * npair
        r_ref = refs[at] if res is not None else None
        at += res is not None
        g_ref = refs[at] if norm is not None else None
        at += norm is not None
        o_ref = refs[at]
        h_ref = refs[at + 1] if norm is not None else None
        part = None
        for t in range(npair):
            for s in ([None] if fold is None else range(fold)):
                a_blk = ab[2 * t][...] if s is None else ab[2 * t][s]
                b_blk = ab[2 * t + 1][...] if s is None else ab[2 * t + 1][s]
                d = _dot(a_blk.astype(BF16), b_blk.astype(BF16), dims)
                part = d if part is None else part + d

        def finish(r):
            if r_ref is not None:
                r = r + r_ref[...]
            o_ref[...] = r.astype(o_ref.dtype)
            if h_ref is not None:
                h_ref[...] = _rms_rows(r, g_ref[...])

        if nk == 1:
            finish(part)
            return
        acc, k = refs[-1], pl.program_id(2)

        @pl.when(k == 0)
        def _():
            acc[...] = part

        @pl.when(k > 0)
        def _():
            acc[...] += part

        @pl.when(k == nk - 1)
        def _():
            finish(acc[...])

    ins = [t for pair in pairs for t in pair]
    in_specs = [t for pair in specs for t in pair]
    if res is not None:
        ins.append(res)
        in_specs.append(o_spec)
    out_specs = o_spec
    if norm is not None:
        ins.append(norm)
        in_specs.append(pl.BlockSpec(norm.shape, lambda *_: (0, 0)))
        out_specs, out_sds = [o_spec, o_spec], [out_sds, jax.ShapeDtypeStruct(out_sds.shape, BF16)]
    return pl.pallas_call(
        body, name=name, grid=grid, in_specs=in_specs, out_specs=out_specs, out_shape=out_sds,
        scratch_shapes=[pltpu.VMEM(acc_shape, F32)] if nk > 1 else [],
        compiler_params=_params(("parallel", "parallel", "arbitrary")),
    )(*ins)


def _rms_rows(x, g):
    return (x * lax.rsqrt(jnp.mean(x * x, axis=-1, keepdims=True) + RMS_EPS) * g).astype(BF16)


def _matmul(a, b, mode, name, res=None, out_dtype=F32, tm=512, tn=None, tk=None, norm=None):
    if mode == "nn":
        (M, K), N = a.shape, b.shape[1]
    elif mode == "nt":
        (M, K), N = a.shape, b.shape[0]
    else:
        (K, M), N = a.shape, b.shape[1]
    tm = min(tm, M)
    tn = tn or _pick(N, (2176, 2048, 1408, 1024, 512))
    tk = tk or _pick(K, (1024, 2176, 1408, 512))
    assert M % tm == 0 and N % tn == 0 and K % tk == 0, (M, N, K, tm, tn, tk)
    a_spec = pl.BlockSpec((tk, tm), lambda i, j, k: (k, i)) if mode == "tn" else pl.BlockSpec((tm, tk), lambda i, j, k: (i, k))
    b_spec = pl.BlockSpec((tn, tk), lambda i, j, k: (j, k)) if mode == "nt" else pl.BlockSpec((tk, tn), lambda i, j, k: (k, j))
    o_spec = pl.BlockSpec((tm, tn), lambda i, j, k: (i, j))
    assert norm is None or tn == N
    return _mm_call(name, mode, [(a, b)], [(a_spec, b_spec)], o_spec, jax.ShapeDtypeStruct((M, N), out_dtype),
                    (M // tm, N // tn, K // tk), (tm, tn), res, norm=None if norm is None else norm.reshape(1, N))


def _mm_shard_out(a, bs, mode, name, out_dtype=F32, tm=512, tk=None):
    J = bs.shape[0]
    n = bs.shape[1] if mode == "nt" else bs.shape[2]
    (K, M) = a.shape if mode == "tn" else a.shape[::-1]
    tm = min(tm, M)
    tk = tk or (K if mode != "tn" else _pick(K, (1024, 512)))
    a_spec = pl.BlockSpec((tk, tm), lambda j, i, k: (k, i)) if mode == "tn" else pl.BlockSpec((tm, tk), lambda j, i, k: (i, k))
    b_spec = pl.BlockSpec((None, n, tk), lambda j, i, k: (j, 0, k)) if mode == "nt" else pl.BlockSpec((None, tk, n), lambda j, i, k: (j, k, 0))
    o_spec = pl.BlockSpec((None, tm, n), lambda j, i, k: (j, i, 0))
    return _mm_call(name, mode, [(a, bs)], [(a_spec, b_spec)], o_spec, jax.ShapeDtypeStruct((J, M, n), out_dtype),
                    (J, M // tm, K // tk), (tm, n))


def _mm_shard_k(pairs, mode, name, res=None, out_dtype=F32, tm=512, tn=None, fold=1):
    J, M, n = pairs[0][0].shape
    N = pairs[0][1].shape[2] if mode == "nn" else pairs[0][1].shape[1]
    tm = min(tm, M)
    tn = tn or _pick(N, (2048, 1024, 512))
    a_spec = pl.BlockSpec((fold, tm, n), lambda i, j, k: (k, i, 0))
    b_spec = pl.BlockSpec((fold, n, tn), lambda i, j, k: (k, 0, j)) if mode == "nn" else pl.BlockSpec((fold, tn, n), lambda i, j, k: (k, j, 0))
    o_spec = pl.BlockSpec((tm, tn), lambda i, j, k: (i, j))
    return _mm_call(name, mode, pairs, [(a_spec, b_spec)] * len(pairs), o_spec, jax.ShapeDtypeStruct((M, N), out_dtype),
                    (M // tm, N // tn, J // fold), (tm, tn), res, fold)


def _mm_shard_m(as_, b, name, out_dtype=F32, tm=None, tn=None, tk=512):
    J, K, n = as_.shape
    N = b.shape[1]
    tm = tm or n
    tn = tn or _pick(N, (2048, 1024, 512))
    tk = min(tk, K)
    nn = N // tn
    a_spec = pl.BlockSpec((None, tk, tm), lambda j, i, k: (j, k, i // nn))
    b_spec = pl.BlockSpec((tk, tn), lambda j, i, k: (k, i % nn))
    o_spec = pl.BlockSpec((None, tm, tn), lambda j, i, k: (j, i // nn, i % nn))
    return _mm_call(name, "tn", [(as_, b)], [(a_spec, b_spec)], o_spec, jax.ShapeDtypeStruct((J, n, N), out_dtype),
                    (J, (n // tm) * nn, K // tk), (tm, tn))


def _out_proj_dx(dh, x, g, dres, w, name):
    T, D = x.shape
    N = w.shape[0]
    tm = min(256, T)

    def body(dh_ref, x_ref, g_ref, dres_ref, w_ref, o_ref, dx_ref, dg_ref):
        @pl.when(pl.program_id(0) == 0)
        def _():
            dg_ref[...] = jnp.zeros_like(dg_ref)

        xv, dhv = x_ref[...], dh_ref[...]
        r = lax.rsqrt(jnp.mean(xv * xv, axis=-1, keepdims=True) + RMS_EPS)
        y = xv * r
        dy = dhv * g_ref[...]
        dx = dres_ref[...] + r * (dy - y * jnp.mean(dy * y, axis=-1, keepdims=True))
        dx_ref[...] = dx
        dg_ref[...] += jnp.sum(dhv * y, axis=0, keepdims=True)
        o_ref[...] = _nt(dx.astype(BF16), w_ref[...])

    row = pl.BlockSpec((tm, D), lambda i: (i, 0))
    vec = pl.BlockSpec((1, D), lambda i: (0, 0))
    return pl.pallas_call(
        body, name=name, grid=(T // tm,), in_specs=[row, row, vec, row, pl.BlockSpec((N, D), lambda i: (0, 0))],
        out_specs=[pl.BlockSpec((tm, N), lambda i: (i, 0)), row, vec],
        out_shape=[jax.ShapeDtypeStruct((T, N), F32), jax.ShapeDtypeStruct((T, D), F32), jax.ShapeDtypeStruct((1, D), F32)],
        compiler_params=_params(("arbitrary",)),
    )(dh, x, g.reshape(1, D), dres, w)


def _in_proj_dw(pieces, h, name):
    T, D = h.shape
    tm = 256
    nbs = [p.shape[1] // tm for p in pieces]
    los = [sum(nbs[:t]) for t in range(len(pieces))]

    def body(*refs):
        h_ref, o_ref = refs[-2:]
        i = pl.program_id(0)
        for p_ref, lo, nb in zip(refs[:-2], los, nbs):
            @pl.when((i >= lo) & (i < lo + nb))
            def _():
                o_ref[...] = _tn(p_ref[...], h_ref[...]).astype(BF16)

    specs = [pl.BlockSpec((T, tm), (lambda lo, nb: lambda i: (0, jnp.clip(i - lo, 0, nb - 1)))(lo, nb)) for lo, nb in zip(los, nbs)]
    return pl.pallas_call(body, name=name, grid=(sum(nbs),), in_specs=specs + [pl.BlockSpec((T, D), lambda i: (0, 0))],
                          out_specs=pl.BlockSpec((tm, D), lambda i: (i, 0)), out_shape=jax.ShapeDtypeStruct((sum(nbs) * tm, D), BF16),
                          compiler_params=_params(("parallel",)))(*pieces, h)


def _in_proj_dx(pieces, w, x, g, dres, name):
    T, D = x.shape
    K = w.shape[0]
    tm = min(256, T)
    n = len(pieces)
    widths = [p.shape[1] for p in pieces]
    offs = [sum(widths[:t]) for t in range(n)]

    def body(*refs):
        w_ref, x_ref, g_ref, dres_ref, dx_ref, dg_ref = refs[n:]

        @pl.when(pl.program_id(0) == 0)
        def _():
            dg_ref[...] = jnp.zeros_like(dg_ref)

        dh = None
        for p_ref, off, wd in zip(refs[:n], offs, widths):
            d = _nn(p_ref[...], w_ref[off:off + wd, :])
            dh = d if dh is None else dh + d
        xv = x_ref[...]
        r = lax.rsqrt(jnp.mean(xv * xv, axis=-1, keepdims=True) + RMS_EPS)
        y = xv * r
        dy = dh * g_ref[...]
        dx_ref[...] = dres_ref[...] + r * (dy - y * jnp.mean(dy * y, axis=-1, keepdims=True))
        dg_ref[...] += jnp.sum(dh * y, axis=0, keepdims=True)

    specs = [pl.BlockSpec((tm, wd), lambda i: (i, 0)) for wd in widths]
    row = pl.BlockSpec((tm, D), lambda i: (i, 0))
    vec = pl.BlockSpec((1, D), lambda i: (0, 0))
    return pl.pallas_call(body, name=name, grid=(T // tm,), in_specs=specs + [pl.BlockSpec((K, D), lambda i: (0, 0)), row, vec, row],
                          out_specs=[row, vec], out_shape=[jax.ShapeDtypeStruct((T, D), F32), jax.ShapeDtypeStruct((1, D), F32)],
                          compiler_params=_params(("arbitrary",)))(*pieces, w, x, g.reshape(1, D), dres)


def _seg_matrix(width):
    return jnp.asarray(np.kron(np.eye(width // HEAD_DIM, dtype=np.float32), np.full((HEAD_DIM, HEAD_DIM), 1.0 / HEAD_DIM, np.float32)), BF16)


def _segmean(v, p):
    hi = v.astype(BF16)
    r = v - hi.astype(F32)
    mid = r.astype(BF16)
    lo = (r - mid.astype(F32)).astype(BF16)
    w = min(256, v.shape[1])
    pw = p[:w, :w]
    halves = []
    for c in range(v.shape[1] // w):
        cols = slice(c * w, (c + 1) * w)
        halves.append(_nn(hi[:, cols], pw) + _nn(mid[:, cols], pw) + _nn(lo[:, cols], pw))
    return halves[0] if len(halves) == 1 else jnp.concatenate(halves, axis=1)


def _gelu(x):
    c0 = math.sqrt(2.0 / math.pi)
    t = jnp.tanh(c0 * (x + 0.044715 * x * x * x))
    return 0.5 * x * (1.0 + t), t


def _gelu_grad(x, t):
    c0 = math.sqrt(2.0 / math.pi)
    return 0.5 * (1.0 + t) + 0.5 * x * (1.0 - t * t) * c0 * (1.0 + 3.0 * 0.044715 * x * x)


def _sigmoid(x):
    return 1.0 / (1.0 + jnp.exp(-x))


def _rms_fwd(x, g, name):
    T, D = x.shape
    tm = min(256, T)

    def body(x_ref, g_ref, o_ref):
        o_ref[...] = _rms_rows(x_ref[...], g_ref[...])

    return pl.pallas_call(
        body, name=name, grid=(T // tm,),
        in_specs=[pl.BlockSpec((tm, D), lambda i: (i, 0)), pl.BlockSpec((1, D), lambda i: (0, 0))],
        out_specs=pl.BlockSpec((tm, D), lambda i: (i, 0)), out_shape=jax.ShapeDtypeStruct((T, D), BF16),
        compiler_params=_params(("parallel",)),
    )(x, g.reshape(1, D))


def _sgu_core(zu, zv, ws_ref, bias, p):
    ug, tu = _gelu(zu)
    vg, tv = _gelu(zv)
    xc = vg - _segmean(vg, p)
    rs = lax.rsqrt(_segmean(xc * xc, p) + LN_EPS)
    vn = xc * rs
    vnb = vn.astype(BF16)
    low = lax.broadcasted_iota(jnp.int32, (SGU_CHUNK, 128), 1) < HEAD_DIM
    parts = []
    for j in range(4):
        vp = vnb[:, 128 * j:128 * (j + 1)]
        parts.append(jnp.where(low, _nn(ws_ref[2 * j], vp), _nn(ws_ref[2 * j + 1], vp)))
    mixed = jnp.concatenate(parts, axis=1) + bias
    return ug, tu, tv, rs, vn, vnb, mixed, low


SGU_ROWS = 4 * SGU_CHUNK


def _sgu_fwd(z, ws, bias, name):
    T = z.shape[0]

    def body(zu_ref, zv_ref, ws_ref, b_ref, p_ref, y_ref):
        for r in range(0, SGU_ROWS, SGU_CHUNK):
            rows = slice(r, r + SGU_CHUNK)
            ug, _, _, _, _, _, mixed, _ = _sgu_core(zu_ref[rows, :], zv_ref[rows, :], ws_ref, b_ref[...], p_ref[...])
            y_ref[rows, :] = ug * mixed

    full = lambda shape: pl.BlockSpec(shape, lambda i: (0,) * len(shape))
    return pl.pallas_call(
        body, name=name, grid=(T // SGU_ROWS,),
        in_specs=[pl.BlockSpec((SGU_ROWS, 512), lambda i: (i, COL_AU)), pl.BlockSpec((SGU_ROWS, 512), lambda i: (i, COL_AV)),
                  full((8, 128, 128)), full((128, 512)), full((512, 512))],
        out_specs=pl.BlockSpec((SGU_ROWS, 512), lambda i: (i, 0)), out_shape=jax.ShapeDtypeStruct((T, 512), F32),
        compiler_params=_params(("parallel",)),
    )(z, z, ws, bias, _seg_matrix(512))


def _sgu_bwd(z, dy, ws, ws_t, bias, name):
    T = z.shape[0]

    def body(zu_ref, zv_ref, dy_ref, ws_ref, wst_ref, b_ref, p_ref, dz_ref, dws_ref, db_ref):
        @pl.when(pl.program_id(0) == 0)
        def _():
            dws_ref[...] = jnp.zeros_like(dws_ref)
            db_ref[...] = jnp.zeros_like(db_ref)

        p = p_ref[...]
        zero = jnp.zeros((SGU_CHUNK, 128), BF16)
        dws = [None] * 8
        db = None
        for r in range(0, SGU_ROWS, SGU_CHUNK):
            rows = slice(r, r + SGU_CHUNK)
            zu, zv = zu_ref[rows, :], zv_ref[rows, :]
            ug, tu, tv, rs, vn, vnb, mixed, low = _sgu_core(zu, zv, ws_ref, b_ref[...], p)
            dyv = dy_ref[rows, :]
            dmixed = dyv * ug
            db = dmixed if db is None else db + dmixed
            dmb = dmixed.astype(BF16)
            parts = []
            for j in range(4):
                dmp, vp = dmb[:, 128 * j:128 * (j + 1)], vnb[:, 128 * j:128 * (j + 1)]
                for g, d in ((2 * j, _nt(jnp.where(low, dmp, zero), vp)), (2 * j + 1, _nt(jnp.where(low, zero, dmp), vp))):
                    dws[g] = d if dws[g] is None else dws[g] + d
                parts.append(jnp.where(low, _nn(wst_ref[2 * j], dmp), _nn(wst_ref[2 * j + 1], dmp)))
            dvn = jnp.concatenate(parts, axis=1)
            dvg = rs * (dvn - _segmean(dvn, p) - vn * _segmean(dvn * vn, p))
            dz_ref[rows, 0:512] = (dyv * mixed * _gelu_grad(zu, tu)).astype(BF16)
            dz_ref[rows, 512:1024] = (dvg * _gelu_grad(zv, tv)).astype(BF16)
        db_ref[...] += db
        for g in range(8):
            dws_ref[g] += dws[g]

    full = lambda shape: pl.BlockSpec(shape, lambda i: (0,) * len(shape))
    return pl.pallas_call(
        body, name=name, grid=(T // SGU_ROWS,),
        in_specs=[pl.BlockSpec((SGU_ROWS, 512), lambda i: (i, COL_AU)), pl.BlockSpec((SGU_ROWS, 512), lambda i: (i, COL_AV)),
                  pl.BlockSpec((SGU_ROWS, 512), lambda i: (i, 0)), full((8, 128, 128)), full((8, 128, 128)), full((128, 512)), full((512, 512))],
        out_specs=[pl.BlockSpec((SGU_ROWS, 1024), lambda i: (i, 0)), full((8, 128, 128)), full((128, 512))],
        out_shape=[jax.ShapeDtypeStruct((T, 1024), BF16), jax.ShapeDtypeStruct((8, 128, 128), F32), jax.ShapeDtypeStruct((128, 512), F32)],
        compiler_params=_params(("arbitrary",)),
    )(z, z, dy, ws, ws_t, bias, _seg_matrix(512))


CONV_ROWS = 256


def _conv_taps(pad_ref, w_ref, base, flip):
    acc = None
    for k in range(CONV_WIDTH):
        wk = w_ref[CONV_WIDTH - 1 - k if flip else k]
        t = wk * pad_ref[base + k + 1:base + k + 1 + CONV_ROWS, :]
        acc = t if acc is None else acc + t
    return acc


def _conv_fwd1(z, w, cb, B, S, name):
    T = B * S
    rows = min(CONV_ROWS, S)
    assert rows == CONV_ROWS

    def body(a_ref, g_ref, w_ref, cb_ref, c_ref, pad):
        pad[0:CONV_PAD, :] = jnp.zeros((CONV_PAD, 128), F32)
        pad[CONV_PAD + S:2 * CONV_PAD + S, :] = jnp.zeros((CONV_PAD, 128), F32)
        pad[CONV_PAD:CONV_PAD + S, :] = a_ref[...] * _sigmoid(g_ref[...])

        for base in range(0, S, CONV_ROWS):
            c_ref[base:base + CONV_ROWS, :] = _conv_taps(pad, w_ref, base, False) + cb_ref[...]

    return pl.pallas_call(
        body, name=name, grid=(4, B),
        in_specs=[pl.BlockSpec((S, 128), lambda j, b: (b, 4 * COL_CA + j)), pl.BlockSpec((S, 128), lambda j, b: (b, 4 * COL_CG + j)),
                  pl.BlockSpec((32, 1, 128), lambda j, b: (0, 0, j)), pl.BlockSpec((1, 128), lambda j, b: (0, j))],
        out_specs=pl.BlockSpec((S, 128), lambda j, b: (b, j)), out_shape=jax.ShapeDtypeStruct((T, 512), F32),
        scratch_shapes=[pltpu.VMEM((S + 2 * CONV_PAD, 128), F32)], compiler_params=_params(("parallel", "parallel")),
    )(z, z, w, cb)


def _ln_rows(c):
    mu = jnp.mean(c, axis=-1, keepdims=True)
    xc = c - mu
    rs = lax.rsqrt(jnp.mean(xc * xc, axis=-1, keepdims=True) + LN_EPS)
    return xc * rs, rs


def _conv_bwd2(z, dc, w, B, S, name):
    T = B * S

    def body(a_ref, g_ref, dc_ref, w_ref, da_ref, dg_ref, dw_ref, hpad, dpad):
        @pl.when(pl.program_id(1) == 0)
        def _():
            dw_ref[...] = jnp.zeros_like(dw_ref)

        zeros = jnp.zeros((CONV_PAD, 128), F32)
        for ref in (hpad, dpad):
            ref[0:CONV_PAD, :] = zeros
            ref[CONV_PAD + S:2 * CONV_PAD + S, :] = zeros
        hpad[CONV_PAD:CONV_PAD + S, :] = a_ref[...] * _sigmoid(g_ref[...])
        dpad[CONV_PAD:CONV_PAD + S, :] = dc_ref[...]
        dws = [None] * CONV_WIDTH
        for base in range(0, S, CONV_ROWS):
            rows = slice(base, base + CONV_ROWS)
            dh = _conv_taps(dpad, w_ref, base, True)
            sg = _sigmoid(g_ref[rows, :])
            da_ref[rows, :] = (dh * sg).astype(BF16)
            dg_ref[rows, :] = (dh * a_ref[rows, :] * sg * (1.0 - sg)).astype(BF16)
            dcv = dc_ref[rows, :]
            for k in range(CONV_WIDTH):
                prod = dcv * hpad[base + k + 1:base + k + 1 + CONV_ROWS, :]
                part = jnp.sum(prod.reshape(CONV_ROWS // 8, 8, 128), axis=0)
                dws[k] = part if dws[k] is None else dws[k] + part
        for k in range(CONV_WIDTH):
            dw_ref[k] += jnp.sum(dws[k], axis=0, keepdims=True)

    return pl.pallas_call(
        body, name=name, grid=(4, B),
        in_specs=[pl.BlockSpec((S, 128), lambda j, b: (b, 4 * COL_CA + j)), pl.BlockSpec((S, 128), lambda j, b: (b, 4 * COL_CG + j)),
                  pl.BlockSpec((S, 128), lambda j, b: (b, j)), pl.BlockSpec((32, 1, 128), lambda j, b: (0, 0, j))],
        out_specs=[pl.BlockSpec((S, 128), lambda j, b: (b, j)), pl.BlockSpec((S, 128), lambda j, b: (b, j)),
                   pl.BlockSpec((32, 1, 128), lambda j, b: (0, 0, j))],
        out_shape=[jax.ShapeDtypeStruct((T, 512), BF16), jax.ShapeDtypeStruct((T, 512), BF16), jax.ShapeDtypeStruct((32, 1, 512), F32)],
        scratch_shapes=[pltpu.VMEM((S + 2 * CONV_PAD, 128), F32), pltpu.VMEM((S + 2 * CONV_PAD, 128), F32)],
        compiler_params=_params(("parallel", "arbitrary")),
    )(z, z, dc, w)


def _swap16(x):
    n = x.shape[1]
    first = (lax.broadcasted_iota(jnp.int32, x.shape, 1) % 32) < 16
    return jnp.where(first, pltpu.roll(x, n - 16, 1), pltpu.roll(x, 16, 1))


def _rope(x, cos, sin):
    return x * cos + _swap16(x) * sin


def _rope_t(dy, cos, sin):
    return dy * cos + _swap16(dy * sin)


def _qk_norm(x, p):
    r = lax.rsqrt(_segmean(x * x, p) + RMS_EPS)
    return x * r, r


def _store_heads(ref, val, n):
    for h in range(n):
        ref[h] = val[:, HEAD_DIM * h:HEAD_DIM * (h + 1)].astype(ref.dtype)


def _load_heads(ref, n):
    return jnp.concatenate([ref[h] for h in range(n)], axis=1)


def _prep_fwd(z, gq, gk, rope, B, S, kv_heads, cols, name):
    tm = min(ROW_TILE, S)
    ns = S // tm
    kw = kv_heads * HEAD_DIM
    scale = HEAD_DIM ** -0.5
    qc, kc, vc = cols

    def body(*refs):
        if rope is None:
            q_ref, k_ref, v_ref, gq_ref, gk_ref, p_ref, qo, ko, vo = refs
        else:
            q_ref, k_ref, v_ref, gq_ref, gk_ref, p_ref, cos_ref, sin_ref, qo, ko, vo = refs
        p = p_ref[...]
        qn, _ = _qk_norm(q_ref[...], p)
        kn, _ = _qk_norm(k_ref[...], p[:kw, :kw])
        qn, kn = qn * gq_ref[...], kn * gk_ref[...]
        if rope is not None:
            cos, sin = cos_ref[...], sin_ref[...]
            qn, kn = _rope(qn, cos, sin), _rope(kn, cos[:, :kw], sin[:, :kw])
        _store_heads(qo, qn * scale, N_HEADS)
        _store_heads(ko, kn, kv_heads)
        _store_heads(vo, v_ref[...], kv_heads)

    row = lambda w, c: pl.BlockSpec((tm, w), lambda b, i: (b * ns + i, c))
    const = lambda shape: pl.BlockSpec(shape, lambda b, i: (0,) * len(shape))
    heads = lambda n: pl.BlockSpec((None, n, tm, HEAD_DIM), lambda b, i: (b, 0, i, 0))
    ins = [z, z, z, gq, gk, _seg_matrix(512)]
    specs = [row(512, qc), row(kw, kc), row(kw, vc), const((1, 512)), const((1, kw)), const((512, 512))]
    if rope is not None:
        ins += list(rope)
        specs += [pl.BlockSpec((tm, 512), lambda b, i: (i, 0))] * 2
    return pl.pallas_call(
        body, name=name, grid=(B, ns), in_specs=specs, out_specs=[heads(N_HEADS), heads(kv_heads), heads(kv_heads)],
        out_shape=[jax.ShapeDtypeStruct((B, N_HEADS, S, HEAD_DIM), BF16), jax.ShapeDtypeStruct((B, kv_heads, S, HEAD_DIM), BF16),
                   jax.ShapeDtypeStruct((B, kv_heads, S, HEAD_DIM), BF16)],
        compiler_params=_params(("parallel", "parallel")),
    )(*ins)


def _prep_bwd(z, dq, dk, dv, gq, gk, rope, B, S, kv_heads, cols, name):
    T = B * S
    tm = min(ROW_TILE, S)
    ns = S // tm
    kw = kv_heads * HEAD_DIM
    scale = HEAD_DIM ** -0.5
    qc, kc, _ = cols

    def body(*refs):
        if rope is None:
            q_ref, k_ref, dq_ref, dk_ref, dv_ref, gq_ref, gk_ref, p_ref, dz_ref, dgq_ref, dgk_ref = refs
        else:
            q_ref, k_ref, dq_ref, dk_ref, dv_ref, gq_ref, gk_ref, p_ref, cos_ref, sin_ref, dz_ref, dgq_ref, dgk_ref = refs

        @pl.when((pl.program_id(0) == 0) & (pl.program_id(1) == 0))
        def _():
            dgq_ref[...] = jnp.zeros_like(dgq_ref)
            dgk_ref[...] = jnp.zeros_like(dgk_ref)

        p = p_ref[...]
        dqv = _load_heads(dq_ref, N_HEADS) * scale
        dkv = _load_heads(dk_ref, kv_heads)
        if rope is not None:
            cos, sin = cos_ref[...], sin_ref[...]
            dqv, dkv = _rope_t(dqv, cos, sin), _rope_t(dkv, cos[:, :kw], sin[:, :kw])

        def through_norm(xv, dy, g, pm, dg_ref):
            xh, r = _qk_norm(xv, pm)
            dg_ref[...] += jnp.sum(dy * xh, axis=0, keepdims=True)
            dxh = dy * g
            return r * (dxh - xh * _segmean(dxh * xh, pm))

        dz_ref[:, 0:512] = through_norm(q_ref[...], dqv, gq_ref[...], p, dgq_ref).astype(BF16)
        dz_ref[:, 512:512 + kw] = through_norm(k_ref[...], dkv, gk_ref[...], p[:kw, :kw], dgk_ref).astype(BF16)
        dz_ref[:, 512 + kw:512 + 2 * kw] = _load_heads(dv_ref, kv_heads).astype(BF16)

    row = lambda w, c: pl.BlockSpec((tm, w), lambda b, i: (b * ns + i, c))
    const = lambda shape: pl.BlockSpec(shape, lambda b, i: (0,) * len(shape))
    heads = lambda n: pl.BlockSpec((None, n, tm, HEAD_DIM), lambda b, i: (b, 0, i, 0))
    ins = [z, z, dq, dk, dv, gq, gk, _seg_matrix(512)]
    specs = [row(512, qc), row(kw, kc), heads(N_HEADS), heads(kv_heads), heads(kv_heads), const((1, 512)), const((1, kw)), const((512, 512))]
    if rope is not None:
        ins += list(rope)
        specs += [pl.BlockSpec((tm, 512), lambda b, i: (i, 0))] * 2
    return pl.pallas_call(
        body, name=name, grid=(B, ns), in_specs=specs, out_specs=[row(512 + 2 * kw, 0), const((1, 512)), const((1, kw))],
        out_shape=[jax.ShapeDtypeStruct((T, 512 + 2 * kw), BF16), jax.ShapeDtypeStruct((1, 512), F32), jax.ShapeDtypeStruct((1, kw), F32)],
        compiler_params=_params(("arbitrary", "arbitrary")),
    )(*ins)


def _toeplitz(win, tq, S):
    r = pltpu.roll(jnp.broadcast_to(win, (tq, S + tq)), 0, 1, stride=1, stride_axis=0)
    return r[:, tq:tq + S]


ATTN_HEADS = 4


def _attn_fwd(q, k, v, win, name, nh=ATTN_HEADS, tq=ATTN_TQ):
    B, H, S, _ = q.shape
    shared = k.shape[1] != H
    assert not shared or H // k.shape[1] == nh
    tq = min(tq, S)

    def body(*refs):
        if win is None:
            q_ref, k_ref, v_ref, o_ref = refs
        else:
            q_ref, k_ref, v_ref, w_ref, o_ref = refs
        kvs = [(k_ref[...], v_ref[...])] * nh if shared else [(k_ref[h], v_ref[h]) for h in range(nh)]
        scores = []
        for h in range(nh):
            s = _nt(q_ref[h], kvs[h][0])
            if win is not None:
                s = s + _toeplitz(w_ref[h], tq, S)
            scores.append(s)
        probs = []
        for s in scores:
            p = jnp.exp(s - jnp.max(s, axis=-1, keepdims=True))
            probs.append((p.astype(BF16), jnp.sum(p, axis=-1, keepdims=True)))
        for h, (p, l) in enumerate(probs):
            o_ref[h] = _nn(p, kvs[h][1]) / l

    qs = pl.BlockSpec((None, nh, tq, HEAD_DIM), lambda b, h, i: (b, h, i, 0))
    ks = (pl.BlockSpec((None, None, S, HEAD_DIM), lambda b, h, i: (b, h, 0, 0)) if shared
          else pl.BlockSpec((None, nh, S, HEAD_DIM), lambda b, h, i: (b, h, 0, 0)))
    ins, specs = [q, k, v], [qs, ks, ks]
    if win is not None:
        ins.append(win)
        specs.append(pl.BlockSpec((nh, None, 1, S + tq), lambda b, h, i: (h, i, 0, 0)))
    return pl.pallas_call(body, name=name, grid=(B, H // nh, S // tq), in_specs=specs, out_specs=qs,
                          out_shape=jax.ShapeDtypeStruct((B, H, S, HEAD_DIM), F32),
                          compiler_params=_params(("parallel", "parallel", "parallel")))(*ins)


def _attn_bwd(q, k, v, o, do, win, name, nh=ATTN_HEADS, tq=ATTN_TQ):
    B, H, S, _ = q.shape
    hkv = k.shape[1]
    shared = hkv != H
    assert not shared or H // hkv == nh
    tq = min(tq, S)
    nq = S // tq

    def body(*refs):
        if win is None:
            q_ref, k_ref, v_ref, o_ref, do_ref, dq_ref, dk_ref, dv_ref = refs
        else:
            q_ref, k_ref, v_ref, o_ref, do_ref, w_ref, rev_ref, dq_ref, dk_ref, dv_ref, dw_ref = refs

        @pl.when(pl.program_id(2) == 0)
        def _():
            dk_ref[...] = jnp.zeros_like(dk_ref)
            dv_ref[...] = jnp.zeros_like(dv_ref)

        kvs = [(k_ref[...], v_ref[...])] * nh if shared else [(k_ref[h], v_ref[h]) for h in range(nh)]
        qvs, dobs, scores, dps = [], [], [], []
        for h in range(nh):
            qv, dov = q_ref[h], do_ref[h]
            dob = dov.astype(BF16)
            s = _nt(qv, kvs[h][0])
            if win is not None:
                s = s + _toeplitz(w_ref[h], tq, S)
            dp = _nt(dob, kvs[h][1]) - jnp.sum(dov * o_ref[h], axis=-1, keepdims=True)
            qvs.append(qv)
            dobs.append(dob)
            scores.append(s)
            dps.append(dp)
        pbs, dsbs = [], []
        for s, dp in zip(scores, dps):
            p = jnp.exp(s - jnp.max(s, axis=-1, keepdims=True))
            p = p * (1.0 / jnp.sum(p, axis=-1, keepdims=True))
            pbs.append(p.astype(BF16))
            dsbs.append((p * dp).astype(BF16))
        dk_acc = dv_acc = None
        for h in range(nh):
            dvh, dkh = _tn(pbs[h], dobs[h]), _tn(dsbs[h], qvs[h])
            dq_ref[h] = _nn(dsbs[h], kvs[h][0])
            if shared:
                dv_acc = dvh if dv_acc is None else dv_acc + dvh
                dk_acc = dkh if dk_acc is None else dk_acc + dkh
            else:
                dv_ref[h] += dvh
                dk_ref[h] += dkh
            if win is not None:
                rev = _nn(rev_ref[...], dsbs[h])
                wide = jnp.concatenate([rev, jnp.zeros((tq, tq), F32)], axis=1)
                dw_ref[h] = jnp.sum(pltpu.roll(wide, 0, 1, stride=1, stride_axis=0), axis=0, keepdims=True)
        if shared:
            dv_ref[...] += dv_acc
            dk_ref[...] += dk_acc

    qs = pl.BlockSpec((None, nh, tq, HEAD_DIM), lambda b, h, i: (b, h, i, 0))
    ks = (pl.BlockSpec((None, None, S, HEAD_DIM), lambda b, h, i: (b, h, 0, 0)) if shared
          else pl.BlockSpec((None, nh, S, HEAD_DIM), lambda b, h, i: (b, h, 0, 0)))
    ins, specs = [q, k, v, o, do], [qs, ks, ks, qs, qs]
    outs = [jax.ShapeDtypeStruct((B, H, S, HEAD_DIM), F32), jax.ShapeDtypeStruct((B, hkv, S, HEAD_DIM), F32), jax.ShapeDtypeStruct((B, hkv, S, HEAD_DIM), F32)]
    ospecs = [qs, ks, ks]
    if win is not None:
        ins += [win, jnp.asarray(np.eye(tq, dtype=np.float32)[::-1].copy(), BF16)]
        specs += [pl.BlockSpec((nh, None, 1, S + tq), lambda b, h, i: (h, i, 0, 0)), pl.BlockSpec((tq, tq), lambda b, h, i: (0, 0))]
        outs.append(jax.ShapeDtypeStruct((B, H, nq, 1, S + tq), F32))
        ospecs.append(pl.BlockSpec((None, nh, None, 1, S + tq), lambda b, h, i: (b, h, i, 0, 0)))
    return pl.pallas_call(body, name=name, grid=(B, H // nh, nq), in_specs=specs, out_specs=ospecs, out_shape=outs,
                          compiler_params=_params(("parallel", "parallel", "arbitrary")))(*ins)


def _pattern_count(delta):
    n = jnp.zeros(delta.shape, jnp.int32)
    for window, dil in DIL_PATTERNS:
        n = n + ((delta % dil == 0) & (jnp.abs(delta) <= window // 2)).astype(jnp.int32)
    return n


def _t5_bucket(rel):
    nb = REL_BUCKETS // 2
    max_exact = nb // 2
    ret = jnp.where(rel > 0, nb, 0)
    n = jnp.abs(rel)
    nf = jnp.maximum(n, 1).astype(F32)
    large = max_exact + (jnp.log(nf / max_exact) / math.log(REL_MAX_DIST / max_exact) * (nb - max_exact)).astype(jnp.int32)
    large = jnp.minimum(large, nb - 1)
    return ret + jnp.where(n < max_exact, n, large)


def _bias_windows(rel_bias, S):
    tq = min(ATTN_TQ, S)
    nq = S // tq
    n = nq * (S + tq)
    delta = (jnp.arange(S + tq)[None, :] - (jnp.arange(nq)[:, None] + 1) * tq).reshape(n)
    count = _pattern_count(delta)
    onehot = (_t5_bucket(delta)[None, :] == jnp.arange(REL_BUCKETS)[:, None]).astype(F32)
    extra = jnp.where(count > 0, jnp.log(jnp.maximum(count, 1).astype(F32)), MASKED).reshape(1, n)
    live = (count > 0).astype(F32).reshape(1, n)

    def body(t_ref, oh_ref, live_ref, extra_ref, o_ref):
        o_ref[...] = _nn(t_ref[...], oh_ref[...], HIGHEST) * live_ref[...] + extra_ref[...]

    val = pl.pallas_call(body, name="bias_windows", out_shape=jax.ShapeDtypeStruct((N_HEADS, n), F32),
                         compiler_params=_params())(rel_bias.T, onehot, live, extra)
    return val.reshape(N_HEADS, nq, 1, S + tq)


def _bias_fold(dwin, S, name):
    B, H, nq = dwin.shape[:3]
    tq = min(ATTN_TQ, S)
    n = nq * (S + tq)
    delta = (jnp.arange(S + tq)[None, :] - (tq - 1) - jnp.arange(nq)[:, None] * tq).reshape(n)
    onehot = (_t5_bucket(delta)[:, None] == jnp.arange(128)[None, :]).astype(F32)

    def body(d_ref, oh_ref, o_ref):
        tot = d_ref[0]
        for b in range(1, B):
            tot = tot + d_ref[b]
        o_ref[...] = _nn(tot, oh_ref[...], HIGHEST)

    out = pl.pallas_call(body, name=name, out_shape=jax.ShapeDtypeStruct((H, 128), F32), compiler_params=_params())(dwin.reshape(B, H, n), onehot)
    return out[:, :REL_BUCKETS].T


def _rope_tables(S):
    half = 16
    freqs = ROPE_THETA ** (-jnp.arange(half, dtype=F32) / half)
    t = jnp.arange(S)
    ang_r = (t // GRID_W).astype(F32)[:, None] * freqs[None, :]
    ang_c = (t % GRID_W).astype(F32)[:, None] * freqs[None, :]
    cos = jnp.concatenate([jnp.cos(ang_r)] * 2 + [jnp.cos(ang_c)] * 2, axis=1)
    sin = jnp.concatenate([-jnp.sin(ang_r), jnp.sin(ang_r), -jnp.sin(ang_c), jnp.sin(ang_c)], axis=1)
    return jnp.tile(cos, (1, N_HEADS)), jnp.tile(sin, (1, N_HEADS))


def _conv_act(c, g, b):
    n, rs = _ln_rows(c)
    t = n * g + b
    return t * _sigmoid(t), n, rs, t


def _mix_fwd(ya, ob, c, od, gain, lng, lnb, B, S, name):
    T = B * S
    tm = min(ROW_TILE, S)
    ns = S // tm

    def body(ya_ref, ob_ref, c_ref, od_ref, g_ref, lg_ref, lb_ref, o_ref):
        yc = _conv_act(c_ref[...], lg_ref[...], lb_ref[...])[0]
        ys = [ya_ref[...], _load_heads(ob_ref, N_HEADS), yc, _load_heads(od_ref, N_HEADS)]
        for m, y in enumerate(ys):
            r = lax.rsqrt(jnp.mean(y * y, axis=-1, keepdims=True) + RMS_EPS)
            o_ref[:, 512 * m:512 * (m + 1)] = (y * r * g_ref[:, 512 * m:512 * (m + 1)]).astype(BF16)

    row = pl.BlockSpec((tm, 512), lambda b, i: (b * ns + i, 0))
    heads = pl.BlockSpec((None, N_HEADS, tm, HEAD_DIM), lambda b, i: (b, 0, i, 0))
    vec = pl.BlockSpec((1, 512), lambda b, i: (0, 0))
    return pl.pallas_call(
        body, name=name, grid=(B, ns), in_specs=[row, heads, row, heads, pl.BlockSpec((1, 2048), lambda b, i: (0, 0)), vec, vec],
        out_specs=pl.BlockSpec((tm, 2048), lambda b, i: (b * ns + i, 0)), out_shape=jax.ShapeDtypeStruct((T, 2048), BF16),
        compiler_params=_params(("parallel", "parallel")),
    )(ya, ob, c, od, gain, lng, lnb)


def _mix_bwd(ya, ob, c, od, dycat, gain, lng, lnb, B, S, name):
    T = B * S
    tm = min(ROW_TILE, S)
    ns = S // tm

    def body(ya_ref, ob_ref, c_ref, od_ref, dy_ref, g_ref, lg_ref, lb_ref, dya_ref, dob_ref, dc_ref, dod_ref, dg_ref, dlg_ref, dlb_ref, dcb_ref):
        @pl.when((pl.program_id(0) == 0) & (pl.program_id(1) == 0))
        def _():
            for ref in (dg_ref, dlg_ref, dlb_ref, dcb_ref):
                ref[...] = jnp.zeros_like(ref)

        yc, n, rs, t = _conv_act(c_ref[...], lg_ref[...], lb_ref[...])
        ys = [ya_ref[...], _load_heads(ob_ref, N_HEADS), yc, _load_heads(od_ref, N_HEADS)]
        outs = [dya_ref, dob_ref, None, dod_ref]
        for m, y in enumerate(ys):
            cols = slice(512 * m, 512 * (m + 1))
            r = lax.rsqrt(jnp.mean(y * y, axis=-1, keepdims=True) + RMS_EPS)
            yh = y * r
            dh = dy_ref[:, cols]
            dg_ref[:, cols] += jnp.sum(dh * yh, axis=0, keepdims=True)
            dyh = dh * g_ref[:, cols]
            dyv = r * (dyh - yh * jnp.mean(dyh * yh, axis=-1, keepdims=True))
            if m == 0:
                outs[m][...] = dyv
            elif m == 2:
                sg = _sigmoid(t)
                dt = dyv * sg * (1.0 + t * (1.0 - sg))
                dlg_ref[...] += jnp.sum(dt * n, axis=0, keepdims=True)
                dlb_ref[...] += jnp.sum(dt, axis=0, keepdims=True)
                dn = dt * lg_ref[...]
                dc = rs * (dn - jnp.mean(dn, axis=-1, keepdims=True) - n * jnp.mean(dn * n, axis=-1, keepdims=True))
                dc_ref[...] = dc
                dcb_ref[...] += jnp.sum(dc, axis=0, keepdims=True)
            else:
                _store_heads(outs[m], dyv, N_HEADS)

    row = pl.BlockSpec((tm, 512), lambda b, i: (b * ns + i, 0))
    heads = pl.BlockSpec((None, N_HEADS, tm, HEAD_DIM), lambda b, i: (b, 0, i, 0))
    vec = pl.BlockSpec((1, 2048), lambda b, i: (0, 0))
    flat = jax.ShapeDtypeStruct((T, 512), F32)
    hm = jax.ShapeDtypeStruct((B, N_HEADS, S, HEAD_DIM), F32)
    v512 = pl.BlockSpec((1, 512), lambda b, i: (0, 0))
    s512 = jax.ShapeDtypeStruct((1, 512), F32)
    return pl.pallas_call(
        body, name=name, grid=(B, ns), in_specs=[row, heads, row, heads, pl.BlockSpec((tm, 2048), lambda b, i: (b * ns + i, 0)), vec, v512, v512],
        out_specs=[row, heads, row, heads, vec, v512, v512, v512],
        out_shape=[flat, hm, flat, hm, jax.ShapeDtypeStruct((1, 2048), F32), s512, s512, s512],
        compiler_params=_params(("arbitrary", "arbitrary")),
    )(ya, ob, c, od, dycat, gain, lng, lnb)


def _ffn_down(gate, up, w_down, res, name, norm=None, target=None):
    J, T, n = gate.shape
    N = w_down.shape[2]
    tm = min(256, T)
    steps = T // tm

    def body(g_ref, u_ref, w_ref, r_ref, *rest):
        x_ref = rest[0] if (norm is not None or target is not None) else None
        o_ref, act_ref = rest[-3:-1] if x_ref is not None else rest[-2:]
        acc = None
        for j in range(J):
            g = g_ref[j].astype(F32)
            a = (g * _sigmoid(g) * u_ref[j].astype(F32)).astype(BF16)
            act_ref[j] = a
            d = _nn(a, w_ref[j])
            acc = d if acc is None else acc + d
        y = acc + r_ref[...]
        if target is None:
            o_ref[...] = y
            if norm is not None:
                rest[-1][...] = _rms_rows(y, x_ref[...])
            return
        err = y - x_ref[...]
        o_ref[...] = err * (1.0 / N)
        loss_ref, i = rest[-1], pl.program_id(0)

        @pl.when(i == 0)
        def _():
            loss_ref[...] = jnp.zeros_like(loss_ref)

        loss_ref[...] += jnp.sum(err * err)

        @pl.when(i == steps - 1)
        def _():
            loss_ref[...] = loss_ref[...] * (0.5 / N)

    gu = pl.BlockSpec((J, tm, n), lambda i: (0, i, 0))
    row = pl.BlockSpec((tm, N), lambda i: (i, 0))
    ins, specs = [gate, up, w_down, res], [gu, gu, pl.BlockSpec((J, n, N), lambda i: (0, 0, 0)), row]
    outs, ospecs = [jax.ShapeDtypeStruct((T, N), F32), jax.ShapeDtypeStruct((J, T, n), BF16)], [row, gu]
    if target is not None:
        ins, specs = ins + [target], specs + [row]
        outs, ospecs = outs + [jax.ShapeDtypeStruct((8, 128), F32)], ospecs + [pl.BlockSpec((8, 128), lambda i: (0, 0))]
    elif norm is not None:
        ins, specs = ins + [norm.reshape(1, N)], specs + [pl.BlockSpec((1, N), lambda i: (0, 0))]
        outs, ospecs = outs + [jax.ShapeDtypeStruct((T, N), BF16)], ospecs + [row]
    return pl.pallas_call(body, name=name, grid=(steps,), in_specs=specs, out_specs=ospecs, out_shape=outs,
                          compiler_params=_params(("arbitrary" if target is not None else "parallel",)))(*ins)


def _ffn_down_dx(dx, w_down, gate, up, name):
    J, n, D = w_down.shape
    T = dx.shape[0]
    tm = min(512, T)

    def body(dx_ref, w_ref, g_ref, u_ref, dg_ref, du_ref):
        d = _nt(dx_ref[...].astype(BF16), w_ref[...])
        g = g_ref[...].astype(F32)
        s = _sigmoid(g)
        dg_ref[...] = (d * u_ref[...].astype(F32) * s * (1.0 + g * (1.0 - s))).astype(BF16)
        du_ref[...] = (d * g * s).astype(BF16)

    blk = pl.BlockSpec((None, tm, n), lambda j, i: (j, i, 0))
    shape = jax.ShapeDtypeStruct((J, T, n), BF16)
    return pl.pallas_call(body, name=name, grid=(J, T // tm),
                          in_specs=[pl.BlockSpec((tm, D), lambda j, i: (i, 0)), pl.BlockSpec((None, n, D), lambda j, i: (j, 0, 0)), blk, blk],
                          out_specs=[blk, blk], out_shape=[shape, shape], compiler_params=_params(("parallel", "parallel")))(dx, w_down, gate, up)


def _row_tile(R):
    best = R
    for cand in range(16, min(R, 272) + 1, 16):
        if R % cand == 0:
            best = cand
    return best


def _adamw(w, m, v, stack, name, layer=None, prev=None):
    n, R, C = stack.shape
    tm = _row_tile(R)
    nb = R // tm
    off = 0 if layer is None else layer * nb
    c1 = 1.0 - ADAM_B1 ** ADAM_STEP
    c2 = 1.0 - ADAM_B2 ** ADAM_STEP

    def body(w_ref, m_ref, v_ref, s_ref, *rest):
        g_ref, d_ref, mo_ref, vo_ref = rest[-4:]
        g = s_ref[0].astype(F32)
        for k in range(1, n):
            g = g + s_ref[k].astype(F32)
        mn = ADAM_B1 * m_ref[...] + (1.0 - ADAM_B1) * g
        vn = ADAM_B2 * v_ref[...] + (1.0 - ADAM_B2) * (g * g)
        g_ref[...] = g
        mo_ref[...] = mn
        vo_ref[...] = vn
        d_ref[...] = -ADAM_LR * ((mn / c1) / (jnp.sqrt(vn / c2) + ADAM_EPS) + ADAM_WD * w_ref[...])

    blk = pl.BlockSpec((tm, C), lambda i: (i + off, 0))
    ins = [w, m, v, stack]
    specs = [blk, blk, blk, pl.BlockSpec((n, tm, C), lambda i: (0, i, 0))]
    aliases = {}
    if prev is not None:
        ins += list(prev)
        specs += [pl.BlockSpec(memory_space=pl.ANY)] * 4
        aliases = {4 + t: t for t in range(4)}
    shape = jax.ShapeDtypeStruct(w.shape, F32)
    return pl.pallas_call(body, name=name, grid=(nb,), in_specs=specs, out_specs=[blk] * 4, out_shape=[shape] * 4,
                          input_output_aliases=aliases, compiler_params=_params(("parallel",)))(*ins)


HBM = pl.BlockSpec(memory_space=pltpu.HBM)
SEM = pl.BlockSpec(memory_space=pltpu.SEMAPHORE)
EFFECT = pltpu.SideEffectType.DATAFLOW_SIDE_EFFECTING


PEERS = {"scatter": (1, 2, 3, 4, 5, 6, 7), "gather": (1, 2, 3, 4, 5, 6, 7), "chips": (1, 2, 4, 6), "forward": (2, 4, 6)}


def _spread_copies(srcs, lands, send_sems, recv_sems, local_sems, kind, waiting, base=0):
    x, y, c = lax.axis_index("x"), lax.axis_index("y"), lax.axis_index("c")
    me = 4 * x + 2 * y + c

    def peer(bits):
        dev = (1 - x if bits & 4 else x, 1 - y if bits & 2 else y, 1 - c if bits & 1 else c)
        return dev, 4 * dev[0] + 2 * dev[1] + dev[2]

    plan = PEERS[kind]
    remote, local = [], []
    for a, l in enumerate(lands):
        for d, bits in enumerate(plan):
            dev, pid = peer(bits)
            if kind == "forward":
                src, dst, dev = l.at[pid], l.at[peer(bits | 1)[1] if waiting else pid], peer(1)[0]
            else:
                src, dst = (srcs[a].at[pid] if kind == "scatter" else srcs[a]), l.at[pid if waiting else me]
            remote.append(pltpu.make_async_remote_copy(
                src_ref=src, dst_ref=dst, send_sem=send_sems.at[(base + a) * len(plan) + d], recv_sem=recv_sems.at[(base + a) * len(plan) + d],
                device_id=dev, device_id_type=MESH_ID))
        if kind != "forward":
            local.append(pltpu.make_async_copy(srcs[a].at[me] if kind == "scatter" else srcs[a], l.at[me], local_sems.at[base + a]))
    return remote, local


def _spread_start(srcs, kind, name, after=None, lands=None):
    if kind == "forward":
        srcs = []
    else:
        shapes = [a.shape if kind == "scatter" else (N_DEV,) + a.shape for a in srcs]
        lands = [lax.empty(shp, a.dtype) for shp, a in zip(shapes, srcs)]
    ns, nl, per = len(srcs), len(lands), len(PEERS[kind])
    extra = [] if after is None else [after]
    sem_shapes = [pltpu.SemaphoreType.DMA((nl * per,))] * 2 + ([pltpu.SemaphoreType.DMA((nl,))] if ns else [])

    def body(*refs):
        src_refs, land_refs = refs[:ns], refs[ns:ns + nl]
        sems = refs[ns + nl + len(extra):ns + nl + len(extra) + len(sem_shapes)]
        remote, local = _spread_copies(src_refs, land_refs, sems[0], sems[1], sems[2] if ns else None, kind, False)
        for cp in remote + local:
            cp.start()
        refs[-1][...] = jnp.zeros((8, 128), F32)

    outs = pl.pallas_call(
        body, name=name,
        out_shape=(*sem_shapes, *[pltpu.HBM(a.shape, a.dtype) for a in srcs + lands], jax.ShapeDtypeStruct((8, 128), F32)),
        in_specs=[HBM] * (ns + nl) + [pl.BlockSpec(memory_space=pl.ANY)] * len(extra),
        out_specs=(*[SEM] * len(sem_shapes), *[HBM] * (ns + nl), pl.BlockSpec(memory_space=pltpu.VMEM)),
        input_output_aliases={i: len(sem_shapes) + i for i in range(ns + nl)},
        compiler_params=pltpu.CompilerParams(has_side_effects=EFFECT),
    )(*[pltpu.with_memory_space_constraint(a, pltpu.HBM) for a in srcs + lands], *extra)
    k = len(sem_shapes)
    return outs[:k], list(outs[k:k + ns]), list(outs[k + ns:k + ns + nl]), outs[-1]


def _spread_wait(sems, srcs, lands, after, kind, name, base=0):
    ns, nl = len(srcs), len(lands)
    after = list(after) if isinstance(after, (list, tuple)) else [after]

    def body(*refs):
        src_refs, land_refs = refs[:ns], refs[ns:ns + nl]
        s = refs[ns + nl:ns + nl + len(sems)]
        remote, local = _spread_copies(src_refs, land_refs, s[0], s[1], s[2] if ns else None, kind, True, base)
        for cp in remote:
            cp.wait_send()
            cp.wait_recv()
        for cp in local:
            cp.wait()

    outs = pl.pallas_call(
        body, name=name, out_shape=tuple(pltpu.HBM(a.shape, a.dtype) for a in srcs + lands),
        in_specs=[HBM] * (ns + nl) + [SEM] * len(sems) + [pl.BlockSpec(memory_space=pl.ANY)] * len(after), out_specs=tuple([HBM] * (ns + nl)),
        input_output_aliases={i: i for i in range(ns + nl)}, compiler_params=pltpu.CompilerParams(has_side_effects=EFFECT),
    )(*srcs, *lands, *sems, *after)
    return list(outs[ns:])


SMALL = ("rel_bias", "norm1_g", "sgu_w", "sgu_b", "dil_qn_g", "dil_kn_g", "conv_w", "conv_b", "conv_ln_g", "conv_ln_b",
         "gqa_qn_g", "gqa_kn_g", "mix_norm_g", "norm2_g")
LARGE = ("w_in", "w_out", "w_gate", "w_up", "w_down")
EARLY = tuple(k for k in SMALL if k != "norm1_g")


def _local_step(x, target, p, B, S, fetch, emit, mid, early):
    T = B * S
    rope = _rope_tables(S)
    win = _bias_windows(p["rel_bias"], S)
    tile8 = lambda g: jnp.tile(g.reshape(1, HEAD_DIM), (1, N_HEADS))
    cols_b = (COL_BQ, COL_BK, COL_BV)
    cols_d = (COL_DQ, COL_DK128, COL_DV128)
    saved = []
    for l in range(DEPTH):
        s = {"x": x}
        s["ws"] = p["sgu_w"][l].astype(BF16)
        s["bias"] = jnp.repeat(p["sgu_b"][l].T, HEAD_DIM, axis=1)
        s["h"] = _rms_fwd(x, p["norm1_g"][l], f"rms1_fwd_{l}") if l == 0 else h_next
        s["win"] = fetch(l, "in", s["h"])
        s["cw"] = jnp.pad(s["win"]["conv_w"], ((0, 1), (0, 0))).reshape(32, 1, 512)
        z = s["z"] = _matmul(s["h"], s["win"]["w_in"], "nt", f"in_proj_{l}", tk=D_MODEL)
        s["bias"] = s["bias"] + mid(l, z)
        s["ya"] = _sgu_fwd(z, s["ws"], s["bias"], f"sgu_fwd_{l}")
        s["c"] = _conv_fwd1(z, s["cw"], p["conv_b"][l].reshape(1, 512), B, S, f"conv_fwd_{l}")
        s["ln"] = (p["conv_ln_g"][l].reshape(1, 512), p["conv_ln_b"][l].reshape(1, 512))
        s["gb"] = (tile8(p["dil_qn_g"][l]), tile8(p["dil_kn_g"][l]))
        s["gd"] = (tile8(p["gqa_qn_g"][l]), tile8(p["gqa_kn_g"][l])[:, :KV_WIDTH])
        s["qkv_b"] = _prep_fwd(z, *s["gb"], None, B, S, N_HEADS, cols_b, f"prep_b_fwd_{l}")
        s["qkv_d"] = _prep_fwd(z, *s["gd"], rope, B, S, KV_HEADS, cols_d, f"prep_d_fwd_{l}")
        s["ob"] = _attn_fwd(*s["qkv_b"], win, f"attn_b_fwd_{l}")
        s["od"] = _attn_fwd(*s["qkv_d"], None, f"attn_d_fwd_{l}", tq=GQA_TQ)
        s["gmix"] = p["mix_norm_g"][l].reshape(1, 2048)
        s["ycat"] = _mix_fwd(s["ya"], s["ob"], s["c"], s["od"], s["gmix"], *s["ln"], B, S, f"mix_fwd_{l}")
        s["wout"] = fetch(l, "out", s["ycat"])["w_out"]
        x1, s["h2"] = _matmul(s["ycat"], s["wout"], "nn", f"out_proj_{l}", res=x, tk=D_MODEL, norm=p["norm2_g"][l])
        s["x1"] = x1
        s["ffn"] = fetch(l, "ffn", s["h2"])
        s["gate"] = _mm_shard_out(s["h2"], s["ffn"]["w_gate"], "nt", f"ffn_gate_{l}", out_dtype=BF16, tm=1024)
        s["up"] = _mm_shard_out(s["h2"], s["ffn"]["w_up"], "nt", f"ffn_up_{l}", out_dtype=BF16, tm=1024)
        if l + 1 < DEPTH:
            x, s["act"], h_next = _ffn_down(s["gate"], s["up"], s["ffn"]["w_down"], x1, f"ffn_down_{l}", norm=p["norm1_g"][l + 1])
        else:
            dx, s["act"], loss_blk = _ffn_down(s["gate"], s["up"], s["ffn"]["w_down"], x1, f"ffn_down_{l}", target=target)
        saved.append(s)

    g = {k: [None] * DEPTH for k in SMALL if k != "rel_bias"}
    dwin_total = None
    for l in reversed(range(DEPTH)):
        s = saved[l]
        z, ffn = s["z"], s["ffn"]
        dgate, dup = _ffn_down_dx(dx, ffn["w_down"], s["gate"], s["up"], f"ffn_down_dx_{l}")
        tok = emit(l, "w_down", _mm_shard_m(s["act"], dx, f"ffn_down_dw_{l}", out_dtype=BF16, tm=FFN_BLOCK, tn=512, tk=T))
        tok += emit(l, "w_gate", _mm_shard_m(dgate, s["h2"], f"ffn_gate_dw_{l}", out_dtype=BF16, tm=FFN_BLOCK, tn=512, tk=T))
        tok += emit(l, "w_up", _mm_shard_m(dup, s["h2"], f"ffn_up_dw_{l}", out_dtype=BF16, tm=FFN_BLOCK, tn=512, tk=T))
        dh2 = _mm_shard_k([(dgate, ffn["w_gate"]), (dup, ffn["w_up"])], "nn", f"ffn_up_dx_{l}", tn=512, fold=FFN_GROUPS)
        dycat, dx1, dg2 = _out_proj_dx(dh2, s["x1"], p["norm2_g"][l] + tok, dx, s["wout"], f"out_proj_dx_{l}")
        g["norm2_g"][l] = dg2[0]
        tok = emit(l, "w_out", _matmul(s["ycat"], dx1, "tn", f"out_proj_dw_{l}", out_dtype=BF16, tn=1024, tk=T))
        dya, dob, dc, dod, dgm, dlg, dlb, dcb = _mix_bwd(s["ya"], s["ob"], s["c"], s["od"], dycat, s["gmix"] + tok, *s["ln"], B, S, f"mix_bwd_{l}")
        g["mix_norm_g"][l] = dgm[0]
        dz_a, dws, dbias = _sgu_bwd(z, dya, s["ws"], jnp.swapaxes(s["ws"], 1, 2), s["bias"], f"sgu_bwd_{l}")
        g["sgu_w"][l] = dws
        g["sgu_b"][l] = dbias.reshape(128, 8, HEAD_DIM).sum(-1).T
        g["conv_ln_g"][l], g["conv_ln_b"][l], g["conv_b"][l] = dlg[0], dlb[0], dcb[0]
        dz_ca, dz_cg, dcw = _conv_bwd2(z, dc, s["cw"], B, S, f"conv_bwd_{l}")
        g["conv_w"][l] = dcw.reshape(32, 512)[:CONV_WIDTH]
        dq, dk, dv, dwin = _attn_bwd(*s["qkv_b"], s["ob"], dob, win, f"attn_b_bwd_{l}")
        dwin_total = dwin if dwin_total is None else dwin_total + dwin
        dz_b, dgq, dgk = _prep_bwd(z, dq, dk, dv, *s["gb"], None, B, S, N_HEADS, cols_b, f"prep_b_bwd_{l}")
        g["dil_qn_g"][l] = dgq.reshape(N_HEADS, HEAD_DIM).sum(0)
        g["dil_kn_g"][l] = dgk.reshape(N_HEADS, HEAD_DIM).sum(0)
        dq, dk, dv = _attn_bwd(*s["qkv_d"], s["od"], dod, None, f"attn_d_bwd_{l}", tq=GQA_TQ)
        dz_d, dgq, dgk = _prep_bwd(z, dq, dk, dv, *s["gd"], rope, B, S, KV_HEADS, cols_d, f"prep_d_bwd_{l}")
        g["gqa_qn_g"][l] = dgq.reshape(N_HEADS, HEAD_DIM).sum(0)
        g["gqa_kn_g"][l] = dgk.reshape(KV_HEADS, HEAD_DIM).sum(0)
        dz = [dz_a, dz_b, dz_ca, dz_cg, dz_d]
        tok = jnp.zeros((), F32)
        if l == 0:
            done = {k: jnp.stack(v) for k, v in g.items() if k != "norm1_g"}
            done["rel_bias"] = _bias_fold(dwin_total, S, "bias_fold")
            tok = early(done)
        tok += emit(l, "w_in", _in_proj_dw(dz, s["h"], f"in_proj_dw_{l}"))
        dx, dg1 = _in_proj_dx(dz, s["win"]["w_in"], s["x"], p["norm1_g"][l] + tok, dx1, f"in_proj_dx_{l}")
        g["norm1_g"][l] = dg1[0]

    return loss_blk[0, 0], dx, jnp.stack(g["norm1_g"])


GROUPS = {"in": ("w_in",), "out": ("w_out",), "ffn": ("w_gate", "w_up", "w_down")}
COL_SHARDED = ("w_in", "w_gate", "w_up")


def kernel(x, rel_bias, norm1_g, w_in, sgu_w, sgu_b, dil_qn_g, dil_kn_g, conv_w, conv_b, conv_ln_g, conv_ln_b, gqa_qn_g, gqa_kn_g, mix_norm_g, w_out, norm2_g, w_gate, w_up, w_down, loss_target, m_rel_bias, m_norm1_g, m_w_in, m_sgu_w, m_sgu_b, m_dil_qn_g, m_dil_kn_g, m_conv_w, m_conv_b, m_conv_ln_g, m_conv_ln_b, m_gqa_qn_g, m_gqa_kn_g, m_mix_norm_g, m_w_out, m_norm2_g, m_w_gate, m_w_up, m_w_down, v_rel_bias, v_norm1_g, v_w_in, v_sgu_w, v_sgu_b, v_dil_qn_g, v_dil_kn_g, v_conv_w, v_conv_b, v_conv_ln_g, v_conv_ln_b, v_gqa_qn_g, v_gqa_kn_g, v_mix_norm_g, v_w_out, v_norm2_g, v_w_gate, v_w_up, v_w_down):
    w = dict(rel_bias=rel_bias, norm1_g=norm1_g, w_in=w_in, sgu_w=sgu_w, sgu_b=sgu_b, dil_qn_g=dil_qn_g, dil_kn_g=dil_kn_g, conv_w=conv_w,
             conv_b=conv_b, conv_ln_g=conv_ln_g, conv_ln_b=conv_ln_b, gqa_qn_g=gqa_qn_g, gqa_kn_g=gqa_kn_g, mix_norm_g=mix_norm_g,
             w_out=w_out, norm2_g=norm2_g, w_gate=w_gate, w_up=w_up, w_down=w_down)
    m = dict(rel_bias=m_rel_bias, norm1_g=m_norm1_g, w_in=m_w_in, sgu_w=m_sgu_w, sgu_b=m_sgu_b, dil_qn_g=m_dil_qn_g, dil_kn_g=m_dil_kn_g,
             conv_w=m_conv_w, conv_b=m_conv_b, conv_ln_g=m_conv_ln_g, conv_ln_b=m_conv_ln_b, gqa_qn_g=m_gqa_qn_g, gqa_kn_g=m_gqa_kn_g,
             mix_norm_g=m_mix_norm_g, w_out=m_w_out, norm2_g=m_norm2_g, w_gate=m_w_gate, w_up=m_w_up, w_down=m_w_down)
    v = dict(rel_bias=v_rel_bias, norm1_g=v_norm1_g, w_in=v_w_in, sgu_w=v_sgu_w, sgu_b=v_sgu_b, dil_qn_g=v_dil_qn_g, dil_kn_g=v_dil_kn_g,
             conv_w=v_conv_w, conv_b=v_conv_b, conv_ln_g=v_conv_ln_g, conv_ln_b=v_conv_ln_b, gqa_qn_g=v_gqa_qn_g, gqa_kn_g=v_gqa_kn_g,
             mix_norm_g=v_mix_norm_g, w_out=v_w_out, norm2_g=v_norm2_g, w_gate=v_w_gate, w_up=v_w_up, w_down=v_w_down)
    names = list(w)
    B, S, D = x.shape
    T = B * S
    me = 4 * lax.axis_index("x") + 2 * lax.axis_index("y") + lax.axis_index("c")

    view = lambda a, k: jnp.swapaxes(a, 1, 2) if k in COL_SHARDED else a
    bf = {k: view(w[k], k).astype(BF16) for k in LARGE}
    spreads, forwards = {}, {}

    def members(group):
        return len(GROUPS[group]) + (group == "in")

    def start_gather(l, after=None):
        srcs = []
        for group in GROUPS:
            srcs += [bf[k][l] for k in GROUPS[group]] + ([conv_w[l]] if group == "in" else [])
        sems, srcs, lands, token = _spread_start(srcs, "chips", f"gather_{l}_start", after)
        base = 0
        for group in GROUPS:
            spreads[l, group] = (sems, srcs[base:base + members(group)], lands[base:base + members(group)], base)
            base += members(group)
        return token[0, 0]

    def arrived(l, group, after):
        sems, srcs, lands, base = spreads[l, group]
        return _spread_wait(sems, srcs, lands, after, "chips", f"gather_{group}_{l}_wait", base)

    def forward(l, groups, after):
        lands = []
        for group in groups:
            lands += arrived(l, group, after)
        sems, _, lands, token = _spread_start(None, "forward", f"forward_{groups[0]}_{l}_start", lands=lands)
        base = 0
        for group in groups:
            forwards[l, group] = (sems, lands[base:base + members(group)], base)
            base += members(group)
        return token

    def landed(l, group, after):
        sems, lands, base = forwards[l, group]
        return _spread_wait(sems, [], lands, after, "forward", f"forward_{group}_{l}_wait", base)

    tok0 = start_gather(0)
    small = {k: w[k] for k in SMALL}
    small["norm1_g"] = norm1_g.at[0].add(tok0)

    def mid(l, z):
        return start_gather(1, z) if l == 0 else jnp.zeros((), F32)

    def fetch(l, group, after):
        if group == "in":
            tok = forward(0, ("in",), after) if l == 0 else after
        elif group == "out":
            tok = forward(l, ("out", "ffn"), after)
        else:
            tok = forward(1, ("in",), after) if l == 0 else after
        got = dict(zip(GROUPS[group] + ("conv_w",), landed(l, group, [after, tok])))
        if group == "in":
            got["w_in"] = got["w_in"].reshape(IN_WIDTH, D)
            got["conv_w"] = jnp.transpose(got["conv_w"], (1, 0, 2)).reshape(CONV_WIDTH, 512)
        if group == "out":
            got["w_out"] = got["w_out"].reshape(D, D)
        if group == "ffn":
            got = {k: a.reshape(FFN_GROUPS, -1, D) for k, a in got.items()}
        return got

    scatters = {}

    def emit(l, k, dw):
        dw = dw.reshape(N_DEV, -1, D)
        scatters[l, k] = _spread_start([dw], "scatter", f"scatter_{k}_{l}_start")
        return scatters[l, k][3][0, 0]

    flat2 = lambda a: a.reshape(-1, a.shape[-1])
    small_spread = []

    def early(done):
        small_spread.append(_spread_start([flat2(done[k]) for k in EARLY], "gather", "gather_small_grads_start"))
        return small_spread[0][3][0, 0]

    loss_part, dx, dnorm1 = _local_step(x.reshape(T, D), loss_target.reshape(T, D), small, B, S, fetch, emit, mid, early)
    loss = lax.psum(loss_part, ("x", "y", "c"))

    out_g, out_d, out_m, out_v = {}, {}, {}, {}

    def update_large(k, after):
        shp = view(w[k], k).shape
        two_d = lambda a: view(a, k).reshape(-1, shp[-1])
        res = None
        for l in reversed(range(DEPTH)):
            sems, srcs, lands, _ = scatters[l, k]
            stack = _spread_wait(sems, srcs, lands, after, "scatter", f"scatter_{k}_{l}_wait")[0]
            res = _adamw(two_d(w[k]), two_d(m[k]), two_d(v[k]), stack.reshape(N_DEV, -1, shp[-1]), f"adamw_{k}_{l}", layer=l, prev=res)
        out_g[k], out_d[k], out_m[k], out_v[k] = [view(a.reshape(shp), k) for a in res]
        return res[0]

    late_sems, late_srcs, late_lands, late_tok = _spread_start([flat2(dnorm1)], "gather", "gather_norm1_grad_start")
    after = [dx, late_tok]
    for k in ("w_down", "w_gate", "w_up", "w_out"):
        after = update_large(k, after)
    sems, srcs, lands, _ = small_spread[0]
    stacks = dict(zip(EARLY, _spread_wait(sems, srcs, lands, after, "gather", "gather_small_grads_wait")))
    stacks["norm1_g"] = _spread_wait(late_sems, late_srcs, late_lands, after, "gather", "gather_norm1_grad_wait")[0]
    for k in SMALL:
        stack = stacks[k]
        if k == "conv_w":
            stack = lax.dynamic_slice_in_dim(stack, me * (512 // N_DEV), 512 // N_DEV, axis=2)
        res = _adamw(flat2(w[k]), flat2(m[k]), flat2(v[k]), stack, f"adamw_{k}")
        out_g[k], out_d[k], out_m[k], out_v[k] = [a.reshape(w[k].shape) for a in res]
        after = res[0]
    update_large("w_in", after)

    return (loss, dx.reshape(B, S, D), *[out_g[k] for k in names], *[out_d[k] for k in names],
            *[out_m[k] for k in names], *[out_v[k] for k in names])
```

```python
import functools
import math

import numpy as np
import jax
import jax.numpy as jnp
from jax import lax
from jax.experimental import pallas as pl
from jax.experimental.pallas import tpu as pltpu

F32 = jnp.float32
BF16 = jnp.bfloat16
HIGHEST = lax.Precision.HIGHEST
MESH_ID = pl.DeviceIdType.MESH

D_MODEL = 2048
DEPTH = 2
HEAD_DIM = 64
GROUP_WIDTH = 512
N_HEADS = 8
KV_HEADS = 2
KV_WIDTH = 128
SGU_CHUNK = 128
CONV_WIDTH = 31
CONV_PAD = 16
GRID_W = 64
ROPE_THETA = 10000.0
REL_BUCKETS = 32
REL_MAX_DIST = 1024
DIL_PATTERNS = ((128, 1), (512, 4), (2048, 16))
FFN_HIDDEN = 5632
IN_WIDTH = 4352
RMS_EPS = 1e-6
LN_EPS = 1e-5
MASKED = -1e30
N_DEV = 8

ADAM_LR = 0.001
ADAM_B1 = 0.9
ADAM_B2 = 0.999
ADAM_EPS = 1e-08
ADAM_WD = 0.01
ADAM_STEP = 10

COL_AU, COL_AV, COL_BQ, COL_BK, COL_BV, COL_CA, COL_CG, COL_DQ = range(8)
COL_DK128, COL_DV128 = 32, 33

VMEM_LIMIT = 56 * 1024 * 1024
FFN_GROUPS = 2
FFN_BLOCK = 1408
ATTN_TQ = 256
GQA_TQ = 512
ROW_TILE = 512


def _params(sem=None, vmem=VMEM_LIMIT):
    return pltpu.CompilerParams(dimension_semantics=sem, vmem_limit_bytes=vmem)


def _dot(a, b, dims, precision=None):
    return lax.dot_general(a, b, (dims, ((), ())), precision=precision, preferred_element_type=F32)


def _nn(a, b, precision=None):
    return _dot(a, b, ((1,), (0,)), precision)


def _nt(a, b):
    return _dot(a, b, ((1,), (1,)))


def _tn(a, b):
    return _dot(a, b, ((0,), (0,)))


DIMS = {"nn": ((1,), (0,)), "nt": ((1,), (1,)), "tn": ((0,), (0,))}


def _pick(n, cands):
    for c in cands:
        if n % c == 0:
            return c
    return n


def _mm_call(name, mode, pairs, specs, o_spec, out_sds, grid, acc_shape, res=None, fold=None, norm=None):
    npair, nk, dims = len(pairs), grid[2], DIMS[mode]

    def body(*refs):
        ab = refs[:2 * npair]
        at = 2 * npair
        r_ref = refs[at] if res is not None else None
        at += res is not None
        g_ref = refs[at] if norm is not None else None
        at += norm is not None
        o_ref = refs[at]
        h_ref = refs[at + 1] if norm is not None else None
        part = None
        for t in range(npair):
            for s in ([None] if fold is None else range(fold)):
                a_blk = ab[2 * t][...] if s is None else ab[2 * t][s]
                b_blk = ab[2 * t + 1][...] if s is None else ab[2 * t + 1][s]
                d = _dot(a_blk.astype(BF16), b_blk.astype(BF16), dims)
                part = d if part is None else part + d

        def finish(r):
            if r_ref is not None:
                r = r + r_ref[...]
            o_ref[...] = r.astype(o_ref.dtype)
            if h_ref is not None:
                h_ref[...] = _rms_rows(r, g_ref[...])

        if nk == 1:
            finish(part)
            return
        acc, k = refs[-1], pl.program_id(2)

        @pl.when(k == 0)
        def _():
            acc[...] = part

        @pl.when(k > 0)
        def _():
            acc[...] += part

        @pl.when(k == nk - 1)
        def _():
            finish(acc[...])

    ins = [t for pair in pairs for t in pair]
    in_specs = [t for pair in specs for t in pair]
    if res is not None:
        ins.append(res)
        in_specs.append(o_spec)
    out_specs = o_spec
    if norm is not None:
        ins.append(norm)
        in_specs.append(pl.BlockSpec(norm.shape, lambda *_: (0, 0)))
        out_specs, out_sds = [o_spec, o_spec], [out_sds, jax.ShapeDtypeStruct(out_sds.shape, BF16)]
    return pl.pallas_call(
        body, name=name, grid=grid, in_specs=in_specs, out_specs=out_specs, out_shape=out_sds,
        scratch_shapes=[pltpu.VMEM(acc_shape, F32)] if nk > 1 else [],
        compiler_params=_params(("parallel", "parallel", "arbitrary")),
    )(*ins)


def _rms_rows(x, g):
    return (x * lax.rsqrt(jnp.mean(x * x, axis=-1, keepdims=True) + RMS_EPS) * g).astype(BF16)


def _matmul(a, b, mode, name, res=None, out_dtype=F32, tm=512, tn=None, tk=None, norm=None):
    if mode == "nn":
        (M, K), N = a.shape, b.shape[1]
    elif mode == "nt":
        (M, K), N = a.shape, b.shape[0]
    else:
        (K, M), N = a.shape, b.shape[1]
    tm = min(tm, M)
    tn = tn or _pick(N, (2176, 2048, 1408, 1024, 512))
    tk = tk or _pick(K, (1024, 2176, 1408, 512))
    assert M % tm == 0 and N % tn == 0 and K % tk == 0, (M, N, K, tm, tn, tk)
    a_spec = pl.BlockSpec((tk, tm), lambda i, j, k: (k, i)) if mode == "tn" else pl.BlockSpec((tm, tk), lambda i, j, k: (i, k))
    b_spec = pl.BlockSpec((tn, tk), lambda i, j, k: (j, k)) if mode == "nt" else pl.BlockSpec((tk, tn), lambda i, j, k: (k, j))
    o_spec = pl.BlockSpec((tm, tn), lambda i, j, k: (i, j))
    assert norm is None or tn == N
    return _mm_call(name, mode, [(a, b)], [(a_spec, b_spec)], o_spec, jax.ShapeDtypeStruct((M, N), out_dtype),
                    (M // tm, N // tn, K // tk), (tm, tn), res, norm=None if norm is None else norm.reshape(1, N))


def _mm_shard_out(a, bs, mode, name, out_dtype=F32, tm=512, tk=None):
    J = bs.shape[0]
    n = bs.shape[1] if mode == "nt" else bs.shape[2]
    (K, M) = a.shape if mode == "tn" else a.shape[::-1]
    tm = min(tm, M)
    tk = tk or (K if mode != "tn" else _pick(K, (1024, 512)))
    a_spec = pl.BlockSpec((tk, tm), lambda j, i, k: (k, i)) if mode == "tn" else pl.BlockSpec((tm, tk), lambda j, i, k: (i, k))
    b_spec = pl.BlockSpec((None, n, tk), lambda j, i, k: (j, 0, k)) if mode == "nt" else pl.BlockSpec((None, tk, n), lambda j, i, k: (j, k, 0))
    o_spec = pl.BlockSpec((None, tm, n), lambda j, i, k: (j, i, 0))
    return _mm_call(name, mode, [(a, bs)], [(a_spec, b_spec)], o_spec, jax.ShapeDtypeStruct((J, M, n), out_dtype),
                    (J, M // tm, K // tk), (tm, n))


def _mm_shard_k(pairs, mode, name, res=None, out_dtype=F32, tm=512, tn=None, fold=1):
    J, M, n = pairs[0][0].shape
    N = pairs[0][1].shape[2] if mode == "nn" else pairs[0][1].shape[1]
    tm = min(tm, M)
    tn = tn or _pick(N, (2048, 1024, 512))
    a_spec = pl.BlockSpec((fold, tm, n), lambda i, j, k: (k, i, 0))
    b_spec = pl.BlockSpec((fold, n, tn), lambda i, j, k: (k, 0, j)) if mode == "nn" else pl.BlockSpec((fold, tn, n), lambda i, j, k: (k, j, 0))
    o_spec = pl.BlockSpec((tm, tn), lambda i, j, k: (i, j))
    return _mm_call(name, mode, pairs, [(a_spec, b_spec)] * len(pairs), o_spec, jax.ShapeDtypeStruct((M, N), out_dtype),
                    (M // tm, N // tn, J // fold), (tm, tn), res, fold)


def _mm_shard_m(as_, b, name, out_dtype=F32, tm=None, tn=None, tk=512):
    J, K, n = as_.shape
    N = b.shape[1]
    tm = tm or n
    tn = tn or _pick(N, (2048, 1024, 512))
    tk = min(tk, K)
    nn = N // tn
    a_spec = pl.BlockSpec((None, tk, tm), lambda j, i, k: (j, k, i // nn))
    b_spec = pl.BlockSpec((tk, tn), lambda j, i, k: (k, i % nn))
    o_spec = pl.BlockSpec((None, tm, tn), lambda j, i, k: (j, i // nn, i % nn))
    return _mm_call(name, "tn", [(as_, b)], [(a_spec, b_spec)], o_spec, jax.ShapeDtypeStruct((J, n, N), out_dtype),
                    (J, (n // tm) * nn, K // tk), (tm, tn))


def _out_proj_dx(dh, x, g, dres, w, name):
    T, D = x.shape
    N = w.shape[0]
    tm = min(256, T)

    def body(dh_ref, x_ref, g_ref, dres_ref, w_ref, o_ref, dx_ref, dg_ref):
        @pl.when(pl.program_id(0) == 0)
        def _():
            dg_ref[...] = jnp.zeros_like(dg_ref)

        xv, dhv = x_ref[...], dh_ref[...]
        r = lax.rsqrt(jnp.mean(xv * xv, axis=-1, keepdims=True) + RMS_EPS)
        y = xv * r
        dy = dhv * g_ref[...]
        dx = dres_ref[...] + r * (dy - y * jnp.mean(dy * y, axis=-1, keepdims=True))
        dx_ref[...] = dx
        dg_ref[...] += jnp.sum(dhv * y, axis=0, keepdims=True)
        o_ref[...] = _nt(dx.astype(BF16), w_ref[...])

    row = pl.BlockSpec((tm, D), lambda i: (i, 0))
    vec = pl.BlockSpec((1, D), lambda i: (0, 0))
    return pl.pallas_call(
        body, name=name, grid=(T // tm,), in_specs=[row, row, vec, row, pl.BlockSpec((N, D), lambda i: (0, 0))],
        out_specs=[pl.BlockSpec((tm, N), lambda i: (i, 0)), row, vec],
        out_shape=[jax.ShapeDtypeStruct((T, N), F32), jax.ShapeDtypeStruct((T, D), F32), jax.ShapeDtypeStruct((1, D), F32)],
        compiler_params=_params(("arbitrary",)),
    )(dh, x, g.reshape(1, D), dres, w)


def _in_proj_dw(pieces, h, name):
    T, D = h.shape
    tm = 256
    nbs = [p.shape[1] // tm for p in pieces]
    los = [sum(nbs[:t]) for t in range(len(pieces))]

    def body(*refs):
        h_ref, o_ref = refs[-2:]
        i = pl.program_id(0)
        for p_ref, lo, nb in zip(refs[:-2], los, nbs):
            @pl.when((i >= lo) & (i < lo + nb))
            def _():
                o_ref[...] = _tn(p_ref[...], h_ref[...]).astype(BF16)

    specs = [pl.BlockSpec((T, tm), (lambda lo, nb: lambda i: (0, jnp.clip(i - lo, 0, nb - 1)))(lo, nb)) for lo, nb in zip(los, nbs)]
    return pl.pallas_call(body, name=name, grid=(sum(nbs),), in_specs=specs + [pl.BlockSpec((T, D), lambda i: (0, 0))],
                          out_specs=pl.BlockSpec((tm, D), lambda i: (i, 0)), out_shape=jax.ShapeDtypeStruct((sum(nbs) * tm, D), BF16),
                          compiler_params=_params(("parallel",)))(*pieces, h)


def _in_proj_dx(pieces, w, x, g, dres, name):
    T, D = x.shape
    K = w.shape[0]
    tm = min(256, T)
    n = len(pieces)
    widths = [p.shape[1] for p in pieces]
    offs = [sum(widths[:t]) for t in range(n)]

    def body(*refs):
        w_ref, x_ref, g_ref, dres_ref, dx_ref, dg_ref = refs[n:]

        @pl.when(pl.program_id(0) == 0)
        def _():
            dg_ref[...] = jnp.zeros_like(dg_ref)

        dh = None
        for p_ref, off, wd in zip(refs[:n], offs, widths):
            d = _nn(p_ref[...], w_ref[off:off + wd, :])
            dh = d if dh is None else dh + d
        xv = x_ref[...]
        r = lax.rsqrt(jnp.mean(xv * xv, axis=-1, keepdims=True) + RMS_EPS)
        y = xv * r
        dy = dh * g_ref[...]
        dx_ref[...] = dres_ref[...] + r * (dy - y * jnp.mean(dy * y, axis=-1, keepdims=True))
        dg_ref[...] += jnp.sum(dh * y, axis=0, keepdims=True)

    specs = [pl.BlockSpec((tm, wd), lambda i: (i, 0)) for wd in widths]
    row = pl.BlockSpec((tm, D), lambda i: (i, 0))
    vec = pl.BlockSpec((1, D), lambda i: (0, 0))
    return pl.pallas_call(body, name=name, grid=(T // tm,), in_specs=specs + [pl.BlockSpec((K, D), lambda i: (0, 0)), row, vec, row],
                          out_specs=[row, vec], out_shape=[jax.ShapeDtypeStruct((T, D), F32), jax.ShapeDtypeStruct((1, D), F32)],
                          compiler_params=_params(("arbitrary",)))(*pieces, w, x, g.reshape(1, D), dres)


def _seg_matrix(width):
    return jnp.asarray(np.kron(np.eye(width // HEAD_DIM, dtype=np.float32), np.full((HEAD_DIM, HEAD_DIM), 1.0 / HEAD_DIM, np.float32)), BF16)


def _segmean(v, p):
    hi = v.astype(BF16)
    r = v - hi.astype(F32)
    mid = r.astype(BF16)
    lo = (r - mid.astype(F32)).astype(BF16)
    w = min(256, v.shape[1])
    pw = p[:w, :w]
    halves = []
    for c in range(v.shape[1] // w):
        cols = slice(c * w, (c + 1) * w)
        halves.append(_nn(hi[:, cols], pw) + _nn(mid[:, cols], pw) + _nn(lo[:, cols], pw))
    return halves[0] if len(halves) == 1 else jnp.concatenate(halves, axis=1)


def _gelu(x):
    c0 = math.sqrt(2.0 / math.pi)
    t = jnp.tanh(c0 * (x + 0.044715 * x * x * x))
    return 0.5 * x * (1.0 + t), t


def _gelu_grad(x, t):
    c0 = math.sqrt(2.0 / math.pi)
    return 0.5 * (1.0 + t) + 0.5 * x * (1.0 - t * t) * c0 * (1.0 + 3.0 * 0.044715 * x * x)


def _sigmoid(x):
    return 1.0 / (1.0 + jnp.exp(-x))


def _rms_fwd(x, g, name):
    T, D = x.shape
    tm = min(256, T)

    def body(x_ref, g_ref, o_ref):
        o_ref[...] = _rms_rows(x_ref[...], g_ref[...])

    return pl.pallas_call(
        body, name=name, grid=(T // tm,),
        in_specs=[pl.BlockSpec((tm, D), lambda i: (i, 0)), pl.BlockSpec((1, D), lambda i: (0, 0))],
        out_specs=pl.BlockSpec((tm, D), lambda i: (i, 0)), out_shape=jax.ShapeDtypeStruct((T, D), BF16),
        compiler_params=_params(("parallel",)),
    )(x, g.reshape(1, D))


def _sgu_core(zu, zv, ws_ref, bias, p):
    ug, tu = _gelu(zu)
    vg, tv = _gelu(zv)
    xc = vg - _segmean(vg, p)
    rs = lax.rsqrt(_segmean(xc * xc, p) + LN_EPS)
    vn = xc * rs
    vnb = vn.astype(BF16)
    low = lax.broadcasted_iota(jnp.int32, (SGU_CHUNK, 128), 1) < HEAD_DIM
    parts = []
    for j in range(4):
        vp = vnb[:, 128 * j:128 * (j + 1)]
        parts.append(jnp.where(low, _nn(ws_ref[2 * j], vp), _nn(ws_ref[2 * j + 1], vp)))
    mixed = jnp.concatenate(parts, axis=1) + bias
    return ug, tu, tv, rs, vn, vnb, mixed, low


SGU_ROWS = 4 * SGU_CHUNK


def _sgu_fwd(z, ws, bias, name):
    T = z.shape[0]

    def body(zu_ref, zv_ref, ws_ref, b_ref, p_ref, y_ref):
        for r in range(0, SGU_ROWS, SGU_CHUNK):
            rows = slice(r, r + SGU_CHUNK)
            ug, _, _, _, _, _, mixed, _ = _sgu_core(zu_ref[rows, :], zv_ref[rows, :], ws_ref, b_ref[...], p_ref[...])
            y_ref[rows, :] = ug * mixed

    full = lambda shape: pl.BlockSpec(shape, lambda i: (0,) * len(shape))
    return pl.pallas_call(
        body, name=name, grid=(T // SGU_ROWS,),
        in_specs=[pl.BlockSpec((SGU_ROWS, 512), lambda i: (i, COL_AU)), pl.BlockSpec((SGU_ROWS, 512), lambda i: (i, COL_AV)),
                  full((8, 128, 128)), full((128, 512)), full((512, 512))],
        out_specs=pl.BlockSpec((SGU_ROWS, 512), lambda i: (i, 0)), out_shape=jax.ShapeDtypeStruct((T, 512), F32),
        compiler_params=_params(("parallel",)),
    )(z, z, ws, bias, _seg_matrix(512))


def _sgu_bwd(z, dy, ws, ws_t, bias, name):
    T = z.shape[0]

    def body(zu_ref, zv_ref, dy_ref, ws_ref, wst_ref, b_ref, p_ref, dz_ref, dws_ref, db_ref):
        @pl.when(pl.program_id(0) == 0)
        def _():
            dws_ref[...] = jnp.zeros_like(dws_ref)
            db_ref[...] = jnp.zeros_like(db_ref)

        p = p_ref[...]
        zero = jnp.zeros((SGU_CHUNK, 128), BF16)
        dws = [None] * 8
        db = None
        for r in range(0, SGU_ROWS, SGU_CHUNK):
            rows = slice(r, r + SGU_CHUNK)
            zu, zv = zu_ref[rows, :], zv_ref[rows, :]
            ug, tu, tv, rs, vn, vnb, mixed, low = _sgu_core(zu, zv, ws_ref, b_ref[...], p)
            dyv = dy_ref[rows, :]
            dmixed = dyv * ug
            db = dmixed if db is None else db + dmixed
            dmb = dmixed.astype(BF16)
            parts = []
            for j in range(4):
                dmp, vp = dmb[:, 128 * j:128 * (j + 1)], vnb[:, 128 * j:128 * (j + 1)]
                for g, d in ((2 * j, _nt(jnp.where(low, dmp, zero), vp)), (2 * j + 1, _nt(jnp.where(low, zero, dmp), vp))):
                    dws[g] = d if dws[g] is None else dws[g] + d
                parts.append(jnp.where(low, _nn(wst_ref[2 * j], dmp), _nn(wst_ref[2 * j + 1], dmp)))
            dvn = jnp.concatenate(parts, axis=1)
            dvg = rs * (dvn - _segmean(dvn, p) - vn * _segmean(dvn * vn, p))
            dz_ref[rows, 0:512] = (dyv * mixed * _gelu_grad(zu, tu)).astype(BF16)
            dz_ref[rows, 512:1024] = (dvg * _gelu_grad(zv, tv)).astype(BF16)
        db_ref[...] += db
        for g in range(8):
            dws_ref[g] += dws[g]

    full = lambda shape: pl.BlockSpec(shape, lambda i: (0,) * len(shape))
    return pl.pallas_call(
        body, name=name, grid=(T // SGU_ROWS,),
        in_specs=[pl.BlockSpec((SGU_ROWS, 512), lambda i: (i, COL_AU)), pl.BlockSpec((SGU_ROWS, 512), lambda i: (i, COL_AV)),
                  pl.BlockSpec((SGU_ROWS, 512), lambda i: (i, 0)), full((8, 128, 128)), full((8, 128, 128)), full((128, 512)), full((512, 512))],
        out_specs=[pl.BlockSpec((SGU_ROWS, 1024), lambda i: (i, 0)), full((8, 128, 128)), full((128, 512))],
        out_shape=[jax.ShapeDtypeStruct((T, 1024), BF16), jax.ShapeDtypeStruct((8, 128, 128), F32), jax.ShapeDtypeStruct((128, 512), F32)],
        compiler_params=_params(("arbitrary",)),
    )(z, z, dy, ws, ws_t, bias, _seg_matrix(512))


CONV_ROWS = 256


def _conv_taps(pad_ref, w_ref, base, flip):
    acc = None
    for k in range(CONV_WIDTH):
        wk = w_ref[CONV_WIDTH - 1 - k if flip else k]
        t = wk * pad_ref[base + k + 1:base + k + 1 + CONV_ROWS, :]
        acc = t if acc is None else acc + t
    return acc


def _conv_fwd1(z, w, cb, B, S, name):
    T = B * S
    rows = min(CONV_ROWS, S)
    assert rows == CONV_ROWS

    def body(a_ref, g_ref, w_ref, cb_ref, c_ref, pad):
        pad[0:CONV_PAD, :] = jnp.zeros((CONV_PAD, 128), F32)
        pad[CONV_PAD + S:2 * CONV_PAD + S, :] = jnp.zeros((CONV_PAD, 128), F32)
        pad[CONV_PAD:CONV_PAD + S, :] = a_ref[...] * _sigmoid(g_ref[...])

        for base in range(0, S, CONV_ROWS):
            c_ref[base:base + CONV_ROWS, :] = _conv_taps(pad, w_ref, base, False) + cb_ref[...]

    return pl.pallas_call(
        body, name=name, grid=(4, B),
        in_specs=[pl.BlockSpec((S, 128), lambda j, b: (b, 4 * COL_CA + j)), pl.BlockSpec((S, 128), lambda j, b: (b, 4 * COL_CG + j)),
                  pl.BlockSpec((32, 1, 128), lambda j, b: (0, 0, j)), pl.BlockSpec((1, 128), lambda j, b: (0, j))],
        out_specs=pl.BlockSpec((S, 128), lambda j, b: (b, j)), out_shape=jax.ShapeDtypeStruct((T, 512), F32),
        scratch_shapes=[pltpu.VMEM((S + 2 * CONV_PAD, 128), F32)], compiler_params=_params(("parallel", "parallel")),
    )(z, z, w, cb)


def _ln_rows(c):
    mu = jnp.mean(c, axis=-1, keepdims=True)
    xc = c - mu
    rs = lax.rsqrt(jnp.mean(xc * xc, axis=-1, keepdims=True) + LN_EPS)
    return xc * rs, rs


def _conv_bwd2(z, dc, w, B, S, name):
    T = B * S

    def body(a_ref, g_ref, dc_ref, w_ref, da_ref, dg_ref, dw_ref, hpad, dpad):
        @pl.when(pl.program_id(1) == 0)
        def _():
            dw_ref[...] = jnp.zeros_like(dw_ref)

        zeros = jnp.zeros((CONV_PAD, 128), F32)
        for ref in (hpad, dpad):
            ref[0:CONV_PAD, :] = zeros
            ref[CONV_PAD + S:2 * CONV_PAD + S, :] = zeros
        hpad[CONV_PAD:CONV_PAD + S, :] = a_ref[...] * _sigmoid(g_ref[...])
        dpad[CONV_PAD:CONV_PAD + S, :] = dc_ref[...]
        dws = [None] * CONV_WIDTH
        for base in range(0, S, CONV_ROWS):
            rows = slice(base, base + CONV_ROWS)
            dh = _conv_taps(dpad, w_ref, base, True)
            sg = _sigmoid(g_ref[rows, :])
            da_ref[rows, :] = (dh * sg).astype(BF16)
            dg_ref[rows, :] = (dh * a_ref[rows, :] * sg * (1.0 - sg)).astype(BF16)
            dcv = dc_ref[rows, :]
            for k in range(CONV_WIDTH):
                prod = dcv * hpad[base + k + 1:base + k + 1 + CONV_ROWS, :]
                part = jnp.sum(prod.reshape(CONV_ROWS // 8, 8, 128), axis=0)
                dws[k] = part if dws[k] is None else dws[k] + part
        for k in range(CONV_WIDTH):
            dw_ref[k] += jnp.sum(dws[k], axis=0, keepdims=True)

    return pl.pallas_call(
        body, name=name, grid=(4, B),
        in_specs=[pl.BlockSpec((S, 128), lambda j, b: (b, 4 * COL_CA + j)), pl.BlockSpec((S, 128), lambda j, b: (b, 4 * COL_CG + j)),
                  pl.BlockSpec((S, 128), lambda j, b: (b, j)), pl.BlockSpec((32, 1, 128), lambda j, b: (0, 0, j))],
        out_specs=[pl.BlockSpec((S, 128), lambda j, b: (b, j)), pl.BlockSpec((S, 128), lambda j, b: (b, j)),
                   pl.BlockSpec((32, 1, 128), lambda j, b: (0, 0, j))],
        out_shape=[jax.ShapeDtypeStruct((T, 512), BF16), jax.ShapeDtypeStruct((T, 512), BF16), jax.ShapeDtypeStruct((32, 1, 512), F32)],
        scratch_shapes=[pltpu.VMEM((S + 2 * CONV_PAD, 128), F32), pltpu.VMEM((S + 2 * CONV_PAD, 128), F32)],
        compiler_params=_params(("parallel", "arbitrary")),
    )(z, z, dc, w)


def _swap16(x):
    n = x.shape[1]
    first = (lax.broadcasted_iota(jnp.int32, x.shape, 1) % 32) < 16
    return jnp.where(first, pltpu.roll(x, n - 16, 1), pltpu.roll(x, 16, 1))


def _rope(x, cos, sin):
    return x * cos + _swap16(x) * sin


def _rope_t(dy, cos, sin):
    return dy * cos + _swap16(dy * sin)


def _qk_norm(x, p):
    r = lax.rsqrt(_segmean(x * x, p) + RMS_EPS)
    return x * r, r


def _store_heads(ref, val, n):
    for h in range(n):
        ref[h] = val[:, HEAD_DIM * h:HEAD_DIM * (h + 1)].astype(ref.dtype)


def _load_heads(ref, n):
    return jnp.concatenate([ref[h] for h in range(n)], axis=1)


def _prep_fwd(z, gq, gk, rope, B, S, kv_heads, cols, name):
    tm = min(ROW_TILE, S)
    ns = S // tm
    kw = kv_heads * HEAD_DIM
    scale = HEAD_DIM ** -0.5
    qc, kc, vc = cols

    def body(*refs):
        if rope is None:
            q_ref, k_ref, v_ref, gq_ref, gk_ref, p_ref, qo, ko, vo = refs
        else:
            q_ref, k_ref, v_ref, gq_ref, gk_ref, p_ref, cos_ref, sin_ref, qo, ko, vo = refs
        p = p_ref[...]
        qn, _ = _qk_norm(q_ref[...], p)
        kn, _ = _qk_norm(k_ref[...], p[:kw, :kw])
        qn, kn = qn * gq_ref[...], kn * gk_ref[...]
        if rope is not None:
            cos, sin = cos_ref[...], sin_ref[...]
            qn, kn = _rope(qn, cos, sin), _rope(kn, cos[:, :kw], sin[:, :kw])
        _store_heads(qo, qn * scale, N_HEADS)
        _store_heads(ko, kn, kv_heads)
        _store_heads(vo, v_ref[...], kv_heads)

    row = lambda w, c: pl.BlockSpec((tm, w), lambda b, i: (b * ns + i, c))
    const = lambda shape: pl.BlockSpec(shape, lambda b, i: (0,) * len(shape))
    heads = lambda n: pl.BlockSpec((None, n, tm, HEAD_DIM), lambda b, i: (b, 0, i, 0))
    ins = [z, z, z, gq, gk, _seg_matrix(512)]
    specs = [row(512, qc), row(kw, kc), row(kw, vc), const((1, 512)), const((1, kw)), const((512, 512))]
    if rope is not None:
        ins += list(rope)
        specs += [pl.BlockSpec((tm, 512), lambda b, i: (i, 0))] * 2
    return pl.pallas_call(
        body, name=name, grid=(B, ns), in_specs=specs, out_specs=[heads(N_HEADS), heads(kv_heads), heads(kv_heads)],
        out_shape=[jax.ShapeDtypeStruct((B, N_HEADS, S, HEAD_DIM), BF16), jax.ShapeDtypeStruct((B, kv_heads, S, HEAD_DIM), BF16),
                   jax.ShapeDtypeStruct((B, kv_heads, S, HEAD_DIM), BF16)],
        compiler_params=_params(("parallel", "parallel")),
    )(*ins)


def _prep_bwd(z, dq, dk, dv, gq, gk, rope, B, S, kv_heads, cols, name):
    T = B * S
    tm = min(ROW_TILE, S)
    ns = S // tm
    kw = kv_heads * HEAD_DIM
    scale = HEAD_DIM ** -0.5
    qc, kc, _ = cols

    def body(*refs):
        if rope is None:
            q_ref, k_ref, dq_ref, dk_ref, dv_ref, gq_ref, gk_ref, p_ref, dz_ref, dgq_ref, dgk_ref = refs
        else:
            q_ref, k_ref, dq_ref, dk_ref, dv_ref, gq_ref, gk_ref, p_ref, cos_ref, sin_ref, dz_ref, dgq_ref, dgk_ref = refs

        @pl.when((pl.program_id(0) == 0) & (pl.program_id(1) == 0))
        def _():
            dgq_ref[...] = jnp.zeros_like(dgq_ref)
            dgk_ref[...] = jnp.zeros_like(dgk_ref)

        p = p_ref[...]
        dqv = _load_heads(dq_ref, N_HEADS) * scale
        dkv = _load_heads(dk_ref, kv_heads)
        if rope is not None:
            cos, sin = cos_ref[...], sin_ref[...]
            dqv, dkv = _rope_t(dqv, cos, sin), _rope_t(dkv, cos[:, :kw], sin[:, :kw])

        def through_norm(xv, dy, g, pm, dg_ref):
            xh, r = _qk_norm(xv, pm)
            dg_ref[...] += jnp.sum(dy * xh, axis=0, keepdims=True)
            dxh = dy * g
            return r * (dxh - xh * _segmean(dxh * xh, pm))

        dz_ref[:, 0:512] = through_norm(q_ref[...], dqv, gq_ref[...], p, dgq_ref).astype(BF16)
        dz_ref[:, 512:512 + kw] = through_norm(k_ref[...], dkv, gk_ref[...], p[:kw, :kw], dgk_ref).astype(BF16)
        dz_ref[:, 512 + kw:512 + 2 * kw] = _load_heads(dv_ref, kv_heads).astype(BF16)

    row = lambda w, c: pl.BlockSpec((tm, w), lambda b, i: (b * ns + i, c))
    const = lambda shape: pl.BlockSpec(shape, lambda b, i: (0,) * len(shape))
    heads = lambda n: pl.BlockSpec((None, n, tm, HEAD_DIM), lambda b, i: (b, 0, i, 0))
    ins = [z, z, dq, dk, dv, gq, gk, _seg_matrix(512)]
    specs = [row(512, qc), row(kw, kc), heads(N_HEADS), heads(kv_heads), heads(kv_heads), const((1, 512)), const((1, kw)), const((512, 512))]
    if rope is not None:
        ins += list(rope)
        specs += [pl.BlockSpec((tm, 512), lambda b, i: (i, 0))] * 2
    return pl.pallas_call(
        body, name=name, grid=(B, ns), in_specs=specs, out_specs=[row(512 + 2 * kw, 0), const((1, 512)), const((1, kw))],
        out_shape=[jax.ShapeDtypeStruct((T, 512 + 2 * kw), BF16), jax.ShapeDtypeStruct((1, 512), F32), jax.ShapeDtypeStruct((1, kw), F32)],
        compiler_params=_params(("arbitrary", "arbitrary")),
    )(*ins)


def _toeplitz(win, tq, S):
    r = pltpu.roll(jnp.broadcast_to(win, (tq, S + tq)), 0, 1, stride=1, stride_axis=0)
    return r[:, tq:tq + S]


ATTN_HEADS = 4


def _attn_fwd(q, k, v, win, name, nh=ATTN_HEADS, tq=ATTN_TQ):
    B, H, S, _ = q.shape
    shared = k.shape[1] != H
    assert not shared or H // k.shape[1] == nh
    tq = min(tq, S)

    def body(*refs):
        if win is None:
            q_ref, k_ref, v_ref, o_ref = refs
        else:
            q_ref, k_ref, v_ref, w_ref, o_ref = refs
        kvs = [(k_ref[...], v_ref[...])] * nh if shared else [(k_ref[h], v_ref[h]) for h in range(nh)]
        scores = []
        for h in range(nh):
            s = _nt(q_ref[h], kvs[h][0])
            if win is not None:
                s = s + _toeplitz(w_ref[h], tq, S)
            scores.append(s)
        probs = []
        for s in scores:
            p = jnp.exp(s - jnp.max(s, axis=-1, keepdims=True))
            probs.append((p.astype(BF16), jnp.sum(p, axis=-1, keepdims=True)))
        for h, (p, l) in enumerate(probs):
            o_ref[h] = _nn(p, kvs[h][1]) / l

    qs = pl.BlockSpec((None, nh, tq, HEAD_DIM), lambda b, h, i: (b, h, i, 0))
    ks = (pl.BlockSpec((None, None, S, HEAD_DIM), lambda b, h, i: (b, h, 0, 0)) if shared
          else pl.BlockSpec((None, nh, S, HEAD_DIM), lambda b, h, i: (b, h, 0, 0)))
    ins, specs = [q, k, v], [qs, ks, ks]
    if win is not None:
        ins.append(win)
        specs.append(pl.BlockSpec((nh, None, 1, S + tq), lambda b, h, i: (h, i, 0, 0)))
    return pl.pallas_call(body, name=name, grid=(B, H // nh, S // tq), in_specs=specs, out_specs=qs,
                          out_shape=jax.ShapeDtypeStruct((B, H, S, HEAD_DIM), F32),
                          compiler_params=_params(("parallel", "parallel", "parallel")))(*ins)


def _attn_bwd(q, k, v, o, do, win, name, nh=ATTN_HEADS, tq=ATTN_TQ):
    B, H, S, _ = q.shape
    hkv = k.shape[1]
    shared = hkv != H
    assert not shared or H // hkv == nh
    tq = min(tq, S)
    nq = S // tq

    def body(*refs):
        if win is None:
            q_ref, k_ref, v_ref, o_ref, do_ref, dq_ref, dk_ref, dv_ref = refs
        else:
            q_ref, k_ref, v_ref, o_ref, do_ref, w_ref, rev_ref, dq_ref, dk_ref, dv_ref, dw_ref = refs

        @pl.when(pl.program_id(2) == 0)
        def _():
            dk_ref[...] = jnp.zeros_like(dk_ref)
            dv_ref[...] = jnp.zeros_like(dv_ref)

        kvs = [(k_ref[...], v_ref[...])] * nh if shared else [(k_ref[h], v_ref[h]) for h in range(nh)]
        qvs, dobs, scores, dps = [], [], [], []
        for h in range(nh):
            qv, dov = q_ref[h], do_ref[h]
            dob = dov.astype(BF16)
            s = _nt(qv, kvs[h][0])
            if win is not None:
                s = s + _toeplitz(w_ref[h], tq, S)
            dp = _nt(dob, kvs[h][1]) - jnp.sum(dov * o_ref[h], axis=-1, keepdims=True)
            qvs.append(qv)
            dobs.append(dob)
            scores.append(s)
            dps.append(dp)
        pbs, dsbs = [], []
        for s, dp in zip(scores, dps):
            p = jnp.exp(s - jnp.max(s, axis=-1, keepdims=True))
            p = p * (1.0 / jnp.sum(p, axis=-1, keepdims=True))
            pbs.append(p.astype(BF16))
            dsbs.append((p * dp).astype(BF16))
        dk_acc = dv_acc = None
        for h in range(nh):
            dvh, dkh = _tn(pbs[h], dobs[h]), _tn(dsbs[h], qvs[h])
            dq_ref[h] = _nn(dsbs[h], kvs[h][0])
            if shared:
                dv_acc = dvh if dv_acc is None else dv_acc + dvh
                dk_acc = dkh if dk_acc is None else dk_acc + dkh
            else:
                dv_ref[h] += dvh
                dk_ref[h] += dkh
            if win is not None:
                rev = _nn(rev_ref[...], dsbs[h])
                half = tq // 2
                top = jnp.concatenate([rev[:half], jnp.zeros((half, tq), F32)], axis=1)
                bot = jnp.concatenate([jnp.zeros((half, half), F32), rev[half:], jnp.zeros((half, half), F32)], axis=1)
                dw_ref[h] = jnp.sum(pltpu.roll(top + bot, 0, 1, stride=1, stride_axis=0), axis=0, keepdims=True)
        if shared:
            dv_ref[...] += dv_acc
            dk_ref[...] += dk_acc

    qs = pl.BlockSpec((None, nh, tq, HEAD_DIM), lambda b, h, i: (b, h, i, 0))
    ks = (pl.BlockSpec((None, None, S, HEAD_DIM), lambda b, h, i: (b, h, 0, 0)) if shared
          else pl.BlockSpec((None, nh, S, HEAD_DIM), lambda b, h, i: (b, h, 0, 0)))
    ins, specs = [q, k, v, o, do], [qs, ks, ks, qs, qs]
    outs = [jax.ShapeDtypeStruct((B, H, S, HEAD_DIM), F32), jax.ShapeDtypeStruct((B, hkv, S, HEAD_DIM), F32), jax.ShapeDtypeStruct((B, hkv, S, HEAD_DIM), F32)]
    ospecs = [qs, ks, ks]
    if win is not None:
        ins += [win, jnp.asarray(np.eye(tq, dtype=np.float32)[::-1].copy(), BF16)]
        specs += [pl.BlockSpec((nh, None, 1, S + tq), lambda b, h, i: (h, i, 0, 0)), pl.BlockSpec((tq, tq), lambda b, h, i: (0, 0))]
        outs.append(jax.ShapeDtypeStruct((B, H, nq, 1, S + tq), F32))
        ospecs.append(pl.BlockSpec((None, nh, None, 1, S + tq), lambda b, h, i: (b, h, i, 0, 0)))
    return pl.pallas_call(body, name=name, grid=(B, H // nh, nq), in_specs=specs, out_specs=ospecs, out_shape=outs,
                          compiler_params=_params(("parallel", "parallel", "arbitrary")))(*ins)


def _pattern_count(delta):
    n = jnp.zeros(delta.shape, jnp.int32)
    for window, dil in DIL_PATTERNS:
        n = n + ((delta % dil == 0) & (jnp.abs(delta) <= window // 2)).astype(jnp.int32)
    return n


def _t5_bucket(rel):
    nb = REL_BUCKETS // 2
    max_exact = nb // 2
    ret = jnp.where(rel > 0, nb, 0)
    n = jnp.abs(rel)
    nf = jnp.maximum(n, 1).astype(F32)
    large = max_exact + (jnp.log(nf / max_exact) / math.log(REL_MAX_DIST / max_exact) * (nb - max_exact)).astype(jnp.int32)
    large = jnp.minimum(large, nb - 1)
    return ret + jnp.where(n < max_exact, n, large)


def _bias_windows(rel_bias, S):
    tq = min(ATTN_TQ, S)
    nq = S // tq
    n = nq * (S + tq)
    delta = (jnp.arange(S + tq)[None, :] - (jnp.arange(nq)[:, None] + 1) * tq).reshape(n)
    count = _pattern_count(delta)
    onehot = (_t5_bucket(delta)[None, :] == jnp.arange(REL_BUCKETS)[:, None]).astype(F32)
    extra = jnp.where(count > 0, jnp.log(jnp.maximum(count, 1).astype(F32)), MASKED).reshape(1, n)
    live = (count > 0).astype(F32).reshape(1, n)

    def body(t_ref, oh_ref, live_ref, extra_ref, o_ref):
        o_ref[...] = _nn(t_ref[...], oh_ref[...], HIGHEST) * live_ref[...] + extra_ref[...]

    val = pl.pallas_call(body, name="bias_windows", out_shape=jax.ShapeDtypeStruct((N_HEADS, n), F32),
                         compiler_params=_params())(rel_bias.T, onehot, live, extra)
    return val.reshape(N_HEADS, nq, 1, S + tq)


def _bias_fold(dwin, S, name):
    B, H, nq = dwin.shape[:3]
    tq = min(ATTN_TQ, S)
    n = nq * (S + tq)
    delta = (jnp.arange(S + tq)[None, :] - (tq - 1) - jnp.arange(nq)[:, None] * tq).reshape(n)
    onehot = (_t5_bucket(delta)[:, None] == jnp.arange(128)[None, :]).astype(F32)

    def body(d_ref, oh_ref, o_ref):
        tot = d_ref[0]
        for b in range(1, B):
            tot = tot + d_ref[b]
        o_ref[...] = _nn(tot, oh_ref[...], HIGHEST)

    out = pl.pallas_call(body, name=name, out_shape=jax.ShapeDtypeStruct((H, 128), F32), compiler_params=_params())(dwin.reshape(B, H, n), onehot)
    return out[:, :REL_BUCKETS].T


def _rope_tables(S):
    half = 16
    freqs = ROPE_THETA ** (-jnp.arange(half, dtype=F32) / half)
    t = jnp.arange(S)
    ang_r = (t // GRID_W).astype(F32)[:, None] * freqs[None, :]
    ang_c = (t % GRID_W).astype(F32)[:, None] * freqs[None, :]
    cos = jnp.concatenate([jnp.cos(ang_r)] * 2 + [jnp.cos(ang_c)] * 2, axis=1)
    sin = jnp.concatenate([-jnp.sin(ang_r), jnp.sin(ang_r), -jnp.sin(ang_c), jnp.sin(ang_c)], axis=1)
    return jnp.tile(cos, (1, N_HEADS)), jnp.tile(sin, (1, N_HEADS))


def _conv_act(c, g, b):
    n, rs = _ln_rows(c)
    t = n * g + b
    return t * _sigmoid(t), n, rs, t


def _mix_fwd(ya, ob, c, od, gain, lng, lnb, B, S, name):
    T = B * S
    tm = min(ROW_TILE, S)
    ns = S // tm

    def body(ya_ref, ob_ref, c_ref, od_ref, g_ref, lg_ref, lb_ref, o_ref):
        yc = _conv_act(c_ref[...], lg_ref[...], lb_ref[...])[0]
        ys = [ya_ref[...], _load_heads(ob_ref, N_HEADS), yc, _load_heads(od_ref, N_HEADS)]
        for m, y in enumerate(ys):
            r = lax.rsqrt(jnp.mean(y * y, axis=-1, keepdims=True) + RMS_EPS)
            o_ref[:, 512 * m:512 * (m + 1)] = (y * r * g_ref[:, 512 * m:512 * (m + 1)]).astype(BF16)

    row = pl.BlockSpec((tm, 512), lambda b, i: (b * ns + i, 0))
    heads = pl.BlockSpec((None, N_HEADS, tm, HEAD_DIM), lambda b, i: (b, 0, i, 0))
    vec = pl.BlockSpec((1, 512), lambda b, i: (0, 0))
    return pl.pallas_call(
        body, name=name, grid=(B, ns), in_specs=[row, heads, row, heads, pl.BlockSpec((1, 2048), lambda b, i: (0, 0)), vec, vec],
        out_specs=pl.BlockSpec((tm, 2048), lambda b, i: (b * ns + i, 0)), out_shape=jax.ShapeDtypeStruct((T, 2048), BF16),
        compiler_params=_params(("parallel", "parallel")),
    )(ya, ob, c, od, gain, lng, lnb)


def _mix_bwd(ya, ob, c, od, dycat, gain, lng, lnb, B, S, name):
    T = B * S
    tm = min(ROW_TILE, S)
    ns = S // tm

    def body(ya_ref, ob_ref, c_ref, od_ref, dy_ref, g_ref, lg_ref, lb_ref, dya_ref, dob_ref, dc_ref, dod_ref, dg_ref, dlg_ref, dlb_ref, dcb_ref):
        @pl.when((pl.program_id(0) == 0) & (pl.program_id(1) == 0))
        def _():
            for ref in (dg_ref, dlg_ref, dlb_ref, dcb_ref):
                ref[...] = jnp.zeros_like(ref)

        yc, n, rs, t = _conv_act(c_ref[...], lg_ref[...], lb_ref[...])
        ys = [ya_ref[...], _load_heads(ob_ref, N_HEADS), yc, _load_heads(od_ref, N_HEADS)]
        outs = [dya_ref, dob_ref, None, dod_ref]
        for m, y in enumerate(ys):
            cols = slice(512 * m, 512 * (m + 1))
            r = lax.rsqrt(jnp.mean(y * y, axis=-1, keepdims=True) + RMS_EPS)
            yh = y * r
            dh = dy_ref[:, cols]
            dg_ref[:, cols] += jnp.sum(dh * yh, axis=0, keepdims=True)
            dyh = dh * g_ref[:, cols]
            dyv = r * (dyh - yh * jnp.mean(dyh * yh, axis=-1, keepdims=True))
            if m == 0:
                outs[m][...] = dyv
            elif m == 2:
                sg = _sigmoid(t)
                dt = dyv * sg * (1.0 + t * (1.0 - sg))
                dlg_ref[...] += jnp.sum(dt * n, axis=0, keepdims=True)
                dlb_ref[...] += jnp.sum(dt, axis=0, keepdims=True)
                dn = dt * lg_ref[...]
                dc = rs * (dn - jnp.mean(dn, axis=-1, keepdims=True) - n * jnp.mean(dn * n, axis=-1, keepdims=True))
                dc_ref[...] = dc
                dcb_ref[...] += jnp.sum(dc, axis=0, keepdims=True)
            else:
                _store_heads(outs[m], dyv, N_HEADS)

    row = pl.BlockSpec((tm, 512), lambda b, i: (b * ns + i, 0))
    heads = pl.BlockSpec((None, N_HEADS, tm, HEAD_DIM), lambda b, i: (b, 0, i, 0))
    vec = pl.BlockSpec((1, 2048), lambda b, i: (0, 0))
    flat = jax.ShapeDtypeStruct((T, 512), F32)
    hm = jax.ShapeDtypeStruct((B, N_HEADS, S, HEAD_DIM), F32)
    v512 = pl.BlockSpec((1, 512), lambda b, i: (0, 0))
    s512 = jax.ShapeDtypeStruct((1, 512), F32)
    return pl.pallas_call(
        body, name=name, grid=(B, ns), in_specs=[row, heads, row, heads, pl.BlockSpec((tm, 2048), lambda b, i: (b * ns + i, 0)), vec, v512, v512],
        out_specs=[row, heads, row, heads, vec, v512, v512, v512],
        out_shape=[flat, hm, flat, hm, jax.ShapeDtypeStruct((1, 2048), F32), s512, s512, s512],
        compiler_params=_params(("arbitrary", "arbitrary")),
    )(ya, ob, c, od, dycat, gain, lng, lnb)


def _ffn_down(gate, up, w_down, res, name, norm=None, target=None):
    J, T, n = gate.shape
    N = w_down.shape[2]
    tm = min(256, T)
    steps = T // tm

    def body(g_ref, u_ref, w_ref, r_ref, *rest):
        x_ref = rest[0] if (norm is not None or target is not None) else None
        o_ref, act_ref = rest[-3:-1] if x_ref is not None else rest[-2:]
        acc = None
        for j in range(J):
            g = g_ref[j].astype(F32)
            a = (g * _sigmoid(g) * u_ref[j].astype(F32)).astype(BF16)
            act_ref[j] = a
            d = _nn(a, w_ref[j])
            acc = d if acc is None else acc + d
        y = acc + r_ref[...]
        if target is None:
            o_ref[...] = y
            if norm is not None:
                rest[-1][...] = _rms_rows(y, x_ref[...])
            return
        err = y - x_ref[...]
        o_ref[...] = err * (1.0 / N)
        loss_ref, i = rest[-1], pl.program_id(0)

        @pl.when(i == 0)
        def _():
            loss_ref[...] = jnp.zeros_like(loss_ref)

        loss_ref[...] += jnp.sum(err * err)

        @pl.when(i == steps - 1)
        def _():
            loss_ref[...] = loss_ref[...] * (0.5 / N)

    gu = pl.BlockSpec((J, tm, n), lambda i: (0, i, 0))
    row = pl.BlockSpec((tm, N), lambda i: (i, 0))
    ins, specs = [gate, up, w_down, res], [gu, gu, pl.BlockSpec((J, n, N), lambda i: (0, 0, 0)), row]
    outs, ospecs = [jax.ShapeDtypeStruct((T, N), F32), jax.ShapeDtypeStruct((J, T, n), BF16)], [row, gu]
    if target is not None:
        ins, specs = ins + [target], specs + [row]
        outs, ospecs = outs + [jax.ShapeDtypeStruct((8, 128), F32)], ospecs + [pl.BlockSpec((8, 128), lambda i: (0, 0))]
    elif norm is not None:
        ins, specs = ins + [norm.reshape(1, N)], specs + [pl.BlockSpec((1, N), lambda i: (0, 0))]
        outs, ospecs = outs + [jax.ShapeDtypeStruct((T, N), BF16)], ospecs + [row]
    return pl.pallas_call(body, name=name, grid=(steps,), in_specs=specs, out_specs=ospecs, out_shape=outs,
                          compiler_params=_params(("arbitrary" if target is not None else "parallel",)))(*ins)


def _ffn_down_dx(dx, w_down, gate, up, name):
    J, n, D = w_down.shape
    T = dx.shape[0]
    tm = min(512, T)

    def body(dx_ref, w_ref, g_ref, u_ref, dg_ref, du_ref):
        d = _nt(dx_ref[...].astype(BF16), w_ref[...])
        g = g_ref[...].astype(F32)
        s = _sigmoid(g)
        dg_ref[...] = (d * u_ref[...].astype(F32) * s * (1.0 + g * (1.0 - s))).astype(BF16)
        du_ref[...] = (d * g * s).astype(BF16)

    blk = pl.BlockSpec((None, tm, n), lambda j, i: (j, i, 0))
    shape = jax.ShapeDtypeStruct((J, T, n), BF16)
    return pl.pallas_call(body, name=name, grid=(J, T // tm),
                          in_specs=[pl.BlockSpec((tm, D), lambda j, i: (i, 0)), pl.BlockSpec((None, n, D), lambda j, i: (j, 0, 0)), blk, blk],
                          out_specs=[blk, blk], out_shape=[shape, shape], compiler_params=_params(("parallel", "parallel")))(dx, w_down, gate, up)


def _row_tile(R):
    best = R
    for cand in range(16, min(R, 272) + 1, 16):
        if R % cand == 0:
            best = cand
    return best


def _adamw(w, m, v, stack, name, layer=None, prev=None):
    n, R, C = stack.shape
    tm = _row_tile(R)
    nb = R // tm
    off = 0 if layer is None else layer * nb
    c1 = 1.0 - ADAM_B1 ** ADAM_STEP
    c2 = 1.0 - ADAM_B2 ** ADAM_STEP

    def body(w_ref, m_ref, v_ref, s_ref, *rest):
        g_ref, d_ref, mo_ref, vo_ref = rest[-4:]
        g = s_ref[0].astype(F32)
        for k in range(1, n):
            g = g + s_ref[k].astype(F32)
        mn = ADAM_B1 * m_ref[...] + (1.0 - ADAM_B1) * g
        vn = ADAM_B2 * v_ref[...] + (1.0 - ADAM_B2) * (g * g)
        g_ref[...] = g
        mo_ref[...] = mn
        vo_ref[...] = vn
        d_ref[...] = -ADAM_LR * ((mn / c1) / (jnp.sqrt(vn / c2) + ADAM_EPS) + ADAM_WD * w_ref[...])

    blk = pl.BlockSpec((tm, C), lambda i: (i + off, 0))
    ins = [w, m, v, stack]
    specs = [blk, blk, blk, pl.BlockSpec((n, tm, C), lambda i: (0, i, 0))]
    aliases = {}
    if prev is not None:
        ins += list(prev)
        specs += [pl.BlockSpec(memory_space=pl.ANY)] * 4
        aliases = {4 + t: t for t in range(4)}
    shape = jax.ShapeDtypeStruct(w.shape, F32)
    return pl.pallas_call(body, name=name, grid=(nb,), in_specs=specs, out_specs=[blk] * 4, out_shape=[shape] * 4,
                          input_output_aliases=aliases, compiler_params=_params(("parallel",)))(*ins)


HBM = pl.BlockSpec(memory_space=pltpu.HBM)
SEM = pl.BlockSpec(memory_space=pltpu.SEMAPHORE)
EFFECT = pltpu.SideEffectType.DATAFLOW_SIDE_EFFECTING


PEERS = {"scatter": (1, 2, 3, 4, 5, 6, 7), "gather": (1, 2, 3, 4, 5, 6, 7), "chips": (1, 2, 4, 6), "forward": (2, 4, 6)}


def _spread_copies(srcs, lands, send_sems, recv_sems, local_sems, kind, waiting):
    x, y, c = lax.axis_index("x"), lax.axis_index("y"), lax.axis_index("c")
    me = 4 * x + 2 * y + c

    def peer(bits):
        dev = (1 - x if bits & 4 else x, 1 - y if bits & 2 else y, 1 - c if bits & 1 else c)
        return dev, 4 * dev[0] + 2 * dev[1] + dev[2]

    plan = PEERS[kind]
    remote, local = [], []
    for a, l in enumerate(lands):
        for d, bits in enumerate(plan):
            dev, pid = peer(bits)
            if kind == "forward":
                src, dst, dev = l.at[pid], l.at[peer(bits | 1)[1] if waiting else pid], peer(1)[0]
            else:
                src, dst = (srcs[a].at[pid] if kind == "scatter" else srcs[a]), l.at[pid if waiting else me]
            remote.append(pltpu.make_async_remote_copy(
                src_ref=src, dst_ref=dst, send_sem=send_sems.at[a * len(plan) + d], recv_sem=recv_sems.at[a * len(plan) + d],
                device_id=dev, device_id_type=MESH_ID))
        if kind != "forward":
            local.append(pltpu.make_async_copy(srcs[a].at[me] if kind == "scatter" else srcs[a], l.at[me], local_sems.at[a]))
    return remote, local


def _spread_start(srcs, kind, name, after=None, lands=None):
    if kind == "forward":
        srcs = []
    else:
        shapes = [a.shape if kind == "scatter" else (N_DEV,) + a.shape for a in srcs]
        lands = [lax.empty(shp, a.dtype) for shp, a in zip(shapes, srcs)]
    ns, nl, per = len(srcs), len(lands), len(PEERS[kind])
    extra = [] if after is None else [after]
    sem_shapes = [pltpu.SemaphoreType.DMA((nl * per,))] * 2 + ([pltpu.SemaphoreType.DMA((nl,))] if ns else [])

    def body(*refs):
        src_refs, land_refs = refs[:ns], refs[ns:ns + nl]
        sems = refs[ns + nl + len(extra):ns + nl + len(extra) + len(sem_shapes)]
        remote, local = _spread_copies(src_refs, land_refs, sems[0], sems[1], sems[2] if ns else None, kind, False)
        for cp in remote + local:
            cp.start()
        refs[-1][...] = jnp.zeros((8, 128), F32)

    outs = pl.pallas_call(
        body, name=name,
        out_shape=(*sem_shapes, *[pltpu.HBM(a.shape, a.dtype) for a in srcs + lands], jax.ShapeDtypeStruct((8, 128), F32)),
        in_specs=[HBM] * (ns + nl) + [pl.BlockSpec(memory_space=pl.ANY)] * len(extra),
        out_specs=(*[SEM] * len(sem_shapes), *[HBM] * (ns + nl), pl.BlockSpec(memory_space=pltpu.VMEM)),
        input_output_aliases={i: len(sem_shapes) + i for i in range(ns + nl)},
        compiler_params=pltpu.CompilerParams(has_side_effects=EFFECT),
    )(*[pltpu.with_memory_space_constraint(a, pltpu.HBM) for a in srcs + lands], *extra)
    k = len(sem_shapes)
    return outs[:k], list(outs[k:k + ns]), list(outs[k + ns:k + ns + nl]), outs[-1]


def _spread_wait(sems, srcs, lands, after, kind, name):
    ns, nl = len(srcs), len(lands)
    after = list(after) if isinstance(after, (list, tuple)) else [after]

    def body(*refs):
        src_refs, land_refs = refs[:ns], refs[ns:ns + nl]
        s = refs[ns + nl:ns + nl + len(sems)]
        remote, local = _spread_copies(src_refs, land_refs, s[0], s[1], s[2] if ns else None, kind, True)
        for cp in remote:
            cp.wait_send()
            cp.wait_recv()
        for cp in local:
            cp.wait()

    outs = pl.pallas_call(
        body, name=name, out_shape=tuple(pltpu.HBM(a.shape, a.dtype) for a in srcs + lands),
        in_specs=[HBM] * (ns + nl) + [SEM] * len(sems) + [pl.BlockSpec(memory_space=pl.ANY)] * len(after), out_specs=tuple([HBM] * (ns + nl)),
        input_output_aliases={i: i for i in range(ns + nl)}, compiler_params=pltpu.CompilerParams(has_side_effects=EFFECT),
    )(*srcs, *lands, *sems, *after)
    return list(outs[ns:])


SMALL = ("rel_bias", "norm1_g", "sgu_w", "sgu_b", "dil_qn_g", "dil_kn_g", "conv_w", "conv_b", "conv_ln_g", "conv_ln_b",
         "gqa_qn_g", "gqa_kn_g", "mix_norm_g", "norm2_g")
LARGE = ("w_in", "w_out", "w_gate", "w_up", "w_down")
EARLY = tuple(k for k in SMALL if k != "norm1_g")


def _local_step(x, target, p, B, S, fetch, emit, mid, early):
    T = B * S
    rope = _rope_tables(S)
    win = _bias_windows(p["rel_bias"], S)
    tile8 = lambda g: jnp.tile(g.reshape(1, HEAD_DIM), (1, N_HEADS))
    cols_b = (COL_BQ, COL_BK, COL_BV)
    cols_d = (COL_DQ, COL_DK128, COL_DV128)
    saved = []
    for l in range(DEPTH):
        s = {"x": x}
        s["ws"] = p["sgu_w"][l].astype(BF16)
        s["bias"] = jnp.repeat(p["sgu_b"][l].T, HEAD_DIM, axis=1)
        s["h"] = _rms_fwd(x, p["norm1_g"][l], f"rms1_fwd_{l}") if l == 0 else h_next
        s["win"] = fetch(l, "in", s["h"])
        s["cw"] = jnp.pad(s["win"]["conv_w"], ((0, 1), (0, 0))).reshape(32, 1, 512)
        z = s["z"] = _matmul(s["h"], s["win"]["w_in"], "nt", f"in_proj_{l}", tk=D_MODEL)
        s["bias"] = s["bias"] + mid(l, z)
        s["ya"] = _sgu_fwd(z, s["ws"], s["bias"], f"sgu_fwd_{l}")
        s["c"] = _conv_fwd1(z, s["cw"], p["conv_b"][l].reshape(1, 512), B, S, f"conv_fwd_{l}")
        s["ln"] = (p["conv_ln_g"][l].reshape(1, 512), p["conv_ln_b"][l].reshape(1, 512))
        s["gb"] = (tile8(p["dil_qn_g"][l]), tile8(p["dil_kn_g"][l]))
        s["gd"] = (tile8(p["gqa_qn_g"][l]), tile8(p["gqa_kn_g"][l])[:, :KV_WIDTH])
        s["qkv_b"] = _prep_fwd(z, *s["gb"], None, B, S, N_HEADS, cols_b, f"prep_b_fwd_{l}")
        s["qkv_d"] = _prep_fwd(z, *s["gd"], rope, B, S, KV_HEADS, cols_d, f"prep_d_fwd_{l}")
        s["ob"] = _attn_fwd(*s["qkv_b"], win, f"attn_b_fwd_{l}")
        s["od"] = _attn_fwd(*s["qkv_d"], None, f"attn_d_fwd_{l}", tq=GQA_TQ)
        s["gmix"] = p["mix_norm_g"][l].reshape(1, 2048)
        s["ycat"] = _mix_fwd(s["ya"], s["ob"], s["c"], s["od"], s["gmix"], *s["ln"], B, S, f"mix_fwd_{l}")
        s["wout"] = fetch(l, "out", s["ycat"])["w_out"]
        x1, s["h2"] = _matmul(s["ycat"], s["wout"], "nn", f"out_proj_{l}", res=x, tk=D_MODEL, norm=p["norm2_g"][l])
        s["x1"] = x1
        s["ffn"] = fetch(l, "ffn", s["h2"])
        s["gate"] = _mm_shard_out(s["h2"], s["ffn"]["w_gate"], "nt", f"ffn_gate_{l}", out_dtype=BF16, tm=1024)
        s["up"] = _mm_shard_out(s["h2"], s["ffn"]["w_up"], "nt", f"ffn_up_{l}", out_dtype=BF16, tm=1024)
        if l + 1 < DEPTH:
            x, s["act"], h_next = _ffn_down(s["gate"], s["up"], s["ffn"]["w_down"], x1, f"ffn_down_{l}", norm=p["norm1_g"][l + 1])
        else:
            dx, s["act"], loss_blk = _ffn_down(s["gate"], s["up"], s["ffn"]["w_down"], x1, f"ffn_down_{l}", target=target)
        saved.append(s)

    g = {k: [None] * DEPTH for k in SMALL if k != "rel_bias"}
    dwin_total = None
    for l in reversed(range(DEPTH)):
        s = saved[l]
        z, ffn = s["z"], s["ffn"]
        dgate, dup = _ffn_down_dx(dx, ffn["w_down"], s["gate"], s["up"], f"ffn_down_dx_{l}")
        tok = emit(l, "w_down", _mm_shard_m(s["act"], dx, f"ffn_down_dw_{l}", out_dtype=BF16, tm=FFN_BLOCK, tn=512, tk=T))
        tok += emit(l, "w_gate", _mm_shard_m(dgate, s["h2"], f"ffn_gate_dw_{l}", out_dtype=BF16, tm=FFN_BLOCK, tn=512, tk=T))
        tok += emit(l, "w_up", _mm_shard_m(dup, s["h2"], f"ffn_up_dw_{l}", out_dtype=BF16, tm=FFN_BLOCK, tn=512, tk=T))
        dh2 = _mm_shard_k([(dgate, ffn["w_gate"]), (dup, ffn["w_up"])], "nn", f"ffn_up_dx_{l}", tn=512, fold=FFN_GROUPS)
        dycat, dx1, dg2 = _out_proj_dx(dh2, s["x1"], p["norm2_g"][l] + tok, dx, s["wout"], f"out_proj_dx_{l}")
        g["norm2_g"][l] = dg2[0]
        tok = emit(l, "w_out", _matmul(s["ycat"], dx1, "tn", f"out_proj_dw_{l}", out_dtype=BF16, tn=1024, tk=T))
        dya, dob, dc, dod, dgm, dlg, dlb, dcb = _mix_bwd(s["ya"], s["ob"], s["c"], s["od"], dycat, s["gmix"] + tok, *s["ln"], B, S, f"mix_bwd_{l}")
        g["mix_norm_g"][l] = dgm[0]
        dz_a, dws, dbias = _sgu_bwd(z, dya, s["ws"], jnp.swapaxes(s["ws"], 1, 2), s["bias"], f"sgu_bwd_{l}")
        g["sgu_w"][l] = dws
        g["sgu_b"][l] = dbias.reshape(128, 8, HEAD_DIM).sum(-1).T
        g["conv_ln_g"][l], g["conv_ln_b"][l], g["conv_b"][l] = dlg[0], dlb[0], dcb[0]
        dz_ca, dz_cg, dcw = _conv_bwd2(z, dc, s["cw"], B, S, f"conv_bwd_{l}")
        g["conv_w"][l] = dcw.reshape(32, 512)[:CONV_WIDTH]
        dq, dk, dv, dwin = _attn_bwd(*s["qkv_b"], s["ob"], dob, win, f"attn_b_bwd_{l}")
        dwin_total = dwin if dwin_total is None else dwin_total + dwin
        dz_b, dgq, dgk = _prep_bwd(z, dq, dk, dv, *s["gb"], None, B, S, N_HEADS, cols_b, f"prep_b_bwd_{l}")
        g["dil_qn_g"][l] = dgq.reshape(N_HEADS, HEAD_DIM).sum(0)
        g["dil_kn_g"][l] = dgk.reshape(N_HEADS, HEAD_DIM).sum(0)
        dq, dk, dv = _attn_bwd(*s["qkv_d"], s["od"], dod, None, f"attn_d_bwd_{l}", tq=GQA_TQ)
        dz_d, dgq, dgk = _prep_bwd(z, dq, dk, dv, *s["gd"], rope, B, S, KV_HEADS, cols_d, f"prep_d_bwd_{l}")
        g["gqa_qn_g"][l] = dgq.reshape(N_HEADS, HEAD_DIM).sum(0)
        g["gqa_kn_g"][l] = dgk.reshape(KV_HEADS, HEAD_DIM).sum(0)
        dz = [dz_a, dz_b, dz_ca, dz_cg, dz_d]
        tok = jnp.zeros((), F32)
        if l == 0:
            done = {k: jnp.stack(v) for k, v in g.items() if k != "norm1_g"}
            done["rel_bias"] = _bias_fold(dwin_total, S, "bias_fold")
            tok = early(done)
        tok += emit(l, "w_in", _in_proj_dw(dz, s["h"], f"in_proj_dw_{l}"))
        dx, dg1 = _in_proj_dx(dz, s["win"]["w_in"], s["x"], p["norm1_g"][l] + tok, dx1, f"in_proj_dx_{l}")
        g["norm1_g"][l] = dg1[0]

    return loss_blk[0, 0], dx, jnp.stack(g["norm1_g"])


GROUPS = {"in": ("w_in",), "out": ("w_out",), "ffn": ("w_gate", "w_up", "w_down")}
COL_SHARDED = ("w_in", "w_gate", "w_up")


def kernel(x, rel_bias, norm1_g, w_in, sgu_w, sgu_b, dil_qn_g, dil_kn_g, conv_w, conv_b, conv_ln_g, conv_ln_b, gqa_qn_g, gqa_kn_g, mix_norm_g, w_out, norm2_g, w_gate, w_up, w_down, loss_target, m_rel_bias, m_norm1_g, m_w_in, m_sgu_w, m_sgu_b, m_dil_qn_g, m_dil_kn_g, m_conv_w, m_conv_b, m_conv_ln_g, m_conv_ln_b, m_gqa_qn_g, m_gqa_kn_g, m_mix_norm_g, m_w_out, m_norm2_g, m_w_gate, m_w_up, m_w_down, v_rel_bias, v_norm1_g, v_w_in, v_sgu_w, v_sgu_b, v_dil_qn_g, v_dil_kn_g, v_conv_w, v_conv_b, v_conv_ln_g, v_conv_ln_b, v_gqa_qn_g, v_gqa_kn_g, v_mix_norm_g, v_w_out, v_norm2_g, v_w_gate, v_w_up, v_w_down):
    w = dict(rel_bias=rel_bias, norm1_g=norm1_g, w_in=w_in, sgu_w=sgu_w, sgu_b=sgu_b, dil_qn_g=dil_qn_g, dil_kn_g=dil_kn_g, conv_w=conv_w,
             conv_b=conv_b, conv_ln_g=conv_ln_g, conv_ln_b=conv_ln_b, gqa_qn_g=gqa_qn_g, gqa_kn_g=gqa_kn_g, mix_norm_g=mix_norm_g,
             w_out=w_out, norm2_g=norm2_g, w_gate=w_gate, w_up=w_up, w_down=w_down)
    m = dict(rel_bias=m_rel_bias, norm1_g=m_norm1_g, w_in=m_w_in, sgu_w=m_sgu_w, sgu_b=m_sgu_b, dil_qn_g=m_dil_qn_g, dil_kn_g=m_dil_kn_g,
             conv_w=m_conv_w, conv_b=m_conv_b, conv_ln_g=m_conv_ln_g, conv_ln_b=m_conv_ln_b, gqa_qn_g=m_gqa_qn_g, gqa_kn_g=m_gqa_kn_g,
             mix_norm_g=m_mix_norm_g, w_out=m_w_out, norm2_g=m_norm2_g, w_gate=m_w_gate, w_up=m_w_up, w_down=m_w_down)
    v = dict(rel_bias=v_rel_bias, norm1_g=v_norm1_g, w_in=v_w_in, sgu_w=v_sgu_w, sgu_b=v_sgu_b, dil_qn_g=v_dil_qn_g, dil_kn_g=v_dil_kn_g,
             conv_w=v_conv_w, conv_b=v_conv_b, conv_ln_g=v_conv_ln_g, conv_ln_b=v_conv_ln_b, gqa_qn_g=v_gqa_qn_g, gqa_kn_g=v_gqa_kn_g,
             mix_norm_g=v_mix_norm_g, w_out=v_w_out, norm2_g=v_norm2_g, w_gate=v_w_gate, w_up=v_w_up, w_down=v_w_down)
    names = list(w)
    B, S, D = x.shape
    T = B * S
    me = 4 * lax.axis_index("x") + 2 * lax.axis_index("y") + lax.axis_index("c")

    view = lambda a, k: jnp.swapaxes(a, 1, 2) if k in COL_SHARDED else a
    bf = {k: view(w[k], k).astype(BF16) for k in LARGE}
    spreads, forwards = {}, {}

    def start_gather(l, group, after=None):
        srcs = [bf[k][l] for k in GROUPS[group]] + ([conv_w[l]] if group == "in" else [])
        spreads[l, group] = _spread_start(srcs, "chips", f"gather_{group}_{l}_start", after)
        return spreads[l, group][3][0, 0]

    def forward(l, group, after):
        sems, srcs, lands, _ = spreads[l, group]
        lands = _spread_wait(sems, srcs, lands, after, "chips", f"gather_{group}_{l}_wait")
        forwards[l, group] = _spread_start(None, "forward", f"forward_{group}_{l}_start", lands=lands)
        return forwards[l, group][3]

    def landed(l, group, after):
        sems, _, lands, _ = forwards[l, group]
        return _spread_wait(sems, [], lands, after, "forward", f"forward_{group}_{l}_wait")

    tok0 = start_gather(0, "in") + start_gather(0, "out") + start_gather(0, "ffn")
    small = {k: w[k] for k in SMALL}
    small["norm1_g"] = norm1_g.at[0].add(tok0)

    def mid(l, z):
        if l > 0:
            return jnp.zeros((), F32)
        return start_gather(1, "in", z) + start_gather(1, "out", z) + start_gather(1, "ffn", z)

    def fetch(l, group, after):
        if group == "in":
            tok = forward(0, "in", after) if l == 0 else after
        elif group == "out":
            tok = forward(l, "ffn", [after, forward(l, "out", after)])
        else:
            tok = forward(1, "in", after) if l == 0 else after
        got = dict(zip(GROUPS[group] + ("conv_w",), landed(l, group, [after, tok])))
        if group == "in":
            got["w_in"] = got["w_in"].reshape(IN_WIDTH, D)
            got["conv_w"] = jnp.transpose(got["conv_w"], (1, 0, 2)).reshape(CONV_WIDTH, 512)
        if group == "out":
            got["w_out"] = got["w_out"].reshape(D, D)
        if group == "ffn":
            got = {k: a.reshape(FFN_GROUPS, -1, D) for k, a in got.items()}
        return got

    scatters = {}

    def emit(l, k, dw):
        dw = dw.reshape(N_DEV, -1, D)
        scatters[l, k] = _spread_start([dw], "scatter", f"scatter_{k}_{l}_start")
        return scatters[l, k][3][0, 0]

    flat2 = lambda a: a.reshape(-1, a.shape[-1])
    small_spread = []

    def early(done):
        small_spread.append(_spread_start([flat2(done[k]) for k in EARLY], "gather", "gather_small_grads_start"))
        return small_spread[0][3][0, 0]

    loss_part, dx, dnorm1 = _local_step(x.reshape(T, D), loss_target.reshape(T, D), small, B, S, fetch, emit, mid, early)
    loss = lax.psum(loss_part, ("x", "y", "c"))

    out_g, out_d, out_m, out_v = {}, {}, {}, {}

    def update_large(k, after):
        shp = view(w[k], k).shape
        two_d = lambda a: view(a, k).reshape(-1, shp[-1])
        res = None
        for l in reversed(range(DEPTH)):
            sems, srcs, lands, _ = scatters[l, k]
            stack = _spread_wait(sems, srcs, lands, after, "scatter", f"scatter_{k}_{l}_wait")[0]
            res = _adamw(two_d(w[k]), two_d(m[k]), two_d(v[k]), stack.reshape(N_DEV, -1, shp[-1]), f"adamw_{k}_{l}", layer=l, prev=res)
        out_g[k], out_d[k], out_m[k], out_v[k] = [view(a.reshape(shp), k) for a in res]
        return res[0]

    late_sems, late_srcs, late_lands, late_tok = _spread_start([flat2(dnorm1)], "gather", "gather_norm1_grad_start")
    after = [dx, late_tok]
    for k in ("w_down", "w_gate", "w_up", "w_out"):
        after = update_large(k, after)
    sems, srcs, lands, _ = small_spread[0]
    stacks = dict(zip(EARLY, _spread_wait(sems, srcs, lands, after, "gather", "gather_small_grads_wait")))
    stacks["norm1_g"] = _spread_wait(late_sems, late_srcs, late_lands, after, "gather", "gather_norm1_grad_wait")[0]
    for k in SMALL:
        stack = stacks[k]
        if k == "conv_w":
            stack = lax.dynamic_slice_in_dim(stack, me * (512 // N_DEV), 512 // N_DEV, axis=2)
        res = _adamw(flat2(w[k]), flat2(m[k]), flat2(v[k]), stack, f"adamw_{k}")
        out_g[k], out_d[k], out_m[k], out_v[k] = [a.reshape(w[k].shape) for a in res]
        after = res[0]
    update_large("w_in", after)

    return (loss, dx.reshape(B, S, D), *[out_g[k] for k in names], *[out_d[k] for k in names],
            *[out_m[k] for k in names], *[out_v[k] for k in names])
```

```python
import functools
import math

import numpy as np
import jax
import jax.numpy as jnp
from jax import lax
from jax.experimental import pallas as pl
from jax.experimental.pallas import tpu as pltpu

F32 = jnp.float32
BF16 = jnp.bfloat16
HIGHEST = lax.Precision.HIGHEST
MESH_ID = pl.DeviceIdType.MESH

D_MODEL = 2048
DEPTH = 2
HEAD_DIM = 64
GROUP_WIDTH = 512
N_HEADS = 8
KV_HEADS = 2
KV_WIDTH = 128
SGU_CHUNK = 128
CONV_WIDTH = 31
CONV_PAD = 16
GRID_W = 64
ROPE_THETA = 10000.0
REL_BUCKETS = 32
REL_MAX_DIST = 1024
DIL_PATTERNS = ((128, 1), (512, 4), (2048, 16))
FFN_HIDDEN = 5632
IN_WIDTH = 4352
RMS_EPS = 1e-6
LN_EPS = 1e-5
MASKED = -1e30
N_DEV = 8

ADAM_LR = 0.001
ADAM_B1 = 0.9
ADAM_B2 = 0.999
ADAM_EPS = 1e-08
ADAM_WD = 0.01
ADAM_STEP = 10

COL_AU, COL_AV, COL_BQ, COL_BK, COL_BV, COL_CA, COL_CG, COL_DQ = range(8)
COL_DK128, COL_DV128 = 32, 33

VMEM_LIMIT = 56 * 1024 * 1024
FFN_GROUPS = 2
FFN_BLOCK = 1408
ATTN_TQ = 256
GQA_TQ = 512
ROW_TILE = 512


def _params(sem=None, vmem=VMEM_LIMIT):
    return pltpu.CompilerParams(dimension_semantics=sem, vmem_limit_bytes=vmem)


def _dot(a, b, dims, precision=None):
    return lax.dot_general(a, b, (dims, ((), ())), precision=precision, preferred_element_type=F32)


def _nn(a, b, precision=None):
    return _dot(a, b, ((1,), (0,)), precision)


def _nt(a, b):
    return _dot(a, b, ((1,), (1,)))


def _tn(a, b):
    return _dot(a, b, ((0,), (0,)))


DIMS = {"nn": ((1,), (0,)), "nt": ((1,), (1,)), "tn": ((0,), (0,))}


def _pick(n, cands):
    for c in cands:
        if n % c == 0:
            return c
    return n


def _mm_call(name, mode, pairs, specs, o_spec, out_sds, grid, acc_shape, res=None, fold=None, norm=None):
    npair, nk, dims = len(pairs), grid[2], DIMS[mode]

    def body(*refs):
        ab = refs[:2 * npair]
        at = 2 * npair
        r_ref = refs[at] if res is not None else None
        at += res is not None
        g_ref = refs[at] if norm is not None else None
        at += norm is not None
        o_ref = refs[at]
        h_ref = refs[at + 1] if norm is not None else None
        part = None
        for t in range(npair):
            for s in ([None] if fold is None else range(fold)):
                a_blk = ab[2 * t][...] if s is None else ab[2 * t][s]
                b_blk = ab[2 * t + 1][...] if s is None else ab[2 * t + 1][s]
                d = _dot(a_blk.astype(BF16), b_blk.astype(BF16), dims)
                part = d if part is None else part + d

        def finish(r):
            if r_ref is not None:
                r = r + r_ref[...]
            o_ref[...] = r.astype(o_ref.dtype)
            if h_ref is not None:
                h_ref[...] = _rms_rows(r, g_ref[...])

        if nk == 1:
            finish(part)
            return
        acc, k = refs[-1], pl.program_id(2)

        @pl.when(k == 0)
        def _():
            acc[...] = part

        @pl.when(k > 0)
        def _():
            acc[...] += part

        @pl.when(k == nk - 1)
        def _():
            finish(acc[...])

    ins = [t for pair in pairs for t in pair]
    in_specs = [t for pair in specs for t in pair]
    if res is not None:
        ins.append(res)
        in_specs.append(o_spec)
    out_specs = o_spec
    if norm is not None:
        ins.append(norm)
        in_specs.append(pl.BlockSpec(norm.shape, lambda *_: (0, 0)))
        out_specs, out_sds = [o_spec, o_spec], [out_sds, jax.ShapeDtypeStruct(out_sds.shape, BF16)]
    return pl.pallas_call(
        body, name=name, grid=grid, in_specs=in_specs, out_specs=out_specs, out_shape=out_sds,
        scratch_shapes=[pltpu.VMEM(acc_shape, F32)] if nk > 1 else [],
        compiler_params=_params(("parallel", "parallel", "arbitrary")),
    )(*ins)


def _rms_rows(x, g):
    return (x * lax.rsqrt(jnp.mean(x * x, axis=-1, keepdims=True) + RMS_EPS) * g).astype(BF16)


def _matmul(a, b, mode, name, res=None, out_dtype=F32, tm=512, tn=None, tk=None, norm=None):
    if mode == "nn":
        (M, K), N = a.shape, b.shape[1]
    elif mode == "nt":
        (M, K), N = a.shape, b.shape[0]
    else:
        (K, M), N = a.shape, b.shape[1]
    tm = min(tm, M)
    tn = tn or _pick(N, (2176, 2048, 1408, 1024, 512))
    tk = tk or _pick(K, (1024, 2176, 1408, 512))
    assert M % tm == 0 and N % tn == 0 and K % tk == 0, (M, N, K, tm, tn, tk)
    a_spec = pl.BlockSpec((tk, tm), lambda i, j, k: (k, i)) if mode == "tn" else pl.BlockSpec((tm, tk), lambda i, j, k: (i, k))
    b_spec = pl.BlockSpec((tn, tk), lambda i, j, k: (j, k)) if mode == "nt" else pl.BlockSpec((tk, tn), lambda i, j, k: (k, j))
    o_spec = pl.BlockSpec((tm, tn), lambda i, j, k: (i, j))
    assert norm is None or tn == N
    return _mm_call(name, mode, [(a, b)], [(a_spec, b_spec)], o_spec, jax.ShapeDtypeStruct((M, N), out_dtype),
                    (M // tm, N // tn, K // tk), (tm, tn), res, norm=None if norm is None else norm.reshape(1, N))


def _mm_shard_out(a, bs, mode, name, out_dtype=F32, tm=512, tk=None):
    J = bs.shape[0]
    n = bs.shape[1] if mode == "nt" else bs.shape[2]
    (K, M) = a.shape if mode == "tn" else a.shape[::-1]
    tm = min(tm, M)
    tk = tk or (K if mode != "tn" else _pick(K, (1024, 512)))
    a_spec = pl.BlockSpec((tk, tm), lambda j, i, k: (k, i)) if mode == "tn" else pl.BlockSpec((tm, tk), lambda j, i, k: (i, k))
    b_spec = pl.BlockSpec((None, n, tk), lambda j, i, k: (j, 0, k)) if mode == "nt" else pl.BlockSpec((None, tk, n), lambda j, i, k: (j, k, 0))
    o_spec = pl.BlockSpec((None, tm, n), lambda j, i, k: (j, i, 0))
    return _mm_call(name, mode, [(a, bs)], [(a_spec, b_spec)], o_spec, jax.ShapeDtypeStruct((J, M, n), out_dtype),
                    (J, M // tm, K // tk), (tm, n))


def _mm_shard_k(pairs, mode, name, res=None, out_dtype=F32, tm=512, tn=None, fold=1):
    J, M, n = pairs[0][0].shape
    N = pairs[0][1].shape[2] if mode == "nn" else pairs[0][1].shape[1]
    tm = min(tm, M)
    tn = tn or _pick(N, (2048, 1024, 512))
    a_spec = pl.BlockSpec((fold, tm, n), lambda i, j, k: (k, i, 0))
    b_spec = pl.BlockSpec((fold, n, tn), lambda i, j, k: (k, 0, j)) if mode == "nn" else pl.BlockSpec((fold, tn, n), lambda i, j, k: (k, j, 0))
    o_spec = pl.BlockSpec((tm, tn), lambda i, j, k: (i, j))
    return _mm_call(name, mode, pairs, [(a_spec, b_spec)] * len(pairs), o_spec, jax.ShapeDtypeStruct((M, N), out_dtype),
                    (M // tm, N // tn, J // fold), (tm, tn), res, fold)


def _mm_shard_m(as_, b, name, out_dtype=F32, tm=None, tn=None, tk=512):
    J, K, n = as_.shape
    N = b.shape[1]
    tm = tm or n
    tn = tn or _pick(N, (2048, 1024, 512))
    tk = min(tk, K)
    nn = N // tn
    a_spec = pl.BlockSpec((None, tk, tm), lambda j, i, k: (j, k, i // nn))
    b_spec = pl.BlockSpec((tk, tn), lambda j, i, k: (k, i % nn))
    o_spec = pl.BlockSpec((None, tm, tn), lambda j, i, k: (j, i // nn, i % nn))
    return _mm_call(name, "tn", [(as_, b)], [(a_spec, b_spec)], o_spec, jax.ShapeDtypeStruct((J, n, N), out_dtype),
                    (J, (n // tm) * nn, K // tk), (tm, tn))


def _out_proj_dx(dh, x, g, dres, w, name):
    T, D = x.shape
    N = w.shape[0]
    tm = min(256, T)

    def body(dh_ref, x_ref, g_ref, dres_ref, w_ref, o_ref, dx_ref, dg_ref):
        @pl.when(pl.program_id(0) == 0)
        def _():
            dg_ref[...] = jnp.zeros_like(dg_ref)

        xv, dhv = x_ref[...], dh_ref[...]
        r = lax.rsqrt(jnp.mean(xv * xv, axis=-1, keepdims=True) + RMS_EPS)
        y = xv * r
        dy = dhv * g_ref[...]
        dx = dres_ref[...] + r * (dy - y * jnp.mean(dy * y, axis=-1, keepdims=True))
        dx_ref[...] = dx
        dg_ref[...] += jnp.sum(dhv * y, axis=0, keepdims=True)
        o_ref[...] = _nt(dx.astype(BF16), w_ref[...])

    row = pl.BlockSpec((tm, D), lambda i: (i, 0))
    vec = pl.BlockSpec((1, D), lambda i: (0, 0))
    return pl.pallas_call(
        body, name=name, grid=(T // tm,), in_specs=[row, row, vec, row, pl.BlockSpec((N, D), lambda i: (0, 0))],
        out_specs=[pl.BlockSpec((tm, N), lambda i: (i, 0)), row, vec],
        out_shape=[jax.ShapeDtypeStruct((T, N), F32), jax.ShapeDtypeStruct((T, D), F32), jax.ShapeDtypeStruct((1, D), F32)],
        compiler_params=_params(("arbitrary",)),
    )(dh, x, g.reshape(1, D), dres, w)


def _in_proj_dw(pieces, h, name):
    T, D = h.shape
    tm = 256
    nbs = [p.shape[1] // tm for p in pieces]
    los = [sum(nbs[:t]) for t in range(len(pieces))]

    def body(*refs):
        h_ref, o_ref = refs[-2:]
        i = pl.program_id(0)
        for p_ref, lo, nb in zip(refs[:-2], los, nbs):
            @pl.when((i >= lo) & (i < lo + nb))
            def _():
                o_ref[...] = _tn(p_ref[...], h_ref[...]).astype(BF16)

    specs = [pl.BlockSpec((T, tm), (lambda lo, nb: lambda i: (0, jnp.clip(i - lo, 0, nb - 1)))(lo, nb)) for lo, nb in zip(los, nbs)]
    return pl.pallas_call(body, name=name, grid=(sum(nbs),), in_specs=specs + [pl.BlockSpec((T, D), lambda i: (0, 0))],
                          out_specs=pl.BlockSpec((tm, D), lambda i: (i, 0)), out_shape=jax.ShapeDtypeStruct((sum(nbs) * tm, D), BF16),
                          compiler_params=_params(("parallel",)))(*pieces, h)


def _in_proj_dx(pieces, w, x, g, dres, name):
    T, D = x.shape
    K = w.shape[0]
    tm = min(256, T)
    n = len(pieces)
    widths = [p.shape[1] for p in pieces]
    offs = [sum(widths[:t]) for t in range(n)]

    def body(*refs):
        w_ref, x_ref, g_ref, dres_ref, dx_ref, dg_ref = refs[n:]

        @pl.when(pl.program_id(0) == 0)
        def _():
            dg_ref[...] = jnp.zeros_like(dg_ref)

        dh = None
        for p_ref, off, wd in zip(refs[:n], offs, widths):
            d = _nn(p_ref[...], w_ref[off:off + wd, :])
            dh = d if dh is None else dh + d
        xv = x_ref[...]
        r = lax.rsqrt(jnp.mean(xv * xv, axis=-1, keepdims=True) + RMS_EPS)
        y = xv * r
        dy = dh * g_ref[...]
        dx_ref[...] = dres_ref[...] + r * (dy - y * jnp.mean(dy * y, axis=-1, keepdims=True))
        dg_ref[...] += jnp.sum(dh * y, axis=0, keepdims=True)

    specs = [pl.BlockSpec((tm, wd), lambda i: (i, 0)) for wd in widths]
    row = pl.BlockSpec((tm, D), lambda i: (i, 0))
    vec = pl.BlockSpec((1, D), lambda i: (0, 0))
    return pl.pallas_call(body, name=name, grid=(T // tm,), in_specs=specs + [pl.BlockSpec((K, D), lambda i: (0, 0)), row, vec, row],
                          out_specs=[row, vec], out_shape=[jax.ShapeDtypeStruct((T, D), F32), jax.ShapeDtypeStruct((1, D), F32)],
                          compiler_params=_params(("arbitrary",)))(*pieces, w, x, g.reshape(1, D), dres)


def _seg_matrix(width):
    return jnp.asarray(np.kron(np.eye(width // HEAD_DIM, dtype=np.float32), np.full((HEAD_DIM, HEAD_DIM), 1.0 / HEAD_DIM, np.float32)), BF16)


def _segmean(v, p):
    hi = v.astype(BF16)
    r = v - hi.astype(F32)
    mid = r.astype(BF16)
    lo = (r - mid.astype(F32)).astype(BF16)
    w = min(256, v.shape[1])
    pw = p[:w, :w]
    halves = []
    for c in range(v.shape[1] // w):
        cols = slice(c * w, (c + 1) * w)
        halves.append(_nn(hi[:, cols], pw) + _nn(mid[:, cols], pw) + _nn(lo[:, cols], pw))
    return halves[0] if len(halves) == 1 else jnp.concatenate(halves, axis=1)


def _gelu(x):
    c0 = math.sqrt(2.0 / math.pi)
    t = jnp.tanh(c0 * (x + 0.044715 * x * x * x))
    return 0.5 * x * (1.0 + t), t


def _gelu_grad(x, t):
    c0 = math.sqrt(2.0 / math.pi)
    return 0.5 * (1.0 + t) + 0.5 * x * (1.0 - t * t) * c0 * (1.0 + 3.0 * 0.044715 * x * x)


def _sigmoid(x):
    return 1.0 / (1.0 + jnp.exp(-x))


def _rms_fwd(x, g, name):
    T, D = x.shape
    tm = min(256, T)

    def body(x_ref, g_ref, o_ref):
        o_ref[...] = _rms_rows(x_ref[...], g_ref[...])

    return pl.pallas_call(
        body, name=name, grid=(T // tm,),
        in_specs=[pl.BlockSpec((tm, D), lambda i: (i, 0)), pl.BlockSpec((1, D), lambda i: (0, 0))],
        out_specs=pl.BlockSpec((tm, D), lambda i: (i, 0)), out_shape=jax.ShapeDtypeStruct((T, D), BF16),
        compiler_params=_params(("parallel",)),
    )(x, g.reshape(1, D))


def _sgu_core(zu, zv, ws_ref, bias, p):
    ug, tu = _gelu(zu)
    vg, tv = _gelu(zv)
    xc = vg - _segmean(vg, p)
    rs = lax.rsqrt(_segmean(xc * xc, p) + LN_EPS)
    vn = xc * rs
    vnb = vn.astype(BF16)
    low = lax.broadcasted_iota(jnp.int32, (SGU_CHUNK, 128), 1) < HEAD_DIM
    parts = []
    for j in range(4):
        vp = vnb[:, 128 * j:128 * (j + 1)]
        parts.append(jnp.where(low, _nn(ws_ref[2 * j], vp), _nn(ws_ref[2 * j + 1], vp)))
    mixed = jnp.concatenate(parts, axis=1) + bias
    return ug, tu, tv, rs, vn, vnb, mixed, low


SGU_ROWS = 8 * SGU_CHUNK


def _sgu_fwd(z, ws, bias, name):
    T = z.shape[0]

    def body(zu_ref, zv_ref, ws_ref, b_ref, p_ref, y_ref):
        for r in range(0, SGU_ROWS, SGU_CHUNK):
            rows = slice(r, r + SGU_CHUNK)
            ug, _, _, _, _, _, mixed, _ = _sgu_core(zu_ref[rows, :], zv_ref[rows, :], ws_ref, b_ref[...], p_ref[...])
            y_ref[rows, :] = ug * mixed

    full = lambda shape: pl.BlockSpec(shape, lambda i: (0,) * len(shape))
    return pl.pallas_call(
        body, name=name, grid=(T // SGU_ROWS,),
        in_specs=[pl.BlockSpec((SGU_ROWS, 512), lambda i: (i, COL_AU)), pl.BlockSpec((SGU_ROWS, 512), lambda i: (i, COL_AV)),
                  full((8, 128, 128)), full((128, 512)), full((512, 512))],
        out_specs=pl.BlockSpec((SGU_ROWS, 512), lambda i: (i, 0)), out_shape=jax.ShapeDtypeStruct((T, 512), F32),
        compiler_params=_params(("parallel",)),
    )(z, z, ws, bias, _seg_matrix(512))


def _sgu_bwd(z, dy, ws, ws_t, bias, name):
    T = z.shape[0]

    def body(zu_ref, zv_ref, dy_ref, ws_ref, wst_ref, b_ref, p_ref, dz_ref, dws_ref, db_ref):
        @pl.when(pl.program_id(0) == 0)
        def _():
            dws_ref[...] = jnp.zeros_like(dws_ref)
            db_ref[...] = jnp.zeros_like(db_ref)

        p = p_ref[...]
        zero = jnp.zeros((SGU_CHUNK, 128), BF16)
        dws = [None] * 8
        db = None
        for r in range(0, SGU_ROWS, SGU_CHUNK):
            rows = slice(r, r + SGU_CHUNK)
            zu, zv = zu_ref[rows, :], zv_ref[rows, :]
            ug, tu, tv, rs, vn, vnb, mixed, low = _sgu_core(zu, zv, ws_ref, b_ref[...], p)
            dyv = dy_ref[rows, :]
            dmixed = dyv * ug
            db = dmixed if db is None else db + dmixed
            dmb = dmixed.astype(BF16)
            parts = []
            for j in range(4):
                dmp, vp = dmb[:, 128 * j:128 * (j + 1)], vnb[:, 128 * j:128 * (j + 1)]
                for g, d in ((2 * j, _nt(jnp.where(low, dmp, zero), vp)), (2 * j + 1, _nt(jnp.where(low, zero, dmp), vp))):
                    dws[g] = d if dws[g] is None else dws[g] + d
                parts.append(jnp.where(low, _nn(wst_ref[2 * j], dmp), _nn(wst_ref[2 * j + 1], dmp)))
            dvn = jnp.concatenate(parts, axis=1)
            dvg = rs * (dvn - _segmean(dvn, p) - vn * _segmean(dvn * vn, p))
            dz_ref[rows, 0:512] = (dyv * mixed * _gelu_grad(zu, tu)).astype(BF16)
            dz_ref[rows, 512:1024] = (dvg * _gelu_grad(zv, tv)).astype(BF16)
        db_ref[...] += db
        for g in range(8):
            dws_ref[g] += dws[g]

    full = lambda shape: pl.BlockSpec(shape, lambda i: (0,) * len(shape))
    return pl.pallas_call(
        body, name=name, grid=(T // SGU_ROWS,),
        in_specs=[pl.BlockSpec((SGU_ROWS, 512), lambda i: (i, COL_AU)), pl.BlockSpec((SGU_ROWS, 512), lambda i: (i, COL_AV)),
                  pl.BlockSpec((SGU_ROWS, 512), lambda i: (i, 0)), full((8, 128, 128)), full((8, 128, 128)), full((128, 512)), full((512, 512))],
        out_specs=[pl.BlockSpec((SGU_ROWS, 1024), lambda i: (i, 0)), full((8, 128, 128)), full((128, 512))],
        out_shape=[jax.ShapeDtypeStruct((T, 1024), BF16), jax.ShapeDtypeStruct((8, 128, 128), F32), jax.ShapeDtypeStruct((128, 512), F32)],
        compiler_params=_params(("arbitrary",)),
    )(z, z, dy, ws, ws_t, bias, _seg_matrix(512))


CONV_ROWS = 256


def _conv_taps(pad_ref, w_ref, base, flip):
    acc = None
    for k in range(CONV_WIDTH):
        wk = w_ref[CONV_WIDTH - 1 - k if flip else k]
        t = wk * pad_ref[base + k + 1:base + k + 1 + CONV_ROWS, :]
        acc = t if acc is None else acc + t
    return acc


def _conv_fwd1(z, w, cb, B, S, name):
    T = B * S
    rows = min(CONV_ROWS, S)
    assert rows == CONV_ROWS

    def body(a_ref, g_ref, w_ref, cb_ref, c_ref, pad):
        pad[0:CONV_PAD, :] = jnp.zeros((CONV_PAD, 128), F32)
        pad[CONV_PAD + S:2 * CONV_PAD + S, :] = jnp.zeros((CONV_PAD, 128), F32)
        pad[CONV_PAD:CONV_PAD + S, :] = a_ref[...] * _sigmoid(g_ref[...])

        for base in range(0, S, CONV_ROWS):
            c_ref[base:base + CONV_ROWS, :] = _conv_taps(pad, w_ref, base, False) + cb_ref[...]

    return pl.pallas_call(
        body, name=name, grid=(4, B),
        in_specs=[pl.BlockSpec((S, 128), lambda j, b: (b, 4 * COL_CA + j)), pl.BlockSpec((S, 128), lambda j, b: (b, 4 * COL_CG + j)),
                  pl.BlockSpec((32, 1, 128), lambda j, b: (0, 0, j)), pl.BlockSpec((1, 128), lambda j, b: (0, j))],
        out_specs=pl.BlockSpec((S, 128), lambda j, b: (b, j)), out_shape=jax.ShapeDtypeStruct((T, 512), F32),
        scratch_shapes=[pltpu.VMEM((S + 2 * CONV_PAD, 128), F32)], compiler_params=_params(("parallel", "parallel")),
    )(z, z, w, cb)


def _ln_rows(c):
    mu = jnp.mean(c, axis=-1, keepdims=True)
    xc = c - mu
    rs = lax.rsqrt(jnp.mean(xc * xc, axis=-1, keepdims=True) + LN_EPS)
    return xc * rs, rs


def _conv_bwd2(z, dc, w, B, S, name):
    T = B * S

    def body(a_ref, g_ref, dc_ref, w_ref, da_ref, dg_ref, dw_ref, hpad, dpad):
        @pl.when(pl.program_id(1) == 0)
        def _():
            dw_ref[...] = jnp.zeros_like(dw_ref)

        zeros = jnp.zeros((CONV_PAD, 128), F32)
        for ref in (hpad, dpad):
            ref[0:CONV_PAD, :] = zeros
            ref[CONV_PAD + S:2 * CONV_PAD + S, :] = zeros
        hpad[CONV_PAD:CONV_PAD + S, :] = a_ref[...] * _sigmoid(g_ref[...])
        dpad[CONV_PAD:CONV_PAD + S, :] = dc_ref[...]
        dws = [None] * CONV_WIDTH
        for base in range(0, S, CONV_ROWS):
            rows = slice(base, base + CONV_ROWS)
            dh = _conv_taps(dpad, w_ref, base, True)
            sg = _sigmoid(g_ref[rows, :])
            da_ref[rows, :] = (dh * sg).astype(BF16)
            dg_ref[rows, :] = (dh * a_ref[rows, :] * sg * (1.0 - sg)).astype(BF16)
            dcv = dc_ref[rows, :]
            for k in range(CONV_WIDTH):
                prod = dcv * hpad[base + k + 1:base + k + 1 + CONV_ROWS, :]
                part = jnp.sum(prod.reshape(CONV_ROWS // 8, 8, 128), axis=0)
                dws[k] = part if dws[k] is None else dws[k] + part
        for k in range(CONV_WIDTH):
            dw_ref[k] += jnp.sum(dws[k], axis=0, keepdims=True)

    return pl.pallas_call(
        body, name=name, grid=(4, B),
        in_specs=[pl.BlockSpec((S, 128), lambda j, b: (b, 4 * COL_CA + j)), pl.BlockSpec((S, 128), lambda j, b: (b, 4 * COL_CG + j)),
                  pl.BlockSpec((S, 128), lambda j, b: (b, j)), pl.BlockSpec((32, 1, 128), lambda j, b: (0, 0, j))],
        out_specs=[pl.BlockSpec((S, 128), lambda j, b: (b, j)), pl.BlockSpec((S, 128), lambda j, b: (b, j)),
                   pl.BlockSpec((32, 1, 128), lambda j, b: (0, 0, j))],
        out_shape=[jax.ShapeDtypeStruct((T, 512), BF16), jax.ShapeDtypeStruct((T, 512), BF16), jax.ShapeDtypeStruct((32, 1, 512), F32)],
        scratch_shapes=[pltpu.VMEM((S + 2 * CONV_PAD, 128), F32), pltpu.VMEM((S + 2 * CONV_PAD, 128), F32)],
        compiler_params=_params(("parallel", "arbitrary")),
    )(z, z, dc, w)


def _swap16(x):
    n = x.shape[1]
    first = (lax.broadcasted_iota(jnp.int32, x.shape, 1) % 32) < 16
    return jnp.where(first, pltpu.roll(x, n - 16, 1), pltpu.roll(x, 16, 1))


def _rope(x, cos, sin):
    return x * cos + _swap16(x) * sin


def _rope_t(dy, cos, sin):
    return dy * cos + _swap16(dy * sin)


def _qk_norm(x, p):
    r = lax.rsqrt(_segmean(x * x, p) + RMS_EPS)
    return x * r, r


def _store_heads(ref, val, n):
    for h in range(n):
        ref[h] = val[:, HEAD_DIM * h:HEAD_DIM * (h + 1)].astype(ref.dtype)


def _load_heads(ref, n):
    return jnp.concatenate([ref[h] for h in range(n)], axis=1)


def _prep_fwd(z, gq, gk, rope, B, S, kv_heads, cols, name):
    tm = min(2 * ROW_TILE, S)
    ns = S // tm
    kw = kv_heads * HEAD_DIM
    scale = HEAD_DIM ** -0.5
    qc, kc, vc = cols

    def body(*refs):
        if rope is None:
            q_ref, k_ref, v_ref, gq_ref, gk_ref, p_ref, qo, ko, vo = refs
        else:
            q_ref, k_ref, v_ref, gq_ref, gk_ref, p_ref, cos_ref, sin_ref, qo, ko, vo = refs
        p = p_ref[...]
        qn, _ = _qk_norm(q_ref[...], p)
        kn, _ = _qk_norm(k_ref[...], p[:kw, :kw])
        qn, kn = qn * gq_ref[...], kn * gk_ref[...]
        if rope is not None:
            cos, sin = cos_ref[...], sin_ref[...]
            qn, kn = _rope(qn, cos, sin), _rope(kn, cos[:, :kw], sin[:, :kw])
        _store_heads(qo, qn * scale, N_HEADS)
        _store_heads(ko, kn, kv_heads)
        _store_heads(vo, v_ref[...], kv_heads)

    row = lambda w, c: pl.BlockSpec((tm, w), lambda b, i: (b * ns + i, c))
    const = lambda shape: pl.BlockSpec(shape, lambda b, i: (0,) * len(shape))
    heads = lambda n: pl.BlockSpec((None, n, tm, HEAD_DIM), lambda b, i: (b, 0, i, 0))
    ins = [z, z, z, gq, gk, _seg_matrix(512)]
    specs = [row(512, qc), row(kw, kc), row(kw, vc), const((1, 512)), const((1, kw)), const((512, 512))]
    if rope is not None:
        ins += list(rope)
        specs += [pl.BlockSpec((tm, 512), lambda b, i: (i, 0))] * 2
    return pl.pallas_call(
        body, name=name, grid=(B, ns), in_specs=specs, out_specs=[heads(N_HEADS), heads(kv_heads), heads(kv_heads)],
        out_shape=[jax.ShapeDtypeStruct((B, N_HEADS, S, HEAD_DIM), BF16), jax.ShapeDtypeStruct((B, kv_heads, S, HEAD_DIM), BF16),
                   jax.ShapeDtypeStruct((B, kv_heads, S, HEAD_DIM), BF16)],
        compiler_params=_params(("parallel", "parallel")),
    )(*ins)


def _prep_bwd(z, dq, dk, dv, gq, gk, rope, B, S, kv_heads, cols, name):
    T = B * S
    tm = min(2 * ROW_TILE, S)
    ns = S // tm
    kw = kv_heads * HEAD_DIM
    scale = HEAD_DIM ** -0.5
    qc, kc, _ = cols

    def body(*refs):
        if rope is None:
            q_ref, k_ref, dq_ref, dk_ref, dv_ref, gq_ref, gk_ref, p_ref, dz_ref, dgq_ref, dgk_ref = refs
        else:
            q_ref, k_ref, dq_ref, dk_ref, dv_ref, gq_ref, gk_ref, p_ref, cos_ref, sin_ref, dz_ref, dgq_ref, dgk_ref = refs

        @pl.when((pl.program_id(0) == 0) & (pl.program_id(1) == 0))
        def _():
            dgq_ref[...] = jnp.zeros_like(dgq_ref)
            dgk_ref[...] = jnp.zeros_like(dgk_ref)

        p = p_ref[...]
        dqv = _load_heads(dq_ref, N_HEADS) * scale
        dkv = _load_heads(dk_ref, kv_heads)
        if rope is not None:
            cos, sin = cos_ref[...], sin_ref[...]
            dqv, dkv = _rope_t(dqv, cos, sin), _rope_t(dkv, cos[:, :kw], sin[:, :kw])

        def through_norm(xv, dy, g, pm, dg_ref):
            xh, r = _qk_norm(xv, pm)
            dg_ref[...] += jnp.sum(dy * xh, axis=0, keepdims=True)
            dxh = dy * g
            return r * (dxh - xh * _segmean(dxh * xh, pm))

        dz_ref[:, 0:512] = through_norm(q_ref[...], dqv, gq_ref[...], p, dgq_ref).astype(BF16)
        dz_ref[:, 512:512 + kw] = through_norm(k_ref[...], dkv, gk_ref[...], p[:kw, :kw], dgk_ref).astype(BF16)
        dz_ref[:, 512 + kw:512 + 2 * kw] = _load_heads(dv_ref, kv_heads).astype(BF16)

    row = lambda w, c: pl.BlockSpec((tm, w), lambda b, i: (b * ns + i, c))
    const = lambda shape: pl.BlockSpec(shape, lambda b, i: (0,) * len(shape))
    heads = lambda n: pl.BlockSpec((None, n, tm, HEAD_DIM), lambda b, i: (b, 0, i, 0))
    ins = [z, z, dq, dk, dv, gq, gk, _seg_matrix(512)]
    specs = [row(512, qc), row(kw, kc), heads(N_HEADS), heads(kv_heads), heads(kv_heads), const((1, 512)), const((1, kw)), const((512, 512))]
    if rope is not None:
        ins += list(rope)
        specs += [pl.BlockSpec((tm, 512), lambda b, i: (i, 0))] * 2
    return pl.pallas_call(
        body, name=name, grid=(B, ns), in_specs=specs, out_specs=[row(512 + 2 * kw, 0), const((1, 512)), const((1, kw))],
        out_shape=[jax.ShapeDtypeStruct((T, 512 + 2 * kw), BF16), jax.ShapeDtypeStruct((1, 512), F32), jax.ShapeDtypeStruct((1, kw), F32)],
        compiler_params=_params(("arbitrary", "arbitrary")),
    )(*ins)


def _toeplitz(win, tq, S):
    r = pltpu.roll(jnp.broadcast_to(win, (tq, S + tq)), 0, 1, stride=1, stride_axis=0)
    return r[:, tq:tq + S]


ATTN_HEADS = 4


def _attn_fwd(q, k, v, win, name, nh=ATTN_HEADS, tq=ATTN_TQ):
    B, H, S, _ = q.shape
    shared = k.shape[1] != H
    assert not shared or H // k.shape[1] == nh
    tq = min(tq, S)

    def body(*refs):
        if win is None:
            q_ref, k_ref, v_ref, o_ref = refs
        else:
            q_ref, k_ref, v_ref, w_ref, o_ref = refs
        kvs = [(k_ref[...], v_ref[...])] * nh if shared else [(k_ref[h], v_ref[h]) for h in range(nh)]
        scores = []
        for h in range(nh):
            s = _nt(q_ref[h], kvs[h][0])
            if win is not None:
                s = s + _toeplitz(w_ref[h], tq, S)
            scores.append(s)
        probs = []
        for s in scores:
            p = jnp.exp(s - jnp.max(s, axis=-1, keepdims=True))
            probs.append((p.astype(BF16), jnp.sum(p, axis=-1, keepdims=True)))
        for h, (p, l) in enumerate(probs):
            o_ref[h] = _nn(p, kvs[h][1]) / l

    qs = pl.BlockSpec((None, nh, tq, HEAD_DIM), lambda b, h, i: (b, h, i, 0))
    ks = (pl.BlockSpec((None, None, S, HEAD_DIM), lambda b, h, i: (b, h, 0, 0)) if shared
          else pl.BlockSpec((None, nh, S, HEAD_DIM), lambda b, h, i: (b, h, 0, 0)))
    ins, specs = [q, k, v], [qs, ks, ks]
    if win is not None:
        ins.append(win)
        specs.append(pl.BlockSpec((nh, None, 1, S + tq), lambda b, h, i: (h, i, 0, 0)))
    return pl.pallas_call(body, name=name, grid=(B, H // nh, S // tq), in_specs=specs, out_specs=qs,
                          out_shape=jax.ShapeDtypeStruct((B, H, S, HEAD_DIM), F32),
                          compiler_params=_params(("parallel", "parallel", "parallel")))(*ins)


def _attn_bwd(q, k, v, o, do, win, name, nh=ATTN_HEADS, tq=ATTN_TQ):
    B, H, S, _ = q.shape
    hkv = k.shape[1]
    shared = hkv != H
    assert not shared or H // hkv == nh
    tq = min(tq, S)
    nq = S // tq

    def body(*refs):
        if win is None:
            q_ref, k_ref, v_ref, o_ref, do_ref, dq_ref, dk_ref, dv_ref = refs
        else:
            q_ref, k_ref, v_ref, o_ref, do_ref, w_ref, rev_ref, dq_ref, dk_ref, dv_ref, dw_ref = refs

        @pl.when(pl.program_id(2) == 0)
        def _():
            dk_ref[...] = jnp.zeros_like(dk_ref)
            dv_ref[...] = jnp.zeros_like(dv_ref)

        kvs = [(k_ref[...], v_ref[...])] * nh if shared else [(k_ref[h], v_ref[h]) for h in range(nh)]
        qvs, dobs, scores, dps = [], [], [], []
        for h in range(nh):
            qv, dov = q_ref[h], do_ref[h]
            dob = dov.astype(BF16)
            s = _nt(qv, kvs[h][0])
            if win is not None:
                s = s + _toeplitz(w_ref[h], tq, S)
            dp = _nt(dob, kvs[h][1]) - jnp.sum(dov * o_ref[h], axis=-1, keepdims=True)
            qvs.append(qv)
            dobs.append(dob)
            scores.append(s)
            dps.append(dp)
        pbs, dsbs = [], []
        for s, dp in zip(scores, dps):
            p = jnp.exp(s - jnp.max(s, axis=-1, keepdims=True))
            p = p * (1.0 / jnp.sum(p, axis=-1, keepdims=True))
            pbs.append(p.astype(BF16))
            dsbs.append((p * dp).astype(BF16))
        dk_acc = dv_acc = None
        for h in range(nh):
            dvh, dkh = _tn(pbs[h], dobs[h]), _tn(dsbs[h], qvs[h])
            dq_ref[h] = _nn(dsbs[h], kvs[h][0])
            if shared:
                dv_acc = dvh if dv_acc is None else dv_acc + dvh
                dk_acc = dkh if dk_acc is None else dk_acc + dkh
            else:
                dv_ref[h] += dvh
                dk_ref[h] += dkh
            if win is not None:
                rev = _nn(rev_ref[...], dsbs[h])
                half = tq // 2
                top = jnp.concatenate([rev[:half], jnp.zeros((half, tq), F32)], axis=1)
                bot = jnp.concatenate([jnp.zeros((half, half), F32), rev[half:], jnp.zeros((half, half), F32)], axis=1)
                dw_ref[h] = jnp.sum(pltpu.roll(top + bot, 0, 1, stride=1, stride_axis=0), axis=0, keepdims=True)
        if shared:
            dv_ref[...] += dv_acc
            dk_ref[...] += dk_acc

    qs = pl.BlockSpec((None, nh, tq, HEAD_DIM), lambda b, h, i: (b, h, i, 0))
    ks = (pl.BlockSpec((None, None, S, HEAD_DIM), lambda b, h, i: (b, h, 0, 0)) if shared
          else pl.BlockSpec((None, nh, S, HEAD_DIM), lambda b, h, i: (b, h, 0, 0)))
    ins, specs = [q, k, v, o, do], [qs, ks, ks, qs, qs]
    outs = [jax.ShapeDtypeStruct((B, H, S, HEAD_DIM), F32), jax.ShapeDtypeStruct((B, hkv, S, HEAD_DIM), F32), jax.ShapeDtypeStruct((B, hkv, S, HEAD_DIM), F32)]
    ospecs = [qs, ks, ks]
    if win is not None:
        ins += [win, jnp.asarray(np.eye(tq, dtype=np.float32)[::-1].copy(), BF16)]
        specs += [pl.BlockSpec((nh, None, 1, S + tq), lambda b, h, i: (h, i, 0, 0)), pl.BlockSpec((tq, tq), lambda b, h, i: (0, 0))]
        outs.append(jax.ShapeDtypeStruct((B, H, nq, 1, S + tq), F32))
        ospecs.append(pl.BlockSpec((None, nh, None, 1, S + tq), lambda b, h, i: (b, h, i, 0, 0)))
    return pl.pallas_call(body, name=name, grid=(B, H // nh, nq), in_specs=specs, out_specs=ospecs, out_shape=outs,
                          compiler_params=_params(("parallel", "parallel", "arbitrary")))(*ins)


def _pattern_count(delta):
    n = jnp.zeros(delta.shape, jnp.int32)
    for window, dil in DIL_PATTERNS:
        n = n + ((delta % dil == 0) & (jnp.abs(delta) <= window // 2)).astype(jnp.int32)
    return n


def _t5_bucket(rel):
    nb = REL_BUCKETS // 2
    max_exact = nb // 2
    ret = jnp.where(rel > 0, nb, 0)
    n = jnp.abs(rel)
    nf = jnp.maximum(n, 1).astype(F32)
    large = max_exact + (jnp.log(nf / max_exact) / math.log(REL_MAX_DIST / max_exact) * (nb - max_exact)).astype(jnp.int32)
    large = jnp.minimum(large, nb - 1)
    return ret + jnp.where(n < max_exact, n, large)


def _bias_windows(rel_bias, S):
    tq = min(ATTN_TQ, S)
    nq = S // tq
    n = nq * (S + tq)
    delta = (jnp.arange(S + tq)[None, :] - (jnp.arange(nq)[:, None] + 1) * tq).reshape(n)
    count = _pattern_count(delta)
    onehot = (_t5_bucket(delta)[None, :] == jnp.arange(REL_BUCKETS)[:, None]).astype(F32)
    extra = jnp.where(count > 0, jnp.log(jnp.maximum(count, 1).astype(F32)), MASKED).reshape(1, n)
    live = (count > 0).astype(F32).reshape(1, n)

    def body(t_ref, oh_ref, live_ref, extra_ref, o_ref):
        o_ref[...] = _nn(t_ref[...], oh_ref[...], HIGHEST) * live_ref[...] + extra_ref[...]

    val = pl.pallas_call(body, name="bias_windows", out_shape=jax.ShapeDtypeStruct((N_HEADS, n), F32),
                         compiler_params=_params())(rel_bias.T, onehot, live, extra)
    return val.reshape(N_HEADS, nq, 1, S + tq)


def _bias_fold(dwin, S, name):
    B, H, nq = dwin.shape[:3]
    tq = min(ATTN_TQ, S)
    n = nq * (S + tq)
    delta = (jnp.arange(S + tq)[None, :] - (tq - 1) - jnp.arange(nq)[:, None] * tq).reshape(n)
    onehot = (_t5_bucket(delta)[:, None] == jnp.arange(128)[None, :]).astype(F32)

    def body(d_ref, oh_ref, o_ref):
        tot = d_ref[0]
        for b in range(1, B):
            tot = tot + d_ref[b]
        o_ref[...] = _nn(tot, oh_ref[...], HIGHEST)

    out = pl.pallas_call(body, name=name, out_shape=jax.ShapeDtypeStruct((H, 128), F32), compiler_params=_params())(dwin.reshape(B, H, n), onehot)
    return out[:, :REL_BUCKETS].T


def _rope_tables(S):
    half = 16
    freqs = ROPE_THETA ** (-jnp.arange(half, dtype=F32) / half)
    t = jnp.arange(S)
    ang_r = (t // GRID_W).astype(F32)[:, None] * freqs[None, :]
    ang_c = (t % GRID_W).astype(F32)[:, None] * freqs[None, :]
    cos = jnp.concatenate([jnp.cos(ang_r)] * 2 + [jnp.cos(ang_c)] * 2, axis=1)
    sin = jnp.concatenate([-jnp.sin(ang_r), jnp.sin(ang_r), -jnp.sin(ang_c), jnp.sin(ang_c)], axis=1)
    return jnp.tile(cos, (1, N_HEADS)), jnp.tile(sin, (1, N_HEADS))


def _conv_act(c, g, b):
    n, rs = _ln_rows(c)
    t = n * g + b
    return t * _sigmoid(t), n, rs, t


def _mix_fwd(ya, ob, c, od, gain, lng, lnb, B, S, name):
    T = B * S
    tm = min(ROW_TILE, S)
    ns = S // tm

    def body(ya_ref, ob_ref, c_ref, od_ref, g_ref, lg_ref, lb_ref, o_ref):
        yc = _conv_act(c_ref[...], lg_ref[...], lb_ref[...])[0]
        ys = [ya_ref[...], _load_heads(ob_ref, N_HEADS), yc, _load_heads(od_ref, N_HEADS)]
        for m, y in enumerate(ys):
            r = lax.rsqrt(jnp.mean(y * y, axis=-1, keepdims=True) + RMS_EPS)
            o_ref[:, 512 * m:512 * (m + 1)] = (y * r * g_ref[:, 512 * m:512 * (m + 1)]).astype(BF16)

    row = pl.BlockSpec((tm, 512), lambda b, i: (b * ns + i, 0))
    heads = pl.BlockSpec((None, N_HEADS, tm, HEAD_DIM), lambda b, i: (b, 0, i, 0))
    vec = pl.BlockSpec((1, 512), lambda b, i: (0, 0))
    return pl.pallas_call(
        body, name=name, grid=(B, ns), in_specs=[row, heads, row, heads, pl.BlockSpec((1, 2048), lambda b, i: (0, 0)), vec, vec],
        out_specs=pl.BlockSpec((tm, 2048), lambda b, i: (b * ns + i, 0)), out_shape=jax.ShapeDtypeStruct((T, 2048), BF16),
        compiler_params=_params(("parallel", "parallel")),
    )(ya, ob, c, od, gain, lng, lnb)


def _mix_bwd(ya, ob, c, od, dycat, gain, lng, lnb, B, S, name):
    T = B * S
    tm = min(ROW_TILE, S)
    ns = S // tm

    def body(ya_ref, ob_ref, c_ref, od_ref, dy_ref, g_ref, lg_ref, lb_ref, dya_ref, dob_ref, dc_ref, dod_ref, dg_ref, dlg_ref, dlb_ref, dcb_ref):
        @pl.when((pl.program_id(0) == 0) & (pl.program_id(1) == 0))
        def _():
            for ref in (dg_ref, dlg_ref, dlb_ref, dcb_ref):
                ref[...] = jnp.zeros_like(ref)

        yc, n, rs, t = _conv_act(c_ref[...], lg_ref[...], lb_ref[...])
        ys = [ya_ref[...], _load_heads(ob_ref, N_HEADS), yc, _load_heads(od_ref, N_HEADS)]
        outs = [dya_ref, dob_ref, None, dod_ref]
        for m, y in enumerate(ys):
            cols = slice(512 * m, 512 * (m + 1))
            r = lax.rsqrt(jnp.mean(y * y, axis=-1, keepdims=True) + RMS_EPS)
            yh = y * r
            dh = dy_ref[:, cols]
            dg_ref[:, cols] += jnp.sum(dh * yh, axis=0, keepdims=True)
            dyh = dh * g_ref[:, cols]
            dyv = r * (dyh - yh * jnp.mean(dyh * yh, axis=-1, keepdims=True))
            if m == 0:
                outs[m][...] = dyv
            elif m == 2:
                sg = _sigmoid(t)
                dt = dyv * sg * (1.0 + t * (1.0 - sg))
                dlg_ref[...] += jnp.sum(dt * n, axis=0, keepdims=True)
                dlb_ref[...] += jnp.sum(dt, axis=0, keepdims=True)
                dn = dt * lg_ref[...]
                dc = rs * (dn - jnp.mean(dn, axis=-1, keepdims=True) - n * jnp.mean(dn * n, axis=-1, keepdims=True))
                dc_ref[...] = dc
                dcb_ref[...] += jnp.sum(dc, axis=0, keepdims=True)
            else:
                _store_heads(outs[m], dyv, N_HEADS)

    row = pl.BlockSpec((tm, 512), lambda b, i: (b * ns + i, 0))
    heads = pl.BlockSpec((None, N_HEADS, tm, HEAD_DIM), lambda b, i: (b, 0, i, 0))
    vec = pl.BlockSpec((1, 2048), lambda b, i: (0, 0))
    flat = jax.ShapeDtypeStruct((T, 512), F32)
    hm = jax.ShapeDtypeStruct((B, N_HEADS, S, HEAD_DIM), F32)
    v512 = pl.BlockSpec((1, 512), lambda b, i: (0, 0))
    s512 = jax.ShapeDtypeStruct((1, 512), F32)
    return pl.pallas_call(
        body, name=name, grid=(B, ns), in_specs=[row, heads, row, heads, pl.BlockSpec((tm, 2048), lambda b, i: (b * ns + i, 0)), vec, v512, v512],
        out_specs=[row, heads, row, heads, vec, v512, v512, v512],
        out_shape=[flat, hm, flat, hm, jax.ShapeDtypeStruct((1, 2048), F32), s512, s512, s512],
        compiler_params=_params(("arbitrary", "arbitrary")),
    )(ya, ob, c, od, dycat, gain, lng, lnb)


def _ffn_down(gate, up, w_down, res, name, norm=None, target=None):
    J, T, n = gate.shape
    N = w_down.shape[2]
    tm = min(256, T)
    steps = T // tm

    def body(g_ref, u_ref, w_ref, r_ref, *rest):
        x_ref = rest[0] if (norm is not None or target is not None) else None
        o_ref, act_ref = rest[-3:-1] if x_ref is not None else rest[-2:]
        acc = None
        for j in range(J):
            g = g_ref[j].astype(F32)
            a = (g * _sigmoid(g) * u_ref[j].astype(F32)).astype(BF16)
            act_ref[j] = a
            d = _nn(a, w_ref[j])
            acc = d if acc is None else acc + d
        y = acc + r_ref[...]
        if target is None:
            o_ref[...] = y
            if norm is not None:
                rest[-1][...] = _rms_rows(y, x_ref[...])
            return
        err = y - x_ref[...]
        o_ref[...] = err * (1.0 / N)
        loss_ref, i = rest[-1], pl.program_id(0)

        @pl.when(i == 0)
        def _():
            loss_ref[...] = jnp.zeros_like(loss_ref)

        loss_ref[...] += jnp.sum(err * err)

        @pl.when(i == steps - 1)
        def _():
            loss_ref[...] = loss_ref[...] * (0.5 / N)

    gu = pl.BlockSpec((J, tm, n), lambda i: (0, i, 0))
    row = pl.BlockSpec((tm, N), lambda i: (i, 0))
    ins, specs = [gate, up, w_down, res], [gu, gu, pl.BlockSpec((J, n, N), lambda i: (0, 0, 0)), row]
    outs, ospecs = [jax.ShapeDtypeStruct((T, N), F32), jax.ShapeDtypeStruct((J, T, n), BF16)], [row, gu]
    if target is not None:
        ins, specs = ins + [target], specs + [row]
        outs, ospecs = outs + [jax.ShapeDtypeStruct((8, 128), F32)], ospecs + [pl.BlockSpec((8, 128), lambda i: (0, 0))]
    elif norm is not None:
        ins, specs = ins + [norm.reshape(1, N)], specs + [pl.BlockSpec((1, N), lambda i: (0, 0))]
        outs, ospecs = outs + [jax.ShapeDtypeStruct((T, N), BF16)], ospecs + [row]
    return pl.pallas_call(body, name=name, grid=(steps,), in_specs=specs, out_specs=ospecs, out_shape=outs,
                          compiler_params=_params(("arbitrary" if target is not None else "parallel",)))(*ins)


def _ffn_down_dx(dx, w_down, gate, up, name):
    J, n, D = w_down.shape
    T = dx.shape[0]
    tm = min(512, T)

    def body(dx_ref, w_ref, g_ref, u_ref, dg_ref, du_ref):
        d = _nt(dx_ref[...].astype(BF16), w_ref[...])
        g = g_ref[...].astype(F32)
        s = _sigmoid(g)
        dg_ref[...] = (d * u_ref[...].astype(F32) * s * (1.0 + g * (1.0 - s))).astype(BF16)
        du_ref[...] = (d * g * s).astype(BF16)

    blk = pl.BlockSpec((None, tm, n), lambda j, i: (j, i, 0))
    shape = jax.ShapeDtypeStruct((J, T, n), BF16)
    return pl.pallas_call(body, name=name, grid=(J, T // tm),
                          in_specs=[pl.BlockSpec((tm, D), lambda j, i: (i, 0)), pl.BlockSpec((None, n, D), lambda j, i: (j, 0, 0)), blk, blk],
                          out_specs=[blk, blk], out_shape=[shape, shape], compiler_params=_params(("parallel", "parallel")))(dx, w_down, gate, up)


def _row_tile(R):
    best = R
    for cand in range(16, min(R, 272) + 1, 16):
        if R % cand == 0:
            best = cand
    return best


def _adamw(w, m, v, stack, name, layer=None, prev=None):
    n, R, C = stack.shape
    tm = _row_tile(R)
    nb = R // tm
    off = 0 if layer is None else layer * nb
    c1 = 1.0 - ADAM_B1 ** ADAM_STEP
    c2 = 1.0 - ADAM_B2 ** ADAM_STEP

    def body(w_ref, m_ref, v_ref, s_ref, *rest):
        g_ref, d_ref, mo_ref, vo_ref = rest[-4:]
        g = s_ref[0].astype(F32)
        for k in range(1, n):
            g = g + s_ref[k].astype(F32)
        mn = ADAM_B1 * m_ref[...] + (1.0 - ADAM_B1) * g
        vn = ADAM_B2 * v_ref[...] + (1.0 - ADAM_B2) * (g * g)
        g_ref[...] = g
        mo_ref[...] = mn
        vo_ref[...] = vn
        d_ref[...] = -ADAM_LR * ((mn / c1) / (jnp.sqrt(vn / c2) + ADAM_EPS) + ADAM_WD * w_ref[...])

    blk = pl.BlockSpec((tm, C), lambda i: (i + off, 0))
    ins = [w, m, v, stack]
    specs = [blk, blk, blk, pl.BlockSpec((n, tm, C), lambda i: (0, i, 0))]
    aliases = {}
    if prev is not None:
        ins += list(prev)
        specs += [pl.BlockSpec(memory_space=pl.ANY)] * 4
        aliases = {4 + t: t for t in range(4)}
    shape = jax.ShapeDtypeStruct(w.shape, F32)
    return pl.pallas_call(body, name=name, grid=(nb,), in_specs=specs, out_specs=[blk] * 4, out_shape=[shape] * 4,
                          input_output_aliases=aliases, compiler_params=_params(("parallel",)))(*ins)


HBM = pl.BlockSpec(memory_space=pltpu.HBM)
SEM = pl.BlockSpec(memory_space=pltpu.SEMAPHORE)
EFFECT = pltpu.SideEffectType.DATAFLOW_SIDE_EFFECTING


PEERS = {"scatter": (1, 2, 3, 4, 5, 6, 7), "gather": (1, 2, 3, 4, 5, 6, 7), "chips": (1, 2, 4, 6), "forward": (2, 4, 6)}


def _spread_copies(srcs, lands, send_sems, recv_sems, local_sems, kind, waiting):
    x, y, c = lax.axis_index("x"), lax.axis_index("y"), lax.axis_index("c")
    me = 4 * x + 2 * y + c

    def peer(bits):
        dev = (1 - x if bits & 4 else x, 1 - y if bits & 2 else y, 1 - c if bits & 1 else c)
        return dev, 4 * dev[0] + 2 * dev[1] + dev[2]

    plan = PEERS[kind]
    remote, local = [], []
    for a, l in enumerate(lands):
        for d, bits in enumerate(plan):
            dev, pid = peer(bits)
            if kind == "forward":
                src, dst, dev = l.at[pid], l.at[peer(bits | 1)[1] if waiting else pid], peer(1)[0]
            else:
                src, dst = (srcs[a].at[pid] if kind == "scatter" else srcs[a]), l.at[pid if waiting else me]
            remote.append(pltpu.make_async_remote_copy(
                src_ref=src, dst_ref=dst, send_sem=send_sems.at[a * len(plan) + d], recv_sem=recv_sems.at[a * len(plan) + d],
                device_id=dev, device_id_type=MESH_ID))
        if kind != "forward":
            local.append(pltpu.make_async_copy(srcs[a].at[me] if kind == "scatter" else srcs[a], l.at[me], local_sems.at[a]))
    return remote, local


def _spread_start(srcs, kind, name, after=None, lands=None):
    if kind == "forward":
        srcs = []
    else:
        shapes = [a.shape if kind == "scatter" else (N_DEV,) + a.shape for a in srcs]
        lands = [lax.empty(shp, a.dtype) for shp, a in zip(shapes, srcs)]
    ns, nl, per = len(srcs), len(lands), len(PEERS[kind])
    extra = [] if after is None else [after]
    sem_shapes = [pltpu.SemaphoreType.DMA((nl * per,))] * 2 + ([pltpu.SemaphoreType.DMA((nl,))] if ns else [])

    def body(*refs):
        src_refs, land_refs = refs[:ns], refs[ns:ns + nl]
        sems = refs[ns + nl + len(extra):ns + nl + len(extra) + len(sem_shapes)]
        remote, local = _spread_copies(src_refs, land_refs, sems[0], sems[1], sems[2] if ns else None, kind, False)
        for cp in remote + local:
            cp.start()
        refs[-1][...] = jnp.zeros((8, 128), F32)

    outs = pl.pallas_call(
        body, name=name,
        out_shape=(*sem_shapes, *[pltpu.HBM(a.shape, a.dtype) for a in srcs + lands], jax.ShapeDtypeStruct((8, 128), F32)),
        in_specs=[HBM] * (ns + nl) + [pl.BlockSpec(memory_space=pl.ANY)] * len(extra),
        out_specs=(*[SEM] * len(sem_shapes), *[HBM] * (ns + nl), pl.BlockSpec(memory_space=pltpu.VMEM)),
        input_output_aliases={i: len(sem_shapes) + i for i in range(ns + nl)},
        compiler_params=pltpu.CompilerParams(has_side_effects=EFFECT),
    )(*[pltpu.with_memory_space_constraint(a, pltpu.HBM) for a in srcs + lands], *extra)
    k = len(sem_shapes)
    return outs[:k], list(outs[k:k + ns]), list(outs[k + ns:k + ns + nl]), outs[-1]


def _spread_wait(sems, srcs, lands, after, kind, name):
    ns, nl = len(srcs), len(lands)
    after = list(after) if isinstance(after, (list, tuple)) else [after]

    def body(*refs):
        src_refs, land_refs = refs[:ns], refs[ns:ns + nl]
        s = refs[ns + nl:ns + nl + len(sems)]
        remote, local = _spread_copies(src_refs, land_refs, s[0], s[1], s[2] if ns else None, kind, True)
        for cp in remote:
            cp.wait_send()
            cp.wait_recv()
        for cp in local:
            cp.wait()

    outs = pl.pallas_call(
        body, name=name, out_shape=tuple(pltpu.HBM(a.shape, a.dtype) for a in srcs + lands),
        in_specs=[HBM] * (ns + nl) + [SEM] * len(sems) + [pl.BlockSpec(memory_space=pl.ANY)] * len(after), out_specs=tuple([HBM] * (ns + nl)),
        input_output_aliases={i: i for i in range(ns + nl)}, compiler_params=pltpu.CompilerParams(has_side_effects=EFFECT),
    )(*srcs, *lands, *sems, *after)
    return list(outs[ns:])


SMALL = ("rel_bias", "norm1_g", "sgu_w", "sgu_b", "dil_qn_g", "dil_kn_g", "conv_w", "conv_b", "conv_ln_g", "conv_ln_b",
         "gqa_qn_g", "gqa_kn_g", "mix_norm_g", "norm2_g")
LARGE = ("w_in", "w_out", "w_gate", "w_up", "w_down")
EARLY = tuple(k for k in SMALL if k != "norm1_g")


def _local_step(x, target, p, B, S, fetch, emit, mid, early):
    T = B * S
    rope = _rope_tables(S)
    win = _bias_windows(p["rel_bias"], S)
    tile8 = lambda g: jnp.tile(g.reshape(1, HEAD_DIM), (1, N_HEADS))
    cols_b = (COL_BQ, COL_BK, COL_BV)
    cols_d = (COL_DQ, COL_DK128, COL_DV128)
    saved = []
    for l in range(DEPTH):
        s = {"x": x}
        s["ws"] = p["sgu_w"][l].astype(BF16)
        s["bias"] = jnp.repeat(p["sgu_b"][l].T, HEAD_DIM, axis=1)
        s["h"] = _rms_fwd(x, p["norm1_g"][l], f"rms1_fwd_{l}") if l == 0 else h_next
        s["win"] = fetch(l, "in", s["h"])
        s["cw"] = jnp.pad(s["win"]["conv_w"], ((0, 1), (0, 0))).reshape(32, 1, 512)
        z = s["z"] = _matmul(s["h"], s["win"]["w_in"], "nt", f"in_proj_{l}", tk=D_MODEL)
        s["bias"] = s["bias"] + mid(l, z)
        s["ya"] = _sgu_fwd(z, s["ws"], s["bias"], f"sgu_fwd_{l}")
        s["c"] = _conv_fwd1(z, s["cw"], p["conv_b"][l].reshape(1, 512), B, S, f"conv_fwd_{l}")
        s["ln"] = (p["conv_ln_g"][l].reshape(1, 512), p["conv_ln_b"][l].reshape(1, 512))
        s["gb"] = (tile8(p["dil_qn_g"][l]), tile8(p["dil_kn_g"][l]))
        s["gd"] = (tile8(p["gqa_qn_g"][l]), tile8(p["gqa_kn_g"][l])[:, :KV_WIDTH])
        s["qkv_b"] = _prep_fwd(z, *s["gb"], None, B, S, N_HEADS, cols_b, f"prep_b_fwd_{l}")
        s["qkv_d"] = _prep_fwd(z, *s["gd"], rope, B, S, KV_HEADS, cols_d, f"prep_d_fwd_{l}")
        s["ob"] = _attn_fwd(*s["qkv_b"], win, f"attn_b_fwd_{l}")
        s["od"] = _attn_fwd(*s["qkv_d"], None, f"attn_d_fwd_{l}", tq=GQA_TQ)
        s["gmix"] = p["mix_norm_g"][l].reshape(1, 2048)
        s["ycat"] = _mix_fwd(s["ya"], s["ob"], s["c"], s["od"], s["gmix"], *s["ln"], B, S, f"mix_fwd_{l}")
        s["wout"] = fetch(l, "out", s["ycat"])["w_out"]
        x1, s["h2"] = _matmul(s["ycat"], s["wout"], "nn", f"out_proj_{l}", res=x, tk=D_MODEL, norm=p["norm2_g"][l])
        s["x1"] = x1
        s["ffn"] = fetch(l, "ffn", s["h2"])
        s["gate"] = _mm_shard_out(s["h2"], s["ffn"]["w_gate"], "nt", f"ffn_gate_{l}", out_dtype=BF16, tm=1024)
        s["up"] = _mm_shard_out(s["h2"], s["ffn"]["w_up"], "nt", f"ffn_up_{l}", out_dtype=BF16, tm=1024)
        if l + 1 < DEPTH:
            x, s["act"], h_next = _ffn_down(s["gate"], s["up"], s["ffn"]["w_down"], x1, f"ffn_down_{l}", norm=p["norm1_g"][l + 1])
        else:
            dx, s["act"], loss_blk = _ffn_down(s["gate"], s["up"], s["ffn"]["w_down"], x1, f"ffn_down_{l}", target=target)
        saved.append(s)

    g = {k: [None] * DEPTH for k in SMALL if k != "rel_bias"}
    dwin_total = None
    for l in reversed(range(DEPTH)):
        s = saved[l]
        z, ffn = s["z"], s["ffn"]
        dgate, dup = _ffn_down_dx(dx, ffn["w_down"], s["gate"], s["up"], f"ffn_down_dx_{l}")
        tok = emit(l, "w_down", _mm_shard_m(s["act"], dx, f"ffn_down_dw_{l}", out_dtype=BF16, tm=FFN_BLOCK, tn=512, tk=T))
        tok += emit(l, "w_gate", _mm_shard_m(dgate, s["h2"], f"ffn_gate_dw_{l}", out_dtype=BF16, tm=FFN_BLOCK, tn=512, tk=T))
        tok += emit(l, "w_up", _mm_shard_m(dup, s["h2"], f"ffn_up_dw_{l}", out_dtype=BF16, tm=FFN_BLOCK, tn=512, tk=T))
        dh2 = _mm_shard_k([(dgate, ffn["w_gate"]), (dup, ffn["w_up"])], "nn", f"ffn_up_dx_{l}", tn=512, fold=FFN_GROUPS)
        dycat, dx1, dg2 = _out_proj_dx(dh2, s["x1"], p["norm2_g"][l] + tok, dx, s["wout"], f"out_proj_dx_{l}")
        g["norm2_g"][l] = dg2[0]
        tok = emit(l, "w_out", _matmul(s["ycat"], dx1, "tn", f"out_proj_dw_{l}", out_dtype=BF16, tn=1024, tk=T))
        dya, dob, dc, dod, dgm, dlg, dlb, dcb = _mix_bwd(s["ya"], s["ob"], s["c"], s["od"], dycat, s["gmix"] + tok, *s["ln"], B, S, f"mix_bwd_{l}")
        g["mix_norm_g"][l] = dgm[0]
        dz_a, dws, dbias = _sgu_bwd(z, dya, s["ws"], jnp.swapaxes(s["ws"], 1, 2), s["bias"], f"sgu_bwd_{l}")
        g["sgu_w"][l] = dws
        g["sgu_b"][l] = dbias.reshape(128, 8, HEAD_DIM).sum(-1).T
        g["conv_ln_g"][l], g["conv_ln_b"][l], g["conv_b"][l] = dlg[0], dlb[0], dcb[0]
        dz_ca, dz_cg, dcw = _conv_bwd2(z, dc, s["cw"], B, S, f"conv_bwd_{l}")
        g["conv_w"][l] = dcw.reshape(32, 512)[:CONV_WIDTH]
        dq, dk, dv, dwin = _attn_bwd(*s["qkv_b"], s["ob"], dob, win, f"attn_b_bwd_{l}")
        dwin_total = dwin if dwin_total is None else dwin_total + dwin
        dz_b, dgq, dgk = _prep_bwd(z, dq, dk, dv, *s["gb"], None, B, S, N_HEADS, cols_b, f"prep_b_bwd_{l}")
        g["dil_qn_g"][l] = dgq.reshape(N_HEADS, HEAD_DIM).sum(0)
        g["dil_kn_g"][l] = dgk.reshape(N_HEADS, HEAD_DIM).sum(0)
        dq, dk, dv = _attn_bwd(*s["qkv_d"], s["od"], dod, None, f"attn_d_bwd_{l}", tq=GQA_TQ)
        dz_d, dgq, dgk = _prep_bwd(z, dq, dk, dv, *s["gd"], rope, B, S, KV_HEADS, cols_d, f"prep_d_bwd_{l}")
        g["gqa_qn_g"][l] = dgq.reshape(N_HEADS, HEAD_DIM).sum(0)
        g["gqa_kn_g"][l] = dgk.reshape(KV_HEADS, HEAD_DIM).sum(0)
        dz = [dz_a, dz_b, dz_ca, dz_cg, dz_d]
        tok = jnp.zeros((), F32)
        if l == 0:
            done = {k: jnp.stack(v) for k, v in g.items() if k != "norm1_g"}
            done["rel_bias"] = _bias_fold(dwin_total, S, "bias_fold")
            tok = early(done)
        tok += emit(l, "w_in", _in_proj_dw(dz, s["h"], f"in_proj_dw_{l}"))
        dx, dg1 = _in_proj_dx(dz, s["win"]["w_in"], s["x"], p["norm1_g"][l] + tok, dx1, f"in_proj_dx_{l}")
        g["norm1_g"][l] = dg1[0]

    return loss_blk[0, 0], dx, jnp.stack(g["norm1_g"])


GROUPS = {"in": ("w_in",), "out": ("w_out",), "ffn": ("w_gate", "w_up", "w_down")}
COL_SHARDED = ("w_in", "w_gate", "w_up")


def kernel(x, rel_bias, norm1_g, w_in, sgu_w, sgu_b, dil_qn_g, dil_kn_g, conv_w, conv_b, conv_ln_g, conv_ln_b, gqa_qn_g, gqa_kn_g, mix_norm_g, w_out, norm2_g, w_gate, w_up, w_down, loss_target, m_rel_bias, m_norm1_g, m_w_in, m_sgu_w, m_sgu_b, m_dil_qn_g, m_dil_kn_g, m_conv_w, m_conv_b, m_conv_ln_g, m_conv_ln_b, m_gqa_qn_g, m_gqa_kn_g, m_mix_norm_g, m_w_out, m_norm2_g, m_w_gate, m_w_up, m_w_down, v_rel_bias, v_norm1_g, v_w_in, v_sgu_w, v_sgu_b, v_dil_qn_g, v_dil_kn_g, v_conv_w, v_conv_b, v_conv_ln_g, v_conv_ln_b, v_gqa_qn_g, v_gqa_kn_g, v_mix_norm_g, v_w_out, v_norm2_g, v_w_gate, v_w_up, v_w_down):
    w = dict(rel_bias=rel_bias, norm1_g=norm1_g, w_in=w_in, sgu_w=sgu_w, sgu_b=sgu_b, dil_qn_g=dil_qn_g, dil_kn_g=dil_kn_g, conv_w=conv_w,
             conv_b=conv_b, conv_ln_g=conv_ln_g, conv_ln_b=conv_ln_b, gqa_qn_g=gqa_qn_g, gqa_kn_g=gqa_kn_g, mix_norm_g=mix_norm_g,
             w_out=w_out, norm2_g=norm2_g, w_gate=w_gate, w_up=w_up, w_down=w_down)
    m = dict(rel_bias=m_rel_bias, norm1_g=m_norm1_g, w_in=m_w_in, sgu_w=m_sgu_w, sgu_b=m_sgu_b, dil_qn_g=m_dil_qn_g, dil_kn_g=m_dil_kn_g,
             conv_w=m_conv_w, conv_b=m_conv_b, conv_ln_g=m_conv_ln_g, conv_ln_b=m_conv_ln_b, gqa_qn_g=m_gqa_qn_g, gqa_kn_g=m_gqa_kn_g,
             mix_norm_g=m_mix_norm_g, w_out=m_w_out, norm2_g=m_norm2_g, w_gate=m_w_gate, w_up=m_w_up, w_down=m_w_down)
    v = dict(rel_bias=v_rel_bias, norm1_g=v_norm1_g, w_in=v_w_in, sgu_w=v_sgu_w, sgu_b=v_sgu_b, dil_qn_g=v_dil_qn_g, dil_kn_g=v_dil_kn_g,
             conv_w=v_conv_w, conv_b=v_conv_b, conv_ln_g=v_conv_ln_g, conv_ln_b=v_conv_ln_b, gqa_qn_g=v_gqa_qn_g, gqa_kn_g=v_gqa_kn_g,
             mix_norm_g=v_mix_norm_g, w_out=v_w_out, norm2_g=v_norm2_g, w_gate=v_w_gate, w_up=v_w_up, w_down=v_w_down)
    names = list(w)
    B, S, D = x.shape
    T = B * S
    me = 4 * lax.axis_index("x") + 2 * lax.axis_index("y") + lax.axis_index("c")

    view = lambda a, k: jnp.swapaxes(a, 1, 2) if k in COL_SHARDED else a
    bf = {k: view(w[k], k).astype(BF16) for k in LARGE}
    spreads, forwards = {}, {}

    def start_gather(l, group, after=None):
        srcs = [bf[k][l] for k in GROUPS[group]] + ([conv_w[l]] if group == "in" else [])
        spreads[l, group] = _spread_start(srcs, "chips", f"gather_{group}_{l}_start", after)
        return spreads[l, group][3][0, 0]

    def forward(l, group, after):
        sems, srcs, lands, _ = spreads[l, group]
        lands = _spread_wait(sems, srcs, lands, after, "chips", f"gather_{group}_{l}_wait")
        forwards[l, group] = _spread_start(None, "forward", f"forward_{group}_{l}_start", lands=lands)
        return forwards[l, group][3]

    def landed(l, group, after):
        sems, _, lands, _ = forwards[l, group]
        return _spread_wait(sems, [], lands, after, "forward", f"forward_{group}_{l}_wait")

    tok0 = start_gather(0, "in") + start_gather(0, "out") + start_gather(0, "ffn")
    small = {k: w[k] for k in SMALL}
    small["norm1_g"] = norm1_g.at[0].add(tok0)

    def mid(l, z):
        if l > 0:
            return jnp.zeros((), F32)
        return start_gather(1, "in", z) + start_gather(1, "out", z) + start_gather(1, "ffn", z)

    def fetch(l, group, after):
        if group == "in":
            tok = forward(0, "in", after) if l == 0 else after
        elif group == "out":
            tok = forward(l, "ffn", [after, forward(l, "out", after)])
        else:
            tok = forward(1, "in", after) if l == 0 else after
        got = dict(zip(GROUPS[group] + ("conv_w",), landed(l, group, [after, tok])))
        if group == "in":
            got["w_in"] = got["w_in"].reshape(IN_WIDTH, D)
            got["conv_w"] = jnp.transpose(got["conv_w"], (1, 0, 2)).reshape(CONV_WIDTH, 512)
        if group == "out":
            got["w_out"] = got["w_out"].reshape(D, D)
        if group == "ffn":
            got = {k: a.reshape(FFN_GROUPS, -1, D) for k, a in got.items()}
        return got

    scatters = {}

    def emit(l, k, dw):
        dw = dw.reshape(N_DEV, -1, D)
        scatters[l, k] = _spread_start([dw], "scatter", f"scatter_{k}_{l}_start")
        return scatters[l, k][3][0, 0]

    flat2 = lambda a: a.reshape(-1, a.shape[-1])
    small_spread = []

    def early(done):
        small_spread.append(_spread_start([flat2(done[k]) for k in EARLY], "gather", "gather_small_grads_start"))
        return small_spread[0][3][0, 0]

    loss_part, dx, dnorm1 = _local_step(x.reshape(T, D), loss_target.reshape(T, D), small, B, S, fetch, emit, mid, early)
    loss = lax.psum(loss_part, ("x", "y", "c"))

    out_g, out_d, out_m, out_v = {}, {}, {}, {}

    def update_large(k, after):
        shp = view(w[k], k).shape
        two_d = lambda a: view(a, k).reshape(-1, shp[-1])
        res = None
        for l in reversed(range(DEPTH)):
            sems, srcs, lands, _ = scatters[l, k]
            stack = _spread_wait(sems, srcs, lands, after, "scatter", f"scatter_{k}_{l}_wait")[0]
            res = _adamw(two_d(w[k]), two_d(m[k]), two_d(v[k]), stack.reshape(N_DEV, -1, shp[-1]), f"adamw_{k}_{l}", layer=l, prev=res)
        out_g[k], out_d[k], out_m[k], out_v[k] = [view(a.reshape(shp), k) for a in res]
        return res[0]

    late_sems, late_srcs, late_lands, late_tok = _spread_start([flat2(dnorm1)], "gather", "gather_norm1_grad_start")
    after = [dx, late_tok]
    for k in ("w_down", "w_gate", "w_up", "w_out"):
        after = update_large(k, after)
    sems, srcs, lands, _ = small_spread[0]
    stacks = dict(zip(EARLY, _spread_wait(sems, srcs, lands, after, "gather", "gather_small_grads_wait")))
    stacks["norm1_g"] = _spread_wait(late_sems, late_srcs, late_lands, after, "gather", "gather_norm1_grad_wait")[0]
    for k in SMALL:
        stack = stacks[k]
        if k == "conv_w":
            stack = lax.dynamic_slice_in_dim(stack, me * (512 // N_DEV), 512 // N_DEV, axis=2)
        res = _adamw(flat2(w[k]), flat2(m[k]), flat2(v[k]), stack, f"adamw_{k}")
        out_g[k], out_d[k], out_m[k], out_v[k] = [a.reshape(w[k].shape) for a in res]
        after = res[0]
    update_large("w_in", after)

    return (loss, dx.reshape(B, S, D), *[out_g[k] for k in names], *[out_d[k] for k in names],
            *[out_m[k] for k in names], *[out_v[k] for k in names])
```

```python
import functools
import math

import numpy as np
import jax
import jax.numpy as jnp
from jax import lax
from jax.experimental import pallas as pl
from jax.experimental.pallas import tpu as pltpu

F32 = jnp.float32
BF16 = jnp.bfloat16
HIGHEST = lax.Precision.HIGHEST
MESH_ID = pl.DeviceIdType.MESH

D_MODEL = 2048
DEPTH = 2
HEAD_DIM = 64
GROUP_WIDTH = 512
N_HEADS = 8
KV_HEADS = 2
KV_WIDTH = 128
SGU_CHUNK = 128
CONV_WIDTH = 31
CONV_PAD = 16
GRID_W = 64
ROPE_THETA = 10000.0
REL_BUCKETS = 32
REL_MAX_DIST = 1024
DIL_PATTERNS = ((128, 1), (512, 4), (2048, 16))
FFN_HIDDEN = 5632
IN_WIDTH = 4352
RMS_EPS = 1e-6
LN_EPS = 1e-5
MASKED = -1e30
N_DEV = 8

ADAM_LR = 0.001
ADAM_B1 = 0.9
ADAM_B2 = 0.999
ADAM_EPS = 1e-08
ADAM_WD = 0.01
ADAM_STEP = 10

COL_AU, COL_AV, COL_BQ, COL_BK, COL_BV, COL_CA, COL_CG, COL_DQ = range(8)
COL_DK128, COL_DV128 = 32, 33

VMEM_LIMIT = 56 * 1024 * 1024
FFN_GROUPS = 2
FFN_BLOCK = 1408
ATTN_TQ = 256
GQA_TQ = 512
ROW_TILE = 512


def _params(sem=None, vmem=VMEM_LIMIT):
    return pltpu.CompilerParams(dimension_semantics=sem, vmem_limit_bytes=vmem)


def _dot(a, b, dims, precision=None):
    return lax.dot_general(a, b, (dims, ((), ())), precision=precision, preferred_element_type=F32)


def _nn(a, b, precision=None):
    return _dot(a, b, ((1,), (0,)), precision)


def _nt(a, b):
    return _dot(a, b, ((1,), (1,)))


def _tn(a, b):
    return _dot(a, b, ((0,), (0,)))


DIMS = {"nn": ((1,), (0,)), "nt": ((1,), (1,)), "tn": ((0,), (0,))}


def _pick(n, cands):
    for c in cands:
        if n % c == 0:
            return c
    return n


def _mm_call(name, mode, pairs, specs, o_spec, out_sds, grid, acc_shape, res=None, fold=None, norm=None):
    npair, nk, dims = len(pairs), grid[2], DIMS[mode]

    def body(*refs):
        ab = refs[:2 * npair]
        at = 2 * npair
        r_ref = refs[at] if res is not None else None
        at += res is not None
        g_ref = refs[at] if norm is not None else None
        at += norm is not None
        o_ref = refs[at]
        h_ref = refs[at + 1] if norm is not None else None
        part = None
        for t in range(npair):
            for s in ([None] if fold is None else range(fold)):
                a_blk = ab[2 * t][...] if s is None else ab[2 * t][s]
                b_blk = ab[2 * t + 1][...] if s is None else ab[2 * t + 1][s]
                d = _dot(a_blk.astype(BF16), b_blk.astype(BF16), dims)
                part = d if part is None else part + d

        def finish(r):
            if r_ref is not None:
                r = r + r_ref[...]
            o_ref[...] = r.astype(o_ref.dtype)
            if h_ref is not None:
                h_ref[...] = _rms_rows(r, g_ref[...])

        if nk == 1:
            finish(part)
            return
        acc, k = refs[-1], pl.program_id(2)

        @pl.when(k == 0)
        def _():
            acc[...] = part

        @pl.when(k > 0)
        def _():
            acc[...] += part

        @pl.when(k == nk - 1)
        def _():
            finish(acc[...])

    ins = [t for pair in pairs for t in pair]
    in_specs = [t for pair in specs for t in pair]
    if res is not None:
        ins.append(res)
        in_specs.append(o_spec)
    out_specs = o_spec
    if norm is not None:
        ins.append(norm)
        in_specs.append(pl.BlockSpec(norm.shape, lambda *_: (0, 0)))
        out_specs, out_sds = [o_spec, o_spec], [out_sds, jax.ShapeDtypeStruct(out_sds.shape, BF16)]
    return pl.pallas_call(
        body, name=name, grid=grid, in_specs=in_specs, out_specs=out_specs, out_shape=out_sds,
        scratch_shapes=[pltpu.VMEM(acc_shape, F32)] if nk > 1 else [],
        compiler_params=_params(("parallel", "parallel", "arbitrary")),
    )(*ins)


def _rms_rows(x, g):
    return (x * lax.rsqrt(jnp.mean(x * x, axis=-1, keepdims=True) + RMS_EPS) * g).astype(BF16)


def _matmul(a, b, mode, name, res=None, out_dtype=F32, tm=512, tn=None, tk=None, norm=None):
    if mode == "nn":
        (M, K), N = a.shape, b.shape[1]
    elif mode == "nt":
        (M, K), N = a.shape, b.shape[0]
    else:
        (K, M), N = a.shape, b.shape[1]
    tm = min(tm, M)
    tn = tn or _pick(N, (2176, 2048, 1408, 1024, 512))
    tk = tk or _pick(K, (1024, 2176, 1408, 512))
    assert M % tm == 0 and N % tn == 0 and K % tk == 0, (M, N, K, tm, tn, tk)
    a_spec = pl.BlockSpec((tk, tm), lambda i, j, k: (k, i)) if mode == "tn" else pl.BlockSpec((tm, tk), lambda i, j, k: (i, k))
    b_spec = pl.BlockSpec((tn, tk), lambda i, j, k: (j, k)) if mode == "nt" else pl.BlockSpec((tk, tn), lambda i, j, k: (k, j))
    o_spec = pl.BlockSpec((tm, tn), lambda i, j, k: (i, j))
    assert norm is None or tn == N
    return _mm_call(name, mode, [(a, b)], [(a_spec, b_spec)], o_spec, jax.ShapeDtypeStruct((M, N), out_dtype),
                    (M // tm, N // tn, K // tk), (tm, tn), res, norm=None if norm is None else norm.reshape(1, N))


def _mm_shard_out(a, bs, mode, name, out_dtype=F32, tm=512, tk=None):
    J = bs.shape[0]
    n = bs.shape[1] if mode == "nt" else bs.shape[2]
    (K, M) = a.shape if mode == "tn" else a.shape[::-1]
    tm = min(tm, M)
    tk = tk or (K if mode != "tn" else _pick(K, (1024, 512)))
    a_spec = pl.BlockSpec((tk, tm), lambda j, i, k: (k, i)) if mode == "tn" else pl.BlockSpec((tm, tk), lambda j, i, k: (i, k))
    b_spec = pl.BlockSpec((None, n, tk), lambda j, i, k: (j, 0, k)) if mode == "nt" else pl.BlockSpec((None, tk, n), lambda j, i, k: (j, k, 0))
    o_spec = pl.BlockSpec((None, tm, n), lambda j, i, k: (j, i, 0))
    return _mm_call(name, mode, [(a, bs)], [(a_spec, b_spec)], o_spec, jax.ShapeDtypeStruct((J, M, n), out_dtype),
                    (J, M // tm, K // tk), (tm, n))


def _mm_shard_k(pairs, mode, name, res=None, out_dtype=F32, tm=512, tn=None, fold=1):
    J, M, n = pairs[0][0].shape
    N = pairs[0][1].shape[2] if mode == "nn" else pairs[0][1].shape[1]
    tm = min(tm, M)
    tn = tn or _pick(N, (2048, 1024, 512))
    a_spec = pl.BlockSpec((fold, tm, n), lambda i, j, k: (k, i, 0))
    b_spec = pl.BlockSpec((fold, n, tn), lambda i, j, k: (k, 0, j)) if mode == "nn" else pl.BlockSpec((fold, tn, n), lambda i, j, k: (k, j, 0))
    o_spec = pl.BlockSpec((tm, tn), lambda i, j, k: (i, j))
    return _mm_call(name, mode, pairs, [(a_spec, b_spec)] * len(pairs), o_spec, jax.ShapeDtypeStruct((M, N), out_dtype),
                    (M // tm, N // tn, J // fold), (tm, tn), res, fold)


def _mm_shard_m(as_, b, name, out_dtype=F32, tm=None, tn=None, tk=512):
    J, K, n = as_.shape
    N = b.shape[1]
    tm = tm or n
    tn = tn or _pick(N, (2048, 1024, 512))
    tk = min(tk, K)
    nn = N // tn
    a_spec = pl.BlockSpec((None, tk, tm), lambda j, i, k: (j, k, i // nn))
    b_spec = pl.BlockSpec((tk, tn), lambda j, i, k: (k, i % nn))
    o_spec = pl.BlockSpec((None, tm, tn), lambda j, i, k: (j, i // nn, i % nn))
    return _mm_call(name, "tn", [(as_, b)], [(a_spec, b_spec)], o_spec, jax.ShapeDtypeStruct((J, n, N), out_dtype),
                    (J, (n // tm) * nn, K // tk), (tm, tn))


def _out_proj_dx(dh, x, g, dres, w, name):
    T, D = x.shape
    N = w.shape[0]
    tm = min(256, T)

    def body(dh_ref, x_ref, g_ref, dres_ref, w_ref, o_ref, dx_ref, dxb_ref, dg_ref):
        @pl.when(pl.program_id(0) == 0)
        def _():
            dg_ref[...] = jnp.zeros_like(dg_ref)

        xv, dhv = x_ref[...], dh_ref[...]
        r = lax.rsqrt(jnp.mean(xv * xv, axis=-1, keepdims=True) + RMS_EPS)
        y = xv * r
        dy = dhv * g_ref[...]
        dx = dres_ref[...] + r * (dy - y * jnp.mean(dy * y, axis=-1, keepdims=True))
        dx_ref[...] = dx
        dxb = dx.astype(BF16)
        dxb_ref[...] = dxb
        dg_ref[...] += jnp.sum(dhv * y, axis=0, keepdims=True)
        o_ref[...] = _nt(dxb, w_ref[...])

    row = pl.BlockSpec((tm, D), lambda i: (i, 0))
    vec = pl.BlockSpec((1, D), lambda i: (0, 0))
    return pl.pallas_call(
        body, name=name, grid=(T // tm,), in_specs=[row, row, vec, row, pl.BlockSpec((N, D), lambda i: (0, 0))],
        out_specs=[pl.BlockSpec((tm, N), lambda i: (i, 0)), row, row, vec],
        out_shape=[jax.ShapeDtypeStruct((T, N), F32), jax.ShapeDtypeStruct((T, D), F32), jax.ShapeDtypeStruct((T, D), BF16),
                   jax.ShapeDtypeStruct((1, D), F32)],
        compiler_params=_params(("arbitrary",)),
    )(dh, x, g.reshape(1, D), dres, w)


def _in_proj_dw(pieces, h, name):
    T, D = h.shape
    tm = 256
    nbs = [p.shape[1] // tm for p in pieces]
    los = [sum(nbs[:t]) for t in range(len(pieces))]

    def body(*refs):
        h_ref, o_ref = refs[-2:]
        i = pl.program_id(0)
        for p_ref, lo, nb in zip(refs[:-2], los, nbs):
            @pl.when((i >= lo) & (i < lo + nb))
            def _():
                o_ref[...] = _tn(p_ref[...], h_ref[...]).astype(BF16)

    specs = [pl.BlockSpec((T, tm), (lambda lo, nb: lambda i: (0, jnp.clip(i - lo, 0, nb - 1)))(lo, nb)) for lo, nb in zip(los, nbs)]
    return pl.pallas_call(body, name=name, grid=(sum(nbs),), in_specs=specs + [pl.BlockSpec((T, D), lambda i: (0, 0))],
                          out_specs=pl.BlockSpec((tm, D), lambda i: (i, 0)), out_shape=jax.ShapeDtypeStruct((sum(nbs) * tm, D), BF16),
                          compiler_params=_params(("parallel",)))(*pieces, h)


def _in_proj_dx(pieces, w, x, g, dres, name):
    T, D = x.shape
    K = w.shape[0]
    tm = min(256, T)
    n = len(pieces)
    widths = [p.shape[1] for p in pieces]
    offs = [sum(widths[:t]) for t in range(n)]

    def body(*refs):
        w_ref, x_ref, g_ref, dres_ref, dx_ref, dxb_ref, dg_ref = refs[n:]

        @pl.when(pl.program_id(0) == 0)
        def _():
            dg_ref[...] = jnp.zeros_like(dg_ref)

        dh = None
        for p_ref, off, wd in zip(refs[:n], offs, widths):
            d = _nn(p_ref[...], w_ref[off:off + wd, :])
            dh = d if dh is None else dh + d
        xv = x_ref[...]
        r = lax.rsqrt(jnp.mean(xv * xv, axis=-1, keepdims=True) + RMS_EPS)
        y = xv * r
        dy = dh * g_ref[...]
        dx = dres_ref[...] + r * (dy - y * jnp.mean(dy * y, axis=-1, keepdims=True))
        dx_ref[...] = dx
        dxb_ref[...] = dx.astype(BF16)
        dg_ref[...] += jnp.sum(dh * y, axis=0, keepdims=True)

    specs = [pl.BlockSpec((tm, wd), lambda i: (i, 0)) for wd in widths]
    row = pl.BlockSpec((tm, D), lambda i: (i, 0))
    vec = pl.BlockSpec((1, D), lambda i: (0, 0))
    return pl.pallas_call(body, name=name, grid=(T // tm,), in_specs=specs + [pl.BlockSpec((K, D), lambda i: (0, 0)), row, vec, row],
                          out_specs=[row, row, vec],
                          out_shape=[jax.ShapeDtypeStruct((T, D), F32), jax.ShapeDtypeStruct((T, D), BF16), jax.ShapeDtypeStruct((1, D), F32)],
                          compiler_params=_params(("arbitrary",)))(*pieces, w, x, g.reshape(1, D), dres)


def _seg_matrix(width):
    return jnp.asarray(np.kron(np.eye(width // HEAD_DIM, dtype=np.float32), np.full((HEAD_DIM, HEAD_DIM), 1.0 / HEAD_DIM, np.float32)), BF16)


def _segmean(v, p):
    hi = v.astype(BF16)
    r = v - hi.astype(F32)
    mid = r.astype(BF16)
    lo = (r - mid.astype(F32)).astype(BF16)
    w = min(256, v.shape[1])
    pw = p[:w, :w]
    halves = []
    for c in range(v.shape[1] // w):
        cols = slice(c * w, (c + 1) * w)
        halves.append(_nn(hi[:, cols], pw) + _nn(mid[:, cols], pw) + _nn(lo[:, cols], pw))
    return halves[0] if len(halves) == 1 else jnp.concatenate(halves, axis=1)


def _gelu(x):
    c0 = math.sqrt(2.0 / math.pi)
    t = jnp.tanh(c0 * (x + 0.044715 * x * x * x))
    return 0.5 * x * (1.0 + t), t


def _gelu_grad(x, t):
    c0 = math.sqrt(2.0 / math.pi)
    return 0.5 * (1.0 + t) + 0.5 * x * (1.0 - t * t) * c0 * (1.0 + 3.0 * 0.044715 * x * x)


def _sigmoid(x):
    return 1.0 / (1.0 + jnp.exp(-x))


def _rms_fwd(x, g, name):
    T, D = x.shape
    tm = min(256, T)

    def body(x_ref, g_ref, o_ref):
        o_ref[...] = _rms_rows(x_ref[...], g_ref[...])

    return pl.pallas_call(
        body, name=name, grid=(T // tm,),
        in_specs=[pl.BlockSpec((tm, D), lambda i: (i, 0)), pl.BlockSpec((1, D), lambda i: (0, 0))],
        out_specs=pl.BlockSpec((tm, D), lambda i: (i, 0)), out_shape=jax.ShapeDtypeStruct((T, D), BF16),
        compiler_params=_params(("parallel",)),
    )(x, g.reshape(1, D))


def _sgu_core(zu, zv, ws_ref, bias, p):
    ug, tu = _gelu(zu)
    vg, tv = _gelu(zv)
    xc = vg - _segmean(vg, p)
    rs = lax.rsqrt(_segmean(xc * xc, p) + LN_EPS)
    vn = xc * rs
    vnb = vn.astype(BF16)
    low = lax.broadcasted_iota(jnp.int32, (SGU_CHUNK, 128), 1) < HEAD_DIM
    parts = []
    for j in range(4):
        vp = vnb[:, 128 * j:128 * (j + 1)]
        parts.append(jnp.where(low, _nn(ws_ref[2 * j], vp), _nn(ws_ref[2 * j + 1], vp)))
    mixed = jnp.concatenate(parts, axis=1) + bias
    return ug, tu, tv, rs, vn, vnb, mixed, low


SGU_ROWS = 8 * SGU_CHUNK


def _sgu_fwd(z, ws, bias, name):
    T = z.shape[0]

    def body(zu_ref, zv_ref, ws_ref, b_ref, p_ref, y_ref):
        for r in range(0, SGU_ROWS, SGU_CHUNK):
            rows = slice(r, r + SGU_CHUNK)
            ug, _, _, _, _, _, mixed, _ = _sgu_core(zu_ref[rows, :], zv_ref[rows, :], ws_ref, b_ref[...], p_ref[...])
            y_ref[rows, :] = ug * mixed

    full = lambda shape: pl.BlockSpec(shape, lambda i: (0,) * len(shape))
    return pl.pallas_call(
        body, name=name, grid=(T // SGU_ROWS,),
        in_specs=[pl.BlockSpec((SGU_ROWS, 512), lambda i: (i, COL_AU)), pl.BlockSpec((SGU_ROWS, 512), lambda i: (i, COL_AV)),
                  full((8, 128, 128)), full((128, 512)), full((512, 512))],
        out_specs=pl.BlockSpec((SGU_ROWS, 512), lambda i: (i, 0)), out_shape=jax.ShapeDtypeStruct((T, 512), F32),
        compiler_params=_params(("parallel",)),
    )(z, z, ws, bias, _seg_matrix(512))


def _sgu_bwd(z, dy, ws, ws_t, bias, name):
    T = z.shape[0]

    def body(zu_ref, zv_ref, dy_ref, ws_ref, wst_ref, b_ref, p_ref, dz_ref, dws_ref, db_ref):
        @pl.when(pl.program_id(0) == 0)
        def _():
            dws_ref[...] = jnp.zeros_like(dws_ref)
            db_ref[...] = jnp.zeros_like(db_ref)

        p = p_ref[...]
        zero = jnp.zeros((SGU_CHUNK, 128), BF16)
        dws = [None] * 8
        db = None
        for r in range(0, SGU_ROWS, SGU_CHUNK):
            rows = slice(r, r + SGU_CHUNK)
            zu, zv = zu_ref[rows, :], zv_ref[rows, :]
            ug, tu, tv, rs, vn, vnb, mixed, low = _sgu_core(zu, zv, ws_ref, b_ref[...], p)
            dyv = dy_ref[rows, :]
            dmixed = dyv * ug
            db = dmixed if db is None else db + dmixed
            dmb = dmixed.astype(BF16)
            parts = []
            for j in range(4):
                dmp, vp = dmb[:, 128 * j:128 * (j + 1)], vnb[:, 128 * j:128 * (j + 1)]
                for g, d in ((2 * j, _nt(jnp.where(low, dmp, zero), vp)), (2 * j + 1, _nt(jnp.where(low, zero, dmp), vp))):
                    dws[g] = d if dws[g] is None else dws[g] + d
                parts.append(jnp.where(low, _nn(wst_ref[2 * j], dmp), _nn(wst_ref[2 * j + 1], dmp)))
            dvn = jnp.concatenate(parts, axis=1)
            dvg = rs * (dvn - _segmean(dvn, p) - vn * _segmean(dvn * vn, p))
            dz_ref[rows, 0:512] = (dyv * mixed * _gelu_grad(zu, tu)).astype(BF16)
            dz_ref[rows, 512:1024] = (dvg * _gelu_grad(zv, tv)).astype(BF16)
        db_ref[...] += db
        for g in range(8):
            dws_ref[g] += dws[g]

    full = lambda shape: pl.BlockSpec(shape, lambda i: (0,) * len(shape))
    return pl.pallas_call(
        body, name=name, grid=(T // SGU_ROWS,),
        in_specs=[pl.BlockSpec((SGU_ROWS, 512), lambda i: (i, COL_AU)), pl.BlockSpec((SGU_ROWS, 512), lambda i: (i, COL_AV)),
                  pl.BlockSpec((SGU_ROWS, 512), lambda i: (i, 0)), full((8, 128, 128)), full((8, 128, 128)), full((128, 512)), full((512, 512))],
        out_specs=[pl.BlockSpec((SGU_ROWS, 1024), lambda i: (i, 0)), full((8, 128, 128)), full((128, 512))],
        out_shape=[jax.ShapeDtypeStruct((T, 1024), BF16), jax.ShapeDtypeStruct((8, 128, 128), F32), jax.ShapeDtypeStruct((128, 512), F32)],
        compiler_params=_params(("arbitrary",)),
    )(z, z, dy, ws, ws_t, bias, _seg_matrix(512))


CONV_ROWS = 256


def _conv_taps(pad_ref, w_ref, base, flip):
    acc = None
    for k in range(CONV_WIDTH):
        wk = w_ref[CONV_WIDTH - 1 - k if flip else k]
        t = wk * pad_ref[base + k + 1:base + k + 1 + CONV_ROWS, :]
        acc = t if acc is None else acc + t
    return acc


def _conv_fwd1(z, w, cb, B, S, name):
    T = B * S
    rows = min(CONV_ROWS, S)
    assert rows == CONV_ROWS

    def body(a_ref, g_ref, w_ref, cb_ref, c_ref, pad):
        pad[0:CONV_PAD, :] = jnp.zeros((CONV_PAD, 128), F32)
        pad[CONV_PAD + S:2 * CONV_PAD + S, :] = jnp.zeros((CONV_PAD, 128), F32)
        pad[CONV_PAD:CONV_PAD + S, :] = a_ref[...] * _sigmoid(g_ref[...])

        for base in range(0, S, CONV_ROWS):
            c_ref[base:base + CONV_ROWS, :] = _conv_taps(pad, w_ref, base, False) + cb_ref[...]

    return pl.pallas_call(
        body, name=name, grid=(4, B),
        in_specs=[pl.BlockSpec((S, 128), lambda j, b: (b, 4 * COL_CA + j)), pl.BlockSpec((S, 128), lambda j, b: (b, 4 * COL_CG + j)),
                  pl.BlockSpec((32, 1, 128), lambda j, b: (0, 0, j)), pl.BlockSpec((1, 128), lambda j, b: (0, j))],
        out_specs=pl.BlockSpec((S, 128), lambda j, b: (b, j)), out_shape=jax.ShapeDtypeStruct((T, 512), F32),
        scratch_shapes=[pltpu.VMEM((S + 2 * CONV_PAD, 128), F32)], compiler_params=_params(("parallel", "parallel")),
    )(z, z, w, cb)


def _ln_rows(c):
    mu = jnp.mean(c, axis=-1, keepdims=True)
    xc = c - mu
    rs = lax.rsqrt(jnp.mean(xc * xc, axis=-1, keepdims=True) + LN_EPS)
    return xc * rs, rs


def _conv_bwd2(z, dc, w, B, S, name):
    T = B * S

    def body(a_ref, g_ref, dc_ref, w_ref, da_ref, dg_ref, dw_ref, hpad, dpad):
        @pl.when(pl.program_id(1) == 0)
        def _():
            dw_ref[...] = jnp.zeros_like(dw_ref)

        zeros = jnp.zeros((CONV_PAD, 128), F32)
        for ref in (hpad, dpad):
            ref[0:CONV_PAD, :] = zeros
            ref[CONV_PAD + S:2 * CONV_PAD + S, :] = zeros
        hpad[CONV_PAD:CONV_PAD + S, :] = a_ref[...] * _sigmoid(g_ref[...])
        dpad[CONV_PAD:CONV_PAD + S, :] = dc_ref[...]
        dws = [None] * CONV_WIDTH
        for base in range(0, S, CONV_ROWS):
            rows = slice(base, base + CONV_ROWS)
            dh = _conv_taps(dpad, w_ref, base, True)
            sg = _sigmoid(g_ref[rows, :])
            da_ref[rows, :] = (dh * sg).astype(BF16)
            dg_ref[rows, :] = (dh * a_ref[rows, :] * sg * (1.0 - sg)).astype(BF16)
            dcv = dc_ref[rows, :]
            for k in range(CONV_WIDTH):
                prod = dcv * hpad[base + k + 1:base + k + 1 + CONV_ROWS, :]
                part = jnp.sum(prod.reshape(CONV_ROWS // 8, 8, 128), axis=0)
                dws[k] = part if dws[k] is None else dws[k] + part
        for k in range(CONV_WIDTH):
            dw_ref[k] += jnp.sum(dws[k], axis=0, keepdims=True)

    return pl.pallas_call(
        body, name=name, grid=(4, B),
        in_specs=[pl.BlockSpec((S, 128), lambda j, b: (b, 4 * COL_CA + j)), pl.BlockSpec((S, 128), lambda j, b: (b, 4 * COL_CG + j)),
                  pl.BlockSpec((S, 128), lambda j, b: (b, j)), pl.BlockSpec((32, 1, 128), lambda j, b: (0, 0, j))],
        out_specs=[pl.BlockSpec((S, 128), lambda j, b: (b, j)), pl.BlockSpec((S, 128), lambda j, b: (b, j)),
                   pl.BlockSpec((32, 1, 128), lambda j, b: (0, 0, j))],
        out_shape=[jax.ShapeDtypeStruct((T, 512), BF16), jax.ShapeDtypeStruct((T, 512), BF16), jax.ShapeDtypeStruct((32, 1, 512), F32)],
        scratch_shapes=[pltpu.VMEM((S + 2 * CONV_PAD, 128), F32), pltpu.VMEM((S + 2 * CONV_PAD, 128), F32)],
        compiler_params=_params(("parallel", "arbitrary")),
    )(z, z, dc, w)


def _swap16(x):
    n = x.shape[1]
    first = (lax.broadcasted_iota(jnp.int32, x.shape, 1) % 32) < 16
    return jnp.where(first, pltpu.roll(x, n - 16, 1), pltpu.roll(x, 16, 1))


def _rope(x, cos, sin):
    return x * cos + _swap16(x) * sin


def _rope_t(dy, cos, sin):
    return dy * cos + _swap16(dy * sin)


def _qk_norm(x, p):
    r = lax.rsqrt(_segmean(x * x, p) + RMS_EPS)
    return x * r, r


def _store_heads(ref, val, n):
    for h in range(n):
        ref[h] = val[:, HEAD_DIM * h:HEAD_DIM * (h + 1)].astype(ref.dtype)


def _load_heads(ref, n):
    return jnp.concatenate([ref[h] for h in range(n)], axis=1)


def _prep_fwd(z, gq, gk, rope, B, S, kv_heads, cols, name):
    tm = min(2 * ROW_TILE, S)
    ns = S // tm
    kw = kv_heads * HEAD_DIM
    scale = HEAD_DIM ** -0.5
    qc, kc, vc = cols

    def body(*refs):
        if rope is None:
            q_ref, k_ref, v_ref, gq_ref, gk_ref, p_ref, qo, ko, vo = refs
        else:
            q_ref, k_ref, v_ref, gq_ref, gk_ref, p_ref, cos_ref, sin_ref, qo, ko, vo = refs
        p = p_ref[...]
        qn, _ = _qk_norm(q_ref[...], p)
        kn, _ = _qk_norm(k_ref[...], p[:kw, :kw])
        qn, kn = qn * gq_ref[...], kn * gk_ref[...]
        if rope is not None:
            cos, sin = cos_ref[...], sin_ref[...]
            qn, kn = _rope(qn, cos, sin), _rope(kn, cos[:, :kw], sin[:, :kw])
        _store_heads(qo, qn * scale, N_HEADS)
        _store_heads(ko, kn, kv_heads)
        _store_heads(vo, v_ref[...], kv_heads)

    row = lambda w, c: pl.BlockSpec((tm, w), lambda b, i: (b * ns + i, c))
    const = lambda shape: pl.BlockSpec(shape, lambda b, i: (0,) * len(shape))
    heads = lambda n: pl.BlockSpec((None, n, tm, HEAD_DIM), lambda b, i: (b, 0, i, 0))
    ins = [z, z, z, gq, gk, _seg_matrix(512)]
    specs = [row(512, qc), row(kw, kc), row(kw, vc), const((1, 512)), const((1, kw)), const((512, 512))]
    if rope is not None:
        ins += list(rope)
        specs += [pl.BlockSpec((tm, 512), lambda b, i: (i, 0))] * 2
    return pl.pallas_call(
        body, name=name, grid=(B, ns), in_specs=specs, out_specs=[heads(N_HEADS), heads(kv_heads), heads(kv_heads)],
        out_shape=[jax.ShapeDtypeStruct((B, N_HEADS, S, HEAD_DIM), BF16), jax.ShapeDtypeStruct((B, kv_heads, S, HEAD_DIM), BF16),
                   jax.ShapeDtypeStruct((B, kv_heads, S, HEAD_DIM), BF16)],
        compiler_params=_params(("parallel", "parallel")),
    )(*ins)


def _prep_bwd(z, dq, dk, dv, gq, gk, rope, B, S, kv_heads, cols, name):
    T = B * S
    tm = min(2 * ROW_TILE, S)
    ns = S // tm
    kw = kv_heads * HEAD_DIM
    scale = HEAD_DIM ** -0.5
    qc, kc, _ = cols

    def body(*refs):
        if rope is None:
            q_ref, k_ref, dq_ref, dk_ref, dv_ref, gq_ref, gk_ref, p_ref, dz_ref, dgq_ref, dgk_ref = refs
        else:
            q_ref, k_ref, dq_ref, dk_ref, dv_ref, gq_ref, gk_ref, p_ref, cos_ref, sin_ref, dz_ref, dgq_ref, dgk_ref = refs

        @pl.when((pl.program_id(0) == 0) & (pl.program_id(1) == 0))
        def _():
            dgq_ref[...] = jnp.zeros_like(dgq_ref)
            dgk_ref[...] = jnp.zeros_like(dgk_ref)

        p = p_ref[...]
        dqv = _load_heads(dq_ref, N_HEADS) * scale
        dkv = _load_heads(dk_ref, kv_heads)
        if rope is not None:
            cos, sin = cos_ref[...], sin_ref[...]
            dqv, dkv = _rope_t(dqv, cos, sin), _rope_t(dkv, cos[:, :kw], sin[:, :kw])

        def through_norm(xv, dy, g, pm, dg_ref):
            xh, r = _qk_norm(xv, pm)
            dg_ref[...] += jnp.sum(dy * xh, axis=0, keepdims=True)
            dxh = dy * g
            return r * (dxh - xh * _segmean(dxh * xh, pm))

        dz_ref[:, 0:512] = through_norm(q_ref[...], dqv, gq_ref[...], p, dgq_ref).astype(BF16)
        dz_ref[:, 512:512 + kw] = through_norm(k_ref[...], dkv, gk_ref[...], p[:kw, :kw], dgk_ref).astype(BF16)
        dz_ref[:, 512 + kw:512 + 2 * kw] = _load_heads(dv_ref, kv_heads).astype(BF16)

    row = lambda w, c: pl.BlockSpec((tm, w), lambda b, i: (b * ns + i, c))
    const = lambda shape: pl.BlockSpec(shape, lambda b, i: (0,) * len(shape))
    heads = lambda n: pl.BlockSpec((None, n, tm, HEAD_DIM), lambda b, i: (b, 0, i, 0))
    ins = [z, z, dq, dk, dv, gq, gk, _seg_matrix(512)]
    specs = [row(512, qc), row(kw, kc), heads(N_HEADS), heads(kv_heads), heads(kv_heads), const((1, 512)), const((1, kw)), const((512, 512))]
    if rope is not None:
        ins += list(rope)
        specs += [pl.BlockSpec((tm, 512), lambda b, i: (i, 0))] * 2
    return pl.pallas_call(
        body, name=name, grid=(B, ns), in_specs=specs, out_specs=[row(512 + 2 * kw, 0), const((1, 512)), const((1, kw))],
        out_shape=[jax.ShapeDtypeStruct((T, 512 + 2 * kw), BF16), jax.ShapeDtypeStruct((1, 512), F32), jax.ShapeDtypeStruct((1, kw), F32)],
        compiler_params=_params(("arbitrary", "arbitrary")),
    )(*ins)


def _toeplitz(win, tq, S):
    r = pltpu.roll(jnp.broadcast_to(win, (tq, S + tq)), 0, 1, stride=1, stride_axis=0)
    return r[:, tq:tq + S]


ATTN_HEADS = 4


def _attn_fwd(q, k, v, win, name, nh=ATTN_HEADS, tq=ATTN_TQ):
    B, H, S, _ = q.shape
    shared = k.shape[1] != H
    assert not shared or H // k.shape[1] == nh
    tq = min(tq, S)

    def body(*refs):
        if win is None:
            q_ref, k_ref, v_ref, o_ref = refs
        else:
            q_ref, k_ref, v_ref, w_ref, o_ref = refs
        kvs = [(k_ref[...], v_ref[...])] * nh if shared else [(k_ref[h], v_ref[h]) for h in range(nh)]
        scores = []
        for h in range(nh):
            s = _nt(q_ref[h], kvs[h][0])
            if win is not None:
                s = s + _toeplitz(w_ref[h], tq, S)
            scores.append(s)
        probs = []
        for s in scores:
            p = jnp.exp(s - jnp.max(s, axis=-1, keepdims=True))
            probs.append((p.astype(BF16), jnp.sum(p, axis=-1, keepdims=True)))
        for h, (p, l) in enumerate(probs):
            o_ref[h] = _nn(p, kvs[h][1]) / l

    qs = pl.BlockSpec((None, nh, tq, HEAD_DIM), lambda b, h, i: (b, h, i, 0))
    ks = (pl.BlockSpec((None, None, S, HEAD_DIM), lambda b, h, i: (b, h, 0, 0)) if shared
          else pl.BlockSpec((None, nh, S, HEAD_DIM), lambda b, h, i: (b, h, 0, 0)))
    ins, specs = [q, k, v], [qs, ks, ks]
    if win is not None:
        ins.append(win)
        specs.append(pl.BlockSpec((nh, None, 1, S + tq), lambda b, h, i: (h, i, 0, 0)))
    return pl.pallas_call(body, name=name, grid=(B, H // nh, S // tq), in_specs=specs, out_specs=qs,
                          out_shape=jax.ShapeDtypeStruct((B, H, S, HEAD_DIM), F32),
                          compiler_params=_params(("parallel", "parallel", "parallel")))(*ins)


def _attn_bwd(q, k, v, o, do, win, name, nh=ATTN_HEADS, tq=ATTN_TQ):
    B, H, S, _ = q.shape
    hkv = k.shape[1]
    shared = hkv != H
    assert not shared or H // hkv == nh
    tq = min(tq, S)
    nq = S // tq

    def body(*refs):
        if win is None:
            q_ref, k_ref, v_ref, o_ref, do_ref, dq_ref, dk_ref, dv_ref = refs
        else:
            q_ref, k_ref, v_ref, o_ref, do_ref, w_ref, rev_ref, dq_ref, dk_ref, dv_ref, dw_ref = refs

        @pl.when(pl.program_id(2) == 0)
        def _():
            dk_ref[...] = jnp.zeros_like(dk_ref)
            dv_ref[...] = jnp.zeros_like(dv_ref)

        kvs = [(k_ref[...], v_ref[...])] * nh if shared else [(k_ref[h], v_ref[h]) for h in range(nh)]
        qvs, dobs, scores, dps = [], [], [], []
        for h in range(nh):
            qv, dov = q_ref[h], do_ref[h]
            dob = dov.astype(BF16)
            s = _nt(qv, kvs[h][0])
            if win is not None:
                s = s + _toeplitz(w_ref[h], tq, S)
            dp = _nt(dob, kvs[h][1]) - jnp.sum(dov * o_ref[h], axis=-1, keepdims=True)
            qvs.append(qv)
            dobs.append(dob)
            scores.append(s)
            dps.append(dp)
        pbs, dsbs = [], []
        for s, dp in zip(scores, dps):
            p = jnp.exp(s - jnp.max(s, axis=-1, keepdims=True))
            p = p * (1.0 / jnp.sum(p, axis=-1, keepdims=True))
            pbs.append(p.astype(BF16))
            dsbs.append((p * dp).astype(BF16))
        dk_acc = dv_acc = None
        for h in range(nh):
            dvh, dkh = _tn(pbs[h], dobs[h]), _tn(dsbs[h], qvs[h])
            dq_ref[h] = _nn(dsbs[h], kvs[h][0])
            if shared:
                dv_acc = dvh if dv_acc is None else dv_acc + dvh
                dk_acc = dkh if dk_acc is None else dk_acc + dkh
            else:
                dv_ref[h] += dvh
                dk_ref[h] += dkh
            if win is not None:
                rev = _nn(rev_ref[...], dsbs[h])
                half = tq // 2
                top = jnp.concatenate([rev[:half], jnp.zeros((half, tq), F32)], axis=1)
                bot = jnp.concatenate([jnp.zeros((half, half), F32), rev[half:], jnp.zeros((half, half), F32)], axis=1)
                dw_ref[h] = jnp.sum(pltpu.roll(top + bot, 0, 1, stride=1, stride_axis=0), axis=0, keepdims=True)
        if shared:
            dv_ref[...] += dv_acc
            dk_ref[...] += dk_acc

    qs = pl.BlockSpec((None, nh, tq, HEAD_DIM), lambda b, h, i: (b, h, i, 0))
    ks = (pl.BlockSpec((None, None, S, HEAD_DIM), lambda b, h, i: (b, h, 0, 0)) if shared
          else pl.BlockSpec((None, nh, S, HEAD_DIM), lambda b, h, i: (b, h, 0, 0)))
    ins, specs = [q, k, v, o, do], [qs, ks, ks, qs, qs]
    outs = [jax.ShapeDtypeStruct((B, H, S, HEAD_DIM), F32), jax.ShapeDtypeStruct((B, hkv, S, HEAD_DIM), F32), jax.ShapeDtypeStruct((B, hkv, S, HEAD_DIM), F32)]
    ospecs = [qs, ks, ks]
    if win is not None:
        ins += [win, jnp.asarray(np.eye(tq, dtype=np.float32)[::-1].copy(), BF16)]
        specs += [pl.BlockSpec((nh, None, 1, S + tq), lambda b, h, i: (h, i, 0, 0)), pl.BlockSpec((tq, tq), lambda b, h, i: (0, 0))]
        outs.append(jax.ShapeDtypeStruct((B, H, nq, 1, S + tq), F32))
        ospecs.append(pl.BlockSpec((None, nh, None, 1, S + tq), lambda b, h, i: (b, h, i, 0, 0)))
    return pl.pallas_call(body, name=name, grid=(B, H // nh, nq), in_specs=specs, out_specs=ospecs, out_shape=outs,
                          compiler_params=_params(("parallel", "parallel", "arbitrary")))(*ins)


def _pattern_count(delta):
    n = jnp.zeros(delta.shape, jnp.int32)
    for window, dil in DIL_PATTERNS:
        n = n + ((delta % dil == 0) & (jnp.abs(delta) <= window // 2)).astype(jnp.int32)
    return n


def _t5_bucket(rel):
    nb = REL_BUCKETS // 2
    max_exact = nb // 2
    ret = jnp.where(rel > 0, nb, 0)
    n = jnp.abs(rel)
    nf = jnp.maximum(n, 1).astype(F32)
    large = max_exact + (jnp.log(nf / max_exact) / math.log(REL_MAX_DIST / max_exact) * (nb - max_exact)).astype(jnp.int32)
    large = jnp.minimum(large, nb - 1)
    return ret + jnp.where(n < max_exact, n, large)


def _bias_windows(rel_bias, S):
    tq = min(ATTN_TQ, S)
    nq = S // tq
    n = nq * (S + tq)
    delta = (jnp.arange(S + tq)[None, :] - (jnp.arange(nq)[:, None] + 1) * tq).reshape(n)
    count = _pattern_count(delta)
    onehot = (_t5_bucket(delta)[None, :] == jnp.arange(REL_BUCKETS)[:, None]).astype(F32)
    extra = jnp.where(count > 0, jnp.log(jnp.maximum(count, 1).astype(F32)), MASKED).reshape(1, n)
    live = (count > 0).astype(F32).reshape(1, n)

    def body(t_ref, oh_ref, live_ref, extra_ref, o_ref):
        o_ref[...] = _nn(t_ref[...], oh_ref[...], HIGHEST) * live_ref[...] + extra_ref[...]

    val = pl.pallas_call(body, name="bias_windows", out_shape=jax.ShapeDtypeStruct((N_HEADS, n), F32),
                         compiler_params=_params())(rel_bias.T, onehot, live, extra)
    return val.reshape(N_HEADS, nq, 1, S + tq)


def _bias_fold(dwin, S, name):
    B, H, nq = dwin.shape[:3]
    tq = min(ATTN_TQ, S)
    n = nq * (S + tq)
    delta = (jnp.arange(S + tq)[None, :] - (tq - 1) - jnp.arange(nq)[:, None] * tq).reshape(n)
    onehot = (_t5_bucket(delta)[:, None] == jnp.arange(128)[None, :]).astype(F32)

    def body(d_ref, oh_ref, o_ref):
        tot = d_ref[0]
        for b in range(1, B):
            tot = tot + d_ref[b]
        o_ref[...] = _nn(tot, oh_ref[...], HIGHEST)

    out = pl.pallas_call(body, name=name, out_shape=jax.ShapeDtypeStruct((H, 128), F32), compiler_params=_params())(dwin.reshape(B, H, n), onehot)
    return out[:, :REL_BUCKETS].T


def _rope_tables(S):
    half = 16
    freqs = ROPE_THETA ** (-jnp.arange(half, dtype=F32) / half)
    t = jnp.arange(S)
    ang_r = (t // GRID_W).astype(F32)[:, None] * freqs[None, :]
    ang_c = (t % GRID_W).astype(F32)[:, None] * freqs[None, :]
    cos = jnp.concatenate([jnp.cos(ang_r)] * 2 + [jnp.cos(ang_c)] * 2, axis=1)
    sin = jnp.concatenate([-jnp.sin(ang_r), jnp.sin(ang_r), -jnp.sin(ang_c), jnp.sin(ang_c)], axis=1)
    return jnp.tile(cos, (1, N_HEADS)), jnp.tile(sin, (1, N_HEADS))


def _conv_act(c, g, b):
    n, rs = _ln_rows(c)
    t = n * g + b
    return t * _sigmoid(t), n, rs, t


def _mix_fwd(ya, ob, c, od, gain, lng, lnb, B, S, name):
    T = B * S
    tm = min(ROW_TILE, S)
    ns = S // tm

    def body(ya_ref, ob_ref, c_ref, od_ref, g_ref, lg_ref, lb_ref, o_ref):
        yc = _conv_act(c_ref[...], lg_ref[...], lb_ref[...])[0]
        ys = [ya_ref[...], _load_heads(ob_ref, N_HEADS), yc, _load_heads(od_ref, N_HEADS)]
        for m, y in enumerate(ys):
            r = lax.rsqrt(jnp.mean(y * y, axis=-1, keepdims=True) + RMS_EPS)
            o_ref[:, 512 * m:512 * (m + 1)] = (y * r * g_ref[:, 512 * m:512 * (m + 1)]).astype(BF16)

    row = pl.BlockSpec((tm, 512), lambda b, i: (b * ns + i, 0))
    heads = pl.BlockSpec((None, N_HEADS, tm, HEAD_DIM), lambda b, i: (b, 0, i, 0))
    vec = pl.BlockSpec((1, 512), lambda b, i: (0, 0))
    return pl.pallas_call(
        body, name=name, grid=(B, ns), in_specs=[row, heads, row, heads, pl.BlockSpec((1, 2048), lambda b, i: (0, 0)), vec, vec],
        out_specs=pl.BlockSpec((tm, 2048), lambda b, i: (b * ns + i, 0)), out_shape=jax.ShapeDtypeStruct((T, 2048), BF16),
        compiler_params=_params(("parallel", "parallel")),
    )(ya, ob, c, od, gain, lng, lnb)


def _mix_bwd(ya, ob, c, od, dycat, gain, lng, lnb, B, S, name):
    T = B * S
    tm = min(ROW_TILE, S)
    ns = S // tm

    def body(ya_ref, ob_ref, c_ref, od_ref, dy_ref, g_ref, lg_ref, lb_ref, dya_ref, dob_ref, dc_ref, dod_ref, dg_ref, dlg_ref, dlb_ref, dcb_ref):
        @pl.when((pl.program_id(0) == 0) & (pl.program_id(1) == 0))
        def _():
            for ref in (dg_ref, dlg_ref, dlb_ref, dcb_ref):
                ref[...] = jnp.zeros_like(ref)

        yc, n, rs, t = _conv_act(c_ref[...], lg_ref[...], lb_ref[...])
        ys = [ya_ref[...], _load_heads(ob_ref, N_HEADS), yc, _load_heads(od_ref, N_HEADS)]
        outs = [dya_ref, dob_ref, None, dod_ref]
        for m, y in enumerate(ys):
            cols = slice(512 * m, 512 * (m + 1))
            r = lax.rsqrt(jnp.mean(y * y, axis=-1, keepdims=True) + RMS_EPS)
            yh = y * r
            dh = dy_ref[:, cols]
            dg_ref[:, cols] += jnp.sum(dh * yh, axis=0, keepdims=True)
            dyh = dh * g_ref[:, cols]
            dyv = r * (dyh - yh * jnp.mean(dyh * yh, axis=-1, keepdims=True))
            if m == 0:
                outs[m][...] = dyv
            elif m == 2:
                sg = _sigmoid(t)
                dt = dyv * sg * (1.0 + t * (1.0 - sg))
                dlg_ref[...] += jnp.sum(dt * n, axis=0, keepdims=True)
                dlb_ref[...] += jnp.sum(dt, axis=0, keepdims=True)
                dn = dt * lg_ref[...]
                dc = rs * (dn - jnp.mean(dn, axis=-1, keepdims=True) - n * jnp.mean(dn * n, axis=-1, keepdims=True))
                dc_ref[...] = dc
                dcb_ref[...] += jnp.sum(dc, axis=0, keepdims=True)
            else:
                _store_heads(outs[m], dyv, N_HEADS)

    row = pl.BlockSpec((tm, 512), lambda b, i: (b * ns + i, 0))
    heads = pl.BlockSpec((None, N_HEADS, tm, HEAD_DIM), lambda b, i: (b, 0, i, 0))
    vec = pl.BlockSpec((1, 2048), lambda b, i: (0, 0))
    flat = jax.ShapeDtypeStruct((T, 512), F32)
    hm = jax.ShapeDtypeStruct((B, N_HEADS, S, HEAD_DIM), F32)
    v512 = pl.BlockSpec((1, 512), lambda b, i: (0, 0))
    s512 = jax.ShapeDtypeStruct((1, 512), F32)
    return pl.pallas_call(
        body, name=name, grid=(B, ns), in_specs=[row, heads, row, heads, pl.BlockSpec((tm, 2048), lambda b, i: (b * ns + i, 0)), vec, v512, v512],
        out_specs=[row, heads, row, heads, vec, v512, v512, v512],
        out_shape=[flat, hm, flat, hm, jax.ShapeDtypeStruct((1, 2048), F32), s512, s512, s512],
        compiler_params=_params(("arbitrary", "arbitrary")),
    )(ya, ob, c, od, dycat, gain, lng, lnb)


def _ffn_down(gate, up, w_down, res, name, norm=None, target=None):
    J, T, n = gate.shape
    N = w_down.shape[2]
    tm = min(256, T)
    steps = T // tm

    def body(g_ref, u_ref, w_ref, r_ref, *rest):
        x_ref = rest[0] if (norm is not None or target is not None) else None
        o_ref, act_ref = rest[1:3] if x_ref is not None else rest[-2:]
        acc = None
        for j in range(J):
            g = g_ref[j].astype(F32)
            a = (g * _sigmoid(g) * u_ref[j].astype(F32)).astype(BF16)
            act_ref[j] = a
            d = _nn(a, w_ref[j])
            acc = d if acc is None else acc + d
        y = acc + r_ref[...]
        if target is None:
            o_ref[...] = y
            if norm is not None:
                rest[-1][...] = _rms_rows(y, x_ref[...])
            return
        err = y - x_ref[...]
        o_ref[...] = err * (1.0 / N)
        rest[-1][...] = (err * (1.0 / N)).astype(BF16)
        loss_ref, i = rest[-2], pl.program_id(0)

        @pl.when(i == 0)
        def _():
            loss_ref[...] = jnp.zeros_like(loss_ref)

        loss_ref[...] += jnp.sum(err * err)

        @pl.when(i == steps - 1)
        def _():
            loss_ref[...] = loss_ref[...] * (0.5 / N)

    gu = pl.BlockSpec((J, tm, n), lambda i: (0, i, 0))
    row = pl.BlockSpec((tm, N), lambda i: (i, 0))
    ins, specs = [gate, up, w_down, res], [gu, gu, pl.BlockSpec((J, n, N), lambda i: (0, 0, 0)), row]
    outs, ospecs = [jax.ShapeDtypeStruct((T, N), F32), jax.ShapeDtypeStruct((J, T, n), BF16)], [row, gu]
    if target is not None:
        ins, specs = ins + [target], specs + [row]
        outs = outs + [jax.ShapeDtypeStruct((8, 128), F32), jax.ShapeDtypeStruct((T, N), BF16)]
        ospecs = ospecs + [pl.BlockSpec((8, 128), lambda i: (0, 0)), row]
    elif norm is not None:
        ins, specs = ins + [norm.reshape(1, N)], specs + [pl.BlockSpec((1, N), lambda i: (0, 0))]
        outs, ospecs = outs + [jax.ShapeDtypeStruct((T, N), BF16)], ospecs + [row]
    return pl.pallas_call(body, name=name, grid=(steps,), in_specs=specs, out_specs=ospecs, out_shape=outs,
                          compiler_params=_params(("arbitrary" if target is not None else "parallel",)))(*ins)


def _ffn_down_dx(dx, w_down, gate, up, name):
    J, n, D = w_down.shape
    T = dx.shape[0]
    tm = min(512, T)

    def body(dx_ref, w_ref, g_ref, u_ref, dg_ref, du_ref):
        d = _nt(dx_ref[...].astype(BF16), w_ref[...])
        g = g_ref[...].astype(F32)
        s = _sigmoid(g)
        dg_ref[...] = (d * u_ref[...].astype(F32) * s * (1.0 + g * (1.0 - s))).astype(BF16)
        du_ref[...] = (d * g * s).astype(BF16)

    blk = pl.BlockSpec((None, tm, n), lambda j, i: (j, i, 0))
    shape = jax.ShapeDtypeStruct((J, T, n), BF16)
    return pl.pallas_call(body, name=name, grid=(J, T // tm),
                          in_specs=[pl.BlockSpec((tm, D), lambda j, i: (i, 0)), pl.BlockSpec((None, n, D), lambda j, i: (j, 0, 0)), blk, blk],
                          out_specs=[blk, blk], out_shape=[shape, shape], compiler_params=_params(("parallel", "parallel")))(dx, w_down, gate, up)


def _row_tile(R):
    best = R
    for cand in range(16, min(R, 272) + 1, 16):
        if R % cand == 0:
            best = cand
    return best


def _adamw(w, m, v, stack, name, layer=None, prev=None):
    n, R, C = stack.shape
    tm = _row_tile(R)
    nb = R // tm
    off = 0 if layer is None else layer * nb
    c1 = 1.0 - ADAM_B1 ** ADAM_STEP
    c2 = 1.0 - ADAM_B2 ** ADAM_STEP

    def body(w_ref, m_ref, v_ref, s_ref, *rest):
        g_ref, d_ref, mo_ref, vo_ref = rest[-4:]
        g = s_ref[0].astype(F32)
        for k in range(1, n):
            g = g + s_ref[k].astype(F32)
        mn = ADAM_B1 * m_ref[...] + (1.0 - ADAM_B1) * g
        vn = ADAM_B2 * v_ref[...] + (1.0 - ADAM_B2) * (g * g)
        g_ref[...] = g
        mo_ref[...] = mn
        vo_ref[...] = vn
        d_ref[...] = -ADAM_LR * ((mn / c1) / (jnp.sqrt(vn / c2) + ADAM_EPS) + ADAM_WD * w_ref[...])

    blk = pl.BlockSpec((tm, C), lambda i: (i + off, 0))
    ins = [w, m, v, stack]
    specs = [blk, blk, blk, pl.BlockSpec((n, tm, C), lambda i: (0, i, 0))]
    aliases = {}
    if prev is not None:
        ins += list(prev)
        specs += [pl.BlockSpec(memory_space=pl.ANY)] * 4
        aliases = {4 + t: t for t in range(4)}
    shape = jax.ShapeDtypeStruct(w.shape, F32)
    return pl.pallas_call(body, name=name, grid=(nb,), in_specs=specs, out_specs=[blk] * 4, out_shape=[shape] * 4,
                          input_output_aliases=aliases, compiler_params=_params(("parallel",)))(*ins)


HBM = pl.BlockSpec(memory_space=pltpu.HBM)
SEM = pl.BlockSpec(memory_space=pltpu.SEMAPHORE)
EFFECT = pltpu.SideEffectType.DATAFLOW_SIDE_EFFECTING


PEERS = {"scatter": (1, 2, 3, 4, 5, 6, 7), "gather": (1, 2, 3, 4, 5, 6, 7), "chips": (1, 2, 4, 6), "forward": (2, 4, 6)}


def _spread_copies(srcs, lands, send_sems, recv_sems, local_sems, kind, waiting):
    x, y, c = lax.axis_index("x"), lax.axis_index("y"), lax.axis_index("c")
    me = 4 * x + 2 * y + c

    def peer(bits):
        dev = (1 - x if bits & 4 else x, 1 - y if bits & 2 else y, 1 - c if bits & 1 else c)
        return dev, 4 * dev[0] + 2 * dev[1] + dev[2]

    plan = PEERS[kind]
    remote, local = [], []
    for a, l in enumerate(lands):
        for d, bits in enumerate(plan):
            dev, pid = peer(bits)
            if kind == "forward":
                src, dst, dev = l.at[pid], l.at[peer(bits | 1)[1] if waiting else pid], peer(1)[0]
            else:
                src, dst = (srcs[a].at[pid] if kind == "scatter" else srcs[a]), l.at[pid if waiting else me]
            remote.append(pltpu.make_async_remote_copy(
                src_ref=src, dst_ref=dst, send_sem=send_sems.at[a * len(plan) + d], recv_sem=recv_sems.at[a * len(plan) + d],
                device_id=dev, device_id_type=MESH_ID))
        if kind != "forward":
            local.append(pltpu.make_async_copy(srcs[a].at[me] if kind == "scatter" else srcs[a], l.at[me], local_sems.at[a]))
    return remote, local


def _spread_start(srcs, kind, name, after=None, lands=None):
    if kind == "forward":
        srcs = []
    else:
        shapes = [a.shape if kind == "scatter" else (N_DEV,) + a.shape for a in srcs]
        lands = [lax.empty(shp, a.dtype) for shp, a in zip(shapes, srcs)]
    ns, nl, per = len(srcs), len(lands), len(PEERS[kind])
    extra = [] if after is None else [after]
    sem_shapes = [pltpu.SemaphoreType.DMA((nl * per,))] * 2 + ([pltpu.SemaphoreType.DMA((nl,))] if ns else [])

    def body(*refs):
        src_refs, land_refs = refs[:ns], refs[ns:ns + nl]
        sems = refs[ns + nl + len(extra):ns + nl + len(extra) + len(sem_shapes)]
        remote, local = _spread_copies(src_refs, land_refs, sems[0], sems[1], sems[2] if ns else None, kind, False)
        for cp in remote + local:
            cp.start()
        refs[-1][...] = jnp.zeros((8, 128), F32)

    outs = pl.pallas_call(
        body, name=name,
        out_shape=(*sem_shapes, *[pltpu.HBM(a.shape, a.dtype) for a in srcs + lands], jax.ShapeDtypeStruct((8, 128), F32)),
        in_specs=[HBM] * (ns + nl) + [pl.BlockSpec(memory_space=pl.ANY)] * len(extra),
        out_specs=(*[SEM] * len(sem_shapes), *[HBM] * (ns + nl), pl.BlockSpec(memory_space=pltpu.VMEM)),
        input_output_aliases={i: len(sem_shapes) + i for i in range(ns + nl)},
        compiler_params=pltpu.CompilerParams(has_side_effects=EFFECT),
    )(*[pltpu.with_memory_space_constraint(a, pltpu.HBM) for a in srcs + lands], *extra)
    k = len(sem_shapes)
    return outs[:k], list(outs[k:k + ns]), list(outs[k + ns:k + ns + nl]), outs[-1]


def _spread_wait(sems, srcs, lands, after, kind, name):
    ns, nl = len(srcs), len(lands)
    after = list(after) if isinstance(after, (list, tuple)) else [after]

    def body(*refs):
        src_refs, land_refs = refs[:ns], refs[ns:ns + nl]
        s = refs[ns + nl:ns + nl + len(sems)]
        remote, local = _spread_copies(src_refs, land_refs, s[0], s[1], s[2] if ns else None, kind, True)
        for cp in remote:
            cp.wait_send()
            cp.wait_recv()
        for cp in local:
            cp.wait()

    outs = pl.pallas_call(
        body, name=name, out_shape=tuple(pltpu.HBM(a.shape, a.dtype) for a in srcs + lands),
        in_specs=[HBM] * (ns + nl) + [SEM] * len(sems) + [pl.BlockSpec(memory_space=pl.ANY)] * len(after), out_specs=tuple([HBM] * (ns + nl)),
        input_output_aliases={i: i for i in range(ns + nl)}, compiler_params=pltpu.CompilerParams(has_side_effects=EFFECT),
    )(*srcs, *lands, *sems, *after)
    return list(outs[ns:])


SMALL = ("rel_bias", "norm1_g", "sgu_w", "sgu_b", "dil_qn_g", "dil_kn_g", "conv_w", "conv_b", "conv_ln_g", "conv_ln_b",
         "gqa_qn_g", "gqa_kn_g", "mix_norm_g", "norm2_g")
LARGE = ("w_in", "w_out", "w_gate", "w_up", "w_down")
EARLY = tuple(k for k in SMALL if k != "norm1_g")


def _local_step(x, target, p, B, S, fetch, emit, mid, early):
    T = B * S
    rope = _rope_tables(S)
    win = _bias_windows(p["rel_bias"], S)
    tile8 = lambda g: jnp.tile(g.reshape(1, HEAD_DIM), (1, N_HEADS))
    cols_b = (COL_BQ, COL_BK, COL_BV)
    cols_d = (COL_DQ, COL_DK128, COL_DV128)
    saved = []
    for l in range(DEPTH):
        s = {"x": x}
        s["ws"] = p["sgu_w"][l].astype(BF16)
        s["bias"] = jnp.repeat(p["sgu_b"][l].T, HEAD_DIM, axis=1)
        s["h"] = _rms_fwd(x, p["norm1_g"][l], f"rms1_fwd_{l}") if l == 0 else h_next
        s["win"] = fetch(l, "in", s["h"])
        s["cw"] = jnp.pad(s["win"]["conv_w"], ((0, 1), (0, 0))).reshape(32, 1, 512)
        z = s["z"] = _matmul(s["h"], s["win"]["w_in"], "nt", f"in_proj_{l}", tk=D_MODEL)
        s["bias"] = s["bias"] + mid(l, z)
        s["ya"] = _sgu_fwd(z, s["ws"], s["bias"], f"sgu_fwd_{l}")
        s["c"] = _conv_fwd1(z, s["cw"], p["conv_b"][l].reshape(1, 512), B, S, f"conv_fwd_{l}")
        s["ln"] = (p["conv_ln_g"][l].reshape(1, 512), p["conv_ln_b"][l].reshape(1, 512))
        s["gb"] = (tile8(p["dil_qn_g"][l]), tile8(p["dil_kn_g"][l]))
        s["gd"] = (tile8(p["gqa_qn_g"][l]), tile8(p["gqa_kn_g"][l])[:, :KV_WIDTH])
        s["qkv_b"] = _prep_fwd(z, *s["gb"], None, B, S, N_HEADS, cols_b, f"prep_b_fwd_{l}")
        s["qkv_d"] = _prep_fwd(z, *s["gd"], rope, B, S, KV_HEADS, cols_d, f"prep_d_fwd_{l}")
        s["ob"] = _attn_fwd(*s["qkv_b"], win, f"attn_b_fwd_{l}")
        s["od"] = _attn_fwd(*s["qkv_d"], None, f"attn_d_fwd_{l}", tq=GQA_TQ)
        s["gmix"] = p["mix_norm_g"][l].reshape(1, 2048)
        s["ycat"] = _mix_fwd(s["ya"], s["ob"], s["c"], s["od"], s["gmix"], *s["ln"], B, S, f"mix_fwd_{l}")
        s["wout"] = fetch(l, "out", s["ycat"])["w_out"]
        x1, s["h2"] = _matmul(s["ycat"], s["wout"], "nn", f"out_proj_{l}", res=x, tk=D_MODEL, norm=p["norm2_g"][l])
        s["x1"] = x1
        s["ffn"] = fetch(l, "ffn", s["h2"])
        s["gate"] = _mm_shard_out(s["h2"], s["ffn"]["w_gate"], "nt", f"ffn_gate_{l}", out_dtype=BF16, tm=1024)
        s["up"] = _mm_shard_out(s["h2"], s["ffn"]["w_up"], "nt", f"ffn_up_{l}", out_dtype=BF16, tm=1024)
        if l + 1 < DEPTH:
            x, s["act"], h_next = _ffn_down(s["gate"], s["up"], s["ffn"]["w_down"], x1, f"ffn_down_{l}", norm=p["norm1_g"][l + 1])
        else:
            dx, s["act"], loss_blk, dxb = _ffn_down(s["gate"], s["up"], s["ffn"]["w_down"], x1, f"ffn_down_{l}", target=target)
        saved.append(s)

    g = {k: [None] * DEPTH for k in SMALL if k != "rel_bias"}
    dwin_total = None
    for l in reversed(range(DEPTH)):
        s = saved[l]
        z, ffn = s["z"], s["ffn"]
        dgate, dup = _ffn_down_dx(dxb, ffn["w_down"], s["gate"], s["up"], f"ffn_down_dx_{l}")
        tok = emit(l, "w_down", _mm_shard_m(s["act"], dxb, f"ffn_down_dw_{l}", out_dtype=BF16, tm=FFN_BLOCK, tn=512, tk=T))
        tok += emit(l, "w_gate", _mm_shard_m(dgate, s["h2"], f"ffn_gate_dw_{l}", out_dtype=BF16, tm=FFN_BLOCK, tn=512, tk=T))
        tok += emit(l, "w_up", _mm_shard_m(dup, s["h2"], f"ffn_up_dw_{l}", out_dtype=BF16, tm=FFN_BLOCK, tn=512, tk=T))
        dh2 = _mm_shard_k([(dgate, ffn["w_gate"]), (dup, ffn["w_up"])], "nn", f"ffn_up_dx_{l}", tn=512, fold=FFN_GROUPS)
        dycat, dx1, dx1b, dg2 = _out_proj_dx(dh2, s["x1"], p["norm2_g"][l] + tok, dx, s["wout"], f"out_proj_dx_{l}")
        g["norm2_g"][l] = dg2[0]
        tok = emit(l, "w_out", _matmul(s["ycat"], dx1b, "tn", f"out_proj_dw_{l}", out_dtype=BF16, tn=D_MODEL, tk=T))
        dya, dob, dc, dod, dgm, dlg, dlb, dcb = _mix_bwd(s["ya"], s["ob"], s["c"], s["od"], dycat, s["gmix"] + tok, *s["ln"], B, S, f"mix_bwd_{l}")
        g["mix_norm_g"][l] = dgm[0]
        dz_a, dws, dbias = _sgu_bwd(z, dya, s["ws"], jnp.swapaxes(s["ws"], 1, 2), s["bias"], f"sgu_bwd_{l}")
        g["sgu_w"][l] = dws
        g["sgu_b"][l] = dbias.reshape(128, 8, HEAD_DIM).sum(-1).T
        g["conv_ln_g"][l], g["conv_ln_b"][l], g["conv_b"][l] = dlg[0], dlb[0], dcb[0]
        dz_ca, dz_cg, dcw = _conv_bwd2(z, dc, s["cw"], B, S, f"conv_bwd_{l}")
        g["conv_w"][l] = dcw.reshape(32, 512)[:CONV_WIDTH]
        dq, dk, dv, dwin = _attn_bwd(*s["qkv_b"], s["ob"], dob, win, f"attn_b_bwd_{l}")
        dwin_total = dwin if dwin_total is None else dwin_total + dwin
        dz_b, dgq, dgk = _prep_bwd(z, dq, dk, dv, *s["gb"], None, B, S, N_HEADS, cols_b, f"prep_b_bwd_{l}")
        g["dil_qn_g"][l] = dgq.reshape(N_HEADS, HEAD_DIM).sum(0)
        g["dil_kn_g"][l] = dgk.reshape(N_HEADS, HEAD_DIM).sum(0)
        dq, dk, dv = _attn_bwd(*s["qkv_d"], s["od"], dod, None, f"attn_d_bwd_{l}", tq=GQA_TQ)
        dz_d, dgq, dgk = _prep_bwd(z, dq, dk, dv, *s["gd"], rope, B, S, KV_HEADS, cols_d, f"prep_d_bwd_{l}")
        g["gqa_qn_g"][l] = dgq.reshape(N_HEADS, HEAD_DIM).sum(0)
        g["gqa_kn_g"][l] = dgk.reshape(KV_HEADS, HEAD_DIM).sum(0)
        dz = [dz_a, dz_b, dz_ca, dz_cg, dz_d]
        tok = jnp.zeros((), F32)
        if l == 0:
            done = {k: jnp.stack(v) for k, v in g.items() if k != "norm1_g"}
            done["rel_bias"] = _bias_fold(dwin_total, S, "bias_fold")
            tok = early(done)
        tok += emit(l, "w_in", _in_proj_dw(dz, s["h"], f"in_proj_dw_{l}"))
        dx, dxb, dg1 = _in_proj_dx(dz, s["win"]["w_in"], s["x"], p["norm1_g"][l] + tok, dx1, f"in_proj_dx_{l}")
        g["norm1_g"][l] = dg1[0]

    return loss_blk[0, 0], dx, jnp.stack(g["norm1_g"])


GROUPS = {"in": ("w_in",), "out": ("w_out",), "ffn": ("w_gate", "w_up", "w_down")}
COL_SHARDED = ("w_in", "w_gate", "w_up")


def kernel(x, rel_bias, norm1_g, w_in, sgu_w, sgu_b, dil_qn_g, dil_kn_g, conv_w, conv_b, conv_ln_g, conv_ln_b, gqa_qn_g, gqa_kn_g, mix_norm_g, w_out, norm2_g, w_gate, w_up, w_down, loss_target, m_rel_bias, m_norm1_g, m_w_in, m_sgu_w, m_sgu_b, m_dil_qn_g, m_dil_kn_g, m_conv_w, m_conv_b, m_conv_ln_g, m_conv_ln_b, m_gqa_qn_g, m_gqa_kn_g, m_mix_norm_g, m_w_out, m_norm2_g, m_w_gate, m_w_up, m_w_down, v_rel_bias, v_norm1_g, v_w_in, v_sgu_w, v_sgu_b, v_dil_qn_g, v_dil_kn_g, v_conv_w, v_conv_b, v_conv_ln_g, v_conv_ln_b, v_gqa_qn_g, v_gqa_kn_g, v_mix_norm_g, v_w_out, v_norm2_g, v_w_gate, v_w_up, v_w_down):
    w = dict(rel_bias=rel_bias, norm1_g=norm1_g, w_in=w_in, sgu_w=sgu_w, sgu_b=sgu_b, dil_qn_g=dil_qn_g, dil_kn_g=dil_kn_g, conv_w=conv_w,
             conv_b=conv_b, conv_ln_g=conv_ln_g, conv_ln_b=conv_ln_b, gqa_qn_g=gqa_qn_g, gqa_kn_g=gqa_kn_g, mix_norm_g=mix_norm_g,
             w_out=w_out, norm2_g=norm2_g, w_gate=w_gate, w_up=w_up, w_down=w_down)
    m = dict(rel_bias=m_rel_bias, norm1_g=m_norm1_g, w_in=m_w_in, sgu_w=m_sgu_w, sgu_b=m_sgu_b, dil_qn_g=m_dil_qn_g, dil_kn_g=m_dil_kn_g,
             conv_w=m_conv_w, conv_b=m_conv_b, conv_ln_g=m_conv_ln_g, conv_ln_b=m_conv_ln_b, gqa_qn_g=m_gqa_qn_g, gqa_kn_g=m_gqa_kn_g,
             mix_norm_g=m_mix_norm_g, w_out=m_w_out, norm2_g=m_norm2_g, w_gate=m_w_gate, w_up=m_w_up, w_down=m_w_down)
    v = dict(rel_bias=v_rel_bias, norm1_g=v_norm1_g, w_in=v_w_in, sgu_w=v_sgu_w, sgu_b=v_sgu_b, dil_qn_g=v_dil_qn_g, dil_kn_g=v_dil_kn_g,
             conv_w=v_conv_w, conv_b=v_conv_b, conv_ln_g=v_conv_ln_g, conv_ln_b=v_conv_ln_b, gqa_qn_g=v_gqa_qn_g, gqa_kn_g=v_gqa_kn_g,
             mix_norm_g=v_mix_norm_g, w_out=v_w_out, norm2_g=v_norm2_g, w_gate=v_w_gate, w_up=v_w_up, w_down=v_w_down)
    names = list(w)
    B, S, D = x.shape
    T = B * S
    me = 4 * lax.axis_index("x") + 2 * lax.axis_index("y") + lax.axis_index("c")

    view = lambda a, k: jnp.swapaxes(a, 1, 2) if k in COL_SHARDED else a
    bf = {k: view(w[k], k).astype(BF16) for k in LARGE}
    spreads, forwards = {}, {}

    def start_gather(l, group, after=None):
        srcs = [bf[k][l] for k in GROUPS[group]] + ([conv_w[l]] if group == "in" else [])
        spreads[l, group] = _spread_start(srcs, "chips", f"gather_{group}_{l}_start", after)
        return spreads[l, group][3][0, 0]

    def forward(l, group, after):
        sems, srcs, lands, _ = spreads[l, group]
        lands = _spread_wait(sems, srcs, lands, after, "chips", f"gather_{group}_{l}_wait")
        forwards[l, group] = _spread_start(None, "forward", f"forward_{group}_{l}_start", lands=lands)
        return forwards[l, group][3]

    def landed(l, group, after):
        sems, _, lands, _ = forwards[l, group]
        return _spread_wait(sems, [], lands, after, "forward", f"forward_{group}_{l}_wait")

    tok0 = start_gather(0, "in") + start_gather(0, "out") + start_gather(0, "ffn")
    small = {k: w[k] for k in SMALL}
    small["norm1_g"] = norm1_g.at[0].add(tok0)

    def mid(l, z):
        if l > 0:
            return jnp.zeros((), F32)
        return start_gather(1, "in", z) + start_gather(1, "out", z) + start_gather(1, "ffn", z)

    def fetch(l, group, after):
        if group == "in":
            tok = forward(0, "in", after) if l == 0 else after
        elif group == "out":
            tok = forward(l, "ffn", [after, forward(l, "out", after)])
        else:
            tok = forward(1, "in", after) if l == 0 else after
        got = dict(zip(GROUPS[group] + ("conv_w",), landed(l, group, [after, tok])))
        if group == "in":
            got["w_in"] = got["w_in"].reshape(IN_WIDTH, D)
            got["conv_w"] = jnp.transpose(got["conv_w"], (1, 0, 2)).reshape(CONV_WIDTH, 512)
        if group == "out":
            got["w_out"] = got["w_out"].reshape(D, D)
        if group == "ffn":
            got = {k: a.reshape(FFN_GROUPS, -1, D) for k, a in got.items()}
        return got

    scatters = {}

    def emit(l, k, dw):
        dw = dw.reshape(N_DEV, -1, D)
        scatters[l, k] = _spread_start([dw], "scatter", f"scatter_{k}_{l}_start")
        return scatters[l, k][3][0, 0]

    flat2 = lambda a: a.reshape(-1, a.shape[-1])
    small_spread = []

    def early(done):
        small_spread.append(_spread_start([flat2(done[k]) for k in EARLY], "gather", "gather_small_grads_start"))
        return small_spread[0][3][0, 0]

    loss_part, dx, dnorm1 = _local_step(x.reshape(T, D), loss_target.reshape(T, D), small, B, S, fetch, emit, mid, early)
    loss = lax.psum(loss_part, ("x", "y", "c"))

    out_g, out_d, out_m, out_v = {}, {}, {}, {}

    def update_large(k, after):
        shp = view(w[k], k).shape
        two_d = lambda a: view(a, k).reshape(-1, shp[-1])
        res = None
        for l in reversed(range(DEPTH)):
            sems, srcs, lands, _ = scatters[l, k]
            stack = _spread_wait(sems, srcs, lands, after, "scatter", f"scatter_{k}_{l}_wait")[0]
            res = _adamw(two_d(w[k]), two_d(m[k]), two_d(v[k]), stack.reshape(N_DEV, -1, shp[-1]), f"adamw_{k}_{l}", layer=l, prev=res)
        out_g[k], out_d[k], out_m[k], out_v[k] = [view(a.reshape(shp), k) for a in res]
        return res[0]

    late_sems, late_srcs, late_lands, late_tok = _spread_start([flat2(dnorm1)], "gather", "gather_norm1_grad_start")
    after = [dx, late_tok]
    for k in ("w_down", "w_gate", "w_up", "w_out"):
        after = update_large(k, after)
    sems, srcs, lands, _ = small_spread[0]
    stacks = dict(zip(EARLY, _spread_wait(sems, srcs, lands, after, "gather", "gather_small_grads_wait")))
    stacks["norm1_g"] = _spread_wait(late_sems, late_srcs, late_lands, after, "gather", "gather_norm1_grad_wait")[0]
    for k in SMALL:
        stack = stacks[k]
        if k == "conv_w":
            stack = lax.dynamic_slice_in_dim(stack, me * (512 // N_DEV), 512 // N_DEV, axis=2)
        res = _adamw(flat2(w[k]), flat2(m[k]), flat2(v[k]), stack, f"adamw_{k}")
        out_g[k], out_d[k], out_m[k], out_v[k] = [a.reshape(w[k].shape) for a in res]
        after = res[0]
    update_large("w_in", after)

    return (loss, dx.reshape(B, S, D), *[out_g[k] for k in names], *[out_d[k] for k in names],
            *[out_m[k] for k in names], *[out_v[k] for k in names])
```

```python
import functools
import math

import numpy as np
import jax
import jax.numpy as jnp
from jax import lax
from jax.experimental import pallas as pl
from jax.experimental.pallas import tpu as pltpu

F32 = jnp.float32
BF16 = jnp.bfloat16
HIGHEST = lax.Precision.HIGHEST
MESH_ID = pl.DeviceIdType.MESH

D_MODEL = 2048
DEPTH = 2
HEAD_DIM = 64
GROUP_WIDTH = 512
N_HEADS = 8
KV_HEADS = 2
KV_WIDTH = 128
SGU_CHUNK = 128
CONV_WIDTH = 31
CONV_PAD = 16
GRID_W = 64
ROPE_THETA = 10000.0
REL_BUCKETS = 32
REL_MAX_DIST = 1024
DIL_PATTERNS = ((128, 1), (512, 4), (2048, 16))
FFN_HIDDEN = 5632
IN_WIDTH = 4352
RMS_EPS = 1e-6
LN_EPS = 1e-5
MASKED = -1e30
N_DEV = 8

ADAM_LR = 0.001
ADAM_B1 = 0.9
ADAM_B2 = 0.999
ADAM_EPS = 1e-08
ADAM_WD = 0.01
ADAM_STEP = 10

COL_AU, COL_AV, COL_BQ, COL_BK, COL_BV, COL_CA, COL_CG, COL_DQ = range(8)
COL_DK128, COL_DV128 = 32, 33

VMEM_LIMIT = 56 * 1024 * 1024
FFN_GROUPS = 2
FFN_BLOCK = 1408
ATTN_TQ = 256
GQA_TQ = 512
ROW_TILE = 512


def _params(sem=None, vmem=VMEM_LIMIT):
    return pltpu.CompilerParams(dimension_semantics=sem, vmem_limit_bytes=vmem)


def _dot(a, b, dims, precision=None):
    return lax.dot_general(a, b, (dims, ((), ())), precision=precision, preferred_element_type=F32)


def _nn(a, b, precision=None):
    return _dot(a, b, ((1,), (0,)), precision)


def _nt(a, b):
    return _dot(a, b, ((1,), (1,)))


def _tn(a, b):
    return _dot(a, b, ((0,), (0,)))


DIMS = {"nn": ((1,), (0,)), "nt": ((1,), (1,)), "tn": ((0,), (0,))}


def _pick(n, cands):
    for c in cands:
        if n % c == 0:
            return c
    return n


def _mm_call(name, mode, pairs, specs, o_spec, out_sds, grid, acc_shape, res=None, fold=None, norm=None):
    npair, nk, dims = len(pairs), grid[2], DIMS[mode]

    def body(*refs):
        ab = refs[:2 * npair]
        at = 2 * npair
        r_ref = refs[at] if res is not None else None
        at += res is not None
        g_ref = refs[at] if norm is not None else None
        at += norm is not None
        o_ref = refs[at]
        h_ref = refs[at + 1] if norm is not None else None
        part = None
        for t in range(npair):
            for s in ([None] if fold is None else range(fold)):
                a_blk = ab[2 * t][...] if s is None else ab[2 * t][s]
                b_blk = ab[2 * t + 1][...] if s is None else ab[2 * t + 1][s]
                d = _dot(a_blk.astype(BF16), b_blk.astype(BF16), dims)
                part = d if part is None else part + d

        def finish(r):
            if r_ref is not None:
                r = r + r_ref[...]
            o_ref[...] = r.astype(o_ref.dtype)
            if h_ref is not None:
                h_ref[...] = _rms_rows(r, g_ref[...])

        if nk == 1:
            finish(part)
            return
        acc, k = refs[-1], pl.program_id(2)

        @pl.when(k == 0)
        def _():
            acc[...] = part

        @pl.when(k > 0)
        def _():
            acc[...] += part

        @pl.when(k == nk - 1)
        def _():
            finish(acc[...])

    ins = [t for pair in pairs for t in pair]
    in_specs = [t for pair in specs for t in pair]
    if res is not None:
        ins.append(res)
        in_specs.append(o_spec)
    out_specs = o_spec
    if norm is not None:
        ins.append(norm)
        in_specs.append(pl.BlockSpec(norm.shape, lambda *_: (0, 0)))
        out_specs, out_sds = [o_spec, o_spec], [out_sds, jax.ShapeDtypeStruct(out_sds.shape, BF16)]
    return pl.pallas_call(
        body, name=name, grid=grid, in_specs=in_specs, out_specs=out_specs, out_shape=out_sds,
        scratch_shapes=[pltpu.VMEM(acc_shape, F32)] if nk > 1 else [],
        compiler_params=_params(("parallel", "parallel", "arbitrary")),
    )(*ins)


def _rms_rows(x, g):
    return (x * lax.rsqrt(jnp.mean(x * x, axis=-1, keepdims=True) + RMS_EPS) * g).astype(BF16)


def _matmul(a, b, mode, name, res=None, out_dtype=F32, tm=512, tn=None, tk=None, norm=None):
    if mode == "nn":
        (M, K), N = a.shape, b.shape[1]
    elif mode == "nt":
        (M, K), N = a.shape, b.shape[0]
    else:
        (K, M), N = a.shape, b.shape[1]
    tm = min(tm, M)
    tn = tn or _pick(N, (2176, 2048, 1408, 1024, 512))
    tk = tk or _pick(K, (1024, 2176, 1408, 512))
    assert M % tm == 0 and N % tn == 0 and K % tk == 0, (M, N, K, tm, tn, tk)
    a_spec = pl.BlockSpec((tk, tm), lambda i, j, k: (k, i)) if mode == "tn" else pl.BlockSpec((tm, tk), lambda i, j, k: (i, k))
    b_spec = pl.BlockSpec((tn, tk), lambda i, j, k: (j, k)) if mode == "nt" else pl.BlockSpec((tk, tn), lambda i, j, k: (k, j))
    o_spec = pl.BlockSpec((tm, tn), lambda i, j, k: (i, j))
    assert norm is None or tn == N
    return _mm_call(name, mode, [(a, b)], [(a_spec, b_spec)], o_spec, jax.ShapeDtypeStruct((M, N), out_dtype),
                    (M // tm, N // tn, K // tk), (tm, tn), res, norm=None if norm is None else norm.reshape(1, N))


def _mm_shard_out(a, bs, mode, name, out_dtype=F32, tm=512, tk=None):
    J = bs.shape[0]
    n = bs.shape[1] if mode == "nt" else bs.shape[2]
    (K, M) = a.shape if mode == "tn" else a.shape[::-1]
    tm = min(tm, M)
    tk = tk or (K if mode != "tn" else _pick(K, (1024, 512)))
    a_spec = pl.BlockSpec((tk, tm), lambda j, i, k: (k, i)) if mode == "tn" else pl.BlockSpec((tm, tk), lambda j, i, k: (i, k))
    b_spec = pl.BlockSpec((None, n, tk), lambda j, i, k: (j, 0, k)) if mode == "nt" else pl.BlockSpec((None, tk, n), lambda j, i, k: (j, k, 0))
    o_spec = pl.BlockSpec((None, tm, n), lambda j, i, k: (j, i, 0))
    return _mm_call(name, mode, [(a, bs)], [(a_spec, b_spec)], o_spec, jax.ShapeDtypeStruct((J, M, n), out_dtype),
                    (J, M // tm, K // tk), (tm, n))


def _mm_shard_k(pairs, mode, name, res=None, out_dtype=F32, tm=512, tn=None, fold=1):
    J, M, n = pairs[0][0].shape
    N = pairs[0][1].shape[2] if mode == "nn" else pairs[0][1].shape[1]
    tm = min(tm, M)
    tn = tn or _pick(N, (2048, 1024, 512))
    a_spec = pl.BlockSpec((fold, tm, n), lambda i, j, k: (k, i, 0))
    b_spec = pl.BlockSpec((fold, n, tn), lambda i, j, k: (k, 0, j)) if mode == "nn" else pl.BlockSpec((fold, tn, n), lambda i, j, k: (k, j, 0))
    o_spec = pl.BlockSpec((tm, tn), lambda i, j, k: (i, j))
    return _mm_call(name, mode, pairs, [(a_spec, b_spec)] * len(pairs), o_spec, jax.ShapeDtypeStruct((M, N), out_dtype),
                    (M // tm, N // tn, J // fold), (tm, tn), res, fold)


def _mm_shard_m(as_, b, name, out_dtype=F32, tm=None, tn=None, tk=512):
    J, K, n = as_.shape
    N = b.shape[1]
    tm = tm or n
    tn = tn or _pick(N, (2048, 1024, 512))
    tk = min(tk, K)
    nn = N // tn
    a_spec = pl.BlockSpec((None, tk, tm), lambda j, i, k: (j, k, i // nn))
    b_spec = pl.BlockSpec((tk, tn), lambda j, i, k: (k, i % nn))
    o_spec = pl.BlockSpec((None, tm, tn), lambda j, i, k: (j, i // nn, i % nn))
    return _mm_call(name, "tn", [(as_, b)], [(a_spec, b_spec)], o_spec, jax.ShapeDtypeStruct((J, n, N), out_dtype),
                    (J, (n // tm) * nn, K // tk), (tm, tn))


def _out_proj_dx(dh, x, g, dres, w, name):
    T, D = x.shape
    N = w.shape[0]
    tm = min(256, T)
    steps = T // tm
    slots = 3

    def body(dh_hbm, x_hbm, g_ref, dres_hbm, w_ref, o_ref, dx_ref, dxb_ref, dg_ref, ring, sems):
        i = pl.program_id(0)
        streams = (dh_hbm, x_hbm, dres_hbm)

        def fetch(step, slot):
            rows = pl.ds(pl.multiple_of(step * tm, tm), tm)
            return [pltpu.make_async_copy(src.at[rows, :], ring.at[k, slot], sems.at[k, slot]) for k, src in enumerate(streams)]

        @pl.when(i == 0)
        def _():
            dg_ref[...] = jnp.zeros_like(dg_ref)
            for step in range(min(slots - 1, steps)):
                for cp in fetch(step, step):
                    cp.start()

        @pl.when(i + slots - 1 < steps)
        def _():
            for cp in fetch(i + slots - 1, (i + slots - 1) % slots):
                cp.start()

        slot = i % slots
        for cp in fetch(i, slot):
            cp.wait()
        dhv, xv, dres = ring[0, slot], ring[1, slot], ring[2, slot]
        r = lax.rsqrt(jnp.mean(xv * xv, axis=-1, keepdims=True) + RMS_EPS)
        y = xv * r
        dy = dhv * g_ref[...]
        dx = dres + r * (dy - y * jnp.mean(dy * y, axis=-1, keepdims=True))
        dx_ref[...] = dx
        dxb = dx.astype(BF16)
        dxb_ref[...] = dxb
        dg_ref[...] += jnp.sum(dhv * y, axis=0, keepdims=True)
        o_ref[...] = _nt(dxb, w_ref[...])

    row = pl.BlockSpec((tm, D), lambda i: (i, 0))
    vec = pl.BlockSpec((1, D), lambda i: (0, 0))
    hbm = pl.BlockSpec(memory_space=pl.ANY)
    return pl.pallas_call(
        body, name=name, grid=(steps,), in_specs=[hbm, hbm, vec, hbm, pl.BlockSpec((N, D), lambda i: (0, 0))],
        out_specs=[pl.BlockSpec((tm, N), lambda i: (i, 0)), row, row, vec],
        out_shape=[jax.ShapeDtypeStruct((T, N), F32), jax.ShapeDtypeStruct((T, D), F32), jax.ShapeDtypeStruct((T, D), BF16),
                   jax.ShapeDtypeStruct((1, D), F32)],
        scratch_shapes=[pltpu.VMEM((3, slots, tm, D), F32), pltpu.SemaphoreType.DMA((3, slots))],
        compiler_params=_params(("arbitrary",)),
    )(dh, x, g.reshape(1, D), dres, w)


def _in_proj_dw(pieces, h, name):
    T, D = h.shape
    tm = 256
    nbs = [p.shape[1] // tm for p in pieces]
    los = [sum(nbs[:t]) for t in range(len(pieces))]

    def body(*refs):
        h_ref, o_ref = refs[-2:]
        i = pl.program_id(0)
        for p_ref, lo, nb in zip(refs[:-2], los, nbs):
            @pl.when((i >= lo) & (i < lo + nb))
            def _():
                o_ref[...] = _tn(p_ref[...], h_ref[...]).astype(BF16)

    specs = [pl.BlockSpec((T, tm), (lambda lo, nb: lambda i: (0, jnp.clip(i - lo, 0, nb - 1)))(lo, nb)) for lo, nb in zip(los, nbs)]
    return pl.pallas_call(body, name=name, grid=(sum(nbs),), in_specs=specs + [pl.BlockSpec((T, D), lambda i: (0, 0))],
                          out_specs=pl.BlockSpec((tm, D), lambda i: (i, 0)), out_shape=jax.ShapeDtypeStruct((sum(nbs) * tm, D), BF16),
                          compiler_params=_params(("parallel",)))(*pieces, h)


def _in_proj_dx(pieces, w, x, g, dres, name):
    T, D = x.shape
    K = w.shape[0]
    tm = min(256, T)
    n = len(pieces)
    widths = [p.shape[1] for p in pieces]
    offs = [sum(widths[:t]) for t in range(n)]

    def body(*refs):
        w_ref, x_ref, g_ref, dres_ref, dx_ref, dxb_ref, dg_ref = refs[n:]

        @pl.when(pl.program_id(0) == 0)
        def _():
            dg_ref[...] = jnp.zeros_like(dg_ref)

        dh = None
        for p_ref, off, wd in zip(refs[:n], offs, widths):
            d = _nn(p_ref[...], w_ref[off:off + wd, :])
            dh = d if dh is None else dh + d
        xv = x_ref[...]
        r = lax.rsqrt(jnp.mean(xv * xv, axis=-1, keepdims=True) + RMS_EPS)
        y = xv * r
        dy = dh * g_ref[...]
        dx = dres_ref[...] + r * (dy - y * jnp.mean(dy * y, axis=-1, keepdims=True))
        dx_ref[...] = dx
        dxb_ref[...] = dx.astype(BF16)
        dg_ref[...] += jnp.sum(dh * y, axis=0, keepdims=True)

    specs = [pl.BlockSpec((tm, wd), lambda i: (i, 0)) for wd in widths]
    row = pl.BlockSpec((tm, D), lambda i: (i, 0))
    vec = pl.BlockSpec((1, D), lambda i: (0, 0))
    return pl.pallas_call(body, name=name, grid=(T // tm,), in_specs=specs + [pl.BlockSpec((K, D), lambda i: (0, 0)), row, vec, row],
                          out_specs=[row, row, vec],
                          out_shape=[jax.ShapeDtypeStruct((T, D), F32), jax.ShapeDtypeStruct((T, D), BF16), jax.ShapeDtypeStruct((1, D), F32)],
                          compiler_params=_params(("arbitrary",)))(*pieces, w, x, g.reshape(1, D), dres)


def _seg_matrix(width):
    return jnp.asarray(np.kron(np.eye(width // HEAD_DIM, dtype=np.float32), np.full((HEAD_DIM, HEAD_DIM), 1.0 / HEAD_DIM, np.float32)), BF16)


def _segmean(v, p):
    hi = v.astype(BF16)
    r = v - hi.astype(F32)
    mid = r.astype(BF16)
    lo = (r - mid.astype(F32)).astype(BF16)
    w = min(256, v.shape[1])
    pw = p[:w, :w]
    halves = []
    for c in range(v.shape[1] // w):
        cols = slice(c * w, (c + 1) * w)
        halves.append(_nn(hi[:, cols], pw) + _nn(mid[:, cols], pw) + _nn(lo[:, cols], pw))
    return halves[0] if len(halves) == 1 else jnp.concatenate(halves, axis=1)


def _gelu(x):
    c0 = math.sqrt(2.0 / math.pi)
    t = jnp.tanh(c0 * (x + 0.044715 * x * x * x))
    return 0.5 * x * (1.0 + t), t


def _gelu_grad(x, t):
    c0 = math.sqrt(2.0 / math.pi)
    return 0.5 * (1.0 + t) + 0.5 * x * (1.0 - t * t) * c0 * (1.0 + 3.0 * 0.044715 * x * x)


def _sigmoid(x):
    return 1.0 / (1.0 + jnp.exp(-x))


def _rms_fwd(x, g, name):
    T, D = x.shape
    tm = min(256, T)

    def body(x_ref, g_ref, o_ref):
        o_ref[...] = _rms_rows(x_ref[...], g_ref[...])

    return pl.pallas_call(
        body, name=name, grid=(T // tm,),
        in_specs=[pl.BlockSpec((tm, D), lambda i: (i, 0)), pl.BlockSpec((1, D), lambda i: (0, 0))],
        out_specs=pl.BlockSpec((tm, D), lambda i: (i, 0)), out_shape=jax.ShapeDtypeStruct((T, D), BF16),
        compiler_params=_params(("parallel",)),
    )(x, g.reshape(1, D))


def _sgu_core(zu, zv, ws_ref, bias, p):
    ug, tu = _gelu(zu)
    vg, tv = _gelu(zv)
    xc = vg - _segmean(vg, p)
    rs = lax.rsqrt(_segmean(xc * xc, p) + LN_EPS)
    vn = xc * rs
    vnb = vn.astype(BF16)
    low = lax.broadcasted_iota(jnp.int32, (SGU_CHUNK, 128), 1) < HEAD_DIM
    parts = []
    for j in range(4):
        vp = vnb[:, 128 * j:128 * (j + 1)]
        parts.append(jnp.where(low, _nn(ws_ref[2 * j], vp), _nn(ws_ref[2 * j + 1], vp)))
    mixed = jnp.concatenate(parts, axis=1) + bias
    return ug, tu, tv, rs, vn, vnb, mixed, low


SGU_ROWS = 8 * SGU_CHUNK


def _sgu_fwd(z, ws, bias, name):
    T = z.shape[0]

    def body(zu_ref, zv_ref, ws_ref, b_ref, p_ref, y_ref):
        for r in range(0, SGU_ROWS, SGU_CHUNK):
            rows = slice(r, r + SGU_CHUNK)
            ug, _, _, _, _, _, mixed, _ = _sgu_core(zu_ref[rows, :], zv_ref[rows, :], ws_ref, b_ref[...], p_ref[...])
            y_ref[rows, :] = ug * mixed

    full = lambda shape: pl.BlockSpec(shape, lambda i: (0,) * len(shape))
    return pl.pallas_call(
        body, name=name, grid=(T // SGU_ROWS,),
        in_specs=[pl.BlockSpec((SGU_ROWS, 512), lambda i: (i, COL_AU)), pl.BlockSpec((SGU_ROWS, 512), lambda i: (i, COL_AV)),
                  full((8, 128, 128)), full((128, 512)), full((512, 512))],
        out_specs=pl.BlockSpec((SGU_ROWS, 512), lambda i: (i, 0)), out_shape=jax.ShapeDtypeStruct((T, 512), F32),
        compiler_params=_params(("parallel",)),
    )(z, z, ws, bias, _seg_matrix(512))


def _sgu_bwd(z, dy, ws, ws_t, bias, name):
    T = z.shape[0]

    def body(zu_ref, zv_ref, dy_ref, ws_ref, wst_ref, b_ref, p_ref, dz_ref, dws_ref, db_ref):
        @pl.when(pl.program_id(0) == 0)
        def _():
            dws_ref[...] = jnp.zeros_like(dws_ref)
            db_ref[...] = jnp.zeros_like(db_ref)

        p = p_ref[...]
        zero = jnp.zeros((SGU_CHUNK, 128), BF16)
        dws = [None] * 8
        db = None
        for r in range(0, SGU_ROWS, SGU_CHUNK):
            rows = slice(r, r + SGU_CHUNK)
            zu, zv = zu_ref[rows, :], zv_ref[rows, :]
            ug, tu, tv, rs, vn, vnb, mixed, low = _sgu_core(zu, zv, ws_ref, b_ref[...], p)
            dyv = dy_ref[rows, :]
            dmixed = dyv * ug
            db = dmixed if db is None else db + dmixed
            dmb = dmixed.astype(BF16)
            parts = []
            for j in range(4):
                dmp, vp = dmb[:, 128 * j:128 * (j + 1)], vnb[:, 128 * j:128 * (j + 1)]
                for g, d in ((2 * j, _nt(jnp.where(low, dmp, zero), vp)), (2 * j + 1, _nt(jnp.where(low, zero, dmp), vp))):
                    dws[g] = d if dws[g] is None else dws[g] + d
                parts.append(jnp.where(low, _nn(wst_ref[2 * j], dmp), _nn(wst_ref[2 * j + 1], dmp)))
            dvn = jnp.concatenate(parts, axis=1)
            dvg = rs * (dvn - _segmean(dvn, p) - vn * _segmean(dvn * vn, p))
            dz_ref[rows, 0:512] = (dyv * mixed * _gelu_grad(zu, tu)).astype(BF16)
            dz_ref[rows, 512:1024] = (dvg * _gelu_grad(zv, tv)).astype(BF16)
        db_ref[...] += db
        for g in range(8):
            dws_ref[g] += dws[g]

    full = lambda shape: pl.BlockSpec(shape, lambda i: (0,) * len(shape))
    return pl.pallas_call(
        body, name=name, grid=(T // SGU_ROWS,),
        in_specs=[pl.BlockSpec((SGU_ROWS, 512), lambda i: (i, COL_AU)), pl.BlockSpec((SGU_ROWS, 512), lambda i: (i, COL_AV)),
                  pl.BlockSpec((SGU_ROWS, 512), lambda i: (i, 0)), full((8, 128, 128)), full((8, 128, 128)), full((128, 512)), full((512, 512))],
        out_specs=[pl.BlockSpec((SGU_ROWS, 1024), lambda i: (i, 0)), full((8, 128, 128)), full((128, 512))],
        out_shape=[jax.ShapeDtypeStruct((T, 1024), BF16), jax.ShapeDtypeStruct((8, 128, 128), F32), jax.ShapeDtypeStruct((128, 512), F32)],
        compiler_params=_params(("arbitrary",)),
    )(z, z, dy, ws, ws_t, bias, _seg_matrix(512))


CONV_ROWS = 256


def _conv_taps(pad_ref, w_ref, base, flip):
    acc = None
    for k in range(CONV_WIDTH):
        wk = w_ref[CONV_WIDTH - 1 - k if flip else k]
        t = wk * pad_ref[base + k + 1:base + k + 1 + CONV_ROWS, :]
        acc = t if acc is None else acc + t
    return acc


def _conv_fwd1(z, w, cb, B, S, name):
    T = B * S
    rows = min(CONV_ROWS, S)
    assert rows == CONV_ROWS

    def body(a_ref, g_ref, w_ref, cb_ref, c_ref, pad):
        pad[0:CONV_PAD, :] = jnp.zeros((CONV_PAD, 128), F32)
        pad[CONV_PAD + S:2 * CONV_PAD + S, :] = jnp.zeros((CONV_PAD, 128), F32)
        pad[CONV_PAD:CONV_PAD + S, :] = a_ref[...] * _sigmoid(g_ref[...])

        for base in range(0, S, CONV_ROWS):
            c_ref[base:base + CONV_ROWS, :] = _conv_taps(pad, w_ref, base, False) + cb_ref[...]

    return pl.pallas_call(
        body, name=name, grid=(4, B),
        in_specs=[pl.BlockSpec((S, 128), lambda j, b: (b, 4 * COL_CA + j)), pl.BlockSpec((S, 128), lambda j, b: (b, 4 * COL_CG + j)),
                  pl.BlockSpec((32, 1, 128), lambda j, b: (0, 0, j)), pl.BlockSpec((1, 128), lambda j, b: (0, j))],
        out_specs=pl.BlockSpec((S, 128), lambda j, b: (b, j)), out_shape=jax.ShapeDtypeStruct((T, 512), F32),
        scratch_shapes=[pltpu.VMEM((S + 2 * CONV_PAD, 128), F32)], compiler_params=_params(("parallel", "parallel")),
    )(z, z, w, cb)


def _ln_rows(c):
    mu = jnp.mean(c, axis=-1, keepdims=True)
    xc = c - mu
    rs = lax.rsqrt(jnp.mean(xc * xc, axis=-1, keepdims=True) + LN_EPS)
    return xc * rs, rs


def _conv_bwd2(z, dc, w, B, S, name):
    T = B * S

    def body(a_ref, g_ref, dc_ref, w_ref, da_ref, dg_ref, dw_ref, hpad, dpad):
        @pl.when(pl.program_id(1) == 0)
        def _():
            dw_ref[...] = jnp.zeros_like(dw_ref)

        zeros = jnp.zeros((CONV_PAD, 128), F32)
        for ref in (hpad, dpad):
            ref[0:CONV_PAD, :] = zeros
            ref[CONV_PAD + S:2 * CONV_PAD + S, :] = zeros
        hpad[CONV_PAD:CONV_PAD + S, :] = a_ref[...] * _sigmoid(g_ref[...])
        dpad[CONV_PAD:CONV_PAD + S, :] = dc_ref[...]
        dws = [None] * CONV_WIDTH
        for base in range(0, S, CONV_ROWS):
            rows = slice(base, base + CONV_ROWS)
            dh = _conv_taps(dpad, w_ref, base, True)
            sg = _sigmoid(g_ref[rows, :])
            da_ref[rows, :] = (dh * sg).astype(BF16)
            dg_ref[rows, :] = (dh * a_ref[rows, :] * sg * (1.0 - sg)).astype(BF16)
            dcv = dc_ref[rows, :]
            for k in range(CONV_WIDTH):
                prod = dcv * hpad[base + k + 1:base + k + 1 + CONV_ROWS, :]
                part = jnp.sum(prod.reshape(CONV_ROWS // 8, 8, 128), axis=0)
                dws[k] = part if dws[k] is None else dws[k] + part
        for k in range(CONV_WIDTH):
            dw_ref[k] += jnp.sum(dws[k], axis=0, keepdims=True)

    return pl.pallas_call(
        body, name=name, grid=(4, B),
        in_specs=[pl.BlockSpec((S, 128), lambda j, b: (b, 4 * COL_CA + j)), pl.BlockSpec((S, 128), lambda j, b: (b, 4 * COL_CG + j)),
                  pl.BlockSpec((S, 128), lambda j, b: (b, j)), pl.BlockSpec((32, 1, 128), lambda j, b: (0, 0, j))],
        out_specs=[pl.BlockSpec((S, 128), lambda j, b: (b, j)), pl.BlockSpec((S, 128), lambda j, b: (b, j)),
                   pl.BlockSpec((32, 1, 128), lambda j, b: (0, 0, j))],
        out_shape=[jax.ShapeDtypeStruct((T, 512), BF16), jax.ShapeDtypeStruct((T, 512), BF16), jax.ShapeDtypeStruct((32, 1, 512), F32)],
        scratch_shapes=[pltpu.VMEM((S + 2 * CONV_PAD, 128), F32), pltpu.VMEM((S + 2 * CONV_PAD, 128), F32)],
        compiler_params=_params(("parallel", "arbitrary")),
    )(z, z, dc, w)


def _swap16(x):
    n = x.shape[1]
    first = (lax.broadcasted_iota(jnp.int32, x.shape, 1) % 32) < 16
    return jnp.where(first, pltpu.roll(x, n - 16, 1), pltpu.roll(x, 16, 1))


def _rope(x, cos, sin):
    return x * cos + _swap16(x) * sin


def _rope_t(dy, cos, sin):
    return dy * cos + _swap16(dy * sin)


def _qk_norm(x, p):
    r = lax.rsqrt(_segmean(x * x, p) + RMS_EPS)
    return x * r, r


def _store_heads(ref, val, n):
    for h in range(n):
        ref[h] = val[:, HEAD_DIM * h:HEAD_DIM * (h + 1)].astype(ref.dtype)


def _load_heads(ref, n):
    return jnp.concatenate([ref[h] for h in range(n)], axis=1)


def _prep_fwd(z, gq, gk, rope, B, S, kv_heads, cols, name):
    tm = min(2 * ROW_TILE, S)
    ns = S // tm
    kw = kv_heads * HEAD_DIM
    scale = HEAD_DIM ** -0.5
    qc, kc, vc = cols

    def body(*refs):
        if rope is None:
            q_ref, k_ref, v_ref, gq_ref, gk_ref, p_ref, qo, ko, vo = refs
        else:
            q_ref, k_ref, v_ref, gq_ref, gk_ref, p_ref, cos_ref, sin_ref, qo, ko, vo = refs
        p = p_ref[...]
        qn, _ = _qk_norm(q_ref[...], p)
        kn, _ = _qk_norm(k_ref[...], p[:kw, :kw])
        qn, kn = qn * gq_ref[...], kn * gk_ref[...]
        if rope is not None:
            cos, sin = cos_ref[...], sin_ref[...]
            qn, kn = _rope(qn, cos, sin), _rope(kn, cos[:, :kw], sin[:, :kw])
        _store_heads(qo, qn * scale, N_HEADS)
        _store_heads(ko, kn, kv_heads)
        _store_heads(vo, v_ref[...], kv_heads)

    row = lambda w, c: pl.BlockSpec((tm, w), lambda b, i: (b * ns + i, c))
    const = lambda shape: pl.BlockSpec(shape, lambda b, i: (0,) * len(shape))
    heads = lambda n: pl.BlockSpec((None, n, tm, HEAD_DIM), lambda b, i: (b, 0, i, 0))
    ins = [z, z, z, gq, gk, _seg_matrix(512)]
    specs = [row(512, qc), row(kw, kc), row(kw, vc), const((1, 512)), const((1, kw)), const((512, 512))]
    if rope is not None:
        ins += list(rope)
        specs += [pl.BlockSpec((tm, 512), lambda b, i: (i, 0))] * 2
    return pl.pallas_call(
        body, name=name, grid=(B, ns), in_specs=specs, out_specs=[heads(N_HEADS), heads(kv_heads), heads(kv_heads)],
        out_shape=[jax.ShapeDtypeStruct((B, N_HEADS, S, HEAD_DIM), BF16), jax.ShapeDtypeStruct((B, kv_heads, S, HEAD_DIM), BF16),
                   jax.ShapeDtypeStruct((B, kv_heads, S, HEAD_DIM), BF16)],
        compiler_params=_params(("parallel", "parallel")),
    )(*ins)


def _prep_bwd(z, dq, dk, dv, gq, gk, rope, B, S, kv_heads, cols, name):
    T = B * S
    tm = min(2 * ROW_TILE, S)
    ns = S // tm
    kw = kv_heads * HEAD_DIM
    scale = HEAD_DIM ** -0.5
    qc, kc, _ = cols

    def body(*refs):
        if rope is None:
            q_ref, k_ref, dq_ref, dk_ref, dv_ref, gq_ref, gk_ref, p_ref, dz_ref, dgq_ref, dgk_ref = refs
        else:
            q_ref, k_ref, dq_ref, dk_ref, dv_ref, gq_ref, gk_ref, p_ref, cos_ref, sin_ref, dz_ref, dgq_ref, dgk_ref = refs

        @pl.when((pl.program_id(0) == 0) & (pl.program_id(1) == 0))
        def _():
            dgq_ref[...] = jnp.zeros_like(dgq_ref)
            dgk_ref[...] = jnp.zeros_like(dgk_ref)

        p = p_ref[...]
        dqv = _load_heads(dq_ref, N_HEADS) * scale
        dkv = _load_heads(dk_ref, kv_heads)
        if rope is not None:
            cos, sin = cos_ref[...], sin_ref[...]
            dqv, dkv = _rope_t(dqv, cos, sin), _rope_t(dkv, cos[:, :kw], sin[:, :kw])

        def through_norm(xv, dy, g, pm, dg_ref):
            xh, r = _qk_norm(xv, pm)
            dg_ref[...] += jnp.sum(dy * xh, axis=0, keepdims=True)
            dxh = dy * g
            return r * (dxh - xh * _segmean(dxh * xh, pm))

        dz_ref[:, 0:512] = through_norm(q_ref[...], dqv, gq_ref[...], p, dgq_ref).astype(BF16)
        dz_ref[:, 512:512 + kw] = through_norm(k_ref[...], dkv, gk_ref[...], p[:kw, :kw], dgk_ref).astype(BF16)
        dz_ref[:, 512 + kw:512 + 2 * kw] = _load_heads(dv_ref, kv_heads).astype(BF16)

    row = lambda w, c: pl.BlockSpec((tm, w), lambda b, i: (b * ns + i, c))
    const = lambda shape: pl.BlockSpec(shape, lambda b, i: (0,) * len(shape))
    heads = lambda n: pl.BlockSpec((None, n, tm, HEAD_DIM), lambda b, i: (b, 0, i, 0))
    ins = [z, z, dq, dk, dv, gq, gk, _seg_matrix(512)]
    specs = [row(512, qc), row(kw, kc), heads(N_HEADS), heads(kv_heads), heads(kv_heads), const((1, 512)), const((1, kw)), const((512, 512))]
    if rope is not None:
        ins += list(rope)
        specs += [pl.BlockSpec((tm, 512), lambda b, i: (i, 0))] * 2
    return pl.pallas_call(
        body, name=name, grid=(B, ns), in_specs=specs, out_specs=[row(512 + 2 * kw, 0), const((1, 512)), const((1, kw))],
        out_shape=[jax.ShapeDtypeStruct((T, 512 + 2 * kw), BF16), jax.ShapeDtypeStruct((1, 512), F32), jax.ShapeDtypeStruct((1, kw), F32)],
        compiler_params=_params(("arbitrary", "arbitrary")),
    )(*ins)


def _toeplitz(win, tq, S):
    r = pltpu.roll(jnp.broadcast_to(win, (tq, S + tq)), 0, 1, stride=1, stride_axis=0)
    return r[:, tq:tq + S]


ATTN_HEADS = 4


def _attn_fwd(q, k, v, win, name, nh=ATTN_HEADS, tq=ATTN_TQ):
    B, H, S, _ = q.shape
    shared = k.shape[1] != H
    assert not shared or H // k.shape[1] == nh
    tq = min(tq, S)

    def body(*refs):
        if win is None:
            q_ref, k_ref, v_ref, o_ref = refs
        else:
            q_ref, k_ref, v_ref, w_ref, o_ref = refs
        kvs = [(k_ref[...], v_ref[...])] * nh if shared else [(k_ref[h], v_ref[h]) for h in range(nh)]
        scores = []
        for h in range(nh):
            s = _nt(q_ref[h], kvs[h][0])
            if win is not None:
                s = s + _toeplitz(w_ref[h], tq, S)
            scores.append(s)
        probs = []
        for s in scores:
            p = jnp.exp(s - jnp.max(s, axis=-1, keepdims=True))
            probs.append((p.astype(BF16), jnp.sum(p, axis=-1, keepdims=True)))
        for h, (p, l) in enumerate(probs):
            o_ref[h] = _nn(p, kvs[h][1]) / l

    qs = pl.BlockSpec((None, nh, tq, HEAD_DIM), lambda b, h, i: (b, h, i, 0))
    ks = (pl.BlockSpec((None, None, S, HEAD_DIM), lambda b, h, i: (b, h, 0, 0)) if shared
          else pl.BlockSpec((None, nh, S, HEAD_DIM), lambda b, h, i: (b, h, 0, 0)))
    ins, specs = [q, k, v], [qs, ks, ks]
    if win is not None:
        ins.append(win)
        specs.append(pl.BlockSpec((nh, None, 1, S + tq), lambda b, h, i: (h, i, 0, 0)))
    return pl.pallas_call(body, name=name, grid=(B, H // nh, S // tq), in_specs=specs, out_specs=qs,
                          out_shape=jax.ShapeDtypeStruct((B, H, S, HEAD_DIM), F32),
                          compiler_params=_params(("parallel", "parallel", "parallel")))(*ins)


def _attn_bwd(q, k, v, o, do, win, name, nh=ATTN_HEADS, tq=ATTN_TQ):
    B, H, S, _ = q.shape
    hkv = k.shape[1]
    shared = hkv != H
    assert not shared or H // hkv == nh
    tq = min(tq, S)
    nq = S // tq

    def body(*refs):
        if win is None:
            q_ref, k_ref, v_ref, o_ref, do_ref, dq_ref, dk_ref, dv_ref = refs
        else:
            q_ref, k_ref, v_ref, o_ref, do_ref, w_ref, rev_ref, dq_ref, dk_ref, dv_ref, dw_ref = refs

        @pl.when(pl.program_id(2) == 0)
        def _():
            dk_ref[...] = jnp.zeros_like(dk_ref)
            dv_ref[...] = jnp.zeros_like(dv_ref)

        kvs = [(k_ref[...], v_ref[...])] * nh if shared else [(k_ref[h], v_ref[h]) for h in range(nh)]
        qvs, dobs, scores, dps = [], [], [], []
        for h in range(nh):
            qv, dov = q_ref[h], do_ref[h]
            dob = dov.astype(BF16)
            s = _nt(qv, kvs[h][0])
            if win is not None:
                s = s + _toeplitz(w_ref[h], tq, S)
            dp = _nt(dob, kvs[h][1]) - jnp.sum(dov * o_ref[h], axis=-1, keepdims=True)
            qvs.append(qv)
            dobs.append(dob)
            scores.append(s)
            dps.append(dp)
        pbs, dsbs = [], []
        for s, dp in zip(scores, dps):
            p = jnp.exp(s - jnp.max(s, axis=-1, keepdims=True))
            p = p * (1.0 / jnp.sum(p, axis=-1, keepdims=True))
            pbs.append(p.astype(BF16))
            dsbs.append((p * dp).astype(BF16))
        dk_acc = dv_acc = None
        for h in range(nh):
            dvh, dkh = _tn(pbs[h], dobs[h]), _tn(dsbs[h], qvs[h])
            dq_ref[h] = _nn(dsbs[h], kvs[h][0])
            if shared:
                dv_acc = dvh if dv_acc is None else dv_acc + dvh
                dk_acc = dkh if dk_acc is None else dk_acc + dkh
            else:
                dv_ref[h] += dvh
                dk_ref[h] += dkh
            if win is not None:
                rev = _nn(rev_ref[...], dsbs[h])
                half = tq // 2
                top = jnp.concatenate([rev[:half], jnp.zeros((half, tq), F32)], axis=1)
                bot = jnp.concatenate([jnp.zeros((half, half), F32), rev[half:], jnp.zeros((half, half), F32)], axis=1)
                dw_ref[h] = jnp.sum(pltpu.roll(top + bot, 0, 1, stride=1, stride_axis=0), axis=0, keepdims=True)
        if shared:
            dv_ref[...] += dv_acc
            dk_ref[...] += dk_acc

    qs = pl.BlockSpec((None, nh, tq, HEAD_DIM), lambda b, h, i: (b, h, i, 0))
    ks = (pl.BlockSpec((None, None, S, HEAD_DIM), lambda b, h, i: (b, h, 0, 0)) if shared
          else pl.BlockSpec((None, nh, S, HEAD_DIM), lambda b, h, i: (b, h, 0, 0)))
    ins, specs = [q, k, v, o, do], [qs, ks, ks, qs, qs]
    outs = [jax.ShapeDtypeStruct((B, H, S, HEAD_DIM), F32), jax.ShapeDtypeStruct((B, hkv, S, HEAD_DIM), F32), jax.ShapeDtypeStruct((B, hkv, S, HEAD_DIM), F32)]
    ospecs = [qs, ks, ks]
    if win is not None:
        ins += [win, jnp.asarray(np.eye(tq, dtype=np.float32)[::-1].copy(), BF16)]
        specs += [pl.BlockSpec((nh, None, 1, S + tq), lambda b, h, i: (h, i, 0, 0)), pl.BlockSpec((tq, tq), lambda b, h, i: (0, 0))]
        outs.append(jax.ShapeDtypeStruct((B, H, nq, 1, S + tq), F32))
        ospecs.append(pl.BlockSpec((None, nh, None, 1, S + tq), lambda b, h, i: (b, h, i, 0, 0)))
    return pl.pallas_call(body, name=name, grid=(B, H // nh, nq), in_specs=specs, out_specs=ospecs, out_shape=outs,
                          compiler_params=_params(("parallel", "parallel", "arbitrary")))(*ins)


def _pattern_count(delta):
    n = jnp.zeros(delta.shape, jnp.int32)
    for window, dil in DIL_PATTERNS:
        n = n + ((delta % dil == 0) & (jnp.abs(delta) <= window // 2)).astype(jnp.int32)
    return n


def _t5_bucket(rel):
    nb = REL_BUCKETS // 2
    max_exact = nb // 2
    ret = jnp.where(rel > 0, nb, 0)
    n = jnp.abs(rel)
    nf = jnp.maximum(n, 1).astype(F32)
    large = max_exact + (jnp.log(nf / max_exact) / math.log(REL_MAX_DIST / max_exact) * (nb - max_exact)).astype(jnp.int32)
    large = jnp.minimum(large, nb - 1)
    return ret + jnp.where(n < max_exact, n, large)


def _bias_windows(rel_bias, S):
    tq = min(ATTN_TQ, S)
    nq = S // tq
    n = nq * (S + tq)
    delta = (jnp.arange(S + tq)[None, :] - (jnp.arange(nq)[:, None] + 1) * tq).reshape(n)
    count = _pattern_count(delta)
    onehot = (_t5_bucket(delta)[None, :] == jnp.arange(REL_BUCKETS)[:, None]).astype(F32)
    extra = jnp.where(count > 0, jnp.log(jnp.maximum(count, 1).astype(F32)), MASKED).reshape(1, n)
    live = (count > 0).astype(F32).reshape(1, n)

    def body(t_ref, oh_ref, live_ref, extra_ref, o_ref):
        o_ref[...] = _nn(t_ref[...], oh_ref[...], HIGHEST) * live_ref[...] + extra_ref[...]

    val = pl.pallas_call(body, name="bias_windows", out_shape=jax.ShapeDtypeStruct((N_HEADS, n), F32),
                         compiler_params=_params())(rel_bias.T, onehot, live, extra)
    return val.reshape(N_HEADS, nq, 1, S + tq)


def _bias_fold(dwin, S, name):
    B, H, nq = dwin.shape[:3]
    tq = min(ATTN_TQ, S)
    n = nq * (S + tq)
    delta = (jnp.arange(S + tq)[None, :] - (tq - 1) - jnp.arange(nq)[:, None] * tq).reshape(n)
    onehot = (_t5_bucket(delta)[:, None] == jnp.arange(128)[None, :]).astype(F32)

    def body(d_ref, oh_ref, o_ref):
        tot = d_ref[0]
        for b in range(1, B):
            tot = tot + d_ref[b]
        o_ref[...] = _nn(tot, oh_ref[...], HIGHEST)

    out = pl.pallas_call(body, name=name, out_shape=jax.ShapeDtypeStruct((H, 128), F32), compiler_params=_params())(dwin.reshape(B, H, n), onehot)
    return out[:, :REL_BUCKETS].T


def _rope_tables(S):
    half = 16
    freqs = ROPE_THETA ** (-jnp.arange(half, dtype=F32) / half)
    t = jnp.arange(S)
    ang_r = (t // GRID_W).astype(F32)[:, None] * freqs[None, :]
    ang_c = (t % GRID_W).astype(F32)[:, None] * freqs[None, :]
    cos = jnp.concatenate([jnp.cos(ang_r)] * 2 + [jnp.cos(ang_c)] * 2, axis=1)
    sin = jnp.concatenate([-jnp.sin(ang_r), jnp.sin(ang_r), -jnp.sin(ang_c), jnp.sin(ang_c)], axis=1)
    return jnp.tile(cos, (1, N_HEADS)), jnp.tile(sin, (1, N_HEADS))


def _conv_act(c, g, b):
    n, rs = _ln_rows(c)
    t = n * g + b
    return t * _sigmoid(t), n, rs, t


def _mix_fwd(ya, ob, c, od, gain, lng, lnb, B, S, name):
    T = B * S
    tm = min(ROW_TILE, S)
    ns = S // tm

    def body(ya_ref, ob_ref, c_ref, od_ref, g_ref, lg_ref, lb_ref, o_ref):
        yc = _conv_act(c_ref[...], lg_ref[...], lb_ref[...])[0]
        ys = [ya_ref[...], _load_heads(ob_ref, N_HEADS), yc, _load_heads(od_ref, N_HEADS)]
        for m, y in enumerate(ys):
            r = lax.rsqrt(jnp.mean(y * y, axis=-1, keepdims=True) + RMS_EPS)
            o_ref[:, 512 * m:512 * (m + 1)] = (y * r * g_ref[:, 512 * m:512 * (m + 1)]).astype(BF16)

    row = pl.BlockSpec((tm, 512), lambda b, i: (b * ns + i, 0))
    heads = pl.BlockSpec((None, N_HEADS, tm, HEAD_DIM), lambda b, i: (b, 0, i, 0))
    vec = pl.BlockSpec((1, 512), lambda b, i: (0, 0))
    return pl.pallas_call(
        body, name=name, grid=(B, ns), in_specs=[row, heads, row, heads, pl.BlockSpec((1, 2048), lambda b, i: (0, 0)), vec, vec],
        out_specs=pl.BlockSpec((tm, 2048), lambda b, i: (b * ns + i, 0)), out_shape=jax.ShapeDtypeStruct((T, 2048), BF16),
        compiler_params=_params(("parallel", "parallel")),
    )(ya, ob, c, od, gain, lng, lnb)


def _mix_bwd(ya, ob, c, od, dycat, gain, lng, lnb, B, S, name):
    T = B * S
    tm = min(ROW_TILE, S)
    ns = S // tm

    def body(ya_ref, ob_ref, c_ref, od_ref, dy_ref, g_ref, lg_ref, lb_ref, dya_ref, dob_ref, dc_ref, dod_ref, dg_ref, dlg_ref, dlb_ref, dcb_ref):
        @pl.when((pl.program_id(0) == 0) & (pl.program_id(1) == 0))
        def _():
            for ref in (dg_ref, dlg_ref, dlb_ref, dcb_ref):
                ref[...] = jnp.zeros_like(ref)

        yc, n, rs, t = _conv_act(c_ref[...], lg_ref[...], lb_ref[...])
        ys = [ya_ref[...], _load_heads(ob_ref, N_HEADS), yc, _load_heads(od_ref, N_HEADS)]
        outs = [dya_ref, dob_ref, None, dod_ref]
        for m, y in enumerate(ys):
            cols = slice(512 * m, 512 * (m + 1))
            r = lax.rsqrt(jnp.mean(y * y, axis=-1, keepdims=True) + RMS_EPS)
            yh = y * r
            dh = dy_ref[:, cols]
            dg_ref[:, cols] += jnp.sum(dh * yh, axis=0, keepdims=True)
            dyh = dh * g_ref[:, cols]
            dyv = r * (dyh - yh * jnp.mean(dyh * yh, axis=-1, keepdims=True))
            if m == 0:
                outs[m][...] = dyv
            elif m == 2:
                sg = _sigmoid(t)
                dt = dyv * sg * (1.0 + t * (1.0 - sg))
                dlg_ref[...] += jnp.sum(dt * n, axis=0, keepdims=True)
                dlb_ref[...] += jnp.sum(dt, axis=0, keepdims=True)
                dn = dt * lg_ref[...]
                dc = rs * (dn - jnp.mean(dn, axis=-1, keepdims=True) - n * jnp.mean(dn * n, axis=-1, keepdims=True))
                dc_ref[...] = dc
                dcb_ref[...] += jnp.sum(dc, axis=0, keepdims=True)
            else:
                _store_heads(outs[m], dyv, N_HEADS)

    row = pl.BlockSpec((tm, 512), lambda b, i: (b * ns + i, 0))
    heads = pl.BlockSpec((None, N_HEADS, tm, HEAD_DIM), lambda b, i: (b, 0, i, 0))
    vec = pl.BlockSpec((1, 2048), lambda b, i: (0, 0))
    flat = jax.ShapeDtypeStruct((T, 512), F32)
    hm = jax.ShapeDtypeStruct((B, N_HEADS, S, HEAD_DIM), F32)
    v512 = pl.BlockSpec((1, 512), lambda b, i: (0, 0))
    s512 = jax.ShapeDtypeStruct((1, 512), F32)
    return pl.pallas_call(
        body, name=name, grid=(B, ns), in_specs=[row, heads, row, heads, pl.BlockSpec((tm, 2048), lambda b, i: (b * ns + i, 0)), vec, v512, v512],
        out_specs=[row, heads, row, heads, vec, v512, v512, v512],
        out_shape=[flat, hm, flat, hm, jax.ShapeDtypeStruct((1, 2048), F32), s512, s512, s512],
        compiler_params=_params(("arbitrary", "arbitrary")),
    )(ya, ob, c, od, dycat, gain, lng, lnb)


def _ffn_down(gate, up, w_down, res, name, norm=None, target=None):
    J, T, n = gate.shape
    N = w_down.shape[2]
    tm = min(256, T)
    steps = T // tm

    def body(g_ref, u_ref, w_ref, r_ref, *rest):
        x_ref = rest[0] if (norm is not None or target is not None) else None
        o_ref, act_ref = rest[1:3] if x_ref is not None else rest[-2:]
        acc = None
        for j in range(J):
            g = g_ref[j].astype(F32)
            a = (g * _sigmoid(g) * u_ref[j].astype(F32)).astype(BF16)
            act_ref[j] = a
            d = _nn(a, w_ref[j])
            acc = d if acc is None else acc + d
        y = acc + r_ref[...]
        if target is None:
            o_ref[...] = y
            if norm is not None:
                rest[-1][...] = _rms_rows(y, x_ref[...])
            return
        err = y - x_ref[...]
        o_ref[...] = err * (1.0 / N)
        rest[-1][...] = (err * (1.0 / N)).astype(BF16)
        loss_ref, i = rest[-2], pl.program_id(0)

        @pl.when(i == 0)
        def _():
            loss_ref[...] = jnp.zeros_like(loss_ref)

        loss_ref[...] += jnp.sum(err * err)

        @pl.when(i == steps - 1)
        def _():
            loss_ref[...] = loss_ref[...] * (0.5 / N)

    gu = pl.BlockSpec((J, tm, n), lambda i: (0, i, 0))
    row = pl.BlockSpec((tm, N), lambda i: (i, 0))
    ins, specs = [gate, up, w_down, res], [gu, gu, pl.BlockSpec((J, n, N), lambda i: (0, 0, 0)), row]
    outs, ospecs = [jax.ShapeDtypeStruct((T, N), F32), jax.ShapeDtypeStruct((J, T, n), BF16)], [row, gu]
    if target is not None:
        ins, specs = ins + [target], specs + [row]
        outs = outs + [jax.ShapeDtypeStruct((8, 128), F32), jax.ShapeDtypeStruct((T, N), BF16)]
        ospecs = ospecs + [pl.BlockSpec((8, 128), lambda i: (0, 0)), row]
    elif norm is not None:
        ins, specs = ins + [norm.reshape(1, N)], specs + [pl.BlockSpec((1, N), lambda i: (0, 0))]
        outs, ospecs = outs + [jax.ShapeDtypeStruct((T, N), BF16)], ospecs + [row]
    return pl.pallas_call(body, name=name, grid=(steps,), in_specs=specs, out_specs=ospecs, out_shape=outs,
                          compiler_params=_params(("arbitrary" if target is not None else "parallel",)))(*ins)


def _ffn_down_dx(dx, w_down, gate, up, name):
    J, n, D = w_down.shape
    T = dx.shape[0]
    tm = min(512, T)

    def body(dx_ref, w_ref, g_ref, u_ref, dg_ref, du_ref):
        d = _nt(dx_ref[...].astype(BF16), w_ref[...])
        g = g_ref[...].astype(F32)
        s = _sigmoid(g)
        dg_ref[...] = (d * u_ref[...].astype(F32) * s * (1.0 + g * (1.0 - s))).astype(BF16)
        du_ref[...] = (d * g * s).astype(BF16)

    blk = pl.BlockSpec((None, tm, n), lambda j, i: (j, i, 0))
    shape = jax.ShapeDtypeStruct((J, T, n), BF16)
    return pl.pallas_call(body, name=name, grid=(J, T // tm),
                          in_specs=[pl.BlockSpec((tm, D), lambda j, i: (i, 0)), pl.BlockSpec((None, n, D), lambda j, i: (j, 0, 0)), blk, blk],
                          out_specs=[blk, blk], out_shape=[shape, shape], compiler_params=_params(("parallel", "parallel")))(dx, w_down, gate, up)


def _row_tile(R):
    best = R
    for cand in range(16, min(R, 272) + 1, 16):
        if R % cand == 0:
            best = cand
    return best


def _adamw(w, m, v, stack, name, layer=None, prev=None):
    n, R, C = stack.shape
    tm = _row_tile(R)
    nb = R // tm
    off = 0 if layer is None else layer * nb
    c1 = 1.0 - ADAM_B1 ** ADAM_STEP
    c2 = 1.0 - ADAM_B2 ** ADAM_STEP

    def body(w_ref, m_ref, v_ref, s_ref, *rest):
        g_ref, d_ref, mo_ref, vo_ref = rest[-4:]
        g = s_ref[0].astype(F32)
        for k in range(1, n):
            g = g + s_ref[k].astype(F32)
        mn = ADAM_B1 * m_ref[...] + (1.0 - ADAM_B1) * g
        vn = ADAM_B2 * v_ref[...] + (1.0 - ADAM_B2) * (g * g)
        g_ref[...] = g
        mo_ref[...] = mn
        vo_ref[...] = vn
        d_ref[...] = -ADAM_LR * ((mn / c1) / (jnp.sqrt(vn / c2) + ADAM_EPS) + ADAM_WD * w_ref[...])

    blk = pl.BlockSpec((tm, C), lambda i: (i + off, 0))
    ins = [w, m, v, stack]
    specs = [blk, blk, blk, pl.BlockSpec((n, tm, C), lambda i: (0, i, 0))]
    aliases = {}
    if prev is not None:
        ins += list(prev)
        specs += [pl.BlockSpec(memory_space=pl.ANY)] * 4
        aliases = {4 + t: t for t in range(4)}
    shape = jax.ShapeDtypeStruct(w.shape, F32)
    return pl.pallas_call(body, name=name, grid=(nb,), in_specs=specs, out_specs=[blk] * 4, out_shape=[shape] * 4,
                          input_output_aliases=aliases, compiler_params=_params(("parallel",)))(*ins)


HBM = pl.BlockSpec(memory_space=pltpu.HBM)
SEM = pl.BlockSpec(memory_space=pltpu.SEMAPHORE)
EFFECT = pltpu.SideEffectType.DATAFLOW_SIDE_EFFECTING


PEERS = {"scatter": (1, 2, 3, 4, 5, 6, 7), "gather": (1, 2, 3, 4, 5, 6, 7), "chips": (1, 2, 4, 6), "forward": (2, 4, 6)}


def _spread_copies(srcs, lands, send_sems, recv_sems, local_sems, kind, waiting):
    x, y, c = lax.axis_index("x"), lax.axis_index("y"), lax.axis_index("c")
    me = 4 * x + 2 * y + c

    def peer(bits):
        dev = (1 - x if bits & 4 else x, 1 - y if bits & 2 else y, 1 - c if bits & 1 else c)
        return dev, 4 * dev[0] + 2 * dev[1] + dev[2]

    plan = PEERS[kind]
    remote, local = [], []
    for a, l in enumerate(lands):
        for d, bits in enumerate(plan):
            dev, pid = peer(bits)
            if kind == "forward":
                src, dst, dev = l.at[pid], l.at[peer(bits | 1)[1] if waiting else pid], peer(1)[0]
            else:
                src, dst = (srcs[a].at[pid] if kind == "scatter" else srcs[a]), l.at[pid if waiting else me]
            remote.append(pltpu.make_async_remote_copy(
                src_ref=src, dst_ref=dst, send_sem=send_sems.at[a * len(plan) + d], recv_sem=recv_sems.at[a * len(plan) + d],
                device_id=dev, device_id_type=MESH_ID))
        if kind != "forward":
            local.append(pltpu.make_async_copy(srcs[a].at[me] if kind == "scatter" else srcs[a], l.at[me], local_sems.at[a]))
    return remote, local


def _spread_start(srcs, kind, name, after=None, lands=None):
    if kind == "forward":
        srcs = []
    else:
        shapes = [a.shape if kind == "scatter" else (N_DEV,) + a.shape for a in srcs]
        lands = [lax.empty(shp, a.dtype) for shp, a in zip(shapes, srcs)]
    ns, nl, per = len(srcs), len(lands), len(PEERS[kind])
    extra = [] if after is None else [after]
    sem_shapes = [pltpu.SemaphoreType.DMA((nl * per,))] * 2 + ([pltpu.SemaphoreType.DMA((nl,))] if ns else [])

    def body(*refs):
        src_refs, land_refs = refs[:ns], refs[ns:ns + nl]
        sems = refs[ns + nl + len(extra):ns + nl + len(extra) + len(sem_shapes)]
        remote, local = _spread_copies(src_refs, land_refs, sems[0], sems[1], sems[2] if ns else None, kind, False)
        for cp in remote + local:
            cp.start()
        refs[-1][...] = jnp.zeros((8, 128), F32)

    outs = pl.pallas_call(
        body, name=name,
        out_shape=(*sem_shapes, *[pltpu.HBM(a.shape, a.dtype) for a in srcs + lands], jax.ShapeDtypeStruct((8, 128), F32)),
        in_specs=[HBM] * (ns + nl) + [pl.BlockSpec(memory_space=pl.ANY)] * len(extra),
        out_specs=(*[SEM] * len(sem_shapes), *[HBM] * (ns + nl), pl.BlockSpec(memory_space=pltpu.VMEM)),
        input_output_aliases={i: len(sem_shapes) + i for i in range(ns + nl)},
        compiler_params=pltpu.CompilerParams(has_side_effects=EFFECT),
    )(*[pltpu.with_memory_space_constraint(a, pltpu.HBM) for a in srcs + lands], *extra)
    k = len(sem_shapes)
    return outs[:k], list(outs[k:k + ns]), list(outs[k + ns:k + ns + nl]), outs[-1]


def _spread_wait(sems, srcs, lands, after, kind, name):
    ns, nl = len(srcs), len(lands)
    after = list(after) if isinstance(after, (list, tuple)) else [after]

    def body(*refs):
        src_refs, land_refs = refs[:ns], refs[ns:ns + nl]
        s = refs[ns + nl:ns + nl + len(sems)]
        remote, local = _spread_copies(src_refs, land_refs, s[0], s[1], s[2] if ns else None, kind, True)
        for cp in remote:
            cp.wait_send()
            cp.wait_recv()
        for cp in local:
            cp.wait()

    outs = pl.pallas_call(
        body, name=name, out_shape=tuple(pltpu.HBM(a.shape, a.dtype) for a in srcs + lands),
        in_specs=[HBM] * (ns + nl) + [SEM] * len(sems) + [pl.BlockSpec(memory_space=pl.ANY)] * len(after), out_specs=tuple([HBM] * (ns + nl)),
        input_output_aliases={i: i for i in range(ns + nl)}, compiler_params=pltpu.CompilerParams(has_side_effects=EFFECT),
    )(*srcs, *lands, *sems, *after)
    return list(outs[ns:])


SMALL = ("rel_bias", "norm1_g", "sgu_w", "sgu_b", "dil_qn_g", "dil_kn_g", "conv_w", "conv_b", "conv_ln_g", "conv_ln_b",
         "gqa_qn_g", "gqa_kn_g", "mix_norm_g", "norm2_g")
LARGE = ("w_in", "w_out", "w_gate", "w_up", "w_down")
EARLY = tuple(k for k in SMALL if k != "norm1_g")


def _local_step(x, target, p, B, S, fetch, emit, mid, early):
    T = B * S
    rope = _rope_tables(S)
    win = _bias_windows(p["rel_bias"], S)
    tile8 = lambda g: jnp.tile(g.reshape(1, HEAD_DIM), (1, N_HEADS))
    cols_b = (COL_BQ, COL_BK, COL_BV)
    cols_d = (COL_DQ, COL_DK128, COL_DV128)
    saved = []
    for l in range(DEPTH):
        s = {"x": x}
        s["ws"] = p["sgu_w"][l].astype(BF16)
        s["bias"] = jnp.repeat(p["sgu_b"][l].T, HEAD_DIM, axis=1)
        s["h"] = _rms_fwd(x, p["norm1_g"][l], f"rms1_fwd_{l}") if l == 0 else h_next
        s["win"] = fetch(l, "in", s["h"])
        s["cw"] = jnp.pad(s["win"]["conv_w"], ((0, 1), (0, 0))).reshape(32, 1, 512)
        z = s["z"] = _matmul(s["h"], s["win"]["w_in"], "nt", f"in_proj_{l}", tk=D_MODEL)
        s["bias"] = s["bias"] + mid(l, z)
        s["ya"] = _sgu_fwd(z, s["ws"], s["bias"], f"sgu_fwd_{l}")
        s["c"] = _conv_fwd1(z, s["cw"], p["conv_b"][l].reshape(1, 512), B, S, f"conv_fwd_{l}")
        s["ln"] = (p["conv_ln_g"][l].reshape(1, 512), p["conv_ln_b"][l].reshape(1, 512))
        s["gb"] = (tile8(p["dil_qn_g"][l]), tile8(p["dil_kn_g"][l]))
        s["gd"] = (tile8(p["gqa_qn_g"][l]), tile8(p["gqa_kn_g"][l])[:, :KV_WIDTH])
        s["qkv_b"] = _prep_fwd(z, *s["gb"], None, B, S, N_HEADS, cols_b, f"prep_b_fwd_{l}")
        s["qkv_d"] = _prep_fwd(z, *s["gd"], rope, B, S, KV_HEADS, cols_d, f"prep_d_fwd_{l}")
        s["ob"] = _attn_fwd(*s["qkv_b"], win, f"attn_b_fwd_{l}")
        s["od"] = _attn_fwd(*s["qkv_d"], None, f"attn_d_fwd_{l}", tq=GQA_TQ)
        s["gmix"] = p["mix_norm_g"][l].reshape(1, 2048)
        s["ycat"] = _mix_fwd(s["ya"], s["ob"], s["c"], s["od"], s["gmix"], *s["ln"], B, S, f"mix_fwd_{l}")
        s["wout"] = fetch(l, "out", s["ycat"])["w_out"]
        x1, s["h2"] = _matmul(s["ycat"], s["wout"], "nn", f"out_proj_{l}", res=x, tk=D_MODEL, norm=p["norm2_g"][l])
        s["x1"] = x1
        s["ffn"] = fetch(l, "ffn", s["h2"])
        s["gate"] = _mm_shard_out(s["h2"], s["ffn"]["w_gate"], "nt", f"ffn_gate_{l}", out_dtype=BF16, tm=1024)
        s["up"] = _mm_shard_out(s["h2"], s["ffn"]["w_up"], "nt", f"ffn_up_{l}", out_dtype=BF16, tm=1024)
        if l + 1 < DEPTH:
            x, s["act"], h_next = _ffn_down(s["gate"], s["up"], s["ffn"]["w_down"], x1, f"ffn_down_{l}", norm=p["norm1_g"][l + 1])
        else:
            dx, s["act"], loss_blk, dxb = _ffn_down(s["gate"], s["up"], s["ffn"]["w_down"], x1, f"ffn_down_{l}", target=target)
        saved.append(s)

    g = {k: [None] * DEPTH for k in SMALL if k != "rel_bias"}
    dwin_total = None
    for l in reversed(range(DEPTH)):
        s = saved[l]
        z, ffn = s["z"], s["ffn"]
        dgate, dup = _ffn_down_dx(dxb, ffn["w_down"], s["gate"], s["up"], f"ffn_down_dx_{l}")
        tok = emit(l, "w_down", _mm_shard_m(s["act"], dxb, f"ffn_down_dw_{l}", out_dtype=BF16, tm=FFN_BLOCK, tn=512, tk=T))
        tok += emit(l, "w_gate", _mm_shard_m(dgate, s["h2"], f"ffn_gate_dw_{l}", out_dtype=BF16, tm=FFN_BLOCK, tn=512, tk=T))
        tok += emit(l, "w_up", _mm_shard_m(dup, s["h2"], f"ffn_up_dw_{l}", out_dtype=BF16, tm=FFN_BLOCK, tn=512, tk=T))
        dh2 = _mm_shard_k([(dgate, ffn["w_gate"]), (dup, ffn["w_up"])], "nn", f"ffn_up_dx_{l}", tn=512, fold=FFN_GROUPS)
        dycat, dx1, dx1b, dg2 = _out_proj_dx(dh2, s["x1"], p["norm2_g"][l] + tok, dx, s["wout"], f"out_proj_dx_{l}")
        g["norm2_g"][l] = dg2[0]
        tok = emit(l, "w_out", _matmul(s["ycat"], dx1b, "tn", f"out_proj_dw_{l}", out_dtype=BF16, tn=D_MODEL, tk=T))
        dya, dob, dc, dod, dgm, dlg, dlb, dcb = _mix_bwd(s["ya"], s["ob"], s["c"], s["od"], dycat, s["gmix"] + tok, *s["ln"], B, S, f"mix_bwd_{l}")
        g["mix_norm_g"][l] = dgm[0]
        dz_a, dws, dbias = _sgu_bwd(z, dya, s["ws"], jnp.swapaxes(s["ws"], 1, 2), s["bias"], f"sgu_bwd_{l}")
        g["sgu_w"][l] = dws
        g["sgu_b"][l] = dbias.reshape(128, 8, HEAD_DIM).sum(-1).T
        g["conv_ln_g"][l], g["conv_ln_b"][l], g["conv_b"][l] = dlg[0], dlb[0], dcb[0]
        dz_ca, dz_cg, dcw = _conv_bwd2(z, dc, s["cw"], B, S, f"conv_bwd_{l}")
        g["conv_w"][l] = dcw.reshape(32, 512)[:CONV_WIDTH]
        dq, dk, dv, dwin = _attn_bwd(*s["qkv_b"], s["ob"], dob, win, f"attn_b_bwd_{l}")
        dwin_total = dwin if dwin_total is None else dwin_total + dwin
        dz_b, dgq, dgk = _prep_bwd(z, dq, dk, dv, *s["gb"], None, B, S, N_HEADS, cols_b, f"prep_b_bwd_{l}")
        g["dil_qn_g"][l] = dgq.reshape(N_HEADS, HEAD_DIM).sum(0)
        g["dil_kn_g"][l] = dgk.reshape(N_HEADS, HEAD_DIM).sum(0)
        dq, dk, dv = _attn_bwd(*s["qkv_d"], s["od"], dod, None, f"attn_d_bwd_{l}", tq=GQA_TQ)
        dz_d, dgq, dgk = _prep_bwd(z, dq, dk, dv, *s["gd"], rope, B, S, KV_HEADS, cols_d, f"prep_d_bwd_{l}")
        g["gqa_qn_g"][l] = dgq.reshape(N_HEADS, HEAD_DIM).sum(0)
        g["gqa_kn_g"][l] = dgk.reshape(KV_HEADS, HEAD_DIM).sum(0)
        dz = [dz_a, dz_b, dz_ca, dz_cg, dz_d]
        tok = jnp.zeros((), F32)
        if l == 0:
            done = {k: jnp.stack(v) for k, v in g.items() if k != "norm1_g"}
            done["rel_bias"] = _bias_fold(dwin_total, S, "bias_fold")
            tok = early(done)
        tok += emit(l, "w_in", _in_proj_dw(dz, s["h"], f"in_proj_dw_{l}"))
        dx, dxb, dg1 = _in_proj_dx(dz, s["win"]["w_in"], s["x"], p["norm1_g"][l] + tok, dx1, f"in_proj_dx_{l}")
        g["norm1_g"][l] = dg1[0]

    return loss_blk[0, 0], dx, jnp.stack(g["norm1_g"])


GROUPS = {"in": ("w_in",), "out": ("w_out",), "ffn": ("w_gate", "w_up", "w_down")}
COL_SHARDED = ("w_in", "w_gate", "w_up")


def kernel(x, rel_bias, norm1_g, w_in, sgu_w, sgu_b, dil_qn_g, dil_kn_g, conv_w, conv_b, conv_ln_g, conv_ln_b, gqa_qn_g, gqa_kn_g, mix_norm_g, w_out, norm2_g, w_gate, w_up, w_down, loss_target, m_rel_bias, m_norm1_g, m_w_in, m_sgu_w, m_sgu_b, m_dil_qn_g, m_dil_kn_g, m_conv_w, m_conv_b, m_conv_ln_g, m_conv_ln_b, m_gqa_qn_g, m_gqa_kn_g, m_mix_norm_g, m_w_out, m_norm2_g, m_w_gate, m_w_up, m_w_down, v_rel_bias, v_norm1_g, v_w_in, v_sgu_w, v_sgu_b, v_dil_qn_g, v_dil_kn_g, v_conv_w, v_conv_b, v_conv_ln_g, v_conv_ln_b, v_gqa_qn_g, v_gqa_kn_g, v_mix_norm_g, v_w_out, v_norm2_g, v_w_gate, v_w_up, v_w_down):
    w = dict(rel_bias=rel_bias, norm1_g=norm1_g, w_in=w_in, sgu_w=sgu_w, sgu_b=sgu_b, dil_qn_g=dil_qn_g, dil_kn_g=dil_kn_g, conv_w=conv_w,
             conv_b=conv_b, conv_ln_g=conv_ln_g, conv_ln_b=conv_ln_b, gqa_qn_g=gqa_qn_g, gqa_kn_g=gqa_kn_g, mix_norm_g=mix_norm_g,
             w_out=w_out, norm2_g=norm2_g, w_gate=w_gate, w_up=w_up, w_down=w_down)
    m = dict(rel_bias=m_rel_bias, norm1_g=m_norm1_g, w_in=m_w_in, sgu_w=m_sgu_w, sgu_b=m_sgu_b, dil_qn_g=m_dil_qn_g, dil_kn_g=m_dil_kn_g,
             conv_w=m_conv_w, conv_b=m_conv_b, conv_ln_g=m_conv_ln_g, conv_ln_b=m_conv_ln_b, gqa_qn_g=m_gqa_qn_g, gqa_kn_g=m_gqa_kn_g,
             mix_norm_g=m_mix_norm_g, w_out=m_w_out, norm2_g=m_norm2_g, w_gate=m_w_gate, w_up=m_w_up, w_down=m_w_down)
    v = dict(rel_bias=v_rel_bias, norm1_g=v_norm1_g, w_in=v_w_in, sgu_w=v_sgu_w, sgu_b=v_sgu_b, dil_qn_g=v_dil_qn_g, dil_kn_g=v_dil_kn_g,
             conv_w=v_conv_w, conv_b=v_conv_b, conv_ln_g=v_conv_ln_g, conv_ln_b=v_conv_ln_b, gqa_qn_g=v_gqa_qn_g, gqa_kn_g=v_gqa_kn_g,
             mix_norm_g=v_mix_norm_g, w_out=v_w_out, norm2_g=v_norm2_g, w_gate=v_w_gate, w_up=v_w_up, w_down=v_w_down)
    names = list(w)
    B, S, D = x.shape
    T = B * S
    me = 4 * lax.axis_index("x") + 2 * lax.axis_index("y") + lax.axis_index("c")

    view = lambda a, k: jnp.swapaxes(a, 1, 2) if k in COL_SHARDED else a
    bf = {k: view(w[k], k).astype(BF16) for k in LARGE}
    spreads, forwards = {}, {}

    def start_gather(l, group, after=None):
        srcs = [bf[k][l] for k in GROUPS[group]] + ([conv_w[l]] if group == "in" else [])
        spreads[l, group] = _spread_start(srcs, "chips", f"gather_{group}_{l}_start", after)
        return spreads[l, group][3][0, 0]

    def forward(l, group, after):
        sems, srcs, lands, _ = spreads[l, group]
        lands = _spread_wait(sems, srcs, lands, after, "chips", f"gather_{group}_{l}_wait")
        forwards[l, group] = _spread_start(None, "forward", f"forward_{group}_{l}_start", lands=lands)
        return forwards[l, group][3]

    def landed(l, group, after):
        sems, _, lands, _ = forwards[l, group]
        return _spread_wait(sems, [], lands, after, "forward", f"forward_{group}_{l}_wait")

    tok0 = start_gather(0, "in") + start_gather(0, "out") + start_gather(0, "ffn")
    small = {k: w[k] for k in SMALL}
    small["norm1_g"] = norm1_g.at[0].add(tok0)

    def mid(l, z):
        if l > 0:
            return jnp.zeros((), F32)
        return start_gather(1, "in", z) + start_gather(1, "out", z) + start_gather(1, "ffn", z)

    def fetch(l, group, after):
        if group == "in":
            tok = forward(0, "in", after) if l == 0 else after
        elif group == "out":
            tok = forward(l, "ffn", [after, forward(l, "out", after)])
        else:
            tok = forward(1, "in", after) if l == 0 else after
        got = dict(zip(GROUPS[group] + ("conv_w",), landed(l, group, [after, tok])))
        if group == "in":
            got["w_in"] = got["w_in"].reshape(IN_WIDTH, D)
            got["conv_w"] = jnp.transpose(got["conv_w"], (1, 0, 2)).reshape(CONV_WIDTH, 512)
        if group == "out":
            got["w_out"] = got["w_out"].reshape(D, D)
        if group == "ffn":
            got = {k: a.reshape(FFN_GROUPS, -1, D) for k, a in got.items()}
        return got

    scatters = {}

    def emit(l, k, dw):
        dw = dw.reshape(N_DEV, -1, D)
        scatters[l, k] = _spread_start([dw], "scatter", f"scatter_{k}_{l}_start")
        return scatters[l, k][3][0, 0]

    flat2 = lambda a: a.reshape(-1, a.shape[-1])
    small_spread = []

    def early(done):
        small_spread.append(_spread_start([flat2(done[k]) for k in EARLY], "gather", "gather_small_grads_start"))
        return small_spread[0][3][0, 0]

    loss_part, dx, dnorm1 = _local_step(x.reshape(T, D), loss_target.reshape(T, D), small, B, S, fetch, emit, mid, early)
    loss = lax.psum(loss_part, ("x", "y", "c"))

    out_g, out_d, out_m, out_v = {}, {}, {}, {}

    def update_large(k, after):
        shp = view(w[k], k).shape
        two_d = lambda a: view(a, k).reshape(-1, shp[-1])
        res = None
        for l in reversed(range(DEPTH)):
            sems, srcs, lands, _ = scatters[l, k]
            stack = _spread_wait(sems, srcs, lands, after, "scatter", f"scatter_{k}_{l}_wait")[0]
            res = _adamw(two_d(w[k]), two_d(m[k]), two_d(v[k]), stack.reshape(N_DEV, -1, shp[-1]), f"adamw_{k}_{l}", layer=l, prev=res)
        out_g[k], out_d[k], out_m[k], out_v[k] = [view(a.reshape(shp), k) for a in res]
        return res[0]

    late_sems, late_srcs, late_lands, late_tok = _spread_start([flat2(dnorm1)], "gather", "gather_norm1_grad_start")
    after = [dx, late_tok]
    for k in ("w_down", "w_gate", "w_up", "w_out"):
        after = update_large(k, after)
    sems, srcs, lands, _ = small_spread[0]
    stacks = dict(zip(EARLY, _spread_wait(sems, srcs, lands, after, "gather", "gather_small_grads_wait")))
    stacks["norm1_g"] = _spread_wait(late_sems, late_srcs, late_lands, after, "gather", "gather_norm1_grad_wait")[0]
    for k in SMALL:
        stack = stacks[k]
        if k == "conv_w":
            stack = lax.dynamic_slice_in_dim(stack, me * (512 // N_DEV), 512 // N_DEV, axis=2)
        res = _adamw(flat2(w[k]), flat2(m[k]), flat2(v[k]), stack, f"adamw_{k}")
        out_g[k], out_d[k], out_m[k], out_v[k] = [a.reshape(w[k].shape) for a in res]
        after = res[0]
    update_large("w_in", after)

    return (loss, dx.reshape(B, S, D), *[out_g[k] for k in names], *[out_d[k] for k in names],
            *[out_m[k] for k in names], *[out_v[k] for k in names])
```
